```python
import math
import jax, jax.numpy as jnp
from jax import lax
import numpy as np

D_MODEL = 2048
BATCH = 8
SEQ = 8192
DEPTH = 1

N_Q_HEADS = 16
N_KV_HEADS = 2
HEAD_DIM = 64
Q_PER_KV = N_Q_HEADS // N_KV_HEADS
WINDOW = 128
BLOCK = 128
ATTN_WIDTH = N_Q_HEADS * HEAD_DIM
KV_WIDTH = N_KV_HEADS * HEAD_DIM
SSM_GROUP = 16
SSM_GROUPS = 32
SSM_WIDTH = SSM_GROUP * SSM_GROUPS
SSM_STATE = 64
DT_MIN = 0.001
DT_MAX = 0.1
D_FF = 5632
CONV_WIDTH = 3
RMS_EPS = 1e-6
IN_COLS = ATTN_WIDTH + 2 * KV_WIDTH + SSM_WIDTH + 2 * D_MODEL
SPLIT_POINTS = (ATTN_WIDTH, ATTN_WIDTH + KV_WIDTH, ATTN_WIDTH + 2 * KV_WIDTH,
                ATTN_WIDTH + 2 * KV_WIDTH + SSM_WIDTH,
                ATTN_WIDTH + 2 * KV_WIDTH + SSM_WIDTH + D_MODEL)
NEG_BIG = -1e30

kernel_name = "hybrid_swa_s5_convffn_block"


def rms_norm(x, g):
    xf = x.astype(jnp.float32)
    y = xf * lax.rsqrt(jnp.mean(xf * xf, axis=-1, keepdims=True) + RMS_EPS)
    return (y * g.astype(jnp.float32)).astype(x.dtype)


def sliding_window_attention(q, k, v, sinks):
    b, l = q.shape[0], q.shape[1]
    nb = l // BLOCK
    qb = q.reshape(b, nb, BLOCK, N_KV_HEADS, Q_PER_KV, HEAD_DIM)
    kb = k.reshape(b, nb, BLOCK, N_KV_HEADS, HEAD_DIM)
    vb = v.reshape(b, nb, BLOCK, N_KV_HEADS, HEAD_DIM)

    def with_prev(t):
        prev = jnp.pad(t, ((0, 0), (1, 0), (0, 0), (0, 0), (0, 0)))[:, :-1]
        return jnp.concatenate([prev, t], axis=2)

    kx = with_prev(kb)
    vx = with_prev(vb)
    scores = jnp.einsum('bnqgrd,bnsgd->bngrqs', qb, kx).astype(jnp.float32) * (HEAD_DIM ** -0.5)
    q_idx = jnp.arange(BLOCK)[:, None]
    s_idx = jnp.arange(2 * BLOCK)[None, :]
    dist = q_idx + BLOCK - s_idx
    band = (dist >= 0) & (dist < WINDOW)
    valid = band[None] & ((jnp.arange(nb)[:, None, None] > 0) | (s_idx[None] >= BLOCK))
    slopes = 2.0 ** (-8.0 * jnp.arange(1, N_Q_HEADS + 1, dtype=jnp.float32) / N_Q_HEADS)
    slopes = slopes.reshape(N_KV_HEADS, Q_PER_KV)
    alibi = -slopes[:, :, None, None] * dist.astype(jnp.float32)[None, None]
    scores = scores + alibi[None, None]
    scores = jnp.where(valid[None, :, None, None], scores, NEG_BIG)
    sink = sinks.astype(jnp.float32).reshape(N_KV_HEADS, Q_PER_KV)[None, None, :, :, None, None]
    m = jnp.maximum(jnp.max(scores, axis=-1, keepdims=True), sink)
    p = jnp.exp(scores - m)
    p = p / (jnp.sum(p, axis=-1, keepdims=True) + jnp.exp(sink - m))
    out = jnp.einsum('bngrqs,bnsgd->bnqgrd', p.astype(v.dtype), vx)
    return out.reshape(b, l, ATTN_WIDTH)


def s5_ssm(u, a_re, a_im, log_dt, b_re, b_im, c_re, c_im, d_skip):
    bsz, l = u.shape[0], u.shape[1]
    ug = u.reshape(bsz, l, SSM_GROUPS, SSM_GROUP)
    dt = jnp.exp(log_dt)[:, None]
    mag = jnp.exp(a_re * dt)
    ab_re = mag * jnp.cos(a_im * dt)
    ab_im = mag * jnp.sin(a_im * dt)
    nr = ab_re - 1.0
    ni = ab_im
    den = a_re * a_re + a_im * a_im
    z_re = (nr * a_re + ni * a_im) / den
    z_im = (ni * a_re - nr * a_im) / den
    bb_re = z_re[..., None] * b_re - z_im[..., None] * b_im
    bb_im = z_re[..., None] * b_im + z_im[..., None] * b_re
    bu_re = jnp.einsum('gph,blgh->blgp', bb_re, ug)
    bu_im = jnp.einsum('gph,blgh->blgp', bb_im, ug)
    a_re_t = jnp.broadcast_to(ab_re, bu_re.shape)
    a_im_t = jnp.broadcast_to(ab_im, bu_im.shape)

    def combine(left, right):
        a1r, a1i, b1r, b1i = left
        a2r, a2i, b2r, b2i = right
        return (a1r * a2r - a1i * a2i,
                a1r * a2i + a1i * a2r,
                a2r * b1r - a2i * b1i + b2r,
                a2r * b1i + a2i * b1r + b2i)

    _, _, xs_re, xs_im = lax.associative_scan(combine, (a_re_t, a_im_t, bu_re, bu_im), axis=1)
    y = (jnp.einsum('ghp,blgp->blgh', c_re, xs_re)
         - jnp.einsum('ghp,blgp->blgh', c_im, xs_im)
         + d_skip.reshape(SSM_GROUPS, SSM_GROUP) * ug)
    return y.reshape(bsz, l, SSM_WIDTH)


def causal_depthwise_conv(x, w, b):
    l = x.shape[1]
    xp = jnp.pad(x, ((0, 0), (CONV_WIDTH - 1, 0), (0, 0)))
    y = b
    for k in range(CONV_WIDTH):
        y = y + w[k] * xp[:, k:k + l]
    return y


def _fwd_setup_inputs(seed: int = 0) -> dict:
    key = jax.random.key(seed)
    ks = jax.random.split(key, 24)
    f32 = jnp.float32
    nrm = lambda k, shape, s: jax.random.normal(k, shape, f32) * s
    x = jax.random.normal(ks[0], (BATCH, SEQ, D_MODEL), f32)
    attn_norm_g = 1.0 + nrm(ks[1], (DEPTH, D_MODEL), 0.02)
    w_in = nrm(ks[2], (DEPTH, D_MODEL, IN_COLS), D_MODEL ** -0.5)
    b_in = nrm(ks[3], (DEPTH, IN_COLS), 0.02)
    attn_sinks = nrm(ks[4], (DEPTH, N_Q_HEADS), 0.5)
    ssm_a_re = -0.5 + nrm(ks[5], (DEPTH, SSM_GROUPS, SSM_STATE), 0.01)
    ssm_a_im = (jnp.pi * jnp.arange(SSM_STATE, dtype=f32))[None, None, :] + nrm(ks[6], (DEPTH, SSM_GROUPS, SSM_STATE), 0.01)
    ssm_log_dt = jax.random.uniform(ks[7], (DEPTH, SSM_GROUPS), f32, minval=math.log(DT_MIN), maxval=math.log(DT_MAX))
    ssm_b_re = nrm(ks[8], (DEPTH, SSM_GROUPS, SSM_STATE, SSM_GROUP), (2 * SSM_GROUP) ** -0.5)
    ssm_b_im = nrm(ks[9], (DEPTH, SSM_GROUPS, SSM_STATE, SSM_GROUP), (2 * SSM_GROUP) ** -0.5)
    ssm_c_re = nrm(ks[10], (DEPTH, SSM_GROUPS, SSM_GROUP, SSM_STATE), (2 * SSM_STATE) ** -0.5)
    ssm_c_im = nrm(ks[11], (DEPTH, SSM_GROUPS, SSM_GROUP, SSM_STATE), (2 * SSM_STATE) ** -0.5)
    ssm_d = nrm(ks[12], (DEPTH, SSM_WIDTH), 1.0)
    w_glu = nrm(ks[13], (DEPTH, SSM_WIDTH, 2 * SSM_WIDTH), SSM_WIDTH ** -0.5)
    b_glu = nrm(ks[14], (DEPTH, 2 * SSM_WIDTH), 0.02)
    w_branch_attn = nrm(ks[15], (DEPTH, ATTN_WIDTH, D_MODEL), ATTN_WIDTH ** -0.5)
    w_branch_ssm = nrm(ks[16], (DEPTH, SSM_WIDTH, D_MODEL), SSM_WIDTH ** -0.5)
    w_out = nrm(ks[17], (DEPTH, D_MODEL, D_MODEL), D_MODEL ** -0.5)
    ffn_norm_g = 1.0 + nrm(ks[18], (DEPTH, D_MODEL), 0.02)
    w_up = nrm(ks[19], (DEPTH, D_MODEL, 2 * D_FF), D_MODEL ** -0.5)
    conv_w = nrm(ks[20], (DEPTH, CONV_WIDTH, D_FF), CONV_WIDTH ** -0.5)
    conv_b = nrm(ks[21], (DEPTH, D_FF), 0.02)
    w_down = nrm(ks[22], (DEPTH, D_FF, D_MODEL), D_FF ** -0.5)
    final_norm_g = 1.0 + nrm(ks[23], (D_MODEL,), 0.02)
    return {"x": x, "attn_norm_g": attn_norm_g, "w_in": w_in, "b_in": b_in, "attn_sinks": attn_sinks,
            "ssm_a_re": ssm_a_re, "ssm_a_im": ssm_a_im, "ssm_log_dt": ssm_log_dt,
            "ssm_b_re": ssm_b_re, "ssm_b_im": ssm_b_im, "ssm_c_re": ssm_c_re, "ssm_c_im": ssm_c_im,
            "ssm_d": ssm_d, "w_glu": w_glu, "b_glu": b_glu, "w_branch_attn": w_branch_attn,
            "w_branch_ssm": w_branch_ssm, "w_out": w_out, "ffn_norm_g": ffn_norm_g, "w_up": w_up,
            "conv_w": conv_w, "conv_b": conv_b, "w_down": w_down, "final_norm_g": final_norm_g}


def _fwd_reference(x, attn_norm_g, w_in, b_in, attn_sinks, ssm_a_re, ssm_a_im, ssm_log_dt, ssm_b_re, ssm_b_im,
              ssm_c_re, ssm_c_im, ssm_d, w_glu, b_glu, w_branch_attn, w_branch_ssm, w_out, ffn_norm_g,
              w_up, conv_w, conv_b, w_down, final_norm_g):
    bsz, seq = x.shape[0], x.shape[1]
    for i in range(DEPTH):
        h = rms_norm(x, attn_norm_g[i])
        proj = h @ w_in[i] + b_in[i]
        q, k, v, u, gate_attn, gate_ssm = jnp.split(proj, SPLIT_POINTS, axis=-1)
        q = q.reshape(bsz, seq, N_Q_HEADS, HEAD_DIM)
        k = k.reshape(bsz, seq, N_KV_HEADS, HEAD_DIM)
        v = v.reshape(bsz, seq, N_KV_HEADS, HEAD_DIM)
        attn = sliding_window_attention(q, k, v, attn_sinks[i])
        y = s5_ssm(u, ssm_a_re[i], ssm_a_im[i], ssm_log_dt[i], ssm_b_re[i], ssm_b_im[i],
                   ssm_c_re[i], ssm_c_im[i], ssm_d[i])
        y_val, y_gate = jnp.split(jax.nn.gelu(y, approximate=False) @ w_glu[i] + b_glu[i], 2, axis=-1)
        ssm = y_val * jax.nn.sigmoid(y_gate)
        merged = (jax.nn.sigmoid(gate_attn) * (attn @ w_branch_attn[i])
                  + jax.nn.sigmoid(gate_ssm) * (ssm @ w_branch_ssm[i]))
        x = x + merged @ w_out[i]
        h = rms_norm(x, ffn_norm_g[i])
        val, gate = jnp.split(h @ w_up[i], 2, axis=-1)
        gate = causal_depthwise_conv(gate, conv_w[i], conv_b[i])
        x = x + (val * jax.nn.gelu(gate, approximate=False)) @ w_down[i]
    return rms_norm(x, final_norm_g)


import jax as _jax
import jax.numpy as _jnp

TWIN_FORMAT = 'train_step'
FWD_PARAMS = ['x', 'attn_norm_g', 'w_in', 'b_in', 'attn_sinks', 'ssm_a_re', 'ssm_a_im', 'ssm_log_dt', 'ssm_b_re', 'ssm_b_im', 'ssm_c_re', 'ssm_c_im', 'ssm_d', 'w_glu', 'b_glu', 'w_branch_attn', 'w_branch_ssm', 'w_out', 'ffn_norm_g', 'w_up', 'conv_w', 'conv_b', 'w_down', 'final_norm_g']
TWIN_WEIGHTS = ['attn_norm_g', 'w_in', 'b_in', 'attn_sinks', 'ssm_a_re', 'ssm_a_im', 'ssm_log_dt', 'ssm_b_re', 'ssm_b_im', 'ssm_c_re', 'ssm_c_im', 'ssm_d', 'w_glu', 'b_glu', 'w_branch_attn', 'w_branch_ssm', 'w_out', 'ffn_norm_g', 'w_up', 'conv_w', 'conv_b', 'w_down', 'final_norm_g']
TWIN_DIFF_INPUT = 'x'
TWIN_INPUTS = ['x', 'attn_norm_g', 'w_in', 'b_in', 'attn_sinks', 'ssm_a_re', 'ssm_a_im', 'ssm_log_dt', 'ssm_b_re', 'ssm_b_im', 'ssm_c_re', 'ssm_c_im', 'ssm_d', 'w_glu', 'b_glu', 'w_branch_attn', 'w_branch_ssm', 'w_out', 'ffn_norm_g', 'w_up', 'conv_w', 'conv_b', 'w_down', 'final_norm_g', 'loss_target', 'm_attn_norm_g', 'm_w_in', 'm_b_in', 'm_attn_sinks', 'm_ssm_a_re', 'm_ssm_a_im', 'm_ssm_log_dt', 'm_ssm_b_re', 'm_ssm_b_im', 'm_ssm_c_re', 'm_ssm_c_im', 'm_ssm_d', 'm_w_glu', 'm_b_glu', 'm_w_branch_attn', 'm_w_branch_ssm', 'm_w_out', 'm_ffn_norm_g', 'm_w_up', 'm_conv_w', 'm_conv_b', 'm_w_down', 'm_final_norm_g', 'v_attn_norm_g', 'v_w_in', 'v_b_in', 'v_attn_sinks', 'v_ssm_a_re', 'v_ssm_a_im', 'v_ssm_log_dt', 'v_ssm_b_re', 'v_ssm_b_im', 'v_ssm_c_re', 'v_ssm_c_im', 'v_ssm_d', 'v_w_glu', 'v_b_glu', 'v_w_branch_attn', 'v_w_branch_ssm', 'v_w_out', 'v_ffn_norm_g', 'v_w_up', 'v_conv_w', 'v_conv_b', 'v_w_down', 'v_final_norm_g']
TWIN_OUTPUTS = ['loss', 'grad_x', 'grad_attn_norm_g', 'grad_w_in', 'grad_b_in', 'grad_attn_sinks', 'grad_ssm_a_re', 'grad_ssm_a_im', 'grad_ssm_log_dt', 'grad_ssm_b_re', 'grad_ssm_b_im', 'grad_ssm_c_re', 'grad_ssm_c_im', 'grad_ssm_d', 'grad_w_glu', 'grad_b_glu', 'grad_w_branch_attn', 'grad_w_branch_ssm', 'grad_w_out', 'grad_ffn_norm_g', 'grad_w_up', 'grad_conv_w', 'grad_conv_b', 'grad_w_down', 'grad_final_norm_g', 'delta_attn_norm_g', 'delta_w_in', 'delta_b_in', 'delta_attn_sinks', 'delta_ssm_a_re', 'delta_ssm_a_im', 'delta_ssm_log_dt', 'delta_ssm_b_re', 'delta_ssm_b_im', 'delta_ssm_c_re', 'delta_ssm_c_im', 'delta_ssm_d', 'delta_w_glu', 'delta_b_glu', 'delta_w_branch_attn', 'delta_w_branch_ssm', 'delta_w_out', 'delta_ffn_norm_g', 'delta_w_up', 'delta_conv_w', 'delta_conv_b', 'delta_w_down', 'delta_final_norm_g', 'new_m_attn_norm_g', 'new_m_w_in', 'new_m_b_in', 'new_m_attn_sinks', 'new_m_ssm_a_re', 'new_m_ssm_a_im', 'new_m_ssm_log_dt', 'new_m_ssm_b_re', 'new_m_ssm_b_im', 'new_m_ssm_c_re', 'new_m_ssm_c_im', 'new_m_ssm_d', 'new_m_w_glu', 'new_m_b_glu', 'new_m_w_branch_attn', 'new_m_w_branch_ssm', 'new_m_w_out', 'new_m_ffn_norm_g', 'new_m_w_up', 'new_m_conv_w', 'new_m_conv_b', 'new_m_w_down', 'new_m_final_norm_g', 'new_v_attn_norm_g', 'new_v_w_in', 'new_v_b_in', 'new_v_attn_sinks', 'new_v_ssm_a_re', 'new_v_ssm_a_im', 'new_v_ssm_log_dt', 'new_v_ssm_b_re', 'new_v_ssm_b_im', 'new_v_ssm_c_re', 'new_v_ssm_c_im', 'new_v_ssm_d', 'new_v_w_glu', 'new_v_b_glu', 'new_v_w_branch_attn', 'new_v_w_branch_ssm', 'new_v_w_out', 'new_v_ffn_norm_g', 'new_v_w_up', 'new_v_conv_w', 'new_v_conv_b', 'new_v_w_down', 'new_v_final_norm_g']
TWIN_LEAF_KINDS = {'loss': 'loss', 'grad_x': 'grad_x', 'grad_attn_norm_g': 'grad_w', 'grad_w_in': 'grad_w', 'grad_b_in': 'grad_w', 'grad_attn_sinks': 'grad_w', 'grad_ssm_a_re': 'grad_w', 'grad_ssm_a_im': 'grad_w', 'grad_ssm_log_dt': 'grad_w', 'grad_ssm_b_re': 'grad_w', 'grad_ssm_b_im': 'grad_w', 'grad_ssm_c_re': 'grad_w', 'grad_ssm_c_im': 'grad_w', 'grad_ssm_d': 'grad_w', 'grad_w_glu': 'grad_w', 'grad_b_glu': 'grad_w', 'grad_w_branch_attn': 'grad_w', 'grad_w_branch_ssm': 'grad_w', 'grad_w_out': 'grad_w', 'grad_ffn_norm_g': 'grad_w', 'grad_w_up': 'grad_w', 'grad_conv_w': 'grad_w', 'grad_conv_b': 'grad_w', 'grad_w_down': 'grad_w', 'grad_final_norm_g': 'grad_w', 'delta_attn_norm_g': 'delta_w', 'delta_w_in': 'delta_w', 'delta_b_in': 'delta_w', 'delta_attn_sinks': 'delta_w', 'delta_ssm_a_re': 'delta_w', 'delta_ssm_a_im': 'delta_w', 'delta_ssm_log_dt': 'delta_w', 'delta_ssm_b_re': 'delta_w', 'delta_ssm_b_im': 'delta_w', 'delta_ssm_c_re': 'delta_w', 'delta_ssm_c_im': 'delta_w', 'delta_ssm_d': 'delta_w', 'delta_w_glu': 'delta_w', 'delta_b_glu': 'delta_w', 'delta_w_branch_attn': 'delta_w', 'delta_w_branch_ssm': 'delta_w', 'delta_w_out': 'delta_w', 'delta_ffn_norm_g': 'delta_w', 'delta_w_up': 'delta_w', 'delta_conv_w': 'delta_w', 'delta_conv_b': 'delta_w', 'delta_w_down': 'delta_w', 'delta_final_norm_g': 'delta_w', 'new_m_attn_norm_g': 'new_m', 'new_m_w_in': 'new_m', 'new_m_b_in': 'new_m', 'new_m_attn_sinks': 'new_m', 'new_m_ssm_a_re': 'new_m', 'new_m_ssm_a_im': 'new_m', 'new_m_ssm_log_dt': 'new_m', 'new_m_ssm_b_re': 'new_m', 'new_m_ssm_b_im': 'new_m', 'new_m_ssm_c_re': 'new_m', 'new_m_ssm_c_im': 'new_m', 'new_m_ssm_d': 'new_m', 'new_m_w_glu': 'new_m', 'new_m_b_glu': 'new_m', 'new_m_w_branch_attn': 'new_m', 'new_m_w_branch_ssm': 'new_m', 'new_m_w_out': 'new_m', 'new_m_ffn_norm_g': 'new_m', 'new_m_w_up': 'new_m', 'new_m_conv_w': 'new_m', 'new_m_conv_b': 'new_m', 'new_m_w_down': 'new_m', 'new_m_final_norm_g': 'new_m', 'new_v_attn_norm_g': 'new_v', 'new_v_w_in': 'new_v', 'new_v_b_in': 'new_v', 'new_v_attn_sinks': 'new_v', 'new_v_ssm_a_re': 'new_v', 'new_v_ssm_a_im': 'new_v', 'new_v_ssm_log_dt': 'new_v', 'new_v_ssm_b_re': 'new_v', 'new_v_ssm_b_im': 'new_v', 'new_v_ssm_c_re': 'new_v', 'new_v_ssm_c_im': 'new_v', 'new_v_ssm_d': 'new_v', 'new_v_w_glu': 'new_v', 'new_v_b_glu': 'new_v', 'new_v_w_branch_attn': 'new_v', 'new_v_w_branch_ssm': 'new_v', 'new_v_w_out': 'new_v', 'new_v_ffn_norm_g': 'new_v', 'new_v_w_up': 'new_v', 'new_v_conv_w': 'new_v', 'new_v_conv_b': 'new_v', 'new_v_w_down': 'new_v', 'new_v_final_norm_g': 'new_v'}


def _forward(args):
    return _fwd_reference(*[args[k] for k in FWD_PARAMS])


def _output_shape():
    def fwd():
        inp = _fwd_setup_inputs(0)
        return _fwd_reference(*[inp[k] for k in FWD_PARAMS])
    out = _jax.eval_shape(fwd)
    return out.shape, out.dtype

N_MICROBATCH = 1
ADAM_LR = 0.001
ADAM_B1 = 0.9
ADAM_B2 = 0.999
ADAM_EPS = 1e-08
ADAM_WD = 0.01
ADAM_STEP = 10
PER_EXAMPLE_BATCH_AXIS = {'x': 0, 'loss_target': 0}
SHARED_INPUTS = []
_WEIGHT_DTYPES = {'attn_norm_g': _jnp.float32, 'w_in': _jnp.float32, 'b_in': _jnp.float32, 'attn_sinks': _jnp.float32, 'ssm_a_re': _jnp.float32, 'ssm_a_im': _jnp.float32, 'ssm_log_dt': _jnp.float32, 'ssm_b_re': _jnp.float32, 'ssm_b_im': _jnp.float32, 'ssm_c_re': _jnp.float32, 'ssm_c_im': _jnp.float32, 'ssm_d': _jnp.float32, 'w_glu': _jnp.float32, 'b_glu': _jnp.float32, 'w_branch_attn': _jnp.float32, 'w_branch_ssm': _jnp.float32, 'w_out': _jnp.float32, 'ffn_norm_g': _jnp.float32, 'w_up': _jnp.float32, 'conv_w': _jnp.float32, 'conv_b': _jnp.float32, 'w_down': _jnp.float32, 'final_norm_g': _jnp.float32}
MOMENT_SCALE = {'attn_norm_g': 4.701795e-02, 'w_in': 2.762401e-02, 'b_in': 4.956542e-02, 'attn_sinks': 1.519012e-01, 'ssm_a_re': 2.708899e-03, 'ssm_a_im': 3.364007e-03, 'ssm_log_dt': 2.731176e+00, 'ssm_b_re': 1.889941e-03, 'ssm_b_im': 1.852877e-03, 'ssm_c_re': 3.769649e-03, 'ssm_c_im': 3.763967e-03, 'ssm_d': 5.795621e-02, 'w_glu': 3.798516e-02, 'b_glu': 5.806253e-02, 'w_branch_attn': 2.419235e-02, 'w_branch_ssm': 2.604985e-02, 'w_out': 3.511012e-02, 'ffn_norm_g': 9.607306e-02, 'w_up': 4.148275e-02, 'conv_w': 4.286165e-02, 'conv_b': 4.272282e-02, 'w_down': 6.778605e-02, 'final_norm_g': 3.195380e+01}


def _to_microbatches(a, axis):
    t = _jnp.moveaxis(a, axis, 0)
    t = t.reshape((N_MICROBATCH, t.shape[0] // N_MICROBATCH) + t.shape[1:])
    return _jnp.moveaxis(t, 1, axis + 1)


def setup_inputs(seed: int = 0) -> dict:
    inp = _fwd_setup_inputs(seed)
    key = _jax.random.fold_in(_jax.random.key(seed), 7919)
    shape, _ = _output_shape()
    out = dict(inp)
    out["loss_target"] = _jax.random.normal(_jax.random.fold_in(key, 0), shape, _jnp.float32)
    for i, name in enumerate(TWIN_WEIGHTS):
        w = inp[name].astype(_jnp.float32)
        if MOMENT_SCALE is None:
            s = _jnp.sqrt(_jnp.mean(_jnp.square(w)) + 1e-30)
        else:
            s = MOMENT_SCALE[name]
        km, kv = _jax.random.split(_jax.random.fold_in(key, i + 1))
        out[name] = w
        out["m_" + name] = s * _jax.random.normal(km, w.shape, _jnp.float32)
        out["v_" + name] = (s * s) * _jax.random.uniform(kv, w.shape, _jnp.float32, 0.5, 1.5)
    if N_MICROBATCH > 1:
        for name, axis in PER_EXAMPLE_BATCH_AXIS.items():
            out[name] = _to_microbatches(out[name], axis)
    return {'x': out['x'], 'attn_norm_g': out['attn_norm_g'], 'w_in': out['w_in'], 'b_in': out['b_in'], 'attn_sinks': out['attn_sinks'], 'ssm_a_re': out['ssm_a_re'], 'ssm_a_im': out['ssm_a_im'], 'ssm_log_dt': out['ssm_log_dt'], 'ssm_b_re': out['ssm_b_re'], 'ssm_b_im': out['ssm_b_im'], 'ssm_c_re': out['ssm_c_re'], 'ssm_c_im': out['ssm_c_im'], 'ssm_d': out['ssm_d'], 'w_glu': out['w_glu'], 'b_glu': out['b_glu'], 'w_branch_attn': out['w_branch_attn'], 'w_branch_ssm': out['w_branch_ssm'], 'w_out': out['w_out'], 'ffn_norm_g': out['ffn_norm_g'], 'w_up': out['w_up'], 'conv_w': out['conv_w'], 'conv_b': out['conv_b'], 'w_down': out['w_down'], 'final_norm_g': out['final_norm_g'], 'loss_target': out['loss_target'], 'm_attn_norm_g': out['m_attn_norm_g'], 'm_w_in': out['m_w_in'], 'm_b_in': out['m_b_in'], 'm_attn_sinks': out['m_attn_sinks'], 'm_ssm_a_re': out['m_ssm_a_re'], 'm_ssm_a_im': out['m_ssm_a_im'], 'm_ssm_log_dt': out['m_ssm_log_dt'], 'm_ssm_b_re': out['m_ssm_b_re'], 'm_ssm_b_im': out['m_ssm_b_im'], 'm_ssm_c_re': out['m_ssm_c_re'], 'm_ssm_c_im': out['m_ssm_c_im'], 'm_ssm_d': out['m_ssm_d'], 'm_w_glu': out['m_w_glu'], 'm_b_glu': out['m_b_glu'], 'm_w_branch_attn': out['m_w_branch_attn'], 'm_w_branch_ssm': out['m_w_branch_ssm'], 'm_w_out': out['m_w_out'], 'm_ffn_norm_g': out['m_ffn_norm_g'], 'm_w_up': out['m_w_up'], 'm_conv_w': out['m_conv_w'], 'm_conv_b': out['m_conv_b'], 'm_w_down': out['m_w_down'], 'm_final_norm_g': out['m_final_norm_g'], 'v_attn_norm_g': out['v_attn_norm_g'], 'v_w_in': out['v_w_in'], 'v_b_in': out['v_b_in'], 'v_attn_sinks': out['v_attn_sinks'], 'v_ssm_a_re': out['v_ssm_a_re'], 'v_ssm_a_im': out['v_ssm_a_im'], 'v_ssm_log_dt': out['v_ssm_log_dt'], 'v_ssm_b_re': out['v_ssm_b_re'], 'v_ssm_b_im': out['v_ssm_b_im'], 'v_ssm_c_re': out['v_ssm_c_re'], 'v_ssm_c_im': out['v_ssm_c_im'], 'v_ssm_d': out['v_ssm_d'], 'v_w_glu': out['v_w_glu'], 'v_b_glu': out['v_b_glu'], 'v_w_branch_attn': out['v_w_branch_attn'], 'v_w_branch_ssm': out['v_w_branch_ssm'], 'v_w_out': out['v_w_out'], 'v_ffn_norm_g': out['v_ffn_norm_g'], 'v_w_up': out['v_w_up'], 'v_conv_w': out['v_conv_w'], 'v_conv_b': out['v_conv_b'], 'v_w_down': out['v_w_down'], 'v_final_norm_g': out['v_final_norm_g']}


def _loss(weights, diff, rest, loss_target):
    with _jax.named_scope("forward"):
        args = {**rest, TWIN_DIFF_INPUT: diff, **{k: w.astype(_WEIGHT_DTYPES[k]) for k, w in weights.items()}}
        y = _forward(args)
    with _jax.named_scope("loss_head"):
        err = _jnp.square(y.astype(_jnp.float32) - loss_target)
        return 0.5 * _jnp.sum(_jnp.mean(err, axis=-1)) if err.ndim else 0.5 * err


def _adamw(w, g, m, v):
    m = ADAM_B1 * m + (1.0 - ADAM_B1) * g
    v = ADAM_B2 * v + (1.0 - ADAM_B2) * _jnp.square(g)
    m_hat = m / (1.0 - ADAM_B1 ** ADAM_STEP)
    v_hat = v / (1.0 - ADAM_B2 ** ADAM_STEP)
    delta = -ADAM_LR * (m_hat / (_jnp.sqrt(v_hat) + ADAM_EPS) + ADAM_WD * w)
    return delta, m, v


def reference(x, attn_norm_g, w_in, b_in, attn_sinks, ssm_a_re, ssm_a_im, ssm_log_dt, ssm_b_re, ssm_b_im, ssm_c_re, ssm_c_im, ssm_d, w_glu, b_glu, w_branch_attn, w_branch_ssm, w_out, ffn_norm_g, w_up, conv_w, conv_b, w_down, final_norm_g, loss_target, m_attn_norm_g, m_w_in, m_b_in, m_attn_sinks, m_ssm_a_re, m_ssm_a_im, m_ssm_log_dt, m_ssm_b_re, m_ssm_b_im, m_ssm_c_re, m_ssm_c_im, m_ssm_d, m_w_glu, m_b_glu, m_w_branch_attn, m_w_branch_ssm, m_w_out, m_ffn_norm_g, m_w_up, m_conv_w, m_conv_b, m_w_down, m_final_norm_g, v_attn_norm_g, v_w_in, v_b_in, v_attn_sinks, v_ssm_a_re, v_ssm_a_im, v_ssm_log_dt, v_ssm_b_re, v_ssm_b_im, v_ssm_c_re, v_ssm_c_im, v_ssm_d, v_w_glu, v_b_glu, v_w_branch_attn, v_w_branch_ssm, v_w_out, v_ffn_norm_g, v_w_up, v_conv_w, v_conv_b, v_w_down, v_final_norm_g):
    given = dict(x=x, attn_norm_g=attn_norm_g, w_in=w_in, b_in=b_in, attn_sinks=attn_sinks, ssm_a_re=ssm_a_re, ssm_a_im=ssm_a_im, ssm_log_dt=ssm_log_dt, ssm_b_re=ssm_b_re, ssm_b_im=ssm_b_im, ssm_c_re=ssm_c_re, ssm_c_im=ssm_c_im, ssm_d=ssm_d, w_glu=w_glu, b_glu=b_glu, w_branch_attn=w_branch_attn, w_branch_ssm=w_branch_ssm, w_out=w_out, ffn_norm_g=ffn_norm_g, w_up=w_up, conv_w=conv_w, conv_b=conv_b, w_down=w_down, final_norm_g=final_norm_g, loss_target=loss_target, m_attn_norm_g=m_attn_norm_g, m_w_in=m_w_in, m_b_in=m_b_in, m_attn_sinks=m_attn_sinks, m_ssm_a_re=m_ssm_a_re, m_ssm_a_im=m_ssm_a_im, m_ssm_log_dt=m_ssm_log_dt, m_ssm_b_re=m_ssm_b_re, m_ssm_b_im=m_ssm_b_im, m_ssm_c_re=m_ssm_c_re, m_ssm_c_im=m_ssm_c_im, m_ssm_d=m_ssm_d, m_w_glu=m_w_glu, m_b_glu=m_b_glu, m_w_branch_attn=m_w_branch_attn, m_w_branch_ssm=m_w_branch_ssm, m_w_out=m_w_out, m_ffn_norm_g=m_ffn_norm_g, m_w_up=m_w_up, m_conv_w=m_conv_w, m_conv_b=m_conv_b, m_w_down=m_w_down, m_final_norm_g=m_final_norm_g, v_attn_norm_g=v_attn_norm_g, v_w_in=v_w_in, v_b_in=v_b_in, v_attn_sinks=v_attn_sinks, v_ssm_a_re=v_ssm_a_re, v_ssm_a_im=v_ssm_a_im, v_ssm_log_dt=v_ssm_log_dt, v_ssm_b_re=v_ssm_b_re, v_ssm_b_im=v_ssm_b_im, v_ssm_c_re=v_ssm_c_re, v_ssm_c_im=v_ssm_c_im, v_ssm_d=v_ssm_d, v_w_glu=v_w_glu, v_b_glu=v_b_glu, v_w_branch_attn=v_w_branch_attn, v_w_branch_ssm=v_w_branch_ssm, v_w_out=v_w_out, v_ffn_norm_g=v_ffn_norm_g, v_w_up=v_w_up, v_conv_w=v_conv_w, v_conv_b=v_conv_b, v_w_down=v_w_down, v_final_norm_g=v_final_norm_g)
    weights = {n: given[n] for n in TWIN_WEIGHTS}
    shared = {n: given[n] for n in SHARED_INPUTS}
    per_example = {n: given[n] for n in ['x']}
    grad_fn = _jax.value_and_grad(_loss, argnums=(0, 1))

    def one_microbatch(ex, loss_target):
        ex = dict(ex)
        diff = ex.pop(TWIN_DIFF_INPUT)
        return grad_fn(weights, diff, {**shared, **ex}, loss_target)

    if N_MICROBATCH == 1:
        loss, (grad_w, grad_x) = one_microbatch(per_example, given["loss_target"])
    else:
        def body(carry, xs):
            loss_sum, grad_sum = carry
            l_k, (gw_k, gx_k) = one_microbatch(xs[0], xs[1])
            with _jax.named_scope("update"):
                return (loss_sum + l_k, _jax.tree.map(_jnp.add, grad_sum, gw_k)), gx_k

        init = (_jnp.zeros((), _jnp.float32), _jax.tree.map(_jnp.zeros_like, weights))
        (loss, grad_w), grad_x = _jax.lax.scan(body, init, (per_example, given["loss_target"]))
    with _jax.named_scope("update"):
        delta_w, new_m, new_v = {}, {}, {}
        for n in TWIN_WEIGHTS:
            delta_w[n], new_m[n], new_v[n] = _adamw(weights[n], grad_w[n], given["m_" + n], given["v_" + n])
    return (loss, grad_x, *[grad_w[n] for n in TWIN_WEIGHTS], *[delta_w[n] for n in TWIN_WEIGHTS],
            *[new_m[n] for n in TWIN_WEIGHTS], *[new_v[n] for n in TWIN_WEIGHTS])
```

```python
import functools
import math

import jax
import jax.numpy as jnp
from jax import lax
from jax.experimental import pallas as pl
from jax.experimental.pallas import tpu as pltpu

F32 = jnp.float32
BF16 = jnp.bfloat16

D_MODEL = 2048
N_Q_HEADS = 16
HEAD_DIM = 64
ATTN_WIDTH = 1024
KV_WIDTH = 128
BLOCK = 128
SSM_WIDTH = 512
SSM_GROUPS = 32
SSM_GROUP = 16
SSM_STATE = 64
D_FF = 5632
IN_COLS = 5888
RMS_EPS = 1e-6
NEG_BIG = -1e30
N_CHIPS = 4
N_DEV = 8

COL_K = 8
COL_V = 9
COL_U = 10
COL_GA = 14
COL_GS = 30

SSM_SPLIT = 4
SSM_U_BLK = 128
SSM_X_BLK = 512
SSM_CHUNK = 256

ADAM_LR = 0.001
ADAM_B1 = 0.9
ADAM_B2 = 0.999
ADAM_EPS = 1e-08
ADAM_WD = 0.01
ADAM_STEP = 10

VMEM_LIMIT_BYTES = 56 * 1024 * 1024
INV_SQRT2 = 1.0 / math.sqrt(2.0)
INV_SQRT2PI = 1.0 / math.sqrt(2.0 * math.pi)
MESH = pl.DeviceIdType.MESH
ANY = pl.BlockSpec(memory_space=pl.ANY)


def _cp(sem):
    return pltpu.CompilerParams(dimension_semantics=sem, vmem_limit_bytes=VMEM_LIMIT_BYTES)


def _gelu(x):
    return 0.5 * x * (1.0 + lax.erf(x * INV_SQRT2))


def _gelu_grad(x):
    return 0.5 * (1.0 + lax.erf(x * INV_SQRT2)) + x * jnp.exp(-0.5 * x * x) * INV_SQRT2PI


def _sigmoid(x):
    return 1.0 / (1.0 + jnp.exp(-x))


def _matmul(a, b, *, ta=False, tb=False, tm, tn, tk, out_dtype, bias=None, res=None, name):
    if ta:
        kdim, m = a.shape
    else:
        m, kdim = a.shape
    if tb:
        n, k2 = b.shape
    else:
        k2, n = b.shape
    assert kdim == k2, (a.shape, b.shape)
    tm, tn, tk = min(tm, m), min(tn, n), min(tk, kdim)
    assert m % tm == 0 and n % tn == 0 and kdim % tk == 0, (name, m, n, kdim, tm, tn, tk)
    nk = kdim // tk
    dn = (((0 if ta else 1,), (1 if tb else 0,)), ((), ()))

    def body(*refs):
        a_ref, b_ref = refs[0], refs[1]
        pos = 2
        bias_ref = res_ref = None
        if bias is not None:
            bias_ref = refs[pos]
            pos += 1
        if res is not None:
            res_ref = refs[pos]
            pos += 1
        o_ref, acc_ref = refs[pos], refs[pos + 1]
        k = pl.program_id(2)

        @pl.when(k == 0)
        def _():
            acc_ref[...] = jnp.zeros_like(acc_ref)

        acc_ref[...] += lax.dot_general(a_ref[...].astype(BF16), b_ref[...].astype(BF16), dn,
                                        preferred_element_type=F32)

        @pl.when(k == nk - 1)
        def _():
            r = acc_ref[...]
            if bias_ref is not None:
                r = r + bias_ref[...]
            if res_ref is not None:
                r = r + res_ref[...]
            o_ref[...] = r.astype(out_dtype)

    a_spec = (pl.BlockSpec((tk, tm), lambda i, j, k: (k, i)) if ta
              else pl.BlockSpec((tm, tk), lambda i, j, k: (i, k)))
    b_spec = (pl.BlockSpec((tn, tk), lambda i, j, k: (j, k)) if tb
              else pl.BlockSpec((tk, tn), lambda i, j, k: (k, j)))
    in_specs = [a_spec, b_spec]
    args = [a, b]
    if bias is not None:
        in_specs.append(pl.BlockSpec((1, tn), lambda i, j, k: (0, j)))
        args.append(bias)
    if res is not None:
        in_specs.append(pl.BlockSpec((tm, tn), lambda i, j, k: (i, j)))
        args.append(res)
    return pl.pallas_call(
        body, grid=(m // tm, n // tn, nk), in_specs=in_specs,
        out_specs=pl.BlockSpec((tm, tn), lambda i, j, k: (i, j)),
        out_shape=jax.ShapeDtypeStruct((m, n), out_dtype),
        scratch_shapes=[pltpu.VMEM((tm, tn), F32)], name=name,
        compiler_params=_cp(("parallel", "parallel", "arbitrary")))(*args)


def _rms_fwd(x, g, *, name):
    l, d = x.shape
    tr = min(256, l)

    def body(x_ref, g_ref, h_ref):
        xf = x_ref[...]
        r = lax.rsqrt(jnp.mean(xf * xf, axis=-1, keepdims=True) + RMS_EPS)
        h_ref[...] = ((xf * r) * g_ref[...]).astype(BF16)

    row = pl.BlockSpec((tr, d), lambda i: (i, 0))
    return pl.pallas_call(
        body, grid=(l // tr,), in_specs=[row, pl.BlockSpec((1, d), lambda i: (0, 0))],
        out_specs=row, out_shape=jax.ShapeDtypeStruct((l, d), BF16), name=name,
        compiler_params=_cp(("parallel",)))(x, g)


def _rms_bwd(dy, x, g, dres, *, with_bf16, name):
    l, d = x.shape
    tr = min(256, l)

    def body(dy_ref, x_ref, g_ref, dres_ref, *outs):
        dx_ref = outs[0]
        dg_ref = outs[-1]
        xf = x_ref[...]
        r = lax.rsqrt(jnp.mean(xf * xf, axis=-1, keepdims=True) + RMS_EPS)
        xhat = xf * r
        dyv = dy_ref[...]
        dxh = dyv * g_ref[...]
        dx = r * (dxh - xhat * jnp.mean(dxh * xhat, axis=-1, keepdims=True)) + dres_ref[...]
        dx_ref[...] = dx
        if with_bf16:
            outs[1][...] = dx.astype(BF16)

        @pl.when(pl.program_id(0) == 0)
        def _():
            dg_ref[...] = jnp.zeros_like(dg_ref)

        dg_ref[...] += jnp.sum(dyv * xhat, axis=0, keepdims=True)

    row = pl.BlockSpec((tr, d), lambda i: (i, 0))
    vec = pl.BlockSpec((1, d), lambda i: (0, 0))
    out_specs = [row] + ([row] if with_bf16 else []) + [vec]
    out_shape = ([jax.ShapeDtypeStruct((l, d), F32)]
                 + ([jax.ShapeDtypeStruct((l, d), BF16)] if with_bf16 else [])
                 + [jax.ShapeDtypeStruct((1, d), F32)])
    return pl.pallas_call(
        body, grid=(l // tr,), in_specs=[row, row, vec, row], out_specs=out_specs,
        out_shape=out_shape, name=name, compiler_params=_cp(("arbitrary",)))(dy, x, g, dres)


def _final_loss(x3, g, target, *, name):
    l, d = x3.shape
    tr = min(256, l)

    def body(x_ref, g_ref, t_ref, dx_ref, dxb_ref, dg_ref, loss_ref):
        xf = x_ref[...]
        gv = g_ref[...]
        r = lax.rsqrt(jnp.mean(xf * xf, axis=-1, keepdims=True) + RMS_EPS)
        xhat = xf * r
        diff = xhat * gv - t_ref[...]
        dout = diff * (1.0 / d)
        dxh = dout * gv
        dx = r * (dxh - xhat * jnp.mean(dxh * xhat, axis=-1, keepdims=True))
        dx_ref[...] = dx
        dxb_ref[...] = dx.astype(BF16)

        @pl.when(pl.program_id(0) == 0)
        def _():
            dg_ref[...] = jnp.zeros_like(dg_ref)
            loss_ref[...] = jnp.zeros_like(loss_ref)

        dg_ref[...] += jnp.sum(dout * xhat, axis=0, keepdims=True)
        part = jnp.sum(jnp.mean(diff * diff, axis=-1, keepdims=True), axis=0, keepdims=True)
        loss_ref[...] += 0.5 * part

    row = pl.BlockSpec((tr, d), lambda i: (i, 0))
    vec = pl.BlockSpec((1, d), lambda i: (0, 0))
    return pl.pallas_call(
        body, grid=(l // tr,), in_specs=[row, vec, row],
        out_specs=[row, row, vec, pl.BlockSpec((1, 1), lambda i: (0, 0))],
        out_shape=[jax.ShapeDtypeStruct((l, d), F32), jax.ShapeDtypeStruct((l, d), BF16),
                   jax.ShapeDtypeStruct((1, d), F32), jax.ShapeDtypeStruct((1, 1), F32)],
        name=name, compiler_params=_cp(("arbitrary",)))(x3, g, target)


def _attn_masks(n):
    q_idx = lax.broadcasted_iota(jnp.int32, (BLOCK, 2 * BLOCK), 0)
    s_idx = lax.broadcasted_iota(jnp.int32, (BLOCK, 2 * BLOCK), 1)
    dist = q_idx + BLOCK - s_idx
    valid = (dist >= 0) & (dist < BLOCK) & ((n > 0) | (s_idx >= BLOCK))
    return dist.astype(F32), valid


def _dup_half(t, kv_head, lo):
    rolled = pltpu.roll(t, HEAD_DIM, axis=1)
    return jnp.where(lo, t, rolled) if kv_head == 0 else jnp.where(lo, rolled, t)


def _head_probs(qm, kdup, dist, valid, sink, head):
    slope = 2.0 ** (-8.0 * (head + 1) / N_Q_HEADS)
    s = lax.dot_general(qm, kdup, (((1,), (1,)), ((), ())), preferred_element_type=F32)
    s = s * (HEAD_DIM ** -0.5) - slope * dist
    s = jnp.where(valid, s, NEG_BIG)
    m = jnp.maximum(jnp.max(s, axis=-1, keepdims=True), sink)
    p = jnp.exp(s - m)
    esink = jnp.exp(sink - m)
    inv = 1.0 / (jnp.sum(p, axis=-1, keepdims=True) + esink)
    return p * inv, esink * inv


def _attn_fwd(proj, sinks, *, name):
    l = proj.shape[0]
    nb = l // BLOCK

    def body(sink_ref, q_ref, kc_ref, kp_ref, vc_ref, vp_ref, o_ref):
        n = pl.program_id(0)
        dist, valid = _attn_masks(n)
        lo = lax.broadcasted_iota(jnp.int32, (1, BLOCK), 1) < HEAD_DIM
        kx = jnp.concatenate([kp_ref[...], kc_ref[...]], axis=0).astype(BF16)
        vx = jnp.concatenate([vp_ref[...], vc_ref[...]], axis=0).astype(BF16)
        for kv_head in range(2):
            kdup = _dup_half(kx, kv_head, lo)
            vdup = _dup_half(vx, kv_head, lo)
            for pr in range(4):
                pair = kv_head * 4 + pr
                qp = q_ref[:, pair * 128:(pair + 1) * 128].astype(BF16)
                o_pair = jnp.zeros((BLOCK, 128), F32)
                for half in range(2):
                    head = 2 * pair + half
                    sel = lo if half == 0 else jnp.logical_not(lo)
                    qm = jnp.where(sel, qp, jnp.zeros_like(qp))
                    p, _ = _head_probs(qm, kdup, dist, valid, sink_ref[head], head)
                    o = jnp.dot(p.astype(BF16), vdup, preferred_element_type=F32)
                    o_pair = o_pair + jnp.where(sel, o, 0.0)
                o_ref[:, pair * 128:(pair + 1) * 128] = o_pair.astype(BF16)

    kv = lambda col, prev: pl.BlockSpec(
        (BLOCK, KV_WIDTH), (lambda n: (jnp.maximum(n - 1, 0), col)) if prev else (lambda n: (n, col)))
    return pl.pallas_call(
        body, grid=(nb,),
        in_specs=[pl.BlockSpec(memory_space=pltpu.SMEM),
                  pl.BlockSpec((BLOCK, ATTN_WIDTH), lambda n: (n, 0)),
                  kv(COL_K, False), kv(COL_K, True), kv(COL_V, False), kv(COL_V, True)],
        out_specs=pl.BlockSpec((BLOCK, ATTN_WIDTH), lambda n: (n, 0)),
        out_shape=jax.ShapeDtypeStruct((l, ATTN_WIDTH), BF16), name=name,
        compiler_params=_cp(("parallel",)))(sinks, proj, proj, proj, proj, proj)


def _attn_bwd(proj, sinks, dattn, *, name):
    l = proj.shape[0]
    nb = l // BLOCK

    def body(sink_ref, q_ref, kc_ref, kp_ref, vc_ref, vp_ref, do_ref,
             dq_ref, dkc_ref, dkp_ref, dvc_ref, dvp_ref, dsink_ref):
        n = pl.program_id(0)
        dist, valid = _attn_masks(n)
        lane = lax.broadcasted_iota(jnp.int32, (1, BLOCK), 1)
        lo = lane < HEAD_DIM
        kx = jnp.concatenate([kp_ref[...], kc_ref[...]], axis=0).astype(BF16)
        vx = jnp.concatenate([vp_ref[...], vc_ref[...]], axis=0).astype(BF16)
        dsink = jnp.zeros((1, BLOCK), F32)
        dk_heads, dv_heads = [], []
        for kv_head in range(2):
            kdup = _dup_half(kx, kv_head, lo)
            vdup = _dup_half(vx, kv_head, lo)
            dk_acc = jnp.zeros((2 * BLOCK, 128), F32)
            dv_acc = jnp.zeros((2 * BLOCK, 128), F32)
            for pr in range(4):
                pair = kv_head * 4 + pr
                qp = q_ref[:, pair * 128:(pair + 1) * 128].astype(BF16)
                dop = do_ref[:, pair * 128:(pair + 1) * 128]
                dq_pair = jnp.zeros((BLOCK, 128), F32)
                for half in range(2):
                    head = 2 * pair + half
                    sel = lo if half == 0 else jnp.logical_not(lo)
                    qm = jnp.where(sel, qp, jnp.zeros_like(qp))
                    dom = jnp.where(sel, dop, jnp.zeros_like(dop))
                    p, psink = _head_probs(qm, kdup, dist, valid, sink_ref[head], head)
                    dp = lax.dot_general(dom, vdup, (((1,), (1,)), ((), ())), preferred_element_type=F32)
                    delta = jnp.sum(p * dp, axis=-1, keepdims=True)
                    ds = (p * (dp - delta) * (HEAD_DIM ** -0.5)).astype(BF16)
                    dsink = dsink + jnp.where(lane == head, jnp.sum(-psink * delta), 0.0)
                    dq = jnp.dot(ds, kdup, preferred_element_type=F32)
                    dq_pair = dq_pair + jnp.where(sel, dq, 0.0)
                    dk_acc = dk_acc + lax.dot_general(ds, qm, (((0,), (0,)), ((), ())),
                                                      preferred_element_type=F32)
                    dv_acc = dv_acc + lax.dot_general(p.astype(BF16), dom, (((0,), (0,)), ((), ())),
                                                      preferred_element_type=F32)
                dq_ref[:, pair * 128:(pair + 1) * 128] = dq_pair.astype(BF16)
            dk_heads.append(dk_acc + pltpu.roll(dk_acc, HEAD_DIM, axis=1))
            dv_heads.append(dv_acc + pltpu.roll(dv_acc, HEAD_DIM, axis=1))
        dk = jnp.where(lo, dk_heads[0], dk_heads[1])
        dv = jnp.where(lo, dv_heads[0], dv_heads[1])
        dkp_ref[...] = dk[:BLOCK]
        dkc_ref[...] = dk[BLOCK:]
        dvp_ref[...] = dv[:BLOCK]
        dvc_ref[...] = dv[BLOCK:]

        @pl.when(n == 0)
        def _():
            dsink_ref[...] = jnp.zeros_like(dsink_ref)

        dsink_ref[...] += dsink

    kv = lambda col, prev: pl.BlockSpec(
        (BLOCK, KV_WIDTH), (lambda n: (jnp.maximum(n - 1, 0), col)) if prev else (lambda n: (n, col)))
    qspec = pl.BlockSpec((BLOCK, ATTN_WIDTH), lambda n: (n, 0))
    kvout = pl.BlockSpec((BLOCK, KV_WIDTH), lambda n: (n, 0))
    kvshape = jax.ShapeDtypeStruct((l, KV_WIDTH), F32)
    return pl.pallas_call(
        body, grid=(nb,),
        in_specs=[pl.BlockSpec(memory_space=pltpu.SMEM), qspec,
                  kv(COL_K, False), kv(COL_K, True), kv(COL_V, False), kv(COL_V, True), qspec],
        out_specs=[qspec, kvout, kvout, kvout, kvout, pl.BlockSpec((1, BLOCK), lambda n: (0, 0))],
        out_shape=[jax.ShapeDtypeStruct((l, ATTN_WIDTH), BF16), kvshape, kvshape, kvshape, kvshape,
                   jax.ShapeDtypeStruct((1, BLOCK), F32)],
        name=name, compiler_params=_cp(("arbitrary",)))(sinks, proj, proj, proj, proj, proj, dattn)


def _kv_grad_merge(dkc, dkp, dvc, dvp, *, name):
    l = dkc.shape[0]
    nb = l // BLOCK

    def body(dkc_ref, dkp_ref, dvc_ref, dvp_ref, o_ref):
        last = pl.program_id(0) == nb - 1
        o_ref[:, :KV_WIDTH] = (dkc_ref[...] + jnp.where(last, 0.0, dkp_ref[...])).astype(BF16)
        o_ref[:, KV_WIDTH:] = (dvc_ref[...] + jnp.where(last, 0.0, dvp_ref[...])).astype(BF16)

    cur = pl.BlockSpec((BLOCK, KV_WIDTH), lambda n: (n, 0))
    nxt = pl.BlockSpec((BLOCK, KV_WIDTH), lambda n: (jnp.minimum(n + 1, nb - 1), 0))
    return pl.pallas_call(
        body, grid=(nb,), in_specs=[cur, nxt, cur, nxt],
        out_specs=pl.BlockSpec((BLOCK, 2 * KV_WIDTH), lambda n: (n, 0)),
        out_shape=jax.ShapeDtypeStruct((l, 2 * KV_WIDTH), BF16), name=name,
        compiler_params=_cp(("parallel",)))(dkc, dkp, dvc, dvp)


def _discretize(a_re, a_im, log_dt, b_re, b_im):
    dt = jnp.exp(log_dt)
    mag = jnp.exp(a_re * dt)
    ab_re = mag * jnp.cos(a_im * dt)
    ab_im = mag * jnp.sin(a_im * dt)
    nr = ab_re - 1.0
    ni = ab_im
    den = a_re * a_re + a_im * a_im
    z_re = (nr * a_re + ni * a_im) / den
    z_im = (ni * a_re - nr * a_im) / den
    bb_re = z_re * b_re - z_im * b_im
    bb_im = z_re * b_im + z_im * b_re
    return ab_re, ab_im, bb_re, bb_im


def _ssm_disc_fwd(a_re, a_im, log_dt, b_re, b_im, *, name):
    def body(ar, ai, ld, br, bi, o_ar, o_ai, o_br, o_bi):
        r = _discretize(ar[...], ai[...], ld[...], br[...], bi[...])
        o_ar[...], o_ai[...], o_br[...], o_bi[...] = r

    col = jax.ShapeDtypeStruct(a_re.shape, F32)
    mat = jax.ShapeDtypeStruct(b_re.shape, F32)
    return pl.pallas_call(body, out_shape=[col, col, mat, mat], name=name)(a_re, a_im, log_dt, b_re, b_im)


def _ssm_disc_bwd(a_re, a_im, log_dt, b_re, b_im, d_ab_re, d_ab_im, d_bb_re, d_bb_im, *, name):
    def body(ar, ai, ld, br, bi, g0, g1, g2, g3, o_ar, o_ai, o_ld, o_br, o_bi):
        _, vjp = jax.vjp(_discretize, ar[...], ai[...], ld[...], br[...], bi[...])
        r = vjp((g0[...], g1[...], g2[...], g3[...]))
        o_ar[...], o_ai[...], o_ld[...], o_br[...], o_bi[...] = r

    col = jax.ShapeDtypeStruct(a_re.shape, F32)
    mat = jax.ShapeDtypeStruct(b_re.shape, F32)
    return pl.pallas_call(body, out_shape=[col, col, col, mat, mat], name=name)(
        a_re, a_im, log_dt, b_re, b_im, d_ab_re, d_ab_im, d_bb_re, d_bb_im)


def _shift_rows(x, d, rows, *, down):
    t = x.shape[0]
    if down:
        return jnp.where(rows >= d, pltpu.roll(x, d, axis=0), 0.0)
    return jnp.where(rows < t - d, pltpu.roll(x, t - d, axis=0), 0.0)


def _scan_chunk(xr, xi, ar, ai, *, down):
    t = xr.shape[0]
    rows = lax.broadcasted_iota(jnp.int32, (t, 1), 0)
    pr, pi = ar, ai
    d = 1
    while d < t:
        sr = _shift_rows(xr, d, rows, down=down)
        si = _shift_rows(xi, d, rows, down=down)
        xr, xi = xr + pr * sr - pi * si, xi + pr * si + pi * sr
        pr, pi = pr * pr - pi * pi, 2.0 * pr * pi
        d *= 2
    return xr, xi


def _ssm_fwd(proj, ab, bd, cd, dskip, *, name):
    l = proj.shape[0]
    t = min(SSM_CHUNK, l)
    nc = l // t

    def body(u_ref, ab_ref, bd_ref, cd_ref, ds_ref, y_ref, gy_ref, xs_ref, carry_ref):
        c = pl.program_id(1)

        @pl.when(c == 0)
        def _():
            carry_ref[...] = jnp.zeros_like(carry_ref)

        u = u_ref[...]
        ar, ai = ab_ref[0, 0:1, :], ab_ref[0, 1:2, :]
        bu = jnp.dot(u.astype(BF16), bd_ref[0], preferred_element_type=F32)
        rows = lax.broadcasted_iota(jnp.int32, (t, 1), 0)
        cr, ci = carry_ref[0:1, :], carry_ref[1:2, :]
        xr = bu[:, :SSM_X_BLK] + jnp.where(rows == 0, ar * cr - ai * ci, 0.0)
        xi = bu[:, SSM_X_BLK:] + jnp.where(rows == 0, ar * ci + ai * cr, 0.0)
        xr, xi = _scan_chunk(xr, xi, ar, ai, down=True)
        xs_ref[0, :, :SSM_X_BLK] = xr
        xs_ref[0, :, SSM_X_BLK:] = xi
        carry_ref[0:1, :] = xs_ref[0, t - 1:t, :SSM_X_BLK]
        carry_ref[1:2, :] = xs_ref[0, t - 1:t, SSM_X_BLK:]
        y = jnp.dot(xs_ref[0].astype(BF16), cd_ref[0], preferred_element_type=F32) + ds_ref[...] * u
        y_ref[...] = y
        gy_ref[...] = _gelu(y).astype(BF16)

    blk = lambda shape: pl.BlockSpec((1,) + shape, lambda j, c: (j, 0, 0))
    ycol = pl.BlockSpec((t, SSM_U_BLK), lambda j, c: (c, j))
    return pl.pallas_call(
        body, grid=(SSM_SPLIT, nc),
        in_specs=[pl.BlockSpec((t, SSM_U_BLK), lambda j, c: (c, COL_U + j)),
                  blk((2, SSM_X_BLK)), blk((SSM_U_BLK, 2 * SSM_X_BLK)), blk((2 * SSM_X_BLK, SSM_U_BLK)),
                  pl.BlockSpec((1, SSM_U_BLK), lambda j, c: (0, j))],
        out_specs=[ycol, ycol, pl.BlockSpec((1, t, 2 * SSM_X_BLK), lambda j, c: (j, c, 0))],
        out_shape=[jax.ShapeDtypeStruct((l, SSM_WIDTH), F32), jax.ShapeDtypeStruct((l, SSM_WIDTH), BF16),
                   jax.ShapeDtypeStruct((SSM_SPLIT, l, 2 * SSM_X_BLK), F32)],
        scratch_shapes=[pltpu.VMEM((2, SSM_X_BLK), F32)], name=name,
        compiler_params=_cp(("parallel", "arbitrary")))(proj, ab, bd, cd, dskip)


def _ssm_bwd(proj, y, dgy, xs, ab, bdt, cdt, dskip, *, name):
    l = proj.shape[0]
    t = min(SSM_CHUNK, l)
    nc = l // t

    def body(u_ref, y_ref, dgy_ref, xs_ref, halo_ref, ab_ref, bdt_ref, cdt_ref, ds_ref,
             du_ref, dbd_ref, dcd_ref, dab_ref, dd_ref, carry_ref):
        c = pl.program_id(1)
        ci_ = nc - 1 - c

        @pl.when(c == 0)
        def _():
            carry_ref[...] = jnp.zeros_like(carry_ref)
            dbd_ref[...] = jnp.zeros_like(dbd_ref)
            dcd_ref[...] = jnp.zeros_like(dcd_ref)
            dab_ref[...] = jnp.zeros_like(dab_ref)
            dd_ref[...] = jnp.zeros_like(dd_ref)

        u = u_ref[...]
        dy = dgy_ref[...] * _gelu_grad(y_ref[...])
        dyb = dy.astype(BF16)
        ar, ai = ab_ref[0, 0:1, :], ab_ref[0, 1:2, :]
        g = jnp.dot(dyb, cdt_ref[0], preferred_element_type=F32)
        rows = lax.broadcasted_iota(jnp.int32, (t, 1), 0)
        cr, ci = carry_ref[0:1, :], carry_ref[1:2, :]
        lr = g[:, :SSM_X_BLK] + jnp.where(rows == t - 1, ar * cr + ai * ci, 0.0)
        li = g[:, SSM_X_BLK:] + jnp.where(rows == t - 1, ar * ci - ai * cr, 0.0)
        lr, li = _scan_chunk(lr, li, ar, -ai, down=False)
        lam = jnp.concatenate([lr, li], axis=1)
        carry_ref[0:1, :] = lr[0:1, :]
        carry_ref[1:2, :] = li[0:1, :]
        lamb = lam.astype(BF16)
        du_ref[...] = (jnp.dot(lamb, bdt_ref[0], preferred_element_type=F32) + ds_ref[...] * dy).astype(BF16)
        dbd_ref[0] += lax.dot_general(u.astype(BF16), lamb, (((0,), (0,)), ((), ())),
                                      preferred_element_type=F32)
        xs = xs_ref[0]
        dcd_ref[0] += lax.dot_general(xs.astype(BF16), dyb, (((0,), (0,)), ((), ())),
                                      preferred_element_type=F32)
        halo = jnp.where(ci_ > 0, halo_ref[0, 7:8, :], 0.0)
        xprev = jnp.where(rows == 0, halo, pltpu.roll(xs, 1, axis=0))
        xpr, xpi = xprev[:, :SSM_X_BLK], xprev[:, SSM_X_BLK:]
        dab_ref[0, 0:1, :] += jnp.sum(lr * xpr + li * xpi, axis=0, keepdims=True)
        dab_ref[0, 1:2, :] += jnp.sum(li * xpr - lr * xpi, axis=0, keepdims=True)
        dd_ref[...] += jnp.sum(dy * u, axis=0, keepdims=True)

    blk = lambda shape: pl.BlockSpec((1,) + shape, lambda j, c: (j, 0, 0))
    rev = lambda j, c: (nc - 1 - c, j)
    ycol = pl.BlockSpec((t, SSM_U_BLK), rev)
    hb = t // 8
    return pl.pallas_call(
        body, grid=(SSM_SPLIT, nc),
        in_specs=[pl.BlockSpec((t, SSM_U_BLK), lambda j, c: (nc - 1 - c, COL_U + j)), ycol, ycol,
                  pl.BlockSpec((1, t, 2 * SSM_X_BLK), lambda j, c: (j, nc - 1 - c, 0)),
                  pl.BlockSpec((1, 8, 2 * SSM_X_BLK),
                               lambda j, c: (j, jnp.maximum((nc - 1 - c) * hb - 1, 0), 0)),
                  blk((2, SSM_X_BLK)), blk((2 * SSM_X_BLK, SSM_U_BLK)), blk((SSM_U_BLK, 2 * SSM_X_BLK)),
                  pl.BlockSpec((1, SSM_U_BLK), lambda j, c: (0, j))],
        out_specs=[ycol, blk((SSM_U_BLK, 2 * SSM_X_BLK)), blk((2 * SSM_X_BLK, SSM_U_BLK)),
                   blk((2, SSM_X_BLK)), pl.BlockSpec((1, SSM_U_BLK), lambda j, c: (0, j))],
        out_shape=[jax.ShapeDtypeStruct((l, SSM_WIDTH), BF16),
                   jax.ShapeDtypeStruct((SSM_SPLIT, SSM_U_BLK, 2 * SSM_X_BLK), F32),
                   jax.ShapeDtypeStruct((SSM_SPLIT, 2 * SSM_X_BLK, SSM_U_BLK), F32),
                   jax.ShapeDtypeStruct((SSM_SPLIT, 2, SSM_X_BLK), F32),
                   jax.ShapeDtypeStruct((1, SSM_WIDTH), F32)],
        scratch_shapes=[pltpu.VMEM((2, SSM_X_BLK), F32)], name=name,
        compiler_params=_cp(("parallel", "arbitrary")))(proj, y, dgy, xs, xs, ab, bdt, cdt, dskip)


def _block_diag(t):
    s, g, a, b = t.shape
    return jnp.einsum('sgab,gk->sgakb', t, jnp.eye(g, dtype=t.dtype)).reshape(s, g * a, g * b)


def _block_diag_take(t, a, b):
    s = t.shape[0]
    return jnp.einsum('sgakb,gk->sgab', t.reshape(s, 8, a, 8, b), jnp.eye(8, dtype=t.dtype))


def _glu_fwd(z, *, name):
    l = z.shape[0]
    tr = min(512, l)

    def body(zv_ref, zg_ref, o_ref):
        o_ref[...] = (zv_ref[...] * _sigmoid(zg_ref[...])).astype(BF16)

    return pl.pallas_call(
        body, grid=(l // tr,),
        in_specs=[pl.BlockSpec((tr, SSM_WIDTH), lambda i: (i, 0)), pl.BlockSpec((tr, SSM_WIDTH), lambda i: (i, 1))],
        out_specs=pl.BlockSpec((tr, SSM_WIDTH), lambda i: (i, 0)),
        out_shape=jax.ShapeDtypeStruct((l, SSM_WIDTH), BF16), name=name,
        compiler_params=_cp(("parallel",)))(z, z)


def _glu_bwd(dssm, z, *, name):
    l = z.shape[0]
    tr = min(512, l)

    def body(d_ref, zv_ref, zg_ref, dz_ref, db_ref):
        d = d_ref[...]
        sg = _sigmoid(zg_ref[...])
        dv = d * sg
        dg = d * zv_ref[...] * sg * (1.0 - sg)
        dz_ref[:, :SSM_WIDTH] = dv.astype(BF16)
        dz_ref[:, SSM_WIDTH:] = dg.astype(BF16)

        @pl.when(pl.program_id(0) == 0)
        def _():
            db_ref[...] = jnp.zeros_like(db_ref)

        db_ref[:, :SSM_WIDTH] += jnp.sum(dv, axis=0, keepdims=True)
        db_ref[:, SSM_WIDTH:] += jnp.sum(dg, axis=0, keepdims=True)

    half = lambda k: pl.BlockSpec((tr, SSM_WIDTH), lambda i: (i, k))
    return pl.pallas_call(
        body, grid=(l // tr,), in_specs=[half(0), half(0), half(1)],
        out_specs=[pl.BlockSpec((tr, 2 * SSM_WIDTH), lambda i: (i, 0)),
                   pl.BlockSpec((1, 2 * SSM_WIDTH), lambda i: (0, 0))],
        out_shape=[jax.ShapeDtypeStruct((l, 2 * SSM_WIDTH), BF16), jax.ShapeDtypeStruct((1, 2 * SSM_WIDTH), F32)],
        name=name, compiler_params=_cp(("arbitrary",)))(dssm, z, z)


GATE_TC = 256
GATE_NJ = D_MODEL // GATE_TC


def _merge_fwd(proj, a, s, *, name):
    l = a.shape[0]
    tr = min(512, l)

    def body(ga_ref, gs_ref, a_ref, s_ref, o_ref):
        o_ref[...] = (_sigmoid(ga_ref[...]) * a_ref[...] + _sigmoid(gs_ref[...]) * s_ref[...]).astype(BF16)

    own = pl.BlockSpec((tr, GATE_TC), lambda i, j: (i, j))
    return pl.pallas_call(
        body, grid=(l // tr, GATE_NJ),
        in_specs=[pl.BlockSpec((tr, GATE_TC), lambda i, j: (i, COL_GA // 2 + j)),
                  pl.BlockSpec((tr, GATE_TC), lambda i, j: (i, COL_GS // 2 + j)), own, own],
        out_specs=own, out_shape=jax.ShapeDtypeStruct((l, D_MODEL), BF16), name=name,
        compiler_params=_cp(("parallel", "parallel")))(proj, proj, a, s)


def _merge_bwd(proj, a, s, dm, *, name):
    l = a.shape[0]
    tr = min(512, l)

    def body(ga_ref, gs_ref, a_ref, s_ref, dm_ref, da_ref, ds_ref, dga_ref, dgs_ref):
        d = dm_ref[...]
        sa = _sigmoid(ga_ref[...])
        ss = _sigmoid(gs_ref[...])
        da_ref[...] = (d * sa).astype(BF16)
        ds_ref[...] = (d * ss).astype(BF16)
        dga_ref[...] = (d * a_ref[...] * sa * (1.0 - sa)).astype(BF16)
        dgs_ref[...] = (d * s_ref[...] * ss * (1.0 - ss)).astype(BF16)

    own = pl.BlockSpec((tr, GATE_TC), lambda i, j: (i, j))
    wide = jax.ShapeDtypeStruct((l, D_MODEL), BF16)
    return pl.pallas_call(
        body, grid=(l // tr, GATE_NJ),
        in_specs=[pl.BlockSpec((tr, GATE_TC), lambda i, j: (i, COL_GA // 2 + j)),
                  pl.BlockSpec((tr, GATE_TC), lambda i, j: (i, COL_GS // 2 + j)), own, own, own],
        out_specs=[own, own, own, own], out_shape=[wide, wide, wide, wide], name=name,
        compiler_params=_cp(("parallel", "parallel")))(proj, proj, a, s, dm)


FF_TC = 512
FF_NJ = D_FF // FF_TC


def _conv_gate(g_ref, halo_ref, w_ref, b_ref, first):
    tr = g_ref.shape[0]
    gate = g_ref[...]
    halo = jnp.where(first, 0.0, halo_ref[...])
    ext = jnp.concatenate([halo, gate], axis=0)
    gm1 = pltpu.roll(ext, 1, axis=0)[8:8 + tr]
    gm2 = pltpu.roll(ext, 2, axis=0)[8:8 + tr]
    gc = b_ref[...] + w_ref[0:1, :] * gm2 + w_ref[1:2, :] * gm1 + w_ref[2:3, :] * gate
    return gc, gate, gm1, gm2


def _ff_specs(tr):
    hb = tr // 8
    val = pl.BlockSpec((tr, FF_TC), lambda j, i: (i, j))
    gate = pl.BlockSpec((tr, FF_TC), lambda j, i: (i, FF_NJ + j))
    halo = pl.BlockSpec((8, FF_TC), lambda j, i: (jnp.maximum(i * hb - 1, 0), FF_NJ + j))
    w = pl.BlockSpec((3, FF_TC), lambda j, i: (0, j))
    b = pl.BlockSpec((1, FF_TC), lambda j, i: (0, j))
    return val, gate, halo, w, b


def _ffn_act_fwd(up, conv_w, conv_b, *, name):
    l = up.shape[0]
    tr = min(256, l)

    def body(v_ref, g_ref, halo_ref, w_ref, b_ref, o_ref):
        gc, _, _, _ = _conv_gate(g_ref, halo_ref, w_ref, b_ref, pl.program_id(1) == 0)
        o_ref[...] = (v_ref[...] * _gelu(gc)).astype(BF16)

    val, gate, halo, w, b = _ff_specs(tr)
    return pl.pallas_call(
        body, grid=(FF_NJ, l // tr), in_specs=[val, gate, halo, w, b], out_specs=val,
        out_shape=jax.ShapeDtypeStruct((l, D_FF), BF16), name=name,
        compiler_params=_cp(("parallel", "parallel")))(up, up, up, conv_w, conv_b)


def _ffn_act_bwd(dact, up, conv_w, conv_b, *, name):
    l = up.shape[0]
    tr = min(256, l)

    def body(d_ref, v_ref, g_ref, halo_ref, w_ref, b_ref, dup_ref, dgc_ref, dw_ref, db_ref):
        gc, gate, gm1, gm2 = _conv_gate(g_ref, halo_ref, w_ref, b_ref, pl.program_id(1) == 0)
        d = d_ref[...]
        dup_ref[...] = (d * _gelu(gc)).astype(BF16)
        dgc = d * v_ref[...] * _gelu_grad(gc)
        dgc_ref[...] = dgc

        @pl.when(pl.program_id(1) == 0)
        def _():
            dw_ref[...] = jnp.zeros_like(dw_ref)
            db_ref[...] = jnp.zeros_like(db_ref)

        dw_ref[0:1, :] += jnp.sum(dgc * gm2, axis=0, keepdims=True)
        dw_ref[1:2, :] += jnp.sum(dgc * gm1, axis=0, keepdims=True)
        dw_ref[2:3, :] += jnp.sum(dgc * gate, axis=0, keepdims=True)
        db_ref[...] += jnp.sum(dgc, axis=0, keepdims=True)

    val, gate, halo, w, b = _ff_specs(tr)
    return pl.pallas_call(
        body, grid=(FF_NJ, l // tr), in_specs=[val, val, gate, halo, w, b],
        out_specs=[val, val, w, b],
        out_shape=[jax.ShapeDtypeStruct((l, 2 * D_FF), BF16), jax.ShapeDtypeStruct((l, D_FF), F32),
                   jax.ShapeDtypeStruct((3, D_FF), F32), jax.ShapeDtypeStruct((1, D_FF), F32)],
        name=name, compiler_params=_cp(("parallel", "arbitrary")))(dact, up, up, up, conv_w, conv_b)


def _ffn_conv_bwd(dgc, dup, conv_w, *, name):
    l = dgc.shape[0]
    tr = min(256, l)
    ni = l // tr
    hb = tr // 8

    def body(d_ref, halo_ref, w_ref, dup_in, o_ref):
        del dup_in
        d = d_ref[...]
        halo = jnp.where(pl.program_id(1) == ni - 1, 0.0, halo_ref[...])
        ext = jnp.concatenate([d, halo], axis=0)
        dp1 = pltpu.roll(ext, tr + 7, axis=0)[:tr]
        dp2 = pltpu.roll(ext, tr + 6, axis=0)[:tr]
        o_ref[...] = (w_ref[2:3, :] * d + w_ref[1:2, :] * dp1 + w_ref[0:1, :] * dp2).astype(BF16)

    return pl.pallas_call(
        body, grid=(FF_NJ, ni),
        in_specs=[pl.BlockSpec((tr, FF_TC), lambda j, i: (i, j)),
                  pl.BlockSpec((8, FF_TC), lambda j, i: (jnp.minimum((i + 1) * hb, l // 8 - 1), j)),
                  pl.BlockSpec((3, FF_TC), lambda j, i: (0, j)), ANY],
        out_specs=pl.BlockSpec((tr, FF_TC), lambda j, i: (i, FF_NJ + j)),
        out_shape=jax.ShapeDtypeStruct((l, 2 * D_FF), BF16), input_output_aliases={3: 0}, name=name,
        compiler_params=_cp(("parallel", "parallel")))(dgc, dgc, conv_w, dup)


def _col_sum(a, *, name):
    l, n = a.shape
    tr = min(512, l)

    def body(a_ref, o_ref):
        @pl.when(pl.program_id(0) == 0)
        def _():
            o_ref[...] = jnp.zeros_like(o_ref)

        o_ref[...] += jnp.sum(a_ref[...].astype(F32), axis=0, keepdims=True)

    return pl.pallas_call(
        body, grid=(l // tr,), in_specs=[pl.BlockSpec((tr, n), lambda i: (i, 0))],
        out_specs=pl.BlockSpec((1, n), lambda i: (0, 0)), out_shape=jax.ShapeDtypeStruct((1, n), F32),
        name=name, compiler_params=_cp(("arbitrary",)))(a)


def _local_step(x, target, wts, small):
    l = x.shape[0]
    mm = _matmul

    col = lambda t: t.reshape(SSM_GROUPS * SSM_STATE, 1)
    a_re, a_im = col(small['ssm_a_re']), col(small['ssm_a_im'])
    log_dt = jnp.repeat(small['ssm_log_dt'].reshape(SSM_GROUPS), SSM_STATE).reshape(-1, 1)
    b_re = small['ssm_b_re'].reshape(SSM_GROUPS * SSM_STATE, SSM_GROUP)
    b_im = small['ssm_b_im'].reshape(SSM_GROUPS * SSM_STATE, SSM_GROUP)
    ab_re, ab_im, bb_re, bb_im = _ssm_disc_fwd(a_re, a_im, log_dt, b_re, b_im, name="ssm_disc_fwd")
    ab = jnp.stack([ab_re.reshape(SSM_SPLIT, SSM_X_BLK), ab_im.reshape(SSM_SPLIT, SSM_X_BLK)], axis=1)
    to_bd = lambda t: _block_diag(t.reshape(SSM_SPLIT, 8, SSM_STATE, SSM_GROUP).transpose(0, 1, 3, 2))
    bd = jnp.concatenate([to_bd(bb_re), to_bd(bb_im)], axis=2)
    c_re = small['ssm_c_re'].reshape(SSM_SPLIT, 8, SSM_GROUP, SSM_STATE)
    c_im = small['ssm_c_im'].reshape(SSM_SPLIT, 8, SSM_GROUP, SSM_STATE)
    cdt = jnp.concatenate([_block_diag(c_re), -_block_diag(c_im)], axis=2)
    bd_b, cdt_b = bd.astype(BF16), cdt.astype(BF16)
    bdt_b, cd_b = bd_b.transpose(0, 2, 1), cdt_b.transpose(0, 2, 1)
    dskip = small['ssm_d'].reshape(1, SSM_WIDTH)

    h1 = _rms_fwd(x, small['attn_norm_g'], name="rms1_fwd")
    proj = mm(h1, wts['w_in_t'], tb=True, tm=512, tn=2944, tk=512, out_dtype=F32,
              bias=small['b_in'], name="mm_in")
    attn = _attn_fwd(proj, small['attn_sinks'].reshape(N_Q_HEADS), name="attn_fwd")
    y, gy, xs = _ssm_fwd(proj, ab, bd_b, cd_b, dskip, name="ssm_fwd")
    z = mm(gy, wts['w_glu_t'], tb=True, tm=1024, tn=1024, tk=512, out_dtype=F32,
           bias=small['b_glu'], name="mm_glu")
    ssm = _glu_fwd(z, name="glu_fwd")
    a_br = mm(attn, wts['w_ba_t'], tb=True, tm=1024, tn=1024, tk=1024, out_dtype=F32, name="mm_ba")
    s_br = mm(ssm, wts['w_bs_t'], tb=True, tm=1024, tn=1024, tk=512, out_dtype=F32, name="mm_bs")
    merged = _merge_fwd(proj, a_br, s_br, name="merge_fwd")
    x2 = mm(merged, wts['w_out'], tm=1024, tn=1024, tk=512, out_dtype=F32, res=x, name="mm_out")
    h2 = _rms_fwd(x2, small['ffn_norm_g'], name="rms2_fwd")
    up = mm(h2, wts['w_up_t'], tb=True, tm=1024, tn=1024, tk=512, out_dtype=F32, name="mm_up")
    conv_w, conv_b = small['conv_w'], small['conv_b']
    act = _ffn_act_fwd(up, conv_w, conv_b, name="ffn_act_fwd")
    x3 = mm(act, wts['w_down'], tm=1024, tn=1024, tk=512, out_dtype=F32, res=x2, name="mm_down")
    dx3, dx3b, d_g3, loss = _final_loss(x3, small['final_norm_g'].reshape(1, D_MODEL), target, name="final_loss")

    grads, sgr = {}, {}
    sgr['final_norm_g'] = d_g3.reshape(D_MODEL)
    dact = mm(dx3b, wts['w_down'], tb=True, tm=512, tn=2816, tk=512, out_dtype=F32, name="mm_dact")
    grads['w_down'] = mm(act, dx3b, ta=True, tm=512, tn=1024, tk=1024, out_dtype=BF16, name="mm_dw_down")
    dup, dgc, sgr['conv_w'], sgr['conv_b'] = _ffn_act_bwd(dact, up, conv_w, conv_b, name="ffn_act_bwd")
    dup = _ffn_conv_bwd(dgc, dup, conv_w, name="ffn_conv_bwd")
    grads['w_up_t'] = mm(dup, h2, ta=True, tm=1024, tn=1024, tk=1024, out_dtype=BF16, name="mm_dw_up")
    dh2 = mm(dup, wts['w_up_t'], tm=1024, tn=1024, tk=1024, out_dtype=F32, name="mm_dh2")
    dx2, dx2b, sgr['ffn_norm_g'] = _rms_bwd(dh2, x2, small['ffn_norm_g'], dx3, with_bf16=True, name="rms2_bwd")

    dm = mm(dx2b, wts['w_out'], tb=True, tm=1024, tn=1024, tk=512, out_dtype=F32, name="mm_dmerged")
    grads['w_out'] = mm(merged, dx2b, ta=True, tm=1024, tn=1024, tk=1024, out_dtype=BF16, name="mm_dw_out")
    d_a, d_s, dga, dgs = _merge_bwd(proj, a_br, s_br, dm, name="merge_bwd")
    dattn = mm(d_a, wts['w_ba_t'], tm=1024, tn=1024, tk=512, out_dtype=BF16, name="mm_dattn")
    grads['w_ba_t'] = mm(d_a, attn, ta=True, tm=1024, tn=1024, tk=1024, out_dtype=BF16, name="mm_dw_ba")
    dssm = mm(d_s, wts['w_bs_t'], tm=1024, tn=512, tk=512, out_dtype=F32, name="mm_dssm")
    grads['w_bs_t'] = mm(d_s, ssm, ta=True, tm=1024, tn=512, tk=1024, out_dtype=BF16, name="mm_dw_bs")
    dz, sgr['b_glu'] = _glu_bwd(dssm, z, name="glu_bwd")
    grads['w_glu_t'] = mm(dz, gy, ta=True, tm=1024, tn=512, tk=1024, out_dtype=BF16, name="mm_dw_glu")
    dgy = mm(dz, wts['w_glu_t'], tm=1024, tn=512, tk=1024, out_dtype=F32, name="mm_dgy")
    du, d_bd, d_cd, d_ab, sgr['ssm_d'] = _ssm_bwd(proj, y, dgy, xs, ab, bdt_b, cdt_b, dskip, name="ssm_bwd")
    dq, dkc, dkp, dvc, dvp, dsink = _attn_bwd(proj, small['attn_sinks'].reshape(N_Q_HEADS), dattn, name="attn_bwd")
    dkv = _kv_grad_merge(dkc, dkp, dvc, dvp, name="kv_grad_merge")
    sgr['attn_sinks'] = dsink[:, :N_Q_HEADS]
    dproj = jnp.concatenate([dq, dkv, du, dga, dgs], axis=1)
    sgr['b_in'] = _col_sum(dproj, name="col_sum_dproj")
    grads['w_in_t'] = mm(dproj, h1, ta=True, tm=2944, tn=1024, tk=512, out_dtype=BF16, name="mm_dw_in")
    dh1 = mm(dproj, wts['w_in_t'], tm=1024, tn=1024, tk=2944, out_dtype=F32, name="mm_dh1")
    grad_x, sgr['attn_norm_g'] = _rms_bwd(dh1, x, small['attn_norm_g'], dx2, with_bf16=False, name="rms1_bwd")

    from_bd = lambda t: _block_diag_take(t, SSM_GROUP, SSM_STATE).transpose(0, 1, 3, 2).reshape(
        SSM_GROUPS * SSM_STATE, SSM_GROUP)
    d_bb_re = from_bd(d_bd[:, :, :SSM_X_BLK])
    d_bb_im = from_bd(d_bd[:, :, SSM_X_BLK:])
    d_cdt = d_cd.transpose(0, 2, 1)
    shape_c = (1, SSM_GROUPS, SSM_GROUP, SSM_STATE)
    sgr['ssm_c_re'] = _block_diag_take(d_cdt[:, :, :SSM_X_BLK], SSM_GROUP, SSM_STATE).reshape(shape_c)
    sgr['ssm_c_im'] = -_block_diag_take(d_cdt[:, :, SSM_X_BLK:], SSM_GROUP, SSM_STATE).reshape(shape_c)
    d_a_re, d_a_im, d_ldt, d_b_re, d_b_im = _ssm_disc_bwd(
        a_re, a_im, log_dt, b_re, b_im, d_ab[:, 0, :].reshape(-1, 1), d_ab[:, 1, :].reshape(-1, 1),
        d_bb_re, d_bb_im, name="ssm_disc_bwd")
    sgr['ssm_a_re'] = d_a_re.reshape(1, SSM_GROUPS, SSM_STATE)
    sgr['ssm_a_im'] = d_a_im.reshape(1, SSM_GROUPS, SSM_STATE)
    sgr['ssm_log_dt'] = d_ldt.reshape(SSM_GROUPS, SSM_STATE).sum(axis=1).reshape(1, SSM_GROUPS)
    sgr['ssm_b_re'] = d_b_re.reshape(1, SSM_GROUPS, SSM_STATE, SSM_GROUP)
    sgr['ssm_b_im'] = d_b_im.reshape(1, SSM_GROUPS, SSM_STATE, SSM_GROUP)
    return loss, grad_x, grads, sgr


def _place():
    return lax.axis_index("x"), lax.axis_index("y"), lax.axis_index("c")


def _other_chips(x, y):
    return [(1 - x, y), (x, 1 - y), (1 - x, 1 - y)]


def _gather_weights(shards, *, name):
    n = len(shards)

    def body(*refs):
        ins, outs = refs[:n], refs[n:2 * n]
        send_sems, recv_sems, local_sems = refs[2 * n:]
        x, y, c = _place()
        blk = 2 * x + y
        copies = []
        for i in range(n):
            r = shards[i].shape[0]
            dst = outs[i].at[pl.ds(blk * r, r), :]
            mine = pltpu.make_async_copy(ins[i], dst, local_sems.at[i])
            mine.start()
            copies.append(mine)
            for k, (px, py) in enumerate(_other_chips(x, y)):
                cp = pltpu.make_async_remote_copy(
                    src_ref=ins[i], dst_ref=dst, send_sem=send_sems.at[3 * i + k],
                    recv_sem=recv_sems.at[3 * i + k], device_id=(px, py, c), device_id_type=MESH)
                cp.start()
                copies.append(cp)
        for cp in copies:
            cp.wait()

    return pl.pallas_call(
        body, in_specs=[ANY] * n, out_specs=[ANY] * n,
        out_shape=[jax.ShapeDtypeStruct((N_CHIPS * s.shape[0], s.shape[1]), s.dtype) for s in shards],
        scratch_shapes=[pltpu.SemaphoreType.DMA((3 * n,)), pltpu.SemaphoreType.DMA((3 * n,)),
                        pltpu.SemaphoreType.DMA((n,))],
        name=name)(*shards)


def _scatter_grads(fulls, *, name):
    n = len(fulls)

    def body(*refs):
        ins, outs = refs[:n], refs[n:2 * n]
        send_sems, recv_sems = refs[2 * n:]
        x, y, c = _place()
        copies = []
        for i in range(n):
            r = fulls[i].shape[0] // N_CHIPS
            for k, (px, py) in enumerate(_other_chips(x, y)):
                cp = pltpu.make_async_remote_copy(
                    src_ref=ins[i].at[pl.ds((2 * px + py) * r, r), :], dst_ref=outs[i].at[k],
                    send_sem=send_sems.at[3 * i + k], recv_sem=recv_sems.at[3 * i + k],
                    device_id=(px, py, c), device_id_type=MESH)
                cp.start()
                copies.append(cp)
        for cp in copies:
            cp.wait()

    return pl.pallas_call(
        body, in_specs=[ANY] * n, out_specs=[ANY] * n,
        out_shape=[jax.ShapeDtypeStruct((3, f.shape[0] // N_CHIPS, f.shape[1]), f.dtype) for f in fulls],
        scratch_shapes=[pltpu.SemaphoreType.DMA((3 * n,)), pltpu.SemaphoreType.DMA((3 * n,))],
        name=name)(*fulls)


def _swap_cores(arrs, *, name):
    n = len(arrs)

    def body(*refs):
        ins, outs = refs[:n], refs[n:2 * n]
        send_sems, recv_sems = refs[2 * n:]
        x, y, c = _place()
        copies = []
        for i in range(n):
            cp = pltpu.make_async_remote_copy(
                src_ref=ins[i], dst_ref=outs[i], send_sem=send_sems.at[i], recv_sem=recv_sems.at[i],
                device_id=(x, y, 1 - c), device_id_type=MESH)
            cp.start()
            copies.append(cp)
        for cp in copies:
            cp.wait()

    return pl.pallas_call(
        body, in_specs=[ANY] * n, out_specs=[ANY] * n,
        out_shape=[jax.ShapeDtypeStruct(a.shape, a.dtype) for a in arrs],
        scratch_shapes=[pltpu.SemaphoreType.DMA((n,)), pltpu.SemaphoreType.DMA((n,))],
        name=name)(*arrs)


def _all_reduce_small(buf, *, name):
    r = buf.shape[0]

    def body(in_ref, out_ref, slots, send_sems, recv_sems):
        x, y, c = _place()
        me = 4 * x + 2 * y + c
        slots[pl.ds(me, 1)] = in_ref[...][None]
        copies = []
        for k in range(N_DEV - 1):
            bx, by, bc = ((k + 1) >> 2) & 1, ((k + 1) >> 1) & 1, (k + 1) & 1
            peer = (1 - x if bx else x, 1 - y if by else y, 1 - c if bc else c)
            cp = pltpu.make_async_remote_copy(
                src_ref=in_ref, dst_ref=slots.at[me], send_sem=send_sems.at[k], recv_sem=recv_sems.at[k],
                device_id=peer, device_id_type=MESH)
            cp.start()
            copies.append(cp)
        for cp in copies:
            cp.wait()
        acc = slots[0]
        for d in range(1, N_DEV):
            acc = acc + slots[d]
        out_ref[...] = acc

    vm = pl.BlockSpec(memory_space=pltpu.VMEM)
    return pl.pallas_call(
        body, in_specs=[vm], out_specs=vm, out_shape=jax.ShapeDtypeStruct((r, 128), F32),
        scratch_shapes=[pltpu.VMEM((N_DEV, r, 128), F32), pltpu.SemaphoreType.DMA((N_DEV - 1,)),
                        pltpu.SemaphoreType.DMA((N_DEV - 1,))],
        name=name)(buf)


def _pack(arrs):
    flat = jnp.concatenate([a.reshape(-1).astype(F32) for a in arrs])
    pad = (-flat.shape[0]) % 1024
    return jnp.pad(flat, (0, pad)).reshape(-1, 128)


def _unpack(buf, shapes):
    flat = buf.reshape(-1)
    out, pos = [], 0
    for s in shapes:
        size = math.prod(s)
        out.append(flat[pos:pos + size].reshape(s))
        pos += size
    return out


TILE_ELEMS = 256 * 1024


def _tile_rows(r, c):
    if r * c <= TILE_ELEMS:
        return r
    for tr in range(TILE_ELEMS // c // 16 * 16, 0, -16):
        if r % tr == 0:
            return tr
    raise ValueError((r, c))


def _sum4(own, recv, *, name):
    r, c = own.shape
    tr = _tile_rows(r, c)

    def body(o_ref, r_ref, out_ref):
        acc = o_ref[...].astype(F32)
        for k in range(3):
            acc = acc + r_ref[k].astype(F32)
        out_ref[...] = acc

    return pl.pallas_call(
        body, grid=(r // tr,),
        in_specs=[pl.BlockSpec((tr, c), lambda i: (i, 0)), pl.BlockSpec((3, tr, c), lambda i: (0, i, 0))],
        out_specs=pl.BlockSpec((tr, c), lambda i: (i, 0)), out_shape=jax.ShapeDtypeStruct((r, c), F32),
        name=name, compiler_params=_cp(("parallel",)))(own, recv)


def _adamw(w, ga, gb, m, v, *, name):
    r, c = w.shape
    tr = _tile_rows(r, c)
    bc1 = 1.0 - ADAM_B1 ** ADAM_STEP
    bc2 = 1.0 - ADAM_B2 ** ADAM_STEP
    two = gb is not None

    def body(*refs):
        w_ref, ga_ref = refs[0], refs[1]
        pos = 2
        g = ga_ref[...]
        if two:
            g = g + refs[pos][...]
            pos += 1
        m_ref, v_ref, g_out, d_out, m_out, v_out = refs[pos:pos + 6]
        mn = ADAM_B1 * m_ref[...] + (1.0 - ADAM_B1) * g
        vn = ADAM_B2 * v_ref[...] + (1.0 - ADAM_B2) * (g * g)
        m_hat = mn / bc1
        v_hat = vn / bc2
        g_out[...] = g
        d_out[...] = -ADAM_LR * (m_hat / (jnp.sqrt(v_hat) + ADAM_EPS) + ADAM_WD * w_ref[...])
        m_out[...] = mn
        v_out[...] = vn

    spec = pl.BlockSpec((tr, c), lambda i: (i, 0))
    args = [w, ga] + ([gb] if two else []) + [m, v]
    shp = jax.ShapeDtypeStruct((r, c), F32)
    return pl.pallas_call(
        body, grid=(r // tr,), in_specs=[spec] * len(args), out_specs=[spec] * 4,
        out_shape=[shp] * 4, name=name, compiler_params=_cp(("parallel",)))(*args)


BIG = ['w_in', 'w_glu', 'w_branch_attn', 'w_branch_ssm', 'w_out', 'w_up', 'w_down']
BIG_KEY = {'w_in': 'w_in_t', 'w_glu': 'w_glu_t', 'w_branch_attn': 'w_ba_t', 'w_branch_ssm': 'w_bs_t',
           'w_out': 'w_out', 'w_up': 'w_up_t', 'w_down': 'w_down'}
COL_SHARDED = {'w_in', 'w_glu', 'w_branch_attn', 'w_branch_ssm', 'w_up'}
SMALL = ['attn_norm_g', 'b_in', 'attn_sinks', 'ssm_a_re', 'ssm_a_im', 'ssm_log_dt', 'ssm_b_re', 'ssm_b_im',
         'ssm_c_re', 'ssm_c_im', 'ssm_d', 'b_glu', 'ffn_norm_g', 'conv_b', 'final_norm_g']
WEIGHTS = ['attn_norm_g', 'w_in', 'b_in', 'attn_sinks', 'ssm_a_re', 'ssm_a_im', 'ssm_log_dt', 'ssm_b_re',
           'ssm_b_im', 'ssm_c_re', 'ssm_c_im', 'ssm_d', 'w_glu', 'b_glu', 'w_branch_attn', 'w_branch_ssm',
           'w_out', 'ffn_norm_g', 'w_up', 'conv_w', 'conv_b', 'w_down', 'final_norm_g']


def _shard_2d(name, t):
    t = t[0]
    return t.T if name in COL_SHARDED else t


def _unshard_2d(name, t):
    return (t.T if name in COL_SHARDED else t)[None]


def kernel(x, attn_norm_g, w_in, b_in, attn_sinks, ssm_a_re, ssm_a_im, ssm_log_dt, ssm_b_re, ssm_b_im, ssm_c_re, ssm_c_im, ssm_d, w_glu, b_glu, w_branch_attn, w_branch_ssm, w_out, ffn_norm_g, w_up, conv_w, conv_b, w_down, final_norm_g, loss_target, m_attn_norm_g, m_w_in, m_b_in, m_attn_sinks, m_ssm_a_re, m_ssm_a_im, m_ssm_log_dt, m_ssm_b_re, m_ssm_b_im, m_ssm_c_re, m_ssm_c_im, m_ssm_d, m_w_glu, m_b_glu, m_w_branch_attn, m_w_branch_ssm, m_w_out, m_ffn_norm_g, m_w_up, m_conv_w, m_conv_b, m_w_down, m_final_norm_g, v_attn_norm_g, v_w_in, v_b_in, v_attn_sinks, v_ssm_a_re, v_ssm_a_im, v_ssm_log_dt, v_ssm_b_re, v_ssm_b_im, v_ssm_c_re, v_ssm_c_im, v_ssm_d, v_w_glu, v_b_glu, v_w_branch_attn, v_w_branch_ssm, v_w_out, v_ffn_norm_g, v_w_up, v_conv_w, v_conv_b, v_w_down, v_final_norm_g):
    args = dict(locals())
    w = {n: args[n] for n in WEIGHTS}
    m = {n: args['m_' + n] for n in WEIGHTS}
    v = {n: args['v_' + n] for n in WEIGHTS}
    xi, yi, ci = _place()
    blk = 2 * xi + yi

    shards = [_shard_2d(n, w[n]).astype(BF16) for n in BIG]
    gathered = _gather_weights(shards, name="gather_weights")
    wts = {BIG_KEY[n]: g for n, g in zip(BIG, gathered)}
    cw_cols = w['conv_w'].shape[2]
    cw_place = lax.dynamic_update_slice(jnp.zeros((3, D_FF), F32), w['conv_w'][0] * (ci == 0).astype(F32),
                                        (0, blk * cw_cols))
    conv_w_full = _unpack(_all_reduce_small(_pack([cw_place]), name="gather_conv_w"), [(3, D_FF)])[0]

    small = {n: w[n] for n in SMALL}
    small['conv_w'] = conv_w_full
    loss_part, grad_x, grads, sgr = _local_step(x[0], loss_target[0], wts, small)

    fulls = [grads[BIG_KEY[n]] for n in BIG]
    recvs = _scatter_grads(fulls, name="scatter_grads")
    halves = []
    for n, full, recv in zip(BIG, fulls, recvs):
        r = full.shape[0] // N_CHIPS
        own = lax.dynamic_slice_in_dim(full, blk * r, r, axis=0)
        halves.append(_sum4(own, recv, name="sum4_" + n))
    others = _swap_cores(halves, name="swap_cores")
    out = {}
    for n, mine, other in zip(BIG, halves, others):
        res = _adamw(_shard_2d(n, w[n]), mine, other, _shard_2d(n, m[n]), _shard_2d(n, v[n]), name="adamw_" + n)
        out[n] = [_unshard_2d(n, t) for t in res]

    names = SMALL + ['conv_w']
    shapes = [w[n].shape for n in SMALL] + [(3, D_FF)]
    packed = _pack([sgr[n] for n in names] + [loss_part])
    summed = _unpack(_all_reduce_small(packed, name="all_reduce_small"), shapes + [(1, 1)])
    loss = summed[-1].reshape(())
    sg = dict(zip(names, summed[:-1]))
    sg['conv_w'] = lax.dynamic_slice_in_dim(sg['conv_w'], blk * cw_cols, cw_cols, axis=1)[None]
    res = _adamw(_pack([w[n] for n in names]), _pack([sg[n] for n in names]), None,
                 _pack([m[n] for n in names]), _pack([v[n] for n in names]), name="adamw_small")
    ushapes = [w[n].shape for n in names]
    unpacked = [_unpack(t, ushapes) for t in res]
    for i, n in enumerate(names):
        out[n] = [u[i] for u in unpacked]

    return (loss, grad_x[None], *[out[n][0] for n in WEIGHTS], *[out[n][1] for n in WEIGHTS],
            *[out[n][2] for n in WEIGHTS], *[out[n][3] for n in WEIGHTS])
```

```python
import functools
import math

import jax
import jax.numpy as jnp
from jax import lax
from jax.experimental import pallas as pl
from jax.experimental.pallas import tpu as pltpu

F32 = jnp.float32
BF16 = jnp.bfloat16

D_MODEL = 2048
N_Q_HEADS = 16
HEAD_DIM = 64
ATTN_WIDTH = 1024
KV_WIDTH = 128
BLOCK = 128
SSM_WIDTH = 512
SSM_GROUPS = 32
SSM_GROUP = 16
SSM_STATE = 64
D_FF = 5632
IN_COLS = 5888
RMS_EPS = 1e-6
NEG_BIG = -1e30
N_CHIPS = 4
N_DEV = 8

COL_K = 8
COL_V = 9
COL_U = 10
COL_GA = 14
COL_GS = 30

SSM_SPLIT = 4
SSM_U_BLK = 128
SSM_X_BLK = 512
SSM_CHUNK = 256

ADAM_LR = 0.001
ADAM_B1 = 0.9
ADAM_B2 = 0.999
ADAM_EPS = 1e-08
ADAM_WD = 0.01
ADAM_STEP = 10

VMEM_LIMIT_BYTES = 56 * 1024 * 1024
INV_SQRT2 = 1.0 / math.sqrt(2.0)
INV_SQRT2PI = 1.0 / math.sqrt(2.0 * math.pi)
MESH = pl.DeviceIdType.MESH
ANY = pl.BlockSpec(memory_space=pl.ANY)


def _cp(sem):
    return pltpu.CompilerParams(dimension_semantics=sem, vmem_limit_bytes=VMEM_LIMIT_BYTES)


def _gelu(x):
    return 0.5 * x * (1.0 + lax.erf(x * INV_SQRT2))


def _gelu_grad(x):
    return 0.5 * (1.0 + lax.erf(x * INV_SQRT2)) + x * jnp.exp(-0.5 * x * x) * INV_SQRT2PI


def _sigmoid(x):
    return 1.0 / (1.0 + jnp.exp(-x))


def _place():
    return lax.axis_index("x"), lax.axis_index("y"), lax.axis_index("c")


def _other_chips(x, y):
    return [(1 - x, y), (x, 1 - y), (1 - x, 1 - y)]


class _GatherPlan:
    def __init__(self, shards):
        self.arrays = list(shards)
        n = len(shards)
        self.out_shape = [jax.ShapeDtypeStruct((N_CHIPS * s.shape[0], s.shape[1]), s.dtype) for s in shards]
        self.scratch = [pltpu.SemaphoreType.DMA((6 * n,)), pltpu.SemaphoreType.DMA((6 * n,)),
                        pltpu.SemaphoreType.DMA((n,))]

    def _copies(self, kind, ins, outs, sems):
        send, recv, local = sems
        n = len(self.arrays)
        x, y, c = _place()
        blk = 2 * x + y
        copies = []
        for i in range(n):
            r = self.arrays[i].shape[0]
            h = r // 2
            if kind == 'mine':
                copies.append(pltpu.make_async_copy(ins[i], outs[i].at[pl.ds(blk * r, r), :], local.at[i]))
                continue
            for k, (px, py) in enumerate(_other_chips(x, y)):
                theirs = (2 * px + py) * r
                if kind in ('ici_out', 'ici_in'):
                    route = dict(send_sem=send.at[3 * i + k], recv_sem=recv.at[3 * i + k],
                                 device_id=(px, py, c), device_id_type=MESH)
                else:
                    route = dict(send_sem=send.at[3 * (n + i) + k], recv_sem=recv.at[3 * (n + i) + k],
                                 device_id=(x, y, 1 - c), device_id_type=MESH)
                if kind == 'ici_out':
                    src, dst = ins[i].at[pl.ds(c * h, h), :], outs[i].at[pl.ds(blk * r + c * h, h), :]
                elif kind == 'd2d_in':
                    src = dst = outs[i].at[pl.ds(theirs + (1 - c) * h, h), :]
                else:
                    src = dst = outs[i].at[pl.ds(theirs + c * h, h), :]
                copies.append(pltpu.make_async_remote_copy(src_ref=src, dst_ref=dst, **route))
        return copies

    def start(self, ins, outs, sems):
        for cp in self._copies('mine', ins, outs, sems) + self._copies('ici_out', ins, outs, sems):
            cp.start()

    def middle(self, ins, outs, sems):
        for arrived, onward in zip(self._copies('ici_in', ins, outs, sems), self._copies('d2d_out', ins, outs, sems)):
            arrived.wait_recv()
            onward.start()

    def finish(self, ins, outs, sems):
        for cp in self._copies('d2d_in', ins, outs, sems):
            cp.wait_recv()
        for cp in self._copies('ici_out', ins, outs, sems) + self._copies('d2d_out', ins, outs, sems):
            cp.wait_send()
        for cp in self._copies('mine', ins, outs, sems):
            cp.wait()


class _ScatterPlan:
    def __init__(self, fulls):
        self.arrays = list(fulls)
        n = len(fulls)
        self.out_shape = [jax.ShapeDtypeStruct((3, f.shape[0] // N_CHIPS, f.shape[1]), f.dtype) for f in fulls]
        self.scratch = [pltpu.SemaphoreType.DMA((3 * n,)), pltpu.SemaphoreType.DMA((3 * n,))]

    def _copies(self, ins, outs, sems):
        send, recv = sems
        x, y, c = _place()
        copies = []
        for i in range(len(self.arrays)):
            r = self.arrays[i].shape[0] // N_CHIPS
            for k, (px, py) in enumerate(_other_chips(x, y)):
                copies.append(pltpu.make_async_remote_copy(
                    src_ref=ins[i].at[pl.ds((2 * px + py) * r, r), :], dst_ref=outs[i].at[k],
                    send_sem=send.at[3 * i + k], recv_sem=recv.at[3 * i + k],
                    device_id=(px, py, c), device_id_type=MESH))
        return copies

    def start(self, ins, outs, sems):
        for cp in self._copies(ins, outs, sems):
            cp.start()

    def middle(self, ins, outs, sems):
        pass

    def finish(self, ins, outs, sems):
        for cp in self._copies(ins, outs, sems):
            cp.wait()


def _hosted_call(body, *, grid, in_specs, out_specs, out_shape, scratch_shapes, sem, name, args, comm=None,
                 aliases=None):
    aliases = aliases or {}
    if comm is None:
        outs = pl.pallas_call(body, grid=grid, in_specs=in_specs, out_specs=out_specs, out_shape=out_shape,
                              scratch_shapes=scratch_shapes, name=name, input_output_aliases=aliases,
                              compiler_params=_cp(sem))(*args)
        return outs, None
    n_in, n_out, n_scr = len(in_specs), len(out_specs), len(scratch_shapes)
    nc, ns = len(comm.arrays), len(comm.scratch)
    total = math.prod(grid)
    mid = min(total - 1, (3 * total) // 4)

    def wrapped(*refs):
        pos = 0
        ins = refs[pos:pos + n_in]; pos += n_in
        cins = refs[pos:pos + nc]; pos += nc
        outs = refs[pos:pos + n_out]; pos += n_out
        couts = refs[pos:pos + nc]; pos += nc
        scr = refs[pos:pos + n_scr]; pos += n_scr
        sems = refs[pos:pos + ns]
        step = 0
        for ax, g in enumerate(grid):
            step = step * g + pl.program_id(ax)

        @pl.when(step == 0)
        def _():
            comm.start(cins, couts, sems)

        body(*ins, *outs, *scr)

        @pl.when(step == mid)
        def _():
            comm.middle(cins, couts, sems)

        @pl.when(step == total - 1)
        def _():
            comm.finish(cins, couts, sems)

    res = pl.pallas_call(
        wrapped, grid=grid, in_specs=list(in_specs) + [ANY] * nc, out_specs=list(out_specs) + [ANY] * nc,
        out_shape=list(out_shape) + list(comm.out_shape), scratch_shapes=list(scratch_shapes) + list(comm.scratch),
        name=name, input_output_aliases=aliases,
        compiler_params=_cp(("arbitrary",) * len(grid)))(*args, *comm.arrays)
    return res[:n_out], res[n_out:]


def _run_plan(comm, *, name):
    nc = len(comm.arrays)

    def body(*refs):
        ins, outs, sems = refs[:nc], refs[nc:2 * nc], refs[2 * nc:]
        comm.start(ins, outs, sems)
        comm.middle(ins, outs, sems)
        comm.finish(ins, outs, sems)

    return pl.pallas_call(body, in_specs=[ANY] * nc, out_specs=[ANY] * nc, out_shape=list(comm.out_shape),
                          scratch_shapes=list(comm.scratch), name=name)(*comm.arrays)


def _matmul(a, b, *, ta=False, tb=False, tm, tn, tk, out_dtype, bias=None, res=None, inner='n', comm=None,
            name):
    if ta:
        kdim, m = a.shape
    else:
        m, kdim = a.shape
    if tb:
        n, k2 = b.shape
    else:
        k2, n = b.shape
    assert kdim == k2, (a.shape, b.shape)
    tm, tn, tk = min(tm, m), min(tn, n), min(tk, kdim)
    assert m % tm == 0 and n % tn == 0 and kdim % tk == 0, (name, m, n, kdim, tm, tn, tk)
    nk = kdim // tk
    dn = (((0 if ta else 1,), (1 if tb else 0,)), ((), ()))

    def body(*refs):
        a_ref, b_ref = refs[0], refs[1]
        pos = 2
        bias_ref = res_ref = None
        if bias is not None:
            bias_ref = refs[pos]
            pos += 1
        if res is not None:
            res_ref = refs[pos]
            pos += 1
        o_ref = refs[pos]

        def finish(r):
            if bias_ref is not None:
                r = r + bias_ref[...]
            if res_ref is not None:
                r = r + res_ref[...]
            o_ref[...] = r.astype(out_dtype)

        prod = lax.dot_general(a_ref[...].astype(BF16), b_ref[...].astype(BF16), dn, preferred_element_type=F32)
        if nk == 1:
            finish(prod)
            return
        acc_ref = refs[pos + 1]
        k = pl.program_id(2)

        @pl.when(k == 0)
        def _():
            acc_ref[...] = prod

        @pl.when(k > 0)
        def _():
            acc_ref[...] += prod

        @pl.when(k == nk - 1)
        def _():
            finish(acc_ref[...])

    if inner == 'n':
        grid = (m // tm, n // tn, nk)
        mi = lambda g0, g1: g0
        ni = lambda g0, g1: g1
    else:
        grid = (n // tn, m // tm, nk)
        mi = lambda g0, g1: g1
        ni = lambda g0, g1: g0
    a_spec = (pl.BlockSpec((tk, tm), lambda g0, g1, k: (k, mi(g0, g1))) if ta
              else pl.BlockSpec((tm, tk), lambda g0, g1, k: (mi(g0, g1), k)))
    b_spec = (pl.BlockSpec((tn, tk), lambda g0, g1, k: (ni(g0, g1), k)) if tb
              else pl.BlockSpec((tk, tn), lambda g0, g1, k: (k, ni(g0, g1))))
    in_specs = [a_spec, b_spec]
    args = [a, b]
    if bias is not None:
        in_specs.append(pl.BlockSpec((1, tn), lambda g0, g1, k: (0, ni(g0, g1))))
        args.append(bias)
    if res is not None:
        in_specs.append(pl.BlockSpec((tm, tn), lambda g0, g1, k: (mi(g0, g1), ni(g0, g1))))
        args.append(res)
    outs, couts = _hosted_call(
        body, grid=grid, in_specs=in_specs,
        out_specs=[pl.BlockSpec((tm, tn), lambda g0, g1, k: (mi(g0, g1), ni(g0, g1)))],
        out_shape=[jax.ShapeDtypeStruct((m, n), out_dtype)],
        scratch_shapes=[pltpu.VMEM((tm, tn), F32)] if nk > 1 else [],
        sem=("parallel", "parallel", "arbitrary"), name=name, args=args, comm=comm)
    return outs[0] if comm is None else (outs[0], couts)


def _rms_fwd(x, g, *, name):
    l, d = x.shape
    tr = min(256, l)

    def body(x_ref, g_ref, h_ref):
        xf = x_ref[...]
        r = lax.rsqrt(jnp.mean(xf * xf, axis=-1, keepdims=True) + RMS_EPS)
        h_ref[...] = ((xf * r) * g_ref[...]).astype(BF16)

    row = pl.BlockSpec((tr, d), lambda i: (i, 0))
    return pl.pallas_call(
        body, grid=(l // tr,), in_specs=[row, pl.BlockSpec((1, d), lambda i: (0, 0))],
        out_specs=row, out_shape=jax.ShapeDtypeStruct((l, d), BF16), name=name,
        compiler_params=_cp(("parallel",)))(x, g)


def _rms_bwd(dy, x, g, dres, *, with_bf16, name):
    l, d = x.shape
    tr = min(256, l)

    def body(dy_ref, x_ref, g_ref, dres_ref, *outs):
        dx_ref = outs[0]
        dg_ref = outs[-1]
        xf = x_ref[...]
        r = lax.rsqrt(jnp.mean(xf * xf, axis=-1, keepdims=True) + RMS_EPS)
        xhat = xf * r
        dyv = dy_ref[...]
        dxh = dyv * g_ref[...]
        dx = r * (dxh - xhat * jnp.mean(dxh * xhat, axis=-1, keepdims=True)) + dres_ref[...]
        dx_ref[...] = dx
        if with_bf16:
            outs[1][...] = dx.astype(BF16)

        @pl.when(pl.program_id(0) == 0)
        def _():
            dg_ref[...] = jnp.zeros_like(dg_ref)

        dg_ref[...] += jnp.sum(dyv * xhat, axis=0, keepdims=True)

    row = pl.BlockSpec((tr, d), lambda i: (i, 0))
    vec = pl.BlockSpec((1, d), lambda i: (0, 0))
    out_specs = [row] + ([row] if with_bf16 else []) + [vec]
    out_shape = ([jax.ShapeDtypeStruct((l, d), F32)]
                 + ([jax.ShapeDtypeStruct((l, d), BF16)] if with_bf16 else [])
                 + [jax.ShapeDtypeStruct((1, d), F32)])
    return pl.pallas_call(
        body, grid=(l // tr,), in_specs=[row, row, vec, row], out_specs=out_specs,
        out_shape=out_shape, name=name, compiler_params=_cp(("arbitrary",)))(dy, x, g, dres)


def _final_loss(x3, g, target, *, name):
    l, d = x3.shape
    tr = min(256, l)

    def body(x_ref, g_ref, t_ref, dx_ref, dxb_ref, dg_ref, loss_ref):
        xf = x_ref[...]
        gv = g_ref[...]
        r = lax.rsqrt(jnp.mean(xf * xf, axis=-1, keepdims=True) + RMS_EPS)
        xhat = xf * r
        diff = xhat * gv - t_ref[...]
        dout = diff * (1.0 / d)
        dxh = dout * gv
        dx = r * (dxh - xhat * jnp.mean(dxh * xhat, axis=-1, keepdims=True))
        dx_ref[...] = dx
        dxb_ref[...] = dx.astype(BF16)

        @pl.when(pl.program_id(0) == 0)
        def _():
            dg_ref[...] = jnp.zeros_like(dg_ref)
            loss_ref[...] = jnp.zeros_like(loss_ref)

        dg_ref[...] += jnp.sum(dout * xhat, axis=0, keepdims=True)
        part = jnp.sum(jnp.mean(diff * diff, axis=-1, keepdims=True), axis=0, keepdims=True)
        loss_ref[...] += 0.5 * part

    row = pl.BlockSpec((tr, d), lambda i: (i, 0))
    vec = pl.BlockSpec((1, d), lambda i: (0, 0))
    return pl.pallas_call(
        body, grid=(l // tr,), in_specs=[row, vec, row],
        out_specs=[row, row, vec, pl.BlockSpec((1, 1), lambda i: (0, 0))],
        out_shape=[jax.ShapeDtypeStruct((l, d), F32), jax.ShapeDtypeStruct((l, d), BF16),
                   jax.ShapeDtypeStruct((1, d), F32), jax.ShapeDtypeStruct((1, 1), F32)],
        name=name, compiler_params=_cp(("arbitrary",)))(x3, g, target)


def _attn_masks(n):
    q_idx = lax.broadcasted_iota(jnp.int32, (BLOCK, 2 * BLOCK), 0)
    s_idx = lax.broadcasted_iota(jnp.int32, (BLOCK, 2 * BLOCK), 1)
    dist = q_idx + BLOCK - s_idx
    valid = (dist >= 0) & (dist < BLOCK) & ((n > 0) | (s_idx >= BLOCK))
    return dist.astype(F32), valid


def _dup_half(t, kv_head, lo):
    rolled = pltpu.roll(t, HEAD_DIM, axis=1)
    return jnp.where(lo, t, rolled) if kv_head == 0 else jnp.where(lo, rolled, t)


def _head_probs(qm, kdup, dist, valid, sink, head):
    slope = 2.0 ** (-8.0 * (head + 1) / N_Q_HEADS)
    s = lax.dot_general(qm, kdup, (((1,), (1,)), ((), ())), preferred_element_type=F32)
    s = s * (HEAD_DIM ** -0.5) - slope * dist
    s = jnp.where(valid, s, NEG_BIG)
    m = jnp.maximum(jnp.max(s, axis=-1, keepdims=True), sink)
    p = jnp.exp(s - m)
    esink = jnp.exp(sink - m)
    inv = 1.0 / (jnp.sum(p, axis=-1, keepdims=True) + esink)
    return p * inv, esink * inv


def _attn_fwd(proj, sinks, *, name):
    l = proj.shape[0]
    nb = l // BLOCK

    def body(sink_ref, q_ref, kc_ref, kp_ref, vc_ref, vp_ref, o_ref):
        n = pl.program_id(0)
        dist, valid = _attn_masks(n)
        lo = lax.broadcasted_iota(jnp.int32, (1, BLOCK), 1) < HEAD_DIM
        kx = jnp.concatenate([kp_ref[...], kc_ref[...]], axis=0).astype(BF16)
        vx = jnp.concatenate([vp_ref[...], vc_ref[...]], axis=0).astype(BF16)
        for kv_head in range(2):
            kdup = _dup_half(kx, kv_head, lo)
            vdup = _dup_half(vx, kv_head, lo)
            for pr in range(4):
                pair = kv_head * 4 + pr
                qp = q_ref[:, pair * 128:(pair + 1) * 128].astype(BF16)
                o_pair = jnp.zeros((BLOCK, 128), F32)
                for half in range(2):
                    head = 2 * pair + half
                    sel = lo if half == 0 else jnp.logical_not(lo)
                    qm = jnp.where(sel, qp, jnp.zeros_like(qp))
                    p, _ = _head_probs(qm, kdup, dist, valid, sink_ref[head], head)
                    o = jnp.dot(p.astype(BF16), vdup, preferred_element_type=F32)
                    o_pair = o_pair + jnp.where(sel, o, 0.0)
                o_ref[:, pair * 128:(pair + 1) * 128] = o_pair.astype(BF16)

    kv = lambda col, prev: pl.BlockSpec(
        (BLOCK, KV_WIDTH), (lambda n: (jnp.maximum(n - 1, 0), col)) if prev else (lambda n: (n, col)))
    return pl.pallas_call(
        body, grid=(nb,),
        in_specs=[pl.BlockSpec(memory_space=pltpu.SMEM),
                  pl.BlockSpec((BLOCK, ATTN_WIDTH), lambda n: (n, 0)),
                  kv(COL_K, False), kv(COL_K, True), kv(COL_V, False), kv(COL_V, True)],
        out_specs=pl.BlockSpec((BLOCK, ATTN_WIDTH), lambda n: (n, 0)),
        out_shape=jax.ShapeDtypeStruct((l, ATTN_WIDTH), BF16), name=name,
        compiler_params=_cp(("parallel",)))(sinks, proj, proj, proj, proj, proj)


def _attn_bwd(proj, sinks, dattn, *, comm=None, name):
    l = proj.shape[0]
    nb = l // BLOCK

    def body(sink_ref, q_ref, kc_ref, kp_ref, vc_ref, vp_ref, do_ref,
             dq_ref, dkc_ref, dkp_ref, dvc_ref, dvp_ref, dsink_ref):
        n = pl.program_id(0)
        dist, valid = _attn_masks(n)
        lane = lax.broadcasted_iota(jnp.int32, (1, BLOCK), 1)
        lo = lane < HEAD_DIM
        kx = jnp.concatenate([kp_ref[...], kc_ref[...]], axis=0).astype(BF16)
        vx = jnp.concatenate([vp_ref[...], vc_ref[...]], axis=0).astype(BF16)
        dsink = jnp.zeros((1, BLOCK), F32)
        dk_heads, dv_heads = [], []
        for kv_head in range(2):
            kdup = _dup_half(kx, kv_head, lo)
            vdup = _dup_half(vx, kv_head, lo)
            dk_acc = jnp.zeros((2 * BLOCK, 128), F32)
            dv_acc = jnp.zeros((2 * BLOCK, 128), F32)
            for pr in range(4):
                pair = kv_head * 4 + pr
                qp = q_ref[:, pair * 128:(pair + 1) * 128].astype(BF16)
                dop = do_ref[:, pair * 128:(pair + 1) * 128]
                dq_pair = jnp.zeros((BLOCK, 128), F32)
                for half in range(2):
                    head = 2 * pair + half
                    sel = lo if half == 0 else jnp.logical_not(lo)
                    qm = jnp.where(sel, qp, jnp.zeros_like(qp))
                    dom = jnp.where(sel, dop, jnp.zeros_like(dop))
                    p, psink = _head_probs(qm, kdup, dist, valid, sink_ref[head], head)
                    dp = lax.dot_general(dom, vdup, (((1,), (1,)), ((), ())), preferred_element_type=F32)
                    delta = jnp.sum(p * dp, axis=-1, keepdims=True)
                    ds = (p * (dp - delta) * (HEAD_DIM ** -0.5)).astype(BF16)
                    dsink = dsink + jnp.where(lane == head, jnp.sum(-psink * delta), 0.0)
                    dq = jnp.dot(ds, kdup, preferred_element_type=F32)
                    dq_pair = dq_pair + jnp.where(sel, dq, 0.0)
                    dk_acc = dk_acc + lax.dot_general(ds, qm, (((0,), (0,)), ((), ())),
                                                      preferred_element_type=F32)
                    dv_acc = dv_acc + lax.dot_general(p.astype(BF16), dom, (((0,), (0,)), ((), ())),
                                                      preferred_element_type=F32)
                dq_ref[:, pair * 128:(pair + 1) * 128] = dq_pair.astype(BF16)
            dk_heads.append(dk_acc + pltpu.roll(dk_acc, HEAD_DIM, axis=1))
            dv_heads.append(dv_acc + pltpu.roll(dv_acc, HEAD_DIM, axis=1))
        dk = jnp.where(lo, dk_heads[0], dk_heads[1])
        dv = jnp.where(lo, dv_heads[0], dv_heads[1])
        dkp_ref[...] = dk[:BLOCK]
        dkc_ref[...] = dk[BLOCK:]
        dvp_ref[...] = dv[:BLOCK]
        dvc_ref[...] = dv[BLOCK:]

        @pl.when(n == 0)
        def _():
            dsink_ref[...] = jnp.zeros_like(dsink_ref)

        dsink_ref[...] += dsink

    kv = lambda col, prev: pl.BlockSpec(
        (BLOCK, KV_WIDTH), (lambda n: (jnp.maximum(n - 1, 0), col)) if prev else (lambda n: (n, col)))
    qspec = pl.BlockSpec((BLOCK, ATTN_WIDTH), lambda n: (n, 0))
    kvout = pl.BlockSpec((BLOCK, KV_WIDTH), lambda n: (n, 0))
    kvshape = jax.ShapeDtypeStruct((l, KV_WIDTH), F32)
    return _hosted_call(
        body, grid=(nb,),
        in_specs=[pl.BlockSpec(memory_space=pltpu.SMEM), qspec,
                  kv(COL_K, False), kv(COL_K, True), kv(COL_V, False), kv(COL_V, True), qspec],
        out_specs=[qspec, kvout, kvout, kvout, kvout, pl.BlockSpec((1, BLOCK), lambda n: (0, 0))],
        out_shape=[jax.ShapeDtypeStruct((l, ATTN_WIDTH), BF16), kvshape, kvshape, kvshape, kvshape,
                   jax.ShapeDtypeStruct((1, BLOCK), F32)],
        scratch_shapes=[], sem=("arbitrary",), name=name,
        args=(sinks, proj, proj, proj, proj, proj, dattn), comm=comm)


def _kv_grad_merge(dkc, dkp, dvc, dvp, *, name):
    l = dkc.shape[0]
    nb = l // BLOCK

    def body(dkc_ref, dkp_ref, dvc_ref, dvp_ref, o_ref):
        last = pl.program_id(0) == nb - 1
        o_ref[:, :KV_WIDTH] = (dkc_ref[...] + jnp.where(last, 0.0, dkp_ref[...])).astype(BF16)
        o_ref[:, KV_WIDTH:] = (dvc_ref[...] + jnp.where(last, 0.0, dvp_ref[...])).astype(BF16)

    cur = pl.BlockSpec((BLOCK, KV_WIDTH), lambda n: (n, 0))
    nxt = pl.BlockSpec((BLOCK, KV_WIDTH), lambda n: (jnp.minimum(n + 1, nb - 1), 0))
    return pl.pallas_call(
        body, grid=(nb,), in_specs=[cur, nxt, cur, nxt],
        out_specs=pl.BlockSpec((BLOCK, 2 * KV_WIDTH), lambda n: (n, 0)),
        out_shape=jax.ShapeDtypeStruct((l, 2 * KV_WIDTH), BF16), name=name,
        compiler_params=_cp(("parallel",)))(dkc, dkp, dvc, dvp)


def _discretize(a_re, a_im, log_dt, b_re, b_im):
    dt = jnp.exp(log_dt)
    mag = jnp.exp(a_re * dt)
    ab_re = mag * jnp.cos(a_im * dt)
    ab_im = mag * jnp.sin(a_im * dt)
    nr = ab_re - 1.0
    ni = ab_im
    den = a_re * a_re + a_im * a_im
    z_re = (nr * a_re + ni * a_im) / den
    z_im = (ni * a_re - nr * a_im) / den
    bb_re = z_re * b_re - z_im * b_im
    bb_im = z_re * b_im + z_im * b_re
    return ab_re, ab_im, bb_re, bb_im


def _ssm_disc_fwd(a_re, a_im, log_dt, b_re, b_im, *, name):
    def body(ar, ai, ld, br, bi, o_ar, o_ai, o_br, o_bi):
        r = _discretize(ar[...], ai[...], ld[...], br[...], bi[...])
        o_ar[...], o_ai[...], o_br[...], o_bi[...] = r

    col = jax.ShapeDtypeStruct(a_re.shape, F32)
    mat = jax.ShapeDtypeStruct(b_re.shape, F32)
    return pl.pallas_call(body, out_shape=[col, col, mat, mat], name=name)(a_re, a_im, log_dt, b_re, b_im)


def _ssm_disc_bwd(a_re, a_im, log_dt, b_re, b_im, d_ab_re, d_ab_im, d_bb_re, d_bb_im, *, name):
    def body(ar, ai, ld, br, bi, g0, g1, g2, g3, o_ar, o_ai, o_ld, o_br, o_bi):
        _, vjp = jax.vjp(_discretize, ar[...], ai[...], ld[...], br[...], bi[...])
        r = vjp((g0[...], g1[...], g2[...], g3[...]))
        o_ar[...], o_ai[...], o_ld[...], o_br[...], o_bi[...] = r

    col = jax.ShapeDtypeStruct(a_re.shape, F32)
    mat = jax.ShapeDtypeStruct(b_re.shape, F32)
    return pl.pallas_call(body, out_shape=[col, col, col, mat, mat], name=name)(
        a_re, a_im, log_dt, b_re, b_im, d_ab_re, d_ab_im, d_bb_re, d_bb_im)


def _shift_rows(x, d, rows, *, down):
    t = x.shape[0]
    if down:
        return jnp.where(rows >= d, pltpu.roll(x, d, axis=0), 0.0)
    return jnp.where(rows < t - d, pltpu.roll(x, t - d, axis=0), 0.0)


def _scan_chunk(xr, xi, ar, ai, *, down):
    t = xr.shape[0]
    rows = lax.broadcasted_iota(jnp.int32, (t, 1), 0)
    pr, pi = ar, ai
    d = 1
    while d < t:
        sr = _shift_rows(xr, d, rows, down=down)
        si = _shift_rows(xi, d, rows, down=down)
        xr, xi = xr + pr * sr - pi * si, xi + pr * si + pi * sr
        pr, pi = pr * pr - pi * pi, 2.0 * pr * pi
        d *= 2
    return xr, xi


def _ssm_fwd(proj, ab, bd, cd, dskip, *, comm=None, name):
    l = proj.shape[0]
    t = min(SSM_CHUNK, l)
    nc = l // t

    def body(u_ref, ab_ref, bd_ref, cd_ref, ds_ref, y_ref, gy_ref, xs_ref, carry_ref):
        c = pl.program_id(1)

        @pl.when(c == 0)
        def _():
            carry_ref[...] = jnp.zeros_like(carry_ref)

        u = u_ref[...]
        ar, ai = ab_ref[0, 0:1, :], ab_ref[0, 1:2, :]
        bu = jnp.dot(u.astype(BF16), bd_ref[0], preferred_element_type=F32)
        rows = lax.broadcasted_iota(jnp.int32, (t, 1), 0)
        cr, ci = carry_ref[0:1, :], carry_ref[1:2, :]
        xr = bu[:, :SSM_X_BLK] + jnp.where(rows == 0, ar * cr - ai * ci, 0.0)
        xi = bu[:, SSM_X_BLK:] + jnp.where(rows == 0, ar * ci + ai * cr, 0.0)
        xr, xi = _scan_chunk(xr, xi, ar, ai, down=True)
        xs_ref[0, :, :SSM_X_BLK] = xr
        xs_ref[0, :, SSM_X_BLK:] = xi
        carry_ref[0:1, :] = xs_ref[0, t - 1:t, :SSM_X_BLK]
        carry_ref[1:2, :] = xs_ref[0, t - 1:t, SSM_X_BLK:]
        y = jnp.dot(xs_ref[0].astype(BF16), cd_ref[0], preferred_element_type=F32) + ds_ref[...] * u
        y_ref[...] = y
        gy_ref[...] = _gelu(y).astype(BF16)

    blk = lambda shape: pl.BlockSpec((1,) + shape, lambda j, c: (j, 0, 0))
    ycol = pl.BlockSpec((t, SSM_U_BLK), lambda j, c: (c, j))
    return _hosted_call(
        body, grid=(SSM_SPLIT, nc),
        in_specs=[pl.BlockSpec((t, SSM_U_BLK), lambda j, c: (c, COL_U + j)),
                  blk((2, SSM_X_BLK)), blk((SSM_U_BLK, 2 * SSM_X_BLK)), blk((2 * SSM_X_BLK, SSM_U_BLK)),
                  pl.BlockSpec((1, SSM_U_BLK), lambda j, c: (0, j))],
        out_specs=[ycol, ycol, pl.BlockSpec((1, t, 2 * SSM_X_BLK), lambda j, c: (j, c, 0))],
        out_shape=[jax.ShapeDtypeStruct((l, SSM_WIDTH), F32), jax.ShapeDtypeStruct((l, SSM_WIDTH), BF16),
                   jax.ShapeDtypeStruct((SSM_SPLIT, l, 2 * SSM_X_BLK), F32)],
        scratch_shapes=[pltpu.VMEM((2, SSM_X_BLK), F32)], sem=("parallel", "arbitrary"), name=name,
        args=(proj, ab, bd, cd, dskip), comm=comm)


def _ssm_bwd(proj, y, dgy, xs, ab, bdt, cdt, dskip, *, name):
    l = proj.shape[0]
    t = min(SSM_CHUNK, l)
    nc = l // t

    def body(u_ref, y_ref, dgy_ref, xs_ref, halo_ref, ab_ref, bdt_ref, cdt_ref, ds_ref,
             du_ref, dbd_ref, dcd_ref, dab_ref, dd_ref, carry_ref):
        c = pl.program_id(1)
        ci_ = nc - 1 - c

        @pl.when(c == 0)
        def _():
            carry_ref[...] = jnp.zeros_like(carry_ref)
            dbd_ref[...] = jnp.zeros_like(dbd_ref)
            dcd_ref[...] = jnp.zeros_like(dcd_ref)
            dab_ref[...] = jnp.zeros_like(dab_ref)
            dd_ref[...] = jnp.zeros_like(dd_ref)

        u = u_ref[...]
        dy = dgy_ref[...] * _gelu_grad(y_ref[...])
        dyb = dy.astype(BF16)
        ar, ai = ab_ref[0, 0:1, :], ab_ref[0, 1:2, :]
        g = jnp.dot(dyb, cdt_ref[0], preferred_element_type=F32)
        rows = lax.broadcasted_iota(jnp.int32, (t, 1), 0)
        cr, ci = carry_ref[0:1, :], carry_ref[1:2, :]
        lr = g[:, :SSM_X_BLK] + jnp.where(rows == t - 1, ar * cr + ai * ci, 0.0)
        li = g[:, SSM_X_BLK:] + jnp.where(rows == t - 1, ar * ci - ai * cr, 0.0)
        lr, li = _scan_chunk(lr, li, ar, -ai, down=False)
        lam = jnp.concatenate([lr, li], axis=1)
        carry_ref[0:1, :] = lr[0:1, :]
        carry_ref[1:2, :] = li[0:1, :]
        lamb = lam.astype(BF16)
        du_ref[...] = (jnp.dot(lamb, bdt_ref[0], preferred_element_type=F32) + ds_ref[...] * dy).astype(BF16)
        dbd_ref[0] += lax.dot_general(u.astype(BF16), lamb, (((0,), (0,)), ((), ())),
                                      preferred_element_type=F32)
        xs = xs_ref[0]
        dcd_ref[0] += lax.dot_general(xs.astype(BF16), dyb, (((0,), (0,)), ((), ())),
                                      preferred_element_type=F32)
        halo = jnp.where(ci_ > 0, halo_ref[0, 7:8, :], 0.0)
        xprev = jnp.where(rows == 0, halo, pltpu.roll(xs, 1, axis=0))
        xpr, xpi = xprev[:, :SSM_X_BLK], xprev[:, SSM_X_BLK:]
        dab_ref[0, 0:1, :] += jnp.sum(lr * xpr + li * xpi, axis=0, keepdims=True)
        dab_ref[0, 1:2, :] += jnp.sum(li * xpr - lr * xpi, axis=0, keepdims=True)
        dd_ref[...] += jnp.sum(dy * u, axis=0, keepdims=True)

    blk = lambda shape: pl.BlockSpec((1,) + shape, lambda j, c: (j, 0, 0))
    rev = lambda j, c: (nc - 1 - c, j)
    ycol = pl.BlockSpec((t, SSM_U_BLK), rev)
    hb = t // 8
    return pl.pallas_call(
        body, grid=(SSM_SPLIT, nc),
        in_specs=[pl.BlockSpec((t, SSM_U_BLK), lambda j, c: (nc - 1 - c, COL_U + j)), ycol, ycol,
                  pl.BlockSpec((1, t, 2 * SSM_X_BLK), lambda j, c: (j, nc - 1 - c, 0)),
                  pl.BlockSpec((1, 8, 2 * SSM_X_BLK),
                               lambda j, c: (j, jnp.maximum((nc - 1 - c) * hb - 1, 0), 0)),
                  blk((2, SSM_X_BLK)), blk((2 * SSM_X_BLK, SSM_U_BLK)), blk((SSM_U_BLK, 2 * SSM_X_BLK)),
                  pl.BlockSpec((1, SSM_U_BLK), lambda j, c: (0, j))],
        out_specs=[ycol, blk((SSM_U_BLK, 2 * SSM_X_BLK)), blk((2 * SSM_X_BLK, SSM_U_BLK)),
                   blk((2, SSM_X_BLK)), pl.BlockSpec((1, SSM_U_BLK), lambda j, c: (0, j))],
        out_shape=[jax.ShapeDtypeStruct((l, SSM_WIDTH), BF16),
                   jax.ShapeDtypeStruct((SSM_SPLIT, SSM_U_BLK, 2 * SSM_X_BLK), F32),
                   jax.ShapeDtypeStruct((SSM_SPLIT, 2 * SSM_X_BLK, SSM_U_BLK), F32),
                   jax.ShapeDtypeStruct((SSM_SPLIT, 2, SSM_X_BLK), F32),
                   jax.ShapeDtypeStruct((1, SSM_WIDTH), F32)],
        scratch_shapes=[pltpu.VMEM((2, SSM_X_BLK), F32)], name=name,
        compiler_params=_cp(("parallel", "arbitrary")))(proj, y, dgy, xs, xs, ab, bdt, cdt, dskip)


def _block_diag(t):
    s, g, a, b = t.shape
    return jnp.einsum('sgab,gk->sgakb', t, jnp.eye(g, dtype=t.dtype)).reshape(s, g * a, g * b)


def _block_diag_take(t, a, b):
    s = t.shape[0]
    return jnp.einsum('sgakb,gk->sgab', t.reshape(s, 8, a, 8, b), jnp.eye(8, dtype=t.dtype))


def _glu_fwd(z, *, name):
    l = z.shape[0]
    tr = min(512, l)

    def body(zv_ref, zg_ref, o_ref):
        o_ref[...] = (zv_ref[...] * _sigmoid(zg_ref[...])).astype(BF16)

    return pl.pallas_call(
        body, grid=(l // tr,),
        in_specs=[pl.BlockSpec((tr, SSM_WIDTH), lambda i: (i, 0)), pl.BlockSpec((tr, SSM_WIDTH), lambda i: (i, 1))],
        out_specs=pl.BlockSpec((tr, SSM_WIDTH), lambda i: (i, 0)),
        out_shape=jax.ShapeDtypeStruct((l, SSM_WIDTH), BF16), name=name,
        compiler_params=_cp(("parallel",)))(z, z)


def _glu_bwd(dssm, z, *, name):
    l = z.shape[0]
    tr = min(512, l)

    def body(d_ref, zv_ref, zg_ref, dz_ref, db_ref):
        d = d_ref[...]
        sg = _sigmoid(zg_ref[...])
        dv = d * sg
        dg = d * zv_ref[...] * sg * (1.0 - sg)
        dz_ref[:, :SSM_WIDTH] = dv.astype(BF16)
        dz_ref[:, SSM_WIDTH:] = dg.astype(BF16)

        @pl.when(pl.program_id(0) == 0)
        def _():
            db_ref[...] = jnp.zeros_like(db_ref)

        db_ref[:, :SSM_WIDTH] += jnp.sum(dv, axis=0, keepdims=True)
        db_ref[:, SSM_WIDTH:] += jnp.sum(dg, axis=0, keepdims=True)

    half = lambda k: pl.BlockSpec((tr, SSM_WIDTH), lambda i: (i, k))
    return pl.pallas_call(
        body, grid=(l // tr,), in_specs=[half(0), half(0), half(1)],
        out_specs=[pl.BlockSpec((tr, 2 * SSM_WIDTH), lambda i: (i, 0)),
                   pl.BlockSpec((1, 2 * SSM_WIDTH), lambda i: (0, 0))],
        out_shape=[jax.ShapeDtypeStruct((l, 2 * SSM_WIDTH), BF16), jax.ShapeDtypeStruct((1, 2 * SSM_WIDTH), F32)],
        name=name, compiler_params=_cp(("arbitrary",)))(dssm, z, z)


GATE_TC = 256
GATE_NJ = D_MODEL // GATE_TC


def _merge_fwd(proj, a, s, *, name):
    l = a.shape[0]
    tr = min(2048, l)

    def body(ga_ref, gs_ref, a_ref, s_ref, o_ref):
        o_ref[...] = (_sigmoid(ga_ref[...]) * a_ref[...] + _sigmoid(gs_ref[...]) * s_ref[...]).astype(BF16)

    own = pl.BlockSpec((tr, GATE_TC), lambda i, j: (i, j))
    return pl.pallas_call(
        body, grid=(l // tr, GATE_NJ),
        in_specs=[pl.BlockSpec((tr, GATE_TC), lambda i, j: (i, COL_GA // 2 + j)),
                  pl.BlockSpec((tr, GATE_TC), lambda i, j: (i, COL_GS // 2 + j)), own, own],
        out_specs=own, out_shape=jax.ShapeDtypeStruct((l, D_MODEL), BF16), name=name,
        compiler_params=_cp(("parallel", "parallel")))(proj, proj, a, s)


def _merge_bwd(proj, a, s, dm, *, name):
    l = a.shape[0]
    tr = min(2048, l)

    def body(ga_ref, gs_ref, a_ref, s_ref, dm_ref, da_ref, ds_ref, dga_ref, dgs_ref):
        d = dm_ref[...]
        sa = _sigmoid(ga_ref[...])
        ss = _sigmoid(gs_ref[...])
        da_ref[...] = (d * sa).astype(BF16)
        ds_ref[...] = (d * ss).astype(BF16)
        dga_ref[...] = (d * a_ref[...] * sa * (1.0 - sa)).astype(BF16)
        dgs_ref[...] = (d * s_ref[...] * ss * (1.0 - ss)).astype(BF16)

    own = pl.BlockSpec((tr, GATE_TC), lambda i, j: (i, j))
    wide = jax.ShapeDtypeStruct((l, D_MODEL), BF16)
    return pl.pallas_call(
        body, grid=(l // tr, GATE_NJ),
        in_specs=[pl.BlockSpec((tr, GATE_TC), lambda i, j: (i, COL_GA // 2 + j)),
                  pl.BlockSpec((tr, GATE_TC), lambda i, j: (i, COL_GS // 2 + j)), own, own, own],
        out_specs=[own, own, own, own], out_shape=[wide, wide, wide, wide], name=name,
        compiler_params=_cp(("parallel", "parallel")))(proj, proj, a, s, dm)


FF_TC = 1408
FF_NJ = D_FF // FF_TC


def _conv_gate(g_ref, halo_ref, w_ref, b_ref, first):
    tr = g_ref.shape[0]
    gate = g_ref[...]
    halo = jnp.where(first, 0.0, halo_ref[...])
    ext = jnp.concatenate([halo, gate], axis=0)
    gm1 = pltpu.roll(ext, 1, axis=0)[8:8 + tr]
    gm2 = pltpu.roll(ext, 2, axis=0)[8:8 + tr]
    gc = b_ref[...] + w_ref[0:1, :] * gm2 + w_ref[1:2, :] * gm1 + w_ref[2:3, :] * gate
    return gc, gate, gm1, gm2


def _ff_specs(tr):
    hb = tr // 8
    val = pl.BlockSpec((tr, FF_TC), lambda j, i: (i, j))
    gate = pl.BlockSpec((tr, FF_TC), lambda j, i: (i, FF_NJ + j))
    halo = pl.BlockSpec((8, FF_TC), lambda j, i: (jnp.maximum(i * hb - 1, 0), FF_NJ + j))
    w = pl.BlockSpec((3, FF_TC), lambda j, i: (0, j))
    b = pl.BlockSpec((1, FF_TC), lambda j, i: (0, j))
    return val, gate, halo, w, b


def _ffn_act_fwd(up, conv_w, conv_b, *, name):
    l = up.shape[0]
    tr = min(256, l)

    def body(v_ref, g_ref, halo_ref, w_ref, b_ref, o_ref):
        gc, _, _, _ = _conv_gate(g_ref, halo_ref, w_ref, b_ref, pl.program_id(1) == 0)
        o_ref[...] = (v_ref[...] * _gelu(gc)).astype(BF16)

    val, gate, halo, w, b = _ff_specs(tr)
    return pl.pallas_call(
        body, grid=(FF_NJ, l // tr), in_specs=[val, gate, halo, w, b], out_specs=val,
        out_shape=jax.ShapeDtypeStruct((l, D_FF), BF16), name=name,
        compiler_params=_cp(("parallel", "parallel")))(up, up, up, conv_w, conv_b)


def _ffn_act_bwd(dact, up, conv_w, conv_b, *, comm=None, name):
    l = up.shape[0]
    tr = min(256, l)

    def body(d_ref, v_ref, g_ref, halo_ref, w_ref, b_ref, dup_ref, dgc_ref, dw_ref, db_ref):
        gc, gate, gm1, gm2 = _conv_gate(g_ref, halo_ref, w_ref, b_ref, pl.program_id(1) == 0)
        d = d_ref[...]
        dup_ref[...] = (d * _gelu(gc)).astype(BF16)
        dgc = d * v_ref[...] * _gelu_grad(gc)
        dgc_ref[...] = dgc

        @pl.when(pl.program_id(1) == 0)
        def _():
            dw_ref[...] = jnp.zeros_like(dw_ref)
            db_ref[...] = jnp.zeros_like(db_ref)

        dw_ref[0:1, :] += jnp.sum(dgc * gm2, axis=0, keepdims=True)
        dw_ref[1:2, :] += jnp.sum(dgc * gm1, axis=0, keepdims=True)
        dw_ref[2:3, :] += jnp.sum(dgc * gate, axis=0, keepdims=True)
        db_ref[...] += jnp.sum(dgc, axis=0, keepdims=True)

    val, gate, halo, w, b = _ff_specs(tr)
    return _hosted_call(
        body, grid=(FF_NJ, l // tr), in_specs=[val, val, gate, halo, w, b],
        out_specs=[val, val, w, b],
        out_shape=[jax.ShapeDtypeStruct((l, 2 * D_FF), BF16), jax.ShapeDtypeStruct((l, D_FF), F32),
                   jax.ShapeDtypeStruct((3, D_FF), F32), jax.ShapeDtypeStruct((1, D_FF), F32)],
        scratch_shapes=[], sem=("parallel", "arbitrary"), name=name,
        args=(dact, up, up, up, conv_w, conv_b), comm=comm)


def _ffn_conv_bwd(dgc, dup, conv_w, *, name):
    l = dgc.shape[0]
    tr = min(256, l)
    ni = l // tr
    hb = tr // 8

    def body(d_ref, halo_ref, w_ref, dup_in, o_ref):
        del dup_in
        d = d_ref[...]
        halo = jnp.where(pl.program_id(1) == ni - 1, 0.0, halo_ref[...])
        ext = jnp.concatenate([d, halo], axis=0)
        dp1 = pltpu.roll(ext, tr + 7, axis=0)[:tr]
        dp2 = pltpu.roll(ext, tr + 6, axis=0)[:tr]
        o_ref[...] = (w_ref[2:3, :] * d + w_ref[1:2, :] * dp1 + w_ref[0:1, :] * dp2).astype(BF16)

    return pl.pallas_call(
        body, grid=(FF_NJ, ni),
        in_specs=[pl.BlockSpec((tr, FF_TC), lambda j, i: (i, j)),
                  pl.BlockSpec((8, FF_TC), lambda j, i: (jnp.minimum((i + 1) * hb, l // 8 - 1), j)),
                  pl.BlockSpec((3, FF_TC), lambda j, i: (0, j)), ANY],
        out_specs=pl.BlockSpec((tr, FF_TC), lambda j, i: (i, FF_NJ + j)),
        out_shape=jax.ShapeDtypeStruct((l, 2 * D_FF), BF16), input_output_aliases={3: 0}, name=name,
        compiler_params=_cp(("parallel", "parallel")))(dgc, dgc, conv_w, dup)


def _col_sum(a, *, name):
    l, n = a.shape
    tr = min(512, l)

    def body(a_ref, o_ref):
        @pl.when(pl.program_id(0) == 0)
        def _():
            o_ref[...] = jnp.zeros_like(o_ref)

        o_ref[...] += jnp.sum(a_ref[...].astype(F32), axis=0, keepdims=True)

    return pl.pallas_call(
        body, grid=(l // tr,), in_specs=[pl.BlockSpec((tr, n), lambda i: (i, 0))],
        out_specs=pl.BlockSpec((1, n), lambda i: (0, 0)), out_shape=jax.ShapeDtypeStruct((1, n), F32),
        name=name, compiler_params=_cp(("arbitrary",)))(a)


def _local_step(x, target, wts, small, shards=None):
    wts = dict(wts)
    grads, recvs, sgr = {}, {}, {}
    gather = (lambda keys: _GatherPlan([shards[k] for k in keys])) if shards is not None else (lambda keys: None)
    scatter = (lambda keys: _ScatterPlan([grads[k] for k in keys])) if shards is not None else (lambda keys: None)

    mm = _matmul

    def take(res, plan, keys, store):
        outs, couts = res
        if plan is not None:
            store.update(zip(keys, couts))
        return outs

    def mm_host(keys, make_plan, store, *args, **kw):
        plan = make_plan(keys)
        if plan is None:
            return _matmul(*args, **kw)
        return take(_matmul(*args, comm=plan, **kw), plan, keys, store)

    col = lambda t: t.reshape(SSM_GROUPS * SSM_STATE, 1)
    a_re, a_im = col(small['ssm_a_re']), col(small['ssm_a_im'])
    log_dt = jnp.repeat(small['ssm_log_dt'].reshape(SSM_GROUPS), SSM_STATE).reshape(-1, 1)
    b_re = small['ssm_b_re'].reshape(SSM_GROUPS * SSM_STATE, SSM_GROUP)
    b_im = small['ssm_b_im'].reshape(SSM_GROUPS * SSM_STATE, SSM_GROUP)
    ab_re, ab_im, bb_re, bb_im = _ssm_disc_fwd(a_re, a_im, log_dt, b_re, b_im, name="ssm_disc_fwd")
    ab = jnp.stack([ab_re.reshape(SSM_SPLIT, SSM_X_BLK), ab_im.reshape(SSM_SPLIT, SSM_X_BLK)], axis=1)
    to_bd = lambda t: _block_diag(t.reshape(SSM_SPLIT, 8, SSM_STATE, SSM_GROUP).transpose(0, 1, 3, 2))
    bd = jnp.concatenate([to_bd(bb_re), to_bd(bb_im)], axis=2)
    c_re = small['ssm_c_re'].reshape(SSM_SPLIT, 8, SSM_GROUP, SSM_STATE)
    c_im = small['ssm_c_im'].reshape(SSM_SPLIT, 8, SSM_GROUP, SSM_STATE)
    cdt = jnp.concatenate([_block_diag(c_re), -_block_diag(c_im)], axis=2)
    bd_b, cdt_b = bd.astype(BF16), cdt.astype(BF16)
    bdt_b, cd_b = bd_b.transpose(0, 2, 1), cdt_b.transpose(0, 2, 1)
    dskip = small['ssm_d'].reshape(1, SSM_WIDTH)

    sinks = small['attn_sinks'].reshape(N_Q_HEADS)
    h1 = _rms_fwd(x, small['attn_norm_g'], name="rms1_fwd")
    proj = mm_host(['w_glu_t', 'w_ba_t', 'w_bs_t', 'w_out'], gather, wts,
                   h1, wts['w_in_t'], tb=True, tm=512, tn=2944, tk=2048, inner='m', out_dtype=F32,
                   bias=small['b_in'], name="mm_in")
    attn = _attn_fwd(proj, sinks, name="attn_fwd")
    plan = gather(['w_up_t'])
    y, gy, xs = take(_ssm_fwd(proj, ab, bd_b, cd_b, dskip, comm=plan, name="ssm_fwd"), plan, ['w_up_t'], wts)
    z = mm(gy, wts['w_glu_t'], tb=True, tm=1024, tn=1024, tk=512, out_dtype=F32,
           bias=small['b_glu'], name="mm_glu")
    ssm = _glu_fwd(z, name="glu_fwd")
    a_br = mm(attn, wts['w_ba_t'], tb=True, tm=1024, tn=1024, tk=1024, out_dtype=F32, name="mm_ba")
    s_br = mm(ssm, wts['w_bs_t'], tb=True, tm=1024, tn=1024, tk=512, out_dtype=F32, name="mm_bs")
    merged = _merge_fwd(proj, a_br, s_br, name="merge_fwd")
    x2 = mm(merged, wts['w_out'], tm=1024, tn=1024, tk=2048, inner='m', out_dtype=F32, res=x, name="mm_out")
    h2 = _rms_fwd(x2, small['ffn_norm_g'], name="rms2_fwd")
    up = mm_host(['w_down'], gather, wts,
                 h2, wts['w_up_t'], tb=True, tm=1024, tn=1024, tk=2048, out_dtype=F32, name="mm_up")
    conv_w, conv_b = small['conv_w'], small['conv_b']
    act = _ffn_act_fwd(up, conv_w, conv_b, name="ffn_act_fwd")
    x3 = mm(act, wts['w_down'], tm=1024, tn=1024, tk=2816, out_dtype=F32, res=x2, name="mm_down")
    dx3, dx3b, d_g3, loss = _final_loss(x3, small['final_norm_g'].reshape(1, D_MODEL), target, name="final_loss")

    sgr['final_norm_g'] = d_g3.reshape(D_MODEL)
    dact = mm(dx3b, wts['w_down'], tb=True, tm=512, tn=2816, tk=2048, inner='m', out_dtype=F32, name="mm_dact")
    grads['w_down'] = mm(act, dx3b, ta=True, tm=512, tn=1024, tk=2048, out_dtype=BF16, name="mm_dw_down")
    plan = scatter(['w_down'])
    dup, dgc, sgr['conv_w'], sgr['conv_b'] = take(
        _ffn_act_bwd(dact, up, conv_w, conv_b, comm=plan, name="ffn_act_bwd"), plan, ['w_down'], recvs)
    dup = _ffn_conv_bwd(dgc, dup, conv_w, name="ffn_conv_bwd")
    grads['w_up_t'] = mm(dup, h2, ta=True, tm=1024, tn=1024, tk=2048, out_dtype=BF16, name="mm_dw_up")
    dh2 = mm_host(['w_up_t'], scatter, recvs,
                  dup, wts['w_up_t'], tm=1024, tn=1024, tk=2816, out_dtype=F32, name="mm_dh2")
    dx2, dx2b, sgr['ffn_norm_g'] = _rms_bwd(dh2, x2, small['ffn_norm_g'], dx3, with_bf16=True, name="rms2_bwd")

    dm = mm(dx2b, wts['w_out'], tb=True, tm=1024, tn=1024, tk=2048, inner='m', out_dtype=F32, name="mm_dmerged")
    grads['w_out'] = mm(merged, dx2b, ta=True, tm=1024, tn=1024, tk=2048, out_dtype=BF16, name="mm_dw_out")
    d_a, d_s, dga, dgs = _merge_bwd(proj, a_br, s_br, dm, name="merge_bwd")
    dattn = mm(d_a, wts['w_ba_t'], tm=1024, tn=1024, tk=2048, inner='m', out_dtype=BF16, name="mm_dattn")
    grads['w_ba_t'] = mm(d_a, attn, ta=True, tm=1024, tn=1024, tk=2048, out_dtype=BF16, name="mm_dw_ba")
    dssm = mm(d_s, wts['w_bs_t'], tm=1024, tn=512, tk=2048, inner='m', out_dtype=F32, name="mm_dssm")
    grads['w_bs_t'] = mm(d_s, ssm, ta=True, tm=1024, tn=512, tk=2048, out_dtype=BF16, name="mm_dw_bs")
    dz, sgr['b_glu'] = _glu_bwd(dssm, z, name="glu_bwd")
    grads['w_glu_t'] = mm(dz, gy, ta=True, tm=1024, tn=512, tk=2048, out_dtype=BF16, name="mm_dw_glu")
    dgy = mm(dz, wts['w_glu_t'], tm=1024, tn=512, tk=1024, inner='m', out_dtype=F32, name="mm_dgy")
    du, d_bd, d_cd, d_ab, sgr['ssm_d'] = _ssm_bwd(proj, y, dgy, xs, ab, bdt_b, cdt_b, dskip, name="ssm_bwd")
    keys = ['w_out', 'w_ba_t', 'w_bs_t', 'w_glu_t']
    plan = scatter(keys)
    dq, dkc, dkp, dvc, dvp, dsink = take(_attn_bwd(proj, sinks, dattn, comm=plan, name="attn_bwd"), plan, keys, recvs)
    dkv = _kv_grad_merge(dkc, dkp, dvc, dvp, name="kv_grad_merge")
    sgr['attn_sinks'] = dsink[:, :N_Q_HEADS]
    dproj = jnp.concatenate([dq, dkv, du, dga, dgs], axis=1)
    sgr['b_in'] = _col_sum(dproj, name="col_sum_dproj")
    grads['w_in_t'] = mm(dproj, h1, ta=True, tm=2944, tn=1024, tk=1024, out_dtype=BF16, name="mm_dw_in")
    dh1 = mm_host(['w_in_t'], scatter, recvs,
                  dproj, wts['w_in_t'], tm=1024, tn=1024, tk=2944, out_dtype=F32, name="mm_dh1")
    grad_x, sgr['attn_norm_g'] = _rms_bwd(dh1, x, small['attn_norm_g'], dx2, with_bf16=False, name="rms1_bwd")

    from_bd = lambda t: _block_diag_take(t, SSM_GROUP, SSM_STATE).transpose(0, 1, 3, 2).reshape(
        SSM_GROUPS * SSM_STATE, SSM_GROUP)
    d_bb_re = from_bd(d_bd[:, :, :SSM_X_BLK])
    d_bb_im = from_bd(d_bd[:, :, SSM_X_BLK:])
    d_cdt = d_cd.transpose(0, 2, 1)
    shape_c = (1, SSM_GROUPS, SSM_GROUP, SSM_STATE)
    sgr['ssm_c_re'] = _block_diag_take(d_cdt[:, :, :SSM_X_BLK], SSM_GROUP, SSM_STATE).reshape(shape_c)
    sgr['ssm_c_im'] = -_block_diag_take(d_cdt[:, :, SSM_X_BLK:], SSM_GROUP, SSM_STATE).reshape(shape_c)
    d_a_re, d_a_im, d_ldt, d_b_re, d_b_im = _ssm_disc_bwd(
        a_re, a_im, log_dt, b_re, b_im, d_ab[:, 0, :].reshape(-1, 1), d_ab[:, 1, :].reshape(-1, 1),
        d_bb_re, d_bb_im, name="ssm_disc_bwd")
    sgr['ssm_a_re'] = d_a_re.reshape(1, SSM_GROUPS, SSM_STATE)
    sgr['ssm_a_im'] = d_a_im.reshape(1, SSM_GROUPS, SSM_STATE)
    sgr['ssm_log_dt'] = d_ldt.reshape(SSM_GROUPS, SSM_STATE).sum(axis=1).reshape(1, SSM_GROUPS)
    sgr['ssm_b_re'] = d_b_re.reshape(1, SSM_GROUPS, SSM_STATE, SSM_GROUP)
    sgr['ssm_b_im'] = d_b_im.reshape(1, SSM_GROUPS, SSM_STATE, SSM_GROUP)
    return loss, grad_x, grads, recvs, sgr


def _swap_cores(arrs, *, name):
    n = len(arrs)

    def body(*refs):
        ins, outs = refs[:n], refs[n:2 * n]
        send_sems, recv_sems = refs[2 * n:]
        x, y, c = _place()
        copies = []
        for i in range(n):
            cp = pltpu.make_async_remote_copy(
                src_ref=ins[i], dst_ref=outs[i], send_sem=send_sems.at[i], recv_sem=recv_sems.at[i],
                device_id=(x, y, 1 - c), device_id_type=MESH)
            cp.start()
            copies.append(cp)
        for cp in copies:
            cp.wait()

    return pl.pallas_call(
        body, in_specs=[ANY] * n, out_specs=[ANY] * n,
        out_shape=[jax.ShapeDtypeStruct(a.shape, a.dtype) for a in arrs],
        scratch_shapes=[pltpu.SemaphoreType.DMA((n,)), pltpu.SemaphoreType.DMA((n,))],
        name=name)(*arrs)


def _all_reduce_small(buf, *, name):
    r = buf.shape[0]

    def body(in_ref, out_ref, slots, send_sems, recv_sems):
        x, y, c = _place()
        me = 4 * x + 2 * y + c
        slots[pl.ds(me, 1)] = in_ref[...][None]
        copies = []
        for k in range(N_DEV - 1):
            bx, by, bc = ((k + 1) >> 2) & 1, ((k + 1) >> 1) & 1, (k + 1) & 1
            peer = (1 - x if bx else x, 1 - y if by else y, 1 - c if bc else c)
            cp = pltpu.make_async_remote_copy(
                src_ref=in_ref, dst_ref=slots.at[me], send_sem=send_sems.at[k], recv_sem=recv_sems.at[k],
                device_id=peer, device_id_type=MESH)
            cp.start()
            copies.append(cp)
        for cp in copies:
            cp.wait()
        acc = slots[0]
        for d in range(1, N_DEV):
            acc = acc + slots[d]
        out_ref[...] = acc

    vm = pl.BlockSpec(memory_space=pltpu.VMEM)
    return pl.pallas_call(
        body, in_specs=[vm], out_specs=vm, out_shape=jax.ShapeDtypeStruct((r, 128), F32),
        scratch_shapes=[pltpu.VMEM((N_DEV, r, 128), F32), pltpu.SemaphoreType.DMA((N_DEV - 1,)),
                        pltpu.SemaphoreType.DMA((N_DEV - 1,))],
        name=name)(buf)


def _pack(arrs):
    flat = jnp.concatenate([a.reshape(-1).astype(F32) for a in arrs])
    pad = (-flat.shape[0]) % 1024
    return jnp.pad(flat, (0, pad)).reshape(-1, 128)


def _unpack(buf, shapes):
    flat = buf.reshape(-1)
    out, pos = [], 0
    for s in shapes:
        size = math.prod(s)
        out.append(flat[pos:pos + size].reshape(s))
        pos += size
    return out


TILE_ELEMS = 256 * 1024


def _tile_rows(r, c):
    if r * c <= TILE_ELEMS:
        return r
    for tr in range(TILE_ELEMS // c // 16 * 16, 0, -16):
        if r % tr == 0:
            return tr
    raise ValueError((r, c))


def _sum4(own, recv, *, name):
    r, c = own.shape
    tr = _tile_rows(r, c)

    def body(o_ref, r_ref, out_ref):
        acc = o_ref[...].astype(F32)
        for k in range(3):
            acc = acc + r_ref[k].astype(F32)
        out_ref[...] = acc

    return pl.pallas_call(
        body, grid=(r // tr,),
        in_specs=[pl.BlockSpec((tr, c), lambda i: (i, 0)), pl.BlockSpec((3, tr, c), lambda i: (0, i, 0))],
        out_specs=pl.BlockSpec((tr, c), lambda i: (i, 0)), out_shape=jax.ShapeDtypeStruct((r, c), F32),
        name=name, compiler_params=_cp(("parallel",)))(own, recv)


def _adamw(w, ga, gb, m, v, *, name):
    r, c = w.shape
    tr = _tile_rows(r, c)
    bc1 = 1.0 - ADAM_B1 ** ADAM_STEP
    bc2 = 1.0 - ADAM_B2 ** ADAM_STEP
    two = gb is not None

    def body(*refs):
        w_ref, ga_ref = refs[0], refs[1]
        pos = 2
        g = ga_ref[...]
        if two:
            g = g + refs[pos][...]
            pos += 1
        m_ref, v_ref, g_out, d_out, m_out, v_out = refs[pos:pos + 6]
        mn = ADAM_B1 * m_ref[...] + (1.0 - ADAM_B1) * g
        vn = ADAM_B2 * v_ref[...] + (1.0 - ADAM_B2) * (g * g)
        m_hat = mn / bc1
        v_hat = vn / bc2
        g_out[...] = g
        d_out[...] = -ADAM_LR * (m_hat / (jnp.sqrt(v_hat) + ADAM_EPS) + ADAM_WD * w_ref[...])
        m_out[...] = mn
        v_out[...] = vn

    spec = pl.BlockSpec((tr, c), lambda i: (i, 0))
    args = [w, ga] + ([gb] if two else []) + [m, v]
    shp = jax.ShapeDtypeStruct((r, c), F32)
    return pl.pallas_call(
        body, grid=(r // tr,), in_specs=[spec] * len(args), out_specs=[spec] * 4,
        out_shape=[shp] * 4, name=name, compiler_params=_cp(("parallel",)))(*args)


BIG = ['w_in', 'w_glu', 'w_branch_attn', 'w_branch_ssm', 'w_out', 'w_up', 'w_down']
BIG_KEY = {'w_in': 'w_in_t', 'w_glu': 'w_glu_t', 'w_branch_attn': 'w_ba_t', 'w_branch_ssm': 'w_bs_t',
           'w_out': 'w_out', 'w_up': 'w_up_t', 'w_down': 'w_down'}
COL_SHARDED = {'w_in', 'w_glu', 'w_branch_attn', 'w_branch_ssm', 'w_up'}
SMALL = ['attn_norm_g', 'b_in', 'attn_sinks', 'ssm_a_re', 'ssm_a_im', 'ssm_log_dt', 'ssm_b_re', 'ssm_b_im',
         'ssm_c_re', 'ssm_c_im', 'ssm_d', 'b_glu', 'ffn_norm_g', 'conv_b', 'final_norm_g']
WEIGHTS = ['attn_norm_g', 'w_in', 'b_in', 'attn_sinks', 'ssm_a_re', 'ssm_a_im', 'ssm_log_dt', 'ssm_b_re',
           'ssm_b_im', 'ssm_c_re', 'ssm_c_im', 'ssm_d', 'w_glu', 'b_glu', 'w_branch_attn', 'w_branch_ssm',
           'w_out', 'ffn_norm_g', 'w_up', 'conv_w', 'conv_b', 'w_down', 'final_norm_g']


def _shard_2d(name, t):
    t = t[0]
    return t.T if name in COL_SHARDED else t


def _unshard_2d(name, t):
    return (t.T if name in COL_SHARDED else t)[None]


def kernel(x, attn_norm_g, w_in, b_in, attn_sinks, ssm_a_re, ssm_a_im, ssm_log_dt, ssm_b_re, ssm_b_im, ssm_c_re, ssm_c_im, ssm_d, w_glu, b_glu, w_branch_attn, w_branch_ssm, w_out, ffn_norm_g, w_up, conv_w, conv_b, w_down, final_norm_g, loss_target, m_attn_norm_g, m_w_in, m_b_in, m_attn_sinks, m_ssm_a_re, m_ssm_a_im, m_ssm_log_dt, m_ssm_b_re, m_ssm_b_im, m_ssm_c_re, m_ssm_c_im, m_ssm_d, m_w_glu, m_b_glu, m_w_branch_attn, m_w_branch_ssm, m_w_out, m_ffn_norm_g, m_w_up, m_conv_w, m_conv_b, m_w_down, m_final_norm_g, v_attn_norm_g, v_w_in, v_b_in, v_attn_sinks, v_ssm_a_re, v_ssm_a_im, v_ssm_log_dt, v_ssm_b_re, v_ssm_b_im, v_ssm_c_re, v_ssm_c_im, v_ssm_d, v_w_glu, v_b_glu, v_w_branch_attn, v_w_branch_ssm, v_w_out, v_ffn_norm_g, v_w_up, v_conv_w, v_conv_b, v_w_down, v_final_norm_g):
    args = dict(locals())
    w = {n: args[n] for n in WEIGHTS}
    m = {n: args['m_' + n] for n in WEIGHTS}
    v = {n: args['v_' + n] for n in WEIGHTS}
    xi, yi, ci = _place()
    blk = 2 * xi + yi

    shards = {BIG_KEY[n]: _shard_2d(n, w[n]).astype(BF16) for n in BIG}
    wts = {'w_in_t': _run_plan(_GatherPlan([shards['w_in_t']]), name="gather_w_in")[0]}
    cw_cols = w['conv_w'].shape[2]
    cw_place = lax.dynamic_update_slice(jnp.zeros((3, D_FF), F32), w['conv_w'][0] * (ci == 0).astype(F32),
                                        (0, blk * cw_cols))
    conv_w_full = _unpack(_all_reduce_small(_pack([cw_place]), name="gather_conv_w"), [(3, D_FF)])[0]

    small = {n: w[n] for n in SMALL}
    small['conv_w'] = conv_w_full
    loss_part, grad_x, grads, recvs, sgr = _local_step(x[0], loss_target[0], wts, small, shards)

    halves = []
    for n in BIG:
        full, recv = grads[BIG_KEY[n]], recvs[BIG_KEY[n]]
        r = full.shape[0] // N_CHIPS
        own = lax.dynamic_slice_in_dim(full, blk * r, r, axis=0)
        halves.append(_sum4(own, recv, name="sum4_" + n))
    others = _swap_cores(halves, name="swap_cores")
    out = {}
    for n, mine, other in zip(BIG, halves, others):
        res = _adamw(_shard_2d(n, w[n]), mine, other, _shard_2d(n, m[n]), _shard_2d(n, v[n]), name="adamw_" + n)
        out[n] = [_unshard_2d(n, t) for t in res]

    names = SMALL + ['conv_w']
    shapes = [w[n].shape for n in SMALL] + [(3, D_FF)]
    packed = _pack([sgr[n] for n in names] + [loss_part])
    summed = _unpack(_all_reduce_small(packed, name="all_reduce_small"), shapes + [(1, 1)])
    loss = summed[-1].reshape(())
    sg = dict(zip(names, summed[:-1]))
    sg['conv_w'] = lax.dynamic_slice_in_dim(sg['conv_w'], blk * cw_cols, cw_cols, axis=1)[None]
    res = _adamw(_pack([w[n] for n in names]), _pack([sg[n] for n in names]), None,
                 _pack([m[n] for n in names]), _pack([v[n] for n in names]), name="adamw_small")
    ushapes = [w[n].shape for n in names]
    unpacked = [_unpack(t, ushapes) for t in res]
    for i, n in enumerate(names):
        out[n] = [u[i] for u in unpacked]

    return (loss, grad_x[None], *[out[n][0] for n in WEIGHTS], *[out[n][1] for n in WEIGHTS],
            *[out[n][2] for n in WEIGHTS], *[out[n][3] for n in WEIGHTS])
```

```python
import functools
import math

import jax
import jax.numpy as jnp
from jax import lax
from jax.experimental import pallas as pl
from jax.experimental.pallas import tpu as pltpu

F32 = jnp.float32
BF16 = jnp.bfloat16

D_MODEL = 2048
N_Q_HEADS = 16
HEAD_DIM = 64
ATTN_WIDTH = 1024
KV_WIDTH = 128
BLOCK = 128
SSM_WIDTH = 512
SSM_GROUPS = 32
SSM_GROUP = 16
SSM_STATE = 64
D_FF = 5632
IN_COLS = 5888
RMS_EPS = 1e-6
NEG_BIG = -1e30
N_CHIPS = 4
N_DEV = 8

COL_K = 8
COL_V = 9
COL_U = 10
COL_GA = 14
COL_GS = 30

SSM_SPLIT = 4
SSM_U_BLK = 128
SSM_X_BLK = 512
SSM_CHUNK = 256

ADAM_LR = 0.001
ADAM_B1 = 0.9
ADAM_B2 = 0.999
ADAM_EPS = 1e-08
ADAM_WD = 0.01
ADAM_STEP = 10

VMEM_LIMIT_BYTES = 56 * 1024 * 1024
INV_SQRT2 = 1.0 / math.sqrt(2.0)
INV_SQRT2PI = 1.0 / math.sqrt(2.0 * math.pi)
MESH = pl.DeviceIdType.MESH
ANY = pl.BlockSpec(memory_space=pl.ANY)


def _cp(sem):
    return pltpu.CompilerParams(dimension_semantics=sem, vmem_limit_bytes=VMEM_LIMIT_BYTES)


def _gelu(x):
    return 0.5 * x * (1.0 + lax.erf(x * INV_SQRT2))


def _gelu_grad(x):
    return 0.5 * (1.0 + lax.erf(x * INV_SQRT2)) + x * jnp.exp(-0.5 * x * x) * INV_SQRT2PI


def _sigmoid(x):
    return 1.0 / (1.0 + jnp.exp(-x))


def _place():
    return lax.axis_index("x"), lax.axis_index("y"), lax.axis_index("c")


def _other_chips(x, y):
    return [(1 - x, y), (x, 1 - y), (1 - x, 1 - y)]


def _block_pos(x, y, interleaved):
    return x + 2 * y if interleaved else 2 * x + y


class _GatherPlan:
    def __init__(self, shards, interleaved):
        self.arrays = list(shards)
        self.interleaved = list(interleaved)
        n = len(shards)
        self.out_shape = [jax.ShapeDtypeStruct((N_CHIPS * s.shape[0], s.shape[1]), s.dtype) for s in shards]
        self.scratch = [pltpu.SemaphoreType.DMA((6 * n,)), pltpu.SemaphoreType.DMA((6 * n,)),
                        pltpu.SemaphoreType.DMA((n,))]

    def _copies(self, kind, ins, outs, sems):
        send, recv, local = sems
        n = len(self.arrays)
        x, y, c = _place()
        copies = []
        for i in range(n):
            r = self.arrays[i].shape[0]
            h = r // 2
            blk = _block_pos(x, y, self.interleaved[i])
            if kind == 'mine':
                copies.append(pltpu.make_async_copy(ins[i], outs[i].at[pl.ds(blk * r, r), :], local.at[i]))
                continue
            for k, (px, py) in enumerate(_other_chips(x, y)):
                theirs = _block_pos(px, py, self.interleaved[i]) * r
                if kind in ('ici_out', 'ici_in'):
                    route = dict(send_sem=send.at[3 * i + k], recv_sem=recv.at[3 * i + k],
                                 device_id=(px, py, c), device_id_type=MESH)
                else:
                    route = dict(send_sem=send.at[3 * (n + i) + k], recv_sem=recv.at[3 * (n + i) + k],
                                 device_id=(x, y, 1 - c), device_id_type=MESH)
                if kind == 'ici_out':
                    src, dst = ins[i].at[pl.ds(c * h, h), :], outs[i].at[pl.ds(blk * r + c * h, h), :]
                elif kind == 'd2d_in':
                    src = dst = outs[i].at[pl.ds(theirs + (1 - c) * h, h), :]
                else:
                    src = dst = outs[i].at[pl.ds(theirs + c * h, h), :]
                copies.append(pltpu.make_async_remote_copy(src_ref=src, dst_ref=dst, **route))
        return copies

    def start(self, ins, outs, sems):
        for cp in self._copies('mine', ins, outs, sems) + self._copies('ici_out', ins, outs, sems):
            cp.start()

    def middle(self, ins, outs, sems):
        for arrived, onward in zip(self._copies('ici_in', ins, outs, sems), self._copies('d2d_out', ins, outs, sems)):
            arrived.wait_recv()
            onward.start()

    def finish(self, ins, outs, sems):
        for cp in self._copies('d2d_in', ins, outs, sems):
            cp.wait_recv()
        for cp in self._copies('ici_out', ins, outs, sems) + self._copies('d2d_out', ins, outs, sems):
            cp.wait_send()
        for cp in self._copies('mine', ins, outs, sems):
            cp.wait()


class _ScatterPlan:
    def __init__(self, fulls, interleaved):
        self.arrays = list(fulls)
        self.interleaved = list(interleaved)
        n = len(fulls)
        self.out_shape = [jax.ShapeDtypeStruct((3, f.shape[0] // N_CHIPS, f.shape[1]), f.dtype) for f in fulls]
        self.scratch = [pltpu.SemaphoreType.DMA((3 * n,)), pltpu.SemaphoreType.DMA((3 * n,))]

    def _copies(self, ins, outs, sems):
        send, recv = sems
        x, y, c = _place()
        copies = []
        for i in range(len(self.arrays)):
            r = self.arrays[i].shape[0] // N_CHIPS
            for k, (px, py) in enumerate(_other_chips(x, y)):
                copies.append(pltpu.make_async_remote_copy(
                    src_ref=ins[i].at[pl.ds(_block_pos(px, py, self.interleaved[i]) * r, r), :], dst_ref=outs[i].at[k],
                    send_sem=send.at[3 * i + k], recv_sem=recv.at[3 * i + k],
                    device_id=(px, py, c), device_id_type=MESH))
        return copies

    def start(self, ins, outs, sems):
        for cp in self._copies(ins, outs, sems):
            cp.start()

    def middle(self, ins, outs, sems):
        pass

    def finish(self, ins, outs, sems):
        for cp in self._copies(ins, outs, sems):
            cp.wait()


def _hosted_call(body, *, grid, in_specs, out_specs, out_shape, scratch_shapes, sem, name, args, comm=None,
                 aliases=None):
    aliases = aliases or {}
    if comm is None:
        outs = pl.pallas_call(body, grid=grid, in_specs=in_specs, out_specs=out_specs, out_shape=out_shape,
                              scratch_shapes=scratch_shapes, name=name, input_output_aliases=aliases,
                              compiler_params=_cp(sem))(*args)
        return outs, None
    n_in, n_out, n_scr = len(in_specs), len(out_specs), len(scratch_shapes)
    nc, ns = len(comm.arrays), len(comm.scratch)
    total = math.prod(grid)
    mid = min(total - 1, (3 * total) // 4)

    def wrapped(*refs):
        pos = 0
        ins = refs[pos:pos + n_in]; pos += n_in
        cins = refs[pos:pos + nc]; pos += nc
        outs = refs[pos:pos + n_out]; pos += n_out
        couts = refs[pos:pos + nc]; pos += nc
        scr = refs[pos:pos + n_scr]; pos += n_scr
        sems = refs[pos:pos + ns]
        step = 0
        for ax, g in enumerate(grid):
            step = step * g + pl.program_id(ax)

        @pl.when(step == 0)
        def _():
            comm.start(cins, couts, sems)

        body(*ins, *outs, *scr)

        @pl.when(step == mid)
        def _():
            comm.middle(cins, couts, sems)

        @pl.when(step == total - 1)
        def _():
            comm.finish(cins, couts, sems)

    res = pl.pallas_call(
        wrapped, grid=grid, in_specs=list(in_specs) + [ANY] * nc, out_specs=list(out_specs) + [ANY] * nc,
        out_shape=list(out_shape) + list(comm.out_shape), scratch_shapes=list(scratch_shapes) + list(comm.scratch),
        name=name, input_output_aliases=aliases,
        compiler_params=_cp(("arbitrary",) * len(grid)))(*args, *comm.arrays)
    return res[:n_out], res[n_out:]


def _matmul(a, b, *, ta=False, tb=False, tm, tn, tk, out_dtype, bias=None, res=None, inner='n', comm=None,
            name):
    if ta:
        kdim, m = a.shape
    else:
        m, kdim = a.shape
    if tb:
        n, k2 = b.shape
    else:
        k2, n = b.shape
    assert kdim == k2, (a.shape, b.shape)
    tm, tn, tk = min(tm, m), min(tn, n), min(tk, kdim)
    assert m % tm == 0 and n % tn == 0 and kdim % tk == 0, (name, m, n, kdim, tm, tn, tk)
    nk = kdim // tk
    dn = (((0 if ta else 1,), (1 if tb else 0,)), ((), ()))

    def body(*refs):
        a_ref, b_ref = refs[0], refs[1]
        pos = 2
        bias_ref = res_ref = None
        if bias is not None:
            bias_ref = refs[pos]
            pos += 1
        if res is not None:
            res_ref = refs[pos]
            pos += 1
        o_ref = refs[pos]

        def finish(r):
            if bias_ref is not None:
                r = r + bias_ref[...]
            if res_ref is not None:
                r = r + res_ref[...]
            o_ref[...] = r.astype(out_dtype)

        prod = lax.dot_general(a_ref[...].astype(BF16), b_ref[...].astype(BF16), dn, preferred_element_type=F32)
        if nk == 1:
            finish(prod)
            return
        acc_ref = refs[pos + 1]
        k = pl.program_id(2)

        @pl.when(k == 0)
        def _():
            acc_ref[...] = prod

        @pl.when(k > 0)
        def _():
            acc_ref[...] += prod

        @pl.when(k == nk - 1)
        def _():
            finish(acc_ref[...])

    if inner == 'n':
        grid = (m // tm, n // tn, nk)
        mi = lambda g0, g1: g0
        ni = lambda g0, g1: g1
    else:
        grid = (n // tn, m // tm, nk)
        mi = lambda g0, g1: g1
        ni = lambda g0, g1: g0
    a_spec = (pl.BlockSpec((tk, tm), lambda g0, g1, k: (k, mi(g0, g1))) if ta
              else pl.BlockSpec((tm, tk), lambda g0, g1, k: (mi(g0, g1), k)))
    b_spec = (pl.BlockSpec((tn, tk), lambda g0, g1, k: (ni(g0, g1), k)) if tb
              else pl.BlockSpec((tk, tn), lambda g0, g1, k: (k, ni(g0, g1))))
    in_specs = [a_spec, b_spec]
    args = [a, b]
    if bias is not None:
        in_specs.append(pl.BlockSpec((1, tn), lambda g0, g1, k: (0, ni(g0, g1))))
        args.append(bias)
    if res is not None:
        in_specs.append(pl.BlockSpec((tm, tn), lambda g0, g1, k: (mi(g0, g1), ni(g0, g1))))
        args.append(res)
    outs, couts = _hosted_call(
        body, grid=grid, in_specs=in_specs,
        out_specs=[pl.BlockSpec((tm, tn), lambda g0, g1, k: (mi(g0, g1), ni(g0, g1)))],
        out_shape=[jax.ShapeDtypeStruct((m, n), out_dtype)],
        scratch_shapes=[pltpu.VMEM((tm, tn), F32)] if nk > 1 else [],
        sem=("parallel", "parallel", "arbitrary"), name=name, args=args, comm=comm)
    return outs[0] if comm is None else (outs[0], couts)


def _rms_fwd(x, g, *, comm=None, name):
    l, d = x.shape
    tr = min(256, l)

    def body(x_ref, g_ref, h_ref):
        xf = x_ref[...]
        r = lax.rsqrt(jnp.mean(xf * xf, axis=-1, keepdims=True) + RMS_EPS)
        h_ref[...] = ((xf * r) * g_ref[...]).astype(BF16)

    row = pl.BlockSpec((tr, d), lambda i: (i, 0))
    return _hosted_call(
        body, grid=(l // tr,), in_specs=[row, pl.BlockSpec((1, d), lambda i: (0, 0))],
        out_specs=[row], out_shape=[jax.ShapeDtypeStruct((l, d), BF16)], scratch_shapes=[],
        sem=("parallel",), name=name, args=(x, g), comm=comm)


def _rms_bwd(dy, x, g, dres, *, with_bf16, name):
    l, d = x.shape
    tr = min(256, l)

    def body(dy_ref, x_ref, g_ref, dres_ref, *outs):
        dx_ref = outs[0]
        dg_ref = outs[-1]
        xf = x_ref[...]
        r = lax.rsqrt(jnp.mean(xf * xf, axis=-1, keepdims=True) + RMS_EPS)
        xhat = xf * r
        dyv = dy_ref[...]
        dxh = dyv * g_ref[...]
        dx = r * (dxh - xhat * jnp.mean(dxh * xhat, axis=-1, keepdims=True)) + dres_ref[...]
        dx_ref[...] = dx
        if with_bf16:
            outs[1][...] = dx.astype(BF16)

        @pl.when(pl.program_id(0) == 0)
        def _():
            dg_ref[...] = jnp.zeros_like(dg_ref)

        dg_ref[...] += jnp.sum(dyv * xhat, axis=0, keepdims=True)

    row = pl.BlockSpec((tr, d), lambda i: (i, 0))
    vec = pl.BlockSpec((1, d), lambda i: (0, 0))
    out_specs = [row] + ([row] if with_bf16 else []) + [vec]
    out_shape = ([jax.ShapeDtypeStruct((l, d), F32)]
                 + ([jax.ShapeDtypeStruct((l, d), BF16)] if with_bf16 else [])
                 + [jax.ShapeDtypeStruct((1, d), F32)])
    return pl.pallas_call(
        body, grid=(l // tr,), in_specs=[row, row, vec, row], out_specs=out_specs,
        out_shape=out_shape, name=name, compiler_params=_cp(("arbitrary",)))(dy, x, g, dres)


def _final_loss(x3, g, target, *, name):
    l, d = x3.shape
    tr = min(256, l)

    def body(x_ref, g_ref, t_ref, dx_ref, dxb_ref, dg_ref, loss_ref):
        xf = x_ref[...]
        gv = g_ref[...]
        r = lax.rsqrt(jnp.mean(xf * xf, axis=-1, keepdims=True) + RMS_EPS)
        xhat = xf * r
        diff = xhat * gv - t_ref[...]
        dout = diff * (1.0 / d)
        dxh = dout * gv
        dx = r * (dxh - xhat * jnp.mean(dxh * xhat, axis=-1, keepdims=True))
        dx_ref[...] = dx
        dxb_ref[...] = dx.astype(BF16)

        @pl.when(pl.program_id(0) == 0)
        def _():
            dg_ref[...] = jnp.zeros_like(dg_ref)
            loss_ref[...] = jnp.zeros_like(loss_ref)

        dg_ref[...] += jnp.sum(dout * xhat, axis=0, keepdims=True)
        part = jnp.sum(jnp.mean(diff * diff, axis=-1, keepdims=True), axis=0, keepdims=True)
        loss_ref[...] += 0.5 * part

    row = pl.BlockSpec((tr, d), lambda i: (i, 0))
    vec = pl.BlockSpec((1, d), lambda i: (0, 0))
    return pl.pallas_call(
        body, grid=(l // tr,), in_specs=[row, vec, row],
        out_specs=[row, row, vec, pl.BlockSpec((1, 1), lambda i: (0, 0))],
        out_shape=[jax.ShapeDtypeStruct((l, d), F32), jax.ShapeDtypeStruct((l, d), BF16),
                   jax.ShapeDtypeStruct((1, d), F32), jax.ShapeDtypeStruct((1, 1), F32)],
        name=name, compiler_params=_cp(("arbitrary",)))(x3, g, target)


Q_PER_KV = 8
GROUP_ROWS = Q_PER_KV * BLOCK


def _attn_masks(n):
    q_idx = lax.broadcasted_iota(jnp.int32, (GROUP_ROWS, 2 * BLOCK), 0) & (BLOCK - 1)
    s_idx = lax.broadcasted_iota(jnp.int32, (GROUP_ROWS, 2 * BLOCK), 1)
    dist = q_idx + BLOCK - s_idx
    valid = (dist >= 0) & (dist < BLOCK) & ((n > 0) | (s_idx >= BLOCK))
    return dist.astype(F32), valid


def _dup_half(t, kv_head, lo):
    rolled = pltpu.roll(t, HEAD_DIM, axis=1)
    return jnp.where(lo, t, rolled) if kv_head == 0 else jnp.where(lo, rolled, t)


def _stack_heads(ref, kv_head, lo):
    pieces = []
    for r in range(Q_PER_KV):
        pair = kv_head * 4 + r // 2
        t = ref[:, pair * 128:(pair + 1) * 128].astype(BF16)
        sel = lo if r % 2 == 0 else jnp.logical_not(lo)
        pieces.append(jnp.where(sel, t, jnp.zeros_like(t)))
    return jnp.concatenate(pieces, axis=0)


def _unstack_heads(t, lo):
    return [jnp.where(lo, t[(2 * i) * BLOCK:(2 * i + 1) * BLOCK], t[(2 * i + 1) * BLOCK:(2 * i + 2) * BLOCK])
            for i in range(Q_PER_KV // 2)]


def _per_head_column(values):
    return jnp.concatenate([jnp.full((BLOCK, 1), v, F32) for v in values], axis=0)


def _group_probs(qm, kdup, dist, valid, sink_ref, kv_head):
    heads = [kv_head * Q_PER_KV + r for r in range(Q_PER_KV)]
    slope = _per_head_column([2.0 ** (-8.0 * (h + 1) / N_Q_HEADS) for h in heads])
    sink = _per_head_column([sink_ref[h] for h in heads])
    s = lax.dot_general(qm, kdup, (((1,), (1,)), ((), ())), preferred_element_type=F32)
    s = s * (HEAD_DIM ** -0.5) - slope * dist
    s = jnp.where(valid, s, NEG_BIG)
    m = jnp.maximum(jnp.max(s, axis=-1, keepdims=True), sink)
    p = jnp.exp(s - m)
    esink = jnp.exp(sink - m)
    inv = 1.0 / (jnp.sum(p, axis=-1, keepdims=True) + esink)
    return p * inv, esink * inv


def _attn_fwd(proj, sinks, *, name):
    l = proj.shape[0]
    nb = l // BLOCK

    def body(sink_ref, q_ref, kc_ref, kp_ref, vc_ref, vp_ref, o_ref):
        n = pl.program_id(0)
        dist, valid = _attn_masks(n)
        lo = lax.broadcasted_iota(jnp.int32, (1, BLOCK), 1) < HEAD_DIM
        kx = jnp.concatenate([kp_ref[...], kc_ref[...]], axis=0).astype(BF16)
        vx = jnp.concatenate([vp_ref[...], vc_ref[...]], axis=0).astype(BF16)
        for kv_head in range(2):
            kdup = _dup_half(kx, kv_head, lo)
            vdup = _dup_half(vx, kv_head, lo)
            qm = _stack_heads(q_ref, kv_head, lo)
            p, _ = _group_probs(qm, kdup, dist, valid, sink_ref, kv_head)
            o = jnp.dot(p.astype(BF16), vdup, preferred_element_type=F32)
            for i, o_pair in enumerate(_unstack_heads(o, lo)):
                pair = kv_head * 4 + i
                o_ref[:, pair * 128:(pair + 1) * 128] = o_pair.astype(BF16)

    kv = lambda col, prev: pl.BlockSpec(
        (BLOCK, KV_WIDTH), (lambda n: (jnp.maximum(n - 1, 0), col)) if prev else (lambda n: (n, col)))
    return pl.pallas_call(
        body, grid=(nb,),
        in_specs=[pl.BlockSpec(memory_space=pltpu.SMEM),
                  pl.BlockSpec((BLOCK, ATTN_WIDTH), lambda n: (n, 0)),
                  kv(COL_K, False), kv(COL_K, True), kv(COL_V, False), kv(COL_V, True)],
        out_specs=pl.BlockSpec((BLOCK, ATTN_WIDTH), lambda n: (n, 0)),
        out_shape=jax.ShapeDtypeStruct((l, ATTN_WIDTH), BF16), name=name,
        compiler_params=_cp(("parallel",)))(sinks, proj, proj, proj, proj, proj)


def _attn_bwd(proj, sinks, dattn, *, comm=None, name):
    l = proj.shape[0]
    nb = l // BLOCK

    def body(sink_ref, q_ref, kc_ref, kp_ref, vc_ref, vp_ref, do_ref,
             dq_ref, dkc_ref, dkp_ref, dvc_ref, dvp_ref, dsink_ref):
        n = pl.program_id(0)
        dist, valid = _attn_masks(n)
        lane = lax.broadcasted_iota(jnp.int32, (1, BLOCK), 1)
        lo = lane < HEAD_DIM
        kx = jnp.concatenate([kp_ref[...], kc_ref[...]], axis=0).astype(BF16)
        vx = jnp.concatenate([vp_ref[...], vc_ref[...]], axis=0).astype(BF16)
        dsink = jnp.zeros((1, BLOCK), F32)
        dk_heads, dv_heads = [], []
        for kv_head in range(2):
            kdup = _dup_half(kx, kv_head, lo)
            vdup = _dup_half(vx, kv_head, lo)
            qm = _stack_heads(q_ref, kv_head, lo)
            dom = _stack_heads(do_ref, kv_head, lo)
            p, psink = _group_probs(qm, kdup, dist, valid, sink_ref, kv_head)
            dp = lax.dot_general(dom, vdup, (((1,), (1,)), ((), ())), preferred_element_type=F32)
            delta = jnp.sum(p * dp, axis=-1, keepdims=True)
            ds = (p * (dp - delta) * (HEAD_DIM ** -0.5)).astype(BF16)
            dsink_rows = -psink * delta
            for r in range(Q_PER_KV):
                part = jnp.sum(dsink_rows[r * BLOCK:(r + 1) * BLOCK])
                dsink = dsink + jnp.where(lane == kv_head * Q_PER_KV + r, part, 0.0)
            dq = jnp.dot(ds, kdup, preferred_element_type=F32)
            for i, dq_pair in enumerate(_unstack_heads(dq, lo)):
                pair = kv_head * 4 + i
                dq_ref[:, pair * 128:(pair + 1) * 128] = dq_pair.astype(BF16)
            dk_acc = lax.dot_general(ds, qm, (((0,), (0,)), ((), ())), preferred_element_type=F32)
            dv_acc = lax.dot_general(p.astype(BF16), dom, (((0,), (0,)), ((), ())), preferred_element_type=F32)
            dk_heads.append(dk_acc + pltpu.roll(dk_acc, HEAD_DIM, axis=1))
            dv_heads.append(dv_acc + pltpu.roll(dv_acc, HEAD_DIM, axis=1))
        dk = jnp.where(lo, dk_heads[0], dk_heads[1])
        dv = jnp.where(lo, dv_heads[0], dv_heads[1])
        dkp_ref[...] = dk[:BLOCK]
        dkc_ref[...] = dk[BLOCK:]
        dvp_ref[...] = dv[:BLOCK]
        dvc_ref[...] = dv[BLOCK:]

        @pl.when(n == 0)
        def _():
            dsink_ref[...] = jnp.zeros_like(dsink_ref)

        dsink_ref[...] += dsink

    kv = lambda col, prev: pl.BlockSpec(
        (BLOCK, KV_WIDTH), (lambda n: (jnp.maximum(n - 1, 0), col)) if prev else (lambda n: (n, col)))
    qspec = pl.BlockSpec((BLOCK, ATTN_WIDTH), lambda n: (n, 0))
    kvout = pl.BlockSpec((BLOCK, KV_WIDTH), lambda n: (n, 0))
    kvshape = jax.ShapeDtypeStruct((l, KV_WIDTH), F32)
    return _hosted_call(
        body, grid=(nb,),
        in_specs=[pl.BlockSpec(memory_space=pltpu.SMEM), qspec,
                  kv(COL_K, False), kv(COL_K, True), kv(COL_V, False), kv(COL_V, True), qspec],
        out_specs=[qspec, kvout, kvout, kvout, kvout, pl.BlockSpec((1, BLOCK), lambda n: (0, 0))],
        out_shape=[jax.ShapeDtypeStruct((l, ATTN_WIDTH), BF16), kvshape, kvshape, kvshape, kvshape,
                   jax.ShapeDtypeStruct((1, BLOCK), F32)],
        scratch_shapes=[], sem=("arbitrary",), name=name,
        args=(sinks, proj, proj, proj, proj, proj, dattn), comm=comm)


def _kv_grad_merge(dkc, dkp, dvc, dvp, *, name):
    l = dkc.shape[0]
    nb = l // BLOCK

    def body(dkc_ref, dkp_ref, dvc_ref, dvp_ref, o_ref):
        last = pl.program_id(0) == nb - 1
        o_ref[:, :KV_WIDTH] = (dkc_ref[...] + jnp.where(last, 0.0, dkp_ref[...])).astype(BF16)
        o_ref[:, KV_WIDTH:] = (dvc_ref[...] + jnp.where(last, 0.0, dvp_ref[...])).astype(BF16)

    cur = pl.BlockSpec((BLOCK, KV_WIDTH), lambda n: (n, 0))
    nxt = pl.BlockSpec((BLOCK, KV_WIDTH), lambda n: (jnp.minimum(n + 1, nb - 1), 0))
    return pl.pallas_call(
        body, grid=(nb,), in_specs=[cur, nxt, cur, nxt],
        out_specs=pl.BlockSpec((BLOCK, 2 * KV_WIDTH), lambda n: (n, 0)),
        out_shape=jax.ShapeDtypeStruct((l, 2 * KV_WIDTH), BF16), name=name,
        compiler_params=_cp(("parallel",)))(dkc, dkp, dvc, dvp)


def _discretize(a_re, a_im, log_dt, b_re, b_im):
    dt = jnp.exp(log_dt)
    mag = jnp.exp(a_re * dt)
    ab_re = mag * jnp.cos(a_im * dt)
    ab_im = mag * jnp.sin(a_im * dt)
    nr = ab_re - 1.0
    ni = ab_im
    den = a_re * a_re + a_im * a_im
    z_re = (nr * a_re + ni * a_im) / den
    z_im = (ni * a_re - nr * a_im) / den
    bb_re = z_re * b_re - z_im * b_im
    bb_im = z_re * b_im + z_im * b_re
    return ab_re, ab_im, bb_re, bb_im


def _ssm_disc_fwd(a_re, a_im, log_dt, b_re, b_im, *, name):
    def body(ar, ai, ld, br, bi, o_ar, o_ai, o_br, o_bi):
        r = _discretize(ar[...], ai[...], ld[...], br[...], bi[...])
        o_ar[...], o_ai[...], o_br[...], o_bi[...] = r

    col = jax.ShapeDtypeStruct(a_re.shape, F32)
    mat = jax.ShapeDtypeStruct(b_re.shape, F32)
    return pl.pallas_call(body, out_shape=[col, col, mat, mat], name=name)(a_re, a_im, log_dt, b_re, b_im)


def _ssm_disc_bwd(a_re, a_im, log_dt, b_re, b_im, d_ab_re, d_ab_im, d_bb_re, d_bb_im, *, name):
    def body(ar, ai, ld, br, bi, g0, g1, g2, g3, o_ar, o_ai, o_ld, o_br, o_bi):
        _, vjp = jax.vjp(_discretize, ar[...], ai[...], ld[...], br[...], bi[...])
        r = vjp((g0[...], g1[...], g2[...], g3[...]))
        o_ar[...], o_ai[...], o_ld[...], o_br[...], o_bi[...] = r

    col = jax.ShapeDtypeStruct(a_re.shape, F32)
    mat = jax.ShapeDtypeStruct(b_re.shape, F32)
    return pl.pallas_call(body, out_shape=[col, col, col, mat, mat], name=name)(
        a_re, a_im, log_dt, b_re, b_im, d_ab_re, d_ab_im, d_bb_re, d_bb_im)


def _shift_rows(x, d, rows, *, down):
    t = x.shape[0]
    if down:
        return jnp.where(rows >= d, pltpu.roll(x, d, axis=0), 0.0)
    return jnp.where(rows < t - d, pltpu.roll(x, t - d, axis=0), 0.0)


def _scan_chunk(xr, xi, ar, ai, *, down):
    t = xr.shape[0]
    rows = lax.broadcasted_iota(jnp.int32, (t, 1), 0)
    pr, pi = ar, ai
    d = 1
    while d < t:
        sr = _shift_rows(xr, d, rows, down=down)
        si = _shift_rows(xi, d, rows, down=down)
        xr, xi = xr + pr * sr - pi * si, xi + pr * si + pi * sr
        pr, pi = pr * pr - pi * pi, 2.0 * pr * pi
        d *= 2
    return xr, xi


def _ssm_fwd(proj, ab, bd, cd, dskip, *, comm=None, name):
    l = proj.shape[0]
    t = min(SSM_CHUNK, l)
    nc = l // t

    def body(u_ref, ab_ref, bd_ref, cd_ref, ds_ref, y_ref, gy_ref, xs_ref, carry_ref):
        c = pl.program_id(1)

        @pl.when(c == 0)
        def _():
            carry_ref[...] = jnp.zeros_like(carry_ref)

        u = u_ref[...]
        ar, ai = ab_ref[0, 0:1, :], ab_ref[0, 1:2, :]
        bu = jnp.dot(u.astype(BF16), bd_ref[0], preferred_element_type=F32)
        rows = lax.broadcasted_iota(jnp.int32, (t, 1), 0)
        cr, ci = carry_ref[0:1, :], carry_ref[1:2, :]
        xr = bu[:, :SSM_X_BLK] + jnp.where(rows == 0, ar * cr - ai * ci, 0.0)
        xi = bu[:, SSM_X_BLK:] + jnp.where(rows == 0, ar * ci + ai * cr, 0.0)
        xr, xi = _scan_chunk(xr, xi, ar, ai, down=True)
        xs_ref[0, :, :SSM_X_BLK] = xr
        xs_ref[0, :, SSM_X_BLK:] = xi
        carry_ref[0:1, :] = xs_ref[0, t - 1:t, :SSM_X_BLK]
        carry_ref[1:2, :] = xs_ref[0, t - 1:t, SSM_X_BLK:]
        y = jnp.dot(xs_ref[0].astype(BF16), cd_ref[0], preferred_element_type=F32) + ds_ref[...] * u
        y_ref[...] = y
        gy_ref[...] = _gelu(y).astype(BF16)

    blk = lambda shape: pl.BlockSpec((1,) + shape, lambda j, c: (j, 0, 0))
    ycol = pl.BlockSpec((t, SSM_U_BLK), lambda j, c: (c, j))
    return _hosted_call(
        body, grid=(SSM_SPLIT, nc),
        in_specs=[pl.BlockSpec((t, SSM_U_BLK), lambda j, c: (c, COL_U + j)),
                  blk((2, SSM_X_BLK)), blk((SSM_U_BLK, 2 * SSM_X_BLK)), blk((2 * SSM_X_BLK, SSM_U_BLK)),
                  pl.BlockSpec((1, SSM_U_BLK), lambda j, c: (0, j))],
        out_specs=[ycol, ycol, pl.BlockSpec((1, t, 2 * SSM_X_BLK), lambda j, c: (j, c, 0))],
        out_shape=[jax.ShapeDtypeStruct((l, SSM_WIDTH), F32), jax.ShapeDtypeStruct((l, SSM_WIDTH), BF16),
                   jax.ShapeDtypeStruct((SSM_SPLIT, l, 2 * SSM_X_BLK), F32)],
        scratch_shapes=[pltpu.VMEM((2, SSM_X_BLK), F32)], sem=("parallel", "arbitrary"), name=name,
        args=(proj, ab, bd, cd, dskip), comm=comm)


def _ssm_bwd(proj, y, dgy, xs, ab, bdt, cdt, dskip, *, name):
    l = proj.shape[0]
    t = min(SSM_CHUNK, l)
    nc = l // t

    def body(u_ref, y_ref, dgy_ref, xs_ref, halo_ref, ab_ref, bdt_ref, cdt_ref, ds_ref,
             du_ref, dbd_ref, dcd_ref, dab_ref, dd_ref, carry_ref):
        c = pl.program_id(1)
        ci_ = nc - 1 - c

        @pl.when(c == 0)
        def _():
            carry_ref[...] = jnp.zeros_like(carry_ref)
            dbd_ref[...] = jnp.zeros_like(dbd_ref)
            dcd_ref[...] = jnp.zeros_like(dcd_ref)
            dab_ref[...] = jnp.zeros_like(dab_ref)
            dd_ref[...] = jnp.zeros_like(dd_ref)

        u = u_ref[...]
        dy = dgy_ref[...] * _gelu_grad(y_ref[...])
        dyb = dy.astype(BF16)
        ar, ai = ab_ref[0, 0:1, :], ab_ref[0, 1:2, :]
        g = jnp.dot(dyb, cdt_ref[0], preferred_element_type=F32)
        rows = lax.broadcasted_iota(jnp.int32, (t, 1), 0)
        cr, ci = carry_ref[0:1, :], carry_ref[1:2, :]
        lr = g[:, :SSM_X_BLK] + jnp.where(rows == t - 1, ar * cr + ai * ci, 0.0)
        li = g[:, SSM_X_BLK:] + jnp.where(rows == t - 1, ar * ci - ai * cr, 0.0)
        lr, li = _scan_chunk(lr, li, ar, -ai, down=False)
        lam = jnp.concatenate([lr, li], axis=1)
        carry_ref[0:1, :] = lr[0:1, :]
        carry_ref[1:2, :] = li[0:1, :]
        lamb = lam.astype(BF16)
        du_ref[...] = (jnp.dot(lamb, bdt_ref[0], preferred_element_type=F32) + ds_ref[...] * dy).astype(BF16)
        dbd_ref[0] += lax.dot_general(u.astype(BF16), lamb, (((0,), (0,)), ((), ())),
                                      preferred_element_type=F32)
        xs = xs_ref[0]
        dcd_ref[0] += lax.dot_general(xs.astype(BF16), dyb, (((0,), (0,)), ((), ())),
                                      preferred_element_type=F32)
        halo = jnp.where(ci_ > 0, halo_ref[0, 7:8, :], 0.0)
        xprev = jnp.where(rows == 0, halo, pltpu.roll(xs, 1, axis=0))
        xpr, xpi = xprev[:, :SSM_X_BLK], xprev[:, SSM_X_BLK:]
        dab_ref[0, 0:1, :] += jnp.sum(lr * xpr + li * xpi, axis=0, keepdims=True)
        dab_ref[0, 1:2, :] += jnp.sum(li * xpr - lr * xpi, axis=0, keepdims=True)
        dd_ref[...] += jnp.sum(dy * u, axis=0, keepdims=True)

    blk = lambda shape: pl.BlockSpec((1,) + shape, lambda j, c: (j, 0, 0))
    rev = lambda j, c: (nc - 1 - c, j)
    ycol = pl.BlockSpec((t, SSM_U_BLK), rev)
    hb = t // 8
    return pl.pallas_call(
        body, grid=(SSM_SPLIT, nc),
        in_specs=[pl.BlockSpec((t, SSM_U_BLK), lambda j, c: (nc - 1 - c, COL_U + j)), ycol, ycol,
                  pl.BlockSpec((1, t, 2 * SSM_X_BLK), lambda j, c: (j, nc - 1 - c, 0)),
                  pl.BlockSpec((1, 8, 2 * SSM_X_BLK),
                               lambda j, c: (j, jnp.maximum((nc - 1 - c) * hb - 1, 0), 0)),
                  blk((2, SSM_X_BLK)), blk((2 * SSM_X_BLK, SSM_U_BLK)), blk((SSM_U_BLK, 2 * SSM_X_BLK)),
                  pl.BlockSpec((1, SSM_U_BLK), lambda j, c: (0, j))],
        out_specs=[ycol, blk((SSM_U_BLK, 2 * SSM_X_BLK)), blk((2 * SSM_X_BLK, SSM_U_BLK)),
                   blk((2, SSM_X_BLK)), pl.BlockSpec((1, SSM_U_BLK), lambda j, c: (0, j))],
        out_shape=[jax.ShapeDtypeStruct((l, SSM_WIDTH), BF16),
                   jax.ShapeDtypeStruct((SSM_SPLIT, SSM_U_BLK, 2 * SSM_X_BLK), F32),
                   jax.ShapeDtypeStruct((SSM_SPLIT, 2 * SSM_X_BLK, SSM_U_BLK), F32),
                   jax.ShapeDtypeStruct((SSM_SPLIT, 2, SSM_X_BLK), F32),
                   jax.ShapeDtypeStruct((1, SSM_WIDTH), F32)],
        scratch_shapes=[pltpu.VMEM((2, SSM_X_BLK), F32)], name=name,
        compiler_params=_cp(("parallel", "arbitrary")))(proj, y, dgy, xs, xs, ab, bdt, cdt, dskip)


def _block_diag(t):
    s, g, a, b = t.shape
    return jnp.einsum('sgab,gk->sgakb', t, jnp.eye(g, dtype=t.dtype)).reshape(s, g * a, g * b)


def _block_diag_take(t, a, b):
    s = t.shape[0]
    return jnp.einsum('sgakb,gk->sgab', t.reshape(s, 8, a, 8, b), jnp.eye(8, dtype=t.dtype))


def _glu_fwd(z, *, name):
    l = z.shape[0]
    tr = min(512, l)

    def body(zv_ref, zg_ref, o_ref):
        o_ref[...] = (zv_ref[...] * _sigmoid(zg_ref[...])).astype(BF16)

    return pl.pallas_call(
        body, grid=(l // tr,),
        in_specs=[pl.BlockSpec((tr, SSM_WIDTH), lambda i: (i, 0)), pl.BlockSpec((tr, SSM_WIDTH), lambda i: (i, 1))],
        out_specs=pl.BlockSpec((tr, SSM_WIDTH), lambda i: (i, 0)),
        out_shape=jax.ShapeDtypeStruct((l, SSM_WIDTH), BF16), name=name,
        compiler_params=_cp(("parallel",)))(z, z)


def _glu_bwd(dssm, z, *, name):
    l = z.shape[0]
    tr = min(512, l)

    def body(d_ref, zv_ref, zg_ref, dz_ref, db_ref):
        d = d_ref[...]
        sg = _sigmoid(zg_ref[...])
        dv = d * sg
        dg = d * zv_ref[...] * sg * (1.0 - sg)
        dz_ref[:, :SSM_WIDTH] = dv.astype(BF16)
        dz_ref[:, SSM_WIDTH:] = dg.astype(BF16)

        @pl.when(pl.program_id(0) == 0)
        def _():
            db_ref[...] = jnp.zeros_like(db_ref)

        db_ref[:, :SSM_WIDTH] += jnp.sum(dv, axis=0, keepdims=True)
        db_ref[:, SSM_WIDTH:] += jnp.sum(dg, axis=0, keepdims=True)

    half = lambda k: pl.BlockSpec((tr, SSM_WIDTH), lambda i: (i, k))
    return pl.pallas_call(
        body, grid=(l // tr,), in_specs=[half(0), half(0), half(1)],
        out_specs=[pl.BlockSpec((tr, 2 * SSM_WIDTH), lambda i: (i, 0)),
                   pl.BlockSpec((1, 2 * SSM_WIDTH), lambda i: (0, 0))],
        out_shape=[jax.ShapeDtypeStruct((l, 2 * SSM_WIDTH), BF16), jax.ShapeDtypeStruct((1, 2 * SSM_WIDTH), F32)],
        name=name, compiler_params=_cp(("arbitrary",)))(dssm, z, z)


GATE_TC = 256
GATE_NJ = D_MODEL // GATE_TC


def _merge_fwd(proj, a, s, *, name):
    l = a.shape[0]
    tr = min(2048, l)

    def body(ga_ref, gs_ref, a_ref, s_ref, o_ref):
        o_ref[...] = (_sigmoid(ga_ref[...]) * a_ref[...] + _sigmoid(gs_ref[...]) * s_ref[...]).astype(BF16)

    own = pl.BlockSpec((tr, GATE_TC), lambda i, j: (i, j))
    return pl.pallas_call(
        body, grid=(l // tr, GATE_NJ),
        in_specs=[pl.BlockSpec((tr, GATE_TC), lambda i, j: (i, COL_GA // 2 + j)),
                  pl.BlockSpec((tr, GATE_TC), lambda i, j: (i, COL_GS // 2 + j)), own, own],
        out_specs=own, out_shape=jax.ShapeDtypeStruct((l, D_MODEL), BF16), name=name,
        compiler_params=_cp(("parallel", "parallel")))(proj, proj, a, s)


def _merge_bwd(proj, a, s, dm, *, name):
    l = a.shape[0]
    tr = min(2048, l)

    def body(ga_ref, gs_ref, a_ref, s_ref, dm_ref, da_ref, ds_ref, dga_ref, dgs_ref):
        d = dm_ref[...]
        sa = _sigmoid(ga_ref[...])
        ss = _sigmoid(gs_ref[...])
        da_ref[...] = (d * sa).astype(BF16)
        ds_ref[...] = (d * ss).astype(BF16)
        dga_ref[...] = (d * a_ref[...] * sa * (1.0 - sa)).astype(BF16)
        dgs_ref[...] = (d * s_ref[...] * ss * (1.0 - ss)).astype(BF16)

    own = pl.BlockSpec((tr, GATE_TC), lambda i, j: (i, j))
    wide = jax.ShapeDtypeStruct((l, D_MODEL), BF16)
    return pl.pallas_call(
        body, grid=(l // tr, GATE_NJ),
        in_specs=[pl.BlockSpec((tr, GATE_TC), lambda i, j: (i, COL_GA // 2 + j)),
                  pl.BlockSpec((tr, GATE_TC), lambda i, j: (i, COL_GS // 2 + j)), own, own, own],
        out_specs=[own, own, own, own], out_shape=[wide, wide, wide, wide], name=name,
        compiler_params=_cp(("parallel", "parallel")))(proj, proj, a, s, dm)


FF_TC = D_FF // 2
FF_NJ = 2
FF_ROWS = 128


def _conv_taps(ext, rows):
    return (ext[8:8 + rows], pltpu.roll(ext, 1, axis=0)[8:8 + rows], pltpu.roll(ext, 2, axis=0)[8:8 + rows])


def _ff_specs(tr, l):
    hb = tr // 8
    last = l // 8 - 1
    prev = lambda i: jnp.maximum(i * hb - 1, 0)
    nxt = lambda i: jnp.minimum((i + 1) * hb, last)
    return dict(
        own=pl.BlockSpec((tr, FF_TC), lambda j, i: (i, j)),
        own_next=pl.BlockSpec((8, FF_TC), lambda j, i: (nxt(i), j)),
        val=pl.BlockSpec((tr, FF_TC), lambda j, i: (i, 2 * j)),
        val_next=pl.BlockSpec((8, FF_TC), lambda j, i: (nxt(i), 2 * j)),
        gate=pl.BlockSpec((tr, FF_TC), lambda j, i: (i, 2 * j + 1)),
        gate_prev=pl.BlockSpec((8, FF_TC), lambda j, i: (prev(i), 2 * j + 1)),
        gate_next=pl.BlockSpec((8, FF_TC), lambda j, i: (nxt(i), 2 * j + 1)),
        pair=pl.BlockSpec((tr, 2 * FF_TC), lambda j, i: (i, j)),
        w=pl.BlockSpec((3, FF_TC), lambda j, i: (0, j)),
        b=pl.BlockSpec((1, FF_TC), lambda j, i: (0, j)))


def _ffn_act_fwd(up, conv_w, conv_b, *, name):
    l = up.shape[0]
    tr = min(FF_ROWS, l)

    def body(v_ref, g_ref, prev_ref, w_ref, b_ref, o_ref):
        prev = jnp.where(pl.program_id(1) == 0, 0.0, prev_ref[...])
        g0, g1, g2 = _conv_taps(jnp.concatenate([prev, g_ref[...]], axis=0), tr)
        gc = b_ref[...] + w_ref[0:1, :] * g2 + w_ref[1:2, :] * g1 + w_ref[2:3, :] * g0
        o_ref[...] = (v_ref[...] * _gelu(gc)).astype(BF16)

    sp = _ff_specs(tr, l)
    return pl.pallas_call(
        body, grid=(FF_NJ, l // tr), in_specs=[sp['val'], sp['gate'], sp['gate_prev'], sp['w'], sp['b']],
        out_specs=sp['own'], out_shape=jax.ShapeDtypeStruct((l, D_FF), BF16), name=name,
        compiler_params=_cp(("parallel", "parallel")))(up, up, up, conv_w, conv_b)


def _ffn_act_bwd(dact, up, conv_w, conv_b, *, comm=None, name):
    l = up.shape[0]
    tr = min(FF_ROWS, l)
    ni = l // tr
    te = tr + 8

    def body(d_ref, dn_ref, v_ref, vn_ref, g_ref, gp_ref, gn_ref, w_ref, b_ref, dup_ref, dw_ref, db_ref):
        i = pl.program_id(1)
        prev = jnp.where(i == 0, 0.0, gp_ref[...])
        g0, g1, g2 = _conv_taps(jnp.concatenate([prev, g_ref[...], gn_ref[...]], axis=0), te)
        w0, w1, w2 = w_ref[0:1, :], w_ref[1:2, :], w_ref[2:3, :]
        gc = b_ref[...] + w0 * g2 + w1 * g1 + w2 * g0
        d = jnp.concatenate([d_ref[...], jnp.where(i == ni - 1, 0.0, dn_ref[...])], axis=0)
        v = jnp.concatenate([v_ref[...], vn_ref[...]], axis=0)
        dgc = d * v * _gelu_grad(gc)
        ahead1 = pltpu.roll(dgc, te - 1, axis=0)[:tr]
        ahead2 = pltpu.roll(dgc, te - 2, axis=0)[:tr]
        own = dgc[:tr]
        dup_ref[:, :FF_TC] = (d_ref[...] * _gelu(gc[:tr])).astype(BF16)
        dup_ref[:, FF_TC:] = (w2 * own + w1 * ahead1 + w0 * ahead2).astype(BF16)

        @pl.when(i == 0)
        def _():
            dw_ref[...] = jnp.zeros_like(dw_ref)
            db_ref[...] = jnp.zeros_like(db_ref)

        dw_ref[0:1, :] += jnp.sum(own * g2[:tr], axis=0, keepdims=True)
        dw_ref[1:2, :] += jnp.sum(own * g1[:tr], axis=0, keepdims=True)
        dw_ref[2:3, :] += jnp.sum(own * g0[:tr], axis=0, keepdims=True)
        db_ref[...] += jnp.sum(own, axis=0, keepdims=True)

    sp = _ff_specs(tr, l)
    return _hosted_call(
        body, grid=(FF_NJ, ni),
        in_specs=[sp['own'], sp['own_next'], sp['val'], sp['val_next'], sp['gate'], sp['gate_prev'],
                  sp['gate_next'], sp['w'], sp['b']],
        out_specs=[sp['pair'], sp['w'], sp['b']],
        out_shape=[jax.ShapeDtypeStruct((l, 2 * D_FF), BF16), jax.ShapeDtypeStruct((3, D_FF), F32),
                   jax.ShapeDtypeStruct((1, D_FF), F32)],
        scratch_shapes=[], sem=("parallel", "arbitrary"), name=name,
        args=(dact, dact, up, up, up, up, up, conv_w, conv_b), comm=comm)


def _col_sum(a, *, name):
    l, n = a.shape
    tr = min(512, l)

    def body(a_ref, o_ref):
        @pl.when(pl.program_id(0) == 0)
        def _():
            o_ref[...] = jnp.zeros_like(o_ref)

        o_ref[...] += jnp.sum(a_ref[...].astype(F32), axis=0, keepdims=True)

    return pl.pallas_call(
        body, grid=(l // tr,), in_specs=[pl.BlockSpec((tr, n), lambda i: (i, 0))],
        out_specs=pl.BlockSpec((1, n), lambda i: (0, 0)), out_shape=jax.ShapeDtypeStruct((1, n), F32),
        name=name, compiler_params=_cp(("arbitrary",)))(a)


def _local_step(x, target, wts, small, shards=None):
    wts = dict(wts)
    grads, recvs, sgr = {}, {}, {}
    inter = lambda keys: [k == 'w_up_t' for k in keys]
    none = lambda keys: None
    gather = (lambda keys: _GatherPlan([shards[k] for k in keys], inter(keys))) if shards is not None else none
    scatter = (lambda keys: _ScatterPlan([grads[k] for k in keys], inter(keys))) if shards is not None else none

    mm = _matmul

    def take(res, plan, keys, store):
        outs, couts = res
        if plan is not None:
            store.update(zip(keys, couts))
        return outs

    def mm_host(keys, make_plan, store, *args, **kw):
        plan = make_plan(keys)
        if plan is None:
            return _matmul(*args, **kw)
        return take(_matmul(*args, comm=plan, **kw), plan, keys, store)

    col = lambda t: t.reshape(SSM_GROUPS * SSM_STATE, 1)
    a_re, a_im = col(small['ssm_a_re']), col(small['ssm_a_im'])
    log_dt = jnp.repeat(small['ssm_log_dt'].reshape(SSM_GROUPS), SSM_STATE).reshape(-1, 1)
    b_re = small['ssm_b_re'].reshape(SSM_GROUPS * SSM_STATE, SSM_GROUP)
    b_im = small['ssm_b_im'].reshape(SSM_GROUPS * SSM_STATE, SSM_GROUP)
    ab_re, ab_im, bb_re, bb_im = _ssm_disc_fwd(a_re, a_im, log_dt, b_re, b_im, name="ssm_disc_fwd")
    ab = jnp.stack([ab_re.reshape(SSM_SPLIT, SSM_X_BLK), ab_im.reshape(SSM_SPLIT, SSM_X_BLK)], axis=1)
    to_bd = lambda t: _block_diag(t.reshape(SSM_SPLIT, 8, SSM_STATE, SSM_GROUP).transpose(0, 1, 3, 2))
    bd = jnp.concatenate([to_bd(bb_re), to_bd(bb_im)], axis=2)
    c_re = small['ssm_c_re'].reshape(SSM_SPLIT, 8, SSM_GROUP, SSM_STATE)
    c_im = small['ssm_c_im'].reshape(SSM_SPLIT, 8, SSM_GROUP, SSM_STATE)
    cdt = jnp.concatenate([_block_diag(c_re), -_block_diag(c_im)], axis=2)
    bd_b, cdt_b = bd.astype(BF16), cdt.astype(BF16)
    bdt_b, cd_b = bd_b.transpose(0, 2, 1), cdt_b.transpose(0, 2, 1)
    dskip = small['ssm_d'].reshape(1, SSM_WIDTH)

    sinks = small['attn_sinks'].reshape(N_Q_HEADS)
    plan = gather(['w_in_t'])
    h1, = take(_rms_fwd(x, small['attn_norm_g'], comm=plan, name="rms1_fwd"), plan, ['w_in_t'], wts)
    proj = mm_host(['w_glu_t', 'w_ba_t', 'w_bs_t', 'w_out', 'w_down'], gather, wts,
                   h1, wts['w_in_t'], tb=True, tm=512, tn=2944, tk=2048, inner='m', out_dtype=F32,
                   bias=small['b_in'], name="mm_in")
    attn = _attn_fwd(proj, sinks, name="attn_fwd")
    plan = gather(['w_up_t'])
    y, gy, xs = take(_ssm_fwd(proj, ab, bd_b, cd_b, dskip, comm=plan, name="ssm_fwd"), plan, ['w_up_t'], wts)
    z = mm(gy, wts['w_glu_t'], tb=True, tm=1024, tn=1024, tk=512, out_dtype=F32,
           bias=small['b_glu'], name="mm_glu")
    ssm = _glu_fwd(z, name="glu_fwd")
    a_br = mm(attn, wts['w_ba_t'], tb=True, tm=1024, tn=1024, tk=1024, out_dtype=F32, name="mm_ba")
    s_br = mm(ssm, wts['w_bs_t'], tb=True, tm=1024, tn=1024, tk=512, out_dtype=F32, name="mm_bs")
    merged = _merge_fwd(proj, a_br, s_br, name="merge_fwd")
    x2 = mm(merged, wts['w_out'], tm=1024, tn=1024, tk=2048, inner='m', out_dtype=F32, res=x, name="mm_out")
    h2, = take(_rms_fwd(x2, small['ffn_norm_g'], name="rms2_fwd"), None, [], wts)
    up = mm(h2, wts['w_up_t'], tb=True, tm=1024, tn=1024, tk=2048, out_dtype=F32, name="mm_up")
    conv_w, conv_b = small['conv_w'], small['conv_b']
    act = _ffn_act_fwd(up, conv_w, conv_b, name="ffn_act_fwd")
    x3 = mm(act, wts['w_down'], tm=1024, tn=1024, tk=2816, out_dtype=F32, res=x2, name="mm_down")
    dx3, dx3b, d_g3, loss = _final_loss(x3, small['final_norm_g'].reshape(1, D_MODEL), target, name="final_loss")

    sgr['final_norm_g'] = d_g3.reshape(D_MODEL)
    dact = mm(dx3b, wts['w_down'], tb=True, tm=512, tn=2816, tk=2048, inner='m', out_dtype=F32, name="mm_dact")
    grads['w_down'] = mm(act, dx3b, ta=True, tm=512, tn=1024, tk=2048, out_dtype=BF16, name="mm_dw_down")
    plan = scatter(['w_down'])
    dup, sgr['conv_w'], sgr['conv_b'] = take(
        _ffn_act_bwd(dact, up, conv_w, conv_b, comm=plan, name="ffn_act_bwd"), plan, ['w_down'], recvs)
    grads['w_up_t'] = mm(dup, h2, ta=True, tm=1024, tn=1024, tk=2048, out_dtype=BF16, name="mm_dw_up")
    dh2 = mm_host(['w_up_t'], scatter, recvs,
                  dup, wts['w_up_t'], tm=1024, tn=1024, tk=2816, out_dtype=F32, name="mm_dh2")
    dx2, dx2b, sgr['ffn_norm_g'] = _rms_bwd(dh2, x2, small['ffn_norm_g'], dx3, with_bf16=True, name="rms2_bwd")

    dm = mm(dx2b, wts['w_out'], tb=True, tm=1024, tn=1024, tk=2048, inner='m', out_dtype=F32, name="mm_dmerged")
    grads['w_out'] = mm(merged, dx2b, ta=True, tm=1024, tn=1024, tk=2048, out_dtype=BF16, name="mm_dw_out")
    d_a, d_s, dga, dgs = _merge_bwd(proj, a_br, s_br, dm, name="merge_bwd")
    dattn = mm(d_a, wts['w_ba_t'], tm=1024, tn=1024, tk=2048, inner='m', out_dtype=BF16, name="mm_dattn")
    grads['w_ba_t'] = mm(d_a, attn, ta=True, tm=1024, tn=1024, tk=2048, out_dtype=BF16, name="mm_dw_ba")
    dssm = mm(d_s, wts['w_bs_t'], tm=1024, tn=512, tk=2048, inner='m', out_dtype=F32, name="mm_dssm")
    grads['w_bs_t'] = mm(d_s, ssm, ta=True, tm=1024, tn=512, tk=2048, out_dtype=BF16, name="mm_dw_bs")
    dz, sgr['b_glu'] = _glu_bwd(dssm, z, name="glu_bwd")
    grads['w_glu_t'] = mm(dz, gy, ta=True, tm=1024, tn=512, tk=2048, out_dtype=BF16, name="mm_dw_glu")
    dgy = mm(dz, wts['w_glu_t'], tm=1024, tn=512, tk=1024, inner='m', out_dtype=F32, name="mm_dgy")
    du, d_bd, d_cd, d_ab, sgr['ssm_d'] = _ssm_bwd(proj, y, dgy, xs, ab, bdt_b, cdt_b, dskip, name="ssm_bwd")
    keys = ['w_out', 'w_ba_t', 'w_bs_t', 'w_glu_t']
    plan = scatter(keys)
    dq, dkc, dkp, dvc, dvp, dsink = take(_attn_bwd(proj, sinks, dattn, comm=plan, name="attn_bwd"), plan, keys, recvs)
    dkv = _kv_grad_merge(dkc, dkp, dvc, dvp, name="kv_grad_merge")
    sgr['attn_sinks'] = dsink[:, :N_Q_HEADS]
    dproj = jnp.concatenate([dq, dkv, du, dga, dgs], axis=1)
    sgr['b_in'] = _col_sum(dproj, name="col_sum_dproj")
    grads['w_in_t'] = mm(dproj, h1, ta=True, tm=2944, tn=1024, tk=1024, out_dtype=BF16, name="mm_dw_in")
    dh1 = mm_host(['w_in_t'], scatter, recvs,
                  dproj, wts['w_in_t'], tm=1024, tn=1024, tk=2944, out_dtype=F32, name="mm_dh1")
    grad_x, sgr['attn_norm_g'] = _rms_bwd(dh1, x, small['attn_norm_g'], dx2, with_bf16=False, name="rms1_bwd")

    from_bd = lambda t: _block_diag_take(t, SSM_GROUP, SSM_STATE).transpose(0, 1, 3, 2).reshape(
        SSM_GROUPS * SSM_STATE, SSM_GROUP)
    d_bb_re = from_bd(d_bd[:, :, :SSM_X_BLK])
    d_bb_im = from_bd(d_bd[:, :, SSM_X_BLK:])
    d_cdt = d_cd.transpose(0, 2, 1)
    shape_c = (1, SSM_GROUPS, SSM_GROUP, SSM_STATE)
    sgr['ssm_c_re'] = _block_diag_take(d_cdt[:, :, :SSM_X_BLK], SSM_GROUP, SSM_STATE).reshape(shape_c)
    sgr['ssm_c_im'] = -_block_diag_take(d_cdt[:, :, SSM_X_BLK:], SSM_GROUP, SSM_STATE).reshape(shape_c)
    d_a_re, d_a_im, d_ldt, d_b_re, d_b_im = _ssm_disc_bwd(
        a_re, a_im, log_dt, b_re, b_im, d_ab[:, 0, :].reshape(-1, 1), d_ab[:, 1, :].reshape(-1, 1),
        d_bb_re, d_bb_im, name="ssm_disc_bwd")
    sgr['ssm_a_re'] = d_a_re.reshape(1, SSM_GROUPS, SSM_STATE)
    sgr['ssm_a_im'] = d_a_im.reshape(1, SSM_GROUPS, SSM_STATE)
    sgr['ssm_log_dt'] = d_ldt.reshape(SSM_GROUPS, SSM_STATE).sum(axis=1).reshape(1, SSM_GROUPS)
    sgr['ssm_b_re'] = d_b_re.reshape(1, SSM_GROUPS, SSM_STATE, SSM_GROUP)
    sgr['ssm_b_im'] = d_b_im.reshape(1, SSM_GROUPS, SSM_STATE, SSM_GROUP)
    return loss, grad_x, grads, recvs, sgr


def _swap_cores(arrs, *, name):
    n = len(arrs)

    def body(*refs):
        ins, outs = refs[:n], refs[n:2 * n]
        send_sems, recv_sems = refs[2 * n:]
        x, y, c = _place()
        copies = []
        for i in range(n):
            cp = pltpu.make_async_remote_copy(
                src_ref=ins[i], dst_ref=outs[i], send_sem=send_sems.at[i], recv_sem=recv_sems.at[i],
                device_id=(x, y, 1 - c), device_id_type=MESH)
            cp.start()
            copies.append(cp)
        for cp in copies:
            cp.wait()

    return pl.pallas_call(
        body, in_specs=[ANY] * n, out_specs=[ANY] * n,
        out_shape=[jax.ShapeDtypeStruct(a.shape, a.dtype) for a in arrs],
        scratch_shapes=[pltpu.SemaphoreType.DMA((n,)), pltpu.SemaphoreType.DMA((n,))],
        name=name)(*arrs)


def _all_reduce_small(buf, *, name):
    r = buf.shape[0]

    def body(in_ref, out_ref, slots, send_sems, recv_sems):
        x, y, c = _place()
        me = 4 * x + 2 * y + c
        slots[pl.ds(me, 1)] = in_ref[...][None]
        copies = []
        for k in range(N_DEV - 1):
            bx, by, bc = ((k + 1) >> 2) & 1, ((k + 1) >> 1) & 1, (k + 1) & 1
            peer = (1 - x if bx else x, 1 - y if by else y, 1 - c if bc else c)
            cp = pltpu.make_async_remote_copy(
                src_ref=in_ref, dst_ref=slots.at[me], send_sem=send_sems.at[k], recv_sem=recv_sems.at[k],
                device_id=peer, device_id_type=MESH)
            cp.start()
            copies.append(cp)
        for cp in copies:
            cp.wait()
        acc = slots[0]
        for d in range(1, N_DEV):
            acc = acc + slots[d]
        out_ref[...] = acc

    vm = pl.BlockSpec(memory_space=pltpu.VMEM)
    return pl.pallas_call(
        body, in_specs=[vm], out_specs=vm, out_shape=jax.ShapeDtypeStruct((r, 128), F32),
        scratch_shapes=[pltpu.VMEM((N_DEV, r, 128), F32), pltpu.SemaphoreType.DMA((N_DEV - 1,)),
                        pltpu.SemaphoreType.DMA((N_DEV - 1,))],
        name=name)(buf)


def _pack(arrs):
    flat = jnp.concatenate([a.reshape(-1).astype(F32) for a in arrs])
    pad = (-flat.shape[0]) % 1024
    return jnp.pad(flat, (0, pad)).reshape(-1, 128)


def _unpack(buf, shapes):
    flat = buf.reshape(-1)
    out, pos = [], 0
    for s in shapes:
        size = math.prod(s)
        out.append(flat[pos:pos + size].reshape(s))
        pos += size
    return out


TILE_ELEMS = 256 * 1024


def _tile_rows(r, c):
    if r * c <= TILE_ELEMS:
        return r
    for tr in range(TILE_ELEMS // c // 16 * 16, 0, -16):
        if r % tr == 0:
            return tr
    raise ValueError((r, c))


def _sum4(own, recv, *, name):
    r, c = own.shape
    tr = _tile_rows(r, c)

    def body(o_ref, r_ref, out_ref):
        acc = o_ref[...].astype(F32)
        for k in range(3):
            acc = acc + r_ref[k].astype(F32)
        out_ref[...] = acc

    return pl.pallas_call(
        body, grid=(r // tr,),
        in_specs=[pl.BlockSpec((tr, c), lambda i: (i, 0)), pl.BlockSpec((3, tr, c), lambda i: (0, i, 0))],
        out_specs=pl.BlockSpec((tr, c), lambda i: (i, 0)), out_shape=jax.ShapeDtypeStruct((r, c), F32),
        name=name, compiler_params=_cp(("parallel",)))(own, recv)


def _adamw(w, ga, gb, m, v, *, name):
    r, c = w.shape
    tr = _tile_rows(r, c)
    bc1 = 1.0 - ADAM_B1 ** ADAM_STEP
    bc2 = 1.0 - ADAM_B2 ** ADAM_STEP
    two = gb is not None

    def body(*refs):
        w_ref, ga_ref = refs[0], refs[1]
        pos = 2
        g = ga_ref[...]
        if two:
            g = g + refs[pos][...]
            pos += 1
        m_ref, v_ref, g_out, d_out, m_out, v_out = refs[pos:pos + 6]
        mn = ADAM_B1 * m_ref[...] + (1.0 - ADAM_B1) * g
        vn = ADAM_B2 * v_ref[...] + (1.0 - ADAM_B2) * (g * g)
        m_hat = mn / bc1
        v_hat = vn / bc2
        g_out[...] = g
        d_out[...] = -ADAM_LR * (m_hat / (jnp.sqrt(v_hat) + ADAM_EPS) + ADAM_WD * w_ref[...])
        m_out[...] = mn
        v_out[...] = vn

    spec = pl.BlockSpec((tr, c), lambda i: (i, 0))
    args = [w, ga] + ([gb] if two else []) + [m, v]
    shp = jax.ShapeDtypeStruct((r, c), F32)
    return pl.pallas_call(
        body, grid=(r // tr,), in_specs=[spec] * len(args), out_specs=[spec] * 4,
        out_shape=[shp] * 4, name=name, compiler_params=_cp(("parallel",)))(*args)


BIG = ['w_in', 'w_glu', 'w_branch_attn', 'w_branch_ssm', 'w_out', 'w_up', 'w_down']
BIG_KEY = {'w_in': 'w_in_t', 'w_glu': 'w_glu_t', 'w_branch_attn': 'w_ba_t', 'w_branch_ssm': 'w_bs_t',
           'w_out': 'w_out', 'w_up': 'w_up_t', 'w_down': 'w_down'}
COL_SHARDED = {'w_in', 'w_glu', 'w_branch_attn', 'w_branch_ssm', 'w_up'}
SMALL = ['attn_norm_g', 'b_in', 'attn_sinks', 'ssm_a_re', 'ssm_a_im', 'ssm_log_dt', 'ssm_b_re', 'ssm_b_im',
         'ssm_c_re', 'ssm_c_im', 'ssm_d', 'b_glu', 'ffn_norm_g', 'conv_b', 'final_norm_g']
WEIGHTS = ['attn_norm_g', 'w_in', 'b_in', 'attn_sinks', 'ssm_a_re', 'ssm_a_im', 'ssm_log_dt', 'ssm_b_re',
           'ssm_b_im', 'ssm_c_re', 'ssm_c_im', 'ssm_d', 'w_glu', 'b_glu', 'w_branch_attn', 'w_branch_ssm',
           'w_out', 'ffn_norm_g', 'w_up', 'conv_w', 'conv_b', 'w_down', 'final_norm_g']


def _shard_2d(name, t):
    t = t[0]
    return t.T if name in COL_SHARDED else t


def _unshard_2d(name, t):
    return (t.T if name in COL_SHARDED else t)[None]


def kernel(x, attn_norm_g, w_in, b_in, attn_sinks, ssm_a_re, ssm_a_im, ssm_log_dt, ssm_b_re, ssm_b_im, ssm_c_re, ssm_c_im, ssm_d, w_glu, b_glu, w_branch_attn, w_branch_ssm, w_out, ffn_norm_g, w_up, conv_w, conv_b, w_down, final_norm_g, loss_target, m_attn_norm_g, m_w_in, m_b_in, m_attn_sinks, m_ssm_a_re, m_ssm_a_im, m_ssm_log_dt, m_ssm_b_re, m_ssm_b_im, m_ssm_c_re, m_ssm_c_im, m_ssm_d, m_w_glu, m_b_glu, m_w_branch_attn, m_w_branch_ssm, m_w_out, m_ffn_norm_g, m_w_up, m_conv_w, m_conv_b, m_w_down, m_final_norm_g, v_attn_norm_g, v_w_in, v_b_in, v_attn_sinks, v_ssm_a_re, v_ssm_a_im, v_ssm_log_dt, v_ssm_b_re, v_ssm_b_im, v_ssm_c_re, v_ssm_c_im, v_ssm_d, v_w_glu, v_b_glu, v_w_branch_attn, v_w_branch_ssm, v_w_out, v_ffn_norm_g, v_w_up, v_conv_w, v_conv_b, v_w_down, v_final_norm_g):
    args = dict(locals())
    w = {n: args[n] for n in WEIGHTS}
    m = {n: args['m_' + n] for n in WEIGHTS}
    v = {n: args['v_' + n] for n in WEIGHTS}
    xi, yi, ci = _place()
    blk = 2 * xi + yi

    shards = {BIG_KEY[n]: _shard_2d(n, w[n]).astype(BF16) for n in BIG}
    cw_cols = w['conv_w'].shape[2]
    cw_place = lax.dynamic_update_slice(jnp.zeros((3, D_FF), F32), w['conv_w'][0] * (ci == 0).astype(F32),
                                        (0, blk * cw_cols))
    conv_w_full = _unpack(_all_reduce_small(_pack([cw_place]), name="gather_conv_w"), [(3, D_FF)])[0]

    small = {n: w[n] for n in SMALL}
    small['conv_w'] = conv_w_full
    loss_part, grad_x, grads, recvs, sgr = _local_step(x[0], loss_target[0], {}, small, shards)

    halves = []
    for n in BIG:
        full, recv = grads[BIG_KEY[n]], recvs[BIG_KEY[n]]
        r = full.shape[0] // N_CHIPS
        own = lax.dynamic_slice_in_dim(full, _block_pos(xi, yi, n == 'w_up') * r, r, axis=0)
        halves.append(_sum4(own, recv, name="sum4_" + n))
    others = _swap_cores(halves, name="swap_cores")
    out = {}
    for n, mine, other in zip(BIG, halves, others):
        res = _adamw(_shard_2d(n, w[n]), mine, other, _shard_2d(n, m[n]), _shard_2d(n, v[n]), name="adamw_" + n)
        out[n] = [_unshard_2d(n, t) for t in res]

    names = SMALL + ['conv_w']
    shapes = [w[n].shape for n in SMALL] + [(3, D_FF)]
    packed = _pack([sgr[n] for n in names] + [loss_part])
    summed = _unpack(_all_reduce_small(packed, name="all_reduce_small"), shapes + [(1, 1)])
    loss = summed[-1].reshape(())
    sg = dict(zip(names, summed[:-1]))
    sg['conv_w'] = lax.dynamic_slice_in_dim(sg['conv_w'], blk * cw_cols, cw_cols, axis=1)[None]
    res = _adamw(_pack([w[n] for n in names]), _pack([sg[n] for n in names]), None,
                 _pack([m[n] for n in names]), _pack([v[n] for n in names]), name="adamw_small")
    ushapes = [w[n].shape for n in names]
    unpacked = [_unpack(t, ushapes) for t in res]
    for i, n in enumerate(names):
        out[n] = [u[i] for u in unpacked]

    return (loss, grad_x[None], *[out[n][0] for n in WEIGHTS], *[out[n][1] for n in WEIGHTS],
            *[out[n][2] for n in WEIGHTS], *[out[n][3] for n in WEIGHTS])
```

```python
import functools
import math

import jax
import jax.numpy as jnp
from jax import lax
from jax.experimental import pallas as pl
from jax.experimental.pallas import tpu as pltpu

F32 = jnp.float32
BF16 = jnp.bfloat16

D_MODEL = 2048
N_Q_HEADS = 16
HEAD_DIM = 64
ATTN_WIDTH = 1024
KV_WIDTH = 128
BLOCK = 128
SSM_WIDTH = 512
SSM_GROUPS = 32
SSM_GROUP = 16
SSM_STATE = 64
D_FF = 5632
IN_COLS = 5888
RMS_EPS = 1e-6
NEG_BIG = -1e30
N_CHIPS = 4
N_DEV = 8

COL_K = 8
COL_V = 9
COL_U = 10
COL_GA = 14
COL_GS = 30

SSM_SPLIT = 4
SSM_U_BLK = 128
SSM_X_BLK = 512
SSM_CHUNK = 256

ADAM_LR = 0.001
ADAM_B1 = 0.9
ADAM_B2 = 0.999
ADAM_EPS = 1e-08
ADAM_WD = 0.01
ADAM_STEP = 10

VMEM_LIMIT_BYTES = 56 * 1024 * 1024
INV_SQRT2 = 1.0 / math.sqrt(2.0)
INV_SQRT2PI = 1.0 / math.sqrt(2.0 * math.pi)
MESH = pl.DeviceIdType.MESH
ANY = pl.BlockSpec(memory_space=pl.ANY)


def _cp(sem):
    return pltpu.CompilerParams(dimension_semantics=sem, vmem_limit_bytes=VMEM_LIMIT_BYTES)


def _gelu(x):
    return 0.5 * x * (1.0 + lax.erf(x * INV_SQRT2))


def _gelu_grad(x):
    return 0.5 * (1.0 + lax.erf(x * INV_SQRT2)) + x * jnp.exp(-0.5 * x * x) * INV_SQRT2PI


def _sigmoid(x):
    return 1.0 / (1.0 + jnp.exp(-x))


def _place():
    return lax.axis_index("x"), lax.axis_index("y"), lax.axis_index("c")


def _other_chips(x, y):
    return [(1 - x, y), (x, 1 - y), (1 - x, 1 - y)]


def _block_pos(x, y, interleaved):
    return x + 2 * y if interleaved else 2 * x + y


class _GatherPlan:
    def __init__(self, shards, interleaved):
        self.arrays = list(shards)
        self.interleaved = list(interleaved)
        n = len(shards)
        self.out_shape = [jax.ShapeDtypeStruct((N_CHIPS * s.shape[0], s.shape[1]), s.dtype) for s in shards]
        self.scratch = [pltpu.SemaphoreType.DMA((6 * n,)), pltpu.SemaphoreType.DMA((6 * n,)),
                        pltpu.SemaphoreType.DMA((n,))]

    def _copies(self, kind, ins, outs, sems):
        send, recv, local = sems
        n = len(self.arrays)
        x, y, c = _place()
        copies = []
        for i in range(n):
            r = self.arrays[i].shape[0]
            h = r // 2
            blk = _block_pos(x, y, self.interleaved[i])
            if kind == 'mine':
                copies.append(pltpu.make_async_copy(ins[i], outs[i].at[pl.ds(blk * r, r), :], local.at[i]))
                continue
            for k, (px, py) in enumerate(_other_chips(x, y)):
                theirs = _block_pos(px, py, self.interleaved[i]) * r
                if kind in ('ici_out', 'ici_in'):
                    route = dict(send_sem=send.at[3 * i + k], recv_sem=recv.at[3 * i + k],
                                 device_id=(px, py, c), device_id_type=MESH)
                else:
                    route = dict(send_sem=send.at[3 * (n + i) + k], recv_sem=recv.at[3 * (n + i) + k],
                                 device_id=(x, y, 1 - c), device_id_type=MESH)
                if kind == 'ici_out':
                    src, dst = ins[i].at[pl.ds(c * h, h), :], outs[i].at[pl.ds(blk * r + c * h, h), :]
                elif kind == 'd2d_in':
                    src = dst = outs[i].at[pl.ds(theirs + (1 - c) * h, h), :]
                else:
                    src = dst = outs[i].at[pl.ds(theirs + c * h, h), :]
                copies.append(pltpu.make_async_remote_copy(src_ref=src, dst_ref=dst, **route))
        return copies

    def start(self, ins, outs, sems):
        for cp in self._copies('mine', ins, outs, sems) + self._copies('ici_out', ins, outs, sems):
            cp.start()

    def middle(self, ins, outs, sems):
        for arrived, onward in zip(self._copies('ici_in', ins, outs, sems), self._copies('d2d_out', ins, outs, sems)):
            arrived.wait_recv()
            onward.start()

    def finish(self, ins, outs, sems):
        for cp in self._copies('d2d_in', ins, outs, sems):
            cp.wait_recv()
        for cp in self._copies('ici_out', ins, outs, sems) + self._copies('d2d_out', ins, outs, sems):
            cp.wait_send()
        for cp in self._copies('mine', ins, outs, sems):
            cp.wait()


class _ScatterPlan:
    def __init__(self, fulls, interleaved):
        self.arrays = list(fulls)
        self.interleaved = list(interleaved)
        n = len(fulls)
        self.out_shape = [jax.ShapeDtypeStruct((3, f.shape[0] // N_CHIPS, f.shape[1]), f.dtype) for f in fulls]
        self.scratch = [pltpu.SemaphoreType.DMA((3 * n,)), pltpu.SemaphoreType.DMA((3 * n,))]

    def _copies(self, ins, outs, sems):
        send, recv = sems
        x, y, c = _place()
        copies = []
        for i in range(len(self.arrays)):
            r = self.arrays[i].shape[0] // N_CHIPS
            for k, (px, py) in enumerate(_other_chips(x, y)):
                copies.append(pltpu.make_async_remote_copy(
                    src_ref=ins[i].at[pl.ds(_block_pos(px, py, self.interleaved[i]) * r, r), :], dst_ref=outs[i].at[k],
                    send_sem=send.at[3 * i + k], recv_sem=recv.at[3 * i + k],
                    device_id=(px, py, c), device_id_type=MESH))
        return copies

    def start(self, ins, outs, sems):
        for cp in self._copies(ins, outs, sems):
            cp.start()

    def middle(self, ins, outs, sems):
        pass

    def finish(self, ins, outs, sems):
        for cp in self._copies(ins, outs, sems):
            cp.wait()


def _hosted_call(body, *, grid, in_specs, out_specs, out_shape, scratch_shapes, sem, name, args, comm=None,
                 aliases=None):
    aliases = aliases or {}
    if comm is None:
        outs = pl.pallas_call(body, grid=grid, in_specs=in_specs, out_specs=out_specs, out_shape=out_shape,
                              scratch_shapes=scratch_shapes, name=name, input_output_aliases=aliases,
                              compiler_params=_cp(sem))(*args)
        return outs, None
    n_in, n_out, n_scr = len(in_specs), len(out_specs), len(scratch_shapes)
    nc, ns = len(comm.arrays), len(comm.scratch)
    total = math.prod(grid)
    mid = min(total - 1, (3 * total) // 4)

    def wrapped(*refs):
        pos = 0
        ins = refs[pos:pos + n_in]; pos += n_in
        cins = refs[pos:pos + nc]; pos += nc
        outs = refs[pos:pos + n_out]; pos += n_out
        couts = refs[pos:pos + nc]; pos += nc
        scr = refs[pos:pos + n_scr]; pos += n_scr
        sems = refs[pos:pos + ns]
        step = 0
        for ax, g in enumerate(grid):
            step = step * g + pl.program_id(ax)

        @pl.when(step == 0)
        def _():
            comm.start(cins, couts, sems)

        body(*ins, *outs, *scr)

        @pl.when(step == mid)
        def _():
            comm.middle(cins, couts, sems)

        @pl.when(step == total - 1)
        def _():
            comm.finish(cins, couts, sems)

    res = pl.pallas_call(
        wrapped, grid=grid, in_specs=list(in_specs) + [ANY] * nc, out_specs=list(out_specs) + [ANY] * nc,
        out_shape=list(out_shape) + list(comm.out_shape), scratch_shapes=list(scratch_shapes) + list(comm.scratch),
        name=name, input_output_aliases=aliases,
        compiler_params=_cp(("arbitrary",) * len(grid)))(*args, *comm.arrays)
    return res[:n_out], res[n_out:]


def _matmul(a, b, *, ta=False, tb=False, tm, tn, tk, out_dtype, bias=None, res=None, inner='n', comm=None,
            name):
    if ta:
        kdim, m = a.shape
    else:
        m, kdim = a.shape
    if tb:
        n, k2 = b.shape
    else:
        k2, n = b.shape
    assert kdim == k2, (a.shape, b.shape)
    tm, tn, tk = min(tm, m), min(tn, n), min(tk, kdim)
    assert m % tm == 0 and n % tn == 0 and kdim % tk == 0, (name, m, n, kdim, tm, tn, tk)
    nk = kdim // tk
    dn = (((0 if ta else 1,), (1 if tb else 0,)), ((), ()))

    def body(*refs):
        a_ref, b_ref = refs[0], refs[1]
        pos = 2
        bias_ref = res_ref = None
        if bias is not None:
            bias_ref = refs[pos]
            pos += 1
        if res is not None:
            res_ref = refs[pos]
            pos += 1
        o_ref = refs[pos]

        def finish(r):
            if bias_ref is not None:
                r = r + bias_ref[...]
            if res_ref is not None:
                r = r + res_ref[...]
            o_ref[...] = r.astype(out_dtype)

        prod = lax.dot_general(a_ref[...].astype(BF16), b_ref[...].astype(BF16), dn, preferred_element_type=F32)
        if nk == 1:
            finish(prod)
            return
        acc_ref = refs[pos + 1]
        k = pl.program_id(2)

        @pl.when(k == 0)
        def _():
            acc_ref[...] = prod

        @pl.when(k > 0)
        def _():
            acc_ref[...] += prod

        @pl.when(k == nk - 1)
        def _():
            finish(acc_ref[...])

    if inner == 'n':
        grid = (m // tm, n // tn, nk)
        mi = lambda g0, g1: g0
        ni = lambda g0, g1: g1
    else:
        grid = (n // tn, m // tm, nk)
        mi = lambda g0, g1: g1
        ni = lambda g0, g1: g0
    a_spec = (pl.BlockSpec((tk, tm), lambda g0, g1, k: (k, mi(g0, g1))) if ta
              else pl.BlockSpec((tm, tk), lambda g0, g1, k: (mi(g0, g1), k)))
    b_spec = (pl.BlockSpec((tn, tk), lambda g0, g1, k: (ni(g0, g1), k)) if tb
              else pl.BlockSpec((tk, tn), lambda g0, g1, k: (k, ni(g0, g1))))
    in_specs = [a_spec, b_spec]
    args = [a, b]
    if bias is not None:
        in_specs.append(pl.BlockSpec((1, tn), lambda g0, g1, k: (0, ni(g0, g1))))
        args.append(bias)
    if res is not None:
        in_specs.append(pl.BlockSpec((tm, tn), lambda g0, g1, k: (mi(g0, g1), ni(g0, g1))))
        args.append(res)
    outs, couts = _hosted_call(
        body, grid=grid, in_specs=in_specs,
        out_specs=[pl.BlockSpec((tm, tn), lambda g0, g1, k: (mi(g0, g1), ni(g0, g1)))],
        out_shape=[jax.ShapeDtypeStruct((m, n), out_dtype)],
        scratch_shapes=[pltpu.VMEM((tm, tn), F32)] if nk > 1 else [],
        sem=("parallel", "parallel", "arbitrary"), name=name, args=args, comm=comm)
    return outs[0] if comm is None else (outs[0], couts)


def _rms_fwd(x, g, *, comm=None, name):
    l, d = x.shape
    tr = min(256, l)

    def body(x_ref, g_ref, h_ref):
        xf = x_ref[...]
        r = lax.rsqrt(jnp.mean(xf * xf, axis=-1, keepdims=True) + RMS_EPS)
        h_ref[...] = ((xf * r) * g_ref[...]).astype(BF16)

    row = pl.BlockSpec((tr, d), lambda i: (i, 0))
    return _hosted_call(
        body, grid=(l // tr,), in_specs=[row, pl.BlockSpec((1, d), lambda i: (0, 0))],
        out_specs=[row], out_shape=[jax.ShapeDtypeStruct((l, d), BF16)], scratch_shapes=[],
        sem=("parallel",), name=name, args=(x, g), comm=comm)


def _rms_bwd(dy, x, g, dres, *, with_bf16, name):
    l, d = x.shape
    tr = min(256, l)

    def body(dy_ref, x_ref, g_ref, dres_ref, *outs):
        dx_ref = outs[0]
        dg_ref = outs[-1]
        xf = x_ref[...]
        r = lax.rsqrt(jnp.mean(xf * xf, axis=-1, keepdims=True) + RMS_EPS)
        xhat = xf * r
        dyv = dy_ref[...]
        dxh = dyv * g_ref[...]
        dx = r * (dxh - xhat * jnp.mean(dxh * xhat, axis=-1, keepdims=True)) + dres_ref[...]
        dx_ref[...] = dx
        if with_bf16:
            outs[1][...] = dx.astype(BF16)

        @pl.when(pl.program_id(0) == 0)
        def _():
            dg_ref[...] = jnp.zeros_like(dg_ref)

        dg_ref[...] += jnp.sum(dyv * xhat, axis=0, keepdims=True)

    row = pl.BlockSpec((tr, d), lambda i: (i, 0))
    vec = pl.BlockSpec((1, d), lambda i: (0, 0))
    out_specs = [row] + ([row] if with_bf16 else []) + [vec]
    out_shape = ([jax.ShapeDtypeStruct((l, d), F32)]
                 + ([jax.ShapeDtypeStruct((l, d), BF16)] if with_bf16 else [])
                 + [jax.ShapeDtypeStruct((1, d), F32)])
    return pl.pallas_call(
        body, grid=(l // tr,), in_specs=[row, row, vec, row], out_specs=out_specs,
        out_shape=out_shape, name=name, compiler_params=_cp(("arbitrary",)))(dy, x, g, dres)


def _final_loss(x3, g, target, *, name):
    l, d = x3.shape
    tr = min(256, l)

    def body(x_ref, g_ref, t_ref, dx_ref, dxb_ref, dg_ref, loss_ref):
        xf = x_ref[...]
        gv = g_ref[...]
        r = lax.rsqrt(jnp.mean(xf * xf, axis=-1, keepdims=True) + RMS_EPS)
        xhat = xf * r
        diff = xhat * gv - t_ref[...]
        dout = diff * (1.0 / d)
        dxh = dout * gv
        dx = r * (dxh - xhat * jnp.mean(dxh * xhat, axis=-1, keepdims=True))
        dx_ref[...] = dx
        dxb_ref[...] = dx.astype(BF16)

        @pl.when(pl.program_id(0) == 0)
        def _():
            dg_ref[...] = jnp.zeros_like(dg_ref)
            loss_ref[...] = jnp.zeros_like(loss_ref)

        dg_ref[...] += jnp.sum(dout * xhat, axis=0, keepdims=True)
        part = jnp.sum(jnp.mean(diff * diff, axis=-1, keepdims=True), axis=0, keepdims=True)
        loss_ref[...] += 0.5 * part

    row = pl.BlockSpec((tr, d), lambda i: (i, 0))
    vec = pl.BlockSpec((1, d), lambda i: (0, 0))
    return pl.pallas_call(
        body, grid=(l // tr,), in_specs=[row, vec, row],
        out_specs=[row, row, vec, pl.BlockSpec((1, 1), lambda i: (0, 0))],
        out_shape=[jax.ShapeDtypeStruct((l, d), F32), jax.ShapeDtypeStruct((l, d), BF16),
                   jax.ShapeDtypeStruct((1, d), F32), jax.ShapeDtypeStruct((1, 1), F32)],
        name=name, compiler_params=_cp(("arbitrary",)))(x3, g, target)


Q_PER_KV = 8
GROUP_ROWS = Q_PER_KV * BLOCK


def _attn_masks(n, rows=GROUP_ROWS):
    q_idx = lax.broadcasted_iota(jnp.int32, (rows, 2 * BLOCK), 0) & (BLOCK - 1)
    s_idx = lax.broadcasted_iota(jnp.int32, (rows, 2 * BLOCK), 1)
    dist = q_idx + BLOCK - s_idx
    valid = (dist >= 0) & (dist < BLOCK) & ((n > 0) | (s_idx >= BLOCK))
    return dist.astype(F32), valid


def _dup_half(t, kv_head, lo):
    rolled = pltpu.roll(t, HEAD_DIM, axis=1)
    return jnp.where(lo, t, rolled) if kv_head == 0 else jnp.where(lo, rolled, t)


def _stack_heads(ref, kv_head, lo):
    pieces = []
    for r in range(Q_PER_KV):
        pair = kv_head * 4 + r // 2
        t = ref[:, pair * 128:(pair + 1) * 128].astype(BF16)
        sel = lo if r % 2 == 0 else jnp.logical_not(lo)
        pieces.append(jnp.where(sel, t, jnp.zeros_like(t)))
    return jnp.concatenate(pieces, axis=0)


def _unstack_heads(t, lo):
    return [jnp.where(lo, t[(2 * i) * BLOCK:(2 * i + 1) * BLOCK], t[(2 * i + 1) * BLOCK:(2 * i + 2) * BLOCK])
            for i in range(Q_PER_KV // 2)]


def _per_head_column(values):
    return jnp.concatenate([jnp.full((BLOCK, 1), v, F32) for v in values], axis=0)


def _group_probs(qm, kdup, dist, valid, sink_ref, kv_head):
    heads = [kv_head * Q_PER_KV + r for r in range(Q_PER_KV)]
    slope = _per_head_column([2.0 ** (-8.0 * (h + 1) / N_Q_HEADS) for h in heads])
    sink = _per_head_column([sink_ref[h] for h in heads])
    return _probs(qm, kdup, dist, valid, sink, slope)


def _probs(qm, kdup, dist, valid, sink, slope):
    s = lax.dot_general(qm, kdup, (((1,), (1,)), ((), ())), preferred_element_type=F32)
    s = s * (HEAD_DIM ** -0.5) - slope * dist
    s = jnp.where(valid, s, NEG_BIG)
    m = jnp.maximum(jnp.max(s, axis=-1, keepdims=True), sink)
    p = jnp.exp(s - m)
    esink = jnp.exp(sink - m)
    inv = 1.0 / (jnp.sum(p, axis=-1, keepdims=True) + esink)
    return p * inv, esink * inv


def _attn_fwd(proj, sinks, *, name):
    l = proj.shape[0]
    nb = l // BLOCK

    def body(sink_ref, q_ref, kc_ref, kp_ref, vc_ref, vp_ref, o_ref):
        n = pl.program_id(0)
        dist, valid = _attn_masks(n, BLOCK)
        lo = lax.broadcasted_iota(jnp.int32, (1, BLOCK), 1) < HEAD_DIM
        kx = jnp.concatenate([kp_ref[...], kc_ref[...]], axis=0).astype(BF16)
        vx = jnp.concatenate([vp_ref[...], vc_ref[...]], axis=0).astype(BF16)
        for kv_head in range(2):
            kdup = _dup_half(kx, kv_head, lo)
            vdup = _dup_half(vx, kv_head, lo)
            for pr in range(4):
                pair = kv_head * 4 + pr
                qp = q_ref[:, pair * 128:(pair + 1) * 128].astype(BF16)
                o_pair = jnp.zeros((BLOCK, 128), F32)
                for half in range(2):
                    head = 2 * pair + half
                    sel = lo if half == 0 else jnp.logical_not(lo)
                    qm = jnp.where(sel, qp, jnp.zeros_like(qp))
                    p, _ = _probs(qm, kdup, dist, valid, sink_ref[head], 2.0 ** (-8.0 * (head + 1) / N_Q_HEADS))
                    o = jnp.dot(p.astype(BF16), vdup, preferred_element_type=F32)
                    o_pair = o_pair + jnp.where(sel, o, 0.0)
                o_ref[:, pair * 128:(pair + 1) * 128] = o_pair.astype(BF16)

    kv = lambda col, prev: pl.BlockSpec(
        (BLOCK, KV_WIDTH), (lambda n: (jnp.maximum(n - 1, 0), col)) if prev else (lambda n: (n, col)))
    return pl.pallas_call(
        body, grid=(nb,),
        in_specs=[pl.BlockSpec(memory_space=pltpu.SMEM),
                  pl.BlockSpec((BLOCK, ATTN_WIDTH), lambda n: (n, 0)),
                  kv(COL_K, False), kv(COL_K, True), kv(COL_V, False), kv(COL_V, True)],
        out_specs=pl.BlockSpec((BLOCK, ATTN_WIDTH), lambda n: (n, 0)),
        out_shape=jax.ShapeDtypeStruct((l, ATTN_WIDTH), BF16), name=name,
        compiler_params=_cp(("parallel",)))(sinks, proj, proj, proj, proj, proj)


def _attn_bwd(proj, sinks, dattn, *, comm=None, name):
    l = proj.shape[0]
    nb = l // BLOCK

    def body(sink_ref, q_ref, kc_ref, kp_ref, vc_ref, vp_ref, do_ref,
             dq_ref, dkc_ref, dkp_ref, dvc_ref, dvp_ref, dsink_ref):
        n = pl.program_id(0)
        dist, valid = _attn_masks(n)
        lane = lax.broadcasted_iota(jnp.int32, (1, BLOCK), 1)
        lo = lane < HEAD_DIM
        kx = jnp.concatenate([kp_ref[...], kc_ref[...]], axis=0).astype(BF16)
        vx = jnp.concatenate([vp_ref[...], vc_ref[...]], axis=0).astype(BF16)
        dsink = jnp.zeros((1, BLOCK), F32)
        dk_heads, dv_heads = [], []
        for kv_head in range(2):
            kdup = _dup_half(kx, kv_head, lo)
            vdup = _dup_half(vx, kv_head, lo)
            qm = _stack_heads(q_ref, kv_head, lo)
            dom = _stack_heads(do_ref, kv_head, lo)
            p, psink = _group_probs(qm, kdup, dist, valid, sink_ref, kv_head)
            dp = lax.dot_general(dom, vdup, (((1,), (1,)), ((), ())), preferred_element_type=F32)
            delta = jnp.sum(p * dp, axis=-1, keepdims=True)
            ds = (p * (dp - delta) * (HEAD_DIM ** -0.5)).astype(BF16)
            dsink_rows = -psink * delta
            for r in range(Q_PER_KV):
                part = jnp.sum(dsink_rows[r * BLOCK:(r + 1) * BLOCK])
                dsink = dsink + jnp.where(lane == kv_head * Q_PER_KV + r, part, 0.0)
            dq = jnp.dot(ds, kdup, preferred_element_type=F32)
            for i, dq_pair in enumerate(_unstack_heads(dq, lo)):
                pair = kv_head * 4 + i
                dq_ref[:, pair * 128:(pair + 1) * 128] = dq_pair.astype(BF16)
            dk_acc = lax.dot_general(ds, qm, (((0,), (0,)), ((), ())), preferred_element_type=F32)
            dv_acc = lax.dot_general(p.astype(BF16), dom, (((0,), (0,)), ((), ())), preferred_element_type=F32)
            dk_heads.append(dk_acc + pltpu.roll(dk_acc, HEAD_DIM, axis=1))
            dv_heads.append(dv_acc + pltpu.roll(dv_acc, HEAD_DIM, axis=1))
        dk = jnp.where(lo, dk_heads[0], dk_heads[1])
        dv = jnp.where(lo, dv_heads[0], dv_heads[1])
        dkp_ref[...] = dk[:BLOCK]
        dkc_ref[...] = dk[BLOCK:]
        dvp_ref[...] = dv[:BLOCK]
        dvc_ref[...] = dv[BLOCK:]

        @pl.when(n == 0)
        def _():
            dsink_ref[...] = jnp.zeros_like(dsink_ref)

        dsink_ref[...] += dsink

    kv = lambda col, prev: pl.BlockSpec(
        (BLOCK, KV_WIDTH), (lambda n: (jnp.maximum(n - 1, 0), col)) if prev else (lambda n: (n, col)))
    qspec = pl.BlockSpec((BLOCK, ATTN_WIDTH), lambda n: (n, 0))
    kvout = pl.BlockSpec((BLOCK, KV_WIDTH), lambda n: (n, 0))
    kvshape = jax.ShapeDtypeStruct((l, KV_WIDTH), F32)
    return _hosted_call(
        body, grid=(nb,),
        in_specs=[pl.BlockSpec(memory_space=pltpu.SMEM), qspec,
                  kv(COL_K, False), kv(COL_K, True), kv(COL_V, False), kv(COL_V, True), qspec],
        out_specs=[qspec, kvout, kvout, kvout, kvout, pl.BlockSpec((1, BLOCK), lambda n: (0, 0))],
        out_shape=[jax.ShapeDtypeStruct((l, ATTN_WIDTH), BF16), kvshape, kvshape, kvshape, kvshape,
                   jax.ShapeDtypeStruct((1, BLOCK), F32)],
        scratch_shapes=[], sem=("arbitrary",), name=name,
        args=(sinks, proj, proj, proj, proj, proj, dattn), comm=comm)


def _kv_grad_merge(dkc, dkp, dvc, dvp, *, name):
    l = dkc.shape[0]
    nb = l // BLOCK

    def body(dkc_ref, dkp_ref, dvc_ref, dvp_ref, o_ref):
        last = pl.program_id(0) == nb - 1
        o_ref[:, :KV_WIDTH] = (dkc_ref[...] + jnp.where(last, 0.0, dkp_ref[...])).astype(BF16)
        o_ref[:, KV_WIDTH:] = (dvc_ref[...] + jnp.where(last, 0.0, dvp_ref[...])).astype(BF16)

    cur = pl.BlockSpec((BLOCK, KV_WIDTH), lambda n: (n, 0))
    nxt = pl.BlockSpec((BLOCK, KV_WIDTH), lambda n: (jnp.minimum(n + 1, nb - 1), 0))
    return pl.pallas_call(
        body, grid=(nb,), in_specs=[cur, nxt, cur, nxt],
        out_specs=pl.BlockSpec((BLOCK, 2 * KV_WIDTH), lambda n: (n, 0)),
        out_shape=jax.ShapeDtypeStruct((l, 2 * KV_WIDTH), BF16), name=name,
        compiler_params=_cp(("parallel",)))(dkc, dkp, dvc, dvp)


def _discretize(a_re, a_im, log_dt, b_re, b_im):
    dt = jnp.exp(log_dt)
    mag = jnp.exp(a_re * dt)
    ab_re = mag * jnp.cos(a_im * dt)
    ab_im = mag * jnp.sin(a_im * dt)
    nr = ab_re - 1.0
    ni = ab_im
    den = a_re * a_re + a_im * a_im
    z_re = (nr * a_re + ni * a_im) / den
    z_im = (ni * a_re - nr * a_im) / den
    bb_re = z_re * b_re - z_im * b_im
    bb_im = z_re * b_im + z_im * b_re
    return ab_re, ab_im, bb_re, bb_im


def _ssm_disc_fwd(a_re, a_im, log_dt, b_re, b_im, *, name):
    def body(ar, ai, ld, br, bi, o_ar, o_ai, o_br, o_bi):
        r = _discretize(ar[...], ai[...], ld[...], br[...], bi[...])
        o_ar[...], o_ai[...], o_br[...], o_bi[...] = r

    col = jax.ShapeDtypeStruct(a_re.shape, F32)
    mat = jax.ShapeDtypeStruct(b_re.shape, F32)
    return pl.pallas_call(body, out_shape=[col, col, mat, mat], name=name)(a_re, a_im, log_dt, b_re, b_im)


def _ssm_disc_bwd(a_re, a_im, log_dt, b_re, b_im, d_ab_re, d_ab_im, d_bb_re, d_bb_im, *, name):
    def body(ar, ai, ld, br, bi, g0, g1, g2, g3, o_ar, o_ai, o_ld, o_br, o_bi):
        _, vjp = jax.vjp(_discretize, ar[...], ai[...], ld[...], br[...], bi[...])
        r = vjp((g0[...], g1[...], g2[...], g3[...]))
        o_ar[...], o_ai[...], o_ld[...], o_br[...], o_bi[...] = r

    col = jax.ShapeDtypeStruct(a_re.shape, F32)
    mat = jax.ShapeDtypeStruct(b_re.shape, F32)
    return pl.pallas_call(body, out_shape=[col, col, col, mat, mat], name=name)(
        a_re, a_im, log_dt, b_re, b_im, d_ab_re, d_ab_im, d_bb_re, d_bb_im)


def _shift_rows(x, d, rows, *, down):
    t = x.shape[0]
    if down:
        return jnp.where(rows >= d, pltpu.roll(x, d, axis=0), 0.0)
    return jnp.where(rows < t - d, pltpu.roll(x, t - d, axis=0), 0.0)


def _scan_chunk(xr, xi, ar, ai, *, down):
    t = xr.shape[0]
    rows = lax.broadcasted_iota(jnp.int32, (t, 1), 0)
    pr, pi = ar, ai
    d = 1
    while d < t:
        sr = _shift_rows(xr, d, rows, down=down)
        si = _shift_rows(xi, d, rows, down=down)
        xr, xi = xr + pr * sr - pi * si, xi + pr * si + pi * sr
        pr, pi = pr * pr - pi * pi, 2.0 * pr * pi
        d *= 2
    return xr, xi


def _ssm_fwd(proj, ab, bd, cd, dskip, *, comm=None, name):
    l = proj.shape[0]
    t = min(SSM_CHUNK, l)
    nc = l // t

    def body(u_ref, ab_ref, bd_ref, cd_ref, ds_ref, y_ref, gy_ref, xs_ref, carry_ref):
        c = pl.program_id(1)

        @pl.when(c == 0)
        def _():
            carry_ref[...] = jnp.zeros_like(carry_ref)

        u = u_ref[...]
        ar, ai = ab_ref[0, 0:1, :], ab_ref[0, 1:2, :]
        bu = jnp.dot(u.astype(BF16), bd_ref[0], preferred_element_type=F32)
        rows = lax.broadcasted_iota(jnp.int32, (t, 1), 0)
        cr, ci = carry_ref[0:1, :], carry_ref[1:2, :]
        xr = bu[:, :SSM_X_BLK] + jnp.where(rows == 0, ar * cr - ai * ci, 0.0)
        xi = bu[:, SSM_X_BLK:] + jnp.where(rows == 0, ar * ci + ai * cr, 0.0)
        xr, xi = _scan_chunk(xr, xi, ar, ai, down=True)
        xs_ref[0, :, :SSM_X_BLK] = xr
        xs_ref[0, :, SSM_X_BLK:] = xi
        carry_ref[0:1, :] = xs_ref[0, t - 1:t, :SSM_X_BLK]
        carry_ref[1:2, :] = xs_ref[0, t - 1:t, SSM_X_BLK:]
        y = jnp.dot(xs_ref[0].astype(BF16), cd_ref[0], preferred_element_type=F32) + ds_ref[...] * u
        y_ref[...] = y
        gy_ref[...] = _gelu(y).astype(BF16)

    blk = lambda shape: pl.BlockSpec((1,) + shape, lambda j, c: (j, 0, 0))
    ycol = pl.BlockSpec((t, SSM_U_BLK), lambda j, c: (c, j))
    return _hosted_call(
        body, grid=(SSM_SPLIT, nc),
        in_specs=[pl.BlockSpec((t, SSM_U_BLK), lambda j, c: (c, COL_U + j)),
                  blk((2, SSM_X_BLK)), blk((SSM_U_BLK, 2 * SSM_X_BLK)), blk((2 * SSM_X_BLK, SSM_U_BLK)),
                  pl.BlockSpec((1, SSM_U_BLK), lambda j, c: (0, j))],
        out_specs=[ycol, ycol, pl.BlockSpec((1, t, 2 * SSM_X_BLK), lambda j, c: (j, c, 0))],
        out_shape=[jax.ShapeDtypeStruct((l, SSM_WIDTH), F32), jax.ShapeDtypeStruct((l, SSM_WIDTH), BF16),
                   jax.ShapeDtypeStruct((SSM_SPLIT, l, 2 * SSM_X_BLK), F32)],
        scratch_shapes=[pltpu.VMEM((2, SSM_X_BLK), F32)], sem=("parallel", "arbitrary"), name=name,
        args=(proj, ab, bd, cd, dskip), comm=comm)


def _ssm_bwd(proj, y, dgy, xs, ab, bdt, cdt, dskip, *, name):
    l = proj.shape[0]
    t = min(SSM_CHUNK, l)
    nc = l // t

    def body(u_ref, y_ref, dgy_ref, xs_ref, halo_ref, ab_ref, bdt_ref, cdt_ref, ds_ref,
             du_ref, dbd_ref, dcd_ref, dab_ref, dd_ref, carry_ref):
        c = pl.program_id(1)
        ci_ = nc - 1 - c

        @pl.when(c == 0)
        def _():
            carry_ref[...] = jnp.zeros_like(carry_ref)
            dbd_ref[...] = jnp.zeros_like(dbd_ref)
            dcd_ref[...] = jnp.zeros_like(dcd_ref)
            dab_ref[...] = jnp.zeros_like(dab_ref)
            dd_ref[...] = jnp.zeros_like(dd_ref)

        u = u_ref[...]
        dy = dgy_ref[...] * _gelu_grad(y_ref[...])
        dyb = dy.astype(BF16)
        ar, ai = ab_ref[0, 0:1, :], ab_ref[0, 1:2, :]
        g = jnp.dot(dyb, cdt_ref[0], preferred_element_type=F32)
        rows = lax.broadcasted_iota(jnp.int32, (t, 1), 0)
        cr, ci = carry_ref[0:1, :], carry_ref[1:2, :]
        lr = g[:, :SSM_X_BLK] + jnp.where(rows == t - 1, ar * cr + ai * ci, 0.0)
        li = g[:, SSM_X_BLK:] + jnp.where(rows == t - 1, ar * ci - ai * cr, 0.0)
        lr, li = _scan_chunk(lr, li, ar, -ai, down=False)
        lam = jnp.concatenate([lr, li], axis=1)
        carry_ref[0:1, :] = lr[0:1, :]
        carry_ref[1:2, :] = li[0:1, :]
        lamb = lam.astype(BF16)
        du_ref[...] = (jnp.dot(lamb, bdt_ref[0], preferred_element_type=F32) + ds_ref[...] * dy).astype(BF16)
        dbd_ref[0] += lax.dot_general(u.astype(BF16), lamb, (((0,), (0,)), ((), ())),
                                      preferred_element_type=F32)
        xs = xs_ref[0]
        dcd_ref[0] += lax.dot_general(xs.astype(BF16), dyb, (((0,), (0,)), ((), ())),
                                      preferred_element_type=F32)
        halo = jnp.where(ci_ > 0, halo_ref[0, 7:8, :], 0.0)
        xprev = jnp.where(rows == 0, halo, pltpu.roll(xs, 1, axis=0))
        xpr, xpi = xprev[:, :SSM_X_BLK], xprev[:, SSM_X_BLK:]
        dab_ref[0, 0:1, :] += jnp.sum(lr * xpr + li * xpi, axis=0, keepdims=True)
        dab_ref[0, 1:2, :] += jnp.sum(li * xpr - lr * xpi, axis=0, keepdims=True)
        dd_ref[...] += jnp.sum(dy * u, axis=0, keepdims=True)

    blk = lambda shape: pl.BlockSpec((1,) + shape, lambda j, c: (j, 0, 0))
    rev = lambda j, c: (nc - 1 - c, j)
    ycol = pl.BlockSpec((t, SSM_U_BLK), rev)
    hb = t // 8
    return pl.pallas_call(
        body, grid=(SSM_SPLIT, nc),
        in_specs=[pl.BlockSpec((t, SSM_U_BLK), lambda j, c: (nc - 1 - c, COL_U + j)), ycol, ycol,
                  pl.BlockSpec((1, t, 2 * SSM_X_BLK), lambda j, c: (j, nc - 1 - c, 0)),
                  pl.BlockSpec((1, 8, 2 * SSM_X_BLK),
                               lambda j, c: (j, jnp.maximum((nc - 1 - c) * hb - 1, 0), 0)),
                  blk((2, SSM_X_BLK)), blk((2 * SSM_X_BLK, SSM_U_BLK)), blk((SSM_U_BLK, 2 * SSM_X_BLK)),
                  pl.BlockSpec((1, SSM_U_BLK), lambda j, c: (0, j))],
        out_specs=[ycol, blk((SSM_U_BLK, 2 * SSM_X_BLK)), blk((2 * SSM_X_BLK, SSM_U_BLK)),
                   blk((2, SSM_X_BLK)), pl.BlockSpec((1, SSM_U_BLK), lambda j, c: (0, j))],
        out_shape=[jax.ShapeDtypeStruct((l, SSM_WIDTH), BF16),
                   jax.ShapeDtypeStruct((SSM_SPLIT, SSM_U_BLK, 2 * SSM_X_BLK), F32),
                   jax.ShapeDtypeStruct((SSM_SPLIT, 2 * SSM_X_BLK, SSM_U_BLK), F32),
                   jax.ShapeDtypeStruct((SSM_SPLIT, 2, SSM_X_BLK), F32),
                   jax.ShapeDtypeStruct((1, SSM_WIDTH), F32)],
        scratch_shapes=[pltpu.VMEM((2, SSM_X_BLK), F32)], name=name,
        compiler_params=_cp(("parallel", "arbitrary")))(proj, y, dgy, xs, xs, ab, bdt, cdt, dskip)


def _block_diag(t):
    s, g, a, b = t.shape
    return jnp.einsum('sgab,gk->sgakb', t, jnp.eye(g, dtype=t.dtype)).reshape(s, g * a, g * b)


def _block_diag_take(t, a, b):
    s = t.shape[0]
    return jnp.einsum('sgakb,gk->sgab', t.reshape(s, 8, a, 8, b), jnp.eye(8, dtype=t.dtype))


def _glu_fwd(z, *, name):
    l = z.shape[0]
    tr = min(512, l)

    def body(zv_ref, zg_ref, o_ref):
        o_ref[...] = (zv_ref[...] * _sigmoid(zg_ref[...])).astype(BF16)

    return pl.pallas_call(
        body, grid=(l // tr,),
        in_specs=[pl.BlockSpec((tr, SSM_WIDTH), lambda i: (i, 0)), pl.BlockSpec((tr, SSM_WIDTH), lambda i: (i, 1))],
        out_specs=pl.BlockSpec((tr, SSM_WIDTH), lambda i: (i, 0)),
        out_shape=jax.ShapeDtypeStruct((l, SSM_WIDTH), BF16), name=name,
        compiler_params=_cp(("parallel",)))(z, z)


def _glu_bwd(dssm, z, *, name):
    l = z.shape[0]
    tr = min(512, l)

    def body(d_ref, zv_ref, zg_ref, dz_ref, db_ref):
        d = d_ref[...]
        sg = _sigmoid(zg_ref[...])
        dv = d * sg
        dg = d * zv_ref[...] * sg * (1.0 - sg)
        dz_ref[:, :SSM_WIDTH] = dv.astype(BF16)
        dz_ref[:, SSM_WIDTH:] = dg.astype(BF16)

        @pl.when(pl.program_id(0) == 0)
        def _():
            db_ref[...] = jnp.zeros_like(db_ref)

        db_ref[:, :SSM_WIDTH] += jnp.sum(dv, axis=0, keepdims=True)
        db_ref[:, SSM_WIDTH:] += jnp.sum(dg, axis=0, keepdims=True)

    half = lambda k: pl.BlockSpec((tr, SSM_WIDTH), lambda i: (i, k))
    return pl.pallas_call(
        body, grid=(l // tr,), in_specs=[half(0), half(0), half(1)],
        out_specs=[pl.BlockSpec((tr, 2 * SSM_WIDTH), lambda i: (i, 0)),
                   pl.BlockSpec((1, 2 * SSM_WIDTH), lambda i: (0, 0))],
        out_shape=[jax.ShapeDtypeStruct((l, 2 * SSM_WIDTH), BF16), jax.ShapeDtypeStruct((1, 2 * SSM_WIDTH), F32)],
        name=name, compiler_params=_cp(("arbitrary",)))(dssm, z, z)


GATE_TC = 256
GATE_NJ = D_MODEL // GATE_TC


def _merge_fwd(proj, a, s, *, name):
    l = a.shape[0]
    tr = min(2048, l)

    def body(ga_ref, gs_ref, a_ref, s_ref, o_ref):
        o_ref[...] = (_sigmoid(ga_ref[...]) * a_ref[...] + _sigmoid(gs_ref[...]) * s_ref[...]).astype(BF16)

    own = pl.BlockSpec((tr, GATE_TC), lambda i, j: (i, j))
    return pl.pallas_call(
        body, grid=(l // tr, GATE_NJ),
        in_specs=[pl.BlockSpec((tr, GATE_TC), lambda i, j: (i, COL_GA // 2 + j)),
                  pl.BlockSpec((tr, GATE_TC), lambda i, j: (i, COL_GS // 2 + j)), own, own],
        out_specs=own, out_shape=jax.ShapeDtypeStruct((l, D_MODEL), BF16), name=name,
        compiler_params=_cp(("parallel", "parallel")))(proj, proj, a, s)


def _merge_bwd(proj, a, s, dm, *, name):
    l = a.shape[0]
    tr = min(2048, l)

    def body(ga_ref, gs_ref, a_ref, s_ref, dm_ref, da_ref, ds_ref, dga_ref, dgs_ref):
        d = dm_ref[...]
        sa = _sigmoid(ga_ref[...])
        ss = _sigmoid(gs_ref[...])
        da_ref[...] = (d * sa).astype(BF16)
        ds_ref[...] = (d * ss).astype(BF16)
        dga_ref[...] = (d * a_ref[...] * sa * (1.0 - sa)).astype(BF16)
        dgs_ref[...] = (d * s_ref[...] * ss * (1.0 - ss)).astype(BF16)

    own = pl.BlockSpec((tr, GATE_TC), lambda i, j: (i, j))
    wide = jax.ShapeDtypeStruct((l, D_MODEL), BF16)
    return pl.pallas_call(
        body, grid=(l // tr, GATE_NJ),
        in_specs=[pl.BlockSpec((tr, GATE_TC), lambda i, j: (i, COL_GA // 2 + j)),
                  pl.BlockSpec((tr, GATE_TC), lambda i, j: (i, COL_GS // 2 + j)), own, own, own],
        out_specs=[own, own, own, own], out_shape=[wide, wide, wide, wide], name=name,
        compiler_params=_cp(("parallel", "parallel")))(proj, proj, a, s, dm)


FF_TC = D_FF // 2
FF_NJ = 2
FF_ROWS = 128


FF_HALO = 16


def _conv_taps(ext, rows):
    h = FF_HALO
    return (ext[h:h + rows], pltpu.roll(ext, 1, axis=0)[h:h + rows], pltpu.roll(ext, 2, axis=0)[h:h + rows])


def _ff_specs(tr, l):
    hb = tr // FF_HALO
    last = l // FF_HALO - 1
    prev = lambda i: jnp.maximum(i * hb - 1, 0)
    nxt = lambda i: jnp.minimum((i + 1) * hb, last)
    return dict(
        own=pl.BlockSpec((tr, FF_TC), lambda j, i: (i, j)),
        own_next=pl.BlockSpec((FF_HALO, FF_TC), lambda j, i: (nxt(i), j)),
        val=pl.BlockSpec((tr, FF_TC), lambda j, i: (i, 2 * j)),
        val_next=pl.BlockSpec((FF_HALO, FF_TC), lambda j, i: (nxt(i), 2 * j)),
        gate=pl.BlockSpec((tr, FF_TC), lambda j, i: (i, 2 * j + 1)),
        gate_prev=pl.BlockSpec((FF_HALO, FF_TC), lambda j, i: (prev(i), 2 * j + 1)),
        gate_next=pl.BlockSpec((FF_HALO, FF_TC), lambda j, i: (nxt(i), 2 * j + 1)),
        pair=pl.BlockSpec((tr, 2 * FF_TC), lambda j, i: (i, j)),
        w=pl.BlockSpec((3, FF_TC), lambda j, i: (0, j)),
        b=pl.BlockSpec((1, FF_TC), lambda j, i: (0, j)))


def _ffn_act_fwd(up, conv_w, conv_b, *, name):
    l = up.shape[0]
    tr = min(FF_ROWS, l)

    def body(v_ref, g_ref, prev_ref, w_ref, b_ref, o_ref):
        prev = jnp.where(pl.program_id(1) == 0, 0.0, prev_ref[...].astype(F32))
        g0, g1, g2 = _conv_taps(jnp.concatenate([prev, g_ref[...].astype(F32)], axis=0), tr)
        gc = b_ref[...] + w_ref[0:1, :] * g2 + w_ref[1:2, :] * g1 + w_ref[2:3, :] * g0
        o_ref[...] = (v_ref[...].astype(F32) * _gelu(gc)).astype(BF16)

    sp = _ff_specs(tr, l)
    return pl.pallas_call(
        body, grid=(FF_NJ, l // tr), in_specs=[sp['val'], sp['gate'], sp['gate_prev'], sp['w'], sp['b']],
        out_specs=sp['own'], out_shape=jax.ShapeDtypeStruct((l, D_FF), BF16), name=name,
        compiler_params=_cp(("parallel", "parallel")))(up, up, up, conv_w, conv_b)


def _ffn_act_bwd(dact, up, conv_w, conv_b, *, comm=None, name):
    l = up.shape[0]
    tr = min(FF_ROWS, l)
    ni = l // tr
    te = tr + 8

    def body(d_ref, dn_ref, v_ref, vn_ref, g_ref, gp_ref, gn_ref, w_ref, b_ref, dup_ref, dw_ref, db_ref):
        i = pl.program_id(1)
        f32 = lambda ref, rows=None: ref[...].astype(F32)[:rows]
        prev = jnp.where(i == 0, 0.0, f32(gp_ref))
        g0, g1, g2 = _conv_taps(jnp.concatenate([prev, f32(g_ref), f32(gn_ref, 8)], axis=0), te)
        w0, w1, w2 = w_ref[0:1, :], w_ref[1:2, :], w_ref[2:3, :]
        gc = b_ref[...] + w0 * g2 + w1 * g1 + w2 * g0
        d_own = f32(d_ref)
        d = jnp.concatenate([d_own, jnp.where(i == ni - 1, 0.0, f32(dn_ref, 8))], axis=0)
        v = jnp.concatenate([f32(v_ref), f32(vn_ref, 8)], axis=0)
        dgc = d * v * _gelu_grad(gc)
        ahead1 = pltpu.roll(dgc, te - 1, axis=0)[:tr]
        ahead2 = pltpu.roll(dgc, te - 2, axis=0)[:tr]
        own = dgc[:tr]
        dup_ref[:, :FF_TC] = (d_own * _gelu(gc[:tr])).astype(BF16)
        dup_ref[:, FF_TC:] = (w2 * own + w1 * ahead1 + w0 * ahead2).astype(BF16)

        @pl.when(i == 0)
        def _():
            dw_ref[...] = jnp.zeros_like(dw_ref)
            db_ref[...] = jnp.zeros_like(db_ref)

        dw_ref[0:1, :] += jnp.sum(own * g2[:tr], axis=0, keepdims=True)
        dw_ref[1:2, :] += jnp.sum(own * g1[:tr], axis=0, keepdims=True)
        dw_ref[2:3, :] += jnp.sum(own * g0[:tr], axis=0, keepdims=True)
        db_ref[...] += jnp.sum(own, axis=0, keepdims=True)

    sp = _ff_specs(tr, l)
    return _hosted_call(
        body, grid=(FF_NJ, ni),
        in_specs=[sp['own'], sp['own_next'], sp['val'], sp['val_next'], sp['gate'], sp['gate_prev'],
                  sp['gate_next'], sp['w'], sp['b']],
        out_specs=[sp['pair'], sp['w'], sp['b']],
        out_shape=[jax.ShapeDtypeStruct((l, 2 * D_FF), BF16), jax.ShapeDtypeStruct((3, D_FF), F32),
                   jax.ShapeDtypeStruct((1, D_FF), F32)],
        scratch_shapes=[], sem=("parallel", "arbitrary"), name=name,
        args=(dact, dact, up, up, up, up, up, conv_w, conv_b), comm=comm)


def _col_sum(a, *, name):
    l, n = a.shape
    tr = min(512, l)

    def body(a_ref, o_ref):
        @pl.when(pl.program_id(0) == 0)
        def _():
            o_ref[...] = jnp.zeros_like(o_ref)

        o_ref[...] += jnp.sum(a_ref[...].astype(F32), axis=0, keepdims=True)

    return pl.pallas_call(
        body, grid=(l // tr,), in_specs=[pl.BlockSpec((tr, n), lambda i: (i, 0))],
        out_specs=pl.BlockSpec((1, n), lambda i: (0, 0)), out_shape=jax.ShapeDtypeStruct((1, n), F32),
        name=name, compiler_params=_cp(("arbitrary",)))(a)


def _local_step(x, target, wts, small, shards=None):
    wts = dict(wts)
    grads, recvs, sgr = {}, {}, {}
    inter = lambda keys: [k == 'w_up_t' for k in keys]
    none = lambda keys: None
    gather = (lambda keys: _GatherPlan([shards[k] for k in keys], inter(keys))) if shards is not None else none
    scatter = (lambda keys: _ScatterPlan([grads[k] for k in keys], inter(keys))) if shards is not None else none

    mm = _matmul

    def take(res, plan, keys, store):
        outs, couts = res
        if plan is not None:
            store.update(zip(keys, couts))
        return outs

    def mm_host(keys, make_plan, store, *args, **kw):
        plan = make_plan(keys)
        if plan is None:
            return _matmul(*args, **kw)
        return take(_matmul(*args, comm=plan, **kw), plan, keys, store)

    col = lambda t: t.reshape(SSM_GROUPS * SSM_STATE, 1)
    a_re, a_im = col(small['ssm_a_re']), col(small['ssm_a_im'])
    log_dt = jnp.repeat(small['ssm_log_dt'].reshape(SSM_GROUPS), SSM_STATE).reshape(-1, 1)
    b_re = small['ssm_b_re'].reshape(SSM_GROUPS * SSM_STATE, SSM_GROUP)
    b_im = small['ssm_b_im'].reshape(SSM_GROUPS * SSM_STATE, SSM_GROUP)
    ab_re, ab_im, bb_re, bb_im = _ssm_disc_fwd(a_re, a_im, log_dt, b_re, b_im, name="ssm_disc_fwd")
    ab = jnp.stack([ab_re.reshape(SSM_SPLIT, SSM_X_BLK), ab_im.reshape(SSM_SPLIT, SSM_X_BLK)], axis=1)
    to_bd = lambda t: _block_diag(t.reshape(SSM_SPLIT, 8, SSM_STATE, SSM_GROUP).transpose(0, 1, 3, 2))
    bd = jnp.concatenate([to_bd(bb_re), to_bd(bb_im)], axis=2)
    c_re = small['ssm_c_re'].reshape(SSM_SPLIT, 8, SSM_GROUP, SSM_STATE)
    c_im = small['ssm_c_im'].reshape(SSM_SPLIT, 8, SSM_GROUP, SSM_STATE)
    cdt = jnp.concatenate([_block_diag(c_re), -_block_diag(c_im)], axis=2)
    bd_b, cdt_b = bd.astype(BF16), cdt.astype(BF16)
    bdt_b, cd_b = bd_b.transpose(0, 2, 1), cdt_b.transpose(0, 2, 1)
    dskip = small['ssm_d'].reshape(1, SSM_WIDTH)

    sinks = small['attn_sinks'].reshape(N_Q_HEADS)
    plan = gather(['w_in_t'])
    h1, = take(_rms_fwd(x, small['attn_norm_g'], comm=plan, name="rms1_fwd"), plan, ['w_in_t'], wts)
    proj = mm_host(['w_glu_t', 'w_ba_t', 'w_bs_t', 'w_out'], gather, wts,
                   h1, wts['w_in_t'], tb=True, tm=512, tn=2944, tk=2048, inner='m', out_dtype=F32,
                   bias=small['b_in'], name="mm_in")
    attn = _attn_fwd(proj, sinks, name="attn_fwd")
    plan = gather(['w_up_t'])
    y, gy, xs = take(_ssm_fwd(proj, ab, bd_b, cd_b, dskip, comm=plan, name="ssm_fwd"), plan, ['w_up_t'], wts)
    z = mm(gy, wts['w_glu_t'], tb=True, tm=1024, tn=1024, tk=512, out_dtype=F32,
           bias=small['b_glu'], name="mm_glu")
    ssm = _glu_fwd(z, name="glu_fwd")
    a_br = mm(attn, wts['w_ba_t'], tb=True, tm=1024, tn=1024, tk=1024, out_dtype=F32, name="mm_ba")
    s_br = mm(ssm, wts['w_bs_t'], tb=True, tm=1024, tn=1024, tk=512, out_dtype=F32, name="mm_bs")
    merged = _merge_fwd(proj, a_br, s_br, name="merge_fwd")
    x2 = mm(merged, wts['w_out'], tm=1024, tn=1024, tk=2048, inner='m', out_dtype=F32, res=x, name="mm_out")
    h2, = take(_rms_fwd(x2, small['ffn_norm_g'], name="rms2_fwd"), None, [], wts)
    up = mm_host(['w_down'], gather, wts,
                 h2, wts['w_up_t'], tb=True, tm=1024, tn=1024, tk=2048, out_dtype=BF16, name="mm_up")
    conv_w, conv_b = small['conv_w'], small['conv_b']
    act = _ffn_act_fwd(up, conv_w, conv_b, name="ffn_act_fwd")
    x3 = mm(act, wts['w_down'], tm=1024, tn=1024, tk=2816, out_dtype=F32, res=x2, name="mm_down")
    dx3, dx3b, d_g3, loss = _final_loss(x3, small['final_norm_g'].reshape(1, D_MODEL), target, name="final_loss")

    sgr['final_norm_g'] = d_g3.reshape(D_MODEL)
    dact = mm(dx3b, wts['w_down'], tb=True, tm=512, tn=2816, tk=2048, inner='m', out_dtype=BF16, name="mm_dact")
    grads['w_down'] = mm(act, dx3b, ta=True, tm=512, tn=1024, tk=2048, out_dtype=BF16, name="mm_dw_down")
    plan = scatter(['w_down'])
    dup, sgr['conv_w'], sgr['conv_b'] = take(
        _ffn_act_bwd(dact, up, conv_w, conv_b, comm=plan, name="ffn_act_bwd"), plan, ['w_down'], recvs)
    grads['w_up_t'] = mm(dup, h2, ta=True, tm=1024, tn=1024, tk=2048, out_dtype=BF16, name="mm_dw_up")
    dh2 = mm_host(['w_up_t'], scatter, recvs,
                  dup, wts['w_up_t'], tm=1024, tn=1024, tk=2816, out_dtype=F32, name="mm_dh2")
    dx2, dx2b, sgr['ffn_norm_g'] = _rms_bwd(dh2, x2, small['ffn_norm_g'], dx3, with_bf16=True, name="rms2_bwd")

    dm = mm(dx2b, wts['w_out'], tb=True, tm=1024, tn=1024, tk=2048, inner='m', out_dtype=F32, name="mm_dmerged")
    grads['w_out'] = mm(merged, dx2b, ta=True, tm=1024, tn=1024, tk=2048, out_dtype=BF16, name="mm_dw_out")
    d_a, d_s, dga, dgs = _merge_bwd(proj, a_br, s_br, dm, name="merge_bwd")
    dattn = mm(d_a, wts['w_ba_t'], tm=1024, tn=1024, tk=2048, inner='m', out_dtype=BF16, name="mm_dattn")
    grads['w_ba_t'] = mm(d_a, attn, ta=True, tm=1024, tn=1024, tk=2048, out_dtype=BF16, name="mm_dw_ba")
    dssm = mm(d_s, wts['w_bs_t'], tm=1024, tn=512, tk=2048, inner='m', out_dtype=F32, name="mm_dssm")
    grads['w_bs_t'] = mm(d_s, ssm, ta=True, tm=1024, tn=512, tk=2048, out_dtype=BF16, name="mm_dw_bs")
    dz, sgr['b_glu'] = _glu_bwd(dssm, z, name="glu_bwd")
    grads['w_glu_t'] = mm(dz, gy, ta=True, tm=1024, tn=512, tk=2048, out_dtype=BF16, name="mm_dw_glu")
    dgy = mm(dz, wts['w_glu_t'], tm=1024, tn=512, tk=1024, inner='m', out_dtype=F32, name="mm_dgy")
    du, d_bd, d_cd, d_ab, sgr['ssm_d'] = _ssm_bwd(proj, y, dgy, xs, ab, bdt_b, cdt_b, dskip, name="ssm_bwd")
    keys = ['w_out', 'w_ba_t', 'w_bs_t', 'w_glu_t']
    plan = scatter(keys)
    dq, dkc, dkp, dvc, dvp, dsink = take(_attn_bwd(proj, sinks, dattn, comm=plan, name="attn_bwd"), plan, keys, recvs)
    dkv = _kv_grad_merge(dkc, dkp, dvc, dvp, name="kv_grad_merge")
    sgr['attn_sinks'] = dsink[:, :N_Q_HEADS]
    dproj = jnp.concatenate([dq, dkv, du, dga, dgs], axis=1)
    sgr['b_in'] = _col_sum(dproj, name="col_sum_dproj")
    grads['w_in_t'] = mm(dproj, h1, ta=True, tm=2944, tn=1024, tk=1024, out_dtype=BF16, name="mm_dw_in")
    dh1 = mm_host(['w_in_t'], scatter, recvs,
                  dproj, wts['w_in_t'], tm=1024, tn=1024, tk=2944, out_dtype=F32, name="mm_dh1")
    grad_x, sgr['attn_norm_g'] = _rms_bwd(dh1, x, small['attn_norm_g'], dx2, with_bf16=False, name="rms1_bwd")

    from_bd = lambda t: _block_diag_take(t, SSM_GROUP, SSM_STATE).transpose(0, 1, 3, 2).reshape(
        SSM_GROUPS * SSM_STATE, SSM_GROUP)
    d_bb_re = from_bd(d_bd[:, :, :SSM_X_BLK])
    d_bb_im = from_bd(d_bd[:, :, SSM_X_BLK:])
    d_cdt = d_cd.transpose(0, 2, 1)
    shape_c = (1, SSM_GROUPS, SSM_GROUP, SSM_STATE)
    sgr['ssm_c_re'] = _block_diag_take(d_cdt[:, :, :SSM_X_BLK], SSM_GROUP, SSM_STATE).reshape(shape_c)
    sgr['ssm_c_im'] = -_block_diag_take(d_cdt[:, :, SSM_X_BLK:], SSM_GROUP, SSM_STATE).reshape(shape_c)
    d_a_re, d_a_im, d_ldt, d_b_re, d_b_im = _ssm_disc_bwd(
        a_re, a_im, log_dt, b_re, b_im, d_ab[:, 0, :].reshape(-1, 1), d_ab[:, 1, :].reshape(-1, 1),
        d_bb_re, d_bb_im, name="ssm_disc_bwd")
    sgr['ssm_a_re'] = d_a_re.reshape(1, SSM_GROUPS, SSM_STATE)
    sgr['ssm_a_im'] = d_a_im.reshape(1, SSM_GROUPS, SSM_STATE)
    sgr['ssm_log_dt'] = d_ldt.reshape(SSM_GROUPS, SSM_STATE).sum(axis=1).reshape(1, SSM_GROUPS)
    sgr['ssm_b_re'] = d_b_re.reshape(1, SSM_GROUPS, SSM_STATE, SSM_GROUP)
    sgr['ssm_b_im'] = d_b_im.reshape(1, SSM_GROUPS, SSM_STATE, SSM_GROUP)
    return loss, grad_x, grads, recvs, sgr


def _swap_cores(arrs, *, name):
    n = len(arrs)

    def body(*refs):
        ins, outs = refs[:n], refs[n:2 * n]
        send_sems, recv_sems = refs[2 * n:]
        x, y, c = _place()
        copies = []
        for i in range(n):
            cp = pltpu.make_async_remote_copy(
                src_ref=ins[i], dst_ref=outs[i], send_sem=send_sems.at[i], recv_sem=recv_sems.at[i],
                device_id=(x, y, 1 - c), device_id_type=MESH)
            cp.start()
            copies.append(cp)
        for cp in copies:
            cp.wait()

    return pl.pallas_call(
        body, in_specs=[ANY] * n, out_specs=[ANY] * n,
        out_shape=[jax.ShapeDtypeStruct(a.shape, a.dtype) for a in arrs],
        scratch_shapes=[pltpu.SemaphoreType.DMA((n,)), pltpu.SemaphoreType.DMA((n,))],
        name=name)(*arrs)


def _all_reduce_small(buf, *, name):
    r = buf.shape[0]

    def body(in_ref, out_ref, slots, send_sems, recv_sems):
        x, y, c = _place()
        me = 4 * x + 2 * y + c
        slots[pl.ds(me, 1)] = in_ref[...][None]
        copies = []
        for k in range(N_DEV - 1):
            bx, by, bc = ((k + 1) >> 2) & 1, ((k + 1) >> 1) & 1, (k + 1) & 1
            peer = (1 - x if bx else x, 1 - y if by else y, 1 - c if bc else c)
            cp = pltpu.make_async_remote_copy(
                src_ref=in_ref, dst_ref=slots.at[me], send_sem=send_sems.at[k], recv_sem=recv_sems.at[k],
                device_id=peer, device_id_type=MESH)
            cp.start()
            copies.append(cp)
        for cp in copies:
            cp.wait()
        acc = slots[0]
        for d in range(1, N_DEV):
            acc = acc + slots[d]
        out_ref[...] = acc

    vm = pl.BlockSpec(memory_space=pltpu.VMEM)
    return pl.pallas_call(
        body, in_specs=[vm], out_specs=vm, out_shape=jax.ShapeDtypeStruct((r, 128), F32),
        scratch_shapes=[pltpu.VMEM((N_DEV, r, 128), F32), pltpu.SemaphoreType.DMA((N_DEV - 1,)),
                        pltpu.SemaphoreType.DMA((N_DEV - 1,))],
        name=name)(buf)


def _pack(arrs):
    flat = jnp.concatenate([a.reshape(-1).astype(F32) for a in arrs])
    pad = (-flat.shape[0]) % 1024
    return jnp.pad(flat, (0, pad)).reshape(-1, 128)


def _unpack(buf, shapes):
    flat = buf.reshape(-1)
    out, pos = [], 0
    for s in shapes:
        size = math.prod(s)
        out.append(flat[pos:pos + size].reshape(s))
        pos += size
    return out


TILE_ELEMS = 256 * 1024


def _tile_rows(r, c):
    if r * c <= TILE_ELEMS:
        return r
    for tr in range(TILE_ELEMS // c // 16 * 16, 0, -16):
        if r % tr == 0:
            return tr
    raise ValueError((r, c))


def _sum4(own, recv, *, name):
    r, c = own.shape
    tr = _tile_rows(r, c)

    def body(o_ref, r_ref, out_ref):
        acc = o_ref[...].astype(F32)
        for k in range(3):
            acc = acc + r_ref[k].astype(F32)
        out_ref[...] = acc

    return pl.pallas_call(
        body, grid=(r // tr,),
        in_specs=[pl.BlockSpec((tr, c), lambda i: (i, 0)), pl.BlockSpec((3, tr, c), lambda i: (0, i, 0))],
        out_specs=pl.BlockSpec((tr, c), lambda i: (i, 0)), out_shape=jax.ShapeDtypeStruct((r, c), F32),
        name=name, compiler_params=_cp(("parallel",)))(own, recv)


def _adamw(w, ga, gb, m, v, *, name):
    r, c = w.shape
    tr = _tile_rows(r, c)
    bc1 = 1.0 - ADAM_B1 ** ADAM_STEP
    bc2 = 1.0 - ADAM_B2 ** ADAM_STEP
    two = gb is not None

    def body(*refs):
        w_ref, ga_ref = refs[0], refs[1]
        pos = 2
        g = ga_ref[...]
        if two:
            g = g + refs[pos][...]
            pos += 1
        m_ref, v_ref, g_out, d_out, m_out, v_out = refs[pos:pos + 6]
        mn = ADAM_B1 * m_ref[...] + (1.0 - ADAM_B1) * g
        vn = ADAM_B2 * v_ref[...] + (1.0 - ADAM_B2) * (g * g)
        m_hat = mn / bc1
        v_hat = vn / bc2
        g_out[...] = g
        d_out[...] = -ADAM_LR * (m_hat / (jnp.sqrt(v_hat) + ADAM_EPS) + ADAM_WD * w_ref[...])
        m_out[...] = mn
        v_out[...] = vn

    spec = pl.BlockSpec((tr, c), lambda i: (i, 0))
    args = [w, ga] + ([gb] if two else []) + [m, v]
    shp = jax.ShapeDtypeStruct((r, c), F32)
    return pl.pallas_call(
        body, grid=(r // tr,), in_specs=[spec] * len(args), out_specs=[spec] * 4,
        out_shape=[shp] * 4, name=name, compiler_params=_cp(("parallel",)))(*args)


BIG = ['w_in', 'w_glu', 'w_branch_attn', 'w_branch_ssm', 'w_out', 'w_up', 'w_down']
BIG_KEY = {'w_in': 'w_in_t', 'w_glu': 'w_glu_t', 'w_branch_attn': 'w_ba_t', 'w_branch_ssm': 'w_bs_t',
           'w_out': 'w_out', 'w_up': 'w_up_t', 'w_down': 'w_down'}
COL_SHARDED = {'w_in', 'w_glu', 'w_branch_attn', 'w_branch_ssm', 'w_up'}
SMALL = ['attn_norm_g', 'b_in', 'attn_sinks', 'ssm_a_re', 'ssm_a_im', 'ssm_log_dt', 'ssm_b_re', 'ssm_b_im',
         'ssm_c_re', 'ssm_c_im', 'ssm_d', 'b_glu', 'ffn_norm_g', 'conv_b', 'final_norm_g']
WEIGHTS = ['attn_norm_g', 'w_in', 'b_in', 'attn_sinks', 'ssm_a_re', 'ssm_a_im', 'ssm_log_dt', 'ssm_b_re',
           'ssm_b_im', 'ssm_c_re', 'ssm_c_im', 'ssm_d', 'w_glu', 'b_glu', 'w_branch_attn', 'w_branch_ssm',
           'w_out', 'ffn_norm_g', 'w_up', 'conv_w', 'conv_b', 'w_down', 'final_norm_g']


def _shard_2d(name, t):
    t = t[0]
    return t.T if name in COL_SHARDED else t


def _unshard_2d(name, t):
    return (t.T if name in COL_SHARDED else t)[None]


def kernel(x, attn_norm_g, w_in, b_in, attn_sinks, ssm_a_re, ssm_a_im, ssm_log_dt, ssm_b_re, ssm_b_im, ssm_c_re, ssm_c_im, ssm_d, w_glu, b_glu, w_branch_attn, w_branch_ssm, w_out, ffn_norm_g, w_up, conv_w, conv_b, w_down, final_norm_g, loss_target, m_attn_norm_g, m_w_in, m_b_in, m_attn_sinks, m_ssm_a_re, m_ssm_a_im, m_ssm_log_dt, m_ssm_b_re, m_ssm_b_im, m_ssm_c_re, m_ssm_c_im, m_ssm_d, m_w_glu, m_b_glu, m_w_branch_attn, m_w_branch_ssm, m_w_out, m_ffn_norm_g, m_w_up, m_conv_w, m_conv_b, m_w_down, m_final_norm_g, v_attn_norm_g, v_w_in, v_b_in, v_attn_sinks, v_ssm_a_re, v_ssm_a_im, v_ssm_log_dt, v_ssm_b_re, v_ssm_b_im, v_ssm_c_re, v_ssm_c_im, v_ssm_d, v_w_glu, v_b_glu, v_w_branch_attn, v_w_branch_ssm, v_w_out, v_ffn_norm_g, v_w_up, v_conv_w, v_conv_b, v_w_down, v_final_norm_g):
    args = dict(locals())
    w = {n: args[n] for n in WEIGHTS}
    m = {n: args['m_' + n] for n in WEIGHTS}
    v = {n: args['v_' + n] for n in WEIGHTS}
    xi, yi, ci = _place()
    blk = 2 * xi + yi

    shards = {BIG_KEY[n]: _shard_2d(n, w[n]).astype(BF16) for n in BIG}
    cw_cols = w['conv_w'].shape[2]
    cw_place = lax.dynamic_update_slice(jnp.zeros((3, D_FF), F32), w['conv_w'][0] * (ci == 0).astype(F32),
                                        (0, blk * cw_cols))
    conv_w_full = _unpack(_all_reduce_small(_pack([cw_place]), name="gather_conv_w"), [(3, D_FF)])[0]

    small = {n: w[n] for n in SMALL}
    small['conv_w'] = conv_w_full
    loss_part, grad_x, grads, recvs, sgr = _local_step(x[0], loss_target[0], {}, small, shards)

    halves = []
    for n in BIG:
        full, recv = grads[BIG_KEY[n]], recvs[BIG_KEY[n]]
        r = full.shape[0] // N_CHIPS
        own = lax.dynamic_slice_in_dim(full, _block_pos(xi, yi, n == 'w_up') * r, r, axis=0)
        halves.append(_sum4(own, recv, name="sum4_" + n))
    others = _swap_cores(halves, name="swap_cores")
    out = {}
    for n, mine, other in zip(BIG, halves, others):
        res = _adamw(_shard_2d(n, w[n]), mine, other, _shard_2d(n, m[n]), _shard_2d(n, v[n]), name="adamw_" + n)
        out[n] = [_unshard_2d(n, t) for t in res]

    names = SMALL + ['conv_w']
    shapes = [w[n].shape for n in SMALL] + [(3, D_FF)]
    packed = _pack([sgr[n] for n in names] + [loss_part])
    summed = _unpack(_all_reduce_small(packed, name="all_reduce_small"), shapes + [(1, 1)])
    loss = summed[-1].reshape(())
    sg = dict(zip(names, summed[:-1]))
    sg['conv_w'] = lax.dynamic_slice_in_dim(sg['conv_w'], blk * cw_cols, cw_cols, axis=1)[None]
    res = _adamw(_pack([w[n] for n in names]), _pack([sg[n] for n in names]), None,
                 _pack([m[n] for n in names]), _pack([v[n] for n in names]), name="adamw_small")
    ushapes = [w[n].shape for n in names]
    unpacked = [_unpack(t, ushapes) for t in res]
    for i, n in enumerate(names):
        out[n] = [u[i] for u in unpacked]

    return (loss, grad_x[None], *[out[n][0] for n in WEIGHTS], *[out[n][1] for n in WEIGHTS],
            *[out[n][2] for n in WEIGHTS], *[out[n][3] for n in WEIGHTS])
```

```python
import functools
import math

import jax
import jax.numpy as jnp
from jax import lax
from jax.experimental import pallas as pl
from jax.experimental.pallas import tpu as pltpu

F32 = jnp.float32
BF16 = jnp.bfloat16

D_MODEL = 2048
N_Q_HEADS = 16
HEAD_DIM = 64
ATTN_WIDTH = 1024
KV_WIDTH = 128
BLOCK = 128
SSM_WIDTH = 512
SSM_GROUPS = 32
SSM_GROUP = 16
SSM_STATE = 64
D_FF = 5632
IN_COLS = 5888
RMS_EPS = 1e-6
NEG_BIG = -1e30
N_CHIPS = 4
N_DEV = 8

COL_K = 8
COL_V = 9
COL_U = 10
COL_GA = 14
COL_GS = 30

SSM_SPLIT = 4
SSM_U_BLK = 128
SSM_X_BLK = 512
SSM_CHUNK = 256

ADAM_LR = 0.001
ADAM_B1 = 0.9
ADAM_B2 = 0.999
ADAM_EPS = 1e-08
ADAM_WD = 0.01
ADAM_STEP = 10

VMEM_LIMIT_BYTES = 56 * 1024 * 1024
INV_SQRT2 = 1.0 / math.sqrt(2.0)
INV_SQRT2PI = 1.0 / math.sqrt(2.0 * math.pi)
MESH = pl.DeviceIdType.MESH
ANY = pl.BlockSpec(memory_space=pl.ANY)


def _cp(sem):
    return pltpu.CompilerParams(dimension_semantics=sem, vmem_limit_bytes=VMEM_LIMIT_BYTES)


def _gelu(x):
    return 0.5 * x * (1.0 + lax.erf(x * INV_SQRT2))


def _gelu_grad(x):
    return 0.5 * (1.0 + lax.erf(x * INV_SQRT2)) + x * jnp.exp(-0.5 * x * x) * INV_SQRT2PI


def _sigmoid(x):
    return 1.0 / (1.0 + jnp.exp(-x))


def _place():
    return lax.axis_index("x"), lax.axis_index("y"), lax.axis_index("c")


def _other_chips(x, y):
    return [(1 - x, y), (x, 1 - y), (1 - x, 1 - y)]


def _block_pos(x, y, interleaved):
    return x + 2 * y if interleaved else 2 * x + y


class _GatherPlan:
    def __init__(self, shards, interleaved):
        self.arrays = list(shards)
        self.interleaved = list(interleaved)
        n = len(shards)
        self.out_shape = [jax.ShapeDtypeStruct((N_CHIPS * s.shape[0], s.shape[1]), s.dtype) for s in shards]
        self.scratch = [pltpu.SemaphoreType.DMA((6 * n,)), pltpu.SemaphoreType.DMA((6 * n,)),
                        pltpu.SemaphoreType.DMA((n,))]

    def _copies(self, kind, ins, outs, sems):
        send, recv, local = sems
        n = len(self.arrays)
        x, y, c = _place()
        copies = []
        for i in range(n):
            r = self.arrays[i].shape[0]
            h = r // 2
            blk = _block_pos(x, y, self.interleaved[i])
            if kind == 'mine':
                copies.append(pltpu.make_async_copy(ins[i], outs[i].at[pl.ds(blk * r, r), :], local.at[i]))
                continue
            for k, (px, py) in enumerate(_other_chips(x, y)):
                theirs = _block_pos(px, py, self.interleaved[i]) * r
                if kind in ('ici_out', 'ici_in'):
                    route = dict(send_sem=send.at[3 * i + k], recv_sem=recv.at[3 * i + k],
                                 device_id=(px, py, c), device_id_type=MESH)
                else:
                    route = dict(send_sem=send.at[3 * (n + i) + k], recv_sem=recv.at[3 * (n + i) + k],
                                 device_id=(x, y, 1 - c), device_id_type=MESH)
                if kind == 'ici_out':
                    src, dst = ins[i].at[pl.ds(c * h, h), :], outs[i].at[pl.ds(blk * r + c * h, h), :]
                elif kind == 'd2d_in':
                    src = dst = outs[i].at[pl.ds(theirs + (1 - c) * h, h), :]
                else:
                    src = dst = outs[i].at[pl.ds(theirs + c * h, h), :]
                copies.append(pltpu.make_async_remote_copy(src_ref=src, dst_ref=dst, **route))
        return copies

    def start(self, ins, outs, sems):
        for cp in self._copies('mine', ins, outs, sems) + self._copies('ici_out', ins, outs, sems):
            cp.start()

    def middle(self, ins, outs, sems):
        for arrived, onward in zip(self._copies('ici_in', ins, outs, sems), self._copies('d2d_out', ins, outs, sems)):
            arrived.wait_recv()
            onward.start()

    def finish(self, ins, outs, sems):
        for cp in self._copies('d2d_in', ins, outs, sems):
            cp.wait_recv()
        for cp in self._copies('ici_out', ins, outs, sems) + self._copies('d2d_out', ins, outs, sems):
            cp.wait_send()
        for cp in self._copies('mine', ins, outs, sems):
            cp.wait()


class _ScatterPlan:
    def __init__(self, fulls, interleaved):
        self.arrays = list(fulls)
        self.interleaved = list(interleaved)
        n = len(fulls)
        self.out_shape = [jax.ShapeDtypeStruct((3, f.shape[0] // N_CHIPS, f.shape[1]), f.dtype) for f in fulls]
        self.scratch = [pltpu.SemaphoreType.DMA((3 * n,)), pltpu.SemaphoreType.DMA((3 * n,))]

    def _copies(self, ins, outs, sems):
        send, recv = sems
        x, y, c = _place()
        copies = []
        for i in range(len(self.arrays)):
            r = self.arrays[i].shape[0] // N_CHIPS
            for k, (px, py) in enumerate(_other_chips(x, y)):
                copies.append(pltpu.make_async_remote_copy(
                    src_ref=ins[i].at[pl.ds(_block_pos(px, py, self.interleaved[i]) * r, r), :], dst_ref=outs[i].at[k],
                    send_sem=send.at[3 * i + k], recv_sem=recv.at[3 * i + k],
                    device_id=(px, py, c), device_id_type=MESH))
        return copies

    def start(self, ins, outs, sems):
        for cp in self._copies(ins, outs, sems):
            cp.start()

    def middle(self, ins, outs, sems):
        pass

    def finish(self, ins, outs, sems):
        for cp in self._copies(ins, outs, sems):
            cp.wait()


def _hosted_call(body, *, grid, in_specs, out_specs, out_shape, scratch_shapes, sem, name, args, comm=None,
                 aliases=None):
    aliases = aliases or {}
    if comm is None:
        outs = pl.pallas_call(body, grid=grid, in_specs=in_specs, out_specs=out_specs, out_shape=out_shape,
                              scratch_shapes=scratch_shapes, name=name, input_output_aliases=aliases,
                              compiler_params=_cp(sem))(*args)
        return outs, None
    n_in, n_out, n_scr = len(in_specs), len(out_specs), len(scratch_shapes)
    nc, ns = len(comm.arrays), len(comm.scratch)
    total = math.prod(grid)
    mid = min(total - 1, (3 * total) // 4)

    def wrapped(*refs):
        pos = 0
        ins = refs[pos:pos + n_in]; pos += n_in
        cins = refs[pos:pos + nc]; pos += nc
        outs = refs[pos:pos + n_out]; pos += n_out
        couts = refs[pos:pos + nc]; pos += nc
        scr = refs[pos:pos + n_scr]; pos += n_scr
        sems = refs[pos:pos + ns]
        step = 0
        for ax, g in enumerate(grid):
            step = step * g + pl.program_id(ax)

        @pl.when(step == 0)
        def _():
            comm.start(cins, couts, sems)

        body(*ins, *outs, *scr)

        @pl.when(step == mid)
        def _():
            comm.middle(cins, couts, sems)

        @pl.when(step == total - 1)
        def _():
            comm.finish(cins, couts, sems)

    res = pl.pallas_call(
        wrapped, grid=grid, in_specs=list(in_specs) + [ANY] * nc, out_specs=list(out_specs) + [ANY] * nc,
        out_shape=list(out_shape) + list(comm.out_shape), scratch_shapes=list(scratch_shapes) + list(comm.scratch),
        name=name, input_output_aliases=aliases,
        compiler_params=_cp(("arbitrary",) * len(grid)))(*args, *comm.arrays)
    return res[:n_out], res[n_out:]


class _Hook:
    def __init__(self, fn, ins=(), in_specs=(), outs=()):
        self.fn, self.ins, self.in_specs, self.outs = fn, list(ins), list(in_specs), list(outs)


def _matmul(a, b, *, ta=False, tb=False, tm, tn, tk, out_dtype=None, bias=None, res=None, inner='n',
            comm=None, prologue=None, epilogue=None, a_shape=None, sequential=False, name):
    if a is None:
        m, kdim = a_shape
    elif ta:
        kdim, m = a.shape
    else:
        m, kdim = a.shape
    if tb:
        n, k2 = b.shape
    else:
        k2, n = b.shape
    assert kdim == k2, (name, kdim, b.shape)
    tm, tn, tk = min(tm, m), min(tn, n), min(tk, kdim)
    assert m % tm == 0 and n % tn == 0 and kdim % tk == 0, (name, m, n, kdim, tm, tn, tk)
    nk = kdim // tk
    dn = (((0 if ta else 1,), (1 if tb else 0,)), ((), ()))
    hooks = [h for h in (prologue, epilogue) if h is not None]
    n_pro_in = len(prologue.ins) if prologue else 0
    n_epi_in = len(epilogue.ins) if epilogue else 0
    n_pro_out = len(prologue.outs) if prologue else 0
    n_epi_out = len(epilogue.outs) if epilogue else 0
    if inner == 'n':
        grid = (m // tm, n // tn, nk)
        mi = lambda g0, g1: g0
        ni = lambda g0, g1: g1
    else:
        grid = (n // tn, m // tm, nk)
        mi = lambda g0, g1: g1
        ni = lambda g0, g1: g0

    def body(*refs):
        refs = list(refs)
        take = lambda cnt: [refs.pop(0) for _ in range(cnt)]
        a_ref = take(1)[0] if a is not None else None
        b_ref = take(1)[0]
        bias_ref = take(1)[0] if bias is not None else None
        res_ref = take(1)[0] if res is not None else None
        pro_in, epi_in = take(n_pro_in), take(n_epi_in)
        o_ref = take(1)[0] if epilogue is None else None
        pro_out, epi_out = take(n_pro_out), take(n_epi_out)
        i, j, k = mi(pl.program_id(0), pl.program_id(1)), ni(pl.program_id(0), pl.program_id(1)), pl.program_id(2)

        def finish(r):
            if bias_ref is not None:
                r = r + bias_ref[...]
            if res_ref is not None:
                r = r + res_ref[...]
            if epilogue is None:
                o_ref[...] = r.astype(out_dtype)
            else:
                epilogue.fn(r, epi_in, epi_out, i, j)

        a_val = a_ref[...] if prologue is None else prologue.fn(a_ref, pro_in, pro_out, i, k)
        prod = lax.dot_general(a_val.astype(BF16), b_ref[...].astype(BF16), dn, preferred_element_type=F32)
        if nk == 1:
            finish(prod)
            return
        acc_ref = refs[0]

        @pl.when(k == 0)
        def _():
            acc_ref[...] = prod

        @pl.when(k > 0)
        def _():
            acc_ref[...] += prod

        @pl.when(k == nk - 1)
        def _():
            finish(acc_ref[...])

    spec = lambda shape, fn: pl.BlockSpec(shape, lambda g0, g1, k: fn(mi(g0, g1), ni(g0, g1), k))
    in_specs, args = [], []
    if a is not None:
        in_specs.append(spec((tk, tm), lambda i, j, k: (k, i)) if ta else spec((tm, tk), lambda i, j, k: (i, k)))
        args.append(a)
    in_specs.append(spec((tn, tk), lambda i, j, k: (j, k)) if tb else spec((tk, tn), lambda i, j, k: (k, j)))
    args.append(b)
    if bias is not None:
        in_specs.append(spec((1, tn), lambda i, j, k: (0, j)))
        args.append(bias)
    if res is not None:
        in_specs.append(spec((tm, tn), lambda i, j, k: (i, j)))
        args.append(res)
    for h in hooks:
        in_specs += [spec(shape, fn) for shape, fn in h.in_specs]
        args += h.ins
    out_specs, out_shape = [], []
    if epilogue is None:
        out_specs.append(spec((tm, tn), lambda i, j, k: (i, j)))
        out_shape.append(jax.ShapeDtypeStruct((m, n), out_dtype))
    for h in hooks:
        out_specs += [spec(blk, fn) for _, _, blk, fn in h.outs]
        out_shape += [jax.ShapeDtypeStruct(shape, dtype) for shape, dtype, _, _ in h.outs]
    outs, couts = _hosted_call(
        body, grid=grid, in_specs=in_specs, out_specs=out_specs, out_shape=out_shape,
        scratch_shapes=[pltpu.VMEM((tm, tn), F32)] if nk > 1 else [],
        sem=("arbitrary",) * 3 if sequential else ("parallel", "parallel", "arbitrary"),
        name=name, args=args, comm=comm)
    outs = outs[0] if not hooks else outs
    return outs if comm is None else (outs, couts)


def _rms_fwd(x, g, *, comm=None, name):
    l, d = x.shape
    tr = min(256, l)

    def body(x_ref, g_ref, h_ref):
        xf = x_ref[...]
        r = lax.rsqrt(jnp.mean(xf * xf, axis=-1, keepdims=True) + RMS_EPS)
        h_ref[...] = ((xf * r) * g_ref[...]).astype(BF16)

    row = pl.BlockSpec((tr, d), lambda i: (i, 0))
    return _hosted_call(
        body, grid=(l // tr,), in_specs=[row, pl.BlockSpec((1, d), lambda i: (0, 0))],
        out_specs=[row], out_shape=[jax.ShapeDtypeStruct((l, d), BF16)], scratch_shapes=[],
        sem=("parallel",), name=name, args=(x, g), comm=comm)


def _rms_fwd_hook(g, l, tm):
    def fn(r, ins, outs, i, j):
        outs[0][...] = r
        rr = lax.rsqrt(jnp.mean(r * r, axis=-1, keepdims=True) + RMS_EPS)
        outs[1][...] = ((r * rr) * ins[0][...]).astype(BF16)

    d = g.shape[1]
    row = lambda i, j, k: (i, 0)
    return _Hook(fn, ins=[g], in_specs=[((1, d), lambda i, j, k: (0, 0))],
                 outs=[((l, d), F32, (tm, d), row), ((l, d), BF16, (tm, d), row)])


def _rms_bwd_hook(x, g, dres, tm, *, with_bf16):
    def fn(dyv, ins, outs, i, j):
        x_ref, g_ref, dres_ref = ins
        dg_ref = outs[-1]
        xf = x_ref[...]
        r = lax.rsqrt(jnp.mean(xf * xf, axis=-1, keepdims=True) + RMS_EPS)
        xhat = xf * r
        dxh = dyv * g_ref[...]
        dx = r * (dxh - xhat * jnp.mean(dxh * xhat, axis=-1, keepdims=True)) + dres_ref[...]
        outs[0][...] = dx
        if with_bf16:
            outs[1][...] = dx.astype(BF16)

        @pl.when(i == 0)
        def _():
            dg_ref[...] = jnp.zeros_like(dg_ref)

        dg_ref[...] += jnp.sum(dyv * xhat, axis=0, keepdims=True)

    l, d = x.shape
    row = lambda i, j, k: (i, 0)
    vec = lambda i, j, k: (0, 0)
    outs = [((l, d), F32, (tm, d), row)] + ([((l, d), BF16, (tm, d), row)] if with_bf16 else [])
    return _Hook(fn, ins=[x, g, dres], in_specs=[((tm, d), row), ((1, d), vec), ((tm, d), row)],
                 outs=outs + [((1, d), F32, (1, d), vec)])


def _final_loss_hook(g, target, tm):
    l, d = target.shape

    def fn(xf, ins, outs, i, j):
        g_ref, t_ref = ins
        dx_ref, dxb_ref, dg_ref, loss_ref = outs
        gv = g_ref[...]
        r = lax.rsqrt(jnp.mean(xf * xf, axis=-1, keepdims=True) + RMS_EPS)
        xhat = xf * r
        diff = xhat * gv - t_ref[...]
        dout = diff * (1.0 / d)
        dxh = dout * gv
        dx = r * (dxh - xhat * jnp.mean(dxh * xhat, axis=-1, keepdims=True))
        dx_ref[...] = dx
        dxb_ref[...] = dx.astype(BF16)

        @pl.when(i == 0)
        def _():
            dg_ref[...] = jnp.zeros_like(dg_ref)
            loss_ref[...] = jnp.zeros_like(loss_ref)

        dg_ref[...] += jnp.sum(dout * xhat, axis=0, keepdims=True)
        part = jnp.sum(jnp.mean(diff * diff, axis=-1, keepdims=True), axis=0, keepdims=True)
        loss_ref[...] += 0.5 * part

    row = lambda i, j, k: (i, 0)
    vec = lambda i, j, k: (0, 0)
    return _Hook(fn, ins=[g, target], in_specs=[((1, d), vec), ((tm, d), row)],
                 outs=[((l, d), F32, (tm, d), row), ((l, d), BF16, (tm, d), row),
                       ((1, d), F32, (1, d), vec), ((1, 1), F32, (1, 1), vec)])


Q_PER_KV = 8
GROUP_ROWS = Q_PER_KV * BLOCK


def _attn_masks(n, rows=GROUP_ROWS):
    q_idx = lax.broadcasted_iota(jnp.int32, (rows, 2 * BLOCK), 0) & (BLOCK - 1)
    s_idx = lax.broadcasted_iota(jnp.int32, (rows, 2 * BLOCK), 1)
    dist = q_idx + BLOCK - s_idx
    valid = (dist >= 0) & (dist < BLOCK) & ((n > 0) | (s_idx >= BLOCK))
    return dist.astype(F32), valid


def _dup_half(t, kv_head, lo):
    rolled = pltpu.roll(t, HEAD_DIM, axis=1)
    return jnp.where(lo, t, rolled) if kv_head == 0 else jnp.where(lo, rolled, t)


def _stack_heads(ref, kv_head, lo):
    pieces = []
    for r in range(Q_PER_KV):
        pair = kv_head * 4 + r // 2
        t = ref[:, pair * 128:(pair + 1) * 128].astype(BF16)
        sel = lo if r % 2 == 0 else jnp.logical_not(lo)
        pieces.append(jnp.where(sel, t, jnp.zeros_like(t)))
    return jnp.concatenate(pieces, axis=0)


def _unstack_heads(t, lo):
    return [jnp.where(lo, t[(2 * i) * BLOCK:(2 * i + 1) * BLOCK], t[(2 * i + 1) * BLOCK:(2 * i + 2) * BLOCK])
            for i in range(Q_PER_KV // 2)]


def _per_head_column(values):
    return jnp.concatenate([jnp.full((BLOCK, 1), v, F32) for v in values], axis=0)


def _group_probs(qm, kdup, dist, valid, sink_ref, kv_head):
    heads = [kv_head * Q_PER_KV + r for r in range(Q_PER_KV)]
    slope = _per_head_column([2.0 ** (-8.0 * (h + 1) / N_Q_HEADS) for h in heads])
    sink = _per_head_column([sink_ref[h] for h in heads])
    return _probs(qm, kdup, dist, valid, sink, slope)


def _probs(qm, kdup, dist, valid, sink, slope):
    s = lax.dot_general(qm, kdup, (((1,), (1,)), ((), ())), preferred_element_type=F32)
    s = s * (HEAD_DIM ** -0.5) - slope * dist
    s = jnp.where(valid, s, NEG_BIG)
    m = jnp.maximum(jnp.max(s, axis=-1, keepdims=True), sink)
    p = jnp.exp(s - m)
    esink = jnp.exp(sink - m)
    inv = 1.0 / (jnp.sum(p, axis=-1, keepdims=True) + esink)
    return p * inv, esink * inv


def _attn_fwd(proj, sinks, *, name):
    l = proj.shape[0]
    nb = l // BLOCK

    def body(sink_ref, q_ref, kc_ref, kp_ref, vc_ref, vp_ref, o_ref):
        n = pl.program_id(0)
        dist, valid = _attn_masks(n, BLOCK)
        lo = lax.broadcasted_iota(jnp.int32, (1, BLOCK), 1) < HEAD_DIM
        kx = jnp.concatenate([kp_ref[...], kc_ref[...]], axis=0).astype(BF16)
        vx = jnp.concatenate([vp_ref[...], vc_ref[...]], axis=0).astype(BF16)
        for kv_head in range(2):
            kdup = _dup_half(kx, kv_head, lo)
            vdup = _dup_half(vx, kv_head, lo)
            for pr in range(4):
                pair = kv_head * 4 + pr
                qp = q_ref[:, pair * 128:(pair + 1) * 128].astype(BF16)
                o_pair = jnp.zeros((BLOCK, 128), F32)
                for half in range(2):
                    head = 2 * pair + half
                    sel = lo if half == 0 else jnp.logical_not(lo)
                    qm = jnp.where(sel, qp, jnp.zeros_like(qp))
                    p, _ = _probs(qm, kdup, dist, valid, sink_ref[head], 2.0 ** (-8.0 * (head + 1) / N_Q_HEADS))
                    o = jnp.dot(p.astype(BF16), vdup, preferred_element_type=F32)
                    o_pair = o_pair + jnp.where(sel, o, 0.0)
                o_ref[:, pair * 128:(pair + 1) * 128] = o_pair.astype(BF16)

    kv = lambda col, prev: pl.BlockSpec(
        (BLOCK, KV_WIDTH), (lambda n: (jnp.maximum(n - 1, 0), col)) if prev else (lambda n: (n, col)))
    return pl.pallas_call(
        body, grid=(nb,),
        in_specs=[pl.BlockSpec(memory_space=pltpu.SMEM),
                  pl.BlockSpec((BLOCK, ATTN_WIDTH), lambda n: (n, 0)),
                  kv(COL_K, False), kv(COL_K, True), kv(COL_V, False), kv(COL_V, True)],
        out_specs=pl.BlockSpec((BLOCK, ATTN_WIDTH), lambda n: (n, 0)),
        out_shape=jax.ShapeDtypeStruct((l, ATTN_WIDTH), BF16), name=name,
        compiler_params=_cp(("parallel",)))(sinks, proj, proj, proj, proj, proj)


def _attn_bwd(proj, sinks, dattn, *, comm=None, name):
    l = proj.shape[0]
    nb = l // BLOCK

    def body(sink_ref, q_ref, kc_ref, kp_ref, vc_ref, vp_ref, do_ref,
             dq_ref, dkc_ref, dkp_ref, dvc_ref, dvp_ref, dsink_ref):
        n = pl.program_id(0)
        dist, valid = _attn_masks(n)
        lane = lax.broadcasted_iota(jnp.int32, (1, BLOCK), 1)
        lo = lane < HEAD_DIM
        kx = jnp.concatenate([kp_ref[...], kc_ref[...]], axis=0).astype(BF16)
        vx = jnp.concatenate([vp_ref[...], vc_ref[...]], axis=0).astype(BF16)
        dsink = jnp.zeros((1, BLOCK), F32)
        dk_heads, dv_heads = [], []
        for kv_head in range(2):
            kdup = _dup_half(kx, kv_head, lo)
            vdup = _dup_half(vx, kv_head, lo)
            qm = _stack_heads(q_ref, kv_head, lo)
            dom = _stack_heads(do_ref, kv_head, lo)
            p, psink = _group_probs(qm, kdup, dist, valid, sink_ref, kv_head)
            dp = lax.dot_general(dom, vdup, (((1,), (1,)), ((), ())), preferred_element_type=F32)
            delta = jnp.sum(p * dp, axis=-1, keepdims=True)
            ds = (p * (dp - delta) * (HEAD_DIM ** -0.5)).astype(BF16)
            dsink_rows = -psink * delta
            for r in range(Q_PER_KV):
                part = jnp.sum(dsink_rows[r * BLOCK:(r + 1) * BLOCK])
                dsink = dsink + jnp.where(lane == kv_head * Q_PER_KV + r, part, 0.0)
            dq = jnp.dot(ds, kdup, preferred_element_type=F32)
            for i, dq_pair in enumerate(_unstack_heads(dq, lo)):
                pair = kv_head * 4 + i
                dq_ref[:, pair * 128:(pair + 1) * 128] = dq_pair.astype(BF16)
            dk_acc = lax.dot_general(ds, qm, (((0,), (0,)), ((), ())), preferred_element_type=F32)
            dv_acc = lax.dot_general(p.astype(BF16), dom, (((0,), (0,)), ((), ())), preferred_element_type=F32)
            dk_heads.append(dk_acc + pltpu.roll(dk_acc, HEAD_DIM, axis=1))
            dv_heads.append(dv_acc + pltpu.roll(dv_acc, HEAD_DIM, axis=1))
        dk = jnp.where(lo, dk_heads[0], dk_heads[1])
        dv = jnp.where(lo, dv_heads[0], dv_heads[1])
        dkp_ref[...] = dk[:BLOCK]
        dkc_ref[...] = dk[BLOCK:]
        dvp_ref[...] = dv[:BLOCK]
        dvc_ref[...] = dv[BLOCK:]

        @pl.when(n == 0)
        def _():
            dsink_ref[...] = jnp.zeros_like(dsink_ref)

        dsink_ref[...] += dsink

    kv = lambda col, prev: pl.BlockSpec(
        (BLOCK, KV_WIDTH), (lambda n: (jnp.maximum(n - 1, 0), col)) if prev else (lambda n: (n, col)))
    qspec = pl.BlockSpec((BLOCK, ATTN_WIDTH), lambda n: (n, 0))
    kvout = pl.BlockSpec((BLOCK, KV_WIDTH), lambda n: (n, 0))
    kvshape = jax.ShapeDtypeStruct((l, KV_WIDTH), F32)
    return _hosted_call(
        body, grid=(nb,),
        in_specs=[pl.BlockSpec(memory_space=pltpu.SMEM), qspec,
                  kv(COL_K, False), kv(COL_K, True), kv(COL_V, False), kv(COL_V, True), qspec],
        out_specs=[qspec, kvout, kvout, kvout, kvout, pl.BlockSpec((1, BLOCK), lambda n: (0, 0))],
        out_shape=[jax.ShapeDtypeStruct((l, ATTN_WIDTH), BF16), kvshape, kvshape, kvshape, kvshape,
                   jax.ShapeDtypeStruct((1, BLOCK), F32)],
        scratch_shapes=[], sem=("arbitrary",), name=name,
        args=(sinks, proj, proj, proj, proj, proj, dattn), comm=comm)


def _kv_grad_merge(dkc, dkp, dvc, dvp, *, name):
    l = dkc.shape[0]
    nb = l // BLOCK

    def body(dkc_ref, dkp_ref, dvc_ref, dvp_ref, o_ref):
        last = pl.program_id(0) == nb - 1
        o_ref[:, :KV_WIDTH] = (dkc_ref[...] + jnp.where(last, 0.0, dkp_ref[...])).astype(BF16)
        o_ref[:, KV_WIDTH:] = (dvc_ref[...] + jnp.where(last, 0.0, dvp_ref[...])).astype(BF16)

    cur = pl.BlockSpec((BLOCK, KV_WIDTH), lambda n: (n, 0))
    nxt = pl.BlockSpec((BLOCK, KV_WIDTH), lambda n: (jnp.minimum(n + 1, nb - 1), 0))
    return pl.pallas_call(
        body, grid=(nb,), in_specs=[cur, nxt, cur, nxt],
        out_specs=pl.BlockSpec((BLOCK, 2 * KV_WIDTH), lambda n: (n, 0)),
        out_shape=jax.ShapeDtypeStruct((l, 2 * KV_WIDTH), BF16), name=name,
        compiler_params=_cp(("parallel",)))(dkc, dkp, dvc, dvp)


def _discretize(a_re, a_im, log_dt, b_re, b_im):
    dt = jnp.exp(log_dt)
    mag = jnp.exp(a_re * dt)
    ab_re = mag * jnp.cos(a_im * dt)
    ab_im = mag * jnp.sin(a_im * dt)
    nr = ab_re - 1.0
    ni = ab_im
    den = a_re * a_re + a_im * a_im
    z_re = (nr * a_re + ni * a_im) / den
    z_im = (ni * a_re - nr * a_im) / den
    bb_re = z_re * b_re - z_im * b_im
    bb_im = z_re * b_im + z_im * b_re
    return ab_re, ab_im, bb_re, bb_im


def _ssm_disc_fwd(a_re, a_im, log_dt, b_re, b_im, *, name):
    def body(ar, ai, ld, br, bi, o_ar, o_ai, o_br, o_bi):
        r = _discretize(ar[...], ai[...], ld[...], br[...], bi[...])
        o_ar[...], o_ai[...], o_br[...], o_bi[...] = r

    col = jax.ShapeDtypeStruct(a_re.shape, F32)
    mat = jax.ShapeDtypeStruct(b_re.shape, F32)
    return pl.pallas_call(body, out_shape=[col, col, mat, mat], name=name)(a_re, a_im, log_dt, b_re, b_im)


def _ssm_disc_bwd(a_re, a_im, log_dt, b_re, b_im, d_ab_re, d_ab_im, d_bb_re, d_bb_im, *, name):
    def body(ar, ai, ld, br, bi, g0, g1, g2, g3, o_ar, o_ai, o_ld, o_br, o_bi):
        _, vjp = jax.vjp(_discretize, ar[...], ai[...], ld[...], br[...], bi[...])
        r = vjp((g0[...], g1[...], g2[...], g3[...]))
        o_ar[...], o_ai[...], o_ld[...], o_br[...], o_bi[...] = r

    col = jax.ShapeDtypeStruct(a_re.shape, F32)
    mat = jax.ShapeDtypeStruct(b_re.shape, F32)
    return pl.pallas_call(body, out_shape=[col, col, col, mat, mat], name=name)(
        a_re, a_im, log_dt, b_re, b_im, d_ab_re, d_ab_im, d_bb_re, d_bb_im)


def _shift_rows(x, d, rows, *, down):
    t = x.shape[0]
    if down:
        return jnp.where(rows >= d, pltpu.roll(x, d, axis=0), 0.0)
    return jnp.where(rows < t - d, pltpu.roll(x, t - d, axis=0), 0.0)


def _scan_chunk(xr, xi, ar, ai, *, down):
    t = xr.shape[0]
    rows = lax.broadcasted_iota(jnp.int32, (t, 1), 0)
    pr, pi = ar, ai
    d = 1
    while d < t:
        sr = _shift_rows(xr, d, rows, down=down)
        si = _shift_rows(xi, d, rows, down=down)
        xr, xi = xr + pr * sr - pi * si, xi + pr * si + pi * sr
        pr, pi = pr * pr - pi * pi, 2.0 * pr * pi
        d *= 2
    return xr, xi


def _ssm_fwd(proj, ab, bd, cd, dskip, *, comm=None, name):
    l = proj.shape[0]
    t = min(SSM_CHUNK, l)
    nc = l // t

    def body(u_ref, ab_ref, bd_ref, cd_ref, ds_ref, y_ref, gy_ref, xs_ref, carry_ref):
        c = pl.program_id(1)

        @pl.when(c == 0)
        def _():
            carry_ref[...] = jnp.zeros_like(carry_ref)

        u = u_ref[...]
        ar, ai = ab_ref[0, 0:1, :], ab_ref[0, 1:2, :]
        bu = jnp.dot(u.astype(BF16), bd_ref[0], preferred_element_type=F32)
        rows = lax.broadcasted_iota(jnp.int32, (t, 1), 0)
        cr, ci = carry_ref[0:1, :], carry_ref[1:2, :]
        xr = bu[:, :SSM_X_BLK] + jnp.where(rows == 0, ar * cr - ai * ci, 0.0)
        xi = bu[:, SSM_X_BLK:] + jnp.where(rows == 0, ar * ci + ai * cr, 0.0)
        xr, xi = _scan_chunk(xr, xi, ar, ai, down=True)
        xs_ref[0, :, :SSM_X_BLK] = xr
        xs_ref[0, :, SSM_X_BLK:] = xi
        carry_ref[0:1, :] = xs_ref[0, t - 1:t, :SSM_X_BLK]
        carry_ref[1:2, :] = xs_ref[0, t - 1:t, SSM_X_BLK:]
        y = jnp.dot(xs_ref[0].astype(BF16), cd_ref[0], preferred_element_type=F32) + ds_ref[...] * u
        y_ref[...] = y
        gy_ref[...] = _gelu(y).astype(BF16)

    blk = lambda shape: pl.BlockSpec((1,) + shape, lambda j, c: (j, 0, 0))
    ycol = pl.BlockSpec((t, SSM_U_BLK), lambda j, c: (c, j))
    return _hosted_call(
        body, grid=(SSM_SPLIT, nc),
        in_specs=[pl.BlockSpec((t, SSM_U_BLK), lambda j, c: (c, COL_U + j)),
                  blk((2, SSM_X_BLK)), blk((SSM_U_BLK, 2 * SSM_X_BLK)), blk((2 * SSM_X_BLK, SSM_U_BLK)),
                  pl.BlockSpec((1, SSM_U_BLK), lambda j, c: (0, j))],
        out_specs=[ycol, ycol, pl.BlockSpec((1, t, 2 * SSM_X_BLK), lambda j, c: (j, c, 0))],
        out_shape=[jax.ShapeDtypeStruct((l, SSM_WIDTH), F32), jax.ShapeDtypeStruct((l, SSM_WIDTH), BF16),
                   jax.ShapeDtypeStruct((SSM_SPLIT, l, 2 * SSM_X_BLK), F32)],
        scratch_shapes=[pltpu.VMEM((2, SSM_X_BLK), F32)], sem=("parallel", "arbitrary"), name=name,
        args=(proj, ab, bd, cd, dskip), comm=comm)


def _ssm_bwd(proj, y, dgy, xs, ab, bdt, cdt, dskip, *, name):
    l = proj.shape[0]
    t = min(SSM_CHUNK, l)
    nc = l // t

    def body(u_ref, y_ref, dgy_ref, xs_ref, halo_ref, ab_ref, bdt_ref, cdt_ref, ds_ref,
             du_ref, dbd_ref, dcd_ref, dab_ref, dd_ref, carry_ref):
        c = pl.program_id(1)
        ci_ = nc - 1 - c

        @pl.when(c == 0)
        def _():
            carry_ref[...] = jnp.zeros_like(carry_ref)
            dbd_ref[...] = jnp.zeros_like(dbd_ref)
            dcd_ref[...] = jnp.zeros_like(dcd_ref)
            dab_ref[...] = jnp.zeros_like(dab_ref)
            dd_ref[...] = jnp.zeros_like(dd_ref)

        u = u_ref[...]
        dy = dgy_ref[...] * _gelu_grad(y_ref[...])
        dyb = dy.astype(BF16)
        ar, ai = ab_ref[0, 0:1, :], ab_ref[0, 1:2, :]
        g = jnp.dot(dyb, cdt_ref[0], preferred_element_type=F32)
        rows = lax.broadcasted_iota(jnp.int32, (t, 1), 0)
        cr, ci = carry_ref[0:1, :], carry_ref[1:2, :]
        lr = g[:, :SSM_X_BLK] + jnp.where(rows == t - 1, ar * cr + ai * ci, 0.0)
        li = g[:, SSM_X_BLK:] + jnp.where(rows == t - 1, ar * ci - ai * cr, 0.0)
        lr, li = _scan_chunk(lr, li, ar, -ai, down=False)
        lam = jnp.concatenate([lr, li], axis=1)
        carry_ref[0:1, :] = lr[0:1, :]
        carry_ref[1:2, :] = li[0:1, :]
        lamb = lam.astype(BF16)
        du_ref[...] = (jnp.dot(lamb, bdt_ref[0], preferred_element_type=F32) + ds_ref[...] * dy).astype(BF16)
        dbd_ref[0] += lax.dot_general(u.astype(BF16), lamb, (((0,), (0,)), ((), ())),
                                      preferred_element_type=F32)
        xs = xs_ref[0]
        dcd_ref[0] += lax.dot_general(xs.astype(BF16), dyb, (((0,), (0,)), ((), ())),
                                      preferred_element_type=F32)
        halo = jnp.where(ci_ > 0, halo_ref[0, 7:8, :], 0.0)
        xprev = jnp.where(rows == 0, halo, pltpu.roll(xs, 1, axis=0))
        xpr, xpi = xprev[:, :SSM_X_BLK], xprev[:, SSM_X_BLK:]
        dab_ref[0, 0:1, :] += jnp.sum(lr * xpr + li * xpi, axis=0, keepdims=True)
        dab_ref[0, 1:2, :] += jnp.sum(li * xpr - lr * xpi, axis=0, keepdims=True)
        dd_ref[...] += jnp.sum(dy * u, axis=0, keepdims=True)

    blk = lambda shape: pl.BlockSpec((1,) + shape, lambda j, c: (j, 0, 0))
    rev = lambda j, c: (nc - 1 - c, j)
    ycol = pl.BlockSpec((t, SSM_U_BLK), rev)
    hb = t // 8
    return pl.pallas_call(
        body, grid=(SSM_SPLIT, nc),
        in_specs=[pl.BlockSpec((t, SSM_U_BLK), lambda j, c: (nc - 1 - c, COL_U + j)), ycol, ycol,
                  pl.BlockSpec((1, t, 2 * SSM_X_BLK), lambda j, c: (j, nc - 1 - c, 0)),
                  pl.BlockSpec((1, 8, 2 * SSM_X_BLK),
                               lambda j, c: (j, jnp.maximum((nc - 1 - c) * hb - 1, 0), 0)),
                  blk((2, SSM_X_BLK)), blk((2 * SSM_X_BLK, SSM_U_BLK)), blk((SSM_U_BLK, 2 * SSM_X_BLK)),
                  pl.BlockSpec((1, SSM_U_BLK), lambda j, c: (0, j))],
        out_specs=[ycol, blk((SSM_U_BLK, 2 * SSM_X_BLK)), blk((2 * SSM_X_BLK, SSM_U_BLK)),
                   blk((2, SSM_X_BLK)), pl.BlockSpec((1, SSM_U_BLK), lambda j, c: (0, j))],
        out_shape=[jax.ShapeDtypeStruct((l, SSM_WIDTH), BF16),
                   jax.ShapeDtypeStruct((SSM_SPLIT, SSM_U_BLK, 2 * SSM_X_BLK), F32),
                   jax.ShapeDtypeStruct((SSM_SPLIT, 2 * SSM_X_BLK, SSM_U_BLK), F32),
                   jax.ShapeDtypeStruct((SSM_SPLIT, 2, SSM_X_BLK), F32),
                   jax.ShapeDtypeStruct((1, SSM_WIDTH), F32)],
        scratch_shapes=[pltpu.VMEM((2, SSM_X_BLK), F32)], name=name,
        compiler_params=_cp(("parallel", "arbitrary")))(proj, y, dgy, xs, xs, ab, bdt, cdt, dskip)


def _block_diag(t):
    s, g, a, b = t.shape
    return jnp.einsum('sgab,gk->sgakb', t, jnp.eye(g, dtype=t.dtype)).reshape(s, g * a, g * b)


def _block_diag_take(t, a, b):
    s = t.shape[0]
    return jnp.einsum('sgakb,gk->sgab', t.reshape(s, 8, a, 8, b), jnp.eye(8, dtype=t.dtype))


def _glu_fwd_hook(l, tm):
    def fn(z, ins, outs, i, j):
        outs[0][...] = z
        outs[1][...] = (z[:, :SSM_WIDTH] * _sigmoid(z[:, SSM_WIDTH:])).astype(BF16)

    row = lambda i, j, k: (i, 0)
    return _Hook(fn, outs=[((l, 2 * SSM_WIDTH), F32, (tm, 2 * SSM_WIDTH), row),
                           ((l, SSM_WIDTH), BF16, (tm, SSM_WIDTH), row)])


def _glu_bwd_hook(z, tm):
    l = z.shape[0]

    def fn(d, ins, outs, i, j):
        zv_ref, zg_ref = ins
        dz_ref, db_ref = outs
        sg = _sigmoid(zg_ref[...])
        dv = d * sg
        dg = d * zv_ref[...] * sg * (1.0 - sg)
        dz_ref[:, :SSM_WIDTH] = dv.astype(BF16)
        dz_ref[:, SSM_WIDTH:] = dg.astype(BF16)

        @pl.when(i == 0)
        def _():
            db_ref[...] = jnp.zeros_like(db_ref)

        db_ref[:, :SSM_WIDTH] += jnp.sum(dv, axis=0, keepdims=True)
        db_ref[:, SSM_WIDTH:] += jnp.sum(dg, axis=0, keepdims=True)

    half = (tm, SSM_WIDTH)
    return _Hook(fn, ins=[z, z], in_specs=[(half, lambda i, j, k: (i, 0)), (half, lambda i, j, k: (i, 1))],
                 outs=[((l, 2 * SSM_WIDTH), BF16, (tm, 2 * SSM_WIDTH), lambda i, j, k: (i, 0)),
                       ((1, 2 * SSM_WIDTH), F32, (1, 2 * SSM_WIDTH), lambda i, j, k: (0, 0))])


GATE_TC = 256


def _merge_fwd_hook(proj, a, s, tm):
    def fn(a_ref, ins, outs, i, k):
        ga_ref, gs_ref, a_br, s_br = ins
        mg = (_sigmoid(ga_ref[...]) * a_br[...] + _sigmoid(gs_ref[...]) * s_br[...]).astype(BF16)
        outs[0][...] = mg
        return mg

    blk = (tm, GATE_TC)
    own = lambda i, j, k: (i, k)
    return _Hook(fn, ins=[proj, proj, a, s],
                 in_specs=[(blk, lambda i, j, k: (i, COL_GA // 2 + k)), (blk, lambda i, j, k: (i, COL_GS // 2 + k)),
                           (blk, own), (blk, own)],
                 outs=[(a.shape, BF16, blk, own)])


def _merge_bwd_hook(proj, a, s, tm):
    def fn(d, ins, outs, i, j):
        ga_ref, gs_ref, a_br, s_br = ins
        sa = _sigmoid(ga_ref[...])
        ss = _sigmoid(gs_ref[...])
        outs[0][...] = (d * sa).astype(BF16)
        outs[1][...] = (d * ss).astype(BF16)
        outs[2][...] = (d * a_br[...] * sa * (1.0 - sa)).astype(BF16)
        outs[3][...] = (d * s_br[...] * ss * (1.0 - ss)).astype(BF16)

    blk = (tm, GATE_TC)
    own = lambda i, j, k: (i, j)
    return _Hook(fn, ins=[proj, proj, a, s],
                 in_specs=[(blk, lambda i, j, k: (i, COL_GA // 2 + j)), (blk, lambda i, j, k: (i, COL_GS // 2 + j)),
                           (blk, own), (blk, own)],
                 outs=[(a.shape, BF16, blk, own)] * 4)


FF_TC = D_FF // 2
FF_NJ = 2
FF_ROWS = 128


FF_HALO = 16


def _conv_taps(ext, rows):
    h = FF_HALO
    return (ext[h:h + rows], pltpu.roll(ext, 1, axis=0)[h:h + rows], pltpu.roll(ext, 2, axis=0)[h:h + rows])


def _ff_specs(tr, l):
    hb = tr // FF_HALO
    last = l // FF_HALO - 1
    prev = lambda i: jnp.maximum(i * hb - 1, 0)
    nxt = lambda i: jnp.minimum((i + 1) * hb, last)
    return dict(
        own=pl.BlockSpec((tr, FF_TC), lambda j, i: (i, j)),
        own_next=pl.BlockSpec((FF_HALO, FF_TC), lambda j, i: (nxt(i), j)),
        val=pl.BlockSpec((tr, FF_TC), lambda j, i: (i, 2 * j)),
        val_next=pl.BlockSpec((FF_HALO, FF_TC), lambda j, i: (nxt(i), 2 * j)),
        gate=pl.BlockSpec((tr, FF_TC), lambda j, i: (i, 2 * j + 1)),
        gate_prev=pl.BlockSpec((FF_HALO, FF_TC), lambda j, i: (prev(i), 2 * j + 1)),
        gate_next=pl.BlockSpec((FF_HALO, FF_TC), lambda j, i: (nxt(i), 2 * j + 1)),
        pair=pl.BlockSpec((tr, 2 * FF_TC), lambda j, i: (i, j)),
        w=pl.BlockSpec((3, FF_TC), lambda j, i: (0, j)),
        b=pl.BlockSpec((1, FF_TC), lambda j, i: (0, j)))


FF_TK = 256
FF_KH = FF_TC // FF_TK


def _ffn_act_hook(up, conv_w, conv_b, tm):
    l = up.shape[0]
    hb = tm // FF_HALO

    def fn(a_ref, ins, outs, i, k):
        v_ref, g_ref, prev_ref, w_ref, b_ref = ins
        prev = jnp.where(i == 0, 0.0, prev_ref[...].astype(F32))
        g0, g1, g2 = _conv_taps(jnp.concatenate([prev, g_ref[...].astype(F32)], axis=0), tm)
        gc = b_ref[...] + w_ref[0:1, :] * g2 + w_ref[1:2, :] * g1 + w_ref[2:3, :] * g0
        act = (v_ref[...].astype(F32) * _gelu(gc)).astype(BF16)
        outs[0][...] = act
        return act

    val = lambda k: (k // FF_KH) * 2 * FF_KH + k % FF_KH
    blk = (tm, FF_TK)
    return _Hook(fn, ins=[up, up, up, conv_w, conv_b],
                 in_specs=[(blk, lambda i, j, k: (i, val(k))), (blk, lambda i, j, k: (i, val(k) + FF_KH)),
                           ((FF_HALO, FF_TK), lambda i, j, k: (jnp.maximum(i * hb - 1, 0), val(k) + FF_KH)),
                           ((3, FF_TK), lambda i, j, k: (0, k)), ((1, FF_TK), lambda i, j, k: (0, k))],
                 outs=[((l, D_FF), BF16, blk, lambda i, j, k: (i, k))])


def _ffn_act_bwd(dact, up, conv_w, conv_b, *, comm=None, name):
    l = up.shape[0]
    tr = min(FF_ROWS, l)
    ni = l // tr
    te = tr + 8

    def body(d_ref, dn_ref, v_ref, vn_ref, g_ref, gp_ref, gn_ref, w_ref, b_ref, dup_ref, dw_ref, db_ref):
        i = pl.program_id(1)
        f32 = lambda ref, rows=None: ref[...].astype(F32)[:rows]
        prev = jnp.where(i == 0, 0.0, f32(gp_ref))
        g0, g1, g2 = _conv_taps(jnp.concatenate([prev, f32(g_ref), f32(gn_ref, 8)], axis=0), te)
        w0, w1, w2 = w_ref[0:1, :], w_ref[1:2, :], w_ref[2:3, :]
        gc = b_ref[...] + w0 * g2 + w1 * g1 + w2 * g0
        d_own = f32(d_ref)
        d = jnp.concatenate([d_own, jnp.where(i == ni - 1, 0.0, f32(dn_ref, 8))], axis=0)
        v = jnp.concatenate([f32(v_ref), f32(vn_ref, 8)], axis=0)
        dgc = d * v * _gelu_grad(gc)
        ahead1 = pltpu.roll(dgc, te - 1, axis=0)[:tr]
        ahead2 = pltpu.roll(dgc, te - 2, axis=0)[:tr]
        own = dgc[:tr]
        dup_ref[:, :FF_TC] = (d_own * _gelu(gc[:tr])).astype(BF16)
        dup_ref[:, FF_TC:] = (w2 * own + w1 * ahead1 + w0 * ahead2).astype(BF16)

        @pl.when(i == 0)
        def _():
            dw_ref[...] = jnp.zeros_like(dw_ref)
            db_ref[...] = jnp.zeros_like(db_ref)

        dw_ref[0:1, :] += jnp.sum(own * g2[:tr], axis=0, keepdims=True)
        dw_ref[1:2, :] += jnp.sum(own * g1[:tr], axis=0, keepdims=True)
        dw_ref[2:3, :] += jnp.sum(own * g0[:tr], axis=0, keepdims=True)
        db_ref[...] += jnp.sum(own, axis=0, keepdims=True)

    sp = _ff_specs(tr, l)
    return _hosted_call(
        body, grid=(FF_NJ, ni),
        in_specs=[sp['own'], sp['own_next'], sp['val'], sp['val_next'], sp['gate'], sp['gate_prev'],
                  sp['gate_next'], sp['w'], sp['b']],
        out_specs=[sp['pair'], sp['w'], sp['b']],
        out_shape=[jax.ShapeDtypeStruct((l, 2 * D_FF), BF16), jax.ShapeDtypeStruct((3, D_FF), F32),
                   jax.ShapeDtypeStruct((1, D_FF), F32)],
        scratch_shapes=[], sem=("parallel", "arbitrary"), name=name,
        args=(dact, dact, up, up, up, up, up, conv_w, conv_b), comm=comm)


def _col_sum(a, *, name):
    l, n = a.shape
    tr = min(512, l)

    def body(a_ref, o_ref):
        @pl.when(pl.program_id(0) == 0)
        def _():
            o_ref[...] = jnp.zeros_like(o_ref)

        o_ref[...] += jnp.sum(a_ref[...].astype(F32), axis=0, keepdims=True)

    return pl.pallas_call(
        body, grid=(l // tr,), in_specs=[pl.BlockSpec((tr, n), lambda i: (i, 0))],
        out_specs=pl.BlockSpec((1, n), lambda i: (0, 0)), out_shape=jax.ShapeDtypeStruct((1, n), F32),
        name=name, compiler_params=_cp(("arbitrary",)))(a)


def _local_step(x, target, wts, small, shards=None):
    l = x.shape[0]
    wts = dict(wts)
    grads, recvs, sgr = {}, {}, {}
    inter = lambda keys: [k == 'w_up_t' for k in keys]
    none = lambda keys: None
    gather = (lambda keys: _GatherPlan([shards[k] for k in keys], inter(keys))) if shards is not None else none
    scatter = (lambda keys: _ScatterPlan([grads[k] for k in keys], inter(keys))) if shards is not None else none

    mm = _matmul

    def take(res, plan, keys, store):
        outs, couts = res
        if plan is not None:
            store.update(zip(keys, couts))
        return outs

    def mm_host(keys, make_plan, store, *args, **kw):
        plan = make_plan(keys)
        if plan is None:
            return _matmul(*args, **kw)
        return take(_matmul(*args, comm=plan, **kw), plan, keys, store)

    col = lambda t: t.reshape(SSM_GROUPS * SSM_STATE, 1)
    a_re, a_im = col(small['ssm_a_re']), col(small['ssm_a_im'])
    log_dt = jnp.repeat(small['ssm_log_dt'].reshape(SSM_GROUPS), SSM_STATE).reshape(-1, 1)
    b_re = small['ssm_b_re'].reshape(SSM_GROUPS * SSM_STATE, SSM_GROUP)
    b_im = small['ssm_b_im'].reshape(SSM_GROUPS * SSM_STATE, SSM_GROUP)
    ab_re, ab_im, bb_re, bb_im = _ssm_disc_fwd(a_re, a_im, log_dt, b_re, b_im, name="ssm_disc_fwd")
    ab = jnp.stack([ab_re.reshape(SSM_SPLIT, SSM_X_BLK), ab_im.reshape(SSM_SPLIT, SSM_X_BLK)], axis=1)
    to_bd = lambda t: _block_diag(t.reshape(SSM_SPLIT, 8, SSM_STATE, SSM_GROUP).transpose(0, 1, 3, 2))
    bd = jnp.concatenate([to_bd(bb_re), to_bd(bb_im)], axis=2)
    c_re = small['ssm_c_re'].reshape(SSM_SPLIT, 8, SSM_GROUP, SSM_STATE)
    c_im = small['ssm_c_im'].reshape(SSM_SPLIT, 8, SSM_GROUP, SSM_STATE)
    cdt = jnp.concatenate([_block_diag(c_re), -_block_diag(c_im)], axis=2)
    bd_b, cdt_b = bd.astype(BF16), cdt.astype(BF16)
    bdt_b, cd_b = bd_b.transpose(0, 2, 1), cdt_b.transpose(0, 2, 1)
    dskip = small['ssm_d'].reshape(1, SSM_WIDTH)

    sinks = small['attn_sinks'].reshape(N_Q_HEADS)
    plan = gather(['w_in_t'])
    h1, = take(_rms_fwd(x, small['attn_norm_g'], comm=plan, name="rms1_fwd"), plan, ['w_in_t'], wts)
    proj = mm_host(['w_glu_t', 'w_ba_t', 'w_bs_t', 'w_out'], gather, wts,
                   h1, wts['w_in_t'], tb=True, tm=512, tn=2944, tk=2048, inner='m', out_dtype=F32,
                   bias=small['b_in'], name="mm_in")
    attn = _attn_fwd(proj, sinks, name="attn_fwd")
    plan = gather(['w_up_t'])
    y, gy, xs = take(_ssm_fwd(proj, ab, bd_b, cd_b, dskip, comm=plan, name="ssm_fwd"), plan, ['w_up_t'], wts)
    z, ssm = mm(gy, wts['w_glu_t'], tb=True, tm=1024, tn=1024, tk=512, bias=small['b_glu'],
                epilogue=_glu_fwd_hook(l, min(1024, l)), name="mm_glu")
    a_br = mm(attn, wts['w_ba_t'], tb=True, tm=1024, tn=1024, tk=1024, out_dtype=F32, name="mm_ba")
    s_br = mm(ssm, wts['w_bs_t'], tb=True, tm=1024, tn=1024, tk=512, out_dtype=F32, name="mm_bs")
    tf = min(512, l)
    merged, x2, h2 = mm(None, wts['w_out'], a_shape=(l, D_MODEL), tm=tf, tn=D_MODEL, tk=GATE_TC, res=x,
                        prologue=_merge_fwd_hook(proj, a_br, s_br, tf),
                        epilogue=_rms_fwd_hook(small['ffn_norm_g'], l, tf), name="mm_out")
    up = mm_host(['w_down'], gather, wts,
                 h2, wts['w_up_t'], tb=True, tm=1024, tn=1024, tk=2048, out_dtype=BF16, name="mm_up")
    conv_w, conv_b = small['conv_w'], small['conv_b']
    act, dx3, dx3b, d_g3, loss = mm(
        None, wts['w_down'], a_shape=(l, D_FF), tm=tf, tn=D_MODEL, tk=FF_TK, res=x2, sequential=True,
        prologue=_ffn_act_hook(up, conv_w, conv_b, tf),
        epilogue=_final_loss_hook(small['final_norm_g'].reshape(1, D_MODEL), target, tf), name="mm_down")

    sgr['final_norm_g'] = d_g3.reshape(D_MODEL)
    dact = mm(dx3b, wts['w_down'], tb=True, tm=512, tn=2816, tk=2048, inner='m', out_dtype=BF16, name="mm_dact")
    grads['w_down'] = mm(act, dx3b, ta=True, tm=512, tn=1024, tk=2048, out_dtype=BF16, name="mm_dw_down")
    plan = scatter(['w_down'])
    dup, sgr['conv_w'], sgr['conv_b'] = take(
        _ffn_act_bwd(dact, up, conv_w, conv_b, comm=plan, name="ffn_act_bwd"), plan, ['w_down'], recvs)
    grads['w_up_t'] = mm(dup, h2, ta=True, tm=1024, tn=1024, tk=2048, out_dtype=BF16, name="mm_dw_up")
    dx2, dx2b, sgr['ffn_norm_g'] = mm_host(
        ['w_up_t'], scatter, recvs, dup, wts['w_up_t'], tm=tf, tn=D_MODEL, tk=1024, sequential=True,
        epilogue=_rms_bwd_hook(x2, small['ffn_norm_g'], dx3, tf, with_bf16=True), name="mm_dh2")

    d_a, d_s, dga, dgs = mm(dx2b, wts['w_out'], tb=True, tm=1024, tn=GATE_TC, tk=2048,
                            epilogue=_merge_bwd_hook(proj, a_br, s_br, min(1024, l)), name="mm_dmerged")
    grads['w_out'] = mm(merged, dx2b, ta=True, tm=1024, tn=1024, tk=2048, out_dtype=BF16, name="mm_dw_out")
    dattn = mm(d_a, wts['w_ba_t'], tm=1024, tn=1024, tk=2048, inner='m', out_dtype=BF16, name="mm_dattn")
    grads['w_ba_t'] = mm(d_a, attn, ta=True, tm=1024, tn=1024, tk=2048, out_dtype=BF16, name="mm_dw_ba")
    dz, sgr['b_glu'] = mm(d_s, wts['w_bs_t'], tm=1024, tn=512, tk=2048, sequential=True,
                          epilogue=_glu_bwd_hook(z, min(1024, l)), name="mm_dssm")
    grads['w_bs_t'] = mm(d_s, ssm, ta=True, tm=1024, tn=512, tk=2048, out_dtype=BF16, name="mm_dw_bs")
    grads['w_glu_t'] = mm(dz, gy, ta=True, tm=1024, tn=512, tk=2048, out_dtype=BF16, name="mm_dw_glu")
    dgy = mm(dz, wts['w_glu_t'], tm=1024, tn=512, tk=1024, inner='m', out_dtype=F32, name="mm_dgy")
    du, d_bd, d_cd, d_ab, sgr['ssm_d'] = _ssm_bwd(proj, y, dgy, xs, ab, bdt_b, cdt_b, dskip, name="ssm_bwd")
    keys = ['w_out', 'w_ba_t', 'w_bs_t', 'w_glu_t']
    plan = scatter(keys)
    dq, dkc, dkp, dvc, dvp, dsink = take(_attn_bwd(proj, sinks, dattn, comm=plan, name="attn_bwd"), plan, keys, recvs)
    dkv = _kv_grad_merge(dkc, dkp, dvc, dvp, name="kv_grad_merge")
    sgr['attn_sinks'] = dsink[:, :N_Q_HEADS]
    dproj = jnp.concatenate([dq, dkv, du, dga, dgs], axis=1)
    sgr['b_in'] = _col_sum(dproj, name="col_sum_dproj")
    grads['w_in_t'] = mm(dproj, h1, ta=True, tm=2944, tn=1024, tk=1024, out_dtype=BF16, name="mm_dw_in")
    grad_x, sgr['attn_norm_g'] = mm_host(
        ['w_in_t'], scatter, recvs, dproj, wts['w_in_t'], tm=tf, tn=D_MODEL, tk=256, sequential=True,
        epilogue=_rms_bwd_hook(x, small['attn_norm_g'], dx2, tf, with_bf16=False), name="mm_dh1")

    from_bd = lambda t: _block_diag_take(t, SSM_GROUP, SSM_STATE).transpose(0, 1, 3, 2).reshape(
        SSM_GROUPS * SSM_STATE, SSM_GROUP)
    d_bb_re = from_bd(d_bd[:, :, :SSM_X_BLK])
    d_bb_im = from_bd(d_bd[:, :, SSM_X_BLK:])
    d_cdt = d_cd.transpose(0, 2, 1)
    shape_c = (1, SSM_GROUPS, SSM_GROUP, SSM_STATE)
    sgr['ssm_c_re'] = _block_diag_take(d_cdt[:, :, :SSM_X_BLK], SSM_GROUP, SSM_STATE).reshape(shape_c)
    sgr['ssm_c_im'] = -_block_diag_take(d_cdt[:, :, SSM_X_BLK:], SSM_GROUP, SSM_STATE).reshape(shape_c)
    d_a_re, d_a_im, d_ldt, d_b_re, d_b_im = _ssm_disc_bwd(
        a_re, a_im, log_dt, b_re, b_im, d_ab[:, 0, :].reshape(-1, 1), d_ab[:, 1, :].reshape(-1, 1),
        d_bb_re, d_bb_im, name="ssm_disc_bwd")
    sgr['ssm_a_re'] = d_a_re.reshape(1, SSM_GROUPS, SSM_STATE)
    sgr['ssm_a_im'] = d_a_im.reshape(1, SSM_GROUPS, SSM_STATE)
    sgr['ssm_log_dt'] = d_ldt.reshape(SSM_GROUPS, SSM_STATE).sum(axis=1).reshape(1, SSM_GROUPS)
    sgr['ssm_b_re'] = d_b_re.reshape(1, SSM_GROUPS, SSM_STATE, SSM_GROUP)
    sgr['ssm_b_im'] = d_b_im.reshape(1, SSM_GROUPS, SSM_STATE, SSM_GROUP)
    return loss, grad_x, grads, recvs, sgr


def _swap_cores(arrs, *, name):
    n = len(arrs)

    def body(*refs):
        ins, outs = refs[:n], refs[n:2 * n]
        send_sems, recv_sems = refs[2 * n:]
        x, y, c = _place()
        copies = []
        for i in range(n):
            cp = pltpu.make_async_remote_copy(
                src_ref=ins[i], dst_ref=outs[i], send_sem=send_sems.at[i], recv_sem=recv_sems.at[i],
                device_id=(x, y, 1 - c), device_id_type=MESH)
            cp.start()
            copies.append(cp)
        for cp in copies:
            cp.wait()

    return pl.pallas_call(
        body, in_specs=[ANY] * n, out_specs=[ANY] * n,
        out_shape=[jax.ShapeDtypeStruct(a.shape, a.dtype) for a in arrs],
        scratch_shapes=[pltpu.SemaphoreType.DMA((n,)), pltpu.SemaphoreType.DMA((n,))],
        name=name)(*arrs)


def _all_reduce_small(buf, *, name):
    r = buf.shape[0]

    def body(in_ref, out_ref, slots, send_sems, recv_sems):
        x, y, c = _place()
        me = 4 * x + 2 * y + c
        slots[pl.ds(me, 1)] = in_ref[...][None]
        copies = []
        for k in range(N_DEV - 1):
            bx, by, bc = ((k + 1) >> 2) & 1, ((k + 1) >> 1) & 1, (k + 1) & 1
            peer = (1 - x if bx else x, 1 - y if by else y, 1 - c if bc else c)
            cp = pltpu.make_async_remote_copy(
                src_ref=in_ref, dst_ref=slots.at[me], send_sem=send_sems.at[k], recv_sem=recv_sems.at[k],
                device_id=peer, device_id_type=MESH)
            cp.start()
            copies.append(cp)
        for cp in copies:
            cp.wait()
        acc = slots[0]
        for d in range(1, N_DEV):
            acc = acc + slots[d]
        out_ref[...] = acc

    vm = pl.BlockSpec(memory_space=pltpu.VMEM)
    return pl.pallas_call(
        body, in_specs=[vm], out_specs=vm, out_shape=jax.ShapeDtypeStruct((r, 128), F32),
        scratch_shapes=[pltpu.VMEM((N_DEV, r, 128), F32), pltpu.SemaphoreType.DMA((N_DEV - 1,)),
                        pltpu.SemaphoreType.DMA((N_DEV - 1,))],
        name=name)(buf)


def _pack(arrs):
    flat = jnp.concatenate([a.reshape(-1).astype(F32) for a in arrs])
    pad = (-flat.shape[0]) % 1024
    return jnp.pad(flat, (0, pad)).reshape(-1, 128)


def _unpack(buf, shapes):
    flat = buf.reshape(-1)
    out, pos = [], 0
    for s in shapes:
        size = math.prod(s)
        out.append(flat[pos:pos + size].reshape(s))
        pos += size
    return out


TILE_ELEMS = 256 * 1024


def _tile_rows(r, c):
    if r * c <= TILE_ELEMS:
        return r
    for tr in range(TILE_ELEMS // c // 16 * 16, 0, -16):
        if r % tr == 0:
            return tr
    raise ValueError((r, c))


def _sum4(own, recv, *, name):
    r, c = own.shape
    tr = _tile_rows(r, c)

    def body(o_ref, r_ref, out_ref):
        acc = o_ref[...].astype(F32)
        for k in range(3):
            acc = acc + r_ref[k].astype(F32)
        out_ref[...] = acc

    return pl.pallas_call(
        body, grid=(r // tr,),
        in_specs=[pl.BlockSpec((tr, c), lambda i: (i, 0)), pl.BlockSpec((3, tr, c), lambda i: (0, i, 0))],
        out_specs=pl.BlockSpec((tr, c), lambda i: (i, 0)), out_shape=jax.ShapeDtypeStruct((r, c), F32),
        name=name, compiler_params=_cp(("parallel",)))(own, recv)


def _adamw(w, ga, gb, m, v, *, name):
    r, c = w.shape
    tr = _tile_rows(r, c)
    bc1 = 1.0 - ADAM_B1 ** ADAM_STEP
    bc2 = 1.0 - ADAM_B2 ** ADAM_STEP
    two = gb is not None

    def body(*refs):
        w_ref, ga_ref = refs[0], refs[1]
        pos = 2
        g = ga_ref[...]
        if two:
            g = g + refs[pos][...]
            pos += 1
        m_ref, v_ref, g_out, d_out, m_out, v_out = refs[pos:pos + 6]
        mn = ADAM_B1 * m_ref[...] + (1.0 - ADAM_B1) * g
        vn = ADAM_B2 * v_ref[...] + (1.0 - ADAM_B2) * (g * g)
        m_hat = mn / bc1
        v_hat = vn / bc2
        g_out[...] = g
        d_out[...] = -ADAM_LR * (m_hat / (jnp.sqrt(v_hat) + ADAM_EPS) + ADAM_WD * w_ref[...])
        m_out[...] = mn
        v_out[...] = vn

    spec = pl.BlockSpec((tr, c), lambda i: (i, 0))
    args = [w, ga] + ([gb] if two else []) + [m, v]
    shp = jax.ShapeDtypeStruct((r, c), F32)
    return pl.pallas_call(
        body, grid=(r // tr,), in_specs=[spec] * len(args), out_specs=[spec] * 4,
        out_shape=[shp] * 4, name=name, compiler_params=_cp(("parallel",)))(*args)


BIG = ['w_in', 'w_glu', 'w_branch_attn', 'w_branch_ssm', 'w_out', 'w_up', 'w_down']
BIG_KEY = {'w_in': 'w_in_t', 'w_glu': 'w_glu_t', 'w_branch_attn': 'w_ba_t', 'w_branch_ssm': 'w_bs_t',
           'w_out': 'w_out', 'w_up': 'w_up_t', 'w_down': 'w_down'}
COL_SHARDED = {'w_in', 'w_glu', 'w_branch_attn', 'w_branch_ssm', 'w_up'}
SMALL = ['attn_norm_g', 'b_in', 'attn_sinks', 'ssm_a_re', 'ssm_a_im', 'ssm_log_dt', 'ssm_b_re', 'ssm_b_im',
         'ssm_c_re', 'ssm_c_im', 'ssm_d', 'b_glu', 'ffn_norm_g', 'conv_b', 'final_norm_g']
WEIGHTS = ['attn_norm_g', 'w_in', 'b_in', 'attn_sinks', 'ssm_a_re', 'ssm_a_im', 'ssm_log_dt', 'ssm_b_re',
           'ssm_b_im', 'ssm_c_re', 'ssm_c_im', 'ssm_d', 'w_glu', 'b_glu', 'w_branch_attn', 'w_branch_ssm',
           'w_out', 'ffn_norm_g', 'w_up', 'conv_w', 'conv_b', 'w_down', 'final_norm_g']


def _shard_2d(name, t):
    t = t[0]
    return t.T if name in COL_SHARDED else t


def _unshard_2d(name, t):
    return (t.T if name in COL_SHARDED else t)[None]


def kernel(x, attn_norm_g, w_in, b_in, attn_sinks, ssm_a_re, ssm_a_im, ssm_log_dt, ssm_b_re, ssm_b_im, ssm_c_re, ssm_c_im, ssm_d, w_glu, b_glu, w_branch_attn, w_branch_ssm, w_out, ffn_norm_g, w_up, conv_w, conv_b, w_down, final_norm_g, loss_target, m_attn_norm_g, m_w_in, m_b_in, m_attn_sinks, m_ssm_a_re, m_ssm_a_im, m_ssm_log_dt, m_ssm_b_re, m_ssm_b_im, m_ssm_c_re, m_ssm_c_im, m_ssm_d, m_w_glu, m_b_glu, m_w_branch_attn, m_w_branch_ssm, m_w_out, m_ffn_norm_g, m_w_up, m_conv_w, m_conv_b, m_w_down, m_final_norm_g, v_attn_norm_g, v_w_in, v_b_in, v_attn_sinks, v_ssm_a_re, v_ssm_a_im, v_ssm_log_dt, v_ssm_b_re, v_ssm_b_im, v_ssm_c_re, v_ssm_c_im, v_ssm_d, v_w_glu, v_b_glu, v_w_branch_attn, v_w_branch_ssm, v_w_out, v_ffn_norm_g, v_w_up, v_conv_w, v_conv_b, v_w_down, v_final_norm_g):
    args = dict(locals())
    w = {n: args[n] for n in WEIGHTS}
    m = {n: args['m_' + n] for n in WEIGHTS}
    v = {n: args['v_' + n] for n in WEIGHTS}
    xi, yi, ci = _place()
    blk = 2 * xi + yi

    shards = {BIG_KEY[n]: _shard_2d(n, w[n]).astype(BF16) for n in BIG}
    cw_cols = w['conv_w'].shape[2]
    cw_place = lax.dynamic_update_slice(jnp.zeros((3, D_FF), F32), w['conv_w'][0] * (ci == 0).astype(F32),
                                        (0, blk * cw_cols))
    conv_w_full = _unpack(_all_reduce_small(_pack([cw_place]), name="gather_conv_w"), [(3, D_FF)])[0]

    small = {n: w[n] for n in SMALL}
    small['conv_w'] = conv_w_full
    loss_part, grad_x, grads, recvs, sgr = _local_step(x[0], loss_target[0], {}, small, shards)

    halves = []
    for n in BIG:
        full, recv = grads[BIG_KEY[n]], recvs[BIG_KEY[n]]
        r = full.shape[0] // N_CHIPS
        own = lax.dynamic_slice_in_dim(full, _block_pos(xi, yi, n == 'w_up') * r, r, axis=0)
        halves.append(_sum4(own, recv, name="sum4_" + n))
    others = _swap_cores(halves, name="swap_cores")
    out = {}
    for n, mine, other in zip(BIG, halves, others):
        res = _adamw(_shard_2d(n, w[n]), mine, other, _shard_2d(n, m[n]), _shard_2d(n, v[n]), name="adamw_" + n)
        out[n] = [_unshard_2d(n, t) for t in res]

    names = SMALL + ['conv_w']
    shapes = [w[n].shape for n in SMALL] + [(3, D_FF)]
    packed = _pack([sgr[n] for n in names] + [loss_part])
    summed = _unpack(_all_reduce_small(packed, name="all_reduce_small"), shapes + [(1, 1)])
    loss = summed[-1].reshape(())
    sg = dict(zip(names, summed[:-1]))
    sg['conv_w'] = lax.dynamic_slice_in_dim(sg['conv_w'], blk * cw_cols, cw_cols, axis=1)[None]
    res = _adamw(_pack([w[n] for n in names]), _pack([sg[n] for n in names]), None,
                 _pack([m[n] for n in names]), _pack([v[n] for n in names]), name="adamw_small")
    ushapes = [w[n].shape for n in names]
    unpacked = [_unpack(t, ushapes) for t in res]
    for i, n in enumerate(names):
        out[n] = [u[i] for u in unpacked]

    return (loss, grad_x[None], *[out[n][0] for n in WEIGHTS], *[out[n][1] for n in WEIGHTS],
            *[out[n][2] for n in WEIGHTS], *[out[n][3] for n in WEIGHTS])
```

```python
import functools
import math

import jax
import jax.numpy as jnp
from jax import lax
from jax.experimental import pallas as pl
from jax.experimental.pallas import tpu as pltpu

F32 = jnp.float32
BF16 = jnp.bfloat16

D_MODEL = 2048
N_Q_HEADS = 16
HEAD_DIM = 64
ATTN_WIDTH = 1024
KV_WIDTH = 128
BLOCK = 128
SSM_WIDTH = 512
SSM_GROUPS = 32
SSM_GROUP = 16
SSM_STATE = 64
D_FF = 5632
IN_COLS = 5888
RMS_EPS = 1e-6
NEG_BIG = -1e30
N_CHIPS = 4
N_DEV = 8

COL_K = 8
COL_V = 9
COL_U = 10
COL_GA = 14
COL_GS = 30

SSM_SPLIT = 4
SSM_U_BLK = 128
SSM_X_BLK = 512
SSM_CHUNK = 256

ADAM_LR = 0.001
ADAM_B1 = 0.9
ADAM_B2 = 0.999
ADAM_EPS = 1e-08
ADAM_WD = 0.01
ADAM_STEP = 10

VMEM_LIMIT_BYTES = 56 * 1024 * 1024
INV_SQRT2 = 1.0 / math.sqrt(2.0)
INV_SQRT2PI = 1.0 / math.sqrt(2.0 * math.pi)
MESH = pl.DeviceIdType.MESH
ANY = pl.BlockSpec(memory_space=pl.ANY)


def _cp(sem):
    return pltpu.CompilerParams(dimension_semantics=sem, vmem_limit_bytes=VMEM_LIMIT_BYTES)


def _gelu(x):
    return 0.5 * x * (1.0 + lax.erf(x * INV_SQRT2))


def _gelu_grad(x):
    return 0.5 * (1.0 + lax.erf(x * INV_SQRT2)) + x * jnp.exp(-0.5 * x * x) * INV_SQRT2PI


def _sigmoid(x):
    return 1.0 / (1.0 + jnp.exp(-x))


def _place():
    return lax.axis_index("x"), lax.axis_index("y"), lax.axis_index("c")


def _other_chips(x, y):
    return [(1 - x, y), (x, 1 - y), (1 - x, 1 - y)]


def _block_pos(x, y, interleaved):
    return x + 2 * y if interleaved else 2 * x + y


class _GatherPlan:
    def __init__(self, shards, interleaved):
        self.arrays = list(shards)
        self.interleaved = list(interleaved)
        n = len(shards)
        self.out_shape = [jax.ShapeDtypeStruct((N_CHIPS * s.shape[0], s.shape[1]), s.dtype) for s in shards]
        self.scratch = [pltpu.SemaphoreType.DMA((6 * n,)), pltpu.SemaphoreType.DMA((6 * n,)),
                        pltpu.SemaphoreType.DMA((n,))]

    def _copies(self, kind, ins, outs, sems):
        send, recv, local = sems
        n = len(self.arrays)
        x, y, c = _place()
        copies = []
        for i in range(n):
            r = self.arrays[i].shape[0]
            h = r // 2
            blk = _block_pos(x, y, self.interleaved[i])
            if kind == 'mine':
                copies.append(pltpu.make_async_copy(ins[i], outs[i].at[pl.ds(blk * r, r), :], local.at[i]))
                continue
            for k, (px, py) in enumerate(_other_chips(x, y)):
                theirs = _block_pos(px, py, self.interleaved[i]) * r
                if kind in ('ici_out', 'ici_in'):
                    route = dict(send_sem=send.at[3 * i + k], recv_sem=recv.at[3 * i + k],
                                 device_id=(px, py, c), device_id_type=MESH)
                else:
                    route = dict(send_sem=send.at[3 * (n + i) + k], recv_sem=recv.at[3 * (n + i) + k],
                                 device_id=(x, y, 1 - c), device_id_type=MESH)
                if kind == 'ici_out':
                    src, dst = ins[i].at[pl.ds(c * h, h), :], outs[i].at[pl.ds(blk * r + c * h, h), :]
                elif kind == 'd2d_in':
                    src = dst = outs[i].at[pl.ds(theirs + (1 - c) * h, h), :]
                else:
                    src = dst = outs[i].at[pl.ds(theirs + c * h, h), :]
                copies.append(pltpu.make_async_remote_copy(src_ref=src, dst_ref=dst, **route))
        return copies

    def start(self, ins, outs, sems):
        for cp in self._copies('mine', ins, outs, sems) + self._copies('ici_out', ins, outs, sems):
            cp.start()

    def middle(self, ins, outs, sems):
        for arrived, onward in zip(self._copies('ici_in', ins, outs, sems), self._copies('d2d_out', ins, outs, sems)):
            arrived.wait_recv()
            onward.start()

    def finish(self, ins, outs, sems):
        for cp in self._copies('d2d_in', ins, outs, sems):
            cp.wait_recv()
        for cp in self._copies('ici_out', ins, outs, sems) + self._copies('d2d_out', ins, outs, sems):
            cp.wait_send()
        for cp in self._copies('mine', ins, outs, sems):
            cp.wait()


class _ScatterPlan:
    def __init__(self, fulls, interleaved):
        self.arrays = list(fulls)
        self.interleaved = list(interleaved)
        n = len(fulls)
        self.out_shape = [jax.ShapeDtypeStruct((3, f.shape[0] // N_CHIPS, f.shape[1]), f.dtype) for f in fulls]
        self.scratch = [pltpu.SemaphoreType.DMA((3 * n,)), pltpu.SemaphoreType.DMA((3 * n,))]

    def _copies(self, ins, outs, sems):
        send, recv = sems
        x, y, c = _place()
        copies = []
        for i in range(len(self.arrays)):
            r = self.arrays[i].shape[0] // N_CHIPS
            for k, (px, py) in enumerate(_other_chips(x, y)):
                copies.append(pltpu.make_async_remote_copy(
                    src_ref=ins[i].at[pl.ds(_block_pos(px, py, self.interleaved[i]) * r, r), :], dst_ref=outs[i].at[k],
                    send_sem=send.at[3 * i + k], recv_sem=recv.at[3 * i + k],
                    device_id=(px, py, c), device_id_type=MESH))
        return copies

    def start(self, ins, outs, sems):
        for cp in self._copies(ins, outs, sems):
            cp.start()

    def middle(self, ins, outs, sems):
        pass

    def finish(self, ins, outs, sems):
        for cp in self._copies(ins, outs, sems):
            cp.wait()


def _hosted_call(body, *, grid, in_specs, out_specs, out_shape, scratch_shapes, sem, name, args, comm=None,
                 aliases=None):
    aliases = aliases or {}
    if comm is None:
        outs = pl.pallas_call(body, grid=grid, in_specs=in_specs, out_specs=out_specs, out_shape=out_shape,
                              scratch_shapes=scratch_shapes, name=name, input_output_aliases=aliases,
                              compiler_params=_cp(sem))(*args)
        return outs, None
    n_in, n_out, n_scr = len(in_specs), len(out_specs), len(scratch_shapes)
    nc, ns = len(comm.arrays), len(comm.scratch)
    total = math.prod(grid)
    mid = min(total - 1, (3 * total) // 4)

    def wrapped(*refs):
        pos = 0
        ins = refs[pos:pos + n_in]; pos += n_in
        cins = refs[pos:pos + nc]; pos += nc
        outs = refs[pos:pos + n_out]; pos += n_out
        couts = refs[pos:pos + nc]; pos += nc
        scr = refs[pos:pos + n_scr]; pos += n_scr
        sems = refs[pos:pos + ns]
        step = 0
        for ax, g in enumerate(grid):
            step = step * g + pl.program_id(ax)

        @pl.when(step == 0)
        def _():
            comm.start(cins, couts, sems)

        body(*ins, *outs, *scr)

        @pl.when(step == mid)
        def _():
            comm.middle(cins, couts, sems)

        @pl.when(step == total - 1)
        def _():
            comm.finish(cins, couts, sems)

    res = pl.pallas_call(
        wrapped, grid=grid, in_specs=list(in_specs) + [ANY] * nc, out_specs=list(out_specs) + [ANY] * nc,
        out_shape=list(out_shape) + list(comm.out_shape), scratch_shapes=list(scratch_shapes) + list(comm.scratch),
        name=name, input_output_aliases=aliases,
        compiler_params=_cp(("arbitrary",) * len(grid)))(*args, *comm.arrays)
    return res[:n_out], res[n_out:]


class _Hook:
    def __init__(self, fn, ins=(), in_specs=(), outs=()):
        self.fn, self.ins, self.in_specs, self.outs = fn, list(ins), list(in_specs), list(outs)


def _matmul(a, b, *, ta=False, tb=False, tm, tn, tk, out_dtype=None, bias=None, res=None, inner='n',
            comm=None, prologue=None, epilogue=None, a_shape=None, sequential=False, name):
    if a is None:
        m, kdim = a_shape
    elif ta:
        kdim, m = a.shape
    else:
        m, kdim = a.shape
    if tb:
        n, k2 = b.shape
    else:
        k2, n = b.shape
    assert kdim == k2, (name, kdim, b.shape)
    tm, tn, tk = min(tm, m), min(tn, n), min(tk, kdim)
    assert m % tm == 0 and n % tn == 0 and kdim % tk == 0, (name, m, n, kdim, tm, tn, tk)
    nk = kdim // tk
    dn = (((0 if ta else 1,), (1 if tb else 0,)), ((), ()))
    hooks = [h for h in (prologue, epilogue) if h is not None]
    n_pro_in = len(prologue.ins) if prologue else 0
    n_epi_in = len(epilogue.ins) if epilogue else 0
    n_pro_out = len(prologue.outs) if prologue else 0
    n_epi_out = len(epilogue.outs) if epilogue else 0
    if inner == 'n':
        grid = (m // tm, n // tn, nk)
        mi = lambda g0, g1: g0
        ni = lambda g0, g1: g1
    else:
        grid = (n // tn, m // tm, nk)
        mi = lambda g0, g1: g1
        ni = lambda g0, g1: g0

    def body(*refs):
        refs = list(refs)
        take = lambda cnt: [refs.pop(0) for _ in range(cnt)]
        a_ref = take(1)[0] if a is not None else None
        b_ref = take(1)[0]
        bias_ref = take(1)[0] if bias is not None else None
        res_ref = take(1)[0] if res is not None else None
        pro_in, epi_in = take(n_pro_in), take(n_epi_in)
        o_ref = take(1)[0] if epilogue is None else None
        pro_out, epi_out = take(n_pro_out), take(n_epi_out)
        i, j, k = mi(pl.program_id(0), pl.program_id(1)), ni(pl.program_id(0), pl.program_id(1)), pl.program_id(2)

        def finish(src):
            def result(rows=slice(None)):
                r = src[rows, :]
                if bias_ref is not None:
                    r = r + bias_ref[...]
                if res_ref is not None:
                    r = r + res_ref[rows, :]
                return r

            if epilogue is None:
                o_ref[...] = result().astype(out_dtype)
            else:
                epilogue.fn(result, epi_in, epi_out, i, j)

        a_val = a_ref[...] if prologue is None else prologue.fn(a_ref, pro_in, pro_out, i, k)
        prod = lax.dot_general(a_val.astype(BF16), b_ref[...].astype(BF16), dn, preferred_element_type=F32)
        if nk == 1:
            finish(prod)
            return
        acc_ref = refs[0]

        @pl.when(k == 0)
        def _():
            acc_ref[...] = prod

        @pl.when(k > 0)
        def _():
            acc_ref[...] += prod

        @pl.when(k == nk - 1)
        def _():
            finish(acc_ref)

    spec = lambda shape, fn: pl.BlockSpec(shape, lambda g0, g1, k: fn(mi(g0, g1), ni(g0, g1), k))
    in_specs, args = [], []
    if a is not None:
        in_specs.append(spec((tk, tm), lambda i, j, k: (k, i)) if ta else spec((tm, tk), lambda i, j, k: (i, k)))
        args.append(a)
    in_specs.append(spec((tn, tk), lambda i, j, k: (j, k)) if tb else spec((tk, tn), lambda i, j, k: (k, j)))
    args.append(b)
    if bias is not None:
        in_specs.append(spec((1, tn), lambda i, j, k: (0, j)))
        args.append(bias)
    if res is not None:
        in_specs.append(spec((tm, tn), lambda i, j, k: (i, j)))
        args.append(res)
    for h in hooks:
        in_specs += [spec(shape, fn) for shape, fn in h.in_specs]
        args += h.ins
    out_specs, out_shape = [], []
    if epilogue is None:
        out_specs.append(spec((tm, tn), lambda i, j, k: (i, j)))
        out_shape.append(jax.ShapeDtypeStruct((m, n), out_dtype))
    for h in hooks:
        out_specs += [spec(blk, fn) for _, _, blk, fn in h.outs]
        out_shape += [jax.ShapeDtypeStruct(shape, dtype) for shape, dtype, _, _ in h.outs]
    outs, couts = _hosted_call(
        body, grid=grid, in_specs=in_specs, out_specs=out_specs, out_shape=out_shape,
        scratch_shapes=[pltpu.VMEM((tm, tn), F32)] if nk > 1 else [],
        sem=("arbitrary",) * 3 if sequential else ("parallel", "parallel", "arbitrary"),
        name=name, args=args, comm=comm)
    outs = outs[0] if not hooks else outs
    return outs if comm is None else (outs, couts)


def _rms_fwd(x, g, *, comm=None, name):
    l, d = x.shape
    tr = min(256, l)

    def body(x_ref, g_ref, h_ref):
        xf = x_ref[...]
        r = lax.rsqrt(jnp.mean(xf * xf, axis=-1, keepdims=True) + RMS_EPS)
        h_ref[...] = ((xf * r) * g_ref[...]).astype(BF16)

    row = pl.BlockSpec((tr, d), lambda i: (i, 0))
    return _hosted_call(
        body, grid=(l // tr,), in_specs=[row, pl.BlockSpec((1, d), lambda i: (0, 0))],
        out_specs=[row], out_shape=[jax.ShapeDtypeStruct((l, d), BF16)], scratch_shapes=[],
        sem=("parallel",), name=name, args=(x, g), comm=comm)


EPI_ROWS = 128


def _row_chunks(tm):
    ch = min(EPI_ROWS, tm)
    return [slice(c * ch, (c + 1) * ch) for c in range(tm // ch)]


def _rms_fwd_hook(g, l, tm):
    def fn(result, ins, outs, i, j):
        for rows in _row_chunks(tm):
            r = result(rows)
            outs[0][rows, :] = r
            rr = lax.rsqrt(jnp.mean(r * r, axis=-1, keepdims=True) + RMS_EPS)
            outs[1][rows, :] = ((r * rr) * ins[0][...]).astype(BF16)

    d = g.shape[1]
    row = lambda i, j, k: (i, 0)
    return _Hook(fn, ins=[g], in_specs=[((1, d), lambda i, j, k: (0, 0))],
                 outs=[((l, d), F32, (tm, d), row), ((l, d), BF16, (tm, d), row)])


def _rms_bwd_hook(x, g, dres, tm, *, with_bf16):
    def fn(result, ins, outs, i, j):
        x_ref, g_ref, dres_ref = ins
        dg_ref = outs[-1]

        @pl.when(i == 0)
        def _():
            dg_ref[...] = jnp.zeros_like(dg_ref)

        for rows in _row_chunks(tm):
            dyv = result(rows)
            xf = x_ref[rows, :]
            r = lax.rsqrt(jnp.mean(xf * xf, axis=-1, keepdims=True) + RMS_EPS)
            xhat = xf * r
            dxh = dyv * g_ref[...]
            dx = r * (dxh - xhat * jnp.mean(dxh * xhat, axis=-1, keepdims=True)) + dres_ref[rows, :]
            outs[0][rows, :] = dx
            if with_bf16:
                outs[1][rows, :] = dx.astype(BF16)
            dg_ref[...] += jnp.sum(dyv * xhat, axis=0, keepdims=True)

    l, d = x.shape
    row = lambda i, j, k: (i, 0)
    vec = lambda i, j, k: (0, 0)
    outs = [((l, d), F32, (tm, d), row)] + ([((l, d), BF16, (tm, d), row)] if with_bf16 else [])
    return _Hook(fn, ins=[x, g, dres], in_specs=[((tm, d), row), ((1, d), vec), ((tm, d), row)],
                 outs=outs + [((1, d), F32, (1, d), vec)])


def _final_loss_hook(g, target, tm):
    l, d = target.shape

    def fn(result, ins, outs, i, j):
        g_ref, t_ref = ins
        dx_ref, dxb_ref, dg_ref, loss_ref = outs
        gv = g_ref[...]

        @pl.when(i == 0)
        def _():
            dg_ref[...] = jnp.zeros_like(dg_ref)
            loss_ref[...] = jnp.zeros_like(loss_ref)

        for rows in _row_chunks(tm):
            xf = result(rows)
            r = lax.rsqrt(jnp.mean(xf * xf, axis=-1, keepdims=True) + RMS_EPS)
            xhat = xf * r
            diff = xhat * gv - t_ref[rows, :]
            dout = diff * (1.0 / d)
            dxh = dout * gv
            dx = r * (dxh - xhat * jnp.mean(dxh * xhat, axis=-1, keepdims=True))
            dx_ref[rows, :] = dx
            dxb_ref[rows, :] = dx.astype(BF16)
            dg_ref[...] += jnp.sum(dout * xhat, axis=0, keepdims=True)
            part = jnp.sum(jnp.mean(diff * diff, axis=-1, keepdims=True), axis=0, keepdims=True)
            loss_ref[...] += 0.5 * part

    row = lambda i, j, k: (i, 0)
    vec = lambda i, j, k: (0, 0)
    return _Hook(fn, ins=[g, target], in_specs=[((1, d), vec), ((tm, d), row)],
                 outs=[((l, d), F32, (tm, d), row), ((l, d), BF16, (tm, d), row),
                       ((1, d), F32, (1, d), vec), ((1, 1), F32, (1, 1), vec)])


Q_PER_KV = 8
GROUP_ROWS = Q_PER_KV * BLOCK


def _attn_masks(n, rows=GROUP_ROWS):
    q_idx = lax.broadcasted_iota(jnp.int32, (rows, 2 * BLOCK), 0) & (BLOCK - 1)
    s_idx = lax.broadcasted_iota(jnp.int32, (rows, 2 * BLOCK), 1)
    dist = q_idx + BLOCK - s_idx
    valid = (dist >= 0) & (dist < BLOCK) & ((n > 0) | (s_idx >= BLOCK))
    return dist.astype(F32), valid


def _dup_half(t, kv_head, lo):
    rolled = pltpu.roll(t, HEAD_DIM, axis=1)
    return jnp.where(lo, t, rolled) if kv_head == 0 else jnp.where(lo, rolled, t)


def _stack_heads(ref, kv_head, lo):
    pieces = []
    for r in range(Q_PER_KV):
        pair = kv_head * 4 + r // 2
        t = ref[:, pair * 128:(pair + 1) * 128].astype(BF16)
        sel = lo if r % 2 == 0 else jnp.logical_not(lo)
        pieces.append(jnp.where(sel, t, jnp.zeros_like(t)))
    return jnp.concatenate(pieces, axis=0)


def _unstack_heads(t, lo):
    return [jnp.where(lo, t[(2 * i) * BLOCK:(2 * i + 1) * BLOCK], t[(2 * i + 1) * BLOCK:(2 * i + 2) * BLOCK])
            for i in range(Q_PER_KV // 2)]


def _per_head_column(values):
    return jnp.concatenate([jnp.full((BLOCK, 1), v, F32) for v in values], axis=0)


def _group_probs(qm, kdup, dist, valid, sink_ref, kv_head):
    heads = [kv_head * Q_PER_KV + r for r in range(Q_PER_KV)]
    slope = _per_head_column([2.0 ** (-8.0 * (h + 1) / N_Q_HEADS) for h in heads])
    sink = _per_head_column([sink_ref[h] for h in heads])
    return _probs(qm, kdup, dist, valid, sink, slope)


def _probs(qm, kdup, dist, valid, sink, slope):
    s = lax.dot_general(qm, kdup, (((1,), (1,)), ((), ())), preferred_element_type=F32)
    s = s * (HEAD_DIM ** -0.5) - slope * dist
    s = jnp.where(valid, s, NEG_BIG)
    m = jnp.maximum(jnp.max(s, axis=-1, keepdims=True), sink)
    p = jnp.exp(s - m)
    esink = jnp.exp(sink - m)
    inv = 1.0 / (jnp.sum(p, axis=-1, keepdims=True) + esink)
    return p * inv, esink * inv


def _attn_fwd(proj, sinks, *, name):
    l = proj.shape[0]
    nb = l // BLOCK

    def body(sink_ref, q_ref, kc_ref, kp_ref, vc_ref, vp_ref, o_ref):
        n = pl.program_id(0)
        dist, valid = _attn_masks(n, BLOCK)
        lo = lax.broadcasted_iota(jnp.int32, (1, BLOCK), 1) < HEAD_DIM
        kx = jnp.concatenate([kp_ref[...], kc_ref[...]], axis=0).astype(BF16)
        vx = jnp.concatenate([vp_ref[...], vc_ref[...]], axis=0).astype(BF16)
        for kv_head in range(2):
            kdup = _dup_half(kx, kv_head, lo)
            vdup = _dup_half(vx, kv_head, lo)
            for pr in range(4):
                pair = kv_head * 4 + pr
                qp = q_ref[:, pair * 128:(pair + 1) * 128].astype(BF16)
                o_pair = jnp.zeros((BLOCK, 128), F32)
                for half in range(2):
                    head = 2 * pair + half
                    sel = lo if half == 0 else jnp.logical_not(lo)
                    qm = jnp.where(sel, qp, jnp.zeros_like(qp))
                    p, _ = _probs(qm, kdup, dist, valid, sink_ref[head], 2.0 ** (-8.0 * (head + 1) / N_Q_HEADS))
                    o = jnp.dot(p.astype(BF16), vdup, preferred_element_type=F32)
                    o_pair = o_pair + jnp.where(sel, o, 0.0)
                o_ref[:, pair * 128:(pair + 1) * 128] = o_pair.astype(BF16)

    kv = lambda col, prev: pl.BlockSpec(
        (BLOCK, KV_WIDTH), (lambda n: (jnp.maximum(n - 1, 0), col)) if prev else (lambda n: (n, col)))
    return pl.pallas_call(
        body, grid=(nb,),
        in_specs=[pl.BlockSpec(memory_space=pltpu.SMEM),
                  pl.BlockSpec((BLOCK, ATTN_WIDTH), lambda n: (n, 0)),
                  kv(COL_K, False), kv(COL_K, True), kv(COL_V, False), kv(COL_V, True)],
        out_specs=pl.BlockSpec((BLOCK, ATTN_WIDTH), lambda n: (n, 0)),
        out_shape=jax.ShapeDtypeStruct((l, ATTN_WIDTH), BF16), name=name,
        compiler_params=_cp(("parallel",)))(sinks, proj, proj, proj, proj, proj)


def _attn_bwd(proj, sinks, dattn, *, comm=None, name):
    l = proj.shape[0]
    nb = l // BLOCK

    def body(sink_ref, q_ref, kc_ref, kp_ref, vc_ref, vp_ref, do_ref,
             dq_ref, dkc_ref, dkp_ref, dvc_ref, dvp_ref, dsink_ref):
        n = pl.program_id(0)
        dist, valid = _attn_masks(n)
        lane = lax.broadcasted_iota(jnp.int32, (1, BLOCK), 1)
        lo = lane < HEAD_DIM
        kx = jnp.concatenate([kp_ref[...], kc_ref[...]], axis=0).astype(BF16)
        vx = jnp.concatenate([vp_ref[...], vc_ref[...]], axis=0).astype(BF16)
        dsink = jnp.zeros((1, BLOCK), F32)
        dk_heads, dv_heads = [], []
        for kv_head in range(2):
            kdup = _dup_half(kx, kv_head, lo)
            vdup = _dup_half(vx, kv_head, lo)
            qm = _stack_heads(q_ref, kv_head, lo)
            dom = _stack_heads(do_ref, kv_head, lo)
            p, psink = _group_probs(qm, kdup, dist, valid, sink_ref, kv_head)
            dp = lax.dot_general(dom, vdup, (((1,), (1,)), ((), ())), preferred_element_type=F32)
            delta = jnp.sum(p * dp, axis=-1, keepdims=True)
            ds = (p * (dp - delta) * (HEAD_DIM ** -0.5)).astype(BF16)
            dsink_rows = -psink * delta
            for r in range(Q_PER_KV):
                part = jnp.sum(dsink_rows[r * BLOCK:(r + 1) * BLOCK])
                dsink = dsink + jnp.where(lane == kv_head * Q_PER_KV + r, part, 0.0)
            dq = jnp.dot(ds, kdup, preferred_element_type=F32)
            for i, dq_pair in enumerate(_unstack_heads(dq, lo)):
                pair = kv_head * 4 + i
                dq_ref[:, pair * 128:(pair + 1) * 128] = dq_pair.astype(BF16)
            dk_acc = lax.dot_general(ds, qm, (((0,), (0,)), ((), ())), preferred_element_type=F32)
            dv_acc = lax.dot_general(p.astype(BF16), dom, (((0,), (0,)), ((), ())), preferred_element_type=F32)
            dk_heads.append(dk_acc + pltpu.roll(dk_acc, HEAD_DIM, axis=1))
            dv_heads.append(dv_acc + pltpu.roll(dv_acc, HEAD_DIM, axis=1))
        dk = jnp.where(lo, dk_heads[0], dk_heads[1])
        dv = jnp.where(lo, dv_heads[0], dv_heads[1])
        dkp_ref[...] = dk[:BLOCK]
        dkc_ref[...] = dk[BLOCK:]
        dvp_ref[...] = dv[:BLOCK]
        dvc_ref[...] = dv[BLOCK:]

        @pl.when(n == 0)
        def _():
            dsink_ref[...] = jnp.zeros_like(dsink_ref)

        dsink_ref[...] += dsink

    kv = lambda col, prev: pl.BlockSpec(
        (BLOCK, KV_WIDTH), (lambda n: (jnp.maximum(n - 1, 0), col)) if prev else (lambda n: (n, col)))
    qspec = pl.BlockSpec((BLOCK, ATTN_WIDTH), lambda n: (n, 0))
    kvout = pl.BlockSpec((BLOCK, KV_WIDTH), lambda n: (n, 0))
    kvshape = jax.ShapeDtypeStruct((l, KV_WIDTH), F32)
    return _hosted_call(
        body, grid=(nb,),
        in_specs=[pl.BlockSpec(memory_space=pltpu.SMEM), qspec,
                  kv(COL_K, False), kv(COL_K, True), kv(COL_V, False), kv(COL_V, True), qspec],
        out_specs=[qspec, kvout, kvout, kvout, kvout, pl.BlockSpec((1, BLOCK), lambda n: (0, 0))],
        out_shape=[jax.ShapeDtypeStruct((l, ATTN_WIDTH), BF16), kvshape, kvshape, kvshape, kvshape,
                   jax.ShapeDtypeStruct((1, BLOCK), F32)],
        scratch_shapes=[], sem=("arbitrary",), name=name,
        args=(sinks, proj, proj, proj, proj, proj, dattn), comm=comm)


def _kv_grad_merge(dkc, dkp, dvc, dvp, *, name):
    l = dkc.shape[0]
    nb = l // BLOCK

    def body(dkc_ref, dkp_ref, dvc_ref, dvp_ref, o_ref):
        last = pl.program_id(0) == nb - 1
        o_ref[:, :KV_WIDTH] = (dkc_ref[...] + jnp.where(last, 0.0, dkp_ref[...])).astype(BF16)
        o_ref[:, KV_WIDTH:] = (dvc_ref[...] + jnp.where(last, 0.0, dvp_ref[...])).astype(BF16)

    cur = pl.BlockSpec((BLOCK, KV_WIDTH), lambda n: (n, 0))
    nxt = pl.BlockSpec((BLOCK, KV_WIDTH), lambda n: (jnp.minimum(n + 1, nb - 1), 0))
    return pl.pallas_call(
        body, grid=(nb,), in_specs=[cur, nxt, cur, nxt],
        out_specs=pl.BlockSpec((BLOCK, 2 * KV_WIDTH), lambda n: (n, 0)),
        out_shape=jax.ShapeDtypeStruct((l, 2 * KV_WIDTH), BF16), name=name,
        compiler_params=_cp(("parallel",)))(dkc, dkp, dvc, dvp)


def _discretize(a_re, a_im, log_dt, b_re, b_im):
    dt = jnp.exp(log_dt)
    mag = jnp.exp(a_re * dt)
    ab_re = mag * jnp.cos(a_im * dt)
    ab_im = mag * jnp.sin(a_im * dt)
    nr = ab_re - 1.0
    ni = ab_im
    den = a_re * a_re + a_im * a_im
    z_re = (nr * a_re + ni * a_im) / den
    z_im = (ni * a_re - nr * a_im) / den
    bb_re = z_re * b_re - z_im * b_im
    bb_im = z_re * b_im + z_im * b_re
    return ab_re, ab_im, bb_re, bb_im


def _ssm_disc_fwd(a_re, a_im, log_dt, b_re, b_im, *, name):
    def body(ar, ai, ld, br, bi, o_ar, o_ai, o_br, o_bi):
        r = _discretize(ar[...], ai[...], ld[...], br[...], bi[...])
        o_ar[...], o_ai[...], o_br[...], o_bi[...] = r

    col = jax.ShapeDtypeStruct(a_re.shape, F32)
    mat = jax.ShapeDtypeStruct(b_re.shape, F32)
    return pl.pallas_call(body, out_shape=[col, col, mat, mat], name=name)(a_re, a_im, log_dt, b_re, b_im)


def _ssm_disc_bwd(a_re, a_im, log_dt, b_re, b_im, d_ab_re, d_ab_im, d_bb_re, d_bb_im, *, name):
    def body(ar, ai, ld, br, bi, g0, g1, g2, g3, o_ar, o_ai, o_ld, o_br, o_bi):
        _, vjp = jax.vjp(_discretize, ar[...], ai[...], ld[...], br[...], bi[...])
        r = vjp((g0[...], g1[...], g2[...], g3[...]))
        o_ar[...], o_ai[...], o_ld[...], o_br[...], o_bi[...] = r

    col = jax.ShapeDtypeStruct(a_re.shape, F32)
    mat = jax.ShapeDtypeStruct(b_re.shape, F32)
    return pl.pallas_call(body, out_shape=[col, col, col, mat, mat], name=name)(
        a_re, a_im, log_dt, b_re, b_im, d_ab_re, d_ab_im, d_bb_re, d_bb_im)


def _shift_rows(x, d, rows, *, down):
    t = x.shape[0]
    if down:
        return jnp.where(rows >= d, pltpu.roll(x, d, axis=0), 0.0)
    return jnp.where(rows < t - d, pltpu.roll(x, t - d, axis=0), 0.0)


def _scan_chunk(xr, xi, ar, ai, *, down):
    t = xr.shape[0]
    rows = lax.broadcasted_iota(jnp.int32, (t, 1), 0)
    pr, pi = ar, ai
    d = 1
    while d < t:
        sr = _shift_rows(xr, d, rows, down=down)
        si = _shift_rows(xi, d, rows, down=down)
        xr, xi = xr + pr * sr - pi * si, xi + pr * si + pi * sr
        pr, pi = pr * pr - pi * pi, 2.0 * pr * pi
        d *= 2
    return xr, xi


def _ssm_fwd(proj, ab, bd, cd, dskip, *, comm=None, name):
    l = proj.shape[0]
    t = min(SSM_CHUNK, l)
    nc = l // t

    def body(u_ref, ab_ref, bd_ref, cd_ref, ds_ref, y_ref, gy_ref, xs_ref, carry_ref):
        c = pl.program_id(1)

        @pl.when(c == 0)
        def _():
            carry_ref[...] = jnp.zeros_like(carry_ref)

        u = u_ref[...]
        ar, ai = ab_ref[0, 0:1, :], ab_ref[0, 1:2, :]
        bu = jnp.dot(u.astype(BF16), bd_ref[0], preferred_element_type=F32)
        rows = lax.broadcasted_iota(jnp.int32, (t, 1), 0)
        cr, ci = carry_ref[0:1, :], carry_ref[1:2, :]
        xr = bu[:, :SSM_X_BLK] + jnp.where(rows == 0, ar * cr - ai * ci, 0.0)
        xi = bu[:, SSM_X_BLK:] + jnp.where(rows == 0, ar * ci + ai * cr, 0.0)
        xr, xi = _scan_chunk(xr, xi, ar, ai, down=True)
        xs_ref[0, :, :SSM_X_BLK] = xr
        xs_ref[0, :, SSM_X_BLK:] = xi
        carry_ref[0:1, :] = xs_ref[0, t - 1:t, :SSM_X_BLK]
        carry_ref[1:2, :] = xs_ref[0, t - 1:t, SSM_X_BLK:]
        y = jnp.dot(xs_ref[0].astype(BF16), cd_ref[0], preferred_element_type=F32) + ds_ref[...] * u
        y_ref[...] = y
        gy_ref[...] = _gelu(y).astype(BF16)

    blk = lambda shape: pl.BlockSpec((1,) + shape, lambda j, c: (j, 0, 0))
    ycol = pl.BlockSpec((t, SSM_U_BLK), lambda j, c: (c, j))
    return _hosted_call(
        body, grid=(SSM_SPLIT, nc),
        in_specs=[pl.BlockSpec((t, SSM_U_BLK), lambda j, c: (c, COL_U + j)),
                  blk((2, SSM_X_BLK)), blk((SSM_U_BLK, 2 * SSM_X_BLK)), blk((2 * SSM_X_BLK, SSM_U_BLK)),
                  pl.BlockSpec((1, SSM_U_BLK), lambda j, c: (0, j))],
        out_specs=[ycol, ycol, pl.BlockSpec((1, t, 2 * SSM_X_BLK), lambda j, c: (j, c, 0))],
        out_shape=[jax.ShapeDtypeStruct((l, SSM_WIDTH), F32), jax.ShapeDtypeStruct((l, SSM_WIDTH), BF16),
                   jax.ShapeDtypeStruct((SSM_SPLIT, l, 2 * SSM_X_BLK), F32)],
        scratch_shapes=[pltpu.VMEM((2, SSM_X_BLK), F32)], sem=("parallel", "arbitrary"), name=name,
        args=(proj, ab, bd, cd, dskip), comm=comm)


def _ssm_bwd(proj, y, dgy, xs, ab, bdt, cdt, dskip, *, comm=None, name):
    l = proj.shape[0]
    t = min(SSM_CHUNK, l)
    nc = l // t

    def body(u_ref, y_ref, dgy_ref, xs_ref, halo_ref, ab_ref, bdt_ref, cdt_ref, ds_ref,
             du_ref, dbd_ref, dcd_ref, dab_ref, dd_ref, carry_ref):
        c = pl.program_id(1)
        ci_ = nc - 1 - c

        @pl.when(c == 0)
        def _():
            carry_ref[...] = jnp.zeros_like(carry_ref)
            dbd_ref[...] = jnp.zeros_like(dbd_ref)
            dcd_ref[...] = jnp.zeros_like(dcd_ref)
            dab_ref[...] = jnp.zeros_like(dab_ref)
            dd_ref[...] = jnp.zeros_like(dd_ref)

        u = u_ref[...]
        dy = dgy_ref[...] * _gelu_grad(y_ref[...])
        dyb = dy.astype(BF16)
        ar, ai = ab_ref[0, 0:1, :], ab_ref[0, 1:2, :]
        g = jnp.dot(dyb, cdt_ref[0], preferred_element_type=F32)
        rows = lax.broadcasted_iota(jnp.int32, (t, 1), 0)
        cr, ci = carry_ref[0:1, :], carry_ref[1:2, :]
        lr = g[:, :SSM_X_BLK] + jnp.where(rows == t - 1, ar * cr + ai * ci, 0.0)
        li = g[:, SSM_X_BLK:] + jnp.where(rows == t - 1, ar * ci - ai * cr, 0.0)
        lr, li = _scan_chunk(lr, li, ar, -ai, down=False)
        lam = jnp.concatenate([lr, li], axis=1)
        carry_ref[0:1, :] = lr[0:1, :]
        carry_ref[1:2, :] = li[0:1, :]
        lamb = lam.astype(BF16)
        du_ref[...] = (jnp.dot(lamb, bdt_ref[0], preferred_element_type=F32) + ds_ref[...] * dy).astype(BF16)
        dbd_ref[0] += lax.dot_general(u.astype(BF16), lamb, (((0,), (0,)), ((), ())),
                                      preferred_element_type=F32)
        xs = xs_ref[0]
        dcd_ref[0] += lax.dot_general(xs.astype(BF16), dyb, (((0,), (0,)), ((), ())),
                                      preferred_element_type=F32)
        halo = jnp.where(ci_ > 0, halo_ref[0, 7:8, :], 0.0)
        xprev = jnp.where(rows == 0, halo, pltpu.roll(xs, 1, axis=0))
        xpr, xpi = xprev[:, :SSM_X_BLK], xprev[:, SSM_X_BLK:]
        dab_ref[0, 0:1, :] += jnp.sum(lr * xpr + li * xpi, axis=0, keepdims=True)
        dab_ref[0, 1:2, :] += jnp.sum(li * xpr - lr * xpi, axis=0, keepdims=True)
        dd_ref[...] += jnp.sum(dy * u, axis=0, keepdims=True)

    blk = lambda shape: pl.BlockSpec((1,) + shape, lambda j, c: (j, 0, 0))
    rev = lambda j, c: (nc - 1 - c, j)
    ycol = pl.BlockSpec((t, SSM_U_BLK), rev)
    hb = t // 8
    return _hosted_call(
        body, grid=(SSM_SPLIT, nc), comm=comm, sem=("parallel", "arbitrary"), name=name,
        args=(proj, y, dgy, xs, xs, ab, bdt, cdt, dskip), scratch_shapes=[pltpu.VMEM((2, SSM_X_BLK), F32)],
        in_specs=[pl.BlockSpec((t, SSM_U_BLK), lambda j, c: (nc - 1 - c, COL_U + j)), ycol, ycol,
                  pl.BlockSpec((1, t, 2 * SSM_X_BLK), lambda j, c: (j, nc - 1 - c, 0)),
                  pl.BlockSpec((1, 8, 2 * SSM_X_BLK),
                               lambda j, c: (j, jnp.maximum((nc - 1 - c) * hb - 1, 0), 0)),
                  blk((2, SSM_X_BLK)), blk((2 * SSM_X_BLK, SSM_U_BLK)), blk((SSM_U_BLK, 2 * SSM_X_BLK)),
                  pl.BlockSpec((1, SSM_U_BLK), lambda j, c: (0, j))],
        out_specs=[ycol, blk((SSM_U_BLK, 2 * SSM_X_BLK)), blk((2 * SSM_X_BLK, SSM_U_BLK)),
                   blk((2, SSM_X_BLK)), pl.BlockSpec((1, SSM_U_BLK), lambda j, c: (0, j))],
        out_shape=[jax.ShapeDtypeStruct((l, SSM_WIDTH), BF16),
                   jax.ShapeDtypeStruct((SSM_SPLIT, SSM_U_BLK, 2 * SSM_X_BLK), F32),
                   jax.ShapeDtypeStruct((SSM_SPLIT, 2 * SSM_X_BLK, SSM_U_BLK), F32),
                   jax.ShapeDtypeStruct((SSM_SPLIT, 2, SSM_X_BLK), F32),
                   jax.ShapeDtypeStruct((1, SSM_WIDTH), F32)])


def _block_diag(t):
    s, g, a, b = t.shape
    return jnp.einsum('sgab,gk->sgakb', t, jnp.eye(g, dtype=t.dtype)).reshape(s, g * a, g * b)


def _block_diag_take(t, a, b):
    s = t.shape[0]
    return jnp.einsum('sgakb,gk->sgab', t.reshape(s, 8, a, 8, b), jnp.eye(8, dtype=t.dtype))


def _glu_fwd_hook(l, tm):
    def fn(result, ins, outs, i, j):
        z = result()
        outs[0][...] = z
        outs[1][...] = (z[:, :SSM_WIDTH] * _sigmoid(z[:, SSM_WIDTH:])).astype(BF16)

    row = lambda i, j, k: (i, 0)
    return _Hook(fn, outs=[((l, 2 * SSM_WIDTH), F32, (tm, 2 * SSM_WIDTH), row),
                           ((l, SSM_WIDTH), BF16, (tm, SSM_WIDTH), row)])


def _glu_bwd_hook(z, tm):
    l = z.shape[0]

    def fn(result, ins, outs, i, j):
        zv_ref, zg_ref = ins
        dz_ref, db_ref = outs
        d = result()
        sg = _sigmoid(zg_ref[...])
        dv = d * sg
        dg = d * zv_ref[...] * sg * (1.0 - sg)
        dz_ref[:, :SSM_WIDTH] = dv.astype(BF16)
        dz_ref[:, SSM_WIDTH:] = dg.astype(BF16)

        @pl.when(i == 0)
        def _():
            db_ref[...] = jnp.zeros_like(db_ref)

        db_ref[:, :SSM_WIDTH] += jnp.sum(dv, axis=0, keepdims=True)
        db_ref[:, SSM_WIDTH:] += jnp.sum(dg, axis=0, keepdims=True)

    half = (tm, SSM_WIDTH)
    return _Hook(fn, ins=[z, z], in_specs=[(half, lambda i, j, k: (i, 0)), (half, lambda i, j, k: (i, 1))],
                 outs=[((l, 2 * SSM_WIDTH), BF16, (tm, 2 * SSM_WIDTH), lambda i, j, k: (i, 0)),
                       ((1, 2 * SSM_WIDTH), F32, (1, 2 * SSM_WIDTH), lambda i, j, k: (0, 0))])


GATE_TC = 256


MERGE_SUB = 4


def _merge_fwd_hook(proj, a, s, tm):
    q = MERGE_SUB

    def fn(a_ref, ins, outs, i, k):
        ga = jnp.concatenate([r[...] for r in ins[:q]], axis=1)
        gs = jnp.concatenate([r[...] for r in ins[q:2 * q]], axis=1)
        a_br, s_br = ins[2 * q], ins[2 * q + 1]
        mg = (_sigmoid(ga) * a_br[...] + _sigmoid(gs) * s_br[...]).astype(BF16)
        outs[0][...] = mg
        return mg

    blk = (tm, GATE_TC)
    wide = (tm, q * GATE_TC)
    own = lambda i, j, k: (i, k)
    gate = lambda col, sub: (blk, lambda i, j, k: (i, col // 2 + q * k + sub))
    return _Hook(fn, ins=[proj] * (2 * q) + [a, s],
                 in_specs=[gate(COL_GA, sub) for sub in range(q)] + [gate(COL_GS, sub) for sub in range(q)]
                 + [(wide, own), (wide, own)],
                 outs=[(a.shape, BF16, wide, own)])


def _merge_bwd_hook(proj, a, s, tm):
    def fn(result, ins, outs, i, j):
        ga_ref, gs_ref, a_br, s_br = ins
        d = result()
        sa = _sigmoid(ga_ref[...])
        ss = _sigmoid(gs_ref[...])
        outs[0][...] = (d * sa).astype(BF16)
        outs[1][...] = (d * ss).astype(BF16)
        outs[2][...] = (d * a_br[...] * sa * (1.0 - sa)).astype(BF16)
        outs[3][...] = (d * s_br[...] * ss * (1.0 - ss)).astype(BF16)

    blk = (tm, GATE_TC)
    own = lambda i, j, k: (i, j)
    return _Hook(fn, ins=[proj, proj, a, s],
                 in_specs=[(blk, lambda i, j, k: (i, COL_GA // 2 + j)), (blk, lambda i, j, k: (i, COL_GS // 2 + j)),
                           (blk, own), (blk, own)],
                 outs=[(a.shape, BF16, blk, own)] * 4)


FF_TC = D_FF // 2
FF_NJ = 2
FF_ROWS = 128


FF_HALO = 16


def _conv_taps(ext, rows):
    h = FF_HALO
    return (ext[h:h + rows], pltpu.roll(ext, 1, axis=0)[h:h + rows], pltpu.roll(ext, 2, axis=0)[h:h + rows])


def _ff_specs(tr, l):
    hb = tr // FF_HALO
    last = l // FF_HALO - 1
    prev = lambda i: jnp.maximum(i * hb - 1, 0)
    nxt = lambda i: jnp.minimum((i + 1) * hb, last)
    return dict(
        own=pl.BlockSpec((tr, FF_TC), lambda j, i: (i, j)),
        own_next=pl.BlockSpec((FF_HALO, FF_TC), lambda j, i: (nxt(i), j)),
        val=pl.BlockSpec((tr, FF_TC), lambda j, i: (i, 2 * j)),
        val_next=pl.BlockSpec((FF_HALO, FF_TC), lambda j, i: (nxt(i), 2 * j)),
        gate=pl.BlockSpec((tr, FF_TC), lambda j, i: (i, 2 * j + 1)),
        gate_prev=pl.BlockSpec((FF_HALO, FF_TC), lambda j, i: (prev(i), 2 * j + 1)),
        gate_next=pl.BlockSpec((FF_HALO, FF_TC), lambda j, i: (nxt(i), 2 * j + 1)),
        pair=pl.BlockSpec((tr, 2 * FF_TC), lambda j, i: (i, j)),
        w=pl.BlockSpec((3, FF_TC), lambda j, i: (0, j)),
        b=pl.BlockSpec((1, FF_TC), lambda j, i: (0, j)))


def _ffn_act_fwd(up, conv_w, conv_b, *, name):
    l = up.shape[0]
    tr = min(FF_ROWS, l)

    def body(v_ref, g_ref, prev_ref, w_ref, b_ref, o_ref):
        prev = jnp.where(pl.program_id(1) == 0, 0.0, prev_ref[...].astype(F32))
        g0, g1, g2 = _conv_taps(jnp.concatenate([prev, g_ref[...].astype(F32)], axis=0), tr)
        gc = b_ref[...] + w_ref[0:1, :] * g2 + w_ref[1:2, :] * g1 + w_ref[2:3, :] * g0
        o_ref[...] = (v_ref[...].astype(F32) * _gelu(gc)).astype(BF16)

    sp = _ff_specs(tr, l)
    return pl.pallas_call(
        body, grid=(FF_NJ, l // tr), in_specs=[sp['val'], sp['gate'], sp['gate_prev'], sp['w'], sp['b']],
        out_specs=sp['own'], out_shape=jax.ShapeDtypeStruct((l, D_FF), BF16), name=name,
        compiler_params=_cp(("parallel", "parallel")))(up, up, up, conv_w, conv_b)


def _ffn_act_bwd(dact, up, conv_w, conv_b, *, comm=None, name):
    l = up.shape[0]
    tr = min(FF_ROWS, l)
    ni = l // tr
    te = tr + 8

    def body(d_ref, dn_ref, v_ref, vn_ref, g_ref, gp_ref, gn_ref, w_ref, b_ref, dup_ref, dw_ref, db_ref):
        i = pl.program_id(1)
        f32 = lambda ref, rows=None: ref[...].astype(F32)[:rows]
        prev = jnp.where(i == 0, 0.0, f32(gp_ref))
        g0, g1, g2 = _conv_taps(jnp.concatenate([prev, f32(g_ref), f32(gn_ref, 8)], axis=0), te)
        w0, w1, w2 = w_ref[0:1, :], w_ref[1:2, :], w_ref[2:3, :]
        gc = b_ref[...] + w0 * g2 + w1 * g1 + w2 * g0
        d_own = f32(d_ref)
        d = jnp.concatenate([d_own, jnp.where(i == ni - 1, 0.0, f32(dn_ref, 8))], axis=0)
        v = jnp.concatenate([f32(v_ref), f32(vn_ref, 8)], axis=0)
        dgc = d * v * _gelu_grad(gc)
        ahead1 = pltpu.roll(dgc, te - 1, axis=0)[:tr]
        ahead2 = pltpu.roll(dgc, te - 2, axis=0)[:tr]
        own = dgc[:tr]
        dup_ref[:, :FF_TC] = (d_own * _gelu(gc[:tr])).astype(BF16)
        dup_ref[:, FF_TC:] = (w2 * own + w1 * ahead1 + w0 * ahead2).astype(BF16)

        @pl.when(i == 0)
        def _():
            dw_ref[...] = jnp.zeros_like(dw_ref)
            db_ref[...] = jnp.zeros_like(db_ref)

        dw_ref[0:1, :] += jnp.sum(own * g2[:tr], axis=0, keepdims=True)
        dw_ref[1:2, :] += jnp.sum(own * g1[:tr], axis=0, keepdims=True)
        dw_ref[2:3, :] += jnp.sum(own * g0[:tr], axis=0, keepdims=True)
        db_ref[...] += jnp.sum(own, axis=0, keepdims=True)

    sp = _ff_specs(tr, l)
    return _hosted_call(
        body, grid=(FF_NJ, ni),
        in_specs=[sp['own'], sp['own_next'], sp['val'], sp['val_next'], sp['gate'], sp['gate_prev'],
                  sp['gate_next'], sp['w'], sp['b']],
        out_specs=[sp['pair'], sp['w'], sp['b']],
        out_shape=[jax.ShapeDtypeStruct((l, 2 * D_FF), BF16), jax.ShapeDtypeStruct((3, D_FF), F32),
                   jax.ShapeDtypeStruct((1, D_FF), F32)],
        scratch_shapes=[], sem=("parallel", "arbitrary"), name=name,
        args=(dact, dact, up, up, up, up, up, conv_w, conv_b), comm=comm)


def _col_sum(a, *, name):
    l, n = a.shape
    tr = min(512, l)

    def body(a_ref, o_ref):
        @pl.when(pl.program_id(0) == 0)
        def _():
            o_ref[...] = jnp.zeros_like(o_ref)

        o_ref[...] += jnp.sum(a_ref[...].astype(F32), axis=0, keepdims=True)

    return pl.pallas_call(
        body, grid=(l // tr,), in_specs=[pl.BlockSpec((tr, n), lambda i: (i, 0))],
        out_specs=pl.BlockSpec((1, n), lambda i: (0, 0)), out_shape=jax.ShapeDtypeStruct((1, n), F32),
        name=name, compiler_params=_cp(("arbitrary",)))(a)


def _local_step(x, target, wts, small, shards=None):
    l = x.shape[0]
    wts = dict(wts)
    grads, recvs, sgr = {}, {}, {}
    inter = lambda keys: [k == 'w_up_t' for k in keys]
    none = lambda keys: None
    gather = (lambda keys: _GatherPlan([shards[k] for k in keys], inter(keys))) if shards is not None else none
    scatter = (lambda keys: _ScatterPlan([grads[k] for k in keys], inter(keys))) if shards is not None else none

    mm = _matmul

    def take(res, plan, keys, store):
        outs, couts = res
        if plan is not None:
            store.update(zip(keys, couts))
        return outs

    def mm_host(keys, make_plan, store, *args, **kw):
        plan = make_plan(keys)
        if plan is None:
            return _matmul(*args, **kw)
        return take(_matmul(*args, comm=plan, **kw), plan, keys, store)

    col = lambda t: t.reshape(SSM_GROUPS * SSM_STATE, 1)
    a_re, a_im = col(small['ssm_a_re']), col(small['ssm_a_im'])
    log_dt = jnp.repeat(small['ssm_log_dt'].reshape(SSM_GROUPS), SSM_STATE).reshape(-1, 1)
    b_re = small['ssm_b_re'].reshape(SSM_GROUPS * SSM_STATE, SSM_GROUP)
    b_im = small['ssm_b_im'].reshape(SSM_GROUPS * SSM_STATE, SSM_GROUP)
    ab_re, ab_im, bb_re, bb_im = _ssm_disc_fwd(a_re, a_im, log_dt, b_re, b_im, name="ssm_disc_fwd")
    ab = jnp.stack([ab_re.reshape(SSM_SPLIT, SSM_X_BLK), ab_im.reshape(SSM_SPLIT, SSM_X_BLK)], axis=1)
    to_bd = lambda t: _block_diag(t.reshape(SSM_SPLIT, 8, SSM_STATE, SSM_GROUP).transpose(0, 1, 3, 2))
    bd = jnp.concatenate([to_bd(bb_re), to_bd(bb_im)], axis=2)
    c_re = small['ssm_c_re'].reshape(SSM_SPLIT, 8, SSM_GROUP, SSM_STATE)
    c_im = small['ssm_c_im'].reshape(SSM_SPLIT, 8, SSM_GROUP, SSM_STATE)
    cdt = jnp.concatenate([_block_diag(c_re), -_block_diag(c_im)], axis=2)
    bd_b, cdt_b = bd.astype(BF16), cdt.astype(BF16)
    bdt_b, cd_b = bd_b.transpose(0, 2, 1), cdt_b.transpose(0, 2, 1)
    dskip = small['ssm_d'].reshape(1, SSM_WIDTH)

    sinks = small['attn_sinks'].reshape(N_Q_HEADS)
    plan = gather(['w_in_t'])
    h1, = take(_rms_fwd(x, small['attn_norm_g'], comm=plan, name="rms1_fwd"), plan, ['w_in_t'], wts)
    proj = mm_host(['w_glu_t', 'w_ba_t', 'w_bs_t', 'w_out'], gather, wts,
                   h1, wts['w_in_t'], tb=True, tm=512, tn=2944, tk=2048, inner='m', out_dtype=F32,
                   bias=small['b_in'], name="mm_in")
    attn = _attn_fwd(proj, sinks, name="attn_fwd")
    plan = gather(['w_up_t'])
    y, gy, xs = take(_ssm_fwd(proj, ab, bd_b, cd_b, dskip, comm=plan, name="ssm_fwd"), plan, ['w_up_t'], wts)
    z, ssm = mm(gy, wts['w_glu_t'], tb=True, tm=1024, tn=1024, tk=512, bias=small['b_glu'],
                epilogue=_glu_fwd_hook(l, min(1024, l)), name="mm_glu")
    a_br = mm(attn, wts['w_ba_t'], tb=True, tm=1024, tn=1024, tk=1024, out_dtype=F32, name="mm_ba")
    s_br = mm(ssm, wts['w_bs_t'], tb=True, tm=1024, tn=1024, tk=512, out_dtype=F32, name="mm_bs")
    tf = min(512, l)
    to = min(256, l)
    merged, x2, h2 = mm(None, wts['w_out'], a_shape=(l, D_MODEL), tm=to, tn=D_MODEL, tk=MERGE_SUB * GATE_TC, res=x,
                        prologue=_merge_fwd_hook(proj, a_br, s_br, to),
                        epilogue=_rms_fwd_hook(small['ffn_norm_g'], l, to), name="mm_out")
    up = mm_host(['w_down'], gather, wts,
                 h2, wts['w_up_t'], tb=True, tm=1024, tn=1024, tk=2048, out_dtype=BF16, name="mm_up")
    conv_w, conv_b = small['conv_w'], small['conv_b']
    act = _ffn_act_fwd(up, conv_w, conv_b, name="ffn_act_fwd")
    dx3, dx3b, d_g3, loss = mm(
        act, wts['w_down'], tm=tf, tn=D_MODEL, tk=512, res=x2, sequential=True,
        epilogue=_final_loss_hook(small['final_norm_g'].reshape(1, D_MODEL), target, tf), name="mm_down")

    sgr['final_norm_g'] = d_g3.reshape(D_MODEL)
    dact = mm(dx3b, wts['w_down'], tb=True, tm=512, tn=2816, tk=2048, inner='m', out_dtype=BF16, name="mm_dact")
    grads['w_down'] = mm(act, dx3b, ta=True, tm=512, tn=1024, tk=2048, out_dtype=BF16, name="mm_dw_down")
    plan = scatter(['w_down'])
    dup, sgr['conv_w'], sgr['conv_b'] = take(
        _ffn_act_bwd(dact, up, conv_w, conv_b, comm=plan, name="ffn_act_bwd"), plan, ['w_down'], recvs)
    grads['w_up_t'] = mm(dup, h2, ta=True, tm=1024, tn=1024, tk=2048, out_dtype=BF16, name="mm_dw_up")
    dx2, dx2b, sgr['ffn_norm_g'] = mm(
        dup, wts['w_up_t'], tm=tf, tn=D_MODEL, tk=1024, sequential=True,
        epilogue=_rms_bwd_hook(x2, small['ffn_norm_g'], dx3, tf, with_bf16=True), name="mm_dh2")

    d_a, d_s, dga, dgs = mm(dx2b, wts['w_out'], tb=True, tm=1024, tn=GATE_TC, tk=2048,
                            epilogue=_merge_bwd_hook(proj, a_br, s_br, min(1024, l)), name="mm_dmerged")
    grads['w_out'] = mm(merged, dx2b, ta=True, tm=1024, tn=1024, tk=2048, out_dtype=BF16, name="mm_dw_out")
    dattn = mm(d_a, wts['w_ba_t'], tm=1024, tn=1024, tk=2048, inner='m', out_dtype=BF16, name="mm_dattn")
    grads['w_ba_t'] = mm(d_a, attn, ta=True, tm=1024, tn=1024, tk=2048, out_dtype=BF16, name="mm_dw_ba")
    dz, sgr['b_glu'] = mm(d_s, wts['w_bs_t'], tm=1024, tn=512, tk=2048, sequential=True,
                          epilogue=_glu_bwd_hook(z, min(1024, l)), name="mm_dssm")
    grads['w_bs_t'] = mm(d_s, ssm, ta=True, tm=1024, tn=512, tk=2048, out_dtype=BF16, name="mm_dw_bs")
    grads['w_glu_t'] = mm(dz, gy, ta=True, tm=1024, tn=512, tk=2048, out_dtype=BF16, name="mm_dw_glu")
    dgy = mm(dz, wts['w_glu_t'], tm=1024, tn=512, tk=1024, inner='m', out_dtype=F32, name="mm_dgy")
    keys = ['w_out', 'w_ba_t', 'w_bs_t', 'w_glu_t']
    plan = scatter(keys)
    du, d_bd, d_cd, d_ab, sgr['ssm_d'] = take(
        _ssm_bwd(proj, y, dgy, xs, ab, bdt_b, cdt_b, dskip, comm=plan, name="ssm_bwd"), plan, keys, recvs)
    plan = scatter(['w_up_t'])
    dq, dkc, dkp, dvc, dvp, dsink = take(
        _attn_bwd(proj, sinks, dattn, comm=plan, name="attn_bwd"), plan, ['w_up_t'], recvs)
    dkv = _kv_grad_merge(dkc, dkp, dvc, dvp, name="kv_grad_merge")
    sgr['attn_sinks'] = dsink[:, :N_Q_HEADS]
    dproj = jnp.concatenate([dq, dkv, du, dga, dgs], axis=1)
    sgr['b_in'] = _col_sum(dproj, name="col_sum_dproj")
    grads['w_in_t'] = mm(dproj, h1, ta=True, tm=2944, tn=1024, tk=1024, out_dtype=BF16, name="mm_dw_in")
    th = min(256, l)
    grad_x, sgr['attn_norm_g'] = mm_host(
        ['w_in_t'], scatter, recvs, dproj, wts['w_in_t'], tm=th, tn=D_MODEL, tk=2944, sequential=True,
        epilogue=_rms_bwd_hook(x, small['attn_norm_g'], dx2, th, with_bf16=False), name="mm_dh1")

    from_bd = lambda t: _block_diag_take(t, SSM_GROUP, SSM_STATE).transpose(0, 1, 3, 2).reshape(
        SSM_GROUPS * SSM_STATE, SSM_GROUP)
    d_bb_re = from_bd(d_bd[:, :, :SSM_X_BLK])
    d_bb_im = from_bd(d_bd[:, :, SSM_X_BLK:])
    d_cdt = d_cd.transpose(0, 2, 1)
    shape_c = (1, SSM_GROUPS, SSM_GROUP, SSM_STATE)
    sgr['ssm_c_re'] = _block_diag_take(d_cdt[:, :, :SSM_X_BLK], SSM_GROUP, SSM_STATE).reshape(shape_c)
    sgr['ssm_c_im'] = -_block_diag_take(d_cdt[:, :, SSM_X_BLK:], SSM_GROUP, SSM_STATE).reshape(shape_c)
    d_a_re, d_a_im, d_ldt, d_b_re, d_b_im = _ssm_disc_bwd(
        a_re, a_im, log_dt, b_re, b_im, d_ab[:, 0, :].reshape(-1, 1), d_ab[:, 1, :].reshape(-1, 1),
        d_bb_re, d_bb_im, name="ssm_disc_bwd")
    sgr['ssm_a_re'] = d_a_re.reshape(1, SSM_GROUPS, SSM_STATE)
    sgr['ssm_a_im'] = d_a_im.reshape(1, SSM_GROUPS, SSM_STATE)
    sgr['ssm_log_dt'] = d_ldt.reshape(SSM_GROUPS, SSM_STATE).sum(axis=1).reshape(1, SSM_GROUPS)
    sgr['ssm_b_re'] = d_b_re.reshape(1, SSM_GROUPS, SSM_STATE, SSM_GROUP)
    sgr['ssm_b_im'] = d_b_im.reshape(1, SSM_GROUPS, SSM_STATE, SSM_GROUP)
    return loss, grad_x, grads, recvs, sgr


def _swap_cores(arrs, *, name):
    n = len(arrs)

    def body(*refs):
        ins, outs = refs[:n], refs[n:2 * n]
        send_sems, recv_sems = refs[2 * n:]
        x, y, c = _place()
        copies = []
        for i in range(n):
            cp = pltpu.make_async_remote_copy(
                src_ref=ins[i], dst_ref=outs[i], send_sem=send_sems.at[i], recv_sem=recv_sems.at[i],
                device_id=(x, y, 1 - c), device_id_type=MESH)
            cp.start()
            copies.append(cp)
        for cp in copies:
            cp.wait()

    return pl.pallas_call(
        body, in_specs=[ANY] * n, out_specs=[ANY] * n,
        out_shape=[jax.ShapeDtypeStruct(a.shape, a.dtype) for a in arrs],
        scratch_shapes=[pltpu.SemaphoreType.DMA((n,)), pltpu.SemaphoreType.DMA((n,))],
        name=name)(*arrs)


def _all_reduce_small(buf, *, name):
    r = buf.shape[0]

    def body(in_ref, out_ref, slots, send_sems, recv_sems):
        x, y, c = _place()
        me = 4 * x + 2 * y + c
        slots[pl.ds(me, 1)] = in_ref[...][None]
        copies = []
        for k in range(N_DEV - 1):
            bx, by, bc = ((k + 1) >> 2) & 1, ((k + 1) >> 1) & 1, (k + 1) & 1
            peer = (1 - x if bx else x, 1 - y if by else y, 1 - c if bc else c)
            cp = pltpu.make_async_remote_copy(
                src_ref=in_ref, dst_ref=slots.at[me], send_sem=send_sems.at[k], recv_sem=recv_sems.at[k],
                device_id=peer, device_id_type=MESH)
            cp.start()
            copies.append(cp)
        for cp in copies:
            cp.wait()
        acc = slots[0]
        for d in range(1, N_DEV):
            acc = acc + slots[d]
        out_ref[...] = acc

    vm = pl.BlockSpec(memory_space=pltpu.VMEM)
    return pl.pallas_call(
        body, in_specs=[vm], out_specs=vm, out_shape=jax.ShapeDtypeStruct((r, 128), F32),
        scratch_shapes=[pltpu.VMEM((N_DEV, r, 128), F32), pltpu.SemaphoreType.DMA((N_DEV - 1,)),
                        pltpu.SemaphoreType.DMA((N_DEV - 1,))],
        name=name)(buf)


def _pack(arrs):
    flat = jnp.concatenate([a.reshape(-1).astype(F32) for a in arrs])
    pad = (-flat.shape[0]) % 1024
    return jnp.pad(flat, (0, pad)).reshape(-1, 128)


def _unpack(buf, shapes):
    flat = buf.reshape(-1)
    out, pos = [], 0
    for s in shapes:
        size = math.prod(s)
        out.append(flat[pos:pos + size].reshape(s))
        pos += size
    return out


TILE_ELEMS = 256 * 1024


def _tile_rows(r, c):
    if r * c <= TILE_ELEMS:
        return r
    for tr in range(TILE_ELEMS // c // 16 * 16, 0, -16):
        if r % tr == 0:
            return tr
    raise ValueError((r, c))


def _sum4(own, recv, *, name):
    r, c = own.shape
    tr = _tile_rows(r, c)

    def body(o_ref, r_ref, out_ref):
        acc = o_ref[...].astype(F32)
        for k in range(3):
            acc = acc + r_ref[k].astype(F32)
        out_ref[...] = acc

    return pl.pallas_call(
        body, grid=(r // tr,),
        in_specs=[pl.BlockSpec((tr, c), lambda i: (i, 0)), pl.BlockSpec((3, tr, c), lambda i: (0, i, 0))],
        out_specs=pl.BlockSpec((tr, c), lambda i: (i, 0)), out_shape=jax.ShapeDtypeStruct((r, c), F32),
        name=name, compiler_params=_cp(("parallel",)))(own, recv)


def _adamw(w, ga, gb, m, v, *, name):
    r, c = w.shape
    tr = _tile_rows(r, c)
    bc1 = 1.0 - ADAM_B1 ** ADAM_STEP
    bc2 = 1.0 - ADAM_B2 ** ADAM_STEP
    two = gb is not None

    def body(*refs):
        w_ref, ga_ref = refs[0], refs[1]
        pos = 2
        g = ga_ref[...]
        if two:
            g = g + refs[pos][...]
            pos += 1
        m_ref, v_ref, g_out, d_out, m_out, v_out = refs[pos:pos + 6]
        mn = ADAM_B1 * m_ref[...] + (1.0 - ADAM_B1) * g
        vn = ADAM_B2 * v_ref[...] + (1.0 - ADAM_B2) * (g * g)
        m_hat = mn / bc1
        v_hat = vn / bc2
        g_out[...] = g
        d_out[...] = -ADAM_LR * (m_hat / (jnp.sqrt(v_hat) + ADAM_EPS) + ADAM_WD * w_ref[...])
        m_out[...] = mn
        v_out[...] = vn

    spec = pl.BlockSpec((tr, c), lambda i: (i, 0))
    args = [w, ga] + ([gb] if two else []) + [m, v]
    shp = jax.ShapeDtypeStruct((r, c), F32)
    return pl.pallas_call(
        body, grid=(r // tr,), in_specs=[spec] * len(args), out_specs=[spec] * 4,
        out_shape=[shp] * 4, name=name, compiler_params=_cp(("parallel",)))(*args)


BIG = ['w_in', 'w_glu', 'w_branch_attn', 'w_branch_ssm', 'w_out', 'w_up', 'w_down']
BIG_KEY = {'w_in': 'w_in_t', 'w_glu': 'w_glu_t', 'w_branch_attn': 'w_ba_t', 'w_branch_ssm': 'w_bs_t',
           'w_out': 'w_out', 'w_up': 'w_up_t', 'w_down': 'w_down'}
COL_SHARDED = {'w_in', 'w_glu', 'w_branch_attn', 'w_branch_ssm', 'w_up'}
SMALL = ['attn_norm_g', 'b_in', 'attn_sinks', 'ssm_a_re', 'ssm_a_im', 'ssm_log_dt', 'ssm_b_re', 'ssm_b_im',
         'ssm_c_re', 'ssm_c_im', 'ssm_d', 'b_glu', 'ffn_norm_g', 'conv_b', 'final_norm_g']
WEIGHTS = ['attn_norm_g', 'w_in', 'b_in', 'attn_sinks', 'ssm_a_re', 'ssm_a_im', 'ssm_log_dt', 'ssm_b_re',
           'ssm_b_im', 'ssm_c_re', 'ssm_c_im', 'ssm_d', 'w_glu', 'b_glu', 'w_branch_attn', 'w_branch_ssm',
           'w_out', 'ffn_norm_g', 'w_up', 'conv_w', 'conv_b', 'w_down', 'final_norm_g']


def _shard_2d(name, t):
    t = t[0]
    return t.T if name in COL_SHARDED else t


def _unshard_2d(name, t):
    return (t.T if name in COL_SHARDED else t)[None]


def kernel(x, attn_norm_g, w_in, b_in, attn_sinks, ssm_a_re, ssm_a_im, ssm_log_dt, ssm_b_re, ssm_b_im, ssm_c_re, ssm_c_im, ssm_d, w_glu, b_glu, w_branch_attn, w_branch_ssm, w_out, ffn_norm_g, w_up, conv_w, conv_b, w_down, final_norm_g, loss_target, m_attn_norm_g, m_w_in, m_b_in, m_attn_sinks, m_ssm_a_re, m_ssm_a_im, m_ssm_log_dt, m_ssm_b_re, m_ssm_b_im, m_ssm_c_re, m_ssm_c_im, m_ssm_d, m_w_glu, m_b_glu, m_w_branch_attn, m_w_branch_ssm, m_w_out, m_ffn_norm_g, m_w_up, m_conv_w, m_conv_b, m_w_down, m_final_norm_g, v_attn_norm_g, v_w_in, v_b_in, v_attn_sinks, v_ssm_a_re, v_ssm_a_im, v_ssm_log_dt, v_ssm_b_re, v_ssm_b_im, v_ssm_c_re, v_ssm_c_im, v_ssm_d, v_w_glu, v_b_glu, v_w_branch_attn, v_w_branch_ssm, v_w_out, v_ffn_norm_g, v_w_up, v_conv_w, v_conv_b, v_w_down, v_final_norm_g):
    args = dict(locals())
    w = {n: args[n] for n in WEIGHTS}
    m = {n: args['m_' + n] for n in WEIGHTS}
    v = {n: args['v_' + n] for n in WEIGHTS}
    xi, yi, ci = _place()
    blk = 2 * xi + yi

    shards = {BIG_KEY[n]: _shard_2d(n, w[n]).astype(BF16) for n in BIG}
    cw_cols = w['conv_w'].shape[2]
    cw_place = lax.dynamic_update_slice(jnp.zeros((3, D_FF), F32), w['conv_w'][0] * (ci == 0).astype(F32),
                                        (0, blk * cw_cols))
    conv_w_full = _unpack(_all_reduce_small(_pack([cw_place]), name="gather_conv_w"), [(3, D_FF)])[0]

    small = {n: w[n] for n in SMALL}
    small['conv_w'] = conv_w_full
    loss_part, grad_x, grads, recvs, sgr = _local_step(x[0], loss_target[0], {}, small, shards)

    halves = []
    for n in BIG:
        full, recv = grads[BIG_KEY[n]], recvs[BIG_KEY[n]]
        r = full.shape[0] // N_CHIPS
        own = lax.dynamic_slice_in_dim(full, _block_pos(xi, yi, n == 'w_up') * r, r, axis=0)
        halves.append(_sum4(own, recv, name="sum4_" + n))
    others = _swap_cores(halves, name="swap_cores")
    out = {}
    for n, mine, other in zip(BIG, halves, others):
        res = _adamw(_shard_2d(n, w[n]), mine, other, _shard_2d(n, m[n]), _shard_2d(n, v[n]), name="adamw_" + n)
        out[n] = [_unshard_2d(n, t) for t in res]

    names = SMALL + ['conv_w']
    shapes = [w[n].shape for n in SMALL] + [(3, D_FF)]
    packed = _pack([sgr[n] for n in names] + [loss_part])
    summed = _unpack(_all_reduce_small(packed, name="all_reduce_small"), shapes + [(1, 1)])
    loss = summed[-1].reshape(())
    sg = dict(zip(names, summed[:-1]))
    sg['conv_w'] = lax.dynamic_slice_in_dim(sg['conv_w'], blk * cw_cols, cw_cols, axis=1)[None]
    res = _adamw(_pack([w[n] for n in names]), _pack([sg[n] for n in names]), None,
                 _pack([m[n] for n in names]), _pack([v[n] for n in names]), name="adamw_small")
    ushapes = [w[n].shape for n in names]
    unpacked = [_unpack(t, ushapes) for t in res]
    for i, n in enumerate(names):
        out[n] = [u[i] for u in unpacked]

    return (loss, grad_x[None], *[out[n][0] for n in WEIGHTS], *[out[n][1] for n in WEIGHTS],
            *[out[n][2] for n in WEIGHTS], *[out[n][3] for n in WEIGHTS])
```

```python
import functools
import math

import jax
import jax.numpy as jnp
from jax import lax
from jax.experimental import pallas as pl
from jax.experimental.pallas import tpu as pltpu

F32 = jnp.float32
BF16 = jnp.bfloat16

D_MODEL = 2048
N_Q_HEADS = 16
HEAD_DIM = 64
ATTN_WIDTH = 1024
KV_WIDTH = 128
BLOCK = 128
SSM_WIDTH = 512
SSM_GROUPS = 32
SSM_GROUP = 16
SSM_STATE = 64
D_FF = 5632
IN_COLS = 5888
RMS_EPS = 1e-6
NEG_BIG = -1e30
N_CHIPS = 4
N_DEV = 8

COL_K = 8
COL_V = 9
COL_U = 10
COL_GA = 14
COL_GS = 30

SSM_SPLIT = 4
SSM_U_BLK = 128
SSM_X_BLK = 512
SSM_CHUNK = 256

ADAM_LR = 0.001
ADAM_B1 = 0.9
ADAM_B2 = 0.999
ADAM_EPS = 1e-08
ADAM_WD = 0.01
ADAM_STEP = 10

VMEM_LIMIT_BYTES = 56 * 1024 * 1024
INV_SQRT2 = 1.0 / math.sqrt(2.0)
INV_SQRT2PI = 1.0 / math.sqrt(2.0 * math.pi)
MESH = pl.DeviceIdType.MESH
ANY = pl.BlockSpec(memory_space=pl.ANY)


def _cp(sem):
    return pltpu.CompilerParams(dimension_semantics=sem, vmem_limit_bytes=VMEM_LIMIT_BYTES)


def _gelu(x):
    return 0.5 * x * (1.0 + lax.erf(x * INV_SQRT2))


def _gelu_grad(x):
    return 0.5 * (1.0 + lax.erf(x * INV_SQRT2)) + x * jnp.exp(-0.5 * x * x) * INV_SQRT2PI


def _sigmoid(x):
    return 1.0 / (1.0 + jnp.exp(-x))


def _place():
    return lax.axis_index("x"), lax.axis_index("y"), lax.axis_index("c")


def _other_chips(x, y):
    return [(1 - x, y), (x, 1 - y), (1 - x, 1 - y)]


def _block_pos(x, y, interleaved):
    return x + 2 * y if interleaved else 2 * x + y


LAYOUT = {'w_in_t': (0, False), 'w_glu': (1, False), 'w_ba': (1, False), 'w_bs': (1, False),
          'w_out': (0, False), 'w_up': (1, True), 'w_down': (0, False)}


def _window(ref, axis, pos, size, rows=None):
    if axis == 0:
        start, count = (0, size) if rows is None else rows
        return ref.at[pl.ds(pos * size + start, count), :]
    cols = pl.ds(pos * size, size)
    return ref.at[:, cols] if rows is None else ref.at[pl.ds(rows[0], rows[1]), cols]


def _gathered_shape(shape, axis):
    return tuple(N_CHIPS * d if a == axis else d for a, d in enumerate(shape))


def _block_shape(shape, axis):
    return tuple(d // N_CHIPS if a == axis else d for a, d in enumerate(shape))


class _GatherPlan:
    def __init__(self, shards, layouts):
        self.arrays = list(shards)
        self.layouts = list(layouts)
        n = len(shards)
        self.out_shape = [jax.ShapeDtypeStruct(_gathered_shape(s.shape, lay[0]), s.dtype)
                          for s, lay in zip(shards, layouts)]
        self.scratch = [pltpu.SemaphoreType.DMA((6 * n,)), pltpu.SemaphoreType.DMA((6 * n,)),
                        pltpu.SemaphoreType.DMA((n,))]

    def _copies(self, kind, ins, outs, sems):
        send, recv, local = sems
        n = len(self.arrays)
        x, y, c = _place()
        copies = []
        for i in range(n):
            axis, interleaved = self.layouts[i]
            size = self.arrays[i].shape[axis]
            h = self.arrays[i].shape[0] // 2
            blk = _block_pos(x, y, interleaved)
            if kind == 'mine':
                copies.append(pltpu.make_async_copy(ins[i], _window(outs[i], axis, blk, size), local.at[i]))
                continue
            for k, (px, py) in enumerate(_other_chips(x, y)):
                theirs = _block_pos(px, py, interleaved)
                if kind in ('ici_out', 'ici_in'):
                    route = dict(send_sem=send.at[3 * i + k], recv_sem=recv.at[3 * i + k],
                                 device_id=(px, py, c), device_id_type=MESH)
                else:
                    route = dict(send_sem=send.at[3 * (n + i) + k], recv_sem=recv.at[3 * (n + i) + k],
                                 device_id=(x, y, 1 - c), device_id_type=MESH)
                if kind == 'ici_out':
                    src, dst = ins[i].at[pl.ds(c * h, h), :], _window(outs[i], axis, blk, size, (c * h, h))
                elif kind == 'd2d_in':
                    src = dst = _window(outs[i], axis, theirs, size, ((1 - c) * h, h))
                else:
                    src = dst = _window(outs[i], axis, theirs, size, (c * h, h))
                copies.append(pltpu.make_async_remote_copy(src_ref=src, dst_ref=dst, **route))
        return copies

    def start(self, ins, outs, sems):
        for cp in self._copies('mine', ins, outs, sems) + self._copies('ici_out', ins, outs, sems):
            cp.start()

    def middle(self, ins, outs, sems):
        for arrived, onward in zip(self._copies('ici_in', ins, outs, sems), self._copies('d2d_out', ins, outs, sems)):
            arrived.wait_recv()
            onward.start()

    def finish(self, ins, outs, sems):
        for cp in self._copies('d2d_in', ins, outs, sems):
            cp.wait_recv()
        for cp in self._copies('ici_out', ins, outs, sems) + self._copies('d2d_out', ins, outs, sems):
            cp.wait_send()
        for cp in self._copies('mine', ins, outs, sems):
            cp.wait()


class _ScatterPlan:
    def __init__(self, fulls, layouts):
        self.arrays = list(fulls)
        self.layouts = list(layouts)
        n = len(fulls)
        self.out_shape = [jax.ShapeDtypeStruct((3,) + _block_shape(f.shape, lay[0]), f.dtype)
                          for f, lay in zip(fulls, layouts)]
        self.scratch = [pltpu.SemaphoreType.DMA((3 * n,)), pltpu.SemaphoreType.DMA((3 * n,))]

    def _copies(self, ins, outs, sems):
        send, recv = sems
        x, y, c = _place()
        copies = []
        for i in range(len(self.arrays)):
            axis, interleaved = self.layouts[i]
            size = self.arrays[i].shape[axis] // N_CHIPS
            for k, (px, py) in enumerate(_other_chips(x, y)):
                copies.append(pltpu.make_async_remote_copy(
                    src_ref=_window(ins[i], axis, _block_pos(px, py, interleaved), size), dst_ref=outs[i].at[k],
                    send_sem=send.at[3 * i + k], recv_sem=recv.at[3 * i + k],
                    device_id=(px, py, c), device_id_type=MESH))
        return copies

    def start(self, ins, outs, sems):
        for cp in self._copies(ins, outs, sems):
            cp.start()

    def middle(self, ins, outs, sems):
        pass

    def finish(self, ins, outs, sems):
        for cp in self._copies(ins, outs, sems):
            cp.wait()


def _hosted_call(body, *, grid, in_specs, out_specs, out_shape, scratch_shapes, sem, name, args, comm=None,
                 aliases=None):
    aliases = aliases or {}
    if comm is None:
        outs = pl.pallas_call(body, grid=grid, in_specs=in_specs, out_specs=out_specs, out_shape=out_shape,
                              scratch_shapes=scratch_shapes, name=name, input_output_aliases=aliases,
                              compiler_params=_cp(sem))(*args)
        return outs, None
    n_in, n_out, n_scr = len(in_specs), len(out_specs), len(scratch_shapes)
    nc, ns = len(comm.arrays), len(comm.scratch)
    total = math.prod(grid)
    mid = min(total - 1, (3 * total) // 4)

    def wrapped(*refs):
        pos = 0
        ins = refs[pos:pos + n_in]; pos += n_in
        cins = refs[pos:pos + nc]; pos += nc
        outs = refs[pos:pos + n_out]; pos += n_out
        couts = refs[pos:pos + nc]; pos += nc
        scr = refs[pos:pos + n_scr]; pos += n_scr
        sems = refs[pos:pos + ns]
        step = 0
        for ax, g in enumerate(grid):
            step = step * g + pl.program_id(ax)

        @pl.when(step == 0)
        def _():
            comm.start(cins, couts, sems)

        body(*ins, *outs, *scr)

        @pl.when(step == mid)
        def _():
            comm.middle(cins, couts, sems)

        @pl.when(step == total - 1)
        def _():
            comm.finish(cins, couts, sems)

    res = pl.pallas_call(
        wrapped, grid=grid, in_specs=list(in_specs) + [ANY] * nc, out_specs=list(out_specs) + [ANY] * nc,
        out_shape=list(out_shape) + list(comm.out_shape), scratch_shapes=list(scratch_shapes) + list(comm.scratch),
        name=name, input_output_aliases=aliases,
        compiler_params=_cp(("arbitrary",) * len(grid)))(*args, *comm.arrays)
    return res[:n_out], res[n_out:]


class _Hook:
    def __init__(self, fn, ins=(), in_specs=(), outs=()):
        self.fn, self.ins, self.in_specs, self.outs = fn, list(ins), list(in_specs), list(outs)


def _matmul(a, b, *, ta=False, tb=False, tm, tn, tk, out_dtype=None, bias=None, res=None, inner='n',
            comm=None, prologue=None, epilogue=None, a_shape=None, sequential=False, name):
    if a is None:
        m, kdim = a_shape
    elif ta:
        kdim, m = a.shape
    else:
        m, kdim = a.shape
    if tb:
        n, k2 = b.shape
    else:
        k2, n = b.shape
    assert kdim == k2, (name, kdim, b.shape)
    tm, tn, tk = min(tm, m), min(tn, n), min(tk, kdim)
    assert m % tm == 0 and n % tn == 0 and kdim % tk == 0, (name, m, n, kdim, tm, tn, tk)
    nk = kdim // tk
    dn = (((0 if ta else 1,), (1 if tb else 0,)), ((), ()))
    hooks = [h for h in (prologue, epilogue) if h is not None]
    n_pro_in = len(prologue.ins) if prologue else 0
    n_epi_in = len(epilogue.ins) if epilogue else 0
    n_pro_out = len(prologue.outs) if prologue else 0
    n_epi_out = len(epilogue.outs) if epilogue else 0
    if inner == 'n':
        grid = (m // tm, n // tn, nk)
        mi = lambda g0, g1: g0
        ni = lambda g0, g1: g1
    else:
        grid = (n // tn, m // tm, nk)
        mi = lambda g0, g1: g1
        ni = lambda g0, g1: g0

    def body(*refs):
        refs = list(refs)
        take = lambda cnt: [refs.pop(0) for _ in range(cnt)]
        a_ref = take(1)[0] if a is not None else None
        b_ref = take(1)[0]
        bias_ref = take(1)[0] if bias is not None else None
        res_ref = take(1)[0] if res is not None else None
        pro_in, epi_in = take(n_pro_in), take(n_epi_in)
        o_ref = take(1)[0] if epilogue is None else None
        pro_out, epi_out = take(n_pro_out), take(n_epi_out)
        i, j, k = mi(pl.program_id(0), pl.program_id(1)), ni(pl.program_id(0), pl.program_id(1)), pl.program_id(2)

        def finish(src):
            def result(rows=slice(None)):
                r = src[rows, :]
                if bias_ref is not None:
                    r = r + bias_ref[...]
                if res_ref is not None:
                    r = r + res_ref[rows, :]
                return r

            if epilogue is None:
                o_ref[...] = result().astype(out_dtype)
            else:
                epilogue.fn(result, epi_in, epi_out, i, j)

        a_val = a_ref[...] if prologue is None else prologue.fn(a_ref, pro_in, pro_out, i, k)
        prod = lax.dot_general(a_val.astype(BF16), b_ref[...].astype(BF16), dn, preferred_element_type=F32)
        if nk == 1:
            finish(prod)
            return
        acc_ref = refs[0]

        @pl.when(k == 0)
        def _():
            acc_ref[...] = prod

        @pl.when(k > 0)
        def _():
            acc_ref[...] += prod

        @pl.when(k == nk - 1)
        def _():
            finish(acc_ref)

    spec = lambda shape, fn: pl.BlockSpec(shape, lambda g0, g1, k: fn(mi(g0, g1), ni(g0, g1), k))
    in_specs, args = [], []
    if a is not None:
        in_specs.append(spec((tk, tm), lambda i, j, k: (k, i)) if ta else spec((tm, tk), lambda i, j, k: (i, k)))
        args.append(a)
    in_specs.append(spec((tn, tk), lambda i, j, k: (j, k)) if tb else spec((tk, tn), lambda i, j, k: (k, j)))
    args.append(b)
    if bias is not None:
        in_specs.append(spec((1, tn), lambda i, j, k: (0, j)))
        args.append(bias)
    if res is not None:
        in_specs.append(spec((tm, tn), lambda i, j, k: (i, j)))
        args.append(res)
    for h in hooks:
        in_specs += [spec(shape, fn) for shape, fn in h.in_specs]
        args += h.ins
    out_specs, out_shape = [], []
    if epilogue is None:
        out_specs.append(spec((tm, tn), lambda i, j, k: (i, j)))
        out_shape.append(jax.ShapeDtypeStruct((m, n), out_dtype))
    for h in hooks:
        out_specs += [spec(blk, fn) for _, _, blk, fn in h.outs]
        out_shape += [jax.ShapeDtypeStruct(shape, dtype) for shape, dtype, _, _ in h.outs]
    outs, couts = _hosted_call(
        body, grid=grid, in_specs=in_specs, out_specs=out_specs, out_shape=out_shape,
        scratch_shapes=[pltpu.VMEM((tm, tn), F32)] if nk > 1 else [],
        sem=("arbitrary",) * 3 if sequential else ("parallel", "parallel", "arbitrary"),
        name=name, args=args, comm=comm)
    outs = outs[0] if not hooks else outs
    return outs if comm is None else (outs, couts)


def _rms_fwd(x, g, *, comm=None, name):
    l, d = x.shape
    tr = min(256, l)

    def body(x_ref, g_ref, h_ref):
        xf = x_ref[...]
        r = lax.rsqrt(jnp.mean(xf * xf, axis=-1, keepdims=True) + RMS_EPS)
        h_ref[...] = ((xf * r) * g_ref[...]).astype(BF16)

    row = pl.BlockSpec((tr, d), lambda i: (i, 0))
    return _hosted_call(
        body, grid=(l // tr,), in_specs=[row, pl.BlockSpec((1, d), lambda i: (0, 0))],
        out_specs=[row], out_shape=[jax.ShapeDtypeStruct((l, d), BF16)], scratch_shapes=[],
        sem=("parallel",), name=name, args=(x, g), comm=comm)


EPI_ROWS = 128
ROW_TILE = 256


def _row_chunks(tm):
    ch = min(EPI_ROWS, tm)
    return [slice(c * ch, (c + 1) * ch) for c in range(tm // ch)]


def _rowwise(hook, src, tm, *, name):
    l, d = src.shape
    n_in = len(hook.ins)

    def body(*refs):
        src_ref, ins, outs = refs[0], refs[1:1 + n_in], refs[1 + n_in:]
        hook.fn(lambda rows=slice(None): src_ref[rows, :], ins, outs, pl.program_id(0), 0)

    spec = lambda shape, fn: pl.BlockSpec(shape, lambda i: fn(i, 0, 0))
    return pl.pallas_call(
        body, grid=(l // tm,),
        in_specs=[pl.BlockSpec((tm, d), lambda i: (i, 0))] + [spec(shape, fn) for shape, fn in hook.in_specs],
        out_specs=[spec(blk, fn) for _, _, blk, fn in hook.outs],
        out_shape=[jax.ShapeDtypeStruct(shape, dtype) for shape, dtype, _, _ in hook.outs],
        name=name, compiler_params=_cp(("arbitrary",)))(src, *hook.ins)


def _rms_bwd_hook(x, g, dres, tm, *, with_bf16):
    def fn(result, ins, outs, i, j):
        x_ref, g_ref, dres_ref = ins
        dg_ref = outs[-1]

        @pl.when(i == 0)
        def _():
            dg_ref[...] = jnp.zeros_like(dg_ref)

        for rows in _row_chunks(tm):
            dyv = result(rows)
            xf = x_ref[rows, :]
            r = lax.rsqrt(jnp.mean(xf * xf, axis=-1, keepdims=True) + RMS_EPS)
            xhat = xf * r
            dxh = dyv * g_ref[...]
            dx = r * (dxh - xhat * jnp.mean(dxh * xhat, axis=-1, keepdims=True)) + dres_ref[rows, :]
            outs[0][rows, :] = dx
            if with_bf16:
                outs[1][rows, :] = dx.astype(BF16)
            dg_ref[...] += jnp.sum(dyv * xhat, axis=0, keepdims=True)

    l, d = x.shape
    row = lambda i, j, k: (i, 0)
    vec = lambda i, j, k: (0, 0)
    outs = [((l, d), F32, (tm, d), row)] + ([((l, d), BF16, (tm, d), row)] if with_bf16 else [])
    return _Hook(fn, ins=[x, g, dres], in_specs=[((tm, d), row), ((1, d), vec), ((tm, d), row)],
                 outs=outs + [((1, d), F32, (1, d), vec)])


def _final_loss_hook(g, target, tm):
    l, d = target.shape

    def fn(result, ins, outs, i, j):
        g_ref, t_ref = ins
        dx_ref, dxb_ref, dg_ref, loss_ref = outs
        gv = g_ref[...]

        @pl.when(i == 0)
        def _():
            dg_ref[...] = jnp.zeros_like(dg_ref)
            loss_ref[...] = jnp.zeros_like(loss_ref)

        for rows in _row_chunks(tm):
            xf = result(rows)
            r = lax.rsqrt(jnp.mean(xf * xf, axis=-1, keepdims=True) + RMS_EPS)
            xhat = xf * r
            diff = xhat * gv - t_ref[rows, :]
            dout = diff * (1.0 / d)
            dxh = dout * gv
            dx = r * (dxh - xhat * jnp.mean(dxh * xhat, axis=-1, keepdims=True))
            dx_ref[rows, :] = dx
            dxb_ref[rows, :] = dx.astype(BF16)
            dg_ref[...] += jnp.sum(dout * xhat, axis=0, keepdims=True)
            part = jnp.sum(jnp.mean(diff * diff, axis=-1, keepdims=True), axis=0, keepdims=True)
            loss_ref[...] += 0.5 * part

    row = lambda i, j, k: (i, 0)
    vec = lambda i, j, k: (0, 0)
    return _Hook(fn, ins=[g, target], in_specs=[((1, d), vec), ((tm, d), row)],
                 outs=[((l, d), F32, (tm, d), row), ((l, d), BF16, (tm, d), row),
                       ((1, d), F32, (1, d), vec), ((1, 1), F32, (1, 1), vec)])


Q_PER_KV = 8
GROUP_ROWS = Q_PER_KV * BLOCK


def _attn_masks(n, rows=GROUP_ROWS):
    q_idx = lax.broadcasted_iota(jnp.int32, (rows, 2 * BLOCK), 0) & (BLOCK - 1)
    s_idx = lax.broadcasted_iota(jnp.int32, (rows, 2 * BLOCK), 1)
    dist = q_idx + BLOCK - s_idx
    valid = (dist >= 0) & (dist < BLOCK) & ((n > 0) | (s_idx >= BLOCK))
    return dist.astype(F32), valid


def _dup_half(t, kv_head, lo):
    rolled = pltpu.roll(t, HEAD_DIM, axis=1)
    return jnp.where(lo, t, rolled) if kv_head == 0 else jnp.where(lo, rolled, t)


def _stack_heads(ref, kv_head, lo):
    pieces = []
    for r in range(Q_PER_KV):
        pair = kv_head * 4 + r // 2
        t = ref[:, pair * 128:(pair + 1) * 128].astype(BF16)
        sel = lo if r % 2 == 0 else jnp.logical_not(lo)
        pieces.append(jnp.where(sel, t, jnp.zeros_like(t)))
    return jnp.concatenate(pieces, axis=0)


def _unstack_heads(t, lo):
    return [jnp.where(lo, t[(2 * i) * BLOCK:(2 * i + 1) * BLOCK], t[(2 * i + 1) * BLOCK:(2 * i + 2) * BLOCK])
            for i in range(Q_PER_KV // 2)]


def _per_head_column(values):
    return jnp.concatenate([jnp.full((BLOCK, 1), v, F32) for v in values], axis=0)


def _group_probs(qm, kdup, dist, valid, sink_ref, kv_head):
    heads = [kv_head * Q_PER_KV + r for r in range(Q_PER_KV)]
    slope = _per_head_column([2.0 ** (-8.0 * (h + 1) / N_Q_HEADS) for h in heads])
    sink = _per_head_column([sink_ref[h] for h in heads])
    return _probs(qm, kdup, dist, valid, sink, slope)


def _probs(qm, kdup, dist, valid, sink, slope):
    s = lax.dot_general(qm, kdup, (((1,), (1,)), ((), ())), preferred_element_type=F32)
    s = s * (HEAD_DIM ** -0.5) - slope * dist
    s = jnp.where(valid, s, NEG_BIG)
    m = jnp.maximum(jnp.max(s, axis=-1, keepdims=True), sink)
    p = jnp.exp(s - m)
    esink = jnp.exp(sink - m)
    inv = 1.0 / (jnp.sum(p, axis=-1, keepdims=True) + esink)
    return p * inv, esink * inv


def _attn_fwd(proj, sinks, *, name):
    l = proj.shape[0]
    nb = l // BLOCK

    def body(sink_ref, q_ref, kc_ref, kp_ref, vc_ref, vp_ref, o_ref):
        n = pl.program_id(0)
        dist, valid = _attn_masks(n, BLOCK)
        lo = lax.broadcasted_iota(jnp.int32, (1, BLOCK), 1) < HEAD_DIM
        kx = jnp.concatenate([kp_ref[...], kc_ref[...]], axis=0).astype(BF16)
        vx = jnp.concatenate([vp_ref[...], vc_ref[...]], axis=0).astype(BF16)
        for kv_head in range(2):
            kdup = _dup_half(kx, kv_head, lo)
            vdup = _dup_half(vx, kv_head, lo)
            for pr in range(4):
                pair = kv_head * 4 + pr
                qp = q_ref[:, pair * 128:(pair + 1) * 128].astype(BF16)
                o_pair = jnp.zeros((BLOCK, 128), F32)
                for half in range(2):
                    head = 2 * pair + half
                    sel = lo if half == 0 else jnp.logical_not(lo)
                    qm = jnp.where(sel, qp, jnp.zeros_like(qp))
                    p, _ = _probs(qm, kdup, dist, valid, sink_ref[head], 2.0 ** (-8.0 * (head + 1) / N_Q_HEADS))
                    o = jnp.dot(p.astype(BF16), vdup, preferred_element_type=F32)
                    o_pair = o_pair + jnp.where(sel, o, 0.0)
                o_ref[:, pair * 128:(pair + 1) * 128] = o_pair.astype(BF16)

    kv = lambda col, prev: pl.BlockSpec(
        (BLOCK, KV_WIDTH), (lambda n: (jnp.maximum(n - 1, 0), col)) if prev else (lambda n: (n, col)))
    return pl.pallas_call(
        body, grid=(nb,),
        in_specs=[pl.BlockSpec(memory_space=pltpu.SMEM),
                  pl.BlockSpec((BLOCK, ATTN_WIDTH), lambda n: (n, 0)),
                  kv(COL_K, False), kv(COL_K, True), kv(COL_V, False), kv(COL_V, True)],
        out_specs=pl.BlockSpec((BLOCK, ATTN_WIDTH), lambda n: (n, 0)),
        out_shape=jax.ShapeDtypeStruct((l, ATTN_WIDTH), BF16), name=name,
        compiler_params=_cp(("parallel",)))(sinks, proj, proj, proj, proj, proj)


def _attn_bwd(proj, sinks, dattn, *, comm=None, name):
    l = proj.shape[0]
    nb = l // BLOCK

    def body(sink_ref, q_ref, kc_ref, kp_ref, vc_ref, vp_ref, do_ref,
             dq_ref, dkc_ref, dkp_ref, dvc_ref, dvp_ref, dsink_ref):
        n = pl.program_id(0)
        dist, valid = _attn_masks(n)
        lane = lax.broadcasted_iota(jnp.int32, (1, BLOCK), 1)
        lo = lane < HEAD_DIM
        kx = jnp.concatenate([kp_ref[...], kc_ref[...]], axis=0).astype(BF16)
        vx = jnp.concatenate([vp_ref[...], vc_ref[...]], axis=0).astype(BF16)
        dsink = jnp.zeros((1, BLOCK), F32)
        dk_heads, dv_heads = [], []
        for kv_head in range(2):
            kdup = _dup_half(kx, kv_head, lo)
            vdup = _dup_half(vx, kv_head, lo)
            qm = _stack_heads(q_ref, kv_head, lo)
            dom = _stack_heads(do_ref, kv_head, lo)
            p, psink = _group_probs(qm, kdup, dist, valid, sink_ref, kv_head)
            dp = lax.dot_general(dom, vdup, (((1,), (1,)), ((), ())), preferred_element_type=F32)
            delta = jnp.sum(p * dp, axis=-1, keepdims=True)
            ds = (p * (dp - delta) * (HEAD_DIM ** -0.5)).astype(BF16)
            dsink_rows = -psink * delta
            for r in range(Q_PER_KV):
                part = jnp.sum(dsink_rows[r * BLOCK:(r + 1) * BLOCK])
                dsink = dsink + jnp.where(lane == kv_head * Q_PER_KV + r, part, 0.0)
            dq = jnp.dot(ds, kdup, preferred_element_type=F32)
            for i, dq_pair in enumerate(_unstack_heads(dq, lo)):
                pair = kv_head * 4 + i
                dq_ref[:, pair * 128:(pair + 1) * 128] = dq_pair.astype(BF16)
            dk_acc = lax.dot_general(ds, qm, (((0,), (0,)), ((), ())), preferred_element_type=F32)
            dv_acc = lax.dot_general(p.astype(BF16), dom, (((0,), (0,)), ((), ())), preferred_element_type=F32)
            dk_heads.append(dk_acc + pltpu.roll(dk_acc, HEAD_DIM, axis=1))
            dv_heads.append(dv_acc + pltpu.roll(dv_acc, HEAD_DIM, axis=1))
        dk = jnp.where(lo, dk_heads[0], dk_heads[1])
        dv = jnp.where(lo, dv_heads[0], dv_heads[1])
        dkp_ref[...] = dk[:BLOCK]
        dkc_ref[...] = dk[BLOCK:]
        dvp_ref[...] = dv[:BLOCK]
        dvc_ref[...] = dv[BLOCK:]

        @pl.when(n == 0)
        def _():
            dsink_ref[...] = jnp.zeros_like(dsink_ref)

        dsink_ref[...] += dsink

    kv = lambda col, prev: pl.BlockSpec(
        (BLOCK, KV_WIDTH), (lambda n: (jnp.maximum(n - 1, 0), col)) if prev else (lambda n: (n, col)))
    qspec = pl.BlockSpec((BLOCK, ATTN_WIDTH), lambda n: (n, 0))
    kvout = pl.BlockSpec((BLOCK, KV_WIDTH), lambda n: (n, 0))
    kvshape = jax.ShapeDtypeStruct((l, KV_WIDTH), F32)
    return _hosted_call(
        body, grid=(nb,),
        in_specs=[pl.BlockSpec(memory_space=pltpu.SMEM), qspec,
                  kv(COL_K, False), kv(COL_K, True), kv(COL_V, False), kv(COL_V, True), qspec],
        out_specs=[qspec, kvout, kvout, kvout, kvout, pl.BlockSpec((1, BLOCK), lambda n: (0, 0))],
        out_shape=[jax.ShapeDtypeStruct((l, ATTN_WIDTH), BF16), kvshape, kvshape, kvshape, kvshape,
                   jax.ShapeDtypeStruct((1, BLOCK), F32)],
        scratch_shapes=[], sem=("arbitrary",), name=name,
        args=(sinks, proj, proj, proj, proj, proj, dattn), comm=comm)


def _kv_grad_merge(dkc, dkp, dvc, dvp, *, name):
    l = dkc.shape[0]
    nb = l // BLOCK

    def body(dkc_ref, dkp_ref, dvc_ref, dvp_ref, o_ref):
        last = pl.program_id(0) == nb - 1
        o_ref[:, :KV_WIDTH] = (dkc_ref[...] + jnp.where(last, 0.0, dkp_ref[...])).astype(BF16)
        o_ref[:, KV_WIDTH:] = (dvc_ref[...] + jnp.where(last, 0.0, dvp_ref[...])).astype(BF16)

    cur = pl.BlockSpec((BLOCK, KV_WIDTH), lambda n: (n, 0))
    nxt = pl.BlockSpec((BLOCK, KV_WIDTH), lambda n: (jnp.minimum(n + 1, nb - 1), 0))
    return pl.pallas_call(
        body, grid=(nb,), in_specs=[cur, nxt, cur, nxt],
        out_specs=pl.BlockSpec((BLOCK, 2 * KV_WIDTH), lambda n: (n, 0)),
        out_shape=jax.ShapeDtypeStruct((l, 2 * KV_WIDTH), BF16), name=name,
        compiler_params=_cp(("parallel",)))(dkc, dkp, dvc, dvp)


def _discretize(a_re, a_im, log_dt, b_re, b_im):
    dt = jnp.exp(log_dt)
    mag = jnp.exp(a_re * dt)
    ab_re = mag * jnp.cos(a_im * dt)
    ab_im = mag * jnp.sin(a_im * dt)
    nr = ab_re - 1.0
    ni = ab_im
    den = a_re * a_re + a_im * a_im
    z_re = (nr * a_re + ni * a_im) / den
    z_im = (ni * a_re - nr * a_im) / den
    bb_re = z_re * b_re - z_im * b_im
    bb_im = z_re * b_im + z_im * b_re
    return ab_re, ab_im, bb_re, bb_im


def _ssm_disc_fwd(a_re, a_im, log_dt, b_re, b_im, *, name):
    def body(ar, ai, ld, br, bi, o_ar, o_ai, o_br, o_bi):
        r = _discretize(ar[...], ai[...], ld[...], br[...], bi[...])
        o_ar[...], o_ai[...], o_br[...], o_bi[...] = r

    col = jax.ShapeDtypeStruct(a_re.shape, F32)
    mat = jax.ShapeDtypeStruct(b_re.shape, F32)
    return pl.pallas_call(body, out_shape=[col, col, mat, mat], name=name)(a_re, a_im, log_dt, b_re, b_im)


def _ssm_disc_bwd(a_re, a_im, log_dt, b_re, b_im, d_ab_re, d_ab_im, d_bb_re, d_bb_im, *, name):
    def body(ar, ai, ld, br, bi, g0, g1, g2, g3, o_ar, o_ai, o_ld, o_br, o_bi):
        _, vjp = jax.vjp(_discretize, ar[...], ai[...], ld[...], br[...], bi[...])
        r = vjp((g0[...], g1[...], g2[...], g3[...]))
        o_ar[...], o_ai[...], o_ld[...], o_br[...], o_bi[...] = r

    col = jax.ShapeDtypeStruct(a_re.shape, F32)
    mat = jax.ShapeDtypeStruct(b_re.shape, F32)
    return pl.pallas_call(body, out_shape=[col, col, col, mat, mat], name=name)(
        a_re, a_im, log_dt, b_re, b_im, d_ab_re, d_ab_im, d_bb_re, d_bb_im)


def _shift_rows(x, d, rows, *, down):
    t = x.shape[0]
    if down:
        return jnp.where(rows >= d, pltpu.roll(x, d, axis=0), 0.0)
    return jnp.where(rows < t - d, pltpu.roll(x, t - d, axis=0), 0.0)


def _scan_chunk(xr, xi, ar, ai, *, down):
    t = xr.shape[0]
    rows = lax.broadcasted_iota(jnp.int32, (t, 1), 0)
    pr, pi = ar, ai
    d = 1
    while d < t:
        sr = _shift_rows(xr, d, rows, down=down)
        si = _shift_rows(xi, d, rows, down=down)
        xr, xi = xr + pr * sr - pi * si, xi + pr * si + pi * sr
        pr, pi = pr * pr - pi * pi, 2.0 * pr * pi
        d *= 2
    return xr, xi


def _ssm_fwd(proj, ab, bd, cd, dskip, *, comm=None, name):
    l = proj.shape[0]
    t = min(SSM_CHUNK, l)
    nc = l // t

    def body(u_ref, ab_ref, bd_ref, cd_ref, ds_ref, y_ref, gy_ref, xs_ref, carry_ref):
        c = pl.program_id(1)

        @pl.when(c == 0)
        def _():
            carry_ref[...] = jnp.zeros_like(carry_ref)

        u = u_ref[...]
        ar, ai = ab_ref[0, 0:1, :], ab_ref[0, 1:2, :]
        bu = jnp.dot(u.astype(BF16), bd_ref[0], preferred_element_type=F32)
        rows = lax.broadcasted_iota(jnp.int32, (t, 1), 0)
        cr, ci = carry_ref[0:1, :], carry_ref[1:2, :]
        xr = bu[:, :SSM_X_BLK] + jnp.where(rows == 0, ar * cr - ai * ci, 0.0)
        xi = bu[:, SSM_X_BLK:] + jnp.where(rows == 0, ar * ci + ai * cr, 0.0)
        xr, xi = _scan_chunk(xr, xi, ar, ai, down=True)
        xs_ref[0, :, :SSM_X_BLK] = xr
        xs_ref[0, :, SSM_X_BLK:] = xi
        carry_ref[0:1, :] = xs_ref[0, t - 1:t, :SSM_X_BLK]
        carry_ref[1:2, :] = xs_ref[0, t - 1:t, SSM_X_BLK:]
        y = jnp.dot(xs_ref[0].astype(BF16), cd_ref[0], preferred_element_type=F32) + ds_ref[...] * u
        y_ref[...] = y
        gy_ref[...] = _gelu(y).astype(BF16)

    blk = lambda shape: pl.BlockSpec((1,) + shape, lambda j, c: (j, 0, 0))
    ycol = pl.BlockSpec((t, SSM_U_BLK), lambda j, c: (c, j))
    return _hosted_call(
        body, grid=(SSM_SPLIT, nc),
        in_specs=[pl.BlockSpec((t, SSM_U_BLK), lambda j, c: (c, COL_U + j)),
                  blk((2, SSM_X_BLK)), blk((SSM_U_BLK, 2 * SSM_X_BLK)), blk((2 * SSM_X_BLK, SSM_U_BLK)),
                  pl.BlockSpec((1, SSM_U_BLK), lambda j, c: (0, j))],
        out_specs=[ycol, ycol, pl.BlockSpec((1, t, 2 * SSM_X_BLK), lambda j, c: (j, c, 0))],
        out_shape=[jax.ShapeDtypeStruct((l, SSM_WIDTH), F32), jax.ShapeDtypeStruct((l, SSM_WIDTH), BF16),
                   jax.ShapeDtypeStruct((SSM_SPLIT, l, 2 * SSM_X_BLK), F32)],
        scratch_shapes=[pltpu.VMEM((2, SSM_X_BLK), F32)], sem=("parallel", "arbitrary"), name=name,
        args=(proj, ab, bd, cd, dskip), comm=comm)


def _ssm_bwd(proj, y, dgy, xs, ab, bdt, cdt, dskip, *, comm=None, name):
    l = proj.shape[0]
    t = min(SSM_CHUNK, l)
    nc = l // t

    def body(u_ref, y_ref, dgy_ref, xs_ref, halo_ref, ab_ref, bdt_ref, cdt_ref, ds_ref,
             du_ref, dbd_ref, dcd_ref, dab_ref, dd_ref, carry_ref):
        c = pl.program_id(1)
        ci_ = nc - 1 - c

        @pl.when(c == 0)
        def _():
            carry_ref[...] = jnp.zeros_like(carry_ref)
            dbd_ref[...] = jnp.zeros_like(dbd_ref)
            dcd_ref[...] = jnp.zeros_like(dcd_ref)
            dab_ref[...] = jnp.zeros_like(dab_ref)
            dd_ref[...] = jnp.zeros_like(dd_ref)

        u = u_ref[...]
        dy = dgy_ref[...] * _gelu_grad(y_ref[...])
        dyb = dy.astype(BF16)
        ar, ai = ab_ref[0, 0:1, :], ab_ref[0, 1:2, :]
        g = jnp.dot(dyb, cdt_ref[0], preferred_element_type=F32)
        rows = lax.broadcasted_iota(jnp.int32, (t, 1), 0)
        cr, ci = carry_ref[0:1, :], carry_ref[1:2, :]
        lr = g[:, :SSM_X_BLK] + jnp.where(rows == t - 1, ar * cr + ai * ci, 0.0)
        li = g[:, SSM_X_BLK:] + jnp.where(rows == t - 1, ar * ci - ai * cr, 0.0)
        lr, li = _scan_chunk(lr, li, ar, -ai, down=False)
        lam = jnp.concatenate([lr, li], axis=1)
        carry_ref[0:1, :] = lr[0:1, :]
        carry_ref[1:2, :] = li[0:1, :]
        lamb = lam.astype(BF16)
        du_ref[...] = (jnp.dot(lamb, bdt_ref[0], preferred_element_type=F32) + ds_ref[...] * dy).astype(BF16)
        dbd_ref[0] += lax.dot_general(u.astype(BF16), lamb, (((0,), (0,)), ((), ())),
                                      preferred_element_type=F32)
        xs = xs_ref[0]
        dcd_ref[0] += lax.dot_general(xs.astype(BF16), dyb, (((0,), (0,)), ((), ())),
                                      preferred_element_type=F32)
        halo = jnp.where(ci_ > 0, halo_ref[0, 7:8, :], 0.0)
        xprev = jnp.where(rows == 0, halo, pltpu.roll(xs, 1, axis=0))
        xpr, xpi = xprev[:, :SSM_X_BLK], xprev[:, SSM_X_BLK:]
        dab_ref[0, 0:1, :] += jnp.sum(lr * xpr + li * xpi, axis=0, keepdims=True)
        dab_ref[0, 1:2, :] += jnp.sum(li * xpr - lr * xpi, axis=0, keepdims=True)
        dd_ref[...] += jnp.sum(dy * u, axis=0, keepdims=True)

    blk = lambda shape: pl.BlockSpec((1,) + shape, lambda j, c: (j, 0, 0))
    rev = lambda j, c: (nc - 1 - c, j)
    ycol = pl.BlockSpec((t, SSM_U_BLK), rev)
    hb = t // 8
    return _hosted_call(
        body, grid=(SSM_SPLIT, nc), comm=comm, sem=("parallel", "arbitrary"), name=name,
        args=(proj, y, dgy, xs, xs, ab, bdt, cdt, dskip), scratch_shapes=[pltpu.VMEM((2, SSM_X_BLK), F32)],
        in_specs=[pl.BlockSpec((t, SSM_U_BLK), lambda j, c: (nc - 1 - c, COL_U + j)), ycol, ycol,
                  pl.BlockSpec((1, t, 2 * SSM_X_BLK), lambda j, c: (j, nc - 1 - c, 0)),
                  pl.BlockSpec((1, 8, 2 * SSM_X_BLK),
                               lambda j, c: (j, jnp.maximum((nc - 1 - c) * hb - 1, 0), 0)),
                  blk((2, SSM_X_BLK)), blk((2 * SSM_X_BLK, SSM_U_BLK)), blk((SSM_U_BLK, 2 * SSM_X_BLK)),
                  pl.BlockSpec((1, SSM_U_BLK), lambda j, c: (0, j))],
        out_specs=[ycol, blk((SSM_U_BLK, 2 * SSM_X_BLK)), blk((2 * SSM_X_BLK, SSM_U_BLK)),
                   blk((2, SSM_X_BLK)), pl.BlockSpec((1, SSM_U_BLK), lambda j, c: (0, j))],
        out_shape=[jax.ShapeDtypeStruct((l, SSM_WIDTH), BF16),
                   jax.ShapeDtypeStruct((SSM_SPLIT, SSM_U_BLK, 2 * SSM_X_BLK), F32),
                   jax.ShapeDtypeStruct((SSM_SPLIT, 2 * SSM_X_BLK, SSM_U_BLK), F32),
                   jax.ShapeDtypeStruct((SSM_SPLIT, 2, SSM_X_BLK), F32),
                   jax.ShapeDtypeStruct((1, SSM_WIDTH), F32)])


def _block_diag(t):
    s, g, a, b = t.shape
    return jnp.einsum('sgab,gk->sgakb', t, jnp.eye(g, dtype=t.dtype)).reshape(s, g * a, g * b)


def _block_diag_take(t, a, b):
    s = t.shape[0]
    return jnp.einsum('sgakb,gk->sgab', t.reshape(s, 8, a, 8, b), jnp.eye(8, dtype=t.dtype))


def _glu_fwd_hook(l, tm):
    def fn(result, ins, outs, i, j):
        z = result()
        outs[0][...] = z
        outs[1][...] = (z[:, :SSM_WIDTH] * _sigmoid(z[:, SSM_WIDTH:])).astype(BF16)

    row = lambda i, j, k: (i, 0)
    return _Hook(fn, outs=[((l, 2 * SSM_WIDTH), F32, (tm, 2 * SSM_WIDTH), row),
                           ((l, SSM_WIDTH), BF16, (tm, SSM_WIDTH), row)])


def _glu_bwd_hook(z, tm):
    l = z.shape[0]

    def fn(result, ins, outs, i, j):
        zv_ref, zg_ref = ins
        dz_ref, db_ref = outs
        d = result()
        sg = _sigmoid(zg_ref[...])
        dv = d * sg
        dg = d * zv_ref[...] * sg * (1.0 - sg)
        dz_ref[:, :SSM_WIDTH] = dv.astype(BF16)
        dz_ref[:, SSM_WIDTH:] = dg.astype(BF16)

        @pl.when(i == 0)
        def _():
            db_ref[...] = jnp.zeros_like(db_ref)

        db_ref[:, :SSM_WIDTH] += jnp.sum(dv, axis=0, keepdims=True)
        db_ref[:, SSM_WIDTH:] += jnp.sum(dg, axis=0, keepdims=True)

    half = (tm, SSM_WIDTH)
    return _Hook(fn, ins=[z, z], in_specs=[(half, lambda i, j, k: (i, 0)), (half, lambda i, j, k: (i, 1))],
                 outs=[((l, 2 * SSM_WIDTH), BF16, (tm, 2 * SSM_WIDTH), lambda i, j, k: (i, 0)),
                       ((1, 2 * SSM_WIDTH), F32, (1, 2 * SSM_WIDTH), lambda i, j, k: (0, 0))])


GATE_TC = 256


def _merge_fwd(proj, a, s, *, name):
    l = a.shape[0]
    tr = min(2048, l)

    def body(ga_ref, gs_ref, a_ref, s_ref, o_ref):
        o_ref[...] = (_sigmoid(ga_ref[...]) * a_ref[...] + _sigmoid(gs_ref[...]) * s_ref[...]).astype(BF16)

    own = pl.BlockSpec((tr, GATE_TC), lambda i, j: (i, j))
    return pl.pallas_call(
        body, grid=(l // tr, D_MODEL // GATE_TC),
        in_specs=[pl.BlockSpec((tr, GATE_TC), lambda i, j: (i, COL_GA // 2 + j)),
                  pl.BlockSpec((tr, GATE_TC), lambda i, j: (i, COL_GS // 2 + j)), own, own],
        out_specs=own, out_shape=jax.ShapeDtypeStruct((l, D_MODEL), BF16), name=name,
        compiler_params=_cp(("parallel", "parallel")))(proj, proj, a, s)


def _merge_bwd_hook(proj, a, s, tm):
    def fn(result, ins, outs, i, j):
        ga_ref, gs_ref, a_br, s_br = ins
        d = result()
        sa = _sigmoid(ga_ref[...])
        ss = _sigmoid(gs_ref[...])
        outs[0][...] = (d * sa).astype(BF16)
        outs[1][...] = (d * ss).astype(BF16)
        outs[2][...] = (d * a_br[...] * sa * (1.0 - sa)).astype(BF16)
        outs[3][...] = (d * s_br[...] * ss * (1.0 - ss)).astype(BF16)

    blk = (tm, GATE_TC)
    own = lambda i, j, k: (i, j)
    return _Hook(fn, ins=[proj, proj, a, s],
                 in_specs=[(blk, lambda i, j, k: (i, COL_GA // 2 + j)), (blk, lambda i, j, k: (i, COL_GS // 2 + j)),
                           (blk, own), (blk, own)],
                 outs=[(a.shape, BF16, blk, own)] * 4)


FF_TC = D_FF // 2
FF_NJ = 2
FF_ROWS = 128


FF_HALO = 16


def _conv_taps(ext, rows):
    h = FF_HALO
    return (ext[h:h + rows], pltpu.roll(ext, 1, axis=0)[h:h + rows], pltpu.roll(ext, 2, axis=0)[h:h + rows])


def _ff_specs(tr, l):
    hb = tr // FF_HALO
    last = l // FF_HALO - 1
    prev = lambda i: jnp.maximum(i * hb - 1, 0)
    nxt = lambda i: jnp.minimum((i + 1) * hb, last)
    return dict(
        own=pl.BlockSpec((tr, FF_TC), lambda j, i: (i, j)),
        own_next=pl.BlockSpec((FF_HALO, FF_TC), lambda j, i: (nxt(i), j)),
        val=pl.BlockSpec((tr, FF_TC), lambda j, i: (i, 2 * j)),
        val_next=pl.BlockSpec((FF_HALO, FF_TC), lambda j, i: (nxt(i), 2 * j)),
        gate=pl.BlockSpec((tr, FF_TC), lambda j, i: (i, 2 * j + 1)),
        gate_prev=pl.BlockSpec((FF_HALO, FF_TC), lambda j, i: (prev(i), 2 * j + 1)),
        gate_next=pl.BlockSpec((FF_HALO, FF_TC), lambda j, i: (nxt(i), 2 * j + 1)),
        pair=pl.BlockSpec((tr, 2 * FF_TC), lambda j, i: (i, j)),
        w=pl.BlockSpec((3, FF_TC), lambda j, i: (0, j)),
        b=pl.BlockSpec((1, FF_TC), lambda j, i: (0, j)))


def _ffn_act_fwd(up, conv_w, conv_b, *, name):
    l = up.shape[0]
    tr = min(FF_ROWS, l)

    def body(v_ref, g_ref, prev_ref, w_ref, b_ref, o_ref):
        prev = jnp.where(pl.program_id(1) == 0, 0.0, prev_ref[...].astype(F32))
        g0, g1, g2 = _conv_taps(jnp.concatenate([prev, g_ref[...].astype(F32)], axis=0), tr)
        gc = b_ref[...] + w_ref[0:1, :] * g2 + w_ref[1:2, :] * g1 + w_ref[2:3, :] * g0
        o_ref[...] = (v_ref[...].astype(F32) * _gelu(gc)).astype(BF16)

    sp = _ff_specs(tr, l)
    return pl.pallas_call(
        body, grid=(FF_NJ, l // tr), in_specs=[sp['val'], sp['gate'], sp['gate_prev'], sp['w'], sp['b']],
        out_specs=sp['own'], out_shape=jax.ShapeDtypeStruct((l, D_FF), BF16), name=name,
        compiler_params=_cp(("parallel", "parallel")))(up, up, up, conv_w, conv_b)


def _ffn_act_bwd(dact, up, conv_w, conv_b, *, comm=None, name):
    l = up.shape[0]
    tr = min(FF_ROWS, l)
    ni = l // tr
    te = tr + 8

    def body(d_ref, dn_ref, v_ref, vn_ref, g_ref, gp_ref, gn_ref, w_ref, b_ref, dup_ref, dw_ref, db_ref):
        i = pl.program_id(1)
        f32 = lambda ref, rows=None: ref[...].astype(F32)[:rows]
        prev = jnp.where(i == 0, 0.0, f32(gp_ref))
        g0, g1, g2 = _conv_taps(jnp.concatenate([prev, f32(g_ref), f32(gn_ref, 8)], axis=0), te)
        w0, w1, w2 = w_ref[0:1, :], w_ref[1:2, :], w_ref[2:3, :]
        gc = b_ref[...] + w0 * g2 + w1 * g1 + w2 * g0
        d_own = f32(d_ref)
        d = jnp.concatenate([d_own, jnp.where(i == ni - 1, 0.0, f32(dn_ref, 8))], axis=0)
        v = jnp.concatenate([f32(v_ref), f32(vn_ref, 8)], axis=0)
        dgc = d * v * _gelu_grad(gc)
        ahead1 = pltpu.roll(dgc, te - 1, axis=0)[:tr]
        ahead2 = pltpu.roll(dgc, te - 2, axis=0)[:tr]
        own = dgc[:tr]
        dup_ref[:, :FF_TC] = (d_own * _gelu(gc[:tr])).astype(BF16)
        dup_ref[:, FF_TC:] = (w2 * own + w1 * ahead1 + w0 * ahead2).astype(BF16)

        @pl.when(i == 0)
        def _():
            dw_ref[...] = jnp.zeros_like(dw_ref)
            db_ref[...] = jnp.zeros_like(db_ref)

        dw_ref[0:1, :] += jnp.sum(own * g2[:tr], axis=0, keepdims=True)
        dw_ref[1:2, :] += jnp.sum(own * g1[:tr], axis=0, keepdims=True)
        dw_ref[2:3, :] += jnp.sum(own * g0[:tr], axis=0, keepdims=True)
        db_ref[...] += jnp.sum(own, axis=0, keepdims=True)

    sp = _ff_specs(tr, l)
    return _hosted_call(
        body, grid=(FF_NJ, ni),
        in_specs=[sp['own'], sp['own_next'], sp['val'], sp['val_next'], sp['gate'], sp['gate_prev'],
                  sp['gate_next'], sp['w'], sp['b']],
        out_specs=[sp['pair'], sp['w'], sp['b']],
        out_shape=[jax.ShapeDtypeStruct((l, 2 * D_FF), BF16), jax.ShapeDtypeStruct((3, D_FF), F32),
                   jax.ShapeDtypeStruct((1, D_FF), F32)],
        scratch_shapes=[], sem=("parallel", "arbitrary"), name=name,
        args=(dact, dact, up, up, up, up, up, conv_w, conv_b), comm=comm)


def _col_sum(a, *, name):
    l, n = a.shape
    tr = min(512, l)

    def body(a_ref, o_ref):
        @pl.when(pl.program_id(0) == 0)
        def _():
            o_ref[...] = jnp.zeros_like(o_ref)

        o_ref[...] += jnp.sum(a_ref[...].astype(F32), axis=0, keepdims=True)

    return pl.pallas_call(
        body, grid=(l // tr,), in_specs=[pl.BlockSpec((tr, n), lambda i: (i, 0))],
        out_specs=pl.BlockSpec((1, n), lambda i: (0, 0)), out_shape=jax.ShapeDtypeStruct((1, n), F32),
        name=name, compiler_params=_cp(("arbitrary",)))(a)


def _local_step(x, target, wts, small, shards=None):
    l = x.shape[0]
    wts = dict(wts)
    grads, recvs, sgr = {}, {}, {}
    lay = lambda keys: [LAYOUT[k] for k in keys]
    none = lambda keys: None
    gather = (lambda keys: _GatherPlan([shards[k] for k in keys], lay(keys))) if shards is not None else none
    scatter = (lambda keys: _ScatterPlan([grads[k] for k in keys], lay(keys))) if shards is not None else none

    mm = _matmul

    def take(res, plan, keys, store):
        outs, couts = res
        if plan is not None:
            store.update(zip(keys, couts))
        return outs

    def mm_host(keys, make_plan, store, *args, **kw):
        plan = make_plan(keys)
        if plan is None:
            return _matmul(*args, **kw)
        return take(_matmul(*args, comm=plan, **kw), plan, keys, store)

    col = lambda t: t.reshape(SSM_GROUPS * SSM_STATE, 1)
    a_re, a_im = col(small['ssm_a_re']), col(small['ssm_a_im'])
    log_dt = jnp.repeat(small['ssm_log_dt'].reshape(SSM_GROUPS), SSM_STATE).reshape(-1, 1)
    b_re = small['ssm_b_re'].reshape(SSM_GROUPS * SSM_STATE, SSM_GROUP)
    b_im = small['ssm_b_im'].reshape(SSM_GROUPS * SSM_STATE, SSM_GROUP)
    ab_re, ab_im, bb_re, bb_im = _ssm_disc_fwd(a_re, a_im, log_dt, b_re, b_im, name="ssm_disc_fwd")
    ab = jnp.stack([ab_re.reshape(SSM_SPLIT, SSM_X_BLK), ab_im.reshape(SSM_SPLIT, SSM_X_BLK)], axis=1)
    to_bd = lambda t: _block_diag(t.reshape(SSM_SPLIT, 8, SSM_STATE, SSM_GROUP).transpose(0, 1, 3, 2))
    bd = jnp.concatenate([to_bd(bb_re), to_bd(bb_im)], axis=2)
    c_re = small['ssm_c_re'].reshape(SSM_SPLIT, 8, SSM_GROUP, SSM_STATE)
    c_im = small['ssm_c_im'].reshape(SSM_SPLIT, 8, SSM_GROUP, SSM_STATE)
    cdt = jnp.concatenate([_block_diag(c_re), -_block_diag(c_im)], axis=2)
    bd_b, cdt_b = bd.astype(BF16), cdt.astype(BF16)
    bdt_b, cd_b = bd_b.transpose(0, 2, 1), cdt_b.transpose(0, 2, 1)
    dskip = small['ssm_d'].reshape(1, SSM_WIDTH)

    sinks = small['attn_sinks'].reshape(N_Q_HEADS)
    plan = gather(['w_in_t'])
    h1, = take(_rms_fwd(x, small['attn_norm_g'], comm=plan, name="rms1_fwd"), plan, ['w_in_t'], wts)
    proj = mm_host(['w_glu', 'w_ba', 'w_bs', 'w_out'], gather, wts,
                   h1, wts['w_in_t'], tb=True, tm=512, tn=2944, tk=2048, inner='m', out_dtype=F32,
                   bias=small['b_in'], name="mm_in")
    attn = _attn_fwd(proj, sinks, name="attn_fwd")
    plan = gather(['w_up'])
    y, gy, xs = take(_ssm_fwd(proj, ab, bd_b, cd_b, dskip, comm=plan, name="ssm_fwd"), plan, ['w_up'], wts)
    z, ssm = mm(gy, wts['w_glu'], tm=1024, tn=1024, tk=512, bias=small['b_glu'],
                epilogue=_glu_fwd_hook(l, min(1024, l)), name="mm_glu")
    a_br = mm(attn, wts['w_ba'], tm=1024, tn=1024, tk=1024, out_dtype=F32, name="mm_ba")
    s_br = mm(ssm, wts['w_bs'], tm=1024, tn=1024, tk=512, out_dtype=F32, name="mm_bs")
    tr = min(ROW_TILE, l)
    merged = _merge_fwd(proj, a_br, s_br, name="merge_fwd")
    x2 = mm(merged, wts['w_out'], tm=1024, tn=1024, tk=2048, inner='m', out_dtype=F32, res=x, name="mm_out")
    h2, = take(_rms_fwd(x2, small['ffn_norm_g'], name="rms2_fwd"), None, [], wts)
    up = mm_host(['w_down'], gather, wts,
                 h2, wts['w_up'], tm=1024, tn=1024, tk=2048, out_dtype=BF16, name="mm_up")
    conv_w, conv_b = small['conv_w'], small['conv_b']
    act = _ffn_act_fwd(up, conv_w, conv_b, name="ffn_act_fwd")
    x3 = mm(act, wts['w_down'], tm=1024, tn=1024, tk=2816, out_dtype=F32, res=x2, name="mm_down")
    dx3, dx3b, d_g3, loss = _rowwise(
        _final_loss_hook(small['final_norm_g'].reshape(1, D_MODEL), target, tr), x3, tr, name="final_loss")

    sgr['final_norm_g'] = d_g3.reshape(D_MODEL)
    dact = mm(dx3b, wts['w_down'], tb=True, tm=512, tn=2816, tk=2048, inner='m', out_dtype=BF16, name="mm_dact")
    grads['w_down'] = mm(act, dx3b, ta=True, tm=512, tn=1024, tk=2048, out_dtype=BF16, name="mm_dw_down")
    plan = scatter(['w_down'])
    dup, sgr['conv_w'], sgr['conv_b'] = take(
        _ffn_act_bwd(dact, up, conv_w, conv_b, comm=plan, name="ffn_act_bwd"), plan, ['w_down'], recvs)
    grads['w_up'] = mm(h2, dup, ta=True, tm=1024, tn=1024, tk=2048, out_dtype=BF16, name="mm_dw_up")
    dh2 = mm(dup, wts['w_up'], tb=True, tm=1024, tn=1024, tk=2816, out_dtype=F32, name="mm_dh2")
    dx2, dx2b, sgr['ffn_norm_g'] = _rowwise(
        _rms_bwd_hook(x2, small['ffn_norm_g'], dx3, tr, with_bf16=True), dh2, tr, name="rms2_bwd")

    d_a, d_s, dga, dgs = mm(dx2b, wts['w_out'], tb=True, tm=1024, tn=GATE_TC, tk=2048,
                            epilogue=_merge_bwd_hook(proj, a_br, s_br, min(1024, l)), name="mm_dmerged")
    grads['w_out'] = mm(merged, dx2b, ta=True, tm=1024, tn=1024, tk=2048, out_dtype=BF16, name="mm_dw_out")
    dattn = mm(d_a, wts['w_ba'], tb=True, tm=1024, tn=1024, tk=2048, inner='m', out_dtype=BF16, name="mm_dattn")
    grads['w_ba'] = mm(attn, d_a, ta=True, tm=1024, tn=1024, tk=2048, out_dtype=BF16, name="mm_dw_ba")
    dz, sgr['b_glu'] = mm(d_s, wts['w_bs'], tb=True, tm=1024, tn=512, tk=2048, sequential=True,
                          epilogue=_glu_bwd_hook(z, min(1024, l)), name="mm_dssm")
    grads['w_bs'] = mm(ssm, d_s, ta=True, tm=512, tn=1024, tk=2048, out_dtype=BF16, name="mm_dw_bs")
    grads['w_glu'] = mm(gy, dz, ta=True, tm=512, tn=1024, tk=2048, out_dtype=BF16, name="mm_dw_glu")
    dgy = mm(dz, wts['w_glu'], tb=True, tm=1024, tn=512, tk=1024, inner='m', out_dtype=F32, name="mm_dgy")
    keys = ['w_out', 'w_ba', 'w_bs', 'w_glu']
    plan = scatter(keys)
    du, d_bd, d_cd, d_ab, sgr['ssm_d'] = take(
        _ssm_bwd(proj, y, dgy, xs, ab, bdt_b, cdt_b, dskip, comm=plan, name="ssm_bwd"), plan, keys, recvs)
    plan = scatter(['w_up'])
    dq, dkc, dkp, dvc, dvp, dsink = take(
        _attn_bwd(proj, sinks, dattn, comm=plan, name="attn_bwd"), plan, ['w_up'], recvs)
    dkv = _kv_grad_merge(dkc, dkp, dvc, dvp, name="kv_grad_merge")
    sgr['attn_sinks'] = dsink[:, :N_Q_HEADS]
    dproj = jnp.concatenate([dq, dkv, du, dga, dgs], axis=1)
    sgr['b_in'] = _col_sum(dproj, name="col_sum_dproj")
    grads['w_in_t'] = mm(dproj, h1, ta=True, tm=2944, tn=1024, tk=1024, out_dtype=BF16, name="mm_dw_in")
    dh1 = mm_host(['w_in_t'], scatter, recvs,
                  dproj, wts['w_in_t'], tm=1024, tn=1024, tk=2944, out_dtype=F32, name="mm_dh1")
    grad_x, sgr['attn_norm_g'] = _rowwise(
        _rms_bwd_hook(x, small['attn_norm_g'], dx2, tr, with_bf16=False), dh1, tr, name="rms1_bwd")

    from_bd = lambda t: _block_diag_take(t, SSM_GROUP, SSM_STATE).transpose(0, 1, 3, 2).reshape(
        SSM_GROUPS * SSM_STATE, SSM_GROUP)
    d_bb_re = from_bd(d_bd[:, :, :SSM_X_BLK])
    d_bb_im = from_bd(d_bd[:, :, SSM_X_BLK:])
    d_cdt = d_cd.transpose(0, 2, 1)
    shape_c = (1, SSM_GROUPS, SSM_GROUP, SSM_STATE)
    sgr['ssm_c_re'] = _block_diag_take(d_cdt[:, :, :SSM_X_BLK], SSM_GROUP, SSM_STATE).reshape(shape_c)
    sgr['ssm_c_im'] = -_block_diag_take(d_cdt[:, :, SSM_X_BLK:], SSM_GROUP, SSM_STATE).reshape(shape_c)
    d_a_re, d_a_im, d_ldt, d_b_re, d_b_im = _ssm_disc_bwd(
        a_re, a_im, log_dt, b_re, b_im, d_ab[:, 0, :].reshape(-1, 1), d_ab[:, 1, :].reshape(-1, 1),
        d_bb_re, d_bb_im, name="ssm_disc_bwd")
    sgr['ssm_a_re'] = d_a_re.reshape(1, SSM_GROUPS, SSM_STATE)
    sgr['ssm_a_im'] = d_a_im.reshape(1, SSM_GROUPS, SSM_STATE)
    sgr['ssm_log_dt'] = d_ldt.reshape(SSM_GROUPS, SSM_STATE).sum(axis=1).reshape(1, SSM_GROUPS)
    sgr['ssm_b_re'] = d_b_re.reshape(1, SSM_GROUPS, SSM_STATE, SSM_GROUP)
    sgr['ssm_b_im'] = d_b_im.reshape(1, SSM_GROUPS, SSM_STATE, SSM_GROUP)
    return loss, grad_x, grads, recvs, sgr


def _swap_cores(arrs, *, name):
    n = len(arrs)

    def body(*refs):
        ins, outs = refs[:n], refs[n:2 * n]
        send_sems, recv_sems = refs[2 * n:]
        x, y, c = _place()
        copies = []
        for i in range(n):
            cp = pltpu.make_async_remote_copy(
                src_ref=ins[i], dst_ref=outs[i], send_sem=send_sems.at[i], recv_sem=recv_sems.at[i],
                device_id=(x, y, 1 - c), device_id_type=MESH)
            cp.start()
            copies.append(cp)
        for cp in copies:
            cp.wait()

    return pl.pallas_call(
        body, in_specs=[ANY] * n, out_specs=[ANY] * n,
        out_shape=[jax.ShapeDtypeStruct(a.shape, a.dtype) for a in arrs],
        scratch_shapes=[pltpu.SemaphoreType.DMA((n,)), pltpu.SemaphoreType.DMA((n,))],
        name=name)(*arrs)


def _all_reduce_small(buf, *, name):
    r = buf.shape[0]

    def body(in_ref, out_ref, slots, send_sems, recv_sems):
        x, y, c = _place()
        me = 4 * x + 2 * y + c
        slots[pl.ds(me, 1)] = in_ref[...][None]
        copies = []
        for k in range(N_DEV - 1):
            bx, by, bc = ((k + 1) >> 2) & 1, ((k + 1) >> 1) & 1, (k + 1) & 1
            peer = (1 - x if bx else x, 1 - y if by else y, 1 - c if bc else c)
            cp = pltpu.make_async_remote_copy(
                src_ref=in_ref, dst_ref=slots.at[me], send_sem=send_sems.at[k], recv_sem=recv_sems.at[k],
                device_id=peer, device_id_type=MESH)
            cp.start()
            copies.append(cp)
        for cp in copies:
            cp.wait()
        acc = slots[0]
        for d in range(1, N_DEV):
            acc = acc + slots[d]
        out_ref[...] = acc

    vm = pl.BlockSpec(memory_space=pltpu.VMEM)
    return pl.pallas_call(
        body, in_specs=[vm], out_specs=vm, out_shape=jax.ShapeDtypeStruct((r, 128), F32),
        scratch_shapes=[pltpu.VMEM((N_DEV, r, 128), F32), pltpu.SemaphoreType.DMA((N_DEV - 1,)),
                        pltpu.SemaphoreType.DMA((N_DEV - 1,))],
        name=name)(buf)


def _pack(arrs):
    flat = jnp.concatenate([a.reshape(-1).astype(F32) for a in arrs])
    pad = (-flat.shape[0]) % 1024
    return jnp.pad(flat, (0, pad)).reshape(-1, 128)


def _unpack(buf, shapes):
    flat = buf.reshape(-1)
    out, pos = [], 0
    for s in shapes:
        size = math.prod(s)
        out.append(flat[pos:pos + size].reshape(s))
        pos += size
    return out


TILE_ELEMS = 256 * 1024


def _tile_rows(r, c):
    if r * c <= TILE_ELEMS:
        return r
    for tr in range(TILE_ELEMS // c // 16 * 16, 0, -16):
        if r % tr == 0:
            return tr
    raise ValueError((r, c))


def _sum4(own, recv, *, name):
    r, c = own.shape
    tr = _tile_rows(r, c)

    def body(o_ref, r_ref, out_ref):
        acc = o_ref[...].astype(F32)
        for k in range(3):
            acc = acc + r_ref[k].astype(F32)
        out_ref[...] = acc

    return pl.pallas_call(
        body, grid=(r // tr,),
        in_specs=[pl.BlockSpec((tr, c), lambda i: (i, 0)), pl.BlockSpec((3, tr, c), lambda i: (0, i, 0))],
        out_specs=pl.BlockSpec((tr, c), lambda i: (i, 0)), out_shape=jax.ShapeDtypeStruct((r, c), F32),
        name=name, compiler_params=_cp(("parallel",)))(own, recv)


def _adamw(w, ga, gb, m, v, *, name):
    r, c = w.shape
    tr = _tile_rows(r, c)
    bc1 = 1.0 - ADAM_B1 ** ADAM_STEP
    bc2 = 1.0 - ADAM_B2 ** ADAM_STEP
    two = gb is not None

    def body(*refs):
        w_ref, ga_ref = refs[0], refs[1]
        pos = 2
        g = ga_ref[...]
        if two:
            g = g + refs[pos][...]
            pos += 1
        m_ref, v_ref, g_out, d_out, m_out, v_out = refs[pos:pos + 6]
        mn = ADAM_B1 * m_ref[...] + (1.0 - ADAM_B1) * g
        vn = ADAM_B2 * v_ref[...] + (1.0 - ADAM_B2) * (g * g)
        m_hat = mn / bc1
        v_hat = vn / bc2
        g_out[...] = g
        d_out[...] = -ADAM_LR * (m_hat / (jnp.sqrt(v_hat) + ADAM_EPS) + ADAM_WD * w_ref[...])
        m_out[...] = mn
        v_out[...] = vn

    spec = pl.BlockSpec((tr, c), lambda i: (i, 0))
    args = [w, ga] + ([gb] if two else []) + [m, v]
    shp = jax.ShapeDtypeStruct((r, c), F32)
    return pl.pallas_call(
        body, grid=(r // tr,), in_specs=[spec] * len(args), out_specs=[spec] * 4,
        out_shape=[shp] * 4, name=name, compiler_params=_cp(("parallel",)))(*args)


BIG = ['w_in', 'w_glu', 'w_branch_attn', 'w_branch_ssm', 'w_out', 'w_up', 'w_down']
BIG_KEY = {'w_in': 'w_in_t', 'w_glu': 'w_glu', 'w_branch_attn': 'w_ba', 'w_branch_ssm': 'w_bs',
           'w_out': 'w_out', 'w_up': 'w_up', 'w_down': 'w_down'}
TRANSPOSED = {'w_in'}
SMALL = ['attn_norm_g', 'b_in', 'attn_sinks', 'ssm_a_re', 'ssm_a_im', 'ssm_log_dt', 'ssm_b_re', 'ssm_b_im',
         'ssm_c_re', 'ssm_c_im', 'ssm_d', 'b_glu', 'ffn_norm_g', 'conv_b', 'final_norm_g']
WEIGHTS = ['attn_norm_g', 'w_in', 'b_in', 'attn_sinks', 'ssm_a_re', 'ssm_a_im', 'ssm_log_dt', 'ssm_b_re',
           'ssm_b_im', 'ssm_c_re', 'ssm_c_im', 'ssm_d', 'w_glu', 'b_glu', 'w_branch_attn', 'w_branch_ssm',
           'w_out', 'ffn_norm_g', 'w_up', 'conv_w', 'conv_b', 'w_down', 'final_norm_g']


def _shard_2d(name, t):
    t = t[0]
    return t.T if name in TRANSPOSED else t


def _unshard_2d(name, t):
    return (t.T if name in TRANSPOSED else t)[None]


def kernel(x, attn_norm_g, w_in, b_in, attn_sinks, ssm_a_re, ssm_a_im, ssm_log_dt, ssm_b_re, ssm_b_im, ssm_c_re, ssm_c_im, ssm_d, w_glu, b_glu, w_branch_attn, w_branch_ssm, w_out, ffn_norm_g, w_up, conv_w, conv_b, w_down, final_norm_g, loss_target, m_attn_norm_g, m_w_in, m_b_in, m_attn_sinks, m_ssm_a_re, m_ssm_a_im, m_ssm_log_dt, m_ssm_b_re, m_ssm_b_im, m_ssm_c_re, m_ssm_c_im, m_ssm_d, m_w_glu, m_b_glu, m_w_branch_attn, m_w_branch_ssm, m_w_out, m_ffn_norm_g, m_w_up, m_conv_w, m_conv_b, m_w_down, m_final_norm_g, v_attn_norm_g, v_w_in, v_b_in, v_attn_sinks, v_ssm_a_re, v_ssm_a_im, v_ssm_log_dt, v_ssm_b_re, v_ssm_b_im, v_ssm_c_re, v_ssm_c_im, v_ssm_d, v_w_glu, v_b_glu, v_w_branch_attn, v_w_branch_ssm, v_w_out, v_ffn_norm_g, v_w_up, v_conv_w, v_conv_b, v_w_down, v_final_norm_g):
    args = dict(locals())
    w = {n: args[n] for n in WEIGHTS}
    m = {n: args['m_' + n] for n in WEIGHTS}
    v = {n: args['v_' + n] for n in WEIGHTS}
    xi, yi, ci = _place()
    blk = 2 * xi + yi

    shards = {BIG_KEY[n]: _shard_2d(n, w[n]).astype(BF16) for n in BIG}
    cw_cols = w['conv_w'].shape[2]
    cw_place = lax.dynamic_update_slice(jnp.zeros((3, D_FF), F32), w['conv_w'][0] * (ci == 0).astype(F32),
                                        (0, blk * cw_cols))
    conv_w_full = _unpack(_all_reduce_small(_pack([cw_place]), name="gather_conv_w"), [(3, D_FF)])[0]

    small = {n: w[n] for n in SMALL}
    small['conv_w'] = conv_w_full
    loss_part, grad_x, grads, recvs, sgr = _local_step(x[0], loss_target[0], {}, small, shards)

    halves = []
    for n in BIG:
        full, recv = grads[BIG_KEY[n]], recvs[BIG_KEY[n]]
        axis, interleaved = LAYOUT[BIG_KEY[n]]
        size = full.shape[axis] // N_CHIPS
        own = lax.dynamic_slice_in_dim(full, _block_pos(xi, yi, interleaved) * size, size, axis=axis)
        halves.append(_sum4(own, recv, name="sum4_" + n))
    others = _swap_cores(halves, name="swap_cores")
    out = {}
    for n, mine, other in zip(BIG, halves, others):
        res = _adamw(_shard_2d(n, w[n]), mine, other, _shard_2d(n, m[n]), _shard_2d(n, v[n]), name="adamw_" + n)
        out[n] = [_unshard_2d(n, t) for t in res]

    names = SMALL + ['conv_w']
    shapes = [w[n].shape for n in SMALL] + [(3, D_FF)]
    packed = _pack([sgr[n] for n in names] + [loss_part])
    summed = _unpack(_all_reduce_small(packed, name="all_reduce_small"), shapes + [(1, 1)])
    loss = summed[-1].reshape(())
    sg = dict(zip(names, summed[:-1]))
    sg['conv_w'] = lax.dynamic_slice_in_dim(sg['conv_w'], blk * cw_cols, cw_cols, axis=1)[None]
    res = _adamw(_pack([w[n] for n in names]), _pack([sg[n] for n in names]), None,
                 _pack([m[n] for n in names]), _pack([v[n] for n in names]), name="adamw_small")
    ushapes = [w[n].shape for n in names]
    unpacked = [_unpack(t, ushapes) for t in res]
    for i, n in enumerate(names):
        out[n] = [u[i] for u in unpacked]

    return (loss, grad_x[None], *[out[n][0] for n in WEIGHTS], *[out[n][1] for n in WEIGHTS],
            *[out[n][2] for n in WEIGHTS], *[out[n][3] for n in WEIGHTS])
```

```python
import functools
import math

import jax
import jax.numpy as jnp
from jax import lax
from jax.experimental import pallas as pl
from jax.experimental.pallas import tpu as pltpu

F32 = jnp.float32
BF16 = jnp.bfloat16

D_MODEL = 2048
N_Q_HEADS = 16
HEAD_DIM = 64
ATTN_WIDTH = 1024
KV_WIDTH = 128
BLOCK = 128
SSM_WIDTH = 512
SSM_GROUPS = 32
SSM_GROUP = 16
SSM_STATE = 64
D_FF = 5632
IN_COLS = 5888
RMS_EPS = 1e-6
NEG_BIG = -1e30
N_CHIPS = 4
N_DEV = 8

COL_K = 8
COL_V = 9
COL_U = 10
COL_GA = 14
COL_GS = 30

SSM_SPLIT = 4
SSM_U_BLK = 128
SSM_X_BLK = 512
SSM_CHUNK = 256

ADAM_LR = 0.001
ADAM_B1 = 0.9
ADAM_B2 = 0.999
ADAM_EPS = 1e-08
ADAM_WD = 0.01
ADAM_STEP = 10

VMEM_LIMIT_BYTES = 56 * 1024 * 1024
INV_SQRT2 = 1.0 / math.sqrt(2.0)
INV_SQRT2PI = 1.0 / math.sqrt(2.0 * math.pi)
MESH = pl.DeviceIdType.MESH
ANY = pl.BlockSpec(memory_space=pl.ANY)


def _cp(sem):
    return pltpu.CompilerParams(dimension_semantics=sem, vmem_limit_bytes=VMEM_LIMIT_BYTES)


def _gelu(x):
    return 0.5 * x * (1.0 + lax.erf(x * INV_SQRT2))


def _gelu_grad(x):
    return 0.5 * (1.0 + lax.erf(x * INV_SQRT2)) + x * jnp.exp(-0.5 * x * x) * INV_SQRT2PI


def _sigmoid(x):
    return 1.0 / (1.0 + jnp.exp(-x))


def _place():
    return lax.axis_index("x"), lax.axis_index("y"), lax.axis_index("c")


def _other_chips(x, y):
    return [(1 - x, y), (x, 1 - y), (1 - x, 1 - y)]


def _block_pos(x, y, interleaved):
    return x + 2 * y if interleaved else 2 * x + y


LAYOUT = {'w_in_t': (0, False), 'w_glu': (1, False), 'w_ba': (1, False), 'w_bs': (1, False),
          'w_out': (0, False), 'w_up': (1, True), 'w_down': (0, False)}


def _window(ref, axis, pos, size, rows=None):
    if axis == 0:
        start, count = (0, size) if rows is None else rows
        return ref.at[pl.ds(pos * size + start, count), :]
    cols = pl.ds(pos * size, size)
    return ref.at[:, cols] if rows is None else ref.at[pl.ds(rows[0], rows[1]), cols]


def _gathered_shape(shape, axis):
    return tuple(N_CHIPS * d if a == axis else d for a, d in enumerate(shape))


def _block_shape(shape, axis):
    return tuple(d // N_CHIPS if a == axis else d for a, d in enumerate(shape))


class _GatherPlan:
    def __init__(self, shards, layouts, part=(0, 1), base=None):
        self.arrays = list(shards)
        self.layouts = list(layouts)
        self.part = part
        self.base = list(base) if base is not None else []
        assert (part[0] == 0) == (base is None)
        n = len(shards)
        self.out_shape = [jax.ShapeDtypeStruct(_gathered_shape(s.shape, lay[0]), s.dtype)
                          for s, lay in zip(shards, layouts)]
        self.scratch = [pltpu.SemaphoreType.DMA((6 * n,)), pltpu.SemaphoreType.DMA((6 * n,)),
                        pltpu.SemaphoreType.DMA((n,))]

    def _copies(self, kind, ins, outs, sems):
        send, recv, local = sems
        n = len(self.arrays)
        x, y, c = _place()
        copies = []
        for i in range(n):
            axis, interleaved = self.layouts[i]
            size = self.arrays[i].shape[axis]
            half = self.arrays[i].shape[0] // 2
            h = half // self.part[1]
            first = lambda core: core * half + self.part[0] * h
            blk = _block_pos(x, y, interleaved)
            if kind == 'mine':
                if self.part[0] == 0:
                    copies.append(pltpu.make_async_copy(ins[i], _window(outs[i], axis, blk, size), local.at[i]))
                continue
            for k, (px, py) in enumerate(_other_chips(x, y)):
                theirs = _block_pos(px, py, interleaved)
                if kind in ('ici_out', 'ici_in'):
                    route = dict(send_sem=send.at[3 * i + k], recv_sem=recv.at[3 * i + k],
                                 device_id=(px, py, c), device_id_type=MESH)
                else:
                    route = dict(send_sem=send.at[3 * (n + i) + k], recv_sem=recv.at[3 * (n + i) + k],
                                 device_id=(x, y, 1 - c), device_id_type=MESH)
                if kind == 'ici_out':
                    src, dst = ins[i].at[pl.ds(first(c), h), :], _window(outs[i], axis, blk, size, (first(c), h))
                elif kind == 'd2d_in':
                    src = dst = _window(outs[i], axis, theirs, size, (first(1 - c), h))
                else:
                    src = dst = _window(outs[i], axis, theirs, size, (first(c), h))
                copies.append(pltpu.make_async_remote_copy(src_ref=src, dst_ref=dst, **route))
        return copies

    def start(self, ins, outs, sems):
        for cp in self._copies('mine', ins, outs, sems) + self._copies('ici_out', ins, outs, sems):
            cp.start()

    def middle(self, ins, outs, sems):
        for arrived, onward in zip(self._copies('ici_in', ins, outs, sems), self._copies('d2d_out', ins, outs, sems)):
            arrived.wait_recv()
            onward.start()

    def finish(self, ins, outs, sems):
        for cp in self._copies('d2d_in', ins, outs, sems):
            cp.wait_recv()
        for cp in self._copies('ici_out', ins, outs, sems) + self._copies('d2d_out', ins, outs, sems):
            cp.wait_send()
        for cp in self._copies('mine', ins, outs, sems):
            cp.wait()


class _ScatterPlan:
    def __init__(self, fulls, layouts, part=(0, 1)):
        self.arrays = list(fulls)
        self.layouts = list(layouts)
        self.part = part
        self.base = []
        n = len(fulls)
        self.out_shape = []
        for f, lay in zip(fulls, layouts):
            rows, cols = _block_shape(f.shape, lay[0])
            self.out_shape.append(jax.ShapeDtypeStruct((3, rows // part[1], cols), f.dtype))
        self.scratch = [pltpu.SemaphoreType.DMA((3 * n,)), pltpu.SemaphoreType.DMA((3 * n,))]

    def _copies(self, ins, outs, sems):
        send, recv = sems
        x, y, c = _place()
        copies = []
        for i in range(len(self.arrays)):
            axis, interleaved = self.layouts[i]
            size = self.arrays[i].shape[axis] // N_CHIPS
            h = _block_shape(self.arrays[i].shape, axis)[0] // self.part[1]
            rows = (self.part[0] * h, h)
            for k, (px, py) in enumerate(_other_chips(x, y)):
                copies.append(pltpu.make_async_remote_copy(
                    src_ref=_window(ins[i], axis, _block_pos(px, py, interleaved), size, rows), dst_ref=outs[i].at[k],
                    send_sem=send.at[3 * i + k], recv_sem=recv.at[3 * i + k],
                    device_id=(px, py, c), device_id_type=MESH))
        return copies

    def start(self, ins, outs, sems):
        for cp in self._copies(ins, outs, sems):
            cp.start()

    def middle(self, ins, outs, sems):
        pass

    def finish(self, ins, outs, sems):
        for cp in self._copies(ins, outs, sems):
            cp.wait()


class _Plans:
    def __init__(self, plans):
        assert not any(p.base for p in plans)
        self.plans = list(plans)
        self.base = []
        self.arrays = [a for p in plans for a in p.arrays]
        self.out_shape = [o for p in plans for o in p.out_shape]
        self.scratch = [s for p in plans for s in p.scratch]

    def _each(self, step, ins, outs, sems):
        a = s = 0
        for p in self.plans:
            na, ns = len(p.arrays), len(p.scratch)
            getattr(p, step)(ins[a:a + na], outs[a:a + na], sems[s:s + ns])
            a, s = a + na, s + ns

    def start(self, ins, outs, sems):
        self._each('start', ins, outs, sems)

    def middle(self, ins, outs, sems):
        self._each('middle', ins, outs, sems)

    def finish(self, ins, outs, sems):
        self._each('finish', ins, outs, sems)


def _hosted_call(body, *, grid, in_specs, out_specs, out_shape, scratch_shapes, sem, name, args, comm=None,
                 aliases=None):
    aliases = aliases or {}
    if comm is None:
        outs = pl.pallas_call(body, grid=grid, in_specs=in_specs, out_specs=out_specs, out_shape=out_shape,
                              scratch_shapes=scratch_shapes, name=name, input_output_aliases=aliases,
                              compiler_params=_cp(sem))(*args)
        return outs, None
    n_in, n_out, n_scr = len(in_specs), len(out_specs), len(scratch_shapes)
    nc, ns, nb = len(comm.arrays), len(comm.scratch), len(comm.base)
    total = math.prod(grid)
    mid = min(total - 1, (3 * total) // 4)
    aliases = {**aliases, **{n_in + nc + b: n_out + b for b in range(nb)}}

    def wrapped(*refs):
        pos = 0
        ins = refs[pos:pos + n_in]; pos += n_in
        cins = refs[pos:pos + nc]; pos += nc + nb
        outs = refs[pos:pos + n_out]; pos += n_out
        couts = refs[pos:pos + nc]; pos += nc
        scr = refs[pos:pos + n_scr]; pos += n_scr
        sems = refs[pos:pos + ns]
        step = 0
        for ax, g in enumerate(grid):
            step = step * g + pl.program_id(ax)

        @pl.when(step == 0)
        def _():
            comm.start(cins, couts, sems)

        body(*ins, *outs, *scr)

        @pl.when(step == mid)
        def _():
            comm.middle(cins, couts, sems)

        @pl.when(step == total - 1)
        def _():
            comm.finish(cins, couts, sems)

    res = pl.pallas_call(
        wrapped, grid=grid, in_specs=list(in_specs) + [ANY] * (nc + nb), out_specs=list(out_specs) + [ANY] * nc,
        out_shape=list(out_shape) + list(comm.out_shape), scratch_shapes=list(scratch_shapes) + list(comm.scratch),
        name=name, input_output_aliases=aliases,
        compiler_params=_cp(("arbitrary",) * len(grid)))(*args, *comm.arrays, *comm.base)
    return res[:n_out], res[n_out:]


class _Hook:
    def __init__(self, fn, ins=(), in_specs=(), outs=()):
        self.fn, self.ins, self.in_specs, self.outs = fn, list(ins), list(in_specs), list(outs)


def _matmul(a, b, *, ta=False, tb=False, tm, tn, tk, out_dtype=None, bias=None, res=None, inner='n',
            comm=None, prologue=None, epilogue=None, a_shape=None, sequential=False, name):
    if a is None:
        m, kdim = a_shape
    elif ta:
        kdim, m = a.shape
    else:
        m, kdim = a.shape
    if tb:
        n, k2 = b.shape
    else:
        k2, n = b.shape
    assert kdim == k2, (name, kdim, b.shape)
    tm, tn, tk = min(tm, m), min(tn, n), min(tk, kdim)
    assert m % tm == 0 and n % tn == 0 and kdim % tk == 0, (name, m, n, kdim, tm, tn, tk)
    nk = kdim // tk
    dn = (((0 if ta else 1,), (1 if tb else 0,)), ((), ()))
    hooks = [h for h in (prologue, epilogue) if h is not None]
    n_pro_in = len(prologue.ins) if prologue else 0
    n_epi_in = len(epilogue.ins) if epilogue else 0
    n_pro_out = len(prologue.outs) if prologue else 0
    n_epi_out = len(epilogue.outs) if epilogue else 0
    if inner == 'n':
        grid = (m // tm, n // tn, nk)
        mi = lambda g0, g1: g0
        ni = lambda g0, g1: g1
    else:
        grid = (n // tn, m // tm, nk)
        mi = lambda g0, g1: g1
        ni = lambda g0, g1: g0

    def body(*refs):
        refs = list(refs)
        take = lambda cnt: [refs.pop(0) for _ in range(cnt)]
        a_ref = take(1)[0] if a is not None else None
        b_ref = take(1)[0]
        bias_ref = take(1)[0] if bias is not None else None
        res_ref = take(1)[0] if res is not None else None
        pro_in, epi_in = take(n_pro_in), take(n_epi_in)
        o_ref = take(1)[0] if epilogue is None else None
        pro_out, epi_out = take(n_pro_out), take(n_epi_out)
        i, j, k = mi(pl.program_id(0), pl.program_id(1)), ni(pl.program_id(0), pl.program_id(1)), pl.program_id(2)

        def finish(src):
            def result(rows=slice(None)):
                r = src[rows, :]
                if bias_ref is not None:
                    r = r + bias_ref[...]
                if res_ref is not None:
                    r = r + res_ref[rows, :]
                return r

            if epilogue is None:
                o_ref[...] = result().astype(out_dtype)
            else:
                epilogue.fn(result, epi_in, epi_out, i, j)

        a_val = a_ref[...] if prologue is None else prologue.fn(a_ref, pro_in, pro_out, i, k)
        prod = lax.dot_general(a_val.astype(BF16), b_ref[...].astype(BF16), dn, preferred_element_type=F32)
        if nk == 1:
            finish(prod)
            return
        acc_ref = refs[0]

        @pl.when(k == 0)
        def _():
            acc_ref[...] = prod

        @pl.when(k > 0)
        def _():
            acc_ref[...] += prod

        @pl.when(k == nk - 1)
        def _():
            finish(acc_ref)

    spec = lambda shape, fn: pl.BlockSpec(shape, lambda g0, g1, k: fn(mi(g0, g1), ni(g0, g1), k))
    in_specs, args = [], []
    if a is not None:
        in_specs.append(spec((tk, tm), lambda i, j, k: (k, i)) if ta else spec((tm, tk), lambda i, j, k: (i, k)))
        args.append(a)
    in_specs.append(spec((tn, tk), lambda i, j, k: (j, k)) if tb else spec((tk, tn), lambda i, j, k: (k, j)))
    args.append(b)
    if bias is not None:
        in_specs.append(spec((1, tn), lambda i, j, k: (0, j)))
        args.append(bias)
    if res is not None:
        in_specs.append(spec((tm, tn), lambda i, j, k: (i, j)))
        args.append(res)
    for h in hooks:
        in_specs += [spec(shape, fn) for shape, fn in h.in_specs]
        args += h.ins
    out_specs, out_shape = [], []
    if epilogue is None:
        out_specs.append(spec((tm, tn), lambda i, j, k: (i, j)))
        out_shape.append(jax.ShapeDtypeStruct((m, n), out_dtype))
    for h in hooks:
        out_specs += [spec(blk, fn) for _, _, blk, fn in h.outs]
        out_shape += [jax.ShapeDtypeStruct(shape, dtype) for shape, dtype, _, _ in h.outs]
    outs, couts = _hosted_call(
        body, grid=grid, in_specs=in_specs, out_specs=out_specs, out_shape=out_shape,
        scratch_shapes=[pltpu.VMEM((tm, tn), F32)] if nk > 1 else [],
        sem=("arbitrary",) * 3 if sequential else ("parallel", "parallel", "arbitrary"),
        name=name, args=args, comm=comm)
    outs = outs[0] if not hooks else outs
    return outs if comm is None else (outs, couts)


def _rms_fwd(x, g, *, comm=None, name):
    l, d = x.shape
    tr = min(256, l)

    def body(x_ref, g_ref, h_ref):
        xf = x_ref[...]
        r = lax.rsqrt(jnp.mean(xf * xf, axis=-1, keepdims=True) + RMS_EPS)
        h_ref[...] = ((xf * r) * g_ref[...]).astype(BF16)

    row = pl.BlockSpec((tr, d), lambda i: (i, 0))
    return _hosted_call(
        body, grid=(l // tr,), in_specs=[row, pl.BlockSpec((1, d), lambda i: (0, 0))],
        out_specs=[row], out_shape=[jax.ShapeDtypeStruct((l, d), BF16)], scratch_shapes=[],
        sem=("parallel",), name=name, args=(x, g), comm=comm)


EPI_ROWS = 128
ROW_TILE = 256


def _row_chunks(tm):
    ch = min(EPI_ROWS, tm)
    return [slice(c * ch, (c + 1) * ch) for c in range(tm // ch)]


def _rowwise(hook, src, tm, *, name):
    l, d = src.shape
    n_in = len(hook.ins)

    def body(*refs):
        src_ref, ins, outs = refs[0], refs[1:1 + n_in], refs[1 + n_in:]
        hook.fn(lambda rows=slice(None): src_ref[rows, :], ins, outs, pl.program_id(0), 0)

    spec = lambda shape, fn: pl.BlockSpec(shape, lambda i: fn(i, 0, 0))
    return pl.pallas_call(
        body, grid=(l // tm,),
        in_specs=[pl.BlockSpec((tm, d), lambda i: (i, 0))] + [spec(shape, fn) for shape, fn in hook.in_specs],
        out_specs=[spec(blk, fn) for _, _, blk, fn in hook.outs],
        out_shape=[jax.ShapeDtypeStruct(shape, dtype) for shape, dtype, _, _ in hook.outs],
        name=name, compiler_params=_cp(("arbitrary",)))(src, *hook.ins)


def _rms_bwd_hook(x, g, dres, tm, *, with_bf16):
    def fn(result, ins, outs, i, j):
        x_ref, g_ref, dres_ref = ins
        dg_ref = outs[-1]

        @pl.when(i == 0)
        def _():
            dg_ref[...] = jnp.zeros_like(dg_ref)

        for rows in _row_chunks(tm):
            dyv = result(rows)
            xf = x_ref[rows, :]
            r = lax.rsqrt(jnp.mean(xf * xf, axis=-1, keepdims=True) + RMS_EPS)
            xhat = xf * r
            dxh = dyv * g_ref[...]
            dx = r * (dxh - xhat * jnp.mean(dxh * xhat, axis=-1, keepdims=True)) + dres_ref[rows, :]
            outs[0][rows, :] = dx
            if with_bf16:
                outs[1][rows, :] = dx.astype(BF16)
            dg_ref[...] += jnp.sum(dyv * xhat, axis=0, keepdims=True)

    l, d = x.shape
    row = lambda i, j, k: (i, 0)
    vec = lambda i, j, k: (0, 0)
    outs = [((l, d), F32, (tm, d), row)] + ([((l, d), BF16, (tm, d), row)] if with_bf16 else [])
    return _Hook(fn, ins=[x, g, dres], in_specs=[((tm, d), row), ((1, d), vec), ((tm, d), row)],
                 outs=outs + [((1, d), F32, (1, d), vec)])


def _final_loss_hook(g, target, tm):
    l, d = target.shape

    def fn(result, ins, outs, i, j):
        g_ref, t_ref = ins
        dx_ref, dxb_ref, dg_ref, loss_ref = outs
        gv = g_ref[...]

        @pl.when(i == 0)
        def _():
            dg_ref[...] = jnp.zeros_like(dg_ref)
            loss_ref[...] = jnp.zeros_like(loss_ref)

        for rows in _row_chunks(tm):
            xf = result(rows)
            r = lax.rsqrt(jnp.mean(xf * xf, axis=-1, keepdims=True) + RMS_EPS)
            xhat = xf * r
            diff = xhat * gv - t_ref[rows, :]
            dout = diff * (1.0 / d)
            dxh = dout * gv
            dx = r * (dxh - xhat * jnp.mean(dxh * xhat, axis=-1, keepdims=True))
            dx_ref[rows, :] = dx
            dxb_ref[rows, :] = dx.astype(BF16)
            dg_ref[...] += jnp.sum(dout * xhat, axis=0, keepdims=True)
            part = jnp.sum(jnp.mean(diff * diff, axis=-1, keepdims=True), axis=0, keepdims=True)
            loss_ref[...] += 0.5 * part

    row = lambda i, j, k: (i, 0)
    vec = lambda i, j, k: (0, 0)
    return _Hook(fn, ins=[g, target], in_specs=[((1, d), vec), ((tm, d), row)],
                 outs=[((l, d), F32, (tm, d), row), ((l, d), BF16, (tm, d), row),
                       ((1, d), F32, (1, d), vec), ((1, 1), F32, (1, 1), vec)])


Q_PER_KV = 8
GROUP_ROWS = Q_PER_KV * BLOCK


def _attn_masks(n, rows=GROUP_ROWS):
    q_idx = lax.broadcasted_iota(jnp.int32, (rows, 2 * BLOCK), 0) & (BLOCK - 1)
    s_idx = lax.broadcasted_iota(jnp.int32, (rows, 2 * BLOCK), 1)
    dist = q_idx + BLOCK - s_idx
    valid = (dist >= 0) & (dist < BLOCK) & ((n > 0) | (s_idx >= BLOCK))
    return dist.astype(F32), valid


def _dup_half(t, kv_head, lo):
    rolled = pltpu.roll(t, HEAD_DIM, axis=1)
    return jnp.where(lo, t, rolled) if kv_head == 0 else jnp.where(lo, rolled, t)


def _stack_heads(ref, kv_head, lo):
    pieces = []
    for r in range(Q_PER_KV):
        pair = kv_head * 4 + r // 2
        t = ref[:, pair * 128:(pair + 1) * 128].astype(BF16)
        sel = lo if r % 2 == 0 else jnp.logical_not(lo)
        pieces.append(jnp.where(sel, t, jnp.zeros_like(t)))
    return jnp.concatenate(pieces, axis=0)


def _unstack_heads(t, lo):
    return [jnp.where(lo, t[(2 * i) * BLOCK:(2 * i + 1) * BLOCK], t[(2 * i + 1) * BLOCK:(2 * i + 2) * BLOCK])
            for i in range(Q_PER_KV // 2)]


def _per_head_column(values):
    return jnp.concatenate([jnp.full((BLOCK, 1), v, F32) for v in values], axis=0)


def _group_probs(qm, kdup, dist, valid, sink_ref, kv_head):
    heads = [kv_head * Q_PER_KV + r for r in range(Q_PER_KV)]
    slope = _per_head_column([2.0 ** (-8.0 * (h + 1) / N_Q_HEADS) for h in heads])
    sink = _per_head_column([sink_ref[h] for h in heads])
    return _probs(qm, kdup, dist, valid, sink, slope)


def _probs(qm, kdup, dist, valid, sink, slope):
    s = lax.dot_general(qm, kdup, (((1,), (1,)), ((), ())), preferred_element_type=F32)
    s = s * (HEAD_DIM ** -0.5) - slope * dist
    s = jnp.where(valid, s, NEG_BIG)
    m = jnp.maximum(jnp.max(s, axis=-1, keepdims=True), sink)
    p = jnp.exp(s - m)
    esink = jnp.exp(sink - m)
    inv = 1.0 / (jnp.sum(p, axis=-1, keepdims=True) + esink)
    return p * inv, esink * inv


def _attn_fwd(proj, sinks, *, name):
    l = proj.shape[0]
    nb = l // BLOCK

    def body(sink_ref, q_ref, kc_ref, kp_ref, vc_ref, vp_ref, o_ref):
        n = pl.program_id(0)
        dist, valid = _attn_masks(n, BLOCK)
        lo = lax.broadcasted_iota(jnp.int32, (1, BLOCK), 1) < HEAD_DIM
        kx = jnp.concatenate([kp_ref[...], kc_ref[...]], axis=0).astype(BF16)
        vx = jnp.concatenate([vp_ref[...], vc_ref[...]], axis=0).astype(BF16)
        for kv_head in range(2):
            kdup = _dup_half(kx, kv_head, lo)
            vdup = _dup_half(vx, kv_head, lo)
            for pr in range(4):
                pair = kv_head * 4 + pr
                qp = q_ref[:, pair * 128:(pair + 1) * 128].astype(BF16)
                o_pair = jnp.zeros((BLOCK, 128), F32)
                for half in range(2):
                    head = 2 * pair + half
                    sel = lo if half == 0 else jnp.logical_not(lo)
                    qm = jnp.where(sel, qp, jnp.zeros_like(qp))
                    p, _ = _probs(qm, kdup, dist, valid, sink_ref[head], 2.0 ** (-8.0 * (head + 1) / N_Q_HEADS))
                    o = jnp.dot(p.astype(BF16), vdup, preferred_element_type=F32)
                    o_pair = o_pair + jnp.where(sel, o, 0.0)
                o_ref[:, pair * 128:(pair + 1) * 128] = o_pair.astype(BF16)

    kv = lambda col, prev: pl.BlockSpec(
        (BLOCK, KV_WIDTH), (lambda n: (jnp.maximum(n - 1, 0), col)) if prev else (lambda n: (n, col)))
    return pl.pallas_call(
        body, grid=(nb,),
        in_specs=[pl.BlockSpec(memory_space=pltpu.SMEM),
                  pl.BlockSpec((BLOCK, ATTN_WIDTH), lambda n: (n, 0)),
                  kv(COL_K, False), kv(COL_K, True), kv(COL_V, False), kv(COL_V, True)],
        out_specs=pl.BlockSpec((BLOCK, ATTN_WIDTH), lambda n: (n, 0)),
        out_shape=jax.ShapeDtypeStruct((l, ATTN_WIDTH), BF16), name=name,
        compiler_params=_cp(("parallel",)))(sinks, proj, proj, proj, proj, proj)


def _attn_bwd(proj, sinks, dattn, *, comm=None, name):
    l = proj.shape[0]
    nb = l // BLOCK

    def body(sink_ref, q_ref, kc_ref, kp_ref, vc_ref, vp_ref, do_ref,
             dq_ref, dkc_ref, dkp_ref, dvc_ref, dvp_ref, dsink_ref):
        n = pl.program_id(0)
        dist, valid = _attn_masks(n)
        lane = lax.broadcasted_iota(jnp.int32, (1, BLOCK), 1)
        lo = lane < HEAD_DIM
        kx = jnp.concatenate([kp_ref[...], kc_ref[...]], axis=0).astype(BF16)
        vx = jnp.concatenate([vp_ref[...], vc_ref[...]], axis=0).astype(BF16)
        dsink = jnp.zeros((1, BLOCK), F32)
        dk_heads, dv_heads = [], []
        for kv_head in range(2):
            kdup = _dup_half(kx, kv_head, lo)
            vdup = _dup_half(vx, kv_head, lo)
            qm = _stack_heads(q_ref, kv_head, lo)
            dom = _stack_heads(do_ref, kv_head, lo)
            p, psink = _group_probs(qm, kdup, dist, valid, sink_ref, kv_head)
            dp = lax.dot_general(dom, vdup, (((1,), (1,)), ((), ())), preferred_element_type=F32)
            delta = jnp.sum(p * dp, axis=-1, keepdims=True)
            ds = (p * (dp - delta) * (HEAD_DIM ** -0.5)).astype(BF16)
            dsink_rows = -psink * delta
            for r in range(Q_PER_KV):
                part = jnp.sum(dsink_rows[r * BLOCK:(r + 1) * BLOCK])
                dsink = dsink + jnp.where(lane == kv_head * Q_PER_KV + r, part, 0.0)
            dq = jnp.dot(ds, kdup, preferred_element_type=F32)
            for i, dq_pair in enumerate(_unstack_heads(dq, lo)):
                pair = kv_head * 4 + i
                dq_ref[:, pair * 128:(pair + 1) * 128] = dq_pair.astype(BF16)
            dk_acc = lax.dot_general(ds, qm, (((0,), (0,)), ((), ())), preferred_element_type=F32)
            dv_acc = lax.dot_general(p.astype(BF16), dom, (((0,), (0,)), ((), ())), preferred_element_type=F32)
            dk_heads.append(dk_acc + pltpu.roll(dk_acc, HEAD_DIM, axis=1))
            dv_heads.append(dv_acc + pltpu.roll(dv_acc, HEAD_DIM, axis=1))
        dk = jnp.where(lo, dk_heads[0], dk_heads[1])
        dv = jnp.where(lo, dv_heads[0], dv_heads[1])
        dkp_ref[...] = dk[:BLOCK]
        dkc_ref[...] = dk[BLOCK:]
        dvp_ref[...] = dv[:BLOCK]
        dvc_ref[...] = dv[BLOCK:]

        @pl.when(n == 0)
        def _():
            dsink_ref[...] = jnp.zeros_like(dsink_ref)

        dsink_ref[...] += dsink

    kv = lambda col, prev: pl.BlockSpec(
        (BLOCK, KV_WIDTH), (lambda n: (jnp.maximum(n - 1, 0), col)) if prev else (lambda n: (n, col)))
    qspec = pl.BlockSpec((BLOCK, ATTN_WIDTH), lambda n: (n, 0))
    kvout = pl.BlockSpec((BLOCK, KV_WIDTH), lambda n: (n, 0))
    kvshape = jax.ShapeDtypeStruct((l, KV_WIDTH), F32)
    return _hosted_call(
        body, grid=(nb,),
        in_specs=[pl.BlockSpec(memory_space=pltpu.SMEM), qspec,
                  kv(COL_K, False), kv(COL_K, True), kv(COL_V, False), kv(COL_V, True), qspec],
        out_specs=[qspec, kvout, kvout, kvout, kvout, pl.BlockSpec((1, BLOCK), lambda n: (0, 0))],
        out_shape=[jax.ShapeDtypeStruct((l, ATTN_WIDTH), BF16), kvshape, kvshape, kvshape, kvshape,
                   jax.ShapeDtypeStruct((1, BLOCK), F32)],
        scratch_shapes=[], sem=("arbitrary",), name=name,
        args=(sinks, proj, proj, proj, proj, proj, dattn), comm=comm)


def _kv_grad_merge(dkc, dkp, dvc, dvp, *, name):
    l = dkc.shape[0]
    nb = l // BLOCK

    def body(dkc_ref, dkp_ref, dvc_ref, dvp_ref, o_ref):
        last = pl.program_id(0) == nb - 1
        o_ref[:, :KV_WIDTH] = (dkc_ref[...] + jnp.where(last, 0.0, dkp_ref[...])).astype(BF16)
        o_ref[:, KV_WIDTH:] = (dvc_ref[...] + jnp.where(last, 0.0, dvp_ref[...])).astype(BF16)

    cur = pl.BlockSpec((BLOCK, KV_WIDTH), lambda n: (n, 0))
    nxt = pl.BlockSpec((BLOCK, KV_WIDTH), lambda n: (jnp.minimum(n + 1, nb - 1), 0))
    return pl.pallas_call(
        body, grid=(nb,), in_specs=[cur, nxt, cur, nxt],
        out_specs=pl.BlockSpec((BLOCK, 2 * KV_WIDTH), lambda n: (n, 0)),
        out_shape=jax.ShapeDtypeStruct((l, 2 * KV_WIDTH), BF16), name=name,
        compiler_params=_cp(("parallel",)))(dkc, dkp, dvc, dvp)


def _discretize(a_re, a_im, log_dt, b_re, b_im):
    dt = jnp.exp(log_dt)
    mag = jnp.exp(a_re * dt)
    ab_re = mag * jnp.cos(a_im * dt)
    ab_im = mag * jnp.sin(a_im * dt)
    nr = ab_re - 1.0
    ni = ab_im
    den = a_re * a_re + a_im * a_im
    z_re = (nr * a_re + ni * a_im) / den
    z_im = (ni * a_re - nr * a_im) / den
    bb_re = z_re * b_re - z_im * b_im
    bb_im = z_re * b_im + z_im * b_re
    return ab_re, ab_im, bb_re, bb_im


def _ssm_disc_fwd(a_re, a_im, log_dt, b_re, b_im, *, name):
    def body(ar, ai, ld, br, bi, o_ar, o_ai, o_br, o_bi):
        r = _discretize(ar[...], ai[...], ld[...], br[...], bi[...])
        o_ar[...], o_ai[...], o_br[...], o_bi[...] = r

    col = jax.ShapeDtypeStruct(a_re.shape, F32)
    mat = jax.ShapeDtypeStruct(b_re.shape, F32)
    return pl.pallas_call(body, out_shape=[col, col, mat, mat], name=name)(a_re, a_im, log_dt, b_re, b_im)


def _ssm_disc_bwd(a_re, a_im, log_dt, b_re, b_im, d_ab_re, d_ab_im, d_bb_re, d_bb_im, *, name):
    def body(ar, ai, ld, br, bi, g0, g1, g2, g3, o_ar, o_ai, o_ld, o_br, o_bi):
        _, vjp = jax.vjp(_discretize, ar[...], ai[...], ld[...], br[...], bi[...])
        r = vjp((g0[...], g1[...], g2[...], g3[...]))
        o_ar[...], o_ai[...], o_ld[...], o_br[...], o_bi[...] = r

    col = jax.ShapeDtypeStruct(a_re.shape, F32)
    mat = jax.ShapeDtypeStruct(b_re.shape, F32)
    return pl.pallas_call(body, out_shape=[col, col, col, mat, mat], name=name)(
        a_re, a_im, log_dt, b_re, b_im, d_ab_re, d_ab_im, d_bb_re, d_bb_im)


def _shift_rows(x, d, rows, *, down):
    t = x.shape[0]
    if down:
        return jnp.where(rows >= d, pltpu.roll(x, d, axis=0), 0.0)
    return jnp.where(rows < t - d, pltpu.roll(x, t - d, axis=0), 0.0)


def _scan_chunk(xr, xi, ar, ai, *, down):
    t = xr.shape[0]
    rows = lax.broadcasted_iota(jnp.int32, (t, 1), 0)
    pr, pi = ar, ai
    d = 1
    while d < t:
        sr = _shift_rows(xr, d, rows, down=down)
        si = _shift_rows(xi, d, rows, down=down)
        xr, xi = xr + pr * sr - pi * si, xi + pr * si + pi * sr
        pr, pi = pr * pr - pi * pi, 2.0 * pr * pi
        d *= 2
    return xr, xi


def _ssm_fwd(proj, ab, bd, cd, dskip, *, comm=None, name):
    l = proj.shape[0]
    t = min(SSM_CHUNK, l)
    nc = l // t

    def body(u_ref, ab_ref, bd_ref, cd_ref, ds_ref, y_ref, gy_ref, xs_ref, carry_ref):
        c = pl.program_id(1)

        @pl.when(c == 0)
        def _():
            carry_ref[...] = jnp.zeros_like(carry_ref)

        u = u_ref[...]
        ar, ai = ab_ref[0, 0:1, :], ab_ref[0, 1:2, :]
        bu = jnp.dot(u.astype(BF16), bd_ref[0], preferred_element_type=F32)
        rows = lax.broadcasted_iota(jnp.int32, (t, 1), 0)
        cr, ci = carry_ref[0:1, :], carry_ref[1:2, :]
        xr = bu[:, :SSM_X_BLK] + jnp.where(rows == 0, ar * cr - ai * ci, 0.0)
        xi = bu[:, SSM_X_BLK:] + jnp.where(rows == 0, ar * ci + ai * cr, 0.0)
        xr, xi = _scan_chunk(xr, xi, ar, ai, down=True)
        xs_ref[0, :, :SSM_X_BLK] = xr
        xs_ref[0, :, SSM_X_BLK:] = xi
        carry_ref[0:1, :] = xs_ref[0, t - 1:t, :SSM_X_BLK]
        carry_ref[1:2, :] = xs_ref[0, t - 1:t, SSM_X_BLK:]
        y = jnp.dot(xs_ref[0].astype(BF16), cd_ref[0], preferred_element_type=F32) + ds_ref[...] * u
        y_ref[...] = y
        gy_ref[...] = _gelu(y).astype(BF16)

    blk = lambda shape: pl.BlockSpec((1,) + shape, lambda j, c: (j, 0, 0))
    ycol = pl.BlockSpec((t, SSM_U_BLK), lambda j, c: (c, j))
    return _hosted_call(
        body, grid=(SSM_SPLIT, nc),
        in_specs=[pl.BlockSpec((t, SSM_U_BLK), lambda j, c: (c, COL_U + j)),
                  blk((2, SSM_X_BLK)), blk((SSM_U_BLK, 2 * SSM_X_BLK)), blk((2 * SSM_X_BLK, SSM_U_BLK)),
                  pl.BlockSpec((1, SSM_U_BLK), lambda j, c: (0, j))],
        out_specs=[ycol, ycol, pl.BlockSpec((1, t, 2 * SSM_X_BLK), lambda j, c: (j, c, 0))],
        out_shape=[jax.ShapeDtypeStruct((l, SSM_WIDTH), F32), jax.ShapeDtypeStruct((l, SSM_WIDTH), BF16),
                   jax.ShapeDtypeStruct((SSM_SPLIT, l, 2 * SSM_X_BLK), F32)],
        scratch_shapes=[pltpu.VMEM((2, SSM_X_BLK), F32)], sem=("parallel", "arbitrary"), name=name,
        args=(proj, ab, bd, cd, dskip), comm=comm)


def _ssm_bwd(proj, y, dgy, xs, ab, bdt, cdt, dskip, *, comm=None, name):
    l = proj.shape[0]
    t = min(SSM_CHUNK, l)
    nc = l // t

    def body(u_ref, y_ref, dgy_ref, xs_ref, halo_ref, ab_ref, bdt_ref, cdt_ref, ds_ref,
             du_ref, dbd_ref, dcd_ref, dab_ref, dd_ref, carry_ref):
        c = pl.program_id(1)
        ci_ = nc - 1 - c

        @pl.when(c == 0)
        def _():
            carry_ref[...] = jnp.zeros_like(carry_ref)
            dbd_ref[...] = jnp.zeros_like(dbd_ref)
            dcd_ref[...] = jnp.zeros_like(dcd_ref)
            dab_ref[...] = jnp.zeros_like(dab_ref)
            dd_ref[...] = jnp.zeros_like(dd_ref)

        u = u_ref[...]
        dy = dgy_ref[...] * _gelu_grad(y_ref[...])
        dyb = dy.astype(BF16)
        ar, ai = ab_ref[0, 0:1, :], ab_ref[0, 1:2, :]
        g = jnp.dot(dyb, cdt_ref[0], preferred_element_type=F32)
        rows = lax.broadcasted_iota(jnp.int32, (t, 1), 0)
        cr, ci = carry_ref[0:1, :], carry_ref[1:2, :]
        lr = g[:, :SSM_X_BLK] + jnp.where(rows == t - 1, ar * cr + ai * ci, 0.0)
        li = g[:, SSM_X_BLK:] + jnp.where(rows == t - 1, ar * ci - ai * cr, 0.0)
        lr, li = _scan_chunk(lr, li, ar, -ai, down=False)
        lam = jnp.concatenate([lr, li], axis=1)
        carry_ref[0:1, :] = lr[0:1, :]
        carry_ref[1:2, :] = li[0:1, :]
        lamb = lam.astype(BF16)
        du_ref[...] = (jnp.dot(lamb, bdt_ref[0], preferred_element_type=F32) + ds_ref[...] * dy).astype(BF16)
        dbd_ref[0] += lax.dot_general(u.astype(BF16), lamb, (((0,), (0,)), ((), ())),
                                      preferred_element_type=F32)
        xs = xs_ref[0]
        dcd_ref[0] += lax.dot_general(xs.astype(BF16), dyb, (((0,), (0,)), ((), ())),
                                      preferred_element_type=F32)
        halo = jnp.where(ci_ > 0, halo_ref[0, 7:8, :], 0.0)
        xprev = jnp.where(rows == 0, halo, pltpu.roll(xs, 1, axis=0))
        xpr, xpi = xprev[:, :SSM_X_BLK], xprev[:, SSM_X_BLK:]
        dab_ref[0, 0:1, :] += jnp.sum(lr * xpr + li * xpi, axis=0, keepdims=True)
        dab_ref[0, 1:2, :] += jnp.sum(li * xpr - lr * xpi, axis=0, keepdims=True)
        dd_ref[...] += jnp.sum(dy * u, axis=0, keepdims=True)

    blk = lambda shape: pl.BlockSpec((1,) + shape, lambda j, c: (j, 0, 0))
    rev = lambda j, c: (nc - 1 - c, j)
    ycol = pl.BlockSpec((t, SSM_U_BLK), rev)
    hb = t // 8
    return _hosted_call(
        body, grid=(SSM_SPLIT, nc), comm=comm, sem=("parallel", "arbitrary"), name=name,
        args=(proj, y, dgy, xs, xs, ab, bdt, cdt, dskip), scratch_shapes=[pltpu.VMEM((2, SSM_X_BLK), F32)],
        in_specs=[pl.BlockSpec((t, SSM_U_BLK), lambda j, c: (nc - 1 - c, COL_U + j)), ycol, ycol,
                  pl.BlockSpec((1, t, 2 * SSM_X_BLK), lambda j, c: (j, nc - 1 - c, 0)),
                  pl.BlockSpec((1, 8, 2 * SSM_X_BLK),
                               lambda j, c: (j, jnp.maximum((nc - 1 - c) * hb - 1, 0), 0)),
                  blk((2, SSM_X_BLK)), blk((2 * SSM_X_BLK, SSM_U_BLK)), blk((SSM_U_BLK, 2 * SSM_X_BLK)),
                  pl.BlockSpec((1, SSM_U_BLK), lambda j, c: (0, j))],
        out_specs=[ycol, blk((SSM_U_BLK, 2 * SSM_X_BLK)), blk((2 * SSM_X_BLK, SSM_U_BLK)),
                   blk((2, SSM_X_BLK)), pl.BlockSpec((1, SSM_U_BLK), lambda j, c: (0, j))],
        out_shape=[jax.ShapeDtypeStruct((l, SSM_WIDTH), BF16),
                   jax.ShapeDtypeStruct((SSM_SPLIT, SSM_U_BLK, 2 * SSM_X_BLK), F32),
                   jax.ShapeDtypeStruct((SSM_SPLIT, 2 * SSM_X_BLK, SSM_U_BLK), F32),
                   jax.ShapeDtypeStruct((SSM_SPLIT, 2, SSM_X_BLK), F32),
                   jax.ShapeDtypeStruct((1, SSM_WIDTH), F32)])


def _block_diag(t):
    s, g, a, b = t.shape
    return jnp.einsum('sgab,gk->sgakb', t, jnp.eye(g, dtype=t.dtype)).reshape(s, g * a, g * b)


def _block_diag_take(t, a, b):
    s = t.shape[0]
    return jnp.einsum('sgakb,gk->sgab', t.reshape(s, 8, a, 8, b), jnp.eye(8, dtype=t.dtype))


def _glu_fwd_hook(l, tm):
    def fn(result, ins, outs, i, j):
        z = result()
        outs[0][...] = z
        outs[1][...] = (z[:, :SSM_WIDTH] * _sigmoid(z[:, SSM_WIDTH:])).astype(BF16)

    row = lambda i, j, k: (i, 0)
    return _Hook(fn, outs=[((l, 2 * SSM_WIDTH), F32, (tm, 2 * SSM_WIDTH), row),
                           ((l, SSM_WIDTH), BF16, (tm, SSM_WIDTH), row)])


def _glu_bwd_hook(z, tm):
    l = z.shape[0]

    def fn(result, ins, outs, i, j):
        zv_ref, zg_ref = ins
        dz_ref, db_ref = outs
        d = result()
        sg = _sigmoid(zg_ref[...])
        dv = d * sg
        dg = d * zv_ref[...] * sg * (1.0 - sg)
        dz_ref[:, :SSM_WIDTH] = dv.astype(BF16)
        dz_ref[:, SSM_WIDTH:] = dg.astype(BF16)

        @pl.when(i == 0)
        def _():
            db_ref[...] = jnp.zeros_like(db_ref)

        db_ref[:, :SSM_WIDTH] += jnp.sum(dv, axis=0, keepdims=True)
        db_ref[:, SSM_WIDTH:] += jnp.sum(dg, axis=0, keepdims=True)

    half = (tm, SSM_WIDTH)
    return _Hook(fn, ins=[z, z], in_specs=[(half, lambda i, j, k: (i, 0)), (half, lambda i, j, k: (i, 1))],
                 outs=[((l, 2 * SSM_WIDTH), BF16, (tm, 2 * SSM_WIDTH), lambda i, j, k: (i, 0)),
                       ((1, 2 * SSM_WIDTH), F32, (1, 2 * SSM_WIDTH), lambda i, j, k: (0, 0))])


GATE_TC = 256


def _merge_fwd(proj, a, s, *, name):
    l = a.shape[0]
    tr = min(2048, l)

    def body(ga_ref, gs_ref, a_ref, s_ref, o_ref):
        o_ref[...] = (_sigmoid(ga_ref[...]) * a_ref[...] + _sigmoid(gs_ref[...]) * s_ref[...]).astype(BF16)

    own = pl.BlockSpec((tr, GATE_TC), lambda i, j: (i, j))
    return pl.pallas_call(
        body, grid=(l // tr, D_MODEL // GATE_TC),
        in_specs=[pl.BlockSpec((tr, GATE_TC), lambda i, j: (i, COL_GA // 2 + j)),
                  pl.BlockSpec((tr, GATE_TC), lambda i, j: (i, COL_GS // 2 + j)), own, own],
        out_specs=own, out_shape=jax.ShapeDtypeStruct((l, D_MODEL), BF16), name=name,
        compiler_params=_cp(("parallel", "parallel")))(proj, proj, a, s)


def _merge_bwd_hook(proj, a, s, tm):
    def fn(result, ins, outs, i, j):
        ga_ref, gs_ref, a_br, s_br = ins
        d = result()
        sa = _sigmoid(ga_ref[...])
        ss = _sigmoid(gs_ref[...])
        outs[0][...] = (d * sa).astype(BF16)
        outs[1][...] = (d * ss).astype(BF16)
        outs[2][...] = (d * a_br[...] * sa * (1.0 - sa)).astype(BF16)
        outs[3][...] = (d * s_br[...] * ss * (1.0 - ss)).astype(BF16)

    blk = (tm, GATE_TC)
    own = lambda i, j, k: (i, j)
    return _Hook(fn, ins=[proj, proj, a, s],
                 in_specs=[(blk, lambda i, j, k: (i, COL_GA // 2 + j)), (blk, lambda i, j, k: (i, COL_GS // 2 + j)),
                           (blk, own), (blk, own)],
                 outs=[(a.shape, BF16, blk, own)] * 4)


FF_TC = D_FF // 2
FF_NJ = 2
FF_ROWS = 128


FF_HALO = 16


def _conv_taps(ext, rows):
    h = FF_HALO
    return (ext[h:h + rows], pltpu.roll(ext, 1, axis=0)[h:h + rows], pltpu.roll(ext, 2, axis=0)[h:h + rows])


def _ff_specs(tr, l):
    hb = tr // FF_HALO
    last = l // FF_HALO - 1
    prev = lambda i: jnp.maximum(i * hb - 1, 0)
    nxt = lambda i: jnp.minimum((i + 1) * hb, last)
    return dict(
        own=pl.BlockSpec((tr, FF_TC), lambda j, i: (i, j)),
        own_next=pl.BlockSpec((FF_HALO, FF_TC), lambda j, i: (nxt(i), j)),
        val=pl.BlockSpec((tr, FF_TC), lambda j, i: (i, 2 * j)),
        val_next=pl.BlockSpec((FF_HALO, FF_TC), lambda j, i: (nxt(i), 2 * j)),
        gate=pl.BlockSpec((tr, FF_TC), lambda j, i: (i, 2 * j + 1)),
        gate_prev=pl.BlockSpec((FF_HALO, FF_TC), lambda j, i: (prev(i), 2 * j + 1)),
        gate_next=pl.BlockSpec((FF_HALO, FF_TC), lambda j, i: (nxt(i), 2 * j + 1)),
        pair=pl.BlockSpec((tr, 2 * FF_TC), lambda j, i: (i, j)),
        w=pl.BlockSpec((3, FF_TC), lambda j, i: (0, j)),
        b=pl.BlockSpec((1, FF_TC), lambda j, i: (0, j)))


def _ffn_act_fwd(up, conv_w, conv_b, *, name):
    l = up.shape[0]
    tr = min(FF_ROWS, l)

    def body(v_ref, g_ref, prev_ref, w_ref, b_ref, o_ref):
        prev = jnp.where(pl.program_id(1) == 0, 0.0, prev_ref[...].astype(F32))
        g0, g1, g2 = _conv_taps(jnp.concatenate([prev, g_ref[...].astype(F32)], axis=0), tr)
        gc = b_ref[...] + w_ref[0:1, :] * g2 + w_ref[1:2, :] * g1 + w_ref[2:3, :] * g0
        o_ref[...] = (v_ref[...].astype(F32) * _gelu(gc)).astype(BF16)

    sp = _ff_specs(tr, l)
    return pl.pallas_call(
        body, grid=(FF_NJ, l // tr), in_specs=[sp['val'], sp['gate'], sp['gate_prev'], sp['w'], sp['b']],
        out_specs=sp['own'], out_shape=jax.ShapeDtypeStruct((l, D_FF), BF16), name=name,
        compiler_params=_cp(("parallel", "parallel")))(up, up, up, conv_w, conv_b)


def _ffn_act_bwd(dact, up, conv_w, conv_b, *, comm=None, name):
    l = up.shape[0]
    tr = min(FF_ROWS, l)
    ni = l // tr
    te = tr + 8

    def body(d_ref, dn_ref, v_ref, vn_ref, g_ref, gp_ref, gn_ref, w_ref, b_ref, dup_ref, dw_ref, db_ref):
        i = pl.program_id(1)
        f32 = lambda ref, rows=None: ref[...].astype(F32)[:rows]
        prev = jnp.where(i == 0, 0.0, f32(gp_ref))
        g0, g1, g2 = _conv_taps(jnp.concatenate([prev, f32(g_ref), f32(gn_ref, 8)], axis=0), te)
        w0, w1, w2 = w_ref[0:1, :], w_ref[1:2, :], w_ref[2:3, :]
        gc = b_ref[...] + w0 * g2 + w1 * g1 + w2 * g0
        d_own = f32(d_ref)
        d = jnp.concatenate([d_own, jnp.where(i == ni - 1, 0.0, f32(dn_ref, 8))], axis=0)
        v = jnp.concatenate([f32(v_ref), f32(vn_ref, 8)], axis=0)
        dgc = d * v * _gelu_grad(gc)
        ahead1 = pltpu.roll(dgc, te - 1, axis=0)[:tr]
        ahead2 = pltpu.roll(dgc, te - 2, axis=0)[:tr]
        own = dgc[:tr]
        dup_ref[:, :FF_TC] = (d_own * _gelu(gc[:tr])).astype(BF16)
        dup_ref[:, FF_TC:] = (w2 * own + w1 * ahead1 + w0 * ahead2).astype(BF16)

        @pl.when(i == 0)
        def _():
            dw_ref[...] = jnp.zeros_like(dw_ref)
            db_ref[...] = jnp.zeros_like(db_ref)

        dw_ref[0:1, :] += jnp.sum(own * g2[:tr], axis=0, keepdims=True)
        dw_ref[1:2, :] += jnp.sum(own * g1[:tr], axis=0, keepdims=True)
        dw_ref[2:3, :] += jnp.sum(own * g0[:tr], axis=0, keepdims=True)
        db_ref[...] += jnp.sum(own, axis=0, keepdims=True)

    sp = _ff_specs(tr, l)
    return _hosted_call(
        body, grid=(FF_NJ, ni),
        in_specs=[sp['own'], sp['own_next'], sp['val'], sp['val_next'], sp['gate'], sp['gate_prev'],
                  sp['gate_next'], sp['w'], sp['b']],
        out_specs=[sp['pair'], sp['w'], sp['b']],
        out_shape=[jax.ShapeDtypeStruct((l, 2 * D_FF), BF16), jax.ShapeDtypeStruct((3, D_FF), F32),
                   jax.ShapeDtypeStruct((1, D_FF), F32)],
        scratch_shapes=[], sem=("parallel", "arbitrary"), name=name,
        args=(dact, dact, up, up, up, up, up, conv_w, conv_b), comm=comm)


def _col_sum(a, *, name):
    l, n = a.shape
    tr = min(512, l)

    def body(a_ref, o_ref):
        @pl.when(pl.program_id(0) == 0)
        def _():
            o_ref[...] = jnp.zeros_like(o_ref)

        o_ref[...] += jnp.sum(a_ref[...].astype(F32), axis=0, keepdims=True)

    return pl.pallas_call(
        body, grid=(l // tr,), in_specs=[pl.BlockSpec((tr, n), lambda i: (i, 0))],
        out_specs=pl.BlockSpec((1, n), lambda i: (0, 0)), out_shape=jax.ShapeDtypeStruct((1, n), F32),
        name=name, compiler_params=_cp(("arbitrary",)))(a)


def _local_step(x, target, wts, small, shards=None):
    l = x.shape[0]
    wts = dict(wts)
    grads, recvs, sgr = {}, {}, {}
    lay = lambda keys: [LAYOUT[k] for k in keys]
    none = lambda keys: None
    gather = (lambda keys: _GatherPlan([shards[k] for k in keys], lay(keys))) if shards is not None else none
    scatter = (lambda keys: _ScatterPlan([grads[k] for k in keys], lay(keys))) if shards is not None else none

    mm = _matmul

    def take(res, plan, keys, store):
        outs, couts = res
        if plan is not None:
            store.update(zip(keys, couts))
        return outs

    def mm_plan(plan, keys, store, *args, **kw):
        if plan is None:
            return _matmul(*args, **kw)
        return take(_matmul(*args, comm=plan, **kw), plan, keys, store)

    def mm_host(keys, make_plan, store, *args, **kw):
        return mm_plan(make_plan(keys), keys, store, *args, **kw)

    dist = shards is not None
    up_gather = lambda p, base: _GatherPlan([shards['w_up']], lay(['w_up']), part=(p, 2), base=base) if dist else None
    up_scatter = lambda p: _ScatterPlan([grads['w_up']], lay(['w_up']), part=(p, 2)) if dist else None

    col = lambda t: t.reshape(SSM_GROUPS * SSM_STATE, 1)
    a_re, a_im = col(small['ssm_a_re']), col(small['ssm_a_im'])
    log_dt = jnp.repeat(small['ssm_log_dt'].reshape(SSM_GROUPS), SSM_STATE).reshape(-1, 1)
    b_re = small['ssm_b_re'].reshape(SSM_GROUPS * SSM_STATE, SSM_GROUP)
    b_im = small['ssm_b_im'].reshape(SSM_GROUPS * SSM_STATE, SSM_GROUP)
    ab_re, ab_im, bb_re, bb_im = _ssm_disc_fwd(a_re, a_im, log_dt, b_re, b_im, name="ssm_disc_fwd")
    ab = jnp.stack([ab_re.reshape(SSM_SPLIT, SSM_X_BLK), ab_im.reshape(SSM_SPLIT, SSM_X_BLK)], axis=1)
    to_bd = lambda t: _block_diag(t.reshape(SSM_SPLIT, 8, SSM_STATE, SSM_GROUP).transpose(0, 1, 3, 2))
    bd = jnp.concatenate([to_bd(bb_re), to_bd(bb_im)], axis=2)
    c_re = small['ssm_c_re'].reshape(SSM_SPLIT, 8, SSM_GROUP, SSM_STATE)
    c_im = small['ssm_c_im'].reshape(SSM_SPLIT, 8, SSM_GROUP, SSM_STATE)
    cdt = jnp.concatenate([_block_diag(c_re), -_block_diag(c_im)], axis=2)
    bd_b, cdt_b = bd.astype(BF16), cdt.astype(BF16)
    bdt_b, cd_b = bd_b.transpose(0, 2, 1), cdt_b.transpose(0, 2, 1)
    dskip = small['ssm_d'].reshape(1, SSM_WIDTH)

    sinks = small['attn_sinks'].reshape(N_Q_HEADS)
    plan = gather(['w_in_t'])
    h1, = take(_rms_fwd(x, small['attn_norm_g'], comm=plan, name="rms1_fwd"), plan, ['w_in_t'], wts)
    keys = ['w_glu', 'w_ba', 'w_bs', 'w_out']
    plan = _Plans([gather(keys), up_gather(0, None)]) if dist else None
    proj = mm_plan(plan, keys + ['w_up'], wts,
                   h1, wts['w_in_t'], tb=True, tm=512, tn=2944, tk=2048, inner='m', out_dtype=F32,
                   bias=small['b_in'], name="mm_in")
    attn = _attn_fwd(proj, sinks, name="attn_fwd")
    plan = up_gather(1, [wts['w_up']])
    y, gy, xs = take(_ssm_fwd(proj, ab, bd_b, cd_b, dskip, comm=plan, name="ssm_fwd"), plan, ['w_up'], wts)
    z, ssm = mm(gy, wts['w_glu'], tm=1024, tn=1024, tk=512, bias=small['b_glu'],
                epilogue=_glu_fwd_hook(l, min(1024, l)), name="mm_glu")
    a_br = mm(attn, wts['w_ba'], tm=1024, tn=1024, tk=1024, out_dtype=F32, name="mm_ba")
    s_br = mm(ssm, wts['w_bs'], tm=1024, tn=1024, tk=512, out_dtype=F32, name="mm_bs")
    tr = min(ROW_TILE, l)
    merged = _merge_fwd(proj, a_br, s_br, name="merge_fwd")
    x2 = mm(merged, wts['w_out'], tm=1024, tn=1024, tk=2048, inner='m', out_dtype=F32, res=x, name="mm_out")
    h2, = take(_rms_fwd(x2, small['ffn_norm_g'], name="rms2_fwd"), None, [], wts)
    up = mm_host(['w_down'], gather, wts,
                 h2, wts['w_up'], tm=1024, tn=1024, tk=2048, out_dtype=BF16, name="mm_up")
    conv_w, conv_b = small['conv_w'], small['conv_b']
    act = _ffn_act_fwd(up, conv_w, conv_b, name="ffn_act_fwd")
    x3 = mm(act, wts['w_down'], tm=1024, tn=1024, tk=2816, out_dtype=F32, res=x2, name="mm_down")
    dx3, dx3b, d_g3, loss = _rowwise(
        _final_loss_hook(small['final_norm_g'].reshape(1, D_MODEL), target, tr), x3, tr, name="final_loss")

    sgr['final_norm_g'] = d_g3.reshape(D_MODEL)
    dact = mm(dx3b, wts['w_down'], tb=True, tm=512, tn=2816, tk=2048, inner='m', out_dtype=BF16, name="mm_dact")
    grads['w_down'] = mm(act, dx3b, ta=True, tm=512, tn=1024, tk=2048, out_dtype=BF16, name="mm_dw_down")
    plan = scatter(['w_down'])
    dup, sgr['conv_w'], sgr['conv_b'] = take(
        _ffn_act_bwd(dact, up, conv_w, conv_b, comm=plan, name="ffn_act_bwd"), plan, ['w_down'], recvs)
    grads['w_up'] = mm(h2, dup, ta=True, tm=1024, tn=1024, tk=2048, out_dtype=BF16, name="mm_dw_up")
    dh2 = mm_plan(up_scatter(0), ['w_up#0'], recvs,
                  dup, wts['w_up'], tb=True, tm=1024, tn=1024, tk=2816, out_dtype=F32, name="mm_dh2")
    dx2, dx2b, sgr['ffn_norm_g'] = _rowwise(
        _rms_bwd_hook(x2, small['ffn_norm_g'], dx3, tr, with_bf16=True), dh2, tr, name="rms2_bwd")

    d_a, d_s, dga, dgs = mm(dx2b, wts['w_out'], tb=True, tm=1024, tn=GATE_TC, tk=2048,
                            epilogue=_merge_bwd_hook(proj, a_br, s_br, min(1024, l)), name="mm_dmerged")
    grads['w_out'] = mm(merged, dx2b, ta=True, tm=1024, tn=1024, tk=2048, out_dtype=BF16, name="mm_dw_out")
    dattn = mm(d_a, wts['w_ba'], tb=True, tm=1024, tn=1024, tk=2048, inner='m', out_dtype=BF16, name="mm_dattn")
    grads['w_ba'] = mm(attn, d_a, ta=True, tm=1024, tn=1024, tk=2048, out_dtype=BF16, name="mm_dw_ba")
    dz, sgr['b_glu'] = mm(d_s, wts['w_bs'], tb=True, tm=1024, tn=512, tk=2048, sequential=True,
                          epilogue=_glu_bwd_hook(z, min(1024, l)), name="mm_dssm")
    grads['w_bs'] = mm(ssm, d_s, ta=True, tm=512, tn=1024, tk=2048, out_dtype=BF16, name="mm_dw_bs")
    grads['w_glu'] = mm(gy, dz, ta=True, tm=512, tn=1024, tk=2048, out_dtype=BF16, name="mm_dw_glu")
    dgy = mm(dz, wts['w_glu'], tb=True, tm=1024, tn=512, tk=1024, inner='m', out_dtype=F32, name="mm_dgy")
    plan = up_scatter(1)
    du, d_bd, d_cd, d_ab, sgr['ssm_d'] = take(
        _ssm_bwd(proj, y, dgy, xs, ab, bdt_b, cdt_b, dskip, comm=plan, name="ssm_bwd"), plan, ['w_up#1'], recvs)
    keys = ['w_out', 'w_ba', 'w_bs', 'w_glu']
    plan = scatter(keys)
    dq, dkc, dkp, dvc, dvp, dsink = take(
        _attn_bwd(proj, sinks, dattn, comm=plan, name="attn_bwd"), plan, keys, recvs)
    dkv = _kv_grad_merge(dkc, dkp, dvc, dvp, name="kv_grad_merge")
    sgr['attn_sinks'] = dsink[:, :N_Q_HEADS]
    dproj = jnp.concatenate([dq, dkv, du, dga, dgs], axis=1)
    sgr['b_in'] = _col_sum(dproj, name="col_sum_dproj")
    grads['w_in_t'] = mm(dproj, h1, ta=True, tm=2944, tn=1024, tk=1024, out_dtype=BF16, name="mm_dw_in")
    dh1 = mm_host(['w_in_t'], scatter, recvs,
                  dproj, wts['w_in_t'], tm=1024, tn=1024, tk=2944, out_dtype=F32, name="mm_dh1")
    grad_x, sgr['attn_norm_g'] = _rowwise(
        _rms_bwd_hook(x, small['attn_norm_g'], dx2, tr, with_bf16=False), dh1, tr, name="rms1_bwd")

    from_bd = lambda t: _block_diag_take(t, SSM_GROUP, SSM_STATE).transpose(0, 1, 3, 2).reshape(
        SSM_GROUPS * SSM_STATE, SSM_GROUP)
    d_bb_re = from_bd(d_bd[:, :, :SSM_X_BLK])
    d_bb_im = from_bd(d_bd[:, :, SSM_X_BLK:])
    d_cdt = d_cd.transpose(0, 2, 1)
    shape_c = (1, SSM_GROUPS, SSM_GROUP, SSM_STATE)
    sgr['ssm_c_re'] = _block_diag_take(d_cdt[:, :, :SSM_X_BLK], SSM_GROUP, SSM_STATE).reshape(shape_c)
    sgr['ssm_c_im'] = -_block_diag_take(d_cdt[:, :, SSM_X_BLK:], SSM_GROUP, SSM_STATE).reshape(shape_c)
    d_a_re, d_a_im, d_ldt, d_b_re, d_b_im = _ssm_disc_bwd(
        a_re, a_im, log_dt, b_re, b_im, d_ab[:, 0, :].reshape(-1, 1), d_ab[:, 1, :].reshape(-1, 1),
        d_bb_re, d_bb_im, name="ssm_disc_bwd")
    sgr['ssm_a_re'] = d_a_re.reshape(1, SSM_GROUPS, SSM_STATE)
    sgr['ssm_a_im'] = d_a_im.reshape(1, SSM_GROUPS, SSM_STATE)
    sgr['ssm_log_dt'] = d_ldt.reshape(SSM_GROUPS, SSM_STATE).sum(axis=1).reshape(1, SSM_GROUPS)
    sgr['ssm_b_re'] = d_b_re.reshape(1, SSM_GROUPS, SSM_STATE, SSM_GROUP)
    sgr['ssm_b_im'] = d_b_im.reshape(1, SSM_GROUPS, SSM_STATE, SSM_GROUP)
    return loss, grad_x, grads, recvs, sgr


def _swap_cores(arrs, *, name):
    n = len(arrs)

    def body(*refs):
        ins, outs = refs[:n], refs[n:2 * n]
        send_sems, recv_sems = refs[2 * n:]
        x, y, c = _place()
        copies = []
        for i in range(n):
            cp = pltpu.make_async_remote_copy(
                src_ref=ins[i], dst_ref=outs[i], send_sem=send_sems.at[i], recv_sem=recv_sems.at[i],
                device_id=(x, y, 1 - c), device_id_type=MESH)
            cp.start()
            copies.append(cp)
        for cp in copies:
            cp.wait()

    return pl.pallas_call(
        body, in_specs=[ANY] * n, out_specs=[ANY] * n,
        out_shape=[jax.ShapeDtypeStruct(a.shape, a.dtype) for a in arrs],
        scratch_shapes=[pltpu.SemaphoreType.DMA((n,)), pltpu.SemaphoreType.DMA((n,))],
        name=name)(*arrs)


def _all_reduce_small(buf, *, name):
    r = buf.shape[0]

    def body(in_ref, out_ref, slots, send_sems, recv_sems):
        x, y, c = _place()
        me = 4 * x + 2 * y + c
        slots[pl.ds(me, 1)] = in_ref[...][None]
        copies = []
        for k in range(N_DEV - 1):
            bx, by, bc = ((k + 1) >> 2) & 1, ((k + 1) >> 1) & 1, (k + 1) & 1
            peer = (1 - x if bx else x, 1 - y if by else y, 1 - c if bc else c)
            cp = pltpu.make_async_remote_copy(
                src_ref=in_ref, dst_ref=slots.at[me], send_sem=send_sems.at[k], recv_sem=recv_sems.at[k],
                device_id=peer, device_id_type=MESH)
            cp.start()
            copies.append(cp)
        for cp in copies:
            cp.wait()
        acc = slots[0]
        for d in range(1, N_DEV):
            acc = acc + slots[d]
        out_ref[...] = acc

    vm = pl.BlockSpec(memory_space=pltpu.VMEM)
    return pl.pallas_call(
        body, in_specs=[vm], out_specs=vm, out_shape=jax.ShapeDtypeStruct((r, 128), F32),
        scratch_shapes=[pltpu.VMEM((N_DEV, r, 128), F32), pltpu.SemaphoreType.DMA((N_DEV - 1,)),
                        pltpu.SemaphoreType.DMA((N_DEV - 1,))],
        name=name)(buf)


def _pack(arrs):
    flat = jnp.concatenate([a.reshape(-1).astype(F32) for a in arrs])
    pad = (-flat.shape[0]) % 1024
    return jnp.pad(flat, (0, pad)).reshape(-1, 128)


def _unpack(buf, shapes):
    flat = buf.reshape(-1)
    out, pos = [], 0
    for s in shapes:
        size = math.prod(s)
        out.append(flat[pos:pos + size].reshape(s))
        pos += size
    return out


TILE_ELEMS = 256 * 1024


def _tile_rows(r, c):
    if r * c <= TILE_ELEMS:
        return r
    for tr in range(TILE_ELEMS // c // 16 * 16, 0, -16):
        if r % tr == 0:
            return tr
    raise ValueError((r, c))


def _sum4(own, recvs, *, name):
    r, c = own.shape
    parts = len(recvs)
    tr = _tile_rows(r // parts, c)
    per = r // parts // tr

    def body(o_ref, *refs):
        out_ref = refs[parts]
        for p in range(parts):
            @pl.when(pl.program_id(0) // per == p)
            def _():
                acc = o_ref[...].astype(F32)
                for k in range(3):
                    acc = acc + refs[p][k].astype(F32)
                out_ref[...] = acc

    part_spec = lambda p: pl.BlockSpec((3, tr, c), lambda i: (0, jnp.clip(i - p * per, 0, per - 1), 0))
    return pl.pallas_call(
        body, grid=(r // tr,),
        in_specs=[pl.BlockSpec((tr, c), lambda i: (i, 0))] + [part_spec(p) for p in range(parts)],
        out_specs=pl.BlockSpec((tr, c), lambda i: (i, 0)), out_shape=jax.ShapeDtypeStruct((r, c), F32),
        name=name, compiler_params=_cp(("parallel",)))(own, *recvs)


def _adamw(w, ga, gb, m, v, *, name):
    r, c = w.shape
    tr = _tile_rows(r, c)
    bc1 = 1.0 - ADAM_B1 ** ADAM_STEP
    bc2 = 1.0 - ADAM_B2 ** ADAM_STEP
    two = gb is not None

    def body(*refs):
        w_ref, ga_ref = refs[0], refs[1]
        pos = 2
        g = ga_ref[...]
        if two:
            g = g + refs[pos][...]
            pos += 1
        m_ref, v_ref, g_out, d_out, m_out, v_out = refs[pos:pos + 6]
        mn = ADAM_B1 * m_ref[...] + (1.0 - ADAM_B1) * g
        vn = ADAM_B2 * v_ref[...] + (1.0 - ADAM_B2) * (g * g)
        m_hat = mn / bc1
        v_hat = vn / bc2
        g_out[...] = g
        d_out[...] = -ADAM_LR * (m_hat / (jnp.sqrt(v_hat) + ADAM_EPS) + ADAM_WD * w_ref[...])
        m_out[...] = mn
        v_out[...] = vn

    spec = pl.BlockSpec((tr, c), lambda i: (i, 0))
    args = [w, ga] + ([gb] if two else []) + [m, v]
    shp = jax.ShapeDtypeStruct((r, c), F32)
    return pl.pallas_call(
        body, grid=(r // tr,), in_specs=[spec] * len(args), out_specs=[spec] * 4,
        out_shape=[shp] * 4, name=name, compiler_params=_cp(("parallel",)))(*args)


BIG = ['w_in', 'w_glu', 'w_branch_attn', 'w_branch_ssm', 'w_out', 'w_up', 'w_down']
BIG_KEY = {'w_in': 'w_in_t', 'w_glu': 'w_glu', 'w_branch_attn': 'w_ba', 'w_branch_ssm': 'w_bs',
           'w_out': 'w_out', 'w_up': 'w_up', 'w_down': 'w_down'}
TRANSPOSED = {'w_in'}
SMALL = ['attn_norm_g', 'b_in', 'attn_sinks', 'ssm_a_re', 'ssm_a_im', 'ssm_log_dt', 'ssm_b_re', 'ssm_b_im',
         'ssm_c_re', 'ssm_c_im', 'ssm_d', 'b_glu', 'ffn_norm_g', 'conv_b', 'final_norm_g']
WEIGHTS = ['attn_norm_g', 'w_in', 'b_in', 'attn_sinks', 'ssm_a_re', 'ssm_a_im', 'ssm_log_dt', 'ssm_b_re',
           'ssm_b_im', 'ssm_c_re', 'ssm_c_im', 'ssm_d', 'w_glu', 'b_glu', 'w_branch_attn', 'w_branch_ssm',
           'w_out', 'ffn_norm_g', 'w_up', 'conv_w', 'conv_b', 'w_down', 'final_norm_g']


def _shard_2d(name, t):
    t = t[0]
    return t.T if name in TRANSPOSED else t


def _unshard_2d(name, t):
    return (t.T if name in TRANSPOSED else t)[None]


def kernel(x, attn_norm_g, w_in, b_in, attn_sinks, ssm_a_re, ssm_a_im, ssm_log_dt, ssm_b_re, ssm_b_im, ssm_c_re, ssm_c_im, ssm_d, w_glu, b_glu, w_branch_attn, w_branch_ssm, w_out, ffn_norm_g, w_up, conv_w, conv_b, w_down, final_norm_g, loss_target, m_attn_norm_g, m_w_in, m_b_in, m_attn_sinks, m_ssm_a_re, m_ssm_a_im, m_ssm_log_dt, m_ssm_b_re, m_ssm_b_im, m_ssm_c_re, m_ssm_c_im, m_ssm_d, m_w_glu, m_b_glu, m_w_branch_attn, m_w_branch_ssm, m_w_out, m_ffn_norm_g, m_w_up, m_conv_w, m_conv_b, m_w_down, m_final_norm_g, v_attn_norm_g, v_w_in, v_b_in, v_attn_sinks, v_ssm_a_re, v_ssm_a_im, v_ssm_log_dt, v_ssm_b_re, v_ssm_b_im, v_ssm_c_re, v_ssm_c_im, v_ssm_d, v_w_glu, v_b_glu, v_w_branch_attn, v_w_branch_ssm, v_w_out, v_ffn_norm_g, v_w_up, v_conv_w, v_conv_b, v_w_down, v_final_norm_g):
    args = dict(locals())
    w = {n: args[n] for n in WEIGHTS}
    m = {n: args['m_' + n] for n in WEIGHTS}
    v = {n: args['v_' + n] for n in WEIGHTS}
    xi, yi, ci = _place()
    blk = 2 * xi + yi

    shards = {BIG_KEY[n]: _shard_2d(n, w[n]).astype(BF16) for n in BIG}
    cw_cols = w['conv_w'].shape[2]
    cw_place = lax.dynamic_update_slice(jnp.zeros((3, D_FF), F32), w['conv_w'][0] * (ci == 0).astype(F32),
                                        (0, blk * cw_cols))
    conv_w_full = _unpack(_all_reduce_small(_pack([cw_place]), name="gather_conv_w"), [(3, D_FF)])[0]

    small = {n: w[n] for n in SMALL}
    small['conv_w'] = conv_w_full
    loss_part, grad_x, grads, recvs, sgr = _local_step(x[0], loss_target[0], {}, small, shards)

    halves = []
    for n in BIG:
        key = BIG_KEY[n]
        full = grads[key]
        recv = [recvs[key]] if key in recvs else [recvs[key + '#0'], recvs[key + '#1']]
        axis, interleaved = LAYOUT[key]
        size = full.shape[axis] // N_CHIPS
        own = lax.dynamic_slice_in_dim(full, _block_pos(xi, yi, interleaved) * size, size, axis=axis)
        halves.append(_sum4(own, recv, name="sum4_" + n))
    others = _swap_cores(halves, name="swap_cores")
    out = {}
    for n, mine, other in zip(BIG, halves, others):
        res = _adamw(_shard_2d(n, w[n]), mine, other, _shard_2d(n, m[n]), _shard_2d(n, v[n]), name="adamw_" + n)
        out[n] = [_unshard_2d(n, t) for t in res]

    names = SMALL + ['conv_w']
    shapes = [w[n].shape for n in SMALL] + [(3, D_FF)]
    packed = _pack([sgr[n] for n in names] + [loss_part])
    summed = _unpack(_all_reduce_small(packed, name="all_reduce_small"), shapes + [(1, 1)])
    loss = summed[-1].reshape(())
    sg = dict(zip(names, summed[:-1]))
    sg['conv_w'] = lax.dynamic_slice_in_dim(sg['conv_w'], blk * cw_cols, cw_cols, axis=1)[None]
    res = _adamw(_pack([w[n] for n in names]), _pack([sg[n] for n in names]), None,
                 _pack([m[n] for n in names]), _pack([v[n] for n in names]), name="adamw_small")
    ushapes = [w[n].shape for n in names]
    unpacked = [_unpack(t, ushapes) for t in res]
    for i, n in enumerate(names):
        out[n] = [u[i] for u in unpacked]

    return (loss, grad_x[None], *[out[n][0] for n in WEIGHTS], *[out[n][1] for n in WEIGHTS],
            *[out[n][2] for n in WEIGHTS], *[out[n][3] for n in WEIGHTS])
```

```python
import functools
import math

import jax
import jax.numpy as jnp
from jax import lax
from jax.experimental import pallas as pl
from jax.experimental.pallas import tpu as pltpu

F32 = jnp.float32
BF16 = jnp.bfloat16

D_MODEL = 2048
N_Q_HEADS = 16
HEAD_DIM = 64
ATTN_WIDTH = 1024
KV_WIDTH = 128
BLOCK = 128
SSM_WIDTH = 512
SSM_GROUPS = 32
SSM_GROUP = 16
SSM_STATE = 64
D_FF = 5632
IN_COLS = 5888
RMS_EPS = 1e-6
NEG_BIG = -1e30
N_CHIPS = 4
N_DEV = 8

COL_K = 8
COL_V = 9
COL_U = 10
COL_GA = 14
COL_GS = 30

SSM_SPLIT = 4
SSM_U_BLK = 128
SSM_X_BLK = 512
SSM_CHUNK = 256

ADAM_LR = 0.001
ADAM_B1 = 0.9
ADAM_B2 = 0.999
ADAM_EPS = 1e-08
ADAM_WD = 0.01
ADAM_STEP = 10

VMEM_LIMIT_BYTES = 56 * 1024 * 1024
INV_SQRT2 = 1.0 / math.sqrt(2.0)
INV_SQRT2PI = 1.0 / math.sqrt(2.0 * math.pi)
MESH = pl.DeviceIdType.MESH
ANY = pl.BlockSpec(memory_space=pl.ANY)


def _cp(sem):
    return pltpu.CompilerParams(dimension_semantics=sem, vmem_limit_bytes=VMEM_LIMIT_BYTES)


def _gelu(x):
    return 0.5 * x * (1.0 + lax.erf(x * INV_SQRT2))


def _gelu_grad(x):
    return 0.5 * (1.0 + lax.erf(x * INV_SQRT2)) + x * jnp.exp(-0.5 * x * x) * INV_SQRT2PI


def _sigmoid(x):
    return 1.0 / (1.0 + jnp.exp(-x))


def _place():
    return lax.axis_index("x"), lax.axis_index("y"), lax.axis_index("c")


def _other_chips(x, y):
    return [(1 - x, y), (x, 1 - y), (1 - x, 1 - y)]


def _block_pos(x, y, interleaved):
    return x + 2 * y if interleaved else 2 * x + y


LAYOUT = {'w_in_t': (0, False), 'w_glu': (0, False), 'w_ba': (0, False), 'w_bs': (0, False),
          'w_out': (0, False), 'w_up': (1, True), 'w_down': (0, False)}
REGROUPED = ('w_glu', 'w_ba', 'w_bs')


def _blocks_to_columns(t):
    k = t.shape[0] // N_CHIPS
    return t.reshape(N_CHIPS, k, t.shape[1]).transpose(1, 0, 2).reshape(k, N_CHIPS * t.shape[1])


def _columns_to_blocks(t):
    k, ns = t.shape[0], t.shape[1] // N_CHIPS
    return t.reshape(k, N_CHIPS, ns).transpose(1, 0, 2).reshape(N_CHIPS * k, ns)


def _window(ref, axis, pos, size, rows=None):
    if axis == 0:
        start, count = (0, size) if rows is None else rows
        return ref.at[pl.ds(pos * size + start, count), :]
    cols = pl.ds(pos * size, size)
    return ref.at[:, cols] if rows is None else ref.at[pl.ds(rows[0], rows[1]), cols]


def _gathered_shape(shape, axis):
    return tuple(N_CHIPS * d if a == axis else d for a, d in enumerate(shape))


def _block_shape(shape, axis):
    return tuple(d // N_CHIPS if a == axis else d for a, d in enumerate(shape))


class _GatherPlan:
    def __init__(self, shards, layouts, part=(0, 1), base=None):
        self.arrays = list(shards)
        self.layouts = list(layouts)
        self.part = part
        self.base = list(base) if base is not None else []
        assert (part[0] == 0) == (base is None)
        n = len(shards)
        self.out_shape = [jax.ShapeDtypeStruct(_gathered_shape(s.shape, lay[0]), s.dtype)
                          for s, lay in zip(shards, layouts)]
        self.scratch = [pltpu.SemaphoreType.DMA((6 * n,)), pltpu.SemaphoreType.DMA((6 * n,)),
                        pltpu.SemaphoreType.DMA((n,))]

    def _copies(self, kind, ins, outs, sems):
        send, recv, local = sems
        n = len(self.arrays)
        x, y, c = _place()
        copies = []
        for i in range(n):
            axis, interleaved = self.layouts[i]
            size = self.arrays[i].shape[axis]
            half = self.arrays[i].shape[0] // 2
            h = half // self.part[1]
            first = lambda core: core * half + self.part[0] * h
            blk = _block_pos(x, y, interleaved)
            if kind == 'mine':
                if self.part[0] == 0:
                    copies.append(pltpu.make_async_copy(ins[i], _window(outs[i], axis, blk, size), local.at[i]))
                continue
            for k, (px, py) in enumerate(_other_chips(x, y)):
                theirs = _block_pos(px, py, interleaved)
                if kind in ('ici_out', 'ici_in'):
                    route = dict(send_sem=send.at[3 * i + k], recv_sem=recv.at[3 * i + k],
                                 device_id=(px, py, c), device_id_type=MESH)
                else:
                    route = dict(send_sem=send.at[3 * (n + i) + k], recv_sem=recv.at[3 * (n + i) + k],
                                 device_id=(x, y, 1 - c), device_id_type=MESH)
                if kind == 'ici_out':
                    src, dst = ins[i].at[pl.ds(first(c), h), :], _window(outs[i], axis, blk, size, (first(c), h))
                elif kind == 'd2d_in':
                    src = dst = _window(outs[i], axis, theirs, size, (first(1 - c), h))
                else:
                    src = dst = _window(outs[i], axis, theirs, size, (first(c), h))
                copies.append(pltpu.make_async_remote_copy(src_ref=src, dst_ref=dst, **route))
        return copies

    def start(self, ins, outs, sems):
        for cp in self._copies('mine', ins, outs, sems) + self._copies('ici_out', ins, outs, sems):
            cp.start()

    def middle(self, ins, outs, sems):
        for arrived, onward in zip(self._copies('ici_in', ins, outs, sems), self._copies('d2d_out', ins, outs, sems)):
            arrived.wait_recv()
            onward.start()

    def finish(self, ins, outs, sems):
        for cp in self._copies('d2d_in', ins, outs, sems):
            cp.wait_recv()
        for cp in self._copies('ici_out', ins, outs, sems) + self._copies('d2d_out', ins, outs, sems):
            cp.wait_send()
        for cp in self._copies('mine', ins, outs, sems):
            cp.wait()


class _ScatterPlan:
    def __init__(self, fulls, layouts, part=(0, 1)):
        self.arrays = list(fulls)
        self.layouts = list(layouts)
        self.part = part
        self.base = []
        n = len(fulls)
        self.out_shape = []
        for f, lay in zip(fulls, layouts):
            rows, cols = _block_shape(f.shape, lay[0])
            self.out_shape.append(jax.ShapeDtypeStruct((3, rows // part[1], cols), f.dtype))
        self.scratch = [pltpu.SemaphoreType.DMA((3 * n,)), pltpu.SemaphoreType.DMA((3 * n,))]

    def _copies(self, ins, outs, sems):
        send, recv = sems
        x, y, c = _place()
        copies = []
        for i in range(len(self.arrays)):
            axis, interleaved = self.layouts[i]
            size = self.arrays[i].shape[axis] // N_CHIPS
            h = _block_shape(self.arrays[i].shape, axis)[0] // self.part[1]
            rows = (self.part[0] * h, h)
            for k, (px, py) in enumerate(_other_chips(x, y)):
                copies.append(pltpu.make_async_remote_copy(
                    src_ref=_window(ins[i], axis, _block_pos(px, py, interleaved), size, rows), dst_ref=outs[i].at[k],
                    send_sem=send.at[3 * i + k], recv_sem=recv.at[3 * i + k],
                    device_id=(px, py, c), device_id_type=MESH))
        return copies

    def start(self, ins, outs, sems):
        for cp in self._copies(ins, outs, sems):
            cp.start()

    def middle(self, ins, outs, sems):
        pass

    def finish(self, ins, outs, sems):
        for cp in self._copies(ins, outs, sems):
            cp.wait()


class _Plans:
    def __init__(self, plans):
        assert not any(p.base for p in plans)
        self.plans = list(plans)
        self.base = []
        self.arrays = [a for p in plans for a in p.arrays]
        self.out_shape = [o for p in plans for o in p.out_shape]
        self.scratch = [s for p in plans for s in p.scratch]

    def _each(self, step, ins, outs, sems):
        a = s = 0
        for p in self.plans:
            na, ns = len(p.arrays), len(p.scratch)
            getattr(p, step)(ins[a:a + na], outs[a:a + na], sems[s:s + ns])
            a, s = a + na, s + ns

    def start(self, ins, outs, sems):
        self._each('start', ins, outs, sems)

    def middle(self, ins, outs, sems):
        self._each('middle', ins, outs, sems)

    def finish(self, ins, outs, sems):
        self._each('finish', ins, outs, sems)


def _hosted_call(body, *, grid, in_specs, out_specs, out_shape, scratch_shapes, sem, name, args, comm=None,
                 aliases=None):
    aliases = aliases or {}
    if comm is None:
        outs = pl.pallas_call(body, grid=grid, in_specs=in_specs, out_specs=out_specs, out_shape=out_shape,
                              scratch_shapes=scratch_shapes, name=name, input_output_aliases=aliases,
                              compiler_params=_cp(sem))(*args)
        return outs, None
    n_in, n_out, n_scr = len(in_specs), len(out_specs), len(scratch_shapes)
    nc, ns, nb = len(comm.arrays), len(comm.scratch), len(comm.base)
    total = math.prod(grid)
    mid = min(total - 1, (3 * total) // 4)
    aliases = {**aliases, **{n_in + nc + b: n_out + b for b in range(nb)}}

    def wrapped(*refs):
        pos = 0
        ins = refs[pos:pos + n_in]; pos += n_in
        cins = refs[pos:pos + nc]; pos += nc + nb
        outs = refs[pos:pos + n_out]; pos += n_out
        couts = refs[pos:pos + nc]; pos += nc
        scr = refs[pos:pos + n_scr]; pos += n_scr
        sems = refs[pos:pos + ns]
        step = 0
        for ax, g in enumerate(grid):
            step = step * g + pl.program_id(ax)

        @pl.when(step == 0)
        def _():
            comm.start(cins, couts, sems)

        body(*ins, *outs, *scr)

        @pl.when(step == mid)
        def _():
            comm.middle(cins, couts, sems)

        @pl.when(step == total - 1)
        def _():
            comm.finish(cins, couts, sems)

    res = pl.pallas_call(
        wrapped, grid=grid, in_specs=list(in_specs) + [ANY] * (nc + nb), out_specs=list(out_specs) + [ANY] * nc,
        out_shape=list(out_shape) + list(comm.out_shape), scratch_shapes=list(scratch_shapes) + list(comm.scratch),
        name=name, input_output_aliases=aliases,
        compiler_params=_cp(("arbitrary",) * len(grid)))(*args, *comm.arrays, *comm.base)
    return res[:n_out], res[n_out:]


class _Hook:
    def __init__(self, fn, ins=(), in_specs=(), outs=()):
        self.fn, self.ins, self.in_specs, self.outs = fn, list(ins), list(in_specs), list(outs)


def _matmul(a, b, *, ta=False, tb=False, tm, tn, tk, out_dtype=None, bias=None, res=None, inner='n',
            comm=None, prologue=None, epilogue=None, a_shape=None, sequential=False, name):
    if a is None:
        m, kdim = a_shape
    elif ta:
        kdim, m = a.shape
    else:
        m, kdim = a.shape
    if tb:
        n, k2 = b.shape
    else:
        k2, n = b.shape
    assert kdim == k2, (name, kdim, b.shape)
    tm, tn, tk = min(tm, m), min(tn, n), min(tk, kdim)
    assert m % tm == 0 and n % tn == 0 and kdim % tk == 0, (name, m, n, kdim, tm, tn, tk)
    nk = kdim // tk
    dn = (((0 if ta else 1,), (1 if tb else 0,)), ((), ()))
    hooks = [h for h in (prologue, epilogue) if h is not None]
    n_pro_in = len(prologue.ins) if prologue else 0
    n_epi_in = len(epilogue.ins) if epilogue else 0
    n_pro_out = len(prologue.outs) if prologue else 0
    n_epi_out = len(epilogue.outs) if epilogue else 0
    if inner == 'n':
        grid = (m // tm, n // tn, nk)
        mi = lambda g0, g1: g0
        ni = lambda g0, g1: g1
    else:
        grid = (n // tn, m // tm, nk)
        mi = lambda g0, g1: g1
        ni = lambda g0, g1: g0

    def body(*refs):
        refs = list(refs)
        take = lambda cnt: [refs.pop(0) for _ in range(cnt)]
        a_ref = take(1)[0] if a is not None else None
        b_ref = take(1)[0]
        bias_ref = take(1)[0] if bias is not None else None
        res_ref = take(1)[0] if res is not None else None
        pro_in, epi_in = take(n_pro_in), take(n_epi_in)
        o_ref = take(1)[0] if epilogue is None else None
        pro_out, epi_out = take(n_pro_out), take(n_epi_out)
        i, j, k = mi(pl.program_id(0), pl.program_id(1)), ni(pl.program_id(0), pl.program_id(1)), pl.program_id(2)

        def finish(src):
            def result(rows=slice(None)):
                r = src[rows, :]
                if bias_ref is not None:
                    r = r + bias_ref[...]
                if res_ref is not None:
                    r = r + res_ref[rows, :]
                return r

            if epilogue is None:
                o_ref[...] = result().astype(out_dtype)
            else:
                epilogue.fn(result, epi_in, epi_out, i, j)

        a_val = a_ref[...] if prologue is None else prologue.fn(a_ref, pro_in, pro_out, i, k)
        prod = lax.dot_general(a_val.astype(BF16), b_ref[...].astype(BF16), dn, preferred_element_type=F32)
        if nk == 1:
            finish(prod)
            return
        acc_ref = refs[0]

        @pl.when(k == 0)
        def _():
            acc_ref[...] = prod

        @pl.when(k > 0)
        def _():
            acc_ref[...] += prod

        @pl.when(k == nk - 1)
        def _():
            finish(acc_ref)

    spec = lambda shape, fn: pl.BlockSpec(shape, lambda g0, g1, k: fn(mi(g0, g1), ni(g0, g1), k))
    in_specs, args = [], []
    if a is not None:
        in_specs.append(spec((tk, tm), lambda i, j, k: (k, i)) if ta else spec((tm, tk), lambda i, j, k: (i, k)))
        args.append(a)
    in_specs.append(spec((tn, tk), lambda i, j, k: (j, k)) if tb else spec((tk, tn), lambda i, j, k: (k, j)))
    args.append(b)
    if bias is not None:
        in_specs.append(spec((1, tn), lambda i, j, k: (0, j)))
        args.append(bias)
    if res is not None:
        in_specs.append(spec((tm, tn), lambda i, j, k: (i, j)))
        args.append(res)
    for h in hooks:
        in_specs += [spec(shape, fn) for shape, fn in h.in_specs]
        args += h.ins
    out_specs, out_shape = [], []
    if epilogue is None:
        out_specs.append(spec((tm, tn), lambda i, j, k: (i, j)))
        out_shape.append(jax.ShapeDtypeStruct((m, n), out_dtype))
    for h in hooks:
        out_specs += [spec(blk, fn) for _, _, blk, fn in h.outs]
        out_shape += [jax.ShapeDtypeStruct(shape, dtype) for shape, dtype, _, _ in h.outs]
    outs, couts = _hosted_call(
        body, grid=grid, in_specs=in_specs, out_specs=out_specs, out_shape=out_shape,
        scratch_shapes=[pltpu.VMEM((tm, tn), F32)] if nk > 1 else [],
        sem=("arbitrary",) * 3 if sequential else ("parallel", "parallel", "arbitrary"),
        name=name, args=args, comm=comm)
    outs = outs[0] if not hooks else outs
    return outs if comm is None else (outs, couts)


def _rms_fwd(x, g, *, comm=None, name):
    l, d = x.shape
    tr = min(256, l)

    def body(x_ref, g_ref, h_ref):
        xf = x_ref[...]
        r = lax.rsqrt(jnp.mean(xf * xf, axis=-1, keepdims=True) + RMS_EPS)
        h_ref[...] = ((xf * r) * g_ref[...]).astype(BF16)

    row = pl.BlockSpec((tr, d), lambda i: (i, 0))
    return _hosted_call(
        body, grid=(l // tr,), in_specs=[row, pl.BlockSpec((1, d), lambda i: (0, 0))],
        out_specs=[row], out_shape=[jax.ShapeDtypeStruct((l, d), BF16)], scratch_shapes=[],
        sem=("parallel",), name=name, args=(x, g), comm=comm)


EPI_ROWS = 128
ROW_TILE = 256


def _row_chunks(tm):
    ch = min(EPI_ROWS, tm)
    return [slice(c * ch, (c + 1) * ch) for c in range(tm // ch)]


def _rowwise(hook, src, tm, *, name):
    l, d = src.shape
    n_in = len(hook.ins)

    def body(*refs):
        src_ref, ins, outs = refs[0], refs[1:1 + n_in], refs[1 + n_in:]
        hook.fn(lambda rows=slice(None): src_ref[rows, :], ins, outs, pl.program_id(0), 0)

    spec = lambda shape, fn: pl.BlockSpec(shape, lambda i: fn(i, 0, 0))
    return pl.pallas_call(
        body, grid=(l // tm,),
        in_specs=[pl.BlockSpec((tm, d), lambda i: (i, 0))] + [spec(shape, fn) for shape, fn in hook.in_specs],
        out_specs=[spec(blk, fn) for _, _, blk, fn in hook.outs],
        out_shape=[jax.ShapeDtypeStruct(shape, dtype) for shape, dtype, _, _ in hook.outs],
        name=name, compiler_params=_cp(("arbitrary",)))(src, *hook.ins)


def _rms_bwd_hook(x, g, dres, tm, *, with_bf16):
    def fn(result, ins, outs, i, j):
        x_ref, g_ref, dres_ref = ins
        dg_ref = outs[-1]

        @pl.when(i == 0)
        def _():
            dg_ref[...] = jnp.zeros_like(dg_ref)

        for rows in _row_chunks(tm):
            dyv = result(rows)
            xf = x_ref[rows, :]
            r = lax.rsqrt(jnp.mean(xf * xf, axis=-1, keepdims=True) + RMS_EPS)
            xhat = xf * r
            dxh = dyv * g_ref[...]
            dx = r * (dxh - xhat * jnp.mean(dxh * xhat, axis=-1, keepdims=True)) + dres_ref[rows, :]
            outs[0][rows, :] = dx
            if with_bf16:
                outs[1][rows, :] = dx.astype(BF16)
            dg_ref[...] += jnp.sum(dyv * xhat, axis=0, keepdims=True)

    l, d = x.shape
    row = lambda i, j, k: (i, 0)
    vec = lambda i, j, k: (0, 0)
    outs = [((l, d), F32, (tm, d), row)] + ([((l, d), BF16, (tm, d), row)] if with_bf16 else [])
    return _Hook(fn, ins=[x, g, dres], in_specs=[((tm, d), row), ((1, d), vec), ((tm, d), row)],
                 outs=outs + [((1, d), F32, (1, d), vec)])


def _final_loss_hook(g, target, tm):
    l, d = target.shape

    def fn(result, ins, outs, i, j):
        g_ref, t_ref = ins
        dx_ref, dxb_ref, dg_ref, loss_ref = outs
        gv = g_ref[...]

        @pl.when(i == 0)
        def _():
            dg_ref[...] = jnp.zeros_like(dg_ref)
            loss_ref[...] = jnp.zeros_like(loss_ref)

        for rows in _row_chunks(tm):
            xf = result(rows)
            r = lax.rsqrt(jnp.mean(xf * xf, axis=-1, keepdims=True) + RMS_EPS)
            xhat = xf * r
            diff = xhat * gv - t_ref[rows, :]
            dout = diff * (1.0 / d)
            dxh = dout * gv
            dx = r * (dxh - xhat * jnp.mean(dxh * xhat, axis=-1, keepdims=True))
            dx_ref[rows, :] = dx
            dxb_ref[rows, :] = dx.astype(BF16)
            dg_ref[...] += jnp.sum(dout * xhat, axis=0, keepdims=True)
            part = jnp.sum(jnp.mean(diff * diff, axis=-1, keepdims=True), axis=0, keepdims=True)
            loss_ref[...] += 0.5 * part

    row = lambda i, j, k: (i, 0)
    vec = lambda i, j, k: (0, 0)
    return _Hook(fn, ins=[g, target], in_specs=[((1, d), vec), ((tm, d), row)],
                 outs=[((l, d), F32, (tm, d), row), ((l, d), BF16, (tm, d), row),
                       ((1, d), F32, (1, d), vec), ((1, 1), F32, (1, 1), vec)])


Q_PER_KV = 8
GROUP_ROWS = Q_PER_KV * BLOCK


def _attn_masks(n, rows=GROUP_ROWS):
    q_idx = lax.broadcasted_iota(jnp.int32, (rows, 2 * BLOCK), 0) & (BLOCK - 1)
    s_idx = lax.broadcasted_iota(jnp.int32, (rows, 2 * BLOCK), 1)
    dist = q_idx + BLOCK - s_idx
    valid = (dist >= 0) & (dist < BLOCK) & ((n > 0) | (s_idx >= BLOCK))
    return dist.astype(F32), valid


def _dup_half(t, kv_head, lo):
    rolled = pltpu.roll(t, HEAD_DIM, axis=1)
    return jnp.where(lo, t, rolled) if kv_head == 0 else jnp.where(lo, rolled, t)


def _stack_heads(ref, kv_head, lo):
    pieces = []
    for r in range(Q_PER_KV):
        pair = kv_head * 4 + r // 2
        t = ref[:, pair * 128:(pair + 1) * 128].astype(BF16)
        sel = lo if r % 2 == 0 else jnp.logical_not(lo)
        pieces.append(jnp.where(sel, t, jnp.zeros_like(t)))
    return jnp.concatenate(pieces, axis=0)


def _unstack_heads(t, lo):
    return [jnp.where(lo, t[(2 * i) * BLOCK:(2 * i + 1) * BLOCK], t[(2 * i + 1) * BLOCK:(2 * i + 2) * BLOCK])
            for i in range(Q_PER_KV // 2)]


def _per_head_column(values):
    return jnp.concatenate([jnp.full((BLOCK, 1), v, F32) for v in values], axis=0)


def _group_probs(qm, kdup, dist, valid, sink_ref, kv_head):
    heads = [kv_head * Q_PER_KV + r for r in range(Q_PER_KV)]
    slope = _per_head_column([2.0 ** (-8.0 * (h + 1) / N_Q_HEADS) for h in heads])
    sink = _per_head_column([sink_ref[h] for h in heads])
    return _probs(qm, kdup, dist, valid, sink, slope)


def _probs(qm, kdup, dist, valid, sink, slope):
    s = lax.dot_general(qm, kdup, (((1,), (1,)), ((), ())), preferred_element_type=F32)
    s = s * (HEAD_DIM ** -0.5) - slope * dist
    s = jnp.where(valid, s, NEG_BIG)
    m = jnp.maximum(jnp.max(s, axis=-1, keepdims=True), sink)
    p = jnp.exp(s - m)
    esink = jnp.exp(sink - m)
    inv = 1.0 / (jnp.sum(p, axis=-1, keepdims=True) + esink)
    return p * inv, esink * inv


def _attn_fwd(proj, sinks, *, name):
    l = proj.shape[0]
    nb = l // BLOCK

    def body(sink_ref, q_ref, kc_ref, kp_ref, vc_ref, vp_ref, o_ref):
        n = pl.program_id(0)
        dist, valid = _attn_masks(n, BLOCK)
        lo = lax.broadcasted_iota(jnp.int32, (1, BLOCK), 1) < HEAD_DIM
        kx = jnp.concatenate([kp_ref[...], kc_ref[...]], axis=0).astype(BF16)
        vx = jnp.concatenate([vp_ref[...], vc_ref[...]], axis=0).astype(BF16)
        for kv_head in range(2):
            kdup = _dup_half(kx, kv_head, lo)
            vdup = _dup_half(vx, kv_head, lo)
            for pr in range(4):
                pair = kv_head * 4 + pr
                qp = q_ref[:, pair * 128:(pair + 1) * 128].astype(BF16)
                o_pair = jnp.zeros((BLOCK, 128), F32)
                for half in range(2):
                    head = 2 * pair + half
                    sel = lo if half == 0 else jnp.logical_not(lo)
                    qm = jnp.where(sel, qp, jnp.zeros_like(qp))
                    p, _ = _probs(qm, kdup, dist, valid, sink_ref[head], 2.0 ** (-8.0 * (head + 1) / N_Q_HEADS))
                    o = jnp.dot(p.astype(BF16), vdup, preferred_element_type=F32)
                    o_pair = o_pair + jnp.where(sel, o, 0.0)
                o_ref[:, pair * 128:(pair + 1) * 128] = o_pair.astype(BF16)

    kv = lambda col, prev: pl.BlockSpec(
        (BLOCK, KV_WIDTH), (lambda n: (jnp.maximum(n - 1, 0), col)) if prev else (lambda n: (n, col)))
    return pl.pallas_call(
        body, grid=(nb,),
        in_specs=[pl.BlockSpec(memory_space=pltpu.SMEM),
                  pl.BlockSpec((BLOCK, ATTN_WIDTH), lambda n: (n, 0)),
                  kv(COL_K, False), kv(COL_K, True), kv(COL_V, False), kv(COL_V, True)],
        out_specs=pl.BlockSpec((BLOCK, ATTN_WIDTH), lambda n: (n, 0)),
        out_shape=jax.ShapeDtypeStruct((l, ATTN_WIDTH), BF16), name=name,
        compiler_params=_cp(("parallel",)))(sinks, proj, proj, proj, proj, proj)


def _attn_bwd(proj, sinks, dattn, *, comm=None, name):
    l = proj.shape[0]
    nb = l // BLOCK

    def body(sink_ref, q_ref, kc_ref, kp_ref, vc_ref, vp_ref, do_ref,
             dq_ref, dkc_ref, dkp_ref, dvc_ref, dvp_ref, dsink_ref):
        n = pl.program_id(0)
        dist, valid = _attn_masks(n)
        lane = lax.broadcasted_iota(jnp.int32, (1, BLOCK), 1)
        lo = lane < HEAD_DIM
        kx = jnp.concatenate([kp_ref[...], kc_ref[...]], axis=0).astype(BF16)
        vx = jnp.concatenate([vp_ref[...], vc_ref[...]], axis=0).astype(BF16)
        dsink = jnp.zeros((1, BLOCK), F32)
        dk_heads, dv_heads = [], []
        for kv_head in range(2):
            kdup = _dup_half(kx, kv_head, lo)
            vdup = _dup_half(vx, kv_head, lo)
            qm = _stack_heads(q_ref, kv_head, lo)
            dom = _stack_heads(do_ref, kv_head, lo)
            p, psink = _group_probs(qm, kdup, dist, valid, sink_ref, kv_head)
            dp = lax.dot_general(dom, vdup, (((1,), (1,)), ((), ())), preferred_element_type=F32)
            delta = jnp.sum(p * dp, axis=-1, keepdims=True)
            ds = (p * (dp - delta) * (HEAD_DIM ** -0.5)).astype(BF16)
            dsink_rows = -psink * delta
            for r in range(Q_PER_KV):
                part = jnp.sum(dsink_rows[r * BLOCK:(r + 1) * BLOCK])
                dsink = dsink + jnp.where(lane == kv_head * Q_PER_KV + r, part, 0.0)
            dq = jnp.dot(ds, kdup, preferred_element_type=F32)
            for i, dq_pair in enumerate(_unstack_heads(dq, lo)):
                pair = kv_head * 4 + i
                dq_ref[:, pair * 128:(pair + 1) * 128] = dq_pair.astype(BF16)
            dk_acc = lax.dot_general(ds, qm, (((0,), (0,)), ((), ())), preferred_element_type=F32)
            dv_acc = lax.dot_general(p.astype(BF16), dom, (((0,), (0,)), ((), ())), preferred_element_type=F32)
            dk_heads.append(dk_acc + pltpu.roll(dk_acc, HEAD_DIM, axis=1))
            dv_heads.append(dv_acc + pltpu.roll(dv_acc, HEAD_DIM, axis=1))
        dk = jnp.where(lo, dk_heads[0], dk_heads[1])
        dv = jnp.where(lo, dv_heads[0], dv_heads[1])
        dkp_ref[...] = dk[:BLOCK]
        dkc_ref[...] = dk[BLOCK:]
        dvp_ref[...] = dv[:BLOCK]
        dvc_ref[...] = dv[BLOCK:]

        @pl.when(n == 0)
        def _():
            dsink_ref[...] = jnp.zeros_like(dsink_ref)

        dsink_ref[...] += dsink

    kv = lambda col, prev: pl.BlockSpec(
        (BLOCK, KV_WIDTH), (lambda n: (jnp.maximum(n - 1, 0), col)) if prev else (lambda n: (n, col)))
    qspec = pl.BlockSpec((BLOCK, ATTN_WIDTH), lambda n: (n, 0))
    kvout = pl.BlockSpec((BLOCK, KV_WIDTH), lambda n: (n, 0))
    kvshape = jax.ShapeDtypeStruct((l, KV_WIDTH), F32)
    return _hosted_call(
        body, grid=(nb,),
        in_specs=[pl.BlockSpec(memory_space=pltpu.SMEM), qspec,
                  kv(COL_K, False), kv(COL_K, True), kv(COL_V, False), kv(COL_V, True), qspec],
        out_specs=[qspec, kvout, kvout, kvout, kvout, pl.BlockSpec((1, BLOCK), lambda n: (0, 0))],
        out_shape=[jax.ShapeDtypeStruct((l, ATTN_WIDTH), BF16), kvshape, kvshape, kvshape, kvshape,
                   jax.ShapeDtypeStruct((1, BLOCK), F32)],
        scratch_shapes=[], sem=("arbitrary",), name=name,
        args=(sinks, proj, proj, proj, proj, proj, dattn), comm=comm)


def _kv_grad_merge(dkc, dkp, dvc, dvp, *, name):
    l = dkc.shape[0]
    nb = l // BLOCK

    def body(dkc_ref, dkp_ref, dvc_ref, dvp_ref, o_ref):
        last = pl.program_id(0) == nb - 1
        o_ref[:, :KV_WIDTH] = (dkc_ref[...] + jnp.where(last, 0.0, dkp_ref[...])).astype(BF16)
        o_ref[:, KV_WIDTH:] = (dvc_ref[...] + jnp.where(last, 0.0, dvp_ref[...])).astype(BF16)

    cur = pl.BlockSpec((BLOCK, KV_WIDTH), lambda n: (n, 0))
    nxt = pl.BlockSpec((BLOCK, KV_WIDTH), lambda n: (jnp.minimum(n + 1, nb - 1), 0))
    return pl.pallas_call(
        body, grid=(nb,), in_specs=[cur, nxt, cur, nxt],
        out_specs=pl.BlockSpec((BLOCK, 2 * KV_WIDTH), lambda n: (n, 0)),
        out_shape=jax.ShapeDtypeStruct((l, 2 * KV_WIDTH), BF16), name=name,
        compiler_params=_cp(("parallel",)))(dkc, dkp, dvc, dvp)


def _discretize(a_re, a_im, log_dt, b_re, b_im):
    dt = jnp.exp(log_dt)
    mag = jnp.exp(a_re * dt)
    ab_re = mag * jnp.cos(a_im * dt)
    ab_im = mag * jnp.sin(a_im * dt)
    nr = ab_re - 1.0
    ni = ab_im
    den = a_re * a_re + a_im * a_im
    z_re = (nr * a_re + ni * a_im) / den
    z_im = (ni * a_re - nr * a_im) / den
    bb_re = z_re * b_re - z_im * b_im
    bb_im = z_re * b_im + z_im * b_re
    return ab_re, ab_im, bb_re, bb_im


def _ssm_disc_fwd(a_re, a_im, log_dt, b_re, b_im, *, name):
    def body(ar, ai, ld, br, bi, o_ar, o_ai, o_br, o_bi):
        r = _discretize(ar[...], ai[...], ld[...], br[...], bi[...])
        o_ar[...], o_ai[...], o_br[...], o_bi[...] = r

    col = jax.ShapeDtypeStruct(a_re.shape, F32)
    mat = jax.ShapeDtypeStruct(b_re.shape, F32)
    return pl.pallas_call(body, out_shape=[col, col, mat, mat], name=name)(a_re, a_im, log_dt, b_re, b_im)


def _ssm_disc_bwd(a_re, a_im, log_dt, b_re, b_im, d_ab_re, d_ab_im, d_bb_re, d_bb_im, *, name):
    def body(ar, ai, ld, br, bi, g0, g1, g2, g3, o_ar, o_ai, o_ld, o_br, o_bi):
        _, vjp = jax.vjp(_discretize, ar[...], ai[...], ld[...], br[...], bi[...])
        r = vjp((g0[...], g1[...], g2[...], g3[...]))
        o_ar[...], o_ai[...], o_ld[...], o_br[...], o_bi[...] = r

    col = jax.ShapeDtypeStruct(a_re.shape, F32)
    mat = jax.ShapeDtypeStruct(b_re.shape, F32)
    return pl.pallas_call(body, out_shape=[col, col, col, mat, mat], name=name)(
        a_re, a_im, log_dt, b_re, b_im, d_ab_re, d_ab_im, d_bb_re, d_bb_im)


def _shift_rows(x, d, rows, *, down):
    t = x.shape[0]
    if down:
        return jnp.where(rows >= d, pltpu.roll(x, d, axis=0), 0.0)
    return jnp.where(rows < t - d, pltpu.roll(x, t - d, axis=0), 0.0)


def _scan_chunk(xr, xi, ar, ai, *, down):
    t = xr.shape[0]
    rows = lax.broadcasted_iota(jnp.int32, (t, 1), 0)
    pr, pi = ar, ai
    d = 1
    while d < t:
        sr = _shift_rows(xr, d, rows, down=down)
        si = _shift_rows(xi, d, rows, down=down)
        xr, xi = xr + pr * sr - pi * si, xi + pr * si + pi * sr
        pr, pi = pr * pr - pi * pi, 2.0 * pr * pi
        d *= 2
    return xr, xi


def _ssm_fwd(proj, ab, bd, cd, dskip, *, comm=None, name):
    l = proj.shape[0]
    t = min(SSM_CHUNK, l)
    nc = l // t

    def body(u_ref, ab_ref, bd_ref, cd_ref, ds_ref, y_ref, gy_ref, xs_ref, carry_ref):
        c = pl.program_id(1)

        @pl.when(c == 0)
        def _():
            carry_ref[...] = jnp.zeros_like(carry_ref)

        u = u_ref[...]
        ar, ai = ab_ref[0, 0:1, :], ab_ref[0, 1:2, :]
        bu = jnp.dot(u.astype(BF16), bd_ref[0], preferred_element_type=F32)
        rows = lax.broadcasted_iota(jnp.int32, (t, 1), 0)
        cr, ci = carry_ref[0:1, :], carry_ref[1:2, :]
        xr = bu[:, :SSM_X_BLK] + jnp.where(rows == 0, ar * cr - ai * ci, 0.0)
        xi = bu[:, SSM_X_BLK:] + jnp.where(rows == 0, ar * ci + ai * cr, 0.0)
        xr, xi = _scan_chunk(xr, xi, ar, ai, down=True)
        xs_ref[0, :, :SSM_X_BLK] = xr
        xs_ref[0, :, SSM_X_BLK:] = xi
        carry_ref[0:1, :] = xs_ref[0, t - 1:t, :SSM_X_BLK]
        carry_ref[1:2, :] = xs_ref[0, t - 1:t, SSM_X_BLK:]
        y = jnp.dot(xs_ref[0].astype(BF16), cd_ref[0], preferred_element_type=F32) + ds_ref[...] * u
        y_ref[...] = y
        gy_ref[...] = _gelu(y).astype(BF16)

    blk = lambda shape: pl.BlockSpec((1,) + shape, lambda j, c: (j, 0, 0))
    ycol = pl.BlockSpec((t, SSM_U_BLK), lambda j, c: (c, j))
    return _hosted_call(
        body, grid=(SSM_SPLIT, nc),
        in_specs=[pl.BlockSpec((t, SSM_U_BLK), lambda j, c: (c, COL_U + j)),
                  blk((2, SSM_X_BLK)), blk((SSM_U_BLK, 2 * SSM_X_BLK)), blk((2 * SSM_X_BLK, SSM_U_BLK)),
                  pl.BlockSpec((1, SSM_U_BLK), lambda j, c: (0, j))],
        out_specs=[ycol, ycol, pl.BlockSpec((1, t, 2 * SSM_X_BLK), lambda j, c: (j, c, 0))],
        out_shape=[jax.ShapeDtypeStruct((l, SSM_WIDTH), F32), jax.ShapeDtypeStruct((l, SSM_WIDTH), BF16),
                   jax.ShapeDtypeStruct((SSM_SPLIT, l, 2 * SSM_X_BLK), F32)],
        scratch_shapes=[pltpu.VMEM((2, SSM_X_BLK), F32)], sem=("parallel", "arbitrary"), name=name,
        args=(proj, ab, bd, cd, dskip), comm=comm)


def _ssm_bwd(proj, y, dgy, xs, ab, bdt, cdt, dskip, *, comm=None, name):
    l = proj.shape[0]
    t = min(SSM_CHUNK, l)
    nc = l // t

    def body(u_ref, y_ref, dgy_ref, xs_ref, halo_ref, ab_ref, bdt_ref, cdt_ref, ds_ref,
             du_ref, dbd_ref, dcd_ref, dab_ref, dd_ref, carry_ref):
        c = pl.program_id(1)
        ci_ = nc - 1 - c

        @pl.when(c == 0)
        def _():
            carry_ref[...] = jnp.zeros_like(carry_ref)
            dbd_ref[...] = jnp.zeros_like(dbd_ref)
            dcd_ref[...] = jnp.zeros_like(dcd_ref)
            dab_ref[...] = jnp.zeros_like(dab_ref)
            dd_ref[...] = jnp.zeros_like(dd_ref)

        u = u_ref[...]
        dy = dgy_ref[...] * _gelu_grad(y_ref[...])
        dyb = dy.astype(BF16)
        ar, ai = ab_ref[0, 0:1, :], ab_ref[0, 1:2, :]
        g = jnp.dot(dyb, cdt_ref[0], preferred_element_type=F32)
        rows = lax.broadcasted_iota(jnp.int32, (t, 1), 0)
        cr, ci = carry_ref[0:1, :], carry_ref[1:2, :]
        lr = g[:, :SSM_X_BLK] + jnp.where(rows == t - 1, ar * cr + ai * ci, 0.0)
        li = g[:, SSM_X_BLK:] + jnp.where(rows == t - 1, ar * ci - ai * cr, 0.0)
        lr, li = _scan_chunk(lr, li, ar, -ai, down=False)
        lam = jnp.concatenate([lr, li], axis=1)
        carry_ref[0:1, :] = lr[0:1, :]
        carry_ref[1:2, :] = li[0:1, :]
        lamb = lam.astype(BF16)
        du_ref[...] = (jnp.dot(lamb, bdt_ref[0], preferred_element_type=F32) + ds_ref[...] * dy).astype(BF16)
        dbd_ref[0] += lax.dot_general(u.astype(BF16), lamb, (((0,), (0,)), ((), ())),
                                      preferred_element_type=F32)
        xs = xs_ref[0]
        dcd_ref[0] += lax.dot_general(xs.astype(BF16), dyb, (((0,), (0,)), ((), ())),
                                      preferred_element_type=F32)
        halo = jnp.where(ci_ > 0, halo_ref[0, 7:8, :], 0.0)
        xprev = jnp.where(rows == 0, halo, pltpu.roll(xs, 1, axis=0))
        xpr, xpi = xprev[:, :SSM_X_BLK], xprev[:, SSM_X_BLK:]
        dab_ref[0, 0:1, :] += jnp.sum(lr * xpr + li * xpi, axis=0, keepdims=True)
        dab_ref[0, 1:2, :] += jnp.sum(li * xpr - lr * xpi, axis=0, keepdims=True)
        dd_ref[...] += jnp.sum(dy * u, axis=0, keepdims=True)

    blk = lambda shape: pl.BlockSpec((1,) + shape, lambda j, c: (j, 0, 0))
    rev = lambda j, c: (nc - 1 - c, j)
    ycol = pl.BlockSpec((t, SSM_U_BLK), rev)
    hb = t // 8
    return _hosted_call(
        body, grid=(SSM_SPLIT, nc), comm=comm, sem=("parallel", "arbitrary"), name=name,
        args=(proj, y, dgy, xs, xs, ab, bdt, cdt, dskip), scratch_shapes=[pltpu.VMEM((2, SSM_X_BLK), F32)],
        in_specs=[pl.BlockSpec((t, SSM_U_BLK), lambda j, c: (nc - 1 - c, COL_U + j)), ycol, ycol,
                  pl.BlockSpec((1, t, 2 * SSM_X_BLK), lambda j, c: (j, nc - 1 - c, 0)),
                  pl.BlockSpec((1, 8, 2 * SSM_X_BLK),
                               lambda j, c: (j, jnp.maximum((nc - 1 - c) * hb - 1, 0), 0)),
                  blk((2, SSM_X_BLK)), blk((2 * SSM_X_BLK, SSM_U_BLK)), blk((SSM_U_BLK, 2 * SSM_X_BLK)),
                  pl.BlockSpec((1, SSM_U_BLK), lambda j, c: (0, j))],
        out_specs=[ycol, blk((SSM_U_BLK, 2 * SSM_X_BLK)), blk((2 * SSM_X_BLK, SSM_U_BLK)),
                   blk((2, SSM_X_BLK)), pl.BlockSpec((1, SSM_U_BLK), lambda j, c: (0, j))],
        out_shape=[jax.ShapeDtypeStruct((l, SSM_WIDTH), BF16),
                   jax.ShapeDtypeStruct((SSM_SPLIT, SSM_U_BLK, 2 * SSM_X_BLK), F32),
                   jax.ShapeDtypeStruct((SSM_SPLIT, 2 * SSM_X_BLK, SSM_U_BLK), F32),
                   jax.ShapeDtypeStruct((SSM_SPLIT, 2, SSM_X_BLK), F32),
                   jax.ShapeDtypeStruct((1, SSM_WIDTH), F32)])


def _block_diag(t):
    s, g, a, b = t.shape
    return jnp.einsum('sgab,gk->sgakb', t, jnp.eye(g, dtype=t.dtype)).reshape(s, g * a, g * b)


def _block_diag_take(t, a, b):
    s = t.shape[0]
    return jnp.einsum('sgakb,gk->sgab', t.reshape(s, 8, a, 8, b), jnp.eye(8, dtype=t.dtype))


def _glu_fwd_hook(l, tm):
    def fn(result, ins, outs, i, j):
        z = result()
        outs[0][...] = z
        outs[1][...] = (z[:, :SSM_WIDTH] * _sigmoid(z[:, SSM_WIDTH:])).astype(BF16)

    row = lambda i, j, k: (i, 0)
    return _Hook(fn, outs=[((l, 2 * SSM_WIDTH), F32, (tm, 2 * SSM_WIDTH), row),
                           ((l, SSM_WIDTH), BF16, (tm, SSM_WIDTH), row)])


def _glu_bwd_hook(z, tm):
    l = z.shape[0]

    def fn(result, ins, outs, i, j):
        zv_ref, zg_ref = ins
        dz_ref, db_ref = outs
        d = result()
        sg = _sigmoid(zg_ref[...])
        dv = d * sg
        dg = d * zv_ref[...] * sg * (1.0 - sg)
        dz_ref[:, :SSM_WIDTH] = dv.astype(BF16)
        dz_ref[:, SSM_WIDTH:] = dg.astype(BF16)

        @pl.when(i == 0)
        def _():
            db_ref[...] = jnp.zeros_like(db_ref)

        db_ref[:, :SSM_WIDTH] += jnp.sum(dv, axis=0, keepdims=True)
        db_ref[:, SSM_WIDTH:] += jnp.sum(dg, axis=0, keepdims=True)

    half = (tm, SSM_WIDTH)
    return _Hook(fn, ins=[z, z], in_specs=[(half, lambda i, j, k: (i, 0)), (half, lambda i, j, k: (i, 1))],
                 outs=[((l, 2 * SSM_WIDTH), BF16, (tm, 2 * SSM_WIDTH), lambda i, j, k: (i, 0)),
                       ((1, 2 * SSM_WIDTH), F32, (1, 2 * SSM_WIDTH), lambda i, j, k: (0, 0))])


GATE_TC = 256


def _merge_fwd(proj, a, s, *, name):
    l = a.shape[0]
    tr = min(2048, l)

    def body(ga_ref, gs_ref, a_ref, s_ref, o_ref):
        o_ref[...] = (_sigmoid(ga_ref[...]) * a_ref[...] + _sigmoid(gs_ref[...]) * s_ref[...]).astype(BF16)

    own = pl.BlockSpec((tr, GATE_TC), lambda i, j: (i, j))
    return pl.pallas_call(
        body, grid=(l // tr, D_MODEL // GATE_TC),
        in_specs=[pl.BlockSpec((tr, GATE_TC), lambda i, j: (i, COL_GA // 2 + j)),
                  pl.BlockSpec((tr, GATE_TC), lambda i, j: (i, COL_GS // 2 + j)), own, own],
        out_specs=own, out_shape=jax.ShapeDtypeStruct((l, D_MODEL), BF16), name=name,
        compiler_params=_cp(("parallel", "parallel")))(proj, proj, a, s)


def _merge_bwd_hook(proj, a, s, tm):
    def fn(result, ins, outs, i, j):
        ga_ref, gs_ref, a_br, s_br = ins
        d = result()
        sa = _sigmoid(ga_ref[...])
        ss = _sigmoid(gs_ref[...])
        outs[0][...] = (d * sa).astype(BF16)
        outs[1][...] = (d * ss).astype(BF16)
        outs[2][...] = (d * a_br[...] * sa * (1.0 - sa)).astype(BF16)
        outs[3][...] = (d * s_br[...] * ss * (1.0 - ss)).astype(BF16)

    blk = (tm, GATE_TC)
    own = lambda i, j, k: (i, j)
    return _Hook(fn, ins=[proj, proj, a, s],
                 in_specs=[(blk, lambda i, j, k: (i, COL_GA // 2 + j)), (blk, lambda i, j, k: (i, COL_GS // 2 + j)),
                           (blk, own), (blk, own)],
                 outs=[(a.shape, BF16, blk, own)] * 4)


FF_TC = D_FF // 2
FF_NJ = 2
FF_ROWS = 128


FF_HALO = 16


def _conv_taps(ext, rows):
    h = FF_HALO
    return (ext[h:h + rows], pltpu.roll(ext, 1, axis=0)[h:h + rows], pltpu.roll(ext, 2, axis=0)[h:h + rows])


def _ff_specs(tr, l):
    hb = tr // FF_HALO
    last = l // FF_HALO - 1
    prev = lambda i: jnp.maximum(i * hb - 1, 0)
    nxt = lambda i: jnp.minimum((i + 1) * hb, last)
    return dict(
        own=pl.BlockSpec((tr, FF_TC), lambda j, i: (i, j)),
        own_next=pl.BlockSpec((FF_HALO, FF_TC), lambda j, i: (nxt(i), j)),
        val=pl.BlockSpec((tr, FF_TC), lambda j, i: (i, 2 * j)),
        val_next=pl.BlockSpec((FF_HALO, FF_TC), lambda j, i: (nxt(i), 2 * j)),
        gate=pl.BlockSpec((tr, FF_TC), lambda j, i: (i, 2 * j + 1)),
        gate_prev=pl.BlockSpec((FF_HALO, FF_TC), lambda j, i: (prev(i), 2 * j + 1)),
        gate_next=pl.BlockSpec((FF_HALO, FF_TC), lambda j, i: (nxt(i), 2 * j + 1)),
        pair=pl.BlockSpec((tr, 2 * FF_TC), lambda j, i: (i, j)),
        w=pl.BlockSpec((3, FF_TC), lambda j, i: (0, j)),
        b=pl.BlockSpec((1, FF_TC), lambda j, i: (0, j)))


def _ffn_act_fwd(up, conv_w, conv_b, *, name):
    l = up.shape[0]
    tr = min(FF_ROWS, l)

    def body(v_ref, g_ref, prev_ref, w_ref, b_ref, o_ref):
        prev = jnp.where(pl.program_id(1) == 0, 0.0, prev_ref[...].astype(F32))
        g0, g1, g2 = _conv_taps(jnp.concatenate([prev, g_ref[...].astype(F32)], axis=0), tr)
        gc = b_ref[...] + w_ref[0:1, :] * g2 + w_ref[1:2, :] * g1 + w_ref[2:3, :] * g0
        o_ref[...] = (v_ref[...].astype(F32) * _gelu(gc)).astype(BF16)

    sp = _ff_specs(tr, l)
    return pl.pallas_call(
        body, grid=(FF_NJ, l // tr), in_specs=[sp['val'], sp['gate'], sp['gate_prev'], sp['w'], sp['b']],
        out_specs=sp['own'], out_shape=jax.ShapeDtypeStruct((l, D_FF), BF16), name=name,
        compiler_params=_cp(("parallel", "parallel")))(up, up, up, conv_w, conv_b)


def _ffn_act_bwd(dact, up, conv_w, conv_b, *, comm=None, name):
    l = up.shape[0]
    tr = min(FF_ROWS, l)
    ni = l // tr
    te = tr + 8

    def body(d_ref, dn_ref, v_ref, vn_ref, g_ref, gp_ref, gn_ref, w_ref, b_ref, dup_ref, dw_ref, db_ref):
        i = pl.program_id(1)
        f32 = lambda ref, rows=None: ref[...].astype(F32)[:rows]
        prev = jnp.where(i == 0, 0.0, f32(gp_ref))
        g0, g1, g2 = _conv_taps(jnp.concatenate([prev, f32(g_ref), f32(gn_ref, 8)], axis=0), te)
        w0, w1, w2 = w_ref[0:1, :], w_ref[1:2, :], w_ref[2:3, :]
        gc = b_ref[...] + w0 * g2 + w1 * g1 + w2 * g0
        d_own = f32(d_ref)
        d = jnp.concatenate([d_own, jnp.where(i == ni - 1, 0.0, f32(dn_ref, 8))], axis=0)
        v = jnp.concatenate([f32(v_ref), f32(vn_ref, 8)], axis=0)
        dgc = d * v * _gelu_grad(gc)
        ahead1 = pltpu.roll(dgc, te - 1, axis=0)[:tr]
        ahead2 = pltpu.roll(dgc, te - 2, axis=0)[:tr]
        own = dgc[:tr]
        dup_ref[:, :FF_TC] = (d_own * _gelu(gc[:tr])).astype(BF16)
        dup_ref[:, FF_TC:] = (w2 * own + w1 * ahead1 + w0 * ahead2).astype(BF16)

        @pl.when(i == 0)
        def _():
            dw_ref[...] = jnp.zeros_like(dw_ref)
            db_ref[...] = jnp.zeros_like(db_ref)

        dw_ref[0:1, :] += jnp.sum(own * g2[:tr], axis=0, keepdims=True)
        dw_ref[1:2, :] += jnp.sum(own * g1[:tr], axis=0, keepdims=True)
        dw_ref[2:3, :] += jnp.sum(own * g0[:tr], axis=0, keepdims=True)
        db_ref[...] += jnp.sum(own, axis=0, keepdims=True)

    sp = _ff_specs(tr, l)
    return _hosted_call(
        body, grid=(FF_NJ, ni),
        in_specs=[sp['own'], sp['own_next'], sp['val'], sp['val_next'], sp['gate'], sp['gate_prev'],
                  sp['gate_next'], sp['w'], sp['b']],
        out_specs=[sp['pair'], sp['w'], sp['b']],
        out_shape=[jax.ShapeDtypeStruct((l, 2 * D_FF), BF16), jax.ShapeDtypeStruct((3, D_FF), F32),
                   jax.ShapeDtypeStruct((1, D_FF), F32)],
        scratch_shapes=[], sem=("parallel", "arbitrary"), name=name,
        args=(dact, dact, up, up, up, up, up, conv_w, conv_b), comm=comm)


def _col_sum(a, *, name):
    l, n = a.shape
    tr = min(512, l)

    def body(a_ref, o_ref):
        @pl.when(pl.program_id(0) == 0)
        def _():
            o_ref[...] = jnp.zeros_like(o_ref)

        o_ref[...] += jnp.sum(a_ref[...].astype(F32), axis=0, keepdims=True)

    return pl.pallas_call(
        body, grid=(l // tr,), in_specs=[pl.BlockSpec((tr, n), lambda i: (i, 0))],
        out_specs=pl.BlockSpec((1, n), lambda i: (0, 0)), out_shape=jax.ShapeDtypeStruct((1, n), F32),
        name=name, compiler_params=_cp(("arbitrary",)))(a)


def _local_step(x, target, wts, small, shards=None):
    l = x.shape[0]
    wts = dict(wts)
    grads, recvs, sgr = {}, {}, {}
    lay = lambda keys: [LAYOUT[k] for k in keys]
    none = lambda keys: None
    gather = (lambda keys: _GatherPlan([shards[k] for k in keys], lay(keys))) if shards is not None else none
    scatter = (lambda keys: _ScatterPlan([grads[k] for k in keys], lay(keys))) if shards is not None else none

    mm = _matmul

    def take(res, plan, keys, store):
        outs, couts = res
        if plan is not None:
            store.update(zip(keys, couts))
        return outs

    def mm_plan(plan, keys, store, *args, **kw):
        if plan is None:
            return _matmul(*args, **kw)
        return take(_matmul(*args, comm=plan, **kw), plan, keys, store)

    def mm_host(keys, make_plan, store, *args, **kw):
        return mm_plan(make_plan(keys), keys, store, *args, **kw)

    dist = shards is not None
    up_gather = lambda p, base: _GatherPlan([shards['w_up']], lay(['w_up']), part=(p, 2), base=base) if dist else None
    up_scatter = lambda p: _ScatterPlan([grads['w_up']], lay(['w_up']), part=(p, 2)) if dist else None

    col = lambda t: t.reshape(SSM_GROUPS * SSM_STATE, 1)
    a_re, a_im = col(small['ssm_a_re']), col(small['ssm_a_im'])
    log_dt = jnp.repeat(small['ssm_log_dt'].reshape(SSM_GROUPS), SSM_STATE).reshape(-1, 1)
    b_re = small['ssm_b_re'].reshape(SSM_GROUPS * SSM_STATE, SSM_GROUP)
    b_im = small['ssm_b_im'].reshape(SSM_GROUPS * SSM_STATE, SSM_GROUP)
    ab_re, ab_im, bb_re, bb_im = _ssm_disc_fwd(a_re, a_im, log_dt, b_re, b_im, name="ssm_disc_fwd")
    ab = jnp.stack([ab_re.reshape(SSM_SPLIT, SSM_X_BLK), ab_im.reshape(SSM_SPLIT, SSM_X_BLK)], axis=1)
    to_bd = lambda t: _block_diag(t.reshape(SSM_SPLIT, 8, SSM_STATE, SSM_GROUP).transpose(0, 1, 3, 2))
    bd = jnp.concatenate([to_bd(bb_re), to_bd(bb_im)], axis=2)
    c_re = small['ssm_c_re'].reshape(SSM_SPLIT, 8, SSM_GROUP, SSM_STATE)
    c_im = small['ssm_c_im'].reshape(SSM_SPLIT, 8, SSM_GROUP, SSM_STATE)
    cdt = jnp.concatenate([_block_diag(c_re), -_block_diag(c_im)], axis=2)
    bd_b, cdt_b = bd.astype(BF16), cdt.astype(BF16)
    bdt_b, cd_b = bd_b.transpose(0, 2, 1), cdt_b.transpose(0, 2, 1)
    dskip = small['ssm_d'].reshape(1, SSM_WIDTH)

    sinks = small['attn_sinks'].reshape(N_Q_HEADS)
    plan = gather(['w_in_t'])
    h1, = take(_rms_fwd(x, small['attn_norm_g'], comm=plan, name="rms1_fwd"), plan, ['w_in_t'], wts)
    keys = ['w_glu', 'w_ba', 'w_bs', 'w_out']
    plan = _Plans([gather(keys), up_gather(0, None)]) if dist else None
    proj = mm_plan(plan, keys + ['w_up'], wts,
                   h1, wts['w_in_t'], tb=True, tm=512, tn=2944, tk=2048, inner='m', out_dtype=F32,
                   bias=small['b_in'], name="mm_in")
    if dist:
        wts.update({k: _blocks_to_columns(wts[k]) for k in REGROUPED})
    attn = _attn_fwd(proj, sinks, name="attn_fwd")
    plan = up_gather(1, [wts['w_up']])
    y, gy, xs = take(_ssm_fwd(proj, ab, bd_b, cd_b, dskip, comm=plan, name="ssm_fwd"), plan, ['w_up'], wts)
    z, ssm = mm(gy, wts['w_glu'], tm=1024, tn=1024, tk=512, bias=small['b_glu'],
                epilogue=_glu_fwd_hook(l, min(1024, l)), name="mm_glu")
    a_br = mm(attn, wts['w_ba'], tm=1024, tn=1024, tk=1024, out_dtype=F32, name="mm_ba")
    s_br = mm(ssm, wts['w_bs'], tm=1024, tn=1024, tk=512, out_dtype=F32, name="mm_bs")
    tr = min(ROW_TILE, l)
    merged = _merge_fwd(proj, a_br, s_br, name="merge_fwd")
    x2 = mm(merged, wts['w_out'], tm=1024, tn=1024, tk=2048, inner='m', out_dtype=F32, res=x, name="mm_out")
    h2, = take(_rms_fwd(x2, small['ffn_norm_g'], name="rms2_fwd"), None, [], wts)
    up = mm_host(['w_down'], gather, wts,
                 h2, wts['w_up'], tm=1024, tn=1024, tk=2048, out_dtype=BF16, name="mm_up")
    conv_w, conv_b = small['conv_w'], small['conv_b']
    act = _ffn_act_fwd(up, conv_w, conv_b, name="ffn_act_fwd")
    x3 = mm(act, wts['w_down'], tm=1024, tn=1024, tk=2816, out_dtype=F32, res=x2, name="mm_down")
    dx3, dx3b, d_g3, loss = _rowwise(
        _final_loss_hook(small['final_norm_g'].reshape(1, D_MODEL), target, tr), x3, tr, name="final_loss")

    sgr['final_norm_g'] = d_g3.reshape(D_MODEL)
    dact = mm(dx3b, wts['w_down'], tb=True, tm=512, tn=2816, tk=2048, inner='m', out_dtype=BF16, name="mm_dact")
    grads['w_down'] = mm(act, dx3b, ta=True, tm=512, tn=1024, tk=2048, out_dtype=BF16, name="mm_dw_down")
    plan = scatter(['w_down'])
    dup, sgr['conv_w'], sgr['conv_b'] = take(
        _ffn_act_bwd(dact, up, conv_w, conv_b, comm=plan, name="ffn_act_bwd"), plan, ['w_down'], recvs)
    grads['w_up'] = mm(h2, dup, ta=True, tm=1024, tn=1024, tk=2048, out_dtype=BF16, name="mm_dw_up")
    dh2 = mm_plan(up_scatter(0), ['w_up#0'], recvs,
                  dup, wts['w_up'], tb=True, tm=1024, tn=1024, tk=2816, out_dtype=F32, name="mm_dh2")
    dx2, dx2b, sgr['ffn_norm_g'] = _rowwise(
        _rms_bwd_hook(x2, small['ffn_norm_g'], dx3, tr, with_bf16=True), dh2, tr, name="rms2_bwd")

    d_a, d_s, dga, dgs = mm(dx2b, wts['w_out'], tb=True, tm=1024, tn=GATE_TC, tk=2048,
                            epilogue=_merge_bwd_hook(proj, a_br, s_br, min(1024, l)), name="mm_dmerged")
    grads['w_out'] = mm(merged, dx2b, ta=True, tm=1024, tn=1024, tk=2048, out_dtype=BF16, name="mm_dw_out")
    dattn = mm(d_a, wts['w_ba'], tb=True, tm=1024, tn=1024, tk=2048, inner='m', out_dtype=BF16, name="mm_dattn")
    regroup = _columns_to_blocks if dist else (lambda t: t)
    grads['w_ba'] = regroup(mm(attn, d_a, ta=True, tm=1024, tn=1024, tk=2048, out_dtype=BF16, name="mm_dw_ba"))
    dz, sgr['b_glu'] = mm(d_s, wts['w_bs'], tb=True, tm=1024, tn=512, tk=2048, sequential=True,
                          epilogue=_glu_bwd_hook(z, min(1024, l)), name="mm_dssm")
    grads['w_bs'] = regroup(mm(ssm, d_s, ta=True, tm=512, tn=1024, tk=2048, out_dtype=BF16, name="mm_dw_bs"))
    grads['w_glu'] = regroup(mm(gy, dz, ta=True, tm=512, tn=1024, tk=2048, out_dtype=BF16, name="mm_dw_glu"))
    dgy = mm(dz, wts['w_glu'], tb=True, tm=1024, tn=512, tk=1024, inner='m', out_dtype=F32, name="mm_dgy")
    plan = up_scatter(1)
    du, d_bd, d_cd, d_ab, sgr['ssm_d'] = take(
        _ssm_bwd(proj, y, dgy, xs, ab, bdt_b, cdt_b, dskip, comm=plan, name="ssm_bwd"), plan, ['w_up#1'], recvs)
    keys = ['w_out', 'w_ba', 'w_bs', 'w_glu']
    plan = scatter(keys)
    dq, dkc, dkp, dvc, dvp, dsink = take(
        _attn_bwd(proj, sinks, dattn, comm=plan, name="attn_bwd"), plan, keys, recvs)
    dkv = _kv_grad_merge(dkc, dkp, dvc, dvp, name="kv_grad_merge")
    sgr['attn_sinks'] = dsink[:, :N_Q_HEADS]
    dproj = jnp.concatenate([dq, dkv, du, dga, dgs], axis=1)
    sgr['b_in'] = _col_sum(dproj, name="col_sum_dproj")
    grads['w_in_t'] = mm(dproj, h1, ta=True, tm=2944, tn=1024, tk=1024, out_dtype=BF16, name="mm_dw_in")
    dh1 = mm_host(['w_in_t'], scatter, recvs,
                  dproj, wts['w_in_t'], tm=1024, tn=1024, tk=2944, out_dtype=F32, name="mm_dh1")
    grad_x, sgr['attn_norm_g'] = _rowwise(
        _rms_bwd_hook(x, small['attn_norm_g'], dx2, tr, with_bf16=False), dh1, tr, name="rms1_bwd")

    from_bd = lambda t: _block_diag_take(t, SSM_GROUP, SSM_STATE).transpose(0, 1, 3, 2).reshape(
        SSM_GROUPS * SSM_STATE, SSM_GROUP)
    d_bb_re = from_bd(d_bd[:, :, :SSM_X_BLK])
    d_bb_im = from_bd(d_bd[:, :, SSM_X_BLK:])
    d_cdt = d_cd.transpose(0, 2, 1)
    shape_c = (1, SSM_GROUPS, SSM_GROUP, SSM_STATE)
    sgr['ssm_c_re'] = _block_diag_take(d_cdt[:, :, :SSM_X_BLK], SSM_GROUP, SSM_STATE).reshape(shape_c)
    sgr['ssm_c_im'] = -_block_diag_take(d_cdt[:, :, SSM_X_BLK:], SSM_GROUP, SSM_STATE).reshape(shape_c)
    d_a_re, d_a_im, d_ldt, d_b_re, d_b_im = _ssm_disc_bwd(
        a_re, a_im, log_dt, b_re, b_im, d_ab[:, 0, :].reshape(-1, 1), d_ab[:, 1, :].reshape(-1, 1),
        d_bb_re, d_bb_im, name="ssm_disc_bwd")
    sgr['ssm_a_re'] = d_a_re.reshape(1, SSM_GROUPS, SSM_STATE)
    sgr['ssm_a_im'] = d_a_im.reshape(1, SSM_GROUPS, SSM_STATE)
    sgr['ssm_log_dt'] = d_ldt.reshape(SSM_GROUPS, SSM_STATE).sum(axis=1).reshape(1, SSM_GROUPS)
    sgr['ssm_b_re'] = d_b_re.reshape(1, SSM_GROUPS, SSM_STATE, SSM_GROUP)
    sgr['ssm_b_im'] = d_b_im.reshape(1, SSM_GROUPS, SSM_STATE, SSM_GROUP)
    return loss, grad_x, grads, recvs, sgr


def _swap_cores(arrs, *, name):
    n = len(arrs)

    def body(*refs):
        ins, outs = refs[:n], refs[n:2 * n]
        send_sems, recv_sems = refs[2 * n:]
        x, y, c = _place()
        copies = []
        for i in range(n):
            cp = pltpu.make_async_remote_copy(
                src_ref=ins[i], dst_ref=outs[i], send_sem=send_sems.at[i], recv_sem=recv_sems.at[i],
                device_id=(x, y, 1 - c), device_id_type=MESH)
            cp.start()
            copies.append(cp)
        for cp in copies:
            cp.wait()

    return pl.pallas_call(
        body, in_specs=[ANY] * n, out_specs=[ANY] * n,
        out_shape=[jax.ShapeDtypeStruct(a.shape, a.dtype) for a in arrs],
        scratch_shapes=[pltpu.SemaphoreType.DMA((n,)), pltpu.SemaphoreType.DMA((n,))],
        name=name)(*arrs)


def _all_reduce_small(buf, *, name):
    r = buf.shape[0]

    def body(in_ref, out_ref, slots, send_sems, recv_sems):
        x, y, c = _place()
        me = 4 * x + 2 * y + c
        slots[pl.ds(me, 1)] = in_ref[...][None]
        copies = []
        for k in range(N_DEV - 1):
            bx, by, bc = ((k + 1) >> 2) & 1, ((k + 1) >> 1) & 1, (k + 1) & 1
            peer = (1 - x if bx else x, 1 - y if by else y, 1 - c if bc else c)
            cp = pltpu.make_async_remote_copy(
                src_ref=in_ref, dst_ref=slots.at[me], send_sem=send_sems.at[k], recv_sem=recv_sems.at[k],
                device_id=peer, device_id_type=MESH)
            cp.start()
            copies.append(cp)
        for cp in copies:
            cp.wait()
        acc = slots[0]
        for d in range(1, N_DEV):
            acc = acc + slots[d]
        out_ref[...] = acc

    vm = pl.BlockSpec(memory_space=pltpu.VMEM)
    return pl.pallas_call(
        body, in_specs=[vm], out_specs=vm, out_shape=jax.ShapeDtypeStruct((r, 128), F32),
        scratch_shapes=[pltpu.VMEM((N_DEV, r, 128), F32), pltpu.SemaphoreType.DMA((N_DEV - 1,)),
                        pltpu.SemaphoreType.DMA((N_DEV - 1,))],
        name=name)(buf)


def _pack(arrs):
    flat = jnp.concatenate([a.reshape(-1).astype(F32) for a in arrs])
    pad = (-flat.shape[0]) % 1024
    return jnp.pad(flat, (0, pad)).reshape(-1, 128)


def _unpack(buf, shapes):
    flat = buf.reshape(-1)
    out, pos = [], 0
    for s in shapes:
        size = math.prod(s)
        out.append(flat[pos:pos + size].reshape(s))
        pos += size
    return out


TILE_ELEMS = 256 * 1024


def _tile_rows(r, c):
    if r * c <= TILE_ELEMS:
        return r
    for tr in range(TILE_ELEMS // c // 16 * 16, 0, -16):
        if r % tr == 0:
            return tr
    raise ValueError((r, c))


def _sum4(own, recvs, *, name):
    r, c = own.shape
    parts = len(recvs)
    tr = _tile_rows(r // parts, c)
    per = r // parts // tr

    def body(o_ref, *refs):
        out_ref = refs[parts]
        for p in range(parts):
            @pl.when(pl.program_id(0) // per == p)
            def _():
                acc = o_ref[...].astype(F32)
                for k in range(3):
                    acc = acc + refs[p][k].astype(F32)
                out_ref[...] = acc

    part_spec = lambda p: pl.BlockSpec((3, tr, c), lambda i: (0, jnp.clip(i - p * per, 0, per - 1), 0))
    return pl.pallas_call(
        body, grid=(r // tr,),
        in_specs=[pl.BlockSpec((tr, c), lambda i: (i, 0))] + [part_spec(p) for p in range(parts)],
        out_specs=pl.BlockSpec((tr, c), lambda i: (i, 0)), out_shape=jax.ShapeDtypeStruct((r, c), F32),
        name=name, compiler_params=_cp(("parallel",)))(own, *recvs)


def _adamw(w, ga, gb, m, v, *, name):
    r, c = w.shape
    tr = _tile_rows(r, c)
    bc1 = 1.0 - ADAM_B1 ** ADAM_STEP
    bc2 = 1.0 - ADAM_B2 ** ADAM_STEP
    two = gb is not None

    def body(*refs):
        w_ref, ga_ref = refs[0], refs[1]
        pos = 2
        g = ga_ref[...]
        if two:
            g = g + refs[pos][...]
            pos += 1
        m_ref, v_ref, g_out, d_out, m_out, v_out = refs[pos:pos + 6]
        mn = ADAM_B1 * m_ref[...] + (1.0 - ADAM_B1) * g
        vn = ADAM_B2 * v_ref[...] + (1.0 - ADAM_B2) * (g * g)
        m_hat = mn / bc1
        v_hat = vn / bc2
        g_out[...] = g
        d_out[...] = -ADAM_LR * (m_hat / (jnp.sqrt(v_hat) + ADAM_EPS) + ADAM_WD * w_ref[...])
        m_out[...] = mn
        v_out[...] = vn

    spec = pl.BlockSpec((tr, c), lambda i: (i, 0))
    args = [w, ga] + ([gb] if two else []) + [m, v]
    shp = jax.ShapeDtypeStruct((r, c), F32)
    return pl.pallas_call(
        body, grid=(r // tr,), in_specs=[spec] * len(args), out_specs=[spec] * 4,
        out_shape=[shp] * 4, name=name, compiler_params=_cp(("parallel",)))(*args)


BIG = ['w_in', 'w_glu', 'w_branch_attn', 'w_branch_ssm', 'w_out', 'w_up', 'w_down']
BIG_KEY = {'w_in': 'w_in_t', 'w_glu': 'w_glu', 'w_branch_attn': 'w_ba', 'w_branch_ssm': 'w_bs',
           'w_out': 'w_out', 'w_up': 'w_up', 'w_down': 'w_down'}
TRANSPOSED = {'w_in'}
SMALL = ['attn_norm_g', 'b_in', 'attn_sinks', 'ssm_a_re', 'ssm_a_im', 'ssm_log_dt', 'ssm_b_re', 'ssm_b_im',
         'ssm_c_re', 'ssm_c_im', 'ssm_d', 'b_glu', 'ffn_norm_g', 'conv_b', 'final_norm_g']
WEIGHTS = ['attn_norm_g', 'w_in', 'b_in', 'attn_sinks', 'ssm_a_re', 'ssm_a_im', 'ssm_log_dt', 'ssm_b_re',
           'ssm_b_im', 'ssm_c_re', 'ssm_c_im', 'ssm_d', 'w_glu', 'b_glu', 'w_branch_attn', 'w_branch_ssm',
           'w_out', 'ffn_norm_g', 'w_up', 'conv_w', 'conv_b', 'w_down', 'final_norm_g']


def _shard_2d(name, t):
    t = t[0]
    return t.T if name in TRANSPOSED else t


def _unshard_2d(name, t):
    return (t.T if name in TRANSPOSED else t)[None]


def kernel(x, attn_norm_g, w_in, b_in, attn_sinks, ssm_a_re, ssm_a_im, ssm_log_dt, ssm_b_re, ssm_b_im, ssm_c_re, ssm_c_im, ssm_d, w_glu, b_glu, w_branch_attn, w_branch_ssm, w_out, ffn_norm_g, w_up, conv_w, conv_b, w_down, final_norm_g, loss_target, m_attn_norm_g, m_w_in, m_b_in, m_attn_sinks, m_ssm_a_re, m_ssm_a_im, m_ssm_log_dt, m_ssm_b_re, m_ssm_b_im, m_ssm_c_re, m_ssm_c_im, m_ssm_d, m_w_glu, m_b_glu, m_w_branch_attn, m_w_branch_ssm, m_w_out, m_ffn_norm_g, m_w_up, m_conv_w, m_conv_b, m_w_down, m_final_norm_g, v_attn_norm_g, v_w_in, v_b_in, v_attn_sinks, v_ssm_a_re, v_ssm_a_im, v_ssm_log_dt, v_ssm_b_re, v_ssm_b_im, v_ssm_c_re, v_ssm_c_im, v_ssm_d, v_w_glu, v_b_glu, v_w_branch_attn, v_w_branch_ssm, v_w_out, v_ffn_norm_g, v_w_up, v_conv_w, v_conv_b, v_w_down, v_final_norm_g):
    args = dict(locals())
    w = {n: args[n] for n in WEIGHTS}
    m = {n: args['m_' + n] for n in WEIGHTS}
    v = {n: args['v_' + n] for n in WEIGHTS}
    xi, yi, ci = _place()
    blk = 2 * xi + yi

    shards = {BIG_KEY[n]: _shard_2d(n, w[n]).astype(BF16) for n in BIG}
    cw_cols = w['conv_w'].shape[2]
    cw_place = lax.dynamic_update_slice(jnp.zeros((3, D_FF), F32), w['conv_w'][0] * (ci == 0).astype(F32),
                                        (0, blk * cw_cols))
    conv_w_full = _unpack(_all_reduce_small(_pack([cw_place]), name="gather_conv_w"), [(3, D_FF)])[0]

    small = {n: w[n] for n in SMALL}
    small['conv_w'] = conv_w_full
    loss_part, grad_x, grads, recvs, sgr = _local_step(x[0], loss_target[0], {}, small, shards)

    halves = []
    for n in BIG:
        key = BIG_KEY[n]
        full = grads[key]
        recv = [recvs[key]] if key in recvs else [recvs[key + '#0'], recvs[key + '#1']]
        axis, interleaved = LAYOUT[key]
        size = full.shape[axis] // N_CHIPS
        own = lax.dynamic_slice_in_dim(full, _block_pos(xi, yi, interleaved) * size, size, axis=axis)
        halves.append(_sum4(own, recv, name="sum4_" + n))
    others = _swap_cores(halves, name="swap_cores")
    out = {}
    for n, mine, other in zip(BIG, halves, others):
        res = _adamw(_shard_2d(n, w[n]), mine, other, _shard_2d(n, m[n]), _shard_2d(n, v[n]), name="adamw_" + n)
        out[n] = [_unshard_2d(n, t) for t in res]

    names = SMALL + ['conv_w']
    shapes = [w[n].shape for n in SMALL] + [(3, D_FF)]
    packed = _pack([sgr[n] for n in names] + [loss_part])
    summed = _unpack(_all_reduce_small(packed, name="all_reduce_small"), shapes + [(1, 1)])
    loss = summed[-1].reshape(())
    sg = dict(zip(names, summed[:-1]))
    sg['conv_w'] = lax.dynamic_slice_in_dim(sg['conv_w'], blk * cw_cols, cw_cols, axis=1)[None]
    res = _adamw(_pack([w[n] for n in names]), _pack([sg[n] for n in names]), None,
                 _pack([m[n] for n in names]), _pack([v[n] for n in names]), name="adamw_small")
    ushapes = [w[n].shape for n in names]
    unpacked = [_unpack(t, ushapes) for t in res]
    for i, n in enumerate(names):
        out[n] = [u[i] for u in unpacked]

    return (loss, grad_x[None], *[out[n][0] for n in WEIGHTS], *[out[n][1] for n in WEIGHTS],
            *[out[n][2] for n in WEIGHTS], *[out[n][3] for n in WEIGHTS])
```

```python
import functools
import math

import jax
import jax.numpy as jnp
from jax import lax
from jax.experimental import pallas as pl
from jax.experimental.pallas import tpu as pltpu

F32 = jnp.float32
BF16 = jnp.bfloat16

D_MODEL = 2048
N_Q_HEADS = 16
HEAD_DIM = 64
ATTN_WIDTH = 1024
KV_WIDTH = 128
BLOCK = 128
SSM_WIDTH = 512
SSM_GROUPS = 32
SSM_GROUP = 16
SSM_STATE = 64
D_FF = 5632
IN_COLS = 5888
RMS_EPS = 1e-6
NEG_BIG = -1e30
N_CHIPS = 4
N_DEV = 8

COL_K = 8
COL_V = 9
COL_U = 10
COL_GA = 14
COL_GS = 30

SSM_SPLIT = 4
SSM_U_BLK = 128
SSM_X_BLK = 512
SSM_CHUNK = 256

ADAM_LR = 0.001
ADAM_B1 = 0.9
ADAM_B2 = 0.999
ADAM_EPS = 1e-08
ADAM_WD = 0.01
ADAM_STEP = 10

VMEM_LIMIT_BYTES = 56 * 1024 * 1024
INV_SQRT2 = 1.0 / math.sqrt(2.0)
INV_SQRT2PI = 1.0 / math.sqrt(2.0 * math.pi)
MESH = pl.DeviceIdType.MESH
ANY = pl.BlockSpec(memory_space=pl.ANY)


def _cp(sem):
    return pltpu.CompilerParams(dimension_semantics=sem, vmem_limit_bytes=VMEM_LIMIT_BYTES)


def _gelu(x):
    return 0.5 * x * (1.0 + lax.erf(x * INV_SQRT2))


def _gelu_grad(x):
    return 0.5 * (1.0 + lax.erf(x * INV_SQRT2)) + x * jnp.exp(-0.5 * x * x) * INV_SQRT2PI


def _sigmoid(x):
    return 1.0 / (1.0 + jnp.exp(-x))


def _place():
    return lax.axis_index("x"), lax.axis_index("y"), lax.axis_index("c")


def _other_chips(x, y):
    return [(1 - x, y), (x, 1 - y), (1 - x, 1 - y)]


def _block_pos(x, y, interleaved):
    return x + 2 * y if interleaved else 2 * x + y


LAYOUT = {'w_in_t': (0, False), 'w_glu': (1, False), 'w_ba': (1, False), 'w_bs': (1, False),
          'w_out': (0, False), 'w_up': (1, True), 'w_down': (0, False)}


def _window(ref, axis, pos, size, rows=None):
    if axis == 0:
        start, count = (0, size) if rows is None else rows
        return ref.at[pl.ds(pos * size + start, count), :]
    cols = pl.ds(pos * size, size)
    return ref.at[:, cols] if rows is None else ref.at[pl.ds(rows[0], rows[1]), cols]


def _gathered_shape(shape, axis):
    return tuple(N_CHIPS * d if a == axis else d for a, d in enumerate(shape))


def _block_shape(shape, axis):
    return tuple(d // N_CHIPS if a == axis else d for a, d in enumerate(shape))


class _GatherPlan:
    def __init__(self, shards, layouts):
        self.arrays = list(shards)
        self.layouts = list(layouts)
        n = len(shards)
        self.out_shape = [jax.ShapeDtypeStruct(_gathered_shape(s.shape, lay[0]), s.dtype)
                          for s, lay in zip(shards, layouts)]
        self.scratch = [pltpu.SemaphoreType.DMA((6 * n,)), pltpu.SemaphoreType.DMA((6 * n,)),
                        pltpu.SemaphoreType.DMA((n,))]

    def _copies(self, kind, ins, outs, sems):
        send, recv, local = sems
        n = len(self.arrays)
        x, y, c = _place()
        copies = []
        for i in range(n):
            axis, interleaved = self.layouts[i]
            size = self.arrays[i].shape[axis]
            h = self.arrays[i].shape[0] // 2
            first = lambda core: core * h
            blk = _block_pos(x, y, interleaved)
            if kind == 'mine':
                copies.append(pltpu.make_async_copy(ins[i], _window(outs[i], axis, blk, size), local.at[i]))
                continue
            for k, (px, py) in enumerate(_other_chips(x, y)):
                theirs = _block_pos(px, py, interleaved)
                if kind in ('ici_out', 'ici_in'):
                    route = dict(send_sem=send.at[3 * i + k], recv_sem=recv.at[3 * i + k],
                                 device_id=(px, py, c), device_id_type=MESH)
                else:
                    route = dict(send_sem=send.at[3 * (n + i) + k], recv_sem=recv.at[3 * (n + i) + k],
                                 device_id=(x, y, 1 - c), device_id_type=MESH)
                if kind == 'ici_out':
                    src, dst = ins[i].at[pl.ds(first(c), h), :], _window(outs[i], axis, blk, size, (first(c), h))
                elif kind == 'd2d_in':
                    src = dst = _window(outs[i], axis, theirs, size, (first(1 - c), h))
                else:
                    src = dst = _window(outs[i], axis, theirs, size, (first(c), h))
                copies.append(pltpu.make_async_remote_copy(src_ref=src, dst_ref=dst, **route))
        return copies

    def start(self, ins, outs, sems):
        for cp in self._copies('mine', ins, outs, sems) + self._copies('ici_out', ins, outs, sems):
            cp.start()

    def middle(self, ins, outs, sems):
        for arrived, onward in zip(self._copies('ici_in', ins, outs, sems), self._copies('d2d_out', ins, outs, sems)):
            arrived.wait_recv()
            onward.start()

    def finish(self, ins, outs, sems):
        for cp in self._copies('d2d_in', ins, outs, sems):
            cp.wait_recv()
        for cp in self._copies('ici_out', ins, outs, sems) + self._copies('d2d_out', ins, outs, sems):
            cp.wait_send()
        for cp in self._copies('mine', ins, outs, sems):
            cp.wait()


class _ScatterPlan:
    def __init__(self, fulls, layouts, part=(0, 1)):
        self.arrays = list(fulls)
        self.layouts = list(layouts)
        self.part = part
        n = len(fulls)
        self.out_shape = []
        for f, lay in zip(fulls, layouts):
            rows, cols = _block_shape(f.shape, lay[0])
            self.out_shape.append(jax.ShapeDtypeStruct((3, rows // part[1], cols), f.dtype))
        self.scratch = [pltpu.SemaphoreType.DMA((3 * n,)), pltpu.SemaphoreType.DMA((3 * n,))]

    def _copies(self, ins, outs, sems):
        send, recv = sems
        x, y, c = _place()
        copies = []
        for i in range(len(self.arrays)):
            axis, interleaved = self.layouts[i]
            size = self.arrays[i].shape[axis] // N_CHIPS
            h = _block_shape(self.arrays[i].shape, axis)[0] // self.part[1]
            rows = (self.part[0] * h, h)
            for k, (px, py) in enumerate(_other_chips(x, y)):
                copies.append(pltpu.make_async_remote_copy(
                    src_ref=_window(ins[i], axis, _block_pos(px, py, interleaved), size, rows), dst_ref=outs[i].at[k],
                    send_sem=send.at[3 * i + k], recv_sem=recv.at[3 * i + k],
                    device_id=(px, py, c), device_id_type=MESH))
        return copies

    def start(self, ins, outs, sems):
        for cp in self._copies(ins, outs, sems):
            cp.start()

    def middle(self, ins, outs, sems):
        pass

    def finish(self, ins, outs, sems):
        for cp in self._copies(ins, outs, sems):
            cp.wait()


def _hosted_call(body, *, grid, in_specs, out_specs, out_shape, scratch_shapes, sem, name, args, comm=None,
                 aliases=None):
    aliases = aliases or {}
    if comm is None:
        outs = pl.pallas_call(body, grid=grid, in_specs=in_specs, out_specs=out_specs, out_shape=out_shape,
                              scratch_shapes=scratch_shapes, name=name, input_output_aliases=aliases,
                              compiler_params=_cp(sem))(*args)
        return outs, None
    n_in, n_out, n_scr = len(in_specs), len(out_specs), len(scratch_shapes)
    nc, ns = len(comm.arrays), len(comm.scratch)
    total = math.prod(grid)
    mid = min(total - 1, (3 * total) // 4)

    def wrapped(*refs):
        pos = 0
        ins = refs[pos:pos + n_in]; pos += n_in
        cins = refs[pos:pos + nc]; pos += nc
        outs = refs[pos:pos + n_out]; pos += n_out
        couts = refs[pos:pos + nc]; pos += nc
        scr = refs[pos:pos + n_scr]; pos += n_scr
        sems = refs[pos:pos + ns]
        step = 0
        for ax, g in enumerate(grid):
            step = step * g + pl.program_id(ax)

        @pl.when(step == 0)
        def _():
            comm.start(cins, couts, sems)

        body(*ins, *outs, *scr)

        @pl.when(step == mid)
        def _():
            comm.middle(cins, couts, sems)

        @pl.when(step == total - 1)
        def _():
            comm.finish(cins, couts, sems)

    res = pl.pallas_call(
        wrapped, grid=grid, in_specs=list(in_specs) + [ANY] * nc, out_specs=list(out_specs) + [ANY] * nc,
        out_shape=list(out_shape) + list(comm.out_shape), scratch_shapes=list(scratch_shapes) + list(comm.scratch),
        name=name, input_output_aliases=aliases,
        compiler_params=_cp(("arbitrary",) * len(grid)))(*args, *comm.arrays)
    return res[:n_out], res[n_out:]


class _Hook:
    def __init__(self, fn, ins=(), in_specs=(), outs=()):
        self.fn, self.ins, self.in_specs, self.outs = fn, list(ins), list(in_specs), list(outs)


def _matmul(a, b, *, ta=False, tb=False, tm, tn, tk, out_dtype=None, bias=None, res=None, inner='n',
            comm=None, prologue=None, epilogue=None, a_shape=None, sequential=False, name):
    if a is None:
        m, kdim = a_shape
    elif ta:
        kdim, m = a.shape
    else:
        m, kdim = a.shape
    if tb:
        n, k2 = b.shape
    else:
        k2, n = b.shape
    assert kdim == k2, (name, kdim, b.shape)
    tm, tn, tk = min(tm, m), min(tn, n), min(tk, kdim)
    assert m % tm == 0 and n % tn == 0 and kdim % tk == 0, (name, m, n, kdim, tm, tn, tk)
    nk = kdim // tk
    dn = (((0 if ta else 1,), (1 if tb else 0,)), ((), ()))
    hooks = [h for h in (prologue, epilogue) if h is not None]
    n_pro_in = len(prologue.ins) if prologue else 0
    n_epi_in = len(epilogue.ins) if epilogue else 0
    n_pro_out = len(prologue.outs) if prologue else 0
    n_epi_out = len(epilogue.outs) if epilogue else 0
    if inner == 'n':
        grid = (m // tm, n // tn, nk)
        mi = lambda g0, g1: g0
        ni = lambda g0, g1: g1
    else:
        grid = (n // tn, m // tm, nk)
        mi = lambda g0, g1: g1
        ni = lambda g0, g1: g0

    def body(*refs):
        refs = list(refs)
        take = lambda cnt: [refs.pop(0) for _ in range(cnt)]
        a_ref = take(1)[0] if a is not None else None
        b_ref = take(1)[0]
        bias_ref = take(1)[0] if bias is not None else None
        res_ref = take(1)[0] if res is not None else None
        pro_in, epi_in = take(n_pro_in), take(n_epi_in)
        o_ref = take(1)[0] if epilogue is None else None
        pro_out, epi_out = take(n_pro_out), take(n_epi_out)
        i, j, k = mi(pl.program_id(0), pl.program_id(1)), ni(pl.program_id(0), pl.program_id(1)), pl.program_id(2)

        def finish(src):
            def result(rows=slice(None)):
                r = src[rows, :]
                if bias_ref is not None:
                    r = r + bias_ref[...]
                if res_ref is not None:
                    r = r + res_ref[rows, :]
                return r

            if epilogue is None:
                o_ref[...] = result().astype(out_dtype)
            else:
                epilogue.fn(result, epi_in, epi_out, i, j)

        a_val = a_ref[...] if prologue is None else prologue.fn(a_ref, pro_in, pro_out, i, k)
        prod = lax.dot_general(a_val.astype(BF16), b_ref[...].astype(BF16), dn, preferred_element_type=F32)
        if nk == 1:
            finish(prod)
            return
        acc_ref = refs[0]

        @pl.when(k == 0)
        def _():
            acc_ref[...] = prod

        @pl.when(k > 0)
        def _():
            acc_ref[...] += prod

        @pl.when(k == nk - 1)
        def _():
            finish(acc_ref)

    spec = lambda shape, fn: pl.BlockSpec(shape, lambda g0, g1, k: fn(mi(g0, g1), ni(g0, g1), k))
    in_specs, args = [], []
    if a is not None:
        in_specs.append(spec((tk, tm), lambda i, j, k: (k, i)) if ta else spec((tm, tk), lambda i, j, k: (i, k)))
        args.append(a)
    in_specs.append(spec((tn, tk), lambda i, j, k: (j, k)) if tb else spec((tk, tn), lambda i, j, k: (k, j)))
    args.append(b)
    if bias is not None:
        in_specs.append(spec((1, tn), lambda i, j, k: (0, j)))
        args.append(bias)
    if res is not None:
        in_specs.append(spec((tm, tn), lambda i, j, k: (i, j)))
        args.append(res)
    for h in hooks:
        in_specs += [spec(shape, fn) for shape, fn in h.in_specs]
        args += h.ins
    out_specs, out_shape = [], []
    if epilogue is None:
        out_specs.append(spec((tm, tn), lambda i, j, k: (i, j)))
        out_shape.append(jax.ShapeDtypeStruct((m, n), out_dtype))
    for h in hooks:
        out_specs += [spec(blk, fn) for _, _, blk, fn in h.outs]
        out_shape += [jax.ShapeDtypeStruct(shape, dtype) for shape, dtype, _, _ in h.outs]
    outs, couts = _hosted_call(
        body, grid=grid, in_specs=in_specs, out_specs=out_specs, out_shape=out_shape,
        scratch_shapes=[pltpu.VMEM((tm, tn), F32)] if nk > 1 else [],
        sem=("arbitrary",) * 3 if sequential else ("parallel", "parallel", "arbitrary"),
        name=name, args=args, comm=comm)
    outs = outs[0] if not hooks else outs
    return outs if comm is None else (outs, couts)


def _rms_fwd(x, g, *, comm=None, name):
    l, d = x.shape
    tr = min(256, l)

    def body(x_ref, g_ref, h_ref):
        xf = x_ref[...]
        r = lax.rsqrt(jnp.mean(xf * xf, axis=-1, keepdims=True) + RMS_EPS)
        h_ref[...] = ((xf * r) * g_ref[...]).astype(BF16)

    row = pl.BlockSpec((tr, d), lambda i: (i, 0))
    return _hosted_call(
        body, grid=(l // tr,), in_specs=[row, pl.BlockSpec((1, d), lambda i: (0, 0))],
        out_specs=[row], out_shape=[jax.ShapeDtypeStruct((l, d), BF16)], scratch_shapes=[],
        sem=("parallel",), name=name, args=(x, g), comm=comm)


EPI_ROWS = 128
ROW_TILE = 256


def _row_chunks(tm):
    ch = min(EPI_ROWS, tm)
    return [slice(c * ch, (c + 1) * ch) for c in range(tm // ch)]


def _rowwise(hook, src, tm, *, name):
    l, d = src.shape
    n_in = len(hook.ins)

    def body(*refs):
        src_ref, ins, outs = refs[0], refs[1:1 + n_in], refs[1 + n_in:]
        hook.fn(lambda rows=slice(None): src_ref[rows, :], ins, outs, pl.program_id(0), 0)

    spec = lambda shape, fn: pl.BlockSpec(shape, lambda i: fn(i, 0, 0))
    return pl.pallas_call(
        body, grid=(l // tm,),
        in_specs=[pl.BlockSpec((tm, d), lambda i: (i, 0))] + [spec(shape, fn) for shape, fn in hook.in_specs],
        out_specs=[spec(blk, fn) for _, _, blk, fn in hook.outs],
        out_shape=[jax.ShapeDtypeStruct(shape, dtype) for shape, dtype, _, _ in hook.outs],
        name=name, compiler_params=_cp(("arbitrary",)))(src, *hook.ins)


def _rms_bwd_hook(x, g, dres, tm, *, with_bf16):
    def fn(result, ins, outs, i, j):
        x_ref, g_ref, dres_ref = ins
        dg_ref = outs[-1]

        @pl.when(i == 0)
        def _():
            dg_ref[...] = jnp.zeros_like(dg_ref)

        for rows in _row_chunks(tm):
            dyv = result(rows)
            xf = x_ref[rows, :]
            r = lax.rsqrt(jnp.mean(xf * xf, axis=-1, keepdims=True) + RMS_EPS)
            xhat = xf * r
            dxh = dyv * g_ref[...]
            dx = r * (dxh - xhat * jnp.mean(dxh * xhat, axis=-1, keepdims=True)) + dres_ref[rows, :]
            outs[0][rows, :] = dx
            if with_bf16:
                outs[1][rows, :] = dx.astype(BF16)
            dg_ref[...] += jnp.sum(dyv * xhat, axis=0, keepdims=True)

    l, d = x.shape
    row = lambda i, j, k: (i, 0)
    vec = lambda i, j, k: (0, 0)
    outs = [((l, d), F32, (tm, d), row)] + ([((l, d), BF16, (tm, d), row)] if with_bf16 else [])
    return _Hook(fn, ins=[x, g, dres], in_specs=[((tm, d), row), ((1, d), vec), ((tm, d), row)],
                 outs=outs + [((1, d), F32, (1, d), vec)])


def _final_loss_hook(g, target, tm):
    l, d = target.shape

    def fn(result, ins, outs, i, j):
        g_ref, t_ref = ins
        dx_ref, dxb_ref, dg_ref, loss_ref = outs
        gv = g_ref[...]

        @pl.when(i == 0)
        def _():
            dg_ref[...] = jnp.zeros_like(dg_ref)
            loss_ref[...] = jnp.zeros_like(loss_ref)

        for rows in _row_chunks(tm):
            xf = result(rows)
            r = lax.rsqrt(jnp.mean(xf * xf, axis=-1, keepdims=True) + RMS_EPS)
            xhat = xf * r
            diff = xhat * gv - t_ref[rows, :]
            dout = diff * (1.0 / d)
            dxh = dout * gv
            dx = r * (dxh - xhat * jnp.mean(dxh * xhat, axis=-1, keepdims=True))
            dx_ref[rows, :] = dx
            dxb_ref[rows, :] = dx.astype(BF16)
            dg_ref[...] += jnp.sum(dout * xhat, axis=0, keepdims=True)
            part = jnp.sum(jnp.mean(diff * diff, axis=-1, keepdims=True), axis=0, keepdims=True)
            loss_ref[...] += 0.5 * part

    row = lambda i, j, k: (i, 0)
    vec = lambda i, j, k: (0, 0)
    return _Hook(fn, ins=[g, target], in_specs=[((1, d), vec), ((tm, d), row)],
                 outs=[((l, d), F32, (tm, d), row), ((l, d), BF16, (tm, d), row),
                       ((1, d), F32, (1, d), vec), ((1, 1), F32, (1, 1), vec)])


Q_PER_KV = 8
GROUP_ROWS = Q_PER_KV * BLOCK


def _attn_masks(n, rows=GROUP_ROWS):
    q_idx = lax.broadcasted_iota(jnp.int32, (rows, 2 * BLOCK), 0) & (BLOCK - 1)
    s_idx = lax.broadcasted_iota(jnp.int32, (rows, 2 * BLOCK), 1)
    dist = q_idx + BLOCK - s_idx
    valid = (dist >= 0) & (dist < BLOCK) & ((n > 0) | (s_idx >= BLOCK))
    return dist.astype(F32), valid


def _dup_half(t, kv_head, lo):
    rolled = pltpu.roll(t, HEAD_DIM, axis=1)
    return jnp.where(lo, t, rolled) if kv_head == 0 else jnp.where(lo, rolled, t)


def _stack_heads(ref, kv_head, lo):
    pieces = []
    for r in range(Q_PER_KV):
        pair = kv_head * 4 + r // 2
        t = ref[:, pair * 128:(pair + 1) * 128].astype(BF16)
        sel = lo if r % 2 == 0 else jnp.logical_not(lo)
        pieces.append(jnp.where(sel, t, jnp.zeros_like(t)))
    return jnp.concatenate(pieces, axis=0)


def _unstack_heads(t, lo):
    return [jnp.where(lo, t[(2 * i) * BLOCK:(2 * i + 1) * BLOCK], t[(2 * i + 1) * BLOCK:(2 * i + 2) * BLOCK])
            for i in range(Q_PER_KV // 2)]


def _per_head_column(values):
    return jnp.concatenate([jnp.full((BLOCK, 1), v, F32) for v in values], axis=0)


def _group_probs(qm, kdup, dist, valid, sink_ref, kv_head):
    heads = [kv_head * Q_PER_KV + r for r in range(Q_PER_KV)]
    slope = _per_head_column([2.0 ** (-8.0 * (h + 1) / N_Q_HEADS) for h in heads])
    sink = _per_head_column([sink_ref[h] for h in heads])
    return _probs(qm, kdup, dist, valid, sink, slope)


def _probs(qm, kdup, dist, valid, sink, slope):
    s = lax.dot_general(qm, kdup, (((1,), (1,)), ((), ())), preferred_element_type=F32)
    s = s * (HEAD_DIM ** -0.5) - slope * dist
    s = jnp.where(valid, s, NEG_BIG)
    m = jnp.maximum(jnp.max(s, axis=-1, keepdims=True), sink)
    p = jnp.exp(s - m)
    esink = jnp.exp(sink - m)
    inv = 1.0 / (jnp.sum(p, axis=-1, keepdims=True) + esink)
    return p * inv, esink * inv


def _attn_fwd(proj, sinks, *, name):
    l = proj.shape[0]
    nb = l // BLOCK

    def body(sink_ref, q_ref, kc_ref, kp_ref, vc_ref, vp_ref, o_ref):
        n = pl.program_id(0)
        dist, valid = _attn_masks(n, BLOCK)
        lo = lax.broadcasted_iota(jnp.int32, (1, BLOCK), 1) < HEAD_DIM
        kx = jnp.concatenate([kp_ref[...], kc_ref[...]], axis=0).astype(BF16)
        vx = jnp.concatenate([vp_ref[...], vc_ref[...]], axis=0).astype(BF16)
        for kv_head in range(2):
            kdup = _dup_half(kx, kv_head, lo)
            vdup = _dup_half(vx, kv_head, lo)
            for pr in range(4):
                pair = kv_head * 4 + pr
                qp = q_ref[:, pair * 128:(pair + 1) * 128].astype(BF16)
                o_pair = jnp.zeros((BLOCK, 128), F32)
                for half in range(2):
                    head = 2 * pair + half
                    sel = lo if half == 0 else jnp.logical_not(lo)
                    qm = jnp.where(sel, qp, jnp.zeros_like(qp))
                    p, _ = _probs(qm, kdup, dist, valid, sink_ref[head], 2.0 ** (-8.0 * (head + 1) / N_Q_HEADS))
                    o = jnp.dot(p.astype(BF16), vdup, preferred_element_type=F32)
                    o_pair = o_pair + jnp.where(sel, o, 0.0)
                o_ref[:, pair * 128:(pair + 1) * 128] = o_pair.astype(BF16)

    kv = lambda col, prev: pl.BlockSpec(
        (BLOCK, KV_WIDTH), (lambda n: (jnp.maximum(n - 1, 0), col)) if prev else (lambda n: (n, col)))
    return pl.pallas_call(
        body, grid=(nb,),
        in_specs=[pl.BlockSpec(memory_space=pltpu.SMEM),
                  pl.BlockSpec((BLOCK, ATTN_WIDTH), lambda n: (n, 0)),
                  kv(COL_K, False), kv(COL_K, True), kv(COL_V, False), kv(COL_V, True)],
        out_specs=pl.BlockSpec((BLOCK, ATTN_WIDTH), lambda n: (n, 0)),
        out_shape=jax.ShapeDtypeStruct((l, ATTN_WIDTH), BF16), name=name,
        compiler_params=_cp(("parallel",)))(sinks, proj, proj, proj, proj, proj)


def _attn_bwd(proj, sinks, dattn, *, comm=None, name):
    l = proj.shape[0]
    nb = l // BLOCK

    def body(sink_ref, q_ref, kc_ref, kp_ref, vc_ref, vp_ref, do_ref,
             dq_ref, dkc_ref, dkp_ref, dvc_ref, dvp_ref, dsink_ref):
        n = pl.program_id(0)
        dist, valid = _attn_masks(n)
        lane = lax.broadcasted_iota(jnp.int32, (1, BLOCK), 1)
        lo = lane < HEAD_DIM
        kx = jnp.concatenate([kp_ref[...], kc_ref[...]], axis=0).astype(BF16)
        vx = jnp.concatenate([vp_ref[...], vc_ref[...]], axis=0).astype(BF16)
        dsink = jnp.zeros((1, BLOCK), F32)
        dk_heads, dv_heads = [], []
        for kv_head in range(2):
            kdup = _dup_half(kx, kv_head, lo)
            vdup = _dup_half(vx, kv_head, lo)
            qm = _stack_heads(q_ref, kv_head, lo)
            dom = _stack_heads(do_ref, kv_head, lo)
            p, psink = _group_probs(qm, kdup, dist, valid, sink_ref, kv_head)
            dp = lax.dot_general(dom, vdup, (((1,), (1,)), ((), ())), preferred_element_type=F32)
            delta = jnp.sum(p * dp, axis=-1, keepdims=True)
            ds = (p * (dp - delta) * (HEAD_DIM ** -0.5)).astype(BF16)
            dsink_rows = -psink * delta
            for r in range(Q_PER_KV):
                part = jnp.sum(dsink_rows[r * BLOCK:(r + 1) * BLOCK])
                dsink = dsink + jnp.where(lane == kv_head * Q_PER_KV + r, part, 0.0)
            dq = jnp.dot(ds, kdup, preferred_element_type=F32)
            for i, dq_pair in enumerate(_unstack_heads(dq, lo)):
                pair = kv_head * 4 + i
                dq_ref[:, pair * 128:(pair + 1) * 128] = dq_pair.astype(BF16)
            dk_acc = lax.dot_general(ds, qm, (((0,), (0,)), ((), ())), preferred_element_type=F32)
            dv_acc = lax.dot_general(p.astype(BF16), dom, (((0,), (0,)), ((), ())), preferred_element_type=F32)
            dk_heads.append(dk_acc + pltpu.roll(dk_acc, HEAD_DIM, axis=1))
            dv_heads.append(dv_acc + pltpu.roll(dv_acc, HEAD_DIM, axis=1))
        dk = jnp.where(lo, dk_heads[0], dk_heads[1])
        dv = jnp.where(lo, dv_heads[0], dv_heads[1])
        dkp_ref[...] = dk[:BLOCK]
        dkc_ref[...] = dk[BLOCK:]
        dvp_ref[...] = dv[:BLOCK]
        dvc_ref[...] = dv[BLOCK:]

        @pl.when(n == 0)
        def _():
            dsink_ref[...] = jnp.zeros_like(dsink_ref)

        dsink_ref[...] += dsink

    kv = lambda col, prev: pl.BlockSpec(
        (BLOCK, KV_WIDTH), (lambda n: (jnp.maximum(n - 1, 0), col)) if prev else (lambda n: (n, col)))
    qspec = pl.BlockSpec((BLOCK, ATTN_WIDTH), lambda n: (n, 0))
    kvout = pl.BlockSpec((BLOCK, KV_WIDTH), lambda n: (n, 0))
    kvshape = jax.ShapeDtypeStruct((l, KV_WIDTH), F32)
    return _hosted_call(
        body, grid=(nb,),
        in_specs=[pl.BlockSpec(memory_space=pltpu.SMEM), qspec,
                  kv(COL_K, False), kv(COL_K, True), kv(COL_V, False), kv(COL_V, True), qspec],
        out_specs=[qspec, kvout, kvout, kvout, kvout, pl.BlockSpec((1, BLOCK), lambda n: (0, 0))],
        out_shape=[jax.ShapeDtypeStruct((l, ATTN_WIDTH), BF16), kvshape, kvshape, kvshape, kvshape,
                   jax.ShapeDtypeStruct((1, BLOCK), F32)],
        scratch_shapes=[], sem=("arbitrary",), name=name,
        args=(sinks, proj, proj, proj, proj, proj, dattn), comm=comm)


def _kv_grad_merge(dkc, dkp, dvc, dvp, *, name):
    l = dkc.shape[0]
    nb = l // BLOCK

    def body(dkc_ref, dkp_ref, dvc_ref, dvp_ref, o_ref):
        last = pl.program_id(0) == nb - 1
        o_ref[:, :KV_WIDTH] = (dkc_ref[...] + jnp.where(last, 0.0, dkp_ref[...])).astype(BF16)
        o_ref[:, KV_WIDTH:] = (dvc_ref[...] + jnp.where(last, 0.0, dvp_ref[...])).astype(BF16)

    cur = pl.BlockSpec((BLOCK, KV_WIDTH), lambda n: (n, 0))
    nxt = pl.BlockSpec((BLOCK, KV_WIDTH), lambda n: (jnp.minimum(n + 1, nb - 1), 0))
    return pl.pallas_call(
        body, grid=(nb,), in_specs=[cur, nxt, cur, nxt],
        out_specs=pl.BlockSpec((BLOCK, 2 * KV_WIDTH), lambda n: (n, 0)),
        out_shape=jax.ShapeDtypeStruct((l, 2 * KV_WIDTH), BF16), name=name,
        compiler_params=_cp(("parallel",)))(dkc, dkp, dvc, dvp)


def _discretize(a_re, a_im, log_dt, b_re, b_im):
    dt = jnp.exp(log_dt)
    mag = jnp.exp(a_re * dt)
    ab_re = mag * jnp.cos(a_im * dt)
    ab_im = mag * jnp.sin(a_im * dt)
    nr = ab_re - 1.0
    ni = ab_im
    den = a_re * a_re + a_im * a_im
    z_re = (nr * a_re + ni * a_im) / den
    z_im = (ni * a_re - nr * a_im) / den
    bb_re = z_re * b_re - z_im * b_im
    bb_im = z_re * b_im + z_im * b_re
    return ab_re, ab_im, bb_re, bb_im


def _ssm_disc_fwd(a_re, a_im, log_dt, b_re, b_im, *, name):
    def body(ar, ai, ld, br, bi, o_ar, o_ai, o_br, o_bi):
        r = _discretize(ar[...], ai[...], ld[...], br[...], bi[...])
        o_ar[...], o_ai[...], o_br[...], o_bi[...] = r

    col = jax.ShapeDtypeStruct(a_re.shape, F32)
    mat = jax.ShapeDtypeStruct(b_re.shape, F32)
    return pl.pallas_call(body, out_shape=[col, col, mat, mat], name=name)(a_re, a_im, log_dt, b_re, b_im)


def _ssm_disc_bwd(a_re, a_im, log_dt, b_re, b_im, d_ab_re, d_ab_im, d_bb_re, d_bb_im, *, name):
    def body(ar, ai, ld, br, bi, g0, g1, g2, g3, o_ar, o_ai, o_ld, o_br, o_bi):
        _, vjp = jax.vjp(_discretize, ar[...], ai[...], ld[...], br[...], bi[...])
        r = vjp((g0[...], g1[...], g2[...], g3[...]))
        o_ar[...], o_ai[...], o_ld[...], o_br[...], o_bi[...] = r

    col = jax.ShapeDtypeStruct(a_re.shape, F32)
    mat = jax.ShapeDtypeStruct(b_re.shape, F32)
    return pl.pallas_call(body, out_shape=[col, col, col, mat, mat], name=name)(
        a_re, a_im, log_dt, b_re, b_im, d_ab_re, d_ab_im, d_bb_re, d_bb_im)


def _shift_rows(x, d, rows, *, down):
    t = x.shape[0]
    if down:
        return jnp.where(rows >= d, pltpu.roll(x, d, axis=0), 0.0)
    return jnp.where(rows < t - d, pltpu.roll(x, t - d, axis=0), 0.0)


def _scan_chunk(xr, xi, ar, ai, *, down):
    t = xr.shape[0]
    rows = lax.broadcasted_iota(jnp.int32, (t, 1), 0)
    pr, pi = ar, ai
    d = 1
    while d < t:
        sr = _shift_rows(xr, d, rows, down=down)
        si = _shift_rows(xi, d, rows, down=down)
        xr, xi = xr + pr * sr - pi * si, xi + pr * si + pi * sr
        pr, pi = pr * pr - pi * pi, 2.0 * pr * pi
        d *= 2
    return xr, xi


def _ssm_fwd(proj, ab, bd, cd, dskip, *, comm=None, name):
    l = proj.shape[0]
    t = min(SSM_CHUNK, l)
    nc = l // t

    def body(u_ref, ab_ref, bd_ref, cd_ref, ds_ref, y_ref, gy_ref, xs_ref, carry_ref):
        c = pl.program_id(1)

        @pl.when(c == 0)
        def _():
            carry_ref[...] = jnp.zeros_like(carry_ref)

        u = u_ref[...]
        ar, ai = ab_ref[0, 0:1, :], ab_ref[0, 1:2, :]
        bu = jnp.dot(u.astype(BF16), bd_ref[0], preferred_element_type=F32)
        rows = lax.broadcasted_iota(jnp.int32, (t, 1), 0)
        cr, ci = carry_ref[0:1, :], carry_ref[1:2, :]
        xr = bu[:, :SSM_X_BLK] + jnp.where(rows == 0, ar * cr - ai * ci, 0.0)
        xi = bu[:, SSM_X_BLK:] + jnp.where(rows == 0, ar * ci + ai * cr, 0.0)
        xr, xi = _scan_chunk(xr, xi, ar, ai, down=True)
        xs_ref[0, :, :SSM_X_BLK] = xr
        xs_ref[0, :, SSM_X_BLK:] = xi
        carry_ref[0:1, :] = xs_ref[0, t - 1:t, :SSM_X_BLK]
        carry_ref[1:2, :] = xs_ref[0, t - 1:t, SSM_X_BLK:]
        y = jnp.dot(xs_ref[0].astype(BF16), cd_ref[0], preferred_element_type=F32) + ds_ref[...] * u
        y_ref[...] = y
        gy_ref[...] = _gelu(y).astype(BF16)

    blk = lambda shape: pl.BlockSpec((1,) + shape, lambda j, c: (j, 0, 0))
    ycol = pl.BlockSpec((t, SSM_U_BLK), lambda j, c: (c, j))
    return _hosted_call(
        body, grid=(SSM_SPLIT, nc),
        in_specs=[pl.BlockSpec((t, SSM_U_BLK), lambda j, c: (c, COL_U + j)),
                  blk((2, SSM_X_BLK)), blk((SSM_U_BLK, 2 * SSM_X_BLK)), blk((2 * SSM_X_BLK, SSM_U_BLK)),
                  pl.BlockSpec((1, SSM_U_BLK), lambda j, c: (0, j))],
        out_specs=[ycol, ycol, pl.BlockSpec((1, t, 2 * SSM_X_BLK), lambda j, c: (j, c, 0))],
        out_shape=[jax.ShapeDtypeStruct((l, SSM_WIDTH), F32), jax.ShapeDtypeStruct((l, SSM_WIDTH), BF16),
                   jax.ShapeDtypeStruct((SSM_SPLIT, l, 2 * SSM_X_BLK), F32)],
        scratch_shapes=[pltpu.VMEM((2, SSM_X_BLK), F32)], sem=("parallel", "arbitrary"), name=name,
        args=(proj, ab, bd, cd, dskip), comm=comm)


def _ssm_bwd(proj, y, dgy, xs, ab, bdt, cdt, dskip, *, comm=None, name):
    l = proj.shape[0]
    t = min(SSM_CHUNK, l)
    nc = l // t

    def body(u_ref, y_ref, dgy_ref, xs_ref, halo_ref, ab_ref, bdt_ref, cdt_ref, ds_ref,
             du_ref, dbd_ref, dcd_ref, dab_ref, dd_ref, carry_ref):
        c = pl.program_id(1)
        ci_ = nc - 1 - c

        @pl.when(c == 0)
        def _():
            carry_ref[...] = jnp.zeros_like(carry_ref)
            dbd_ref[...] = jnp.zeros_like(dbd_ref)
            dcd_ref[...] = jnp.zeros_like(dcd_ref)
            dab_ref[...] = jnp.zeros_like(dab_ref)
            dd_ref[...] = jnp.zeros_like(dd_ref)

        u = u_ref[...]
        dy = dgy_ref[...] * _gelu_grad(y_ref[...])
        dyb = dy.astype(BF16)
        ar, ai = ab_ref[0, 0:1, :], ab_ref[0, 1:2, :]
        g = jnp.dot(dyb, cdt_ref[0], preferred_element_type=F32)
        rows = lax.broadcasted_iota(jnp.int32, (t, 1), 0)
        cr, ci = carry_ref[0:1, :], carry_ref[1:2, :]
        lr = g[:, :SSM_X_BLK] + jnp.where(rows == t - 1, ar * cr + ai * ci, 0.0)
        li = g[:, SSM_X_BLK:] + jnp.where(rows == t - 1, ar * ci - ai * cr, 0.0)
        lr, li = _scan_chunk(lr, li, ar, -ai, down=False)
        lam = jnp.concatenate([lr, li], axis=1)
        carry_ref[0:1, :] = lr[0:1, :]
        carry_ref[1:2, :] = li[0:1, :]
        lamb = lam.astype(BF16)
        du_ref[...] = (jnp.dot(lamb, bdt_ref[0], preferred_element_type=F32) + ds_ref[...] * dy).astype(BF16)
        dbd_ref[0] += lax.dot_general(u.astype(BF16), lamb, (((0,), (0,)), ((), ())),
                                      preferred_element_type=F32)
        xs = xs_ref[0]
        dcd_ref[0] += lax.dot_general(xs.astype(BF16), dyb, (((0,), (0,)), ((), ())),
                                      preferred_element_type=F32)
        halo = jnp.where(ci_ > 0, halo_ref[0, 7:8, :], 0.0)
        xprev = jnp.where(rows == 0, halo, pltpu.roll(xs, 1, axis=0))
        xpr, xpi = xprev[:, :SSM_X_BLK], xprev[:, SSM_X_BLK:]
        dab_ref[0, 0:1, :] += jnp.sum(lr * xpr + li * xpi, axis=0, keepdims=True)
        dab_ref[0, 1:2, :] += jnp.sum(li * xpr - lr * xpi, axis=0, keepdims=True)
        dd_ref[...] += jnp.sum(dy * u, axis=0, keepdims=True)

    blk = lambda shape: pl.BlockSpec((1,) + shape, lambda j, c: (j, 0, 0))
    rev = lambda j, c: (nc - 1 - c, j)
    ycol = pl.BlockSpec((t, SSM_U_BLK), rev)
    hb = t // 8
    return _hosted_call(
        body, grid=(SSM_SPLIT, nc), comm=comm, sem=("parallel", "arbitrary"), name=name,
        args=(proj, y, dgy, xs, xs, ab, bdt, cdt, dskip), scratch_shapes=[pltpu.VMEM((2, SSM_X_BLK), F32)],
        in_specs=[pl.BlockSpec((t, SSM_U_BLK), lambda j, c: (nc - 1 - c, COL_U + j)), ycol, ycol,
                  pl.BlockSpec((1, t, 2 * SSM_X_BLK), lambda j, c: (j, nc - 1 - c, 0)),
                  pl.BlockSpec((1, 8, 2 * SSM_X_BLK),
                               lambda j, c: (j, jnp.maximum((nc - 1 - c) * hb - 1, 0), 0)),
                  blk((2, SSM_X_BLK)), blk((2 * SSM_X_BLK, SSM_U_BLK)), blk((SSM_U_BLK, 2 * SSM_X_BLK)),
                  pl.BlockSpec((1, SSM_U_BLK), lambda j, c: (0, j))],
        out_specs=[ycol, blk((SSM_U_BLK, 2 * SSM_X_BLK)), blk((2 * SSM_X_BLK, SSM_U_BLK)),
                   blk((2, SSM_X_BLK)), pl.BlockSpec((1, SSM_U_BLK), lambda j, c: (0, j))],
        out_shape=[jax.ShapeDtypeStruct((l, SSM_WIDTH), BF16),
                   jax.ShapeDtypeStruct((SSM_SPLIT, SSM_U_BLK, 2 * SSM_X_BLK), F32),
                   jax.ShapeDtypeStruct((SSM_SPLIT, 2 * SSM_X_BLK, SSM_U_BLK), F32),
                   jax.ShapeDtypeStruct((SSM_SPLIT, 2, SSM_X_BLK), F32),
                   jax.ShapeDtypeStruct((1, SSM_WIDTH), F32)])


def _block_diag(t):
    s, g, a, b = t.shape
    return jnp.einsum('sgab,gk->sgakb', t, jnp.eye(g, dtype=t.dtype)).reshape(s, g * a, g * b)


def _block_diag_take(t, a, b):
    s = t.shape[0]
    return jnp.einsum('sgakb,gk->sgab', t.reshape(s, 8, a, 8, b), jnp.eye(8, dtype=t.dtype))


def _glu_fwd_hook(l, tm):
    def fn(result, ins, outs, i, j):
        z = result()
        outs[0][...] = z
        outs[1][...] = (z[:, :SSM_WIDTH] * _sigmoid(z[:, SSM_WIDTH:])).astype(BF16)

    row = lambda i, j, k: (i, 0)
    return _Hook(fn, outs=[((l, 2 * SSM_WIDTH), F32, (tm, 2 * SSM_WIDTH), row),
                           ((l, SSM_WIDTH), BF16, (tm, SSM_WIDTH), row)])


def _glu_bwd_hook(z, tm):
    l = z.shape[0]

    def fn(result, ins, outs, i, j):
        zv_ref, zg_ref = ins
        dz_ref, db_ref = outs
        d = result()
        sg = _sigmoid(zg_ref[...])
        dv = d * sg
        dg = d * zv_ref[...] * sg * (1.0 - sg)
        dz_ref[:, :SSM_WIDTH] = dv.astype(BF16)
        dz_ref[:, SSM_WIDTH:] = dg.astype(BF16)

        @pl.when(i == 0)
        def _():
            db_ref[...] = jnp.zeros_like(db_ref)

        db_ref[:, :SSM_WIDTH] += jnp.sum(dv, axis=0, keepdims=True)
        db_ref[:, SSM_WIDTH:] += jnp.sum(dg, axis=0, keepdims=True)

    half = (tm, SSM_WIDTH)
    return _Hook(fn, ins=[z, z], in_specs=[(half, lambda i, j, k: (i, 0)), (half, lambda i, j, k: (i, 1))],
                 outs=[((l, 2 * SSM_WIDTH), BF16, (tm, 2 * SSM_WIDTH), lambda i, j, k: (i, 0)),
                       ((1, 2 * SSM_WIDTH), F32, (1, 2 * SSM_WIDTH), lambda i, j, k: (0, 0))])


GATE_TC = 256


def _merge_fwd(proj, a, s, *, name):
    l = a.shape[0]
    tr = min(2048, l)

    def body(ga_ref, gs_ref, a_ref, s_ref, o_ref):
        o_ref[...] = (_sigmoid(ga_ref[...]) * a_ref[...] + _sigmoid(gs_ref[...]) * s_ref[...]).astype(BF16)

    own = pl.BlockSpec((tr, GATE_TC), lambda i, j: (i, j))
    return pl.pallas_call(
        body, grid=(l // tr, D_MODEL // GATE_TC),
        in_specs=[pl.BlockSpec((tr, GATE_TC), lambda i, j: (i, COL_GA // 2 + j)),
                  pl.BlockSpec((tr, GATE_TC), lambda i, j: (i, COL_GS // 2 + j)), own, own],
        out_specs=own, out_shape=jax.ShapeDtypeStruct((l, D_MODEL), BF16), name=name,
        compiler_params=_cp(("parallel", "parallel")))(proj, proj, a, s)


def _merge_bwd_hook(proj, a, s, tm):
    def fn(result, ins, outs, i, j):
        ga_ref, gs_ref, a_br, s_br = ins
        d = result()
        sa = _sigmoid(ga_ref[...])
        ss = _sigmoid(gs_ref[...])
        outs[0][...] = (d * sa).astype(BF16)
        outs[1][...] = (d * ss).astype(BF16)
        outs[2][...] = (d * a_br[...] * sa * (1.0 - sa)).astype(BF16)
        outs[3][...] = (d * s_br[...] * ss * (1.0 - ss)).astype(BF16)

    blk = (tm, GATE_TC)
    own = lambda i, j, k: (i, j)
    return _Hook(fn, ins=[proj, proj, a, s],
                 in_specs=[(blk, lambda i, j, k: (i, COL_GA // 2 + j)), (blk, lambda i, j, k: (i, COL_GS // 2 + j)),
                           (blk, own), (blk, own)],
                 outs=[(a.shape, BF16, blk, own)] * 4)


FF_TC = D_FF // 2
FF_NJ = 2
FF_ROWS = 128


FF_HALO = 16


def _conv_taps(ext, rows):
    h = FF_HALO
    return (ext[h:h + rows], pltpu.roll(ext, 1, axis=0)[h:h + rows], pltpu.roll(ext, 2, axis=0)[h:h + rows])


def _ff_specs(tr, l):
    hb = tr // FF_HALO
    last = l // FF_HALO - 1
    prev = lambda i: jnp.maximum(i * hb - 1, 0)
    nxt = lambda i: jnp.minimum((i + 1) * hb, last)
    return dict(
        own=pl.BlockSpec((tr, FF_TC), lambda j, i: (i, j)),
        own_next=pl.BlockSpec((FF_HALO, FF_TC), lambda j, i: (nxt(i), j)),
        val=pl.BlockSpec((tr, FF_TC), lambda j, i: (i, 2 * j)),
        val_next=pl.BlockSpec((FF_HALO, FF_TC), lambda j, i: (nxt(i), 2 * j)),
        gate=pl.BlockSpec((tr, FF_TC), lambda j, i: (i, 2 * j + 1)),
        gate_prev=pl.BlockSpec((FF_HALO, FF_TC), lambda j, i: (prev(i), 2 * j + 1)),
        gate_next=pl.BlockSpec((FF_HALO, FF_TC), lambda j, i: (nxt(i), 2 * j + 1)),
        pair=pl.BlockSpec((tr, 2 * FF_TC), lambda j, i: (i, j)),
        w=pl.BlockSpec((3, FF_TC), lambda j, i: (0, j)),
        b=pl.BlockSpec((1, FF_TC), lambda j, i: (0, j)))


def _ffn_act_fwd(up, conv_w, conv_b, *, name):
    l = up.shape[0]
    tr = min(FF_ROWS, l)

    def body(v_ref, g_ref, prev_ref, w_ref, b_ref, o_ref):
        prev = jnp.where(pl.program_id(1) == 0, 0.0, prev_ref[...].astype(F32))
        g0, g1, g2 = _conv_taps(jnp.concatenate([prev, g_ref[...].astype(F32)], axis=0), tr)
        gc = b_ref[...] + w_ref[0:1, :] * g2 + w_ref[1:2, :] * g1 + w_ref[2:3, :] * g0
        o_ref[...] = (v_ref[...].astype(F32) * _gelu(gc)).astype(BF16)

    sp = _ff_specs(tr, l)
    return pl.pallas_call(
        body, grid=(FF_NJ, l // tr), in_specs=[sp['val'], sp['gate'], sp['gate_prev'], sp['w'], sp['b']],
        out_specs=sp['own'], out_shape=jax.ShapeDtypeStruct((l, D_FF), BF16), name=name,
        compiler_params=_cp(("parallel", "parallel")))(up, up, up, conv_w, conv_b)


def _ffn_act_bwd(dact, up, conv_w, conv_b, *, comm=None, name):
    l = up.shape[0]
    tr = min(FF_ROWS, l)
    ni = l // tr
    te = tr + 8

    def body(d_ref, dn_ref, v_ref, vn_ref, g_ref, gp_ref, gn_ref, w_ref, b_ref, dup_ref, dw_ref, db_ref):
        i = pl.program_id(1)
        f32 = lambda ref, rows=None: ref[...].astype(F32)[:rows]
        prev = jnp.where(i == 0, 0.0, f32(gp_ref))
        g0, g1, g2 = _conv_taps(jnp.concatenate([prev, f32(g_ref), f32(gn_ref, 8)], axis=0), te)
        w0, w1, w2 = w_ref[0:1, :], w_ref[1:2, :], w_ref[2:3, :]
        gc = b_ref[...] + w0 * g2 + w1 * g1 + w2 * g0
        d_own = f32(d_ref)
        d = jnp.concatenate([d_own, jnp.where(i == ni - 1, 0.0, f32(dn_ref, 8))], axis=0)
        v = jnp.concatenate([f32(v_ref), f32(vn_ref, 8)], axis=0)
        dgc = d * v * _gelu_grad(gc)
        ahead1 = pltpu.roll(dgc, te - 1, axis=0)[:tr]
        ahead2 = pltpu.roll(dgc, te - 2, axis=0)[:tr]
        own = dgc[:tr]
        dup_ref[:, :FF_TC] = (d_own * _gelu(gc[:tr])).astype(BF16)
        dup_ref[:, FF_TC:] = (w2 * own + w1 * ahead1 + w0 * ahead2).astype(BF16)

        @pl.when(i == 0)
        def _():
            dw_ref[...] = jnp.zeros_like(dw_ref)
            db_ref[...] = jnp.zeros_like(db_ref)

        dw_ref[0:1, :] += jnp.sum(own * g2[:tr], axis=0, keepdims=True)
        dw_ref[1:2, :] += jnp.sum(own * g1[:tr], axis=0, keepdims=True)
        dw_ref[2:3, :] += jnp.sum(own * g0[:tr], axis=0, keepdims=True)
        db_ref[...] += jnp.sum(own, axis=0, keepdims=True)

    sp = _ff_specs(tr, l)
    return _hosted_call(
        body, grid=(FF_NJ, ni),
        in_specs=[sp['own'], sp['own_next'], sp['val'], sp['val_next'], sp['gate'], sp['gate_prev'],
                  sp['gate_next'], sp['w'], sp['b']],
        out_specs=[sp['pair'], sp['w'], sp['b']],
        out_shape=[jax.ShapeDtypeStruct((l, 2 * D_FF), BF16), jax.ShapeDtypeStruct((3, D_FF), F32),
                   jax.ShapeDtypeStruct((1, D_FF), F32)],
        scratch_shapes=[], sem=("parallel", "arbitrary"), name=name,
        args=(dact, dact, up, up, up, up, up, conv_w, conv_b), comm=comm)


def _col_sum(a, *, name):
    l, n = a.shape
    tr = min(512, l)

    def body(a_ref, o_ref):
        @pl.when(pl.program_id(0) == 0)
        def _():
            o_ref[...] = jnp.zeros_like(o_ref)

        o_ref[...] += jnp.sum(a_ref[...].astype(F32), axis=0, keepdims=True)

    return pl.pallas_call(
        body, grid=(l // tr,), in_specs=[pl.BlockSpec((tr, n), lambda i: (i, 0))],
        out_specs=pl.BlockSpec((1, n), lambda i: (0, 0)), out_shape=jax.ShapeDtypeStruct((1, n), F32),
        name=name, compiler_params=_cp(("arbitrary",)))(a)


def _local_step(x, target, wts, small, shards=None):
    l = x.shape[0]
    wts = dict(wts)
    grads, recvs, sgr = {}, {}, {}
    lay = lambda keys: [LAYOUT[k] for k in keys]
    none = lambda keys: None
    gather = (lambda keys: _GatherPlan([shards[k] for k in keys], lay(keys))) if shards is not None else none
    scatter = (lambda keys: _ScatterPlan([grads[k] for k in keys], lay(keys))) if shards is not None else none

    mm = _matmul

    def take(res, plan, keys, store):
        outs, couts = res
        if plan is not None:
            store.update(zip(keys, couts))
        return outs

    def mm_plan(plan, keys, store, *args, **kw):
        if plan is None:
            return _matmul(*args, **kw)
        return take(_matmul(*args, comm=plan, **kw), plan, keys, store)

    def mm_host(keys, make_plan, store, *args, **kw):
        return mm_plan(make_plan(keys), keys, store, *args, **kw)

    up_scatter = lambda p: _ScatterPlan([grads['w_up']], lay(['w_up']), part=(p, 2)) if shards is not None else None

    col = lambda t: t.reshape(SSM_GROUPS * SSM_STATE, 1)
    a_re, a_im = col(small['ssm_a_re']), col(small['ssm_a_im'])
    log_dt = jnp.repeat(small['ssm_log_dt'].reshape(SSM_GROUPS), SSM_STATE).reshape(-1, 1)
    b_re = small['ssm_b_re'].reshape(SSM_GROUPS * SSM_STATE, SSM_GROUP)
    b_im = small['ssm_b_im'].reshape(SSM_GROUPS * SSM_STATE, SSM_GROUP)
    ab_re, ab_im, bb_re, bb_im = _ssm_disc_fwd(a_re, a_im, log_dt, b_re, b_im, name="ssm_disc_fwd")
    ab = jnp.stack([ab_re.reshape(SSM_SPLIT, SSM_X_BLK), ab_im.reshape(SSM_SPLIT, SSM_X_BLK)], axis=1)
    to_bd = lambda t: _block_diag(t.reshape(SSM_SPLIT, 8, SSM_STATE, SSM_GROUP).transpose(0, 1, 3, 2))
    bd = jnp.concatenate([to_bd(bb_re), to_bd(bb_im)], axis=2)
    c_re = small['ssm_c_re'].reshape(SSM_SPLIT, 8, SSM_GROUP, SSM_STATE)
    c_im = small['ssm_c_im'].reshape(SSM_SPLIT, 8, SSM_GROUP, SSM_STATE)
    cdt = jnp.concatenate([_block_diag(c_re), -_block_diag(c_im)], axis=2)
    bd_b, cdt_b = bd.astype(BF16), cdt.astype(BF16)
    bdt_b, cd_b = bd_b.transpose(0, 2, 1), cdt_b.transpose(0, 2, 1)
    dskip = small['ssm_d'].reshape(1, SSM_WIDTH)

    sinks = small['attn_sinks'].reshape(N_Q_HEADS)
    plan = gather(['w_in_t'])
    h1, = take(_rms_fwd(x, small['attn_norm_g'], comm=plan, name="rms1_fwd"), plan, ['w_in_t'], wts)
    proj = mm_host(['w_glu', 'w_ba', 'w_bs', 'w_out'], gather, wts,
                   h1, wts['w_in_t'], tb=True, tm=512, tn=2944, tk=2048, inner='m', out_dtype=F32,
                   bias=small['b_in'], name="mm_in")
    attn = _attn_fwd(proj, sinks, name="attn_fwd")
    plan = gather(['w_up'])
    y, gy, xs = take(_ssm_fwd(proj, ab, bd_b, cd_b, dskip, comm=plan, name="ssm_fwd"), plan, ['w_up'], wts)
    z, ssm = mm(gy, wts['w_glu'], tm=1024, tn=1024, tk=512, bias=small['b_glu'],
                epilogue=_glu_fwd_hook(l, min(1024, l)), name="mm_glu")
    a_br = mm(attn, wts['w_ba'], tm=1024, tn=1024, tk=1024, out_dtype=F32, name="mm_ba")
    s_br = mm(ssm, wts['w_bs'], tm=1024, tn=1024, tk=512, out_dtype=F32, name="mm_bs")
    tr = min(ROW_TILE, l)
    merged = _merge_fwd(proj, a_br, s_br, name="merge_fwd")
    x2 = mm(merged, wts['w_out'], tm=1024, tn=1024, tk=2048, inner='m', out_dtype=F32, res=x, name="mm_out")
    h2, = take(_rms_fwd(x2, small['ffn_norm_g'], name="rms2_fwd"), None, [], wts)
    up = mm_host(['w_down'], gather, wts,
                 h2, wts['w_up'], tm=1024, tn=1024, tk=2048, out_dtype=BF16, name="mm_up")
    conv_w, conv_b = small['conv_w'], small['conv_b']
    act = _ffn_act_fwd(up, conv_w, conv_b, name="ffn_act_fwd")
    x3 = mm(act, wts['w_down'], tm=1024, tn=1024, tk=2816, out_dtype=F32, res=x2, name="mm_down")
    dx3, dx3b, d_g3, loss = _rowwise(
        _final_loss_hook(small['final_norm_g'].reshape(1, D_MODEL), target, tr), x3, tr, name="final_loss")

    sgr['final_norm_g'] = d_g3.reshape(D_MODEL)
    dact = mm(dx3b, wts['w_down'], tb=True, tm=512, tn=2816, tk=2048, inner='m', out_dtype=BF16, name="mm_dact")
    grads['w_down'] = mm(act, dx3b, ta=True, tm=512, tn=1024, tk=2048, out_dtype=BF16, name="mm_dw_down")
    plan = scatter(['w_down'])
    dup, sgr['conv_w'], sgr['conv_b'] = take(
        _ffn_act_bwd(dact, up, conv_w, conv_b, comm=plan, name="ffn_act_bwd"), plan, ['w_down'], recvs)
    grads['w_up'] = mm(h2, dup, ta=True, tm=1024, tn=1024, tk=2048, out_dtype=BF16, name="mm_dw_up")
    dh2 = mm_plan(up_scatter(0), ['w_up#0'], recvs,
                  dup, wts['w_up'], tb=True, tm=1024, tn=1024, tk=2816, out_dtype=F32, name="mm_dh2")
    dx2, dx2b, sgr['ffn_norm_g'] = _rowwise(
        _rms_bwd_hook(x2, small['ffn_norm_g'], dx3, tr, with_bf16=True), dh2, tr, name="rms2_bwd")

    d_a, d_s, dga, dgs = mm(dx2b, wts['w_out'], tb=True, tm=1024, tn=GATE_TC, tk=2048,
                            epilogue=_merge_bwd_hook(proj, a_br, s_br, min(1024, l)), name="mm_dmerged")
    grads['w_out'] = mm(merged, dx2b, ta=True, tm=1024, tn=1024, tk=2048, out_dtype=BF16, name="mm_dw_out")
    dattn = mm(d_a, wts['w_ba'], tb=True, tm=1024, tn=1024, tk=2048, inner='m', out_dtype=BF16, name="mm_dattn")
    grads['w_ba'] = mm(attn, d_a, ta=True, tm=1024, tn=1024, tk=2048, out_dtype=BF16, name="mm_dw_ba")
    dz, sgr['b_glu'] = mm(d_s, wts['w_bs'], tb=True, tm=1024, tn=512, tk=2048, sequential=True,
                          epilogue=_glu_bwd_hook(z, min(1024, l)), name="mm_dssm")
    grads['w_bs'] = mm(ssm, d_s, ta=True, tm=512, tn=1024, tk=2048, out_dtype=BF16, name="mm_dw_bs")
    grads['w_glu'] = mm(gy, dz, ta=True, tm=512, tn=1024, tk=2048, out_dtype=BF16, name="mm_dw_glu")
    dgy = mm(dz, wts['w_glu'], tb=True, tm=1024, tn=512, tk=1024, inner='m', out_dtype=F32, name="mm_dgy")
    plan = up_scatter(1)
    du, d_bd, d_cd, d_ab, sgr['ssm_d'] = take(
        _ssm_bwd(proj, y, dgy, xs, ab, bdt_b, cdt_b, dskip, comm=plan, name="ssm_bwd"), plan, ['w_up#1'], recvs)
    keys = ['w_out', 'w_ba', 'w_bs', 'w_glu']
    plan = scatter(keys)
    dq, dkc, dkp, dvc, dvp, dsink = take(
        _attn_bwd(proj, sinks, dattn, comm=plan, name="attn_bwd"), plan, keys, recvs)
    dkv = _kv_grad_merge(dkc, dkp, dvc, dvp, name="kv_grad_merge")
    sgr['attn_sinks'] = dsink[:, :N_Q_HEADS]
    dproj = jnp.concatenate([dq, dkv, du, dga, dgs], axis=1)
    sgr['b_in'] = _col_sum(dproj, name="col_sum_dproj")
    grads['w_in_t'] = mm(dproj, h1, ta=True, tm=2944, tn=1024, tk=1024, out_dtype=BF16, name="mm_dw_in")
    dh1 = mm_host(['w_in_t'], scatter, recvs,
                  dproj, wts['w_in_t'], tm=1024, tn=1024, tk=2944, out_dtype=F32, name="mm_dh1")
    grad_x, sgr['attn_norm_g'] = _rowwise(
        _rms_bwd_hook(x, small['attn_norm_g'], dx2, tr, with_bf16=False), dh1, tr, name="rms1_bwd")

    from_bd = lambda t: _block_diag_take(t, SSM_GROUP, SSM_STATE).transpose(0, 1, 3, 2).reshape(
        SSM_GROUPS * SSM_STATE, SSM_GROUP)
    d_bb_re = from_bd(d_bd[:, :, :SSM_X_BLK])
    d_bb_im = from_bd(d_bd[:, :, SSM_X_BLK:])
    d_cdt = d_cd.transpose(0, 2, 1)
    shape_c = (1, SSM_GROUPS, SSM_GROUP, SSM_STATE)
    sgr['ssm_c_re'] = _block_diag_take(d_cdt[:, :, :SSM_X_BLK], SSM_GROUP, SSM_STATE).reshape(shape_c)
    sgr['ssm_c_im'] = -_block_diag_take(d_cdt[:, :, SSM_X_BLK:], SSM_GROUP, SSM_STATE).reshape(shape_c)
    d_a_re, d_a_im, d_ldt, d_b_re, d_b_im = _ssm_disc_bwd(
        a_re, a_im, log_dt, b_re, b_im, d_ab[:, 0, :].reshape(-1, 1), d_ab[:, 1, :].reshape(-1, 1),
        d_bb_re, d_bb_im, name="ssm_disc_bwd")
    sgr['ssm_a_re'] = d_a_re.reshape(1, SSM_GROUPS, SSM_STATE)
    sgr['ssm_a_im'] = d_a_im.reshape(1, SSM_GROUPS, SSM_STATE)
    sgr['ssm_log_dt'] = d_ldt.reshape(SSM_GROUPS, SSM_STATE).sum(axis=1).reshape(1, SSM_GROUPS)
    sgr['ssm_b_re'] = d_b_re.reshape(1, SSM_GROUPS, SSM_STATE, SSM_GROUP)
    sgr['ssm_b_im'] = d_b_im.reshape(1, SSM_GROUPS, SSM_STATE, SSM_GROUP)
    return loss, grad_x, grads, recvs, sgr


def _swap_cores(arrs, *, name):
    n = len(arrs)

    def body(*refs):
        ins, outs = refs[:n], refs[n:2 * n]
        send_sems, recv_sems = refs[2 * n:]
        x, y, c = _place()
        copies = []
        for i in range(n):
            cp = pltpu.make_async_remote_copy(
                src_ref=ins[i], dst_ref=outs[i], send_sem=send_sems.at[i], recv_sem=recv_sems.at[i],
                device_id=(x, y, 1 - c), device_id_type=MESH)
            cp.start()
            copies.append(cp)
        for cp in copies:
            cp.wait()

    return pl.pallas_call(
        body, in_specs=[ANY] * n, out_specs=[ANY] * n,
        out_shape=[jax.ShapeDtypeStruct(a.shape, a.dtype) for a in arrs],
        scratch_shapes=[pltpu.SemaphoreType.DMA((n,)), pltpu.SemaphoreType.DMA((n,))],
        name=name)(*arrs)


def _all_reduce_small(buf, *, name):
    r = buf.shape[0]

    def body(in_ref, out_ref, slots, send_sems, recv_sems):
        x, y, c = _place()
        me = 4 * x + 2 * y + c
        slots[pl.ds(me, 1)] = in_ref[...][None]
        copies = []
        for k in range(N_DEV - 1):
            bx, by, bc = ((k + 1) >> 2) & 1, ((k + 1) >> 1) & 1, (k + 1) & 1
            peer = (1 - x if bx else x, 1 - y if by else y, 1 - c if bc else c)
            cp = pltpu.make_async_remote_copy(
                src_ref=in_ref, dst_ref=slots.at[me], send_sem=send_sems.at[k], recv_sem=recv_sems.at[k],
                device_id=peer, device_id_type=MESH)
            cp.start()
            copies.append(cp)
        for cp in copies:
            cp.wait()
        acc = slots[0]
        for d in range(1, N_DEV):
            acc = acc + slots[d]
        out_ref[...] = acc

    vm = pl.BlockSpec(memory_space=pltpu.VMEM)
    return pl.pallas_call(
        body, in_specs=[vm], out_specs=vm, out_shape=jax.ShapeDtypeStruct((r, 128), F32),
        scratch_shapes=[pltpu.VMEM((N_DEV, r, 128), F32), pltpu.SemaphoreType.DMA((N_DEV - 1,)),
                        pltpu.SemaphoreType.DMA((N_DEV - 1,))],
        name=name)(buf)


def _pack(arrs):
    flat = jnp.concatenate([a.reshape(-1).astype(F32) for a in arrs])
    pad = (-flat.shape[0]) % 1024
    return jnp.pad(flat, (0, pad)).reshape(-1, 128)


def _unpack(buf, shapes):
    flat = buf.reshape(-1)
    out, pos = [], 0
    for s in shapes:
        size = math.prod(s)
        out.append(flat[pos:pos + size].reshape(s))
        pos += size
    return out


TILE_ELEMS = 256 * 1024


def _tile_rows(r, c):
    if r * c <= TILE_ELEMS:
        return r
    for tr in range(TILE_ELEMS // c // 16 * 16, 0, -16):
        if r % tr == 0:
            return tr
    raise ValueError((r, c))


def _sum4(own, recvs, *, name):
    r, c = own.shape
    parts = len(recvs)
    tr = _tile_rows(r // parts, c)
    per = r // parts // tr

    def body(o_ref, *refs):
        out_ref = refs[parts]
        for p in range(parts):
            @pl.when(pl.program_id(0) // per == p)
            def _():
                acc = o_ref[...].astype(F32)
                for k in range(3):
                    acc = acc + refs[p][k].astype(F32)
                out_ref[...] = acc.astype(BF16)

    part_spec = lambda p: pl.BlockSpec((3, tr, c), lambda i: (0, jnp.clip(i - p * per, 0, per - 1), 0))
    return pl.pallas_call(
        body, grid=(r // tr,),
        in_specs=[pl.BlockSpec((tr, c), lambda i: (i, 0))] + [part_spec(p) for p in range(parts)],
        out_specs=pl.BlockSpec((tr, c), lambda i: (i, 0)), out_shape=jax.ShapeDtypeStruct((r, c), BF16),
        name=name, compiler_params=_cp(("parallel",)))(own, *recvs)


def _adamw(w, ga, gb, m, v, *, name):
    r, c = w.shape
    tr = _tile_rows(r, c)
    bc1 = 1.0 - ADAM_B1 ** ADAM_STEP
    bc2 = 1.0 - ADAM_B2 ** ADAM_STEP
    two = gb is not None

    def body(*refs):
        w_ref, ga_ref = refs[0], refs[1]
        pos = 2
        g = ga_ref[...].astype(F32)
        if two:
            g = g + refs[pos][...].astype(F32)
            pos += 1
        m_ref, v_ref, g_out, d_out, m_out, v_out = refs[pos:pos + 6]
        mn = ADAM_B1 * m_ref[...] + (1.0 - ADAM_B1) * g
        vn = ADAM_B2 * v_ref[...] + (1.0 - ADAM_B2) * (g * g)
        m_hat = mn / bc1
        v_hat = vn / bc2
        g_out[...] = g
        d_out[...] = -ADAM_LR * (m_hat / (jnp.sqrt(v_hat) + ADAM_EPS) + ADAM_WD * w_ref[...])
        m_out[...] = mn
        v_out[...] = vn

    spec = pl.BlockSpec((tr, c), lambda i: (i, 0))
    args = [w, ga] + ([gb] if two else []) + [m, v]
    shp = jax.ShapeDtypeStruct((r, c), F32)
    return pl.pallas_call(
        body, grid=(r // tr,), in_specs=[spec] * len(args), out_specs=[spec] * 4,
        out_shape=[shp] * 4, name=name, compiler_params=_cp(("parallel",)))(*args)


BIG = ['w_in', 'w_glu', 'w_branch_attn', 'w_branch_ssm', 'w_out', 'w_up', 'w_down']
BIG_KEY = {'w_in': 'w_in_t', 'w_glu': 'w_glu', 'w_branch_attn': 'w_ba', 'w_branch_ssm': 'w_bs',
           'w_out': 'w_out', 'w_up': 'w_up', 'w_down': 'w_down'}
TRANSPOSED = {'w_in'}
SMALL = ['attn_norm_g', 'b_in', 'attn_sinks', 'ssm_a_re', 'ssm_a_im', 'ssm_log_dt', 'ssm_b_re', 'ssm_b_im',
         'ssm_c_re', 'ssm_c_im', 'ssm_d', 'b_glu', 'ffn_norm_g', 'conv_b', 'final_norm_g']
WEIGHTS = ['attn_norm_g', 'w_in', 'b_in', 'attn_sinks', 'ssm_a_re', 'ssm_a_im', 'ssm_log_dt', 'ssm_b_re',
           'ssm_b_im', 'ssm_c_re', 'ssm_c_im', 'ssm_d', 'w_glu', 'b_glu', 'w_branch_attn', 'w_branch_ssm',
           'w_out', 'ffn_norm_g', 'w_up', 'conv_w', 'conv_b', 'w_down', 'final_norm_g']


def _shard_2d(name, t):
    t = t[0]
    return t.T if name in TRANSPOSED else t


def _unshard_2d(name, t):
    return (t.T if name in TRANSPOSED else t)[None]


def kernel(x, attn_norm_g, w_in, b_in, attn_sinks, ssm_a_re, ssm_a_im, ssm_log_dt, ssm_b_re, ssm_b_im, ssm_c_re, ssm_c_im, ssm_d, w_glu, b_glu, w_branch_attn, w_branch_ssm, w_out, ffn_norm_g, w_up, conv_w, conv_b, w_down, final_norm_g, loss_target, m_attn_norm_g, m_w_in, m_b_in, m_attn_sinks, m_ssm_a_re, m_ssm_a_im, m_ssm_log_dt, m_ssm_b_re, m_ssm_b_im, m_ssm_c_re, m_ssm_c_im, m_ssm_d, m_w_glu, m_b_glu, m_w_branch_attn, m_w_branch_ssm, m_w_out, m_ffn_norm_g, m_w_up, m_conv_w, m_conv_b, m_w_down, m_final_norm_g, v_attn_norm_g, v_w_in, v_b_in, v_attn_sinks, v_ssm_a_re, v_ssm_a_im, v_ssm_log_dt, v_ssm_b_re, v_ssm_b_im, v_ssm_c_re, v_ssm_c_im, v_ssm_d, v_w_glu, v_b_glu, v_w_branch_attn, v_w_branch_ssm, v_w_out, v_ffn_norm_g, v_w_up, v_conv_w, v_conv_b, v_w_down, v_final_norm_g):
    args = dict(locals())
    w = {n: args[n] for n in WEIGHTS}
    m = {n: args['m_' + n] for n in WEIGHTS}
    v = {n: args['v_' + n] for n in WEIGHTS}
    xi, yi, ci = _place()
    blk = 2 * xi + yi

    shards = {BIG_KEY[n]: _shard_2d(n, w[n]).astype(BF16) for n in BIG}
    cw_cols = w['conv_w'].shape[2]
    cw_place = lax.dynamic_update_slice(jnp.zeros((3, D_FF), F32), w['conv_w'][0] * (ci == 0).astype(F32),
                                        (0, blk * cw_cols))
    conv_w_full = _unpack(_all_reduce_small(_pack([cw_place]), name="gather_conv_w"), [(3, D_FF)])[0]

    small = {n: w[n] for n in SMALL}
    small['conv_w'] = conv_w_full
    loss_part, grad_x, grads, recvs, sgr = _local_step(x[0], loss_target[0], {}, small, shards)

    halves = []
    for n in BIG:
        key = BIG_KEY[n]
        full = grads[key]
        recv = [recvs[key]] if key in recvs else [recvs[key + '#0'], recvs[key + '#1']]
        axis, interleaved = LAYOUT[key]
        size = full.shape[axis] // N_CHIPS
        own = lax.dynamic_slice_in_dim(full, _block_pos(xi, yi, interleaved) * size, size, axis=axis)
        halves.append(_sum4(own, recv, name="sum4_" + n))
    others = _swap_cores(halves, name="swap_cores")
    out = {}
    for n, mine, other in zip(BIG, halves, others):
        res = _adamw(_shard_2d(n, w[n]), mine, other, _shard_2d(n, m[n]), _shard_2d(n, v[n]), name="adamw_" + n)
        out[n] = [_unshard_2d(n, t) for t in res]

    names = SMALL + ['conv_w']
    shapes = [w[n].shape for n in SMALL] + [(3, D_FF)]
    packed = _pack([sgr[n] for n in names] + [loss_part])
    summed = _unpack(_all_reduce_small(packed, name="all_reduce_small"), shapes + [(1, 1)])
    loss = summed[-1].reshape(())
    sg = dict(zip(names, summed[:-1]))
    sg['conv_w'] = lax.dynamic_slice_in_dim(sg['conv_w'], blk * cw_cols, cw_cols, axis=1)[None]
    res = _adamw(_pack([w[n] for n in names]), _pack([sg[n] for n in names]), None,
                 _pack([m[n] for n in names]), _pack([v[n] for n in names]), name="adamw_small")
    ushapes = [w[n].shape for n in names]
    unpacked = [_unpack(t, ushapes) for t in res]
    for i, n in enumerate(names):
        out[n] = [u[i] for u in unpacked]

    return (loss, grad_x[None], *[out[n][0] for n in WEIGHTS], *[out[n][1] for n in WEIGHTS],
            *[out[n][2] for n in WEIGHTS], *[out[n][3] for n in WEIGHTS])
```

```python
import functools
import math

import jax
import jax.numpy as jnp
from jax import lax
from jax.experimental import pallas as pl
from jax.experimental.pallas import tpu as pltpu

F32 = jnp.float32
BF16 = jnp.bfloat16

D_MODEL = 2048
N_Q_HEADS = 16
HEAD_DIM = 64
ATTN_WIDTH = 1024
KV_WIDTH = 128
BLOCK = 128
SSM_WIDTH = 512
SSM_GROUPS = 32
SSM_GROUP = 16
SSM_STATE = 64
D_FF = 5632
IN_COLS = 5888
RMS_EPS = 1e-6
NEG_BIG = -1e30
N_CHIPS = 4
N_DEV = 8

COL_K = 8
COL_V = 9
COL_U = 10
COL_GA = 14
COL_GS = 30

SSM_SPLIT = 4
SSM_U_BLK = 128
SSM_X_BLK = 512
SSM_CHUNK = 256

ADAM_LR = 0.001
ADAM_B1 = 0.9
ADAM_B2 = 0.999
ADAM_EPS = 1e-08
ADAM_WD = 0.01
ADAM_STEP = 10

VMEM_LIMIT_BYTES = 56 * 1024 * 1024
INV_SQRT2 = 1.0 / math.sqrt(2.0)
INV_SQRT2PI = 1.0 / math.sqrt(2.0 * math.pi)
MESH = pl.DeviceIdType.MESH
ANY = pl.BlockSpec(memory_space=pl.ANY)


def _cp(sem):
    return pltpu.CompilerParams(dimension_semantics=sem, vmem_limit_bytes=VMEM_LIMIT_BYTES)


def _gelu(x):
    return 0.5 * x * (1.0 + lax.erf(x * INV_SQRT2))


def _gelu_grad(x):
    return 0.5 * (1.0 + lax.erf(x * INV_SQRT2)) + x * jnp.exp(-0.5 * x * x) * INV_SQRT2PI


def _sigmoid(x):
    return 1.0 / (1.0 + jnp.exp(-x))


def _place():
    return lax.axis_index("x"), lax.axis_index("y"), lax.axis_index("c")


def _other_chips(x, y):
    return [(1 - x, y), (x, 1 - y), (1 - x, 1 - y)]


def _block_pos(x, y, interleaved):
    return x + 2 * y if interleaved else 2 * x + y


LAYOUT = {'w_in_t': (0, False), 'w_glu': (1, False), 'w_ba': (1, False), 'w_bs': (1, False),
          'w_out': (0, False), 'w_up': (1, True), 'w_down': (0, False)}


def _window(ref, axis, pos, size, rows=None):
    if axis == 0:
        start, count = (0, size) if rows is None else rows
        return ref.at[pl.ds(pos * size + start, count), :]
    cols = pl.ds(pos * size, size)
    return ref.at[:, cols] if rows is None else ref.at[pl.ds(rows[0], rows[1]), cols]


def _gathered_shape(shape, axis):
    return tuple(N_CHIPS * d if a == axis else d for a, d in enumerate(shape))


def _block_shape(shape, axis):
    return tuple(d // N_CHIPS if a == axis else d for a, d in enumerate(shape))


class _GatherPlan:
    def __init__(self, shards, layouts):
        self.arrays = list(shards)
        self.layouts = list(layouts)
        n = len(shards)
        self.out_shape = [jax.ShapeDtypeStruct(_gathered_shape(s.shape, lay[0]), s.dtype)
                          for s, lay in zip(shards, layouts)]
        self.scratch = [pltpu.SemaphoreType.DMA((6 * n,)), pltpu.SemaphoreType.DMA((6 * n,)),
                        pltpu.SemaphoreType.DMA((n,))]

    def _copies(self, kind, ins, outs, sems):
        send, recv, local = sems
        n = len(self.arrays)
        x, y, c = _place()
        copies = []
        for i in range(n):
            axis, interleaved = self.layouts[i]
            size = self.arrays[i].shape[axis]
            h = self.arrays[i].shape[0] // 2
            first = lambda core: core * h
            blk = _block_pos(x, y, interleaved)
            if kind == 'mine':
                copies.append(pltpu.make_async_copy(ins[i], _window(outs[i], axis, blk, size), local.at[i]))
                continue
            for k, (px, py) in enumerate(_other_chips(x, y)):
                theirs = _block_pos(px, py, interleaved)
                if kind in ('ici_out', 'ici_in'):
                    route = dict(send_sem=send.at[3 * i + k], recv_sem=recv.at[3 * i + k],
                                 device_id=(px, py, c), device_id_type=MESH)
                else:
                    route = dict(send_sem=send.at[3 * (n + i) + k], recv_sem=recv.at[3 * (n + i) + k],
                                 device_id=(x, y, 1 - c), device_id_type=MESH)
                if kind == 'ici_out':
                    src, dst = ins[i].at[pl.ds(first(c), h), :], _window(outs[i], axis, blk, size, (first(c), h))
                elif kind == 'd2d_in':
                    src = dst = _window(outs[i], axis, theirs, size, (first(1 - c), h))
                else:
                    src = dst = _window(outs[i], axis, theirs, size, (first(c), h))
                copies.append(pltpu.make_async_remote_copy(src_ref=src, dst_ref=dst, **route))
        return copies

    def start(self, ins, outs, sems):
        for cp in self._copies('mine', ins, outs, sems) + self._copies('ici_out', ins, outs, sems):
            cp.start()

    def middle(self, ins, outs, sems):
        for arrived, onward in zip(self._copies('ici_in', ins, outs, sems), self._copies('d2d_out', ins, outs, sems)):
            arrived.wait_recv()
            onward.start()

    def finish(self, ins, outs, sems):
        for cp in self._copies('d2d_in', ins, outs, sems):
            cp.wait_recv()
        for cp in self._copies('ici_out', ins, outs, sems) + self._copies('d2d_out', ins, outs, sems):
            cp.wait_send()
        for cp in self._copies('mine', ins, outs, sems):
            cp.wait()


class _ScatterPlan:
    def __init__(self, fulls, layouts, part=(0, 1)):
        self.arrays = list(fulls)
        self.layouts = list(layouts)
        self.part = part
        n = len(fulls)
        self.out_shape = []
        for f, lay in zip(fulls, layouts):
            rows, cols = _block_shape(f.shape, lay[0])
            self.out_shape.append(jax.ShapeDtypeStruct((3, rows // part[1], cols), f.dtype))
        self.scratch = [pltpu.SemaphoreType.DMA((3 * n,)), pltpu.SemaphoreType.DMA((3 * n,))]

    def _copies(self, ins, outs, sems):
        send, recv = sems
        x, y, c = _place()
        copies = []
        for i in range(len(self.arrays)):
            axis, interleaved = self.layouts[i]
            size = self.arrays[i].shape[axis] // N_CHIPS
            h = _block_shape(self.arrays[i].shape, axis)[0] // self.part[1]
            rows = (self.part[0] * h, h)
            for k, (px, py) in enumerate(_other_chips(x, y)):
                copies.append(pltpu.make_async_remote_copy(
                    src_ref=_window(ins[i], axis, _block_pos(px, py, interleaved), size, rows), dst_ref=outs[i].at[k],
                    send_sem=send.at[3 * i + k], recv_sem=recv.at[3 * i + k],
                    device_id=(px, py, c), device_id_type=MESH))
        return copies

    def start(self, ins, outs, sems):
        for cp in self._copies(ins, outs, sems):
            cp.start()

    def middle(self, ins, outs, sems):
        pass

    def finish(self, ins, outs, sems):
        for cp in self._copies(ins, outs, sems):
            cp.wait()


def _hosted_call(body, *, grid, in_specs, out_specs, out_shape, scratch_shapes, sem, name, args, comm=None,
                 aliases=None):
    aliases = aliases or {}
    if comm is None:
        outs = pl.pallas_call(body, grid=grid, in_specs=in_specs, out_specs=out_specs, out_shape=out_shape,
                              scratch_shapes=scratch_shapes, name=name, input_output_aliases=aliases,
                              compiler_params=_cp(sem))(*args)
        return outs, None
    n_in, n_out, n_scr = len(in_specs), len(out_specs), len(scratch_shapes)
    nc, ns = len(comm.arrays), len(comm.scratch)
    total = math.prod(grid)
    mid = total - max(1, total // 8)

    def wrapped(*refs):
        pos = 0
        ins = refs[pos:pos + n_in]; pos += n_in
        cins = refs[pos:pos + nc]; pos += nc
        outs = refs[pos:pos + n_out]; pos += n_out
        couts = refs[pos:pos + nc]; pos += nc
        scr = refs[pos:pos + n_scr]; pos += n_scr
        sems = refs[pos:pos + ns]
        step = 0
        for ax, g in enumerate(grid):
            step = step * g + pl.program_id(ax)

        @pl.when(step == 0)
        def _():
            comm.start(cins, couts, sems)

        body(*ins, *outs, *scr)

        @pl.when(step == mid)
        def _():
            comm.middle(cins, couts, sems)

        @pl.when(step == total - 1)
        def _():
            comm.finish(cins, couts, sems)

    res = pl.pallas_call(
        wrapped, grid=grid, in_specs=list(in_specs) + [ANY] * nc, out_specs=list(out_specs) + [ANY] * nc,
        out_shape=list(out_shape) + list(comm.out_shape), scratch_shapes=list(scratch_shapes) + list(comm.scratch),
        name=name, input_output_aliases=aliases,
        compiler_params=_cp(("arbitrary",) * len(grid)))(*args, *comm.arrays)
    return res[:n_out], res[n_out:]


class _Hook:
    def __init__(self, fn, ins=(), in_specs=(), outs=()):
        self.fn, self.ins, self.in_specs, self.outs = fn, list(ins), list(in_specs), list(outs)


def _matmul(a, b, *, ta=False, tb=False, tm, tn, tk, out_dtype=None, bias=None, res=None, inner='n',
            comm=None, prologue=None, epilogue=None, a_shape=None, sequential=False, name):
    if a is None:
        m, kdim = a_shape
    elif ta:
        kdim, m = a.shape
    else:
        m, kdim = a.shape
    if tb:
        n, k2 = b.shape
    else:
        k2, n = b.shape
    assert kdim == k2, (name, kdim, b.shape)
    tm, tn, tk = min(tm, m), min(tn, n), min(tk, kdim)
    assert m % tm == 0 and n % tn == 0 and kdim % tk == 0, (name, m, n, kdim, tm, tn, tk)
    nk = kdim // tk
    dn = (((0 if ta else 1,), (1 if tb else 0,)), ((), ()))
    hooks = [h for h in (prologue, epilogue) if h is not None]
    n_pro_in = len(prologue.ins) if prologue else 0
    n_epi_in = len(epilogue.ins) if epilogue else 0
    n_pro_out = len(prologue.outs) if prologue else 0
    n_epi_out = len(epilogue.outs) if epilogue else 0
    if inner == 'n':
        grid = (m // tm, n // tn, nk)
        mi = lambda g0, g1: g0
        ni = lambda g0, g1: g1
    else:
        grid = (n // tn, m // tm, nk)
        mi = lambda g0, g1: g1
        ni = lambda g0, g1: g0

    def body(*refs):
        refs = list(refs)
        take = lambda cnt: [refs.pop(0) for _ in range(cnt)]
        a_ref = take(1)[0] if a is not None else None
        b_ref = take(1)[0]
        bias_ref = take(1)[0] if bias is not None else None
        res_ref = take(1)[0] if res is not None else None
        pro_in, epi_in = take(n_pro_in), take(n_epi_in)
        o_ref = take(1)[0] if epilogue is None else None
        pro_out, epi_out = take(n_pro_out), take(n_epi_out)
        i, j, k = mi(pl.program_id(0), pl.program_id(1)), ni(pl.program_id(0), pl.program_id(1)), pl.program_id(2)

        def finish(src):
            def result(rows=slice(None)):
                r = src[rows, :]
                if bias_ref is not None:
                    r = r + bias_ref[...]
                if res_ref is not None:
                    r = r + res_ref[rows, :]
                return r

            if epilogue is None:
                o_ref[...] = result().astype(out_dtype)
            else:
                epilogue.fn(result, epi_in, epi_out, i, j)

        a_val = a_ref[...] if prologue is None else prologue.fn(a_ref, pro_in, pro_out, i, k)
        prod = lax.dot_general(a_val.astype(BF16), b_ref[...].astype(BF16), dn, preferred_element_type=F32)
        if nk == 1:
            finish(prod)
            return
        acc_ref = refs[0]

        @pl.when(k == 0)
        def _():
            acc_ref[...] = prod

        @pl.when(k > 0)
        def _():
            acc_ref[...] += prod

        @pl.when(k == nk - 1)
        def _():
            finish(acc_ref)

    spec = lambda shape, fn: pl.BlockSpec(shape, lambda g0, g1, k: fn(mi(g0, g1), ni(g0, g1), k))
    in_specs, args = [], []
    if a is not None:
        in_specs.append(spec((tk, tm), lambda i, j, k: (k, i)) if ta else spec((tm, tk), lambda i, j, k: (i, k)))
        args.append(a)
    in_specs.append(spec((tn, tk), lambda i, j, k: (j, k)) if tb else spec((tk, tn), lambda i, j, k: (k, j)))
    args.append(b)
    if bias is not None:
        in_specs.append(spec((1, tn), lambda i, j, k: (0, j)))
        args.append(bias)
    if res is not None:
        in_specs.append(spec((tm, tn), lambda i, j, k: (i, j)))
        args.append(res)
    for h in hooks:
        in_specs += [spec(shape, fn) for shape, fn in h.in_specs]
        args += h.ins
    out_specs, out_shape = [], []
    if epilogue is None:
        out_specs.append(spec((tm, tn), lambda i, j, k: (i, j)))
        out_shape.append(jax.ShapeDtypeStruct((m, n), out_dtype))
    for h in hooks:
        out_specs += [spec(blk, fn) for _, _, blk, fn in h.outs]
        out_shape += [jax.ShapeDtypeStruct(shape, dtype) for shape, dtype, _, _ in h.outs]
    outs, couts = _hosted_call(
        body, grid=grid, in_specs=in_specs, out_specs=out_specs, out_shape=out_shape,
        scratch_shapes=[pltpu.VMEM((tm, tn), F32)] if nk > 1 else [],
        sem=("arbitrary",) * 3 if sequential else ("parallel", "parallel", "arbitrary"),
        name=name, args=args, comm=comm)
    outs = outs[0] if not hooks else outs
    return outs if comm is None else (outs, couts)


def _rms_fwd(x, g, *, comm=None, name):
    l, d = x.shape
    tr = min(256, l)

    def body(x_ref, g_ref, h_ref):
        xf = x_ref[...]
        r = lax.rsqrt(jnp.mean(xf * xf, axis=-1, keepdims=True) + RMS_EPS)
        h_ref[...] = ((xf * r) * g_ref[...]).astype(BF16)

    row = pl.BlockSpec((tr, d), lambda i: (i, 0))
    return _hosted_call(
        body, grid=(l // tr,), in_specs=[row, pl.BlockSpec((1, d), lambda i: (0, 0))],
        out_specs=[row], out_shape=[jax.ShapeDtypeStruct((l, d), BF16)], scratch_shapes=[],
        sem=("parallel",), name=name, args=(x, g), comm=comm)


EPI_ROWS = 128
ROW_TILE = 256


def _row_chunks(tm):
    ch = min(EPI_ROWS, tm)
    return [slice(c * ch, (c + 1) * ch) for c in range(tm // ch)]


def _rowwise(hook, src, tm, *, name):
    l, d = src.shape
    n_in = len(hook.ins)

    def body(*refs):
        src_ref, ins, outs = refs[0], refs[1:1 + n_in], refs[1 + n_in:]
        hook.fn(lambda rows=slice(None): src_ref[rows, :], ins, outs, pl.program_id(0), 0)

    spec = lambda shape, fn: pl.BlockSpec(shape, lambda i: fn(i, 0, 0))
    return pl.pallas_call(
        body, grid=(l // tm,),
        in_specs=[pl.BlockSpec((tm, d), lambda i: (i, 0))] + [spec(shape, fn) for shape, fn in hook.in_specs],
        out_specs=[spec(blk, fn) for _, _, blk, fn in hook.outs],
        out_shape=[jax.ShapeDtypeStruct(shape, dtype) for shape, dtype, _, _ in hook.outs],
        name=name, compiler_params=_cp(("arbitrary",)))(src, *hook.ins)


def _rms_bwd_hook(x, g, dres, tm, out_dtype):
    def fn(result, ins, outs, i, j):
        x_ref, g_ref, dres_ref = ins
        dx_ref, dg_ref = outs

        @pl.when(i == 0)
        def _():
            dg_ref[...] = jnp.zeros_like(dg_ref)

        for rows in _row_chunks(tm):
            dyv = result(rows)
            xf = x_ref[rows, :]
            r = lax.rsqrt(jnp.mean(xf * xf, axis=-1, keepdims=True) + RMS_EPS)
            xhat = xf * r
            dxh = dyv * g_ref[...]
            dx = r * (dxh - xhat * jnp.mean(dxh * xhat, axis=-1, keepdims=True)) + dres_ref[rows, :].astype(F32)
            dx_ref[rows, :] = dx.astype(out_dtype)
            dg_ref[...] += jnp.sum(dyv * xhat, axis=0, keepdims=True)

    l, d = x.shape
    row = lambda i, j, k: (i, 0)
    vec = lambda i, j, k: (0, 0)
    return _Hook(fn, ins=[x, g, dres], in_specs=[((tm, d), row), ((1, d), vec), ((tm, d), row)],
                 outs=[((l, d), out_dtype, (tm, d), row), ((1, d), F32, (1, d), vec)])


def _final_loss_hook(g, target, tm):
    l, d = target.shape

    def fn(result, ins, outs, i, j):
        g_ref, t_ref = ins
        dxb_ref, dg_ref, loss_ref = outs
        gv = g_ref[...]

        @pl.when(i == 0)
        def _():
            dg_ref[...] = jnp.zeros_like(dg_ref)
            loss_ref[...] = jnp.zeros_like(loss_ref)

        for rows in _row_chunks(tm):
            xf = result(rows)
            r = lax.rsqrt(jnp.mean(xf * xf, axis=-1, keepdims=True) + RMS_EPS)
            xhat = xf * r
            diff = xhat * gv - t_ref[rows, :]
            dout = diff * (1.0 / d)
            dxh = dout * gv
            dx = r * (dxh - xhat * jnp.mean(dxh * xhat, axis=-1, keepdims=True))
            dxb_ref[rows, :] = dx.astype(BF16)
            dg_ref[...] += jnp.sum(dout * xhat, axis=0, keepdims=True)
            part = jnp.sum(jnp.mean(diff * diff, axis=-1, keepdims=True), axis=0, keepdims=True)
            loss_ref[...] += 0.5 * part

    row = lambda i, j, k: (i, 0)
    vec = lambda i, j, k: (0, 0)
    return _Hook(fn, ins=[g, target], in_specs=[((1, d), vec), ((tm, d), row)],
                 outs=[((l, d), BF16, (tm, d), row), ((1, d), F32, (1, d), vec), ((1, 1), F32, (1, 1), vec)])


Q_PER_KV = 8
GROUP_ROWS = Q_PER_KV * BLOCK


def _attn_masks(n, rows=GROUP_ROWS):
    q_idx = lax.broadcasted_iota(jnp.int32, (rows, 2 * BLOCK), 0) & (BLOCK - 1)
    s_idx = lax.broadcasted_iota(jnp.int32, (rows, 2 * BLOCK), 1)
    dist = q_idx + BLOCK - s_idx
    valid = (dist >= 0) & (dist < BLOCK) & ((n > 0) | (s_idx >= BLOCK))
    return dist.astype(F32), valid


def _dup_half(t, kv_head, lo):
    rolled = pltpu.roll(t, HEAD_DIM, axis=1)
    return jnp.where(lo, t, rolled) if kv_head == 0 else jnp.where(lo, rolled, t)


def _stack_heads(ref, kv_head, lo):
    pieces = []
    for r in range(Q_PER_KV):
        pair = kv_head * 4 + r // 2
        t = ref[:, pair * 128:(pair + 1) * 128].astype(BF16)
        sel = lo if r % 2 == 0 else jnp.logical_not(lo)
        pieces.append(jnp.where(sel, t, jnp.zeros_like(t)))
    return jnp.concatenate(pieces, axis=0)


def _unstack_heads(t, lo):
    return [jnp.where(lo, t[(2 * i) * BLOCK:(2 * i + 1) * BLOCK], t[(2 * i + 1) * BLOCK:(2 * i + 2) * BLOCK])
            for i in range(Q_PER_KV // 2)]


def _per_head_column(values):
    return jnp.concatenate([jnp.full((BLOCK, 1), v, F32) for v in values], axis=0)


def _group_probs(qm, kdup, dist, valid, sink_ref, kv_head):
    heads = [kv_head * Q_PER_KV + r for r in range(Q_PER_KV)]
    slope = _per_head_column([2.0 ** (-8.0 * (h + 1) / N_Q_HEADS) for h in heads])
    sink = _per_head_column([sink_ref[h] for h in heads])
    return _probs(qm, kdup, dist, valid, sink, slope)


def _probs(qm, kdup, dist, valid, sink, slope):
    s = lax.dot_general(qm, kdup, (((1,), (1,)), ((), ())), preferred_element_type=F32)
    s = s * (HEAD_DIM ** -0.5) - slope * dist
    s = jnp.where(valid, s, NEG_BIG)
    m = jnp.maximum(jnp.max(s, axis=-1, keepdims=True), sink)
    p = jnp.exp(s - m)
    esink = jnp.exp(sink - m)
    inv = 1.0 / (jnp.sum(p, axis=-1, keepdims=True) + esink)
    return p * inv, esink * inv


def _attn_fwd(proj, sinks, *, name):
    l = proj.shape[0]
    nb = l // BLOCK

    def body(sink_ref, q_ref, kc_ref, kp_ref, vc_ref, vp_ref, o_ref):
        n = pl.program_id(0)
        dist, valid = _attn_masks(n, BLOCK)
        lo = lax.broadcasted_iota(jnp.int32, (1, BLOCK), 1) < HEAD_DIM
        kx = jnp.concatenate([kp_ref[...], kc_ref[...]], axis=0).astype(BF16)
        vx = jnp.concatenate([vp_ref[...], vc_ref[...]], axis=0).astype(BF16)
        for kv_head in range(2):
            kdup = _dup_half(kx, kv_head, lo)
            vdup = _dup_half(vx, kv_head, lo)
            for pr in range(4):
                pair = kv_head * 4 + pr
                qp = q_ref[:, pair * 128:(pair + 1) * 128].astype(BF16)
                o_pair = jnp.zeros((BLOCK, 128), F32)
                for half in range(2):
                    head = 2 * pair + half
                    sel = lo if half == 0 else jnp.logical_not(lo)
                    qm = jnp.where(sel, qp, jnp.zeros_like(qp))
                    p, _ = _probs(qm, kdup, dist, valid, sink_ref[head], 2.0 ** (-8.0 * (head + 1) / N_Q_HEADS))
                    o = jnp.dot(p.astype(BF16), vdup, preferred_element_type=F32)
                    o_pair = o_pair + jnp.where(sel, o, 0.0)
                o_ref[:, pair * 128:(pair + 1) * 128] = o_pair.astype(BF16)

    kv = lambda col, prev: pl.BlockSpec(
        (BLOCK, KV_WIDTH), (lambda n: (jnp.maximum(n - 1, 0), col)) if prev else (lambda n: (n, col)))
    return pl.pallas_call(
        body, grid=(nb,),
        in_specs=[pl.BlockSpec(memory_space=pltpu.SMEM),
                  pl.BlockSpec((BLOCK, ATTN_WIDTH), lambda n: (n, 0)),
                  kv(COL_K, False), kv(COL_K, True), kv(COL_V, False), kv(COL_V, True)],
        out_specs=pl.BlockSpec((BLOCK, ATTN_WIDTH), lambda n: (n, 0)),
        out_shape=jax.ShapeDtypeStruct((l, ATTN_WIDTH), BF16), name=name,
        compiler_params=_cp(("parallel",)))(sinks, proj, proj, proj, proj, proj)


def _attn_bwd(proj, sinks, dattn, *, comm=None, name):
    l = proj.shape[0]
    nb = l // BLOCK

    def body(sink_ref, q_ref, kc_ref, kp_ref, vc_ref, vp_ref, do_ref,
             dq_ref, dkc_ref, dkp_ref, dvc_ref, dvp_ref, dsink_ref):
        n = pl.program_id(0)
        dist, valid = _attn_masks(n)
        lane = lax.broadcasted_iota(jnp.int32, (1, BLOCK), 1)
        lo = lane < HEAD_DIM
        kx = jnp.concatenate([kp_ref[...], kc_ref[...]], axis=0).astype(BF16)
        vx = jnp.concatenate([vp_ref[...], vc_ref[...]], axis=0).astype(BF16)
        dsink = jnp.zeros((1, BLOCK), F32)
        dk_heads, dv_heads = [], []
        for kv_head in range(2):
            kdup = _dup_half(kx, kv_head, lo)
            vdup = _dup_half(vx, kv_head, lo)
            qm = _stack_heads(q_ref, kv_head, lo)
            dom = _stack_heads(do_ref, kv_head, lo)
            p, psink = _group_probs(qm, kdup, dist, valid, sink_ref, kv_head)
            dp = lax.dot_general(dom, vdup, (((1,), (1,)), ((), ())), preferred_element_type=F32)
            delta = jnp.sum(p * dp, axis=-1, keepdims=True)
            ds = (p * (dp - delta) * (HEAD_DIM ** -0.5)).astype(BF16)
            dsink_rows = -psink * delta
            for r in range(Q_PER_KV):
                part = jnp.sum(dsink_rows[r * BLOCK:(r + 1) * BLOCK])
                dsink = dsink + jnp.where(lane == kv_head * Q_PER_KV + r, part, 0.0)
            dq = jnp.dot(ds, kdup, preferred_element_type=F32)
            for i, dq_pair in enumerate(_unstack_heads(dq, lo)):
                pair = kv_head * 4 + i
                dq_ref[:, pair * 128:(pair + 1) * 128] = dq_pair.astype(BF16)
            dk_acc = lax.dot_general(ds, qm, (((0,), (0,)), ((), ())), preferred_element_type=F32)
            dv_acc = lax.dot_general(p.astype(BF16), dom, (((0,), (0,)), ((), ())), preferred_element_type=F32)
            dk_heads.append(dk_acc + pltpu.roll(dk_acc, HEAD_DIM, axis=1))
            dv_heads.append(dv_acc + pltpu.roll(dv_acc, HEAD_DIM, axis=1))
        dk = jnp.where(lo, dk_heads[0], dk_heads[1])
        dv = jnp.where(lo, dv_heads[0], dv_heads[1])
        dkp_ref[...] = dk[:BLOCK]
        dkc_ref[...] = dk[BLOCK:]
        dvp_ref[...] = dv[:BLOCK]
        dvc_ref[...] = dv[BLOCK:]

        @pl.when(n == 0)
        def _():
            dsink_ref[...] = jnp.zeros_like(dsink_ref)

        dsink_ref[...] += dsink

    kv = lambda col, prev: pl.BlockSpec(
        (BLOCK, KV_WIDTH), (lambda n: (jnp.maximum(n - 1, 0), col)) if prev else (lambda n: (n, col)))
    qspec = pl.BlockSpec((BLOCK, ATTN_WIDTH), lambda n: (n, 0))
    kvout = pl.BlockSpec((BLOCK, KV_WIDTH), lambda n: (n, 0))
    kvshape = jax.ShapeDtypeStruct((l, KV_WIDTH), F32)
    return _hosted_call(
        body, grid=(nb,),
        in_specs=[pl.BlockSpec(memory_space=pltpu.SMEM), qspec,
                  kv(COL_K, False), kv(COL_K, True), kv(COL_V, False), kv(COL_V, True), qspec],
        out_specs=[qspec, kvout, kvout, kvout, kvout, pl.BlockSpec((1, BLOCK), lambda n: (0, 0))],
        out_shape=[jax.ShapeDtypeStruct((l, ATTN_WIDTH), BF16), kvshape, kvshape, kvshape, kvshape,
                   jax.ShapeDtypeStruct((1, BLOCK), F32)],
        scratch_shapes=[], sem=("arbitrary",), name=name,
        args=(sinks, proj, proj, proj, proj, proj, dattn), comm=comm)


def _kv_grad_merge(dkc, dkp, dvc, dvp, *, name):
    l = dkc.shape[0]
    nb = l // BLOCK

    def body(dkc_ref, dkp_ref, dvc_ref, dvp_ref, o_ref):
        last = pl.program_id(0) == nb - 1
        o_ref[:, :KV_WIDTH] = (dkc_ref[...] + jnp.where(last, 0.0, dkp_ref[...])).astype(BF16)
        o_ref[:, KV_WIDTH:] = (dvc_ref[...] + jnp.where(last, 0.0, dvp_ref[...])).astype(BF16)

    cur = pl.BlockSpec((BLOCK, KV_WIDTH), lambda n: (n, 0))
    nxt = pl.BlockSpec((BLOCK, KV_WIDTH), lambda n: (jnp.minimum(n + 1, nb - 1), 0))
    return pl.pallas_call(
        body, grid=(nb,), in_specs=[cur, nxt, cur, nxt],
        out_specs=pl.BlockSpec((BLOCK, 2 * KV_WIDTH), lambda n: (n, 0)),
        out_shape=jax.ShapeDtypeStruct((l, 2 * KV_WIDTH), BF16), name=name,
        compiler_params=_cp(("parallel",)))(dkc, dkp, dvc, dvp)


def _discretize(a_re, a_im, log_dt, b_re, b_im):
    dt = jnp.exp(log_dt)
    mag = jnp.exp(a_re * dt)
    ab_re = mag * jnp.cos(a_im * dt)
    ab_im = mag * jnp.sin(a_im * dt)
    nr = ab_re - 1.0
    ni = ab_im
    den = a_re * a_re + a_im * a_im
    z_re = (nr * a_re + ni * a_im) / den
    z_im = (ni * a_re - nr * a_im) / den
    bb_re = z_re * b_re - z_im * b_im
    bb_im = z_re * b_im + z_im * b_re
    return ab_re, ab_im, bb_re, bb_im


def _ssm_disc_fwd(a_re, a_im, log_dt, b_re, b_im, *, name):
    def body(ar, ai, ld, br, bi, o_ar, o_ai, o_br, o_bi):
        r = _discretize(ar[...], ai[...], ld[...], br[...], bi[...])
        o_ar[...], o_ai[...], o_br[...], o_bi[...] = r

    col = jax.ShapeDtypeStruct(a_re.shape, F32)
    mat = jax.ShapeDtypeStruct(b_re.shape, F32)
    return pl.pallas_call(body, out_shape=[col, col, mat, mat], name=name)(a_re, a_im, log_dt, b_re, b_im)


def _ssm_disc_bwd(a_re, a_im, log_dt, b_re, b_im, d_ab_re, d_ab_im, d_bb_re, d_bb_im, *, name):
    def body(ar, ai, ld, br, bi, g0, g1, g2, g3, o_ar, o_ai, o_ld, o_br, o_bi):
        _, vjp = jax.vjp(_discretize, ar[...], ai[...], ld[...], br[...], bi[...])
        r = vjp((g0[...], g1[...], g2[...], g3[...]))
        o_ar[...], o_ai[...], o_ld[...], o_br[...], o_bi[...] = r

    col = jax.ShapeDtypeStruct(a_re.shape, F32)
    mat = jax.ShapeDtypeStruct(b_re.shape, F32)
    return pl.pallas_call(body, out_shape=[col, col, col, mat, mat], name=name)(
        a_re, a_im, log_dt, b_re, b_im, d_ab_re, d_ab_im, d_bb_re, d_bb_im)


def _shift_rows(x, d, rows, *, down):
    t = x.shape[0]
    if down:
        return jnp.where(rows >= d, pltpu.roll(x, d, axis=0), 0.0)
    return jnp.where(rows < t - d, pltpu.roll(x, t - d, axis=0), 0.0)


def _scan_chunk(xr, xi, ar, ai, *, down):
    t = xr.shape[0]
    rows = lax.broadcasted_iota(jnp.int32, (t, 1), 0)
    pr, pi = ar, ai
    d = 1
    while d < t:
        sr = _shift_rows(xr, d, rows, down=down)
        si = _shift_rows(xi, d, rows, down=down)
        xr, xi = xr + pr * sr - pi * si, xi + pr * si + pi * sr
        pr, pi = pr * pr - pi * pi, 2.0 * pr * pi
        d *= 2
    return xr, xi


def _ssm_fwd(proj, ab, bd, cd, dskip, *, comm=None, name):
    l = proj.shape[0]
    t = min(SSM_CHUNK, l)
    nc = l // t

    def body(u_ref, ab_ref, bd_ref, cd_ref, ds_ref, y_ref, gy_ref, xs_ref, carry_ref):
        c = pl.program_id(1)

        @pl.when(c == 0)
        def _():
            carry_ref[...] = jnp.zeros_like(carry_ref)

        u = u_ref[...]
        ar, ai = ab_ref[0, 0:1, :], ab_ref[0, 1:2, :]
        bu = jnp.dot(u.astype(BF16), bd_ref[0], preferred_element_type=F32)
        rows = lax.broadcasted_iota(jnp.int32, (t, 1), 0)
        cr, ci = carry_ref[0:1, :], carry_ref[1:2, :]
        xr = bu[:, :SSM_X_BLK] + jnp.where(rows == 0, ar * cr - ai * ci, 0.0)
        xi = bu[:, SSM_X_BLK:] + jnp.where(rows == 0, ar * ci + ai * cr, 0.0)
        xr, xi = _scan_chunk(xr, xi, ar, ai, down=True)
        xs_ref[0, :, :SSM_X_BLK] = xr
        xs_ref[0, :, SSM_X_BLK:] = xi
        carry_ref[0:1, :] = xs_ref[0, t - 1:t, :SSM_X_BLK]
        carry_ref[1:2, :] = xs_ref[0, t - 1:t, SSM_X_BLK:]
        y = jnp.dot(xs_ref[0].astype(BF16), cd_ref[0], preferred_element_type=F32) + ds_ref[...] * u
        y_ref[...] = y
        gy_ref[...] = _gelu(y).astype(BF16)

    blk = lambda shape: pl.BlockSpec((1,) + shape, lambda j, c: (j, 0, 0))
    ycol = pl.BlockSpec((t, SSM_U_BLK), lambda j, c: (c, j))
    return _hosted_call(
        body, grid=(SSM_SPLIT, nc),
        in_specs=[pl.BlockSpec((t, SSM_U_BLK), lambda j, c: (c, COL_U + j)),
                  blk((2, SSM_X_BLK)), blk((SSM_U_BLK, 2 * SSM_X_BLK)), blk((2 * SSM_X_BLK, SSM_U_BLK)),
                  pl.BlockSpec((1, SSM_U_BLK), lambda j, c: (0, j))],
        out_specs=[ycol, ycol, pl.BlockSpec((1, t, 2 * SSM_X_BLK), lambda j, c: (j, c, 0))],
        out_shape=[jax.ShapeDtypeStruct((l, SSM_WIDTH), F32), jax.ShapeDtypeStruct((l, SSM_WIDTH), BF16),
                   jax.ShapeDtypeStruct((SSM_SPLIT, l, 2 * SSM_X_BLK), F32)],
        scratch_shapes=[pltpu.VMEM((2, SSM_X_BLK), F32)], sem=("parallel", "arbitrary"), name=name,
        args=(proj, ab, bd, cd, dskip), comm=comm)


def _ssm_bwd(proj, y, dgy, xs, ab, bdt, cdt, dskip, *, comm=None, name):
    l = proj.shape[0]
    t = min(SSM_CHUNK, l)
    nc = l // t

    def body(u_ref, y_ref, dgy_ref, xs_ref, halo_ref, ab_ref, bdt_ref, cdt_ref, ds_ref,
             du_ref, dbd_ref, dcd_ref, dab_ref, dd_ref, carry_ref):
        c = pl.program_id(1)
        ci_ = nc - 1 - c

        @pl.when(c == 0)
        def _():
            carry_ref[...] = jnp.zeros_like(carry_ref)
            dbd_ref[...] = jnp.zeros_like(dbd_ref)
            dcd_ref[...] = jnp.zeros_like(dcd_ref)
            dab_ref[...] = jnp.zeros_like(dab_ref)
            dd_ref[...] = jnp.zeros_like(dd_ref)

        u = u_ref[...]
        dy = dgy_ref[...] * _gelu_grad(y_ref[...])
        dyb = dy.astype(BF16)
        ar, ai = ab_ref[0, 0:1, :], ab_ref[0, 1:2, :]
        g = jnp.dot(dyb, cdt_ref[0], preferred_element_type=F32)
        rows = lax.broadcasted_iota(jnp.int32, (t, 1), 0)
        cr, ci = carry_ref[0:1, :], carry_ref[1:2, :]
        lr = g[:, :SSM_X_BLK] + jnp.where(rows == t - 1, ar * cr + ai * ci, 0.0)
        li = g[:, SSM_X_BLK:] + jnp.where(rows == t - 1, ar * ci - ai * cr, 0.0)
        lr, li = _scan_chunk(lr, li, ar, -ai, down=False)
        lam = jnp.concatenate([lr, li], axis=1)
        carry_ref[0:1, :] = lr[0:1, :]
        carry_ref[1:2, :] = li[0:1, :]
        lamb = lam.astype(BF16)
        du_ref[...] = (jnp.dot(lamb, bdt_ref[0], preferred_element_type=F32) + ds_ref[...] * dy).astype(BF16)
        dbd_ref[0] += lax.dot_general(u.astype(BF16), lamb, (((0,), (0,)), ((), ())),
                                      preferred_element_type=F32)
        xs = xs_ref[0]
        dcd_ref[0] += lax.dot_general(xs.astype(BF16), dyb, (((0,), (0,)), ((), ())),
                                      preferred_element_type=F32)
        halo = jnp.where(ci_ > 0, halo_ref[0, 7:8, :], 0.0)
        xprev = jnp.where(rows == 0, halo, pltpu.roll(xs, 1, axis=0))
        xpr, xpi = xprev[:, :SSM_X_BLK], xprev[:, SSM_X_BLK:]
        dab_ref[0, 0:1, :] += jnp.sum(lr * xpr + li * xpi, axis=0, keepdims=True)
        dab_ref[0, 1:2, :] += jnp.sum(li * xpr - lr * xpi, axis=0, keepdims=True)
        dd_ref[...] += jnp.sum(dy * u, axis=0, keepdims=True)

    blk = lambda shape: pl.BlockSpec((1,) + shape, lambda j, c: (j, 0, 0))
    rev = lambda j, c: (nc - 1 - c, j)
    ycol = pl.BlockSpec((t, SSM_U_BLK), rev)
    hb = t // 8
    return _hosted_call(
        body, grid=(SSM_SPLIT, nc), comm=comm, sem=("parallel", "arbitrary"), name=name,
        args=(proj, y, dgy, xs, xs, ab, bdt, cdt, dskip), scratch_shapes=[pltpu.VMEM((2, SSM_X_BLK), F32)],
        in_specs=[pl.BlockSpec((t, SSM_U_BLK), lambda j, c: (nc - 1 - c, COL_U + j)), ycol, ycol,
                  pl.BlockSpec((1, t, 2 * SSM_X_BLK), lambda j, c: (j, nc - 1 - c, 0)),
                  pl.BlockSpec((1, 8, 2 * SSM_X_BLK),
                               lambda j, c: (j, jnp.maximum((nc - 1 - c) * hb - 1, 0), 0)),
                  blk((2, SSM_X_BLK)), blk((2 * SSM_X_BLK, SSM_U_BLK)), blk((SSM_U_BLK, 2 * SSM_X_BLK)),
                  pl.BlockSpec((1, SSM_U_BLK), lambda j, c: (0, j))],
        out_specs=[ycol, blk((SSM_U_BLK, 2 * SSM_X_BLK)), blk((2 * SSM_X_BLK, SSM_U_BLK)),
                   blk((2, SSM_X_BLK)), pl.BlockSpec((1, SSM_U_BLK), lambda j, c: (0, j))],
        out_shape=[jax.ShapeDtypeStruct((l, SSM_WIDTH), BF16),
                   jax.ShapeDtypeStruct((SSM_SPLIT, SSM_U_BLK, 2 * SSM_X_BLK), F32),
                   jax.ShapeDtypeStruct((SSM_SPLIT, 2 * SSM_X_BLK, SSM_U_BLK), F32),
                   jax.ShapeDtypeStruct((SSM_SPLIT, 2, SSM_X_BLK), F32),
                   jax.ShapeDtypeStruct((1, SSM_WIDTH), F32)])


def _block_diag(t):
    s, g, a, b = t.shape
    return jnp.einsum('sgab,gk->sgakb', t, jnp.eye(g, dtype=t.dtype)).reshape(s, g * a, g * b)


def _block_diag_take(t, a, b):
    s = t.shape[0]
    return jnp.einsum('sgakb,gk->sgab', t.reshape(s, 8, a, 8, b), jnp.eye(8, dtype=t.dtype))


def _glu_fwd_hook(l, tm):
    def fn(result, ins, outs, i, j):
        z = result()
        outs[0][...] = z
        outs[1][...] = (z[:, :SSM_WIDTH] * _sigmoid(z[:, SSM_WIDTH:])).astype(BF16)

    row = lambda i, j, k: (i, 0)
    return _Hook(fn, outs=[((l, 2 * SSM_WIDTH), F32, (tm, 2 * SSM_WIDTH), row),
                           ((l, SSM_WIDTH), BF16, (tm, SSM_WIDTH), row)])


def _glu_bwd_hook(z, tm):
    l = z.shape[0]

    def fn(result, ins, outs, i, j):
        zv_ref, zg_ref = ins
        dz_ref, db_ref = outs
        d = result()
        sg = _sigmoid(zg_ref[...])
        dv = d * sg
        dg = d * zv_ref[...] * sg * (1.0 - sg)
        dz_ref[:, :SSM_WIDTH] = dv.astype(BF16)
        dz_ref[:, SSM_WIDTH:] = dg.astype(BF16)

        @pl.when(i == 0)
        def _():
            db_ref[...] = jnp.zeros_like(db_ref)

        db_ref[:, :SSM_WIDTH] += jnp.sum(dv, axis=0, keepdims=True)
        db_ref[:, SSM_WIDTH:] += jnp.sum(dg, axis=0, keepdims=True)

    half = (tm, SSM_WIDTH)
    return _Hook(fn, ins=[z, z], in_specs=[(half, lambda i, j, k: (i, 0)), (half, lambda i, j, k: (i, 1))],
                 outs=[((l, 2 * SSM_WIDTH), BF16, (tm, 2 * SSM_WIDTH), lambda i, j, k: (i, 0)),
                       ((1, 2 * SSM_WIDTH), F32, (1, 2 * SSM_WIDTH), lambda i, j, k: (0, 0))])


GATE_TC = 256


def _merge_fwd(proj, a, s, *, name):
    l = a.shape[0]
    tr = min(2048, l)

    def body(ga_ref, gs_ref, a_ref, s_ref, o_ref):
        o_ref[...] = (_sigmoid(ga_ref[...]) * a_ref[...] + _sigmoid(gs_ref[...]) * s_ref[...]).astype(BF16)

    own = pl.BlockSpec((tr, GATE_TC), lambda i, j: (i, j))
    return pl.pallas_call(
        body, grid=(l // tr, D_MODEL // GATE_TC),
        in_specs=[pl.BlockSpec((tr, GATE_TC), lambda i, j: (i, COL_GA // 2 + j)),
                  pl.BlockSpec((tr, GATE_TC), lambda i, j: (i, COL_GS // 2 + j)), own, own],
        out_specs=own, out_shape=jax.ShapeDtypeStruct((l, D_MODEL), BF16), name=name,
        compiler_params=_cp(("parallel", "parallel")))(proj, proj, a, s)


def _merge_bwd_hook(proj, a, s, tm):
    def fn(result, ins, outs, i, j):
        ga_ref, gs_ref, a_br, s_br = ins
        d = result()
        sa = _sigmoid(ga_ref[...])
        ss = _sigmoid(gs_ref[...])
        outs[0][...] = (d * sa).astype(BF16)
        outs[1][...] = (d * ss).astype(BF16)
        outs[2][...] = (d * a_br[...] * sa * (1.0 - sa)).astype(BF16)
        outs[3][...] = (d * s_br[...] * ss * (1.0 - ss)).astype(BF16)

    blk = (tm, GATE_TC)
    own = lambda i, j, k: (i, j)
    return _Hook(fn, ins=[proj, proj, a, s],
                 in_specs=[(blk, lambda i, j, k: (i, COL_GA // 2 + j)), (blk, lambda i, j, k: (i, COL_GS // 2 + j)),
                           (blk, own), (blk, own)],
                 outs=[(a.shape, BF16, blk, own)] * 4)


FF_TC = D_FF // 2
FF_NJ = 2
FF_ROWS = 128


FF_HALO = 16


def _conv_taps(ext, rows):
    h = FF_HALO
    return (ext[h:h + rows], pltpu.roll(ext, 1, axis=0)[h:h + rows], pltpu.roll(ext, 2, axis=0)[h:h + rows])


def _ff_specs(tr, l):
    hb = tr // FF_HALO
    last = l // FF_HALO - 1
    prev = lambda i: jnp.maximum(i * hb - 1, 0)
    nxt = lambda i: jnp.minimum((i + 1) * hb, last)
    return dict(
        own=pl.BlockSpec((tr, FF_TC), lambda j, i: (i, j)),
        own_next=pl.BlockSpec((FF_HALO, FF_TC), lambda j, i: (nxt(i), j)),
        val=pl.BlockSpec((tr, FF_TC), lambda j, i: (i, 2 * j)),
        val_next=pl.BlockSpec((FF_HALO, FF_TC), lambda j, i: (nxt(i), 2 * j)),
        gate=pl.BlockSpec((tr, FF_TC), lambda j, i: (i, 2 * j + 1)),
        gate_prev=pl.BlockSpec((FF_HALO, FF_TC), lambda j, i: (prev(i), 2 * j + 1)),
        gate_next=pl.BlockSpec((FF_HALO, FF_TC), lambda j, i: (nxt(i), 2 * j + 1)),
        pair=pl.BlockSpec((tr, 2 * FF_TC), lambda j, i: (i, j)),
        w=pl.BlockSpec((3, FF_TC), lambda j, i: (0, j)),
        b=pl.BlockSpec((1, FF_TC), lambda j, i: (0, j)))


def _ffn_act_fwd(up, conv_w, conv_b, *, name):
    l = up.shape[0]
    tr = min(FF_ROWS, l)

    def body(v_ref, g_ref, prev_ref, w_ref, b_ref, o_ref):
        prev = jnp.where(pl.program_id(1) == 0, 0.0, prev_ref[...].astype(F32))
        g0, g1, g2 = _conv_taps(jnp.concatenate([prev, g_ref[...].astype(F32)], axis=0), tr)
        gc = b_ref[...] + w_ref[0:1, :] * g2 + w_ref[1:2, :] * g1 + w_ref[2:3, :] * g0
        o_ref[...] = (v_ref[...].astype(F32) * _gelu(gc)).astype(BF16)

    sp = _ff_specs(tr, l)
    return pl.pallas_call(
        body, grid=(FF_NJ, l // tr), in_specs=[sp['val'], sp['gate'], sp['gate_prev'], sp['w'], sp['b']],
        out_specs=sp['own'], out_shape=jax.ShapeDtypeStruct((l, D_FF), BF16), name=name,
        compiler_params=_cp(("parallel", "parallel")))(up, up, up, conv_w, conv_b)


def _ffn_act_bwd(dact, up, conv_w, conv_b, *, comm=None, name):
    l = up.shape[0]
    tr = min(FF_ROWS, l)
    ni = l // tr
    te = tr + 8

    def body(d_ref, dn_ref, v_ref, vn_ref, g_ref, gp_ref, gn_ref, w_ref, b_ref, dup_ref, dw_ref, db_ref):
        i = pl.program_id(1)
        f32 = lambda ref, rows=None: ref[...].astype(F32)[:rows]
        prev = jnp.where(i == 0, 0.0, f32(gp_ref))
        g0, g1, g2 = _conv_taps(jnp.concatenate([prev, f32(g_ref), f32(gn_ref, 8)], axis=0), te)
        w0, w1, w2 = w_ref[0:1, :], w_ref[1:2, :], w_ref[2:3, :]
        gc = b_ref[...] + w0 * g2 + w1 * g1 + w2 * g0
        d_own = f32(d_ref)
        d = jnp.concatenate([d_own, jnp.where(i == ni - 1, 0.0, f32(dn_ref, 8))], axis=0)
        v = jnp.concatenate([f32(v_ref), f32(vn_ref, 8)], axis=0)
        dgc = d * v * _gelu_grad(gc)
        ahead1 = pltpu.roll(dgc, te - 1, axis=0)[:tr]
        ahead2 = pltpu.roll(dgc, te - 2, axis=0)[:tr]
        own = dgc[:tr]
        dup_ref[:, :FF_TC] = (d_own * _gelu(gc[:tr])).astype(BF16)
        dup_ref[:, FF_TC:] = (w2 * own + w1 * ahead1 + w0 * ahead2).astype(BF16)

        @pl.when(i == 0)
        def _():
            dw_ref[...] = jnp.zeros_like(dw_ref)
            db_ref[...] = jnp.zeros_like(db_ref)

        dw_ref[0:1, :] += jnp.sum(own * g2[:tr], axis=0, keepdims=True)
        dw_ref[1:2, :] += jnp.sum(own * g1[:tr], axis=0, keepdims=True)
        dw_ref[2:3, :] += jnp.sum(own * g0[:tr], axis=0, keepdims=True)
        db_ref[...] += jnp.sum(own, axis=0, keepdims=True)

    sp = _ff_specs(tr, l)
    return _hosted_call(
        body, grid=(FF_NJ, ni),
        in_specs=[sp['own'], sp['own_next'], sp['val'], sp['val_next'], sp['gate'], sp['gate_prev'],
                  sp['gate_next'], sp['w'], sp['b']],
        out_specs=[sp['pair'], sp['w'], sp['b']],
        out_shape=[jax.ShapeDtypeStruct((l, 2 * D_FF), BF16), jax.ShapeDtypeStruct((3, D_FF), F32),
                   jax.ShapeDtypeStruct((1, D_FF), F32)],
        scratch_shapes=[], sem=("parallel", "arbitrary"), name=name,
        args=(dact, dact, up, up, up, up, up, conv_w, conv_b), comm=comm)


def _col_sum(a, *, name):
    l, n = a.shape
    tr = min(512, l)

    def body(a_ref, o_ref):
        @pl.when(pl.program_id(0) == 0)
        def _():
            o_ref[...] = jnp.zeros_like(o_ref)

        o_ref[...] += jnp.sum(a_ref[...].astype(F32), axis=0, keepdims=True)

    return pl.pallas_call(
        body, grid=(l // tr,), in_specs=[pl.BlockSpec((tr, n), lambda i: (i, 0))],
        out_specs=pl.BlockSpec((1, n), lambda i: (0, 0)), out_shape=jax.ShapeDtypeStruct((1, n), F32),
        name=name, compiler_params=_cp(("arbitrary",)))(a)


def _local_step(x, target, wts, small, shards=None):
    l = x.shape[0]
    wts = dict(wts)
    grads, recvs, sgr = {}, {}, {}
    lay = lambda keys: [LAYOUT[k] for k in keys]
    none = lambda keys: None
    gather = (lambda keys: _GatherPlan([shards[k] for k in keys], lay(keys))) if shards is not None else none
    scatter = (lambda keys: _ScatterPlan([grads[k] for k in keys], lay(keys))) if shards is not None else none

    mm = _matmul

    def take(res, plan, keys, store):
        outs, couts = res
        if plan is not None:
            store.update(zip(keys, couts))
        return outs

    def mm_plan(plan, keys, store, *args, **kw):
        if plan is None:
            return _matmul(*args, **kw)
        return take(_matmul(*args, comm=plan, **kw), plan, keys, store)

    def mm_host(keys, make_plan, store, *args, **kw):
        return mm_plan(make_plan(keys), keys, store, *args, **kw)

    up_scatter = lambda p: _ScatterPlan([grads['w_up']], lay(['w_up']), part=(p, 2)) if shards is not None else None

    col = lambda t: t.reshape(SSM_GROUPS * SSM_STATE, 1)
    a_re, a_im = col(small['ssm_a_re']), col(small['ssm_a_im'])
    log_dt = jnp.repeat(small['ssm_log_dt'].reshape(SSM_GROUPS), SSM_STATE).reshape(-1, 1)
    b_re = small['ssm_b_re'].reshape(SSM_GROUPS * SSM_STATE, SSM_GROUP)
    b_im = small['ssm_b_im'].reshape(SSM_GROUPS * SSM_STATE, SSM_GROUP)
    ab_re, ab_im, bb_re, bb_im = _ssm_disc_fwd(a_re, a_im, log_dt, b_re, b_im, name="ssm_disc_fwd")
    ab = jnp.stack([ab_re.reshape(SSM_SPLIT, SSM_X_BLK), ab_im.reshape(SSM_SPLIT, SSM_X_BLK)], axis=1)
    to_bd = lambda t: _block_diag(t.reshape(SSM_SPLIT, 8, SSM_STATE, SSM_GROUP).transpose(0, 1, 3, 2))
    bd = jnp.concatenate([to_bd(bb_re), to_bd(bb_im)], axis=2)
    c_re = small['ssm_c_re'].reshape(SSM_SPLIT, 8, SSM_GROUP, SSM_STATE)
    c_im = small['ssm_c_im'].reshape(SSM_SPLIT, 8, SSM_GROUP, SSM_STATE)
    cdt = jnp.concatenate([_block_diag(c_re), -_block_diag(c_im)], axis=2)
    bd_b, cdt_b = bd.astype(BF16), cdt.astype(BF16)
    bdt_b, cd_b = bd_b.transpose(0, 2, 1), cdt_b.transpose(0, 2, 1)
    dskip = small['ssm_d'].reshape(1, SSM_WIDTH)

    sinks = small['attn_sinks'].reshape(N_Q_HEADS)
    plan = gather(['w_in_t'])
    h1, = take(_rms_fwd(x, small['attn_norm_g'], comm=plan, name="rms1_fwd"), plan, ['w_in_t'], wts)
    proj = mm_host(['w_glu', 'w_ba', 'w_bs', 'w_out'], gather, wts,
                   h1, wts['w_in_t'], tb=True, tm=512, tn=2944, tk=2048, inner='m', out_dtype=F32,
                   bias=small['b_in'], name="mm_in")
    attn = _attn_fwd(proj, sinks, name="attn_fwd")
    plan = gather(['w_up'])
    y, gy, xs = take(_ssm_fwd(proj, ab, bd_b, cd_b, dskip, comm=plan, name="ssm_fwd"), plan, ['w_up'], wts)
    z, ssm = mm(gy, wts['w_glu'], tm=1024, tn=1024, tk=512, bias=small['b_glu'],
                epilogue=_glu_fwd_hook(l, min(1024, l)), name="mm_glu")
    a_br = mm(attn, wts['w_ba'], tm=1024, tn=1024, tk=1024, out_dtype=F32, name="mm_ba")
    s_br = mm(ssm, wts['w_bs'], tm=1024, tn=1024, tk=512, out_dtype=F32, name="mm_bs")
    tr = min(ROW_TILE, l)
    merged = _merge_fwd(proj, a_br, s_br, name="merge_fwd")
    x2 = mm(merged, wts['w_out'], tm=1024, tn=1024, tk=2048, inner='m', out_dtype=F32, res=x, name="mm_out")
    h2, = take(_rms_fwd(x2, small['ffn_norm_g'], name="rms2_fwd"), None, [], wts)
    up = mm_host(['w_down'], gather, wts,
                 h2, wts['w_up'], tm=1024, tn=1024, tk=2048, out_dtype=BF16, name="mm_up")
    conv_w, conv_b = small['conv_w'], small['conv_b']
    act = _ffn_act_fwd(up, conv_w, conv_b, name="ffn_act_fwd")
    x3 = mm(act, wts['w_down'], tm=1024, tn=1024, tk=2816, out_dtype=F32, res=x2, name="mm_down")
    dx3b, d_g3, loss = _rowwise(
        _final_loss_hook(small['final_norm_g'].reshape(1, D_MODEL), target, tr), x3, tr, name="final_loss")

    sgr['final_norm_g'] = d_g3.reshape(D_MODEL)
    dact = mm(dx3b, wts['w_down'], tb=True, tm=512, tn=2816, tk=2048, inner='m', out_dtype=BF16, name="mm_dact")
    grads['w_down'] = mm(act, dx3b, ta=True, tm=512, tn=1024, tk=2048, out_dtype=BF16, name="mm_dw_down")
    plan = scatter(['w_down'])
    dup, sgr['conv_w'], sgr['conv_b'] = take(
        _ffn_act_bwd(dact, up, conv_w, conv_b, comm=plan, name="ffn_act_bwd"), plan, ['w_down'], recvs)
    grads['w_up'] = mm(h2, dup, ta=True, tm=1024, tn=1024, tk=2048, out_dtype=BF16, name="mm_dw_up")
    dh2 = mm_plan(up_scatter(0), ['w_up#0'], recvs,
                  dup, wts['w_up'], tb=True, tm=1024, tn=1024, tk=2816, out_dtype=F32, name="mm_dh2")
    dx2b, sgr['ffn_norm_g'] = _rowwise(
        _rms_bwd_hook(x2, small['ffn_norm_g'], dx3b, tr, BF16), dh2, tr, name="rms2_bwd")

    d_a, d_s, dga, dgs = mm(dx2b, wts['w_out'], tb=True, tm=1024, tn=GATE_TC, tk=2048,
                            epilogue=_merge_bwd_hook(proj, a_br, s_br, min(1024, l)), name="mm_dmerged")
    grads['w_out'] = mm(merged, dx2b, ta=True, tm=1024, tn=1024, tk=2048, out_dtype=BF16, name="mm_dw_out")
    dattn = mm(d_a, wts['w_ba'], tb=True, tm=1024, tn=1024, tk=2048, inner='m', out_dtype=BF16, name="mm_dattn")
    grads['w_ba'] = mm(attn, d_a, ta=True, tm=1024, tn=1024, tk=2048, out_dtype=BF16, name="mm_dw_ba")
    dz, sgr['b_glu'] = mm(d_s, wts['w_bs'], tb=True, tm=1024, tn=512, tk=2048, sequential=True,
                          epilogue=_glu_bwd_hook(z, min(1024, l)), name="mm_dssm")
    grads['w_bs'] = mm(ssm, d_s, ta=True, tm=512, tn=1024, tk=2048, out_dtype=BF16, name="mm_dw_bs")
    grads['w_glu'] = mm(gy, dz, ta=True, tm=512, tn=1024, tk=2048, out_dtype=BF16, name="mm_dw_glu")
    dgy = mm(dz, wts['w_glu'], tb=True, tm=1024, tn=512, tk=1024, inner='m', out_dtype=F32, name="mm_dgy")
    plan = up_scatter(1)
    du, d_bd, d_cd, d_ab, sgr['ssm_d'] = take(
        _ssm_bwd(proj, y, dgy, xs, ab, bdt_b, cdt_b, dskip, comm=plan, name="ssm_bwd"), plan, ['w_up#1'], recvs)
    keys = ['w_out', 'w_ba', 'w_bs', 'w_glu']
    plan = scatter(keys)
    dq, dkc, dkp, dvc, dvp, dsink = take(
        _attn_bwd(proj, sinks, dattn, comm=plan, name="attn_bwd"), plan, keys, recvs)
    dkv = _kv_grad_merge(dkc, dkp, dvc, dvp, name="kv_grad_merge")
    sgr['attn_sinks'] = dsink[:, :N_Q_HEADS]
    dproj = jnp.concatenate([dq, dkv, du, dga, dgs], axis=1)
    sgr['b_in'] = _col_sum(dproj, name="col_sum_dproj")
    grads['w_in_t'] = mm(dproj, h1, ta=True, tm=2944, tn=1024, tk=1024, out_dtype=BF16, name="mm_dw_in")
    dh1 = mm_host(['w_in_t'], scatter, recvs,
                  dproj, wts['w_in_t'], tm=1024, tn=1024, tk=2944, out_dtype=F32, name="mm_dh1")
    grad_x, sgr['attn_norm_g'] = _rowwise(
        _rms_bwd_hook(x, small['attn_norm_g'], dx2b, tr, F32), dh1, tr, name="rms1_bwd")

    from_bd = lambda t: _block_diag_take(t, SSM_GROUP, SSM_STATE).transpose(0, 1, 3, 2).reshape(
        SSM_GROUPS * SSM_STATE, SSM_GROUP)
    d_bb_re = from_bd(d_bd[:, :, :SSM_X_BLK])
    d_bb_im = from_bd(d_bd[:, :, SSM_X_BLK:])
    d_cdt = d_cd.transpose(0, 2, 1)
    shape_c = (1, SSM_GROUPS, SSM_GROUP, SSM_STATE)
    sgr['ssm_c_re'] = _block_diag_take(d_cdt[:, :, :SSM_X_BLK], SSM_GROUP, SSM_STATE).reshape(shape_c)
    sgr['ssm_c_im'] = -_block_diag_take(d_cdt[:, :, SSM_X_BLK:], SSM_GROUP, SSM_STATE).reshape(shape_c)
    d_a_re, d_a_im, d_ldt, d_b_re, d_b_im = _ssm_disc_bwd(
        a_re, a_im, log_dt, b_re, b_im, d_ab[:, 0, :].reshape(-1, 1), d_ab[:, 1, :].reshape(-1, 1),
        d_bb_re, d_bb_im, name="ssm_disc_bwd")
    sgr['ssm_a_re'] = d_a_re.reshape(1, SSM_GROUPS, SSM_STATE)
    sgr['ssm_a_im'] = d_a_im.reshape(1, SSM_GROUPS, SSM_STATE)
    sgr['ssm_log_dt'] = d_ldt.reshape(SSM_GROUPS, SSM_STATE).sum(axis=1).reshape(1, SSM_GROUPS)
    sgr['ssm_b_re'] = d_b_re.reshape(1, SSM_GROUPS, SSM_STATE, SSM_GROUP)
    sgr['ssm_b_im'] = d_b_im.reshape(1, SSM_GROUPS, SSM_STATE, SSM_GROUP)
    return loss, grad_x, grads, recvs, sgr


def _swap_cores(arrs, *, name):
    n = len(arrs)

    def body(*refs):
        ins, outs = refs[:n], refs[n:2 * n]
        send_sems, recv_sems = refs[2 * n:]
        x, y, c = _place()
        copies = []
        for i in range(n):
            cp = pltpu.make_async_remote_copy(
                src_ref=ins[i], dst_ref=outs[i], send_sem=send_sems.at[i], recv_sem=recv_sems.at[i],
                device_id=(x, y, 1 - c), device_id_type=MESH)
            cp.start()
            copies.append(cp)
        for cp in copies:
            cp.wait()

    return pl.pallas_call(
        body, in_specs=[ANY] * n, out_specs=[ANY] * n,
        out_shape=[jax.ShapeDtypeStruct(a.shape, a.dtype) for a in arrs],
        scratch_shapes=[pltpu.SemaphoreType.DMA((n,)), pltpu.SemaphoreType.DMA((n,))],
        name=name)(*arrs)


def _all_reduce_small(buf, *, name):
    r = buf.shape[0]

    def body(in_ref, out_ref, slots, send_sems, recv_sems):
        x, y, c = _place()
        me = 4 * x + 2 * y + c
        slots[pl.ds(me, 1)] = in_ref[...][None]
        copies = []
        for k in range(N_DEV - 1):
            bx, by, bc = ((k + 1) >> 2) & 1, ((k + 1) >> 1) & 1, (k + 1) & 1
            peer = (1 - x if bx else x, 1 - y if by else y, 1 - c if bc else c)
            cp = pltpu.make_async_remote_copy(
                src_ref=in_ref, dst_ref=slots.at[me], send_sem=send_sems.at[k], recv_sem=recv_sems.at[k],
                device_id=peer, device_id_type=MESH)
            cp.start()
            copies.append(cp)
        for cp in copies:
            cp.wait()
        acc = slots[0]
        for d in range(1, N_DEV):
            acc = acc + slots[d]
        out_ref[...] = acc

    vm = pl.BlockSpec(memory_space=pltpu.VMEM)
    return pl.pallas_call(
        body, in_specs=[vm], out_specs=vm, out_shape=jax.ShapeDtypeStruct((r, 128), F32),
        scratch_shapes=[pltpu.VMEM((N_DEV, r, 128), F32), pltpu.SemaphoreType.DMA((N_DEV - 1,)),
                        pltpu.SemaphoreType.DMA((N_DEV - 1,))],
        name=name)(buf)


def _pack(arrs):
    flat = jnp.concatenate([a.reshape(-1).astype(F32) for a in arrs])
    pad = (-flat.shape[0]) % 1024
    return jnp.pad(flat, (0, pad)).reshape(-1, 128)


def _unpack(buf, shapes):
    flat = buf.reshape(-1)
    out, pos = [], 0
    for s in shapes:
        size = math.prod(s)
        out.append(flat[pos:pos + size].reshape(s))
        pos += size
    return out


TILE_ELEMS = 256 * 1024


def _tile_rows(r, c):
    if r * c <= TILE_ELEMS:
        return r
    for tr in range(TILE_ELEMS // c // 16 * 16, 0, -16):
        if r % tr == 0:
            return tr
    raise ValueError((r, c))


def _sum4(own, recvs, *, name):
    r, c = own.shape
    parts = len(recvs)
    tr = _tile_rows(r // parts, c)
    per = r // parts // tr

    def body(o_ref, *refs):
        out_ref = refs[parts]
        for p in range(parts):
            @pl.when(pl.program_id(0) // per == p)
            def _():
                acc = o_ref[...].astype(F32)
                for k in range(3):
                    acc = acc + refs[p][k].astype(F32)
                out_ref[...] = acc.astype(BF16)

    part_spec = lambda p: pl.BlockSpec((3, tr, c), lambda i: (0, jnp.clip(i - p * per, 0, per - 1), 0))
    return pl.pallas_call(
        body, grid=(r // tr,),
        in_specs=[pl.BlockSpec((tr, c), lambda i: (i, 0))] + [part_spec(p) for p in range(parts)],
        out_specs=pl.BlockSpec((tr, c), lambda i: (i, 0)), out_shape=jax.ShapeDtypeStruct((r, c), BF16),
        name=name, compiler_params=_cp(("parallel",)))(own, *recvs)


def _adamw(w, ga, gb, m, v, *, name):
    r, c = w.shape
    tr = _tile_rows(r, c)
    bc1 = 1.0 - ADAM_B1 ** ADAM_STEP
    bc2 = 1.0 - ADAM_B2 ** ADAM_STEP
    two = gb is not None

    def body(*refs):
        w_ref, ga_ref = refs[0], refs[1]
        pos = 2
        g = ga_ref[...].astype(F32)
        if two:
            g = g + refs[pos][...].astype(F32)
            pos += 1
        m_ref, v_ref, g_out, d_out, m_out, v_out = refs[pos:pos + 6]
        mn = ADAM_B1 * m_ref[...] + (1.0 - ADAM_B1) * g
        vn = ADAM_B2 * v_ref[...] + (1.0 - ADAM_B2) * (g * g)
        m_hat = mn / bc1
        v_hat = vn / bc2
        g_out[...] = g
        d_out[...] = -ADAM_LR * (m_hat / (jnp.sqrt(v_hat) + ADAM_EPS) + ADAM_WD * w_ref[...])
        m_out[...] = mn
        v_out[...] = vn

    spec = pl.BlockSpec((tr, c), lambda i: (i, 0))
    args = [w, ga] + ([gb] if two else []) + [m, v]
    shp = jax.ShapeDtypeStruct((r, c), F32)
    return pl.pallas_call(
        body, grid=(r // tr,), in_specs=[spec] * len(args), out_specs=[spec] * 4,
        out_shape=[shp] * 4, name=name, compiler_params=_cp(("parallel",)))(*args)


BIG = ['w_in', 'w_glu', 'w_branch_attn', 'w_branch_ssm', 'w_out', 'w_up', 'w_down']
BIG_KEY = {'w_in': 'w_in_t', 'w_glu': 'w_glu', 'w_branch_attn': 'w_ba', 'w_branch_ssm': 'w_bs',
           'w_out': 'w_out', 'w_up': 'w_up', 'w_down': 'w_down'}
TRANSPOSED = {'w_in'}
SMALL = ['attn_norm_g', 'b_in', 'attn_sinks', 'ssm_a_re', 'ssm_a_im', 'ssm_log_dt', 'ssm_b_re', 'ssm_b_im',
         'ssm_c_re', 'ssm_c_im', 'ssm_d', 'b_glu', 'ffn_norm_g', 'conv_b', 'final_norm_g']
WEIGHTS = ['attn_norm_g', 'w_in', 'b_in', 'attn_sinks', 'ssm_a_re', 'ssm_a_im', 'ssm_log_dt', 'ssm_b_re',
           'ssm_b_im', 'ssm_c_re', 'ssm_c_im', 'ssm_d', 'w_glu', 'b_glu', 'w_branch_attn', 'w_branch_ssm',
           'w_out', 'ffn_norm_g', 'w_up', 'conv_w', 'conv_b', 'w_down', 'final_norm_g']


def _shard_2d(name, t):
    t = t[0]
    return t.T if name in TRANSPOSED else t


def _unshard_2d(name, t):
    return (t.T if name in TRANSPOSED else t)[None]


def kernel(x, attn_norm_g, w_in, b_in, attn_sinks, ssm_a_re, ssm_a_im, ssm_log_dt, ssm_b_re, ssm_b_im, ssm_c_re, ssm_c_im, ssm_d, w_glu, b_glu, w_branch_attn, w_branch_ssm, w_out, ffn_norm_g, w_up, conv_w, conv_b, w_down, final_norm_g, loss_target, m_attn_norm_g, m_w_in, m_b_in, m_attn_sinks, m_ssm_a_re, m_ssm_a_im, m_ssm_log_dt, m_ssm_b_re, m_ssm_b_im, m_ssm_c_re, m_ssm_c_im, m_ssm_d, m_w_glu, m_b_glu, m_w_branch_attn, m_w_branch_ssm, m_w_out, m_ffn_norm_g, m_w_up, m_conv_w, m_conv_b, m_w_down, m_final_norm_g, v_attn_norm_g, v_w_in, v_b_in, v_attn_sinks, v_ssm_a_re, v_ssm_a_im, v_ssm_log_dt, v_ssm_b_re, v_ssm_b_im, v_ssm_c_re, v_ssm_c_im, v_ssm_d, v_w_glu, v_b_glu, v_w_branch_attn, v_w_branch_ssm, v_w_out, v_ffn_norm_g, v_w_up, v_conv_w, v_conv_b, v_w_down, v_final_norm_g):
    args = dict(locals())
    w = {n: args[n] for n in WEIGHTS}
    m = {n: args['m_' + n] for n in WEIGHTS}
    v = {n: args['v_' + n] for n in WEIGHTS}
    xi, yi, ci = _place()
    blk = 2 * xi + yi

    shards = {BIG_KEY[n]: _shard_2d(n, w[n]).astype(BF16) for n in BIG}
    cw_cols = w['conv_w'].shape[2]
    cw_place = lax.dynamic_update_slice(jnp.zeros((3, D_FF), F32), w['conv_w'][0] * (ci == 0).astype(F32),
                                        (0, blk * cw_cols))
    conv_w_full = _unpack(_all_reduce_small(_pack([cw_place]), name="gather_conv_w"), [(3, D_FF)])[0]

    small = {n: w[n] for n in SMALL}
    small['conv_w'] = conv_w_full
    loss_part, grad_x, grads, recvs, sgr = _local_step(x[0], loss_target[0], {}, small, shards)

    halves = []
    for n in BIG:
        key = BIG_KEY[n]
        full = grads[key]
        recv = [recvs[key]] if key in recvs else [recvs[key + '#0'], recvs[key + '#1']]
        axis, interleaved = LAYOUT[key]
        size = full.shape[axis] // N_CHIPS
        own = lax.dynamic_slice_in_dim(full, _block_pos(xi, yi, interleaved) * size, size, axis=axis)
        halves.append(_sum4(own, recv, name="sum4_" + n))
    others = _swap_cores(halves, name="swap_cores")
    out = {}
    for n, mine, other in zip(BIG, halves, others):
        res = _adamw(_shard_2d(n, w[n]), mine, other, _shard_2d(n, m[n]), _shard_2d(n, v[n]), name="adamw_" + n)
        out[n] = [_unshard_2d(n, t) for t in res]

    names = SMALL + ['conv_w']
    shapes = [w[n].shape for n in SMALL] + [(3, D_FF)]
    packed = _pack([sgr[n] for n in names] + [loss_part])
    summed = _unpack(_all_reduce_small(packed, name="all_reduce_small"), shapes + [(1, 1)])
    loss = summed[-1].reshape(())
    sg = dict(zip(names, summed[:-1]))
    sg['conv_w'] = lax.dynamic_slice_in_dim(sg['conv_w'], blk * cw_cols, cw_cols, axis=1)[None]
    res = _adamw(_pack([w[n] for n in names]), _pack([sg[n] for n in names]), None,
                 _pack([m[n] for n in names]), _pack([v[n] for n in names]), name="adamw_small")
    ushapes = [w[n].shape for n in names]
    unpacked = [_unpack(t, ushapes) for t in res]
    for i, n in enumerate(names):
        out[n] = [u[i] for u in unpacked]

    return (loss, grad_x[None], *[out[n][0] for n in WEIGHTS], *[out[n][1] for n in WEIGHTS],
            *[out[n][2] for n in WEIGHTS], *[out[n][3] for n in WEIGHTS])
```

```python
import functools
import math

import jax
import jax.numpy as jnp
from jax import lax
from jax.experimental import pallas as pl
from jax.experimental.pallas import tpu as pltpu

F32 = jnp.float32
BF16 = jnp.bfloat16

D_MODEL = 2048
N_Q_HEADS = 16
HEAD_DIM = 64
ATTN_WIDTH = 1024
KV_WIDTH = 128
BLOCK = 128
SSM_WIDTH = 512
SSM_GROUPS = 32
SSM_GROUP = 16
SSM_STATE = 64
D_FF = 5632
IN_COLS = 5888
RMS_EPS = 1e-6
NEG_BIG = -1e30
N_CHIPS = 4
N_DEV = 8

COL_K = 8
COL_V = 9
COL_U = 10
COL_GA = 14
COL_GS = 30

SSM_SPLIT = 4
SSM_U_BLK = 128
SSM_X_BLK = 512
SSM_CHUNK = 256

ADAM_LR = 0.001
ADAM_B1 = 0.9
ADAM_B2 = 0.999
ADAM_EPS = 1e-08
ADAM_WD = 0.01
ADAM_STEP = 10

VMEM_LIMIT_BYTES = 56 * 1024 * 1024
INV_SQRT2 = 1.0 / math.sqrt(2.0)
INV_SQRT2PI = 1.0 / math.sqrt(2.0 * math.pi)
MESH = pl.DeviceIdType.MESH
ANY = pl.BlockSpec(memory_space=pl.ANY)


def _cp(sem):
    return pltpu.CompilerParams(dimension_semantics=sem, vmem_limit_bytes=VMEM_LIMIT_BYTES)


def _gelu(x):
    return 0.5 * x * (1.0 + lax.erf(x * INV_SQRT2))


def _gelu_grad(x):
    return 0.5 * (1.0 + lax.erf(x * INV_SQRT2)) + x * jnp.exp(-0.5 * x * x) * INV_SQRT2PI


def _sigmoid(x):
    return 1.0 / (1.0 + jnp.exp(-x))


def _place():
    return lax.axis_index("x"), lax.axis_index("y"), lax.axis_index("c")


def _other_chips(x, y):
    return [(1 - x, y), (x, 1 - y), (1 - x, 1 - y)]


def _block_pos(x, y, interleaved):
    return x + 2 * y if interleaved else 2 * x + y


LAYOUT = {'w_in_t': (0, False), 'w_glu': (1, False), 'w_ba': (1, False), 'w_bs': (1, False),
          'w_out': (0, False), 'w_up': (1, True), 'w_down': (0, False)}


def _window(ref, axis, pos, size, rows=None):
    if axis == 0:
        start, count = (0, size) if rows is None else rows
        return ref.at[pl.ds(pos * size + start, count), :]
    cols = pl.ds(pos * size, size)
    return ref.at[:, cols] if rows is None else ref.at[pl.ds(rows[0], rows[1]), cols]


def _gathered_shape(shape, axis):
    return tuple(N_CHIPS * d if a == axis else d for a, d in enumerate(shape))


def _block_shape(shape, axis):
    return tuple(d // N_CHIPS if a == axis else d for a, d in enumerate(shape))


class _GatherPlan:
    def __init__(self, shards, layouts):
        self.arrays = list(shards)
        self.layouts = list(layouts)
        n = len(shards)
        self.out_shape = [jax.ShapeDtypeStruct(_gathered_shape(s.shape, lay[0]), s.dtype)
                          for s, lay in zip(shards, layouts)]
        self.scratch = [pltpu.SemaphoreType.DMA((6 * n,)), pltpu.SemaphoreType.DMA((6 * n,)),
                        pltpu.SemaphoreType.DMA((n,))]

    def _copies(self, kind, ins, outs, sems):
        send, recv, local = sems
        n = len(self.arrays)
        x, y, c = _place()
        copies = []
        for i in range(n):
            axis, interleaved = self.layouts[i]
            size = self.arrays[i].shape[axis]
            h = self.arrays[i].shape[0] // 2
            first = lambda core: core * h
            blk = _block_pos(x, y, interleaved)
            if kind == 'mine':
                copies.append(pltpu.make_async_copy(ins[i], _window(outs[i], axis, blk, size), local.at[i]))
                continue
            for k, (px, py) in enumerate(_other_chips(x, y)):
                theirs = _block_pos(px, py, interleaved)
                if kind in ('ici_out', 'ici_in'):
                    route = dict(send_sem=send.at[3 * i + k], recv_sem=recv.at[3 * i + k],
                                 device_id=(px, py, c), device_id_type=MESH)
                else:
                    route = dict(send_sem=send.at[3 * (n + i) + k], recv_sem=recv.at[3 * (n + i) + k],
                                 device_id=(x, y, 1 - c), device_id_type=MESH)
                if kind == 'ici_out':
                    src, dst = ins[i].at[pl.ds(first(c), h), :], _window(outs[i], axis, blk, size, (first(c), h))
                elif kind == 'd2d_in':
                    src = dst = _window(outs[i], axis, theirs, size, (first(1 - c), h))
                else:
                    src = dst = _window(outs[i], axis, theirs, size, (first(c), h))
                copies.append(pltpu.make_async_remote_copy(src_ref=src, dst_ref=dst, **route))
        return copies

    def start(self, ins, outs, sems):
        for cp in self._copies('mine', ins, outs, sems) + self._copies('ici_out', ins, outs, sems):
            cp.start()

    def middle(self, ins, outs, sems):
        for arrived, onward in zip(self._copies('ici_in', ins, outs, sems), self._copies('d2d_out', ins, outs, sems)):
            arrived.wait_recv()
            onward.start()

    def finish(self, ins, outs, sems):
        for cp in self._copies('d2d_in', ins, outs, sems):
            cp.wait_recv()
        for cp in self._copies('ici_out', ins, outs, sems) + self._copies('d2d_out', ins, outs, sems):
            cp.wait_send()
        for cp in self._copies('mine', ins, outs, sems):
            cp.wait()


class _ScatterPlan:
    def __init__(self, fulls, layouts, part=(0, 1)):
        self.arrays = list(fulls)
        self.layouts = list(layouts)
        self.part = part
        n = len(fulls)
        self.out_shape = []
        for f, lay in zip(fulls, layouts):
            rows, cols = _block_shape(f.shape, lay[0])
            self.out_shape.append(jax.ShapeDtypeStruct((3, rows // part[1], cols), f.dtype))
        self.scratch = [pltpu.SemaphoreType.DMA((3 * n,)), pltpu.SemaphoreType.DMA((3 * n,))]

    def _copies(self, ins, outs, sems):
        send, recv = sems
        x, y, c = _place()
        copies = []
        for i in range(len(self.arrays)):
            axis, interleaved = self.layouts[i]
            size = self.arrays[i].shape[axis] // N_CHIPS
            h = _block_shape(self.arrays[i].shape, axis)[0] // self.part[1]
            rows = (self.part[0] * h, h)
            for k, (px, py) in enumerate(_other_chips(x, y)):
                copies.append(pltpu.make_async_remote_copy(
                    src_ref=_window(ins[i], axis, _block_pos(px, py, interleaved), size, rows), dst_ref=outs[i].at[k],
                    send_sem=send.at[3 * i + k], recv_sem=recv.at[3 * i + k],
                    device_id=(px, py, c), device_id_type=MESH))
        return copies

    def start(self, ins, outs, sems):
        for cp in self._copies(ins, outs, sems):
            cp.start()

    def middle(self, ins, outs, sems):
        pass

    def finish(self, ins, outs, sems):
        for cp in self._copies(ins, outs, sems):
            cp.wait()


def _hosted_call(body, *, grid, in_specs, out_specs, out_shape, scratch_shapes, sem, name, args, comm=None,
                 aliases=None):
    aliases = aliases or {}
    if comm is None:
        outs = pl.pallas_call(body, grid=grid, in_specs=in_specs, out_specs=out_specs, out_shape=out_shape,
                              scratch_shapes=scratch_shapes, name=name, input_output_aliases=aliases,
                              compiler_params=_cp(sem))(*args)
        return outs, None
    n_in, n_out, n_scr = len(in_specs), len(out_specs), len(scratch_shapes)
    nc, ns = len(comm.arrays), len(comm.scratch)
    total = math.prod(grid)
    mid = total - max(1, total // 8)

    def wrapped(*refs):
        pos = 0
        ins = refs[pos:pos + n_in]; pos += n_in
        cins = refs[pos:pos + nc]; pos += nc
        outs = refs[pos:pos + n_out]; pos += n_out
        couts = refs[pos:pos + nc]; pos += nc
        scr = refs[pos:pos + n_scr]; pos += n_scr
        sems = refs[pos:pos + ns]
        step = 0
        for ax, g in enumerate(grid):
            step = step * g + pl.program_id(ax)

        @pl.when(step == 0)
        def _():
            comm.start(cins, couts, sems)

        body(*ins, *outs, *scr)

        @pl.when(step == mid)
        def _():
            comm.middle(cins, couts, sems)

        @pl.when(step == total - 1)
        def _():
            comm.finish(cins, couts, sems)

    res = pl.pallas_call(
        wrapped, grid=grid, in_specs=list(in_specs) + [ANY] * nc, out_specs=list(out_specs) + [ANY] * nc,
        out_shape=list(out_shape) + list(comm.out_shape), scratch_shapes=list(scratch_shapes) + list(comm.scratch),
        name=name, input_output_aliases=aliases,
        compiler_params=_cp(("arbitrary",) * len(grid)))(*args, *comm.arrays)
    return res[:n_out], res[n_out:]


class _Hook:
    def __init__(self, fn, ins=(), in_specs=(), outs=()):
        self.fn, self.ins, self.in_specs, self.outs = fn, list(ins), list(in_specs), list(outs)


def _matmul(a, b, *, ta=False, tb=False, tm, tn, tk, out_dtype=None, bias=None, res=None, inner='n',
            comm=None, prologue=None, epilogue=None, a_shape=None, sequential=False, name):
    if a is None:
        m, kdim = a_shape
    elif ta:
        kdim, m = a.shape
    else:
        m, kdim = a.shape
    if tb:
        n, k2 = b.shape
    else:
        k2, n = b.shape
    assert kdim == k2, (name, kdim, b.shape)
    tm, tn, tk = min(tm, m), min(tn, n), min(tk, kdim)
    assert m % tm == 0 and n % tn == 0 and kdim % tk == 0, (name, m, n, kdim, tm, tn, tk)
    nk = kdim // tk
    dn = (((0 if ta else 1,), (1 if tb else 0,)), ((), ()))
    hooks = [h for h in (prologue, epilogue) if h is not None]
    n_pro_in = len(prologue.ins) if prologue else 0
    n_epi_in = len(epilogue.ins) if epilogue else 0
    n_pro_out = len(prologue.outs) if prologue else 0
    n_epi_out = len(epilogue.outs) if epilogue else 0
    if inner == 'n':
        grid = (m // tm, n // tn, nk)
        mi = lambda g0, g1: g0
        ni = lambda g0, g1: g1
    else:
        grid = (n // tn, m // tm, nk)
        mi = lambda g0, g1: g1
        ni = lambda g0, g1: g0

    def body(*refs):
        refs = list(refs)
        take = lambda cnt: [refs.pop(0) for _ in range(cnt)]
        a_ref = take(1)[0] if a is not None else None
        b_ref = take(1)[0]
        bias_ref = take(1)[0] if bias is not None else None
        res_ref = take(1)[0] if res is not None else None
        pro_in, epi_in = take(n_pro_in), take(n_epi_in)
        o_ref = take(1)[0] if epilogue is None else None
        pro_out, epi_out = take(n_pro_out), take(n_epi_out)
        i, j, k = mi(pl.program_id(0), pl.program_id(1)), ni(pl.program_id(0), pl.program_id(1)), pl.program_id(2)

        def finish(src):
            def result(rows=slice(None)):
                r = src[rows, :]
                if bias_ref is not None:
                    r = r + bias_ref[...]
                if res_ref is not None:
                    r = r + res_ref[rows, :]
                return r

            if epilogue is None:
                o_ref[...] = result().astype(out_dtype)
            else:
                epilogue.fn(result, epi_in, epi_out, i, j)

        a_val = a_ref[...] if prologue is None else prologue.fn(a_ref, pro_in, pro_out, i, k)
        prod = lax.dot_general(a_val.astype(BF16), b_ref[...].astype(BF16), dn, preferred_element_type=F32)
        if nk == 1:
            finish(prod)
            return
        acc_ref = refs[0]

        @pl.when(k == 0)
        def _():
            acc_ref[...] = prod

        @pl.when(k > 0)
        def _():
            acc_ref[...] += prod

        @pl.when(k == nk - 1)
        def _():
            finish(acc_ref)

    spec = lambda shape, fn: pl.BlockSpec(shape, lambda g0, g1, k: fn(mi(g0, g1), ni(g0, g1), k))
    in_specs, args = [], []
    if a is not None:
        in_specs.append(spec((tk, tm), lambda i, j, k: (k, i)) if ta else spec((tm, tk), lambda i, j, k: (i, k)))
        args.append(a)
    in_specs.append(spec((tn, tk), lambda i, j, k: (j, k)) if tb else spec((tk, tn), lambda i, j, k: (k, j)))
    args.append(b)
    if bias is not None:
        in_specs.append(spec((1, tn), lambda i, j, k: (0, j)))
        args.append(bias)
    if res is not None:
        in_specs.append(spec((tm, tn), lambda i, j, k: (i, j)))
        args.append(res)
    for h in hooks:
        in_specs += [spec(shape, fn) for shape, fn in h.in_specs]
        args += h.ins
    out_specs, out_shape = [], []
    if epilogue is None:
        out_specs.append(spec((tm, tn), lambda i, j, k: (i, j)))
        out_shape.append(jax.ShapeDtypeStruct((m, n), out_dtype))
    for h in hooks:
        out_specs += [spec(blk, fn) for _, _, blk, fn in h.outs]
        out_shape += [jax.ShapeDtypeStruct(shape, dtype) for shape, dtype, _, _ in h.outs]
    outs, couts = _hosted_call(
        body, grid=grid, in_specs=in_specs, out_specs=out_specs, out_shape=out_shape,
        scratch_shapes=[pltpu.VMEM((tm, tn), F32)] if nk > 1 else [],
        sem=("arbitrary",) * 3 if sequential else ("parallel", "parallel", "arbitrary"),
        name=name, args=args, comm=comm)
    outs = outs[0] if not hooks else outs
    return outs if comm is None else (outs, couts)


def _rms_fwd(x, g, *, comm=None, name):
    l, d = x.shape
    tr = min(256, l)

    def body(x_ref, g_ref, h_ref):
        xf = x_ref[...]
        r = lax.rsqrt(jnp.mean(xf * xf, axis=-1, keepdims=True) + RMS_EPS)
        h_ref[...] = ((xf * r) * g_ref[...]).astype(BF16)

    row = pl.BlockSpec((tr, d), lambda i: (i, 0))
    return _hosted_call(
        body, grid=(l // tr,), in_specs=[row, pl.BlockSpec((1, d), lambda i: (0, 0))],
        out_specs=[row], out_shape=[jax.ShapeDtypeStruct((l, d), BF16)], scratch_shapes=[],
        sem=("parallel",), name=name, args=(x, g), comm=comm)


EPI_ROWS = 128
ROW_TILE = 256


def _row_chunks(tm):
    ch = min(EPI_ROWS, tm)
    return [slice(c * ch, (c + 1) * ch) for c in range(tm // ch)]


def _rowwise(hook, src, tm, *, name):
    l, d = src.shape
    n_in = len(hook.ins)

    def body(*refs):
        src_ref, ins, outs = refs[0], refs[1:1 + n_in], refs[1 + n_in:]
        hook.fn(lambda rows=slice(None): src_ref[rows, :], ins, outs, pl.program_id(0), 0)

    spec = lambda shape, fn: pl.BlockSpec(shape, lambda i: fn(i, 0, 0))
    return pl.pallas_call(
        body, grid=(l // tm,),
        in_specs=[pl.BlockSpec((tm, d), lambda i: (i, 0))] + [spec(shape, fn) for shape, fn in hook.in_specs],
        out_specs=[spec(blk, fn) for _, _, blk, fn in hook.outs],
        out_shape=[jax.ShapeDtypeStruct(shape, dtype) for shape, dtype, _, _ in hook.outs],
        name=name, compiler_params=_cp(("arbitrary",)))(src, *hook.ins)


def _rms_bwd_hook(x, g, dres, tm, out_dtype):
    def fn(result, ins, outs, i, j):
        x_ref, g_ref, dres_ref = ins
        dx_ref, dg_ref = outs

        @pl.when(i == 0)
        def _():
            dg_ref[...] = jnp.zeros_like(dg_ref)

        for rows in _row_chunks(tm):
            dyv = result(rows)
            xf = x_ref[rows, :]
            r = lax.rsqrt(jnp.mean(xf * xf, axis=-1, keepdims=True) + RMS_EPS)
            xhat = xf * r
            dxh = dyv * g_ref[...]
            dx = r * (dxh - xhat * jnp.mean(dxh * xhat, axis=-1, keepdims=True)) + dres_ref[rows, :].astype(F32)
            dx_ref[rows, :] = dx.astype(out_dtype)
            dg_ref[...] += jnp.sum(dyv * xhat, axis=0, keepdims=True)

    l, d = x.shape
    row = lambda i, j, k: (i, 0)
    vec = lambda i, j, k: (0, 0)
    return _Hook(fn, ins=[x, g, dres], in_specs=[((tm, d), row), ((1, d), vec), ((tm, d), row)],
                 outs=[((l, d), out_dtype, (tm, d), row), ((1, d), F32, (1, d), vec)])


def _final_loss_hook(g, target, tm):
    l, d = target.shape

    def fn(result, ins, outs, i, j):
        g_ref, t_ref = ins
        dxb_ref, dg_ref, loss_ref = outs
        gv = g_ref[...]

        @pl.when(i == 0)
        def _():
            dg_ref[...] = jnp.zeros_like(dg_ref)
            loss_ref[...] = jnp.zeros_like(loss_ref)

        for rows in _row_chunks(tm):
            xf = result(rows)
            r = lax.rsqrt(jnp.mean(xf * xf, axis=-1, keepdims=True) + RMS_EPS)
            xhat = xf * r
            diff = xhat * gv - t_ref[rows, :]
            dout = diff * (1.0 / d)
            dxh = dout * gv
            dx = r * (dxh - xhat * jnp.mean(dxh * xhat, axis=-1, keepdims=True))
            dxb_ref[rows, :] = dx.astype(BF16)
            dg_ref[...] += jnp.sum(dout * xhat, axis=0, keepdims=True)
            part = jnp.sum(jnp.mean(diff * diff, axis=-1, keepdims=True), axis=0, keepdims=True)
            loss_ref[...] += 0.5 * part

    row = lambda i, j, k: (i, 0)
    vec = lambda i, j, k: (0, 0)
    return _Hook(fn, ins=[g, target], in_specs=[((1, d), vec), ((tm, d), row)],
                 outs=[((l, d), BF16, (tm, d), row), ((1, d), F32, (1, d), vec), ((1, 1), F32, (1, 1), vec)])


Q_PER_KV = 8
GROUP_ROWS = Q_PER_KV * BLOCK


def _attn_masks(n, rows=GROUP_ROWS):
    q_idx = lax.broadcasted_iota(jnp.int32, (rows, 2 * BLOCK), 0) & (BLOCK - 1)
    s_idx = lax.broadcasted_iota(jnp.int32, (rows, 2 * BLOCK), 1)
    dist = q_idx + BLOCK - s_idx
    valid = (dist >= 0) & (dist < BLOCK) & ((n > 0) | (s_idx >= BLOCK))
    return dist.astype(F32), valid


def _dup_half(t, kv_head, lo):
    rolled = pltpu.roll(t, HEAD_DIM, axis=1)
    return jnp.where(lo, t, rolled) if kv_head == 0 else jnp.where(lo, rolled, t)


def _stack_heads(ref, kv_head, lo):
    pieces = []
    for r in range(Q_PER_KV):
        pair = kv_head * 4 + r // 2
        t = ref[:, pair * 128:(pair + 1) * 128].astype(BF16)
        sel = lo if r % 2 == 0 else jnp.logical_not(lo)
        pieces.append(jnp.where(sel, t, jnp.zeros_like(t)))
    return jnp.concatenate(pieces, axis=0)


def _unstack_heads(t, lo):
    return [jnp.where(lo, t[(2 * i) * BLOCK:(2 * i + 1) * BLOCK], t[(2 * i + 1) * BLOCK:(2 * i + 2) * BLOCK])
            for i in range(Q_PER_KV // 2)]


def _per_head_column(values):
    return jnp.concatenate([jnp.full((BLOCK, 1), v, F32) for v in values], axis=0)


def _group_probs(qm, kdup, dist, valid, sink_ref, kv_head):
    heads = [kv_head * Q_PER_KV + r for r in range(Q_PER_KV)]
    slope = _per_head_column([2.0 ** (-8.0 * (h + 1) / N_Q_HEADS) for h in heads])
    sink = _per_head_column([sink_ref[h] for h in heads])
    return _probs(qm, kdup, dist, valid, sink, slope)


def _probs(qm, kdup, dist, valid, sink, slope):
    s = lax.dot_general(qm, kdup, (((1,), (1,)), ((), ())), preferred_element_type=F32)
    s = s * (HEAD_DIM ** -0.5) - slope * dist
    s = jnp.where(valid, s, NEG_BIG)
    m = jnp.maximum(jnp.max(s, axis=-1, keepdims=True), sink)
    p = jnp.exp(s - m)
    esink = jnp.exp(sink - m)
    inv = 1.0 / (jnp.sum(p, axis=-1, keepdims=True) + esink)
    return p * inv, esink * inv


def _attn_fwd(proj, sinks, *, name):
    l = proj.shape[0]
    nb = l // BLOCK

    def body(sink_ref, q_ref, kc_ref, kp_ref, vc_ref, vp_ref, o_ref):
        n = pl.program_id(0)
        dist, valid = _attn_masks(n, BLOCK)
        lo = lax.broadcasted_iota(jnp.int32, (1, BLOCK), 1) < HEAD_DIM
        kx = jnp.concatenate([kp_ref[...], kc_ref[...]], axis=0).astype(BF16)
        vx = jnp.concatenate([vp_ref[...], vc_ref[...]], axis=0).astype(BF16)
        for kv_head in range(2):
            kdup = _dup_half(kx, kv_head, lo)
            vdup = _dup_half(vx, kv_head, lo)
            for pr in range(4):
                pair = kv_head * 4 + pr
                qp = q_ref[:, pair * 128:(pair + 1) * 128].astype(BF16)
                o_pair = jnp.zeros((BLOCK, 128), F32)
                for half in range(2):
                    head = 2 * pair + half
                    sel = lo if half == 0 else jnp.logical_not(lo)
                    qm = jnp.where(sel, qp, jnp.zeros_like(qp))
                    p, _ = _probs(qm, kdup, dist, valid, sink_ref[head], 2.0 ** (-8.0 * (head + 1) / N_Q_HEADS))
                    o = jnp.dot(p.astype(BF16), vdup, preferred_element_type=F32)
                    o_pair = o_pair + jnp.where(sel, o, 0.0)
                o_ref[:, pair * 128:(pair + 1) * 128] = o_pair.astype(BF16)

    kv = lambda col, prev: pl.BlockSpec(
        (BLOCK, KV_WIDTH), (lambda n: (jnp.maximum(n - 1, 0), col)) if prev else (lambda n: (n, col)))
    return pl.pallas_call(
        body, grid=(nb,),
        in_specs=[pl.BlockSpec(memory_space=pltpu.SMEM),
                  pl.BlockSpec((BLOCK, ATTN_WIDTH), lambda n: (n, 0)),
                  kv(COL_K, False), kv(COL_K, True), kv(COL_V, False), kv(COL_V, True)],
        out_specs=pl.BlockSpec((BLOCK, ATTN_WIDTH), lambda n: (n, 0)),
        out_shape=jax.ShapeDtypeStruct((l, ATTN_WIDTH), BF16), name=name,
        compiler_params=_cp(("parallel",)))(sinks, proj, proj, proj, proj, proj)


def _attn_bwd(proj, sinks, dattn, *, comm=None, name):
    l = proj.shape[0]
    nb = l // BLOCK

    def body(sink_ref, q_ref, kc_ref, kp_ref, vc_ref, vp_ref, do_ref,
             dq_ref, dkc_ref, dkp_ref, dvc_ref, dvp_ref, dsink_ref):
        n = pl.program_id(0)
        dist, valid = _attn_masks(n)
        lane = lax.broadcasted_iota(jnp.int32, (1, BLOCK), 1)
        lo = lane < HEAD_DIM
        kx = jnp.concatenate([kp_ref[...], kc_ref[...]], axis=0).astype(BF16)
        vx = jnp.concatenate([vp_ref[...], vc_ref[...]], axis=0).astype(BF16)
        dsink = jnp.zeros((1, BLOCK), F32)
        dk_heads, dv_heads = [], []
        for kv_head in range(2):
            kdup = _dup_half(kx, kv_head, lo)
            vdup = _dup_half(vx, kv_head, lo)
            qm = _stack_heads(q_ref, kv_head, lo)
            dom = _stack_heads(do_ref, kv_head, lo)
            p, psink = _group_probs(qm, kdup, dist, valid, sink_ref, kv_head)
            dp = lax.dot_general(dom, vdup, (((1,), (1,)), ((), ())), preferred_element_type=F32)
            delta = jnp.sum(p * dp, axis=-1, keepdims=True)
            ds = (p * (dp - delta) * (HEAD_DIM ** -0.5)).astype(BF16)
            dsink_rows = -psink * delta
            for r in range(Q_PER_KV):
                part = jnp.sum(dsink_rows[r * BLOCK:(r + 1) * BLOCK])
                dsink = dsink + jnp.where(lane == kv_head * Q_PER_KV + r, part, 0.0)
            dq = jnp.dot(ds, kdup, preferred_element_type=F32)
            for i, dq_pair in enumerate(_unstack_heads(dq, lo)):
                pair = kv_head * 4 + i
                dq_ref[:, pair * 128:(pair + 1) * 128] = dq_pair.astype(BF16)
            dk_acc = lax.dot_general(ds, qm, (((0,), (0,)), ((), ())), preferred_element_type=F32)
            dv_acc = lax.dot_general(p.astype(BF16), dom, (((0,), (0,)), ((), ())), preferred_element_type=F32)
            dk_heads.append(dk_acc + pltpu.roll(dk_acc, HEAD_DIM, axis=1))
            dv_heads.append(dv_acc + pltpu.roll(dv_acc, HEAD_DIM, axis=1))
        dk = jnp.where(lo, dk_heads[0], dk_heads[1])
        dv = jnp.where(lo, dv_heads[0], dv_heads[1])
        dkp_ref[...] = dk[:BLOCK]
        dkc_ref[...] = dk[BLOCK:]
        dvp_ref[...] = dv[:BLOCK]
        dvc_ref[...] = dv[BLOCK:]

        @pl.when(n == 0)
        def _():
            dsink_ref[...] = jnp.zeros_like(dsink_ref)

        dsink_ref[...] += dsink

    kv = lambda col, prev: pl.BlockSpec(
        (BLOCK, KV_WIDTH), (lambda n: (jnp.maximum(n - 1, 0), col)) if prev else (lambda n: (n, col)))
    qspec = pl.BlockSpec((BLOCK, ATTN_WIDTH), lambda n: (n, 0))
    kvout = pl.BlockSpec((BLOCK, KV_WIDTH), lambda n: (n, 0))
    kvshape = jax.ShapeDtypeStruct((l, KV_WIDTH), F32)
    return _hosted_call(
        body, grid=(nb,),
        in_specs=[pl.BlockSpec(memory_space=pltpu.SMEM), qspec,
                  kv(COL_K, False), kv(COL_K, True), kv(COL_V, False), kv(COL_V, True), qspec],
        out_specs=[qspec, kvout, kvout, kvout, kvout, pl.BlockSpec((1, BLOCK), lambda n: (0, 0))],
        out_shape=[jax.ShapeDtypeStruct((l, ATTN_WIDTH), BF16), kvshape, kvshape, kvshape, kvshape,
                   jax.ShapeDtypeStruct((1, BLOCK), F32)],
        scratch_shapes=[], sem=("arbitrary",), name=name,
        args=(sinks, proj, proj, proj, proj, proj, dattn), comm=comm)


def _kv_grad_merge(dkc, dkp, dvc, dvp, *, name):
    l = dkc.shape[0]
    nb = l // BLOCK

    def body(dkc_ref, dkp_ref, dvc_ref, dvp_ref, o_ref):
        last = pl.program_id(0) == nb - 1
        o_ref[:, :KV_WIDTH] = (dkc_ref[...] + jnp.where(last, 0.0, dkp_ref[...])).astype(BF16)
        o_ref[:, KV_WIDTH:] = (dvc_ref[...] + jnp.where(last, 0.0, dvp_ref[...])).astype(BF16)

    cur = pl.BlockSpec((BLOCK, KV_WIDTH), lambda n: (n, 0))
    nxt = pl.BlockSpec((BLOCK, KV_WIDTH), lambda n: (jnp.minimum(n + 1, nb - 1), 0))
    return pl.pallas_call(
        body, grid=(nb,), in_specs=[cur, nxt, cur, nxt],
        out_specs=pl.BlockSpec((BLOCK, 2 * KV_WIDTH), lambda n: (n, 0)),
        out_shape=jax.ShapeDtypeStruct((l, 2 * KV_WIDTH), BF16), name=name,
        compiler_params=_cp(("parallel",)))(dkc, dkp, dvc, dvp)


def _discretize(a_re, a_im, log_dt, b_re, b_im):
    dt = jnp.exp(log_dt)
    mag = jnp.exp(a_re * dt)
    ab_re = mag * jnp.cos(a_im * dt)
    ab_im = mag * jnp.sin(a_im * dt)
    nr = ab_re - 1.0
    ni = ab_im
    den = a_re * a_re + a_im * a_im
    z_re = (nr * a_re + ni * a_im) / den
    z_im = (ni * a_re - nr * a_im) / den
    bb_re = z_re * b_re - z_im * b_im
    bb_im = z_re * b_im + z_im * b_re
    return ab_re, ab_im, bb_re, bb_im


def _ssm_disc_fwd(a_re, a_im, log_dt, b_re, b_im, *, name):
    def body(ar, ai, ld, br, bi, o_ar, o_ai, o_br, o_bi):
        r = _discretize(ar[...], ai[...], ld[...], br[...], bi[...])
        o_ar[...], o_ai[...], o_br[...], o_bi[...] = r

    col = jax.ShapeDtypeStruct(a_re.shape, F32)
    mat = jax.ShapeDtypeStruct(b_re.shape, F32)
    return pl.pallas_call(body, out_shape=[col, col, mat, mat], name=name)(a_re, a_im, log_dt, b_re, b_im)


def _ssm_disc_bwd(a_re, a_im, log_dt, b_re, b_im, d_ab_re, d_ab_im, d_bb_re, d_bb_im, *, name):
    def body(ar, ai, ld, br, bi, g0, g1, g2, g3, o_ar, o_ai, o_ld, o_br, o_bi):
        _, vjp = jax.vjp(_discretize, ar[...], ai[...], ld[...], br[...], bi[...])
        r = vjp((g0[...], g1[...], g2[...], g3[...]))
        o_ar[...], o_ai[...], o_ld[...], o_br[...], o_bi[...] = r

    col = jax.ShapeDtypeStruct(a_re.shape, F32)
    mat = jax.ShapeDtypeStruct(b_re.shape, F32)
    return pl.pallas_call(body, out_shape=[col, col, col, mat, mat], name=name)(
        a_re, a_im, log_dt, b_re, b_im, d_ab_re, d_ab_im, d_bb_re, d_bb_im)


def _shift_rows(x, d, rows, *, down):
    t = x.shape[0]
    if down:
        return jnp.where(rows >= d, pltpu.roll(x, d, axis=0), 0.0)
    return jnp.where(rows < t - d, pltpu.roll(x, t - d, axis=0), 0.0)


def _scan_chunk(xr, xi, ar, ai, *, down):
    t = xr.shape[0]
    rows = lax.broadcasted_iota(jnp.int32, (t, 1), 0)
    pr, pi = ar, ai
    d = 1
    while d < t:
        sr = _shift_rows(xr, d, rows, down=down)
        si = _shift_rows(xi, d, rows, down=down)
        xr, xi = xr + pr * sr - pi * si, xi + pr * si + pi * sr
        pr, pi = pr * pr - pi * pi, 2.0 * pr * pi
        d *= 2
    return xr, xi


def _ssm_fwd(proj, ab, bd, cd, dskip, *, comm=None, name):
    l = proj.shape[0]
    t = min(SSM_CHUNK, l)
    nc = l // t

    def body(u_ref, ab_ref, bd_ref, cd_ref, ds_ref, y_ref, gy_ref, xs_ref, carry_ref):
        c = pl.program_id(1)

        @pl.when(c == 0)
        def _():
            carry_ref[...] = jnp.zeros_like(carry_ref)

        u = u_ref[...]
        ar, ai = ab_ref[0, 0:1, :], ab_ref[0, 1:2, :]
        bu = jnp.dot(u.astype(BF16), bd_ref[0], preferred_element_type=F32)
        rows = lax.broadcasted_iota(jnp.int32, (t, 1), 0)
        cr, ci = carry_ref[0:1, :], carry_ref[1:2, :]
        xr = bu[:, :SSM_X_BLK] + jnp.where(rows == 0, ar * cr - ai * ci, 0.0)
        xi = bu[:, SSM_X_BLK:] + jnp.where(rows == 0, ar * ci + ai * cr, 0.0)
        xr, xi = _scan_chunk(xr, xi, ar, ai, down=True)
        xs_ref[0, :, :SSM_X_BLK] = xr
        xs_ref[0, :, SSM_X_BLK:] = xi
        carry_ref[0:1, :] = xs_ref[0, t - 1:t, :SSM_X_BLK]
        carry_ref[1:2, :] = xs_ref[0, t - 1:t, SSM_X_BLK:]
        y = jnp.dot(xs_ref[0].astype(BF16), cd_ref[0], preferred_element_type=F32) + ds_ref[...] * u
        y_ref[...] = y
        gy_ref[...] = _gelu(y).astype(BF16)

    blk = lambda shape: pl.BlockSpec((1,) + shape, lambda j, c: (j, 0, 0))
    ycol = pl.BlockSpec((t, SSM_U_BLK), lambda j, c: (c, j))
    return _hosted_call(
        body, grid=(SSM_SPLIT, nc),
        in_specs=[pl.BlockSpec((t, SSM_U_BLK), lambda j, c: (c, COL_U + j)),
                  blk((2, SSM_X_BLK)), blk((SSM_U_BLK, 2 * SSM_X_BLK)), blk((2 * SSM_X_BLK, SSM_U_BLK)),
                  pl.BlockSpec((1, SSM_U_BLK), lambda j, c: (0, j))],
        out_specs=[ycol, ycol, pl.BlockSpec((1, t, 2 * SSM_X_BLK), lambda j, c: (j, c, 0))],
        out_shape=[jax.ShapeDtypeStruct((l, SSM_WIDTH), F32), jax.ShapeDtypeStruct((l, SSM_WIDTH), BF16),
                   jax.ShapeDtypeStruct((SSM_SPLIT, l, 2 * SSM_X_BLK), F32)],
        scratch_shapes=[pltpu.VMEM((2, SSM_X_BLK), F32)], sem=("parallel", "arbitrary"), name=name,
        args=(proj, ab, bd, cd, dskip), comm=comm)


def _ssm_bwd(proj, y, dgy, xs, ab, bdt, cdt, dskip, *, comm=None, name):
    l = proj.shape[0]
    t = min(SSM_CHUNK, l)
    nc = l // t

    def body(u_ref, y_ref, dgy_ref, xs_ref, halo_ref, ab_ref, bdt_ref, cdt_ref, ds_ref,
             du_ref, dbd_ref, dcd_ref, dab_ref, dd_ref, carry_ref):
        c = pl.program_id(1)
        ci_ = nc - 1 - c

        @pl.when(c == 0)
        def _():
            carry_ref[...] = jnp.zeros_like(carry_ref)
            dbd_ref[...] = jnp.zeros_like(dbd_ref)
            dcd_ref[...] = jnp.zeros_like(dcd_ref)
            dab_ref[...] = jnp.zeros_like(dab_ref)
            dd_ref[...] = jnp.zeros_like(dd_ref)

        u = u_ref[...]
        dy = dgy_ref[...] * _gelu_grad(y_ref[...])
        dyb = dy.astype(BF16)
        ar, ai = ab_ref[0, 0:1, :], ab_ref[0, 1:2, :]
        g = jnp.dot(dyb, cdt_ref[0], preferred_element_type=F32)
        rows = lax.broadcasted_iota(jnp.int32, (t, 1), 0)
        cr, ci = carry_ref[0:1, :], carry_ref[1:2, :]
        lr = g[:, :SSM_X_BLK] + jnp.where(rows == t - 1, ar * cr + ai * ci, 0.0)
        li = g[:, SSM_X_BLK:] + jnp.where(rows == t - 1, ar * ci - ai * cr, 0.0)
        lr, li = _scan_chunk(lr, li, ar, -ai, down=False)
        lam = jnp.concatenate([lr, li], axis=1)
        carry_ref[0:1, :] = lr[0:1, :]
        carry_ref[1:2, :] = li[0:1, :]
        lamb = lam.astype(BF16)
        du_ref[...] = (jnp.dot(lamb, bdt_ref[0], preferred_element_type=F32) + ds_ref[...] * dy).astype(BF16)
        dbd_ref[0] += lax.dot_general(u.astype(BF16), lamb, (((0,), (0,)), ((), ())),
                                      preferred_element_type=F32)
        xs = xs_ref[0]
        dcd_ref[0] += lax.dot_general(xs.astype(BF16), dyb, (((0,), (0,)), ((), ())),
                                      preferred_element_type=F32)
        halo = jnp.where(ci_ > 0, halo_ref[0, 7:8, :], 0.0)
        xprev = jnp.where(rows == 0, halo, pltpu.roll(xs, 1, axis=0))
        xpr, xpi = xprev[:, :SSM_X_BLK], xprev[:, SSM_X_BLK:]
        dab_ref[0, 0:1, :] += jnp.sum(lr * xpr + li * xpi, axis=0, keepdims=True)
        dab_ref[0, 1:2, :] += jnp.sum(li * xpr - lr * xpi, axis=0, keepdims=True)
        dd_ref[...] += jnp.sum(dy * u, axis=0, keepdims=True)

    blk = lambda shape: pl.BlockSpec((1,) + shape, lambda j, c: (j, 0, 0))
    rev = lambda j, c: (nc - 1 - c, j)
    ycol = pl.BlockSpec((t, SSM_U_BLK), rev)
    hb = t // 8
    return _hosted_call(
        body, grid=(SSM_SPLIT, nc), comm=comm, sem=("parallel", "arbitrary"), name=name,
        args=(proj, y, dgy, xs, xs, ab, bdt, cdt, dskip), scratch_shapes=[pltpu.VMEM((2, SSM_X_BLK), F32)],
        in_specs=[pl.BlockSpec((t, SSM_U_BLK), lambda j, c: (nc - 1 - c, COL_U + j)), ycol, ycol,
                  pl.BlockSpec((1, t, 2 * SSM_X_BLK), lambda j, c: (j, nc - 1 - c, 0)),
                  pl.BlockSpec((1, 8, 2 * SSM_X_BLK),
                               lambda j, c: (j, jnp.maximum((nc - 1 - c) * hb - 1, 0), 0)),
                  blk((2, SSM_X_BLK)), blk((2 * SSM_X_BLK, SSM_U_BLK)), blk((SSM_U_BLK, 2 * SSM_X_BLK)),
                  pl.BlockSpec((1, SSM_U_BLK), lambda j, c: (0, j))],
        out_specs=[ycol, blk((SSM_U_BLK, 2 * SSM_X_BLK)), blk((2 * SSM_X_BLK, SSM_U_BLK)),
                   blk((2, SSM_X_BLK)), pl.BlockSpec((1, SSM_U_BLK), lambda j, c: (0, j))],
        out_shape=[jax.ShapeDtypeStruct((l, SSM_WIDTH), BF16),
                   jax.ShapeDtypeStruct((SSM_SPLIT, SSM_U_BLK, 2 * SSM_X_BLK), F32),
                   jax.ShapeDtypeStruct((SSM_SPLIT, 2 * SSM_X_BLK, SSM_U_BLK), F32),
                   jax.ShapeDtypeStruct((SSM_SPLIT, 2, SSM_X_BLK), F32),
                   jax.ShapeDtypeStruct((1, SSM_WIDTH), F32)])


def _block_diag(t):
    s, g, a, b = t.shape
    return jnp.einsum('sgab,gk->sgakb', t, jnp.eye(g, dtype=t.dtype)).reshape(s, g * a, g * b)


def _block_diag_take(t, a, b):
    s = t.shape[0]
    return jnp.einsum('sgakb,gk->sgab', t.reshape(s, 8, a, 8, b), jnp.eye(8, dtype=t.dtype))


def _glu_fwd_hook(l, tm):
    def fn(result, ins, outs, i, j):
        z = result()
        outs[0][...] = z
        outs[1][...] = (z[:, :SSM_WIDTH] * _sigmoid(z[:, SSM_WIDTH:])).astype(BF16)

    row = lambda i, j, k: (i, 0)
    return _Hook(fn, outs=[((l, 2 * SSM_WIDTH), F32, (tm, 2 * SSM_WIDTH), row),
                           ((l, SSM_WIDTH), BF16, (tm, SSM_WIDTH), row)])


def _glu_bwd_hook(z, tm):
    l = z.shape[0]

    def fn(result, ins, outs, i, j):
        zv_ref, zg_ref = ins
        dz_ref, db_ref = outs
        d = result()
        sg = _sigmoid(zg_ref[...])
        dv = d * sg
        dg = d * zv_ref[...] * sg * (1.0 - sg)
        dz_ref[:, :SSM_WIDTH] = dv.astype(BF16)
        dz_ref[:, SSM_WIDTH:] = dg.astype(BF16)

        @pl.when(i == 0)
        def _():
            db_ref[...] = jnp.zeros_like(db_ref)

        db_ref[:, :SSM_WIDTH] += jnp.sum(dv, axis=0, keepdims=True)
        db_ref[:, SSM_WIDTH:] += jnp.sum(dg, axis=0, keepdims=True)

    half = (tm, SSM_WIDTH)
    return _Hook(fn, ins=[z, z], in_specs=[(half, lambda i, j, k: (i, 0)), (half, lambda i, j, k: (i, 1))],
                 outs=[((l, 2 * SSM_WIDTH), BF16, (tm, 2 * SSM_WIDTH), lambda i, j, k: (i, 0)),
                       ((1, 2 * SSM_WIDTH), F32, (1, 2 * SSM_WIDTH), lambda i, j, k: (0, 0))])


GATE_TC = 256


def _merge_fwd(proj, a, s, *, name):
    l = a.shape[0]
    tr = min(2048, l)

    def body(ga_ref, gs_ref, a_ref, s_ref, o_ref):
        o_ref[...] = (_sigmoid(ga_ref[...]) * a_ref[...] + _sigmoid(gs_ref[...]) * s_ref[...]).astype(BF16)

    own = pl.BlockSpec((tr, GATE_TC), lambda i, j: (i, j))
    return pl.pallas_call(
        body, grid=(l // tr, D_MODEL // GATE_TC),
        in_specs=[pl.BlockSpec((tr, GATE_TC), lambda i, j: (i, COL_GA // 2 + j)),
                  pl.BlockSpec((tr, GATE_TC), lambda i, j: (i, COL_GS // 2 + j)), own, own],
        out_specs=own, out_shape=jax.ShapeDtypeStruct((l, D_MODEL), BF16), name=name,
        compiler_params=_cp(("parallel", "parallel")))(proj, proj, a, s)


def _merge_bwd_hook(proj, a, s, tm):
    def fn(result, ins, outs, i, j):
        ga_ref, gs_ref, a_br, s_br = ins
        d = result()
        sa = _sigmoid(ga_ref[...])
        ss = _sigmoid(gs_ref[...])
        outs[0][...] = (d * sa).astype(BF16)
        outs[1][...] = (d * ss).astype(BF16)
        outs[2][...] = (d * a_br[...] * sa * (1.0 - sa)).astype(BF16)
        outs[3][...] = (d * s_br[...] * ss * (1.0 - ss)).astype(BF16)

    blk = (tm, GATE_TC)
    own = lambda i, j, k: (i, j)
    return _Hook(fn, ins=[proj, proj, a, s],
                 in_specs=[(blk, lambda i, j, k: (i, COL_GA // 2 + j)), (blk, lambda i, j, k: (i, COL_GS // 2 + j)),
                           (blk, own), (blk, own)],
                 outs=[(a.shape, BF16, blk, own)] * 4)


FF_TC = D_FF // 2
FF_NJ = 2
FF_ROWS = 128


FF_HALO = 16


def _conv_taps(ext, rows):
    h = FF_HALO
    return (ext[h:h + rows], pltpu.roll(ext, 1, axis=0)[h:h + rows], pltpu.roll(ext, 2, axis=0)[h:h + rows])


def _ff_specs(tr, l):
    hb = tr // FF_HALO
    last = l // FF_HALO - 1
    prev = lambda i: jnp.maximum(i * hb - 1, 0)
    nxt = lambda i: jnp.minimum((i + 1) * hb, last)
    return dict(
        own=pl.BlockSpec((tr, FF_TC), lambda j, i: (i, j)),
        own_next=pl.BlockSpec((FF_HALO, FF_TC), lambda j, i: (nxt(i), j)),
        val=pl.BlockSpec((tr, FF_TC), lambda j, i: (i, 2 * j)),
        val_next=pl.BlockSpec((FF_HALO, FF_TC), lambda j, i: (nxt(i), 2 * j)),
        gate=pl.BlockSpec((tr, FF_TC), lambda j, i: (i, 2 * j + 1)),
        gate_prev=pl.BlockSpec((FF_HALO, FF_TC), lambda j, i: (prev(i), 2 * j + 1)),
        gate_next=pl.BlockSpec((FF_HALO, FF_TC), lambda j, i: (nxt(i), 2 * j + 1)),
        pair=pl.BlockSpec((tr, 2 * FF_TC), lambda j, i: (i, j)),
        w=pl.BlockSpec((3, FF_TC), lambda j, i: (0, j)),
        b=pl.BlockSpec((1, FF_TC), lambda j, i: (0, j)))


def _ffn_act_fwd(up, conv_w, conv_b, *, name):
    l = up.shape[0]
    tr = min(FF_ROWS, l)

    def body(v_ref, g_ref, prev_ref, w_ref, b_ref, o_ref):
        prev = jnp.where(pl.program_id(1) == 0, 0.0, prev_ref[...].astype(F32))
        g0, g1, g2 = _conv_taps(jnp.concatenate([prev, g_ref[...].astype(F32)], axis=0), tr)
        gc = b_ref[...] + w_ref[0:1, :] * g2 + w_ref[1:2, :] * g1 + w_ref[2:3, :] * g0
        o_ref[...] = (v_ref[...].astype(F32) * _gelu(gc)).astype(BF16)

    sp = _ff_specs(tr, l)
    return pl.pallas_call(
        body, grid=(FF_NJ, l // tr), in_specs=[sp['val'], sp['gate'], sp['gate_prev'], sp['w'], sp['b']],
        out_specs=sp['own'], out_shape=jax.ShapeDtypeStruct((l, D_FF), BF16), name=name,
        compiler_params=_cp(("parallel", "parallel")))(up, up, up, conv_w, conv_b)


def _ffn_act_bwd(dact, up, conv_w, conv_b, *, comm=None, name):
    l = up.shape[0]
    tr = min(FF_ROWS, l)
    ni = l // tr
    te = tr + 8

    def body(d_ref, dn_ref, v_ref, vn_ref, g_ref, gp_ref, gn_ref, w_ref, b_ref, dup_ref, dw_ref, db_ref):
        i = pl.program_id(1)
        f32 = lambda ref, rows=None: ref[...].astype(F32)[:rows]
        prev = jnp.where(i == 0, 0.0, f32(gp_ref))
        g0, g1, g2 = _conv_taps(jnp.concatenate([prev, f32(g_ref), f32(gn_ref, 8)], axis=0), te)
        w0, w1, w2 = w_ref[0:1, :], w_ref[1:2, :], w_ref[2:3, :]
        gc = b_ref[...] + w0 * g2 + w1 * g1 + w2 * g0
        d_own = f32(d_ref)
        d = jnp.concatenate([d_own, jnp.where(i == ni - 1, 0.0, f32(dn_ref, 8))], axis=0)
        v = jnp.concatenate([f32(v_ref), f32(vn_ref, 8)], axis=0)
        dgc = d * v * _gelu_grad(gc)
        ahead1 = pltpu.roll(dgc, te - 1, axis=0)[:tr]
        ahead2 = pltpu.roll(dgc, te - 2, axis=0)[:tr]
        own = dgc[:tr]
        dup_ref[:, :FF_TC] = (d_own * _gelu(gc[:tr])).astype(BF16)
        dup_ref[:, FF_TC:] = (w2 * own + w1 * ahead1 + w0 * ahead2).astype(BF16)

        @pl.when(i == 0)
        def _():
            dw_ref[...] = jnp.zeros_like(dw_ref)
            db_ref[...] = jnp.zeros_like(db_ref)

        dw_ref[0:1, :] += jnp.sum(own * g2[:tr], axis=0, keepdims=True)
        dw_ref[1:2, :] += jnp.sum(own * g1[:tr], axis=0, keepdims=True)
        dw_ref[2:3, :] += jnp.sum(own * g0[:tr], axis=0, keepdims=True)
        db_ref[...] += jnp.sum(own, axis=0, keepdims=True)

    sp = _ff_specs(tr, l)
    return _hosted_call(
        body, grid=(FF_NJ, ni),
        in_specs=[sp['own'], sp['own_next'], sp['val'], sp['val_next'], sp['gate'], sp['gate_prev'],
                  sp['gate_next'], sp['w'], sp['b']],
        out_specs=[sp['pair'], sp['w'], sp['b']],
        out_shape=[jax.ShapeDtypeStruct((l, 2 * D_FF), BF16), jax.ShapeDtypeStruct((3, D_FF), F32),
                   jax.ShapeDtypeStruct((1, D_FF), F32)],
        scratch_shapes=[], sem=("parallel", "arbitrary"), name=name,
        args=(dact, dact, up, up, up, up, up, conv_w, conv_b), comm=comm)


def _col_sum(a, *, name):
    l, n = a.shape
    tr = min(512, l)

    def body(a_ref, o_ref):
        @pl.when(pl.program_id(0) == 0)
        def _():
            o_ref[...] = jnp.zeros_like(o_ref)

        o_ref[...] += jnp.sum(a_ref[...].astype(F32), axis=0, keepdims=True)

    return pl.pallas_call(
        body, grid=(l // tr,), in_specs=[pl.BlockSpec((tr, n), lambda i: (i, 0))],
        out_specs=pl.BlockSpec((1, n), lambda i: (0, 0)), out_shape=jax.ShapeDtypeStruct((1, n), F32),
        name=name, compiler_params=_cp(("arbitrary",)))(a)


def _local_step(x, target, wts, small, shards=None):
    l = x.shape[0]
    wts = dict(wts)
    grads, recvs, sgr = {}, {}, {}
    lay = lambda keys: [LAYOUT[k] for k in keys]
    none = lambda keys: None
    gather = (lambda keys: _GatherPlan([shards[k] for k in keys], lay(keys))) if shards is not None else none
    scatter = (lambda keys: _ScatterPlan([grads[k] for k in keys], lay(keys))) if shards is not None else none

    mm = _matmul

    def take(res, plan, keys, store):
        outs, couts = res
        if plan is not None:
            store.update(zip(keys, couts))
        return outs

    def mm_plan(plan, keys, store, *args, **kw):
        if plan is None:
            return _matmul(*args, **kw)
        return take(_matmul(*args, comm=plan, **kw), plan, keys, store)

    def mm_host(keys, make_plan, store, *args, **kw):
        return mm_plan(make_plan(keys), keys, store, *args, **kw)

    up_scatter = lambda p: _ScatterPlan([grads['w_up']], lay(['w_up']), part=(p, 2)) if shards is not None else None

    col = lambda t: t.reshape(SSM_GROUPS * SSM_STATE, 1)
    a_re, a_im = col(small['ssm_a_re']), col(small['ssm_a_im'])
    log_dt = jnp.repeat(small['ssm_log_dt'].reshape(SSM_GROUPS), SSM_STATE).reshape(-1, 1)
    b_re = small['ssm_b_re'].reshape(SSM_GROUPS * SSM_STATE, SSM_GROUP)
    b_im = small['ssm_b_im'].reshape(SSM_GROUPS * SSM_STATE, SSM_GROUP)
    ab_re, ab_im, bb_re, bb_im = _ssm_disc_fwd(a_re, a_im, log_dt, b_re, b_im, name="ssm_disc_fwd")
    ab = jnp.stack([ab_re.reshape(SSM_SPLIT, SSM_X_BLK), ab_im.reshape(SSM_SPLIT, SSM_X_BLK)], axis=1)
    to_bd = lambda t: _block_diag(t.reshape(SSM_SPLIT, 8, SSM_STATE, SSM_GROUP).transpose(0, 1, 3, 2))
    bd = jnp.concatenate([to_bd(bb_re), to_bd(bb_im)], axis=2)
    c_re = small['ssm_c_re'].reshape(SSM_SPLIT, 8, SSM_GROUP, SSM_STATE)
    c_im = small['ssm_c_im'].reshape(SSM_SPLIT, 8, SSM_GROUP, SSM_STATE)
    cdt = jnp.concatenate([_block_diag(c_re), -_block_diag(c_im)], axis=2)
    bd_b, cdt_b = bd.astype(BF16), cdt.astype(BF16)
    bdt_b, cd_b = bd_b.transpose(0, 2, 1), cdt_b.transpose(0, 2, 1)
    dskip = small['ssm_d'].reshape(1, SSM_WIDTH)

    sinks = small['attn_sinks'].reshape(N_Q_HEADS)
    plan = gather(['w_in_t'])
    h1, = take(_rms_fwd(x, small['attn_norm_g'], comm=plan, name="rms1_fwd"), plan, ['w_in_t'], wts)
    proj = mm_host(['w_glu', 'w_ba', 'w_bs', 'w_out'], gather, wts,
                   h1, wts['w_in_t'], tb=True, tm=512, tn=2944, tk=2048, inner='m', out_dtype=F32,
                   bias=small['b_in'], name="mm_in")
    attn = _attn_fwd(proj, sinks, name="attn_fwd")
    plan = gather(['w_up'])
    y, gy, xs = take(_ssm_fwd(proj, ab, bd_b, cd_b, dskip, comm=plan, name="ssm_fwd"), plan, ['w_up'], wts)
    z, ssm = mm(gy, wts['w_glu'], tm=1024, tn=1024, tk=512, bias=small['b_glu'],
                epilogue=_glu_fwd_hook(l, min(1024, l)), name="mm_glu")
    a_br = mm(attn, wts['w_ba'], tm=1024, tn=1024, tk=1024, out_dtype=F32, name="mm_ba")
    s_br = mm(ssm, wts['w_bs'], tm=1024, tn=1024, tk=512, out_dtype=F32, name="mm_bs")
    tr = min(ROW_TILE, l)
    merged = _merge_fwd(proj, a_br, s_br, name="merge_fwd")
    x2 = mm(merged, wts['w_out'], tm=1024, tn=1024, tk=2048, inner='m', out_dtype=F32, res=x, name="mm_out")
    h2, = take(_rms_fwd(x2, small['ffn_norm_g'], name="rms2_fwd"), None, [], wts)
    up = mm_host(['w_down'], gather, wts,
                 h2, wts['w_up'], tm=1024, tn=1024, tk=2048, out_dtype=BF16, name="mm_up")
    conv_w, conv_b = small['conv_w'], small['conv_b']
    act = _ffn_act_fwd(up, conv_w, conv_b, name="ffn_act_fwd")
    x3 = mm(act, wts['w_down'], tm=1024, tn=1024, tk=2816, out_dtype=F32, res=x2, name="mm_down")
    dx3b, d_g3, loss = _rowwise(
        _final_loss_hook(small['final_norm_g'].reshape(1, D_MODEL), target, tr), x3, tr, name="final_loss")

    sgr['final_norm_g'] = d_g3.reshape(D_MODEL)
    dact = mm(dx3b, wts['w_down'], tb=True, tm=512, tn=2816, tk=2048, inner='m', out_dtype=BF16, name="mm_dact")
    grads['w_down'] = mm(act, dx3b, ta=True, tm=512, tn=1024, tk=2048, out_dtype=BF16, name="mm_dw_down")
    plan = scatter(['w_down'])
    dup, sgr['conv_w'], sgr['conv_b'] = take(
        _ffn_act_bwd(dact, up, conv_w, conv_b, comm=plan, name="ffn_act_bwd"), plan, ['w_down'], recvs)
    grads['w_up'] = mm(h2, dup, ta=True, tm=1024, tn=1024, tk=2048, out_dtype=BF16, name="mm_dw_up")
    dh2 = mm_plan(up_scatter(0), ['w_up#0'], recvs,
                  dup, wts['w_up'], tb=True, tm=1024, tn=1024, tk=2816, out_dtype=F32, name="mm_dh2")
    dx2b, sgr['ffn_norm_g'] = _rowwise(
        _rms_bwd_hook(x2, small['ffn_norm_g'], dx3b, tr, BF16), dh2, tr, name="rms2_bwd")

    d_a, d_s, dga, dgs = mm(dx2b, wts['w_out'], tb=True, tm=1024, tn=GATE_TC, tk=2048,
                            epilogue=_merge_bwd_hook(proj, a_br, s_br, min(1024, l)), name="mm_dmerged")
    grads['w_out'] = mm(merged, dx2b, ta=True, tm=1024, tn=1024, tk=2048, out_dtype=BF16, name="mm_dw_out")
    dattn = mm(d_a, wts['w_ba'], tb=True, tm=1024, tn=1024, tk=2048, inner='m', out_dtype=BF16, name="mm_dattn")
    grads['w_ba'] = mm(attn, d_a, ta=True, tm=1024, tn=1024, tk=2048, out_dtype=BF16, name="mm_dw_ba")
    dz, sgr['b_glu'] = mm(d_s, wts['w_bs'], tb=True, tm=1024, tn=512, tk=2048, sequential=True,
                          epilogue=_glu_bwd_hook(z, min(1024, l)), name="mm_dssm")
    grads['w_bs'] = mm(ssm, d_s, ta=True, tm=512, tn=1024, tk=2048, out_dtype=BF16, name="mm_dw_bs")
    grads['w_glu'] = mm(gy, dz, ta=True, tm=512, tn=1024, tk=2048, out_dtype=BF16, name="mm_dw_glu")
    dgy = mm(dz, wts['w_glu'], tb=True, tm=1024, tn=512, tk=1024, inner='m', out_dtype=F32, name="mm_dgy")
    plan = up_scatter(1)
    du, d_bd, d_cd, d_ab, sgr['ssm_d'] = take(
        _ssm_bwd(proj, y, dgy, xs, ab, bdt_b, cdt_b, dskip, comm=plan, name="ssm_bwd"), plan, ['w_up#1'], recvs)
    keys = ['w_out', 'w_ba', 'w_bs', 'w_glu']
    plan = scatter(keys)
    dq, dkc, dkp, dvc, dvp, dsink = take(
        _attn_bwd(proj, sinks, dattn, comm=plan, name="attn_bwd"), plan, keys, recvs)
    dkv = _kv_grad_merge(dkc, dkp, dvc, dvp, name="kv_grad_merge")
    sgr['attn_sinks'] = dsink[:, :N_Q_HEADS]
    dproj = jnp.concatenate([dq, dkv, du, dga, dgs], axis=1)
    sgr['b_in'] = _col_sum(dproj, name="col_sum_dproj")
    grads['w_in_t'] = mm(dproj, h1, ta=True, tm=2944, tn=1024, tk=1024, out_dtype=BF16, name="mm_dw_in")
    dh1 = mm_host(['w_in_t'], scatter, recvs,
                  dproj, wts['w_in_t'], tm=1024, tn=1024, tk=2944, out_dtype=F32, name="mm_dh1")
    grad_x, sgr['attn_norm_g'] = _rowwise(
        _rms_bwd_hook(x, small['attn_norm_g'], dx2b, tr, F32), dh1, tr, name="rms1_bwd")

    from_bd = lambda t: _block_diag_take(t, SSM_GROUP, SSM_STATE).transpose(0, 1, 3, 2).reshape(
        SSM_GROUPS * SSM_STATE, SSM_GROUP)
    d_bb_re = from_bd(d_bd[:, :, :SSM_X_BLK])
    d_bb_im = from_bd(d_bd[:, :, SSM_X_BLK:])
    d_cdt = d_cd.transpose(0, 2, 1)
    shape_c = (1, SSM_GROUPS, SSM_GROUP, SSM_STATE)
    sgr['ssm_c_re'] = _block_diag_take(d_cdt[:, :, :SSM_X_BLK], SSM_GROUP, SSM_STATE).reshape(shape_c)
    sgr['ssm_c_im'] = -_block_diag_take(d_cdt[:, :, SSM_X_BLK:], SSM_GROUP, SSM_STATE).reshape(shape_c)
    d_a_re, d_a_im, d_ldt, d_b_re, d_b_im = _ssm_disc_bwd(
        a_re, a_im, log_dt, b_re, b_im, d_ab[:, 0, :].reshape(-1, 1), d_ab[:, 1, :].reshape(-1, 1),
        d_bb_re, d_bb_im, name="ssm_disc_bwd")
    sgr['ssm_a_re'] = d_a_re.reshape(1, SSM_GROUPS, SSM_STATE)
    sgr['ssm_a_im'] = d_a_im.reshape(1, SSM_GROUPS, SSM_STATE)
    sgr['ssm_log_dt'] = d_ldt.reshape(SSM_GROUPS, SSM_STATE).sum(axis=1).reshape(1, SSM_GROUPS)
    sgr['ssm_b_re'] = d_b_re.reshape(1, SSM_GROUPS, SSM_STATE, SSM_GROUP)
    sgr['ssm_b_im'] = d_b_im.reshape(1, SSM_GROUPS, SSM_STATE, SSM_GROUP)
    return loss, grad_x, grads, recvs, sgr


def _swap_cores(arrs, *, name):
    n = len(arrs)

    def body(*refs):
        ins, outs = refs[:n], refs[n:2 * n]
        send_sems, recv_sems = refs[2 * n:]
        x, y, c = _place()
        copies = []
        for i in range(n):
            cp = pltpu.make_async_remote_copy(
                src_ref=ins[i], dst_ref=outs[i], send_sem=send_sems.at[i], recv_sem=recv_sems.at[i],
                device_id=(x, y, 1 - c), device_id_type=MESH)
            cp.start()
            copies.append(cp)
        for cp in copies:
            cp.wait()

    return pl.pallas_call(
        body, in_specs=[ANY] * n, out_specs=[ANY] * n,
        out_shape=[jax.ShapeDtypeStruct(a.shape, a.dtype) for a in arrs],
        scratch_shapes=[pltpu.SemaphoreType.DMA((n,)), pltpu.SemaphoreType.DMA((n,))],
        name=name)(*arrs)


def _all_reduce_small(buf, *, name):
    r = buf.shape[0]

    def body(in_ref, out_ref, slots, send_sems, recv_sems):
        x, y, c = _place()
        me = 4 * x + 2 * y + c
        slots[pl.ds(me, 1)] = in_ref[...][None]
        copies = []
        for k in range(N_DEV - 1):
            bx, by, bc = ((k + 1) >> 2) & 1, ((k + 1) >> 1) & 1, (k + 1) & 1
            peer = (1 - x if bx else x, 1 - y if by else y, 1 - c if bc else c)
            cp = pltpu.make_async_remote_copy(
                src_ref=in_ref, dst_ref=slots.at[me], send_sem=send_sems.at[k], recv_sem=recv_sems.at[k],
                device_id=peer, device_id_type=MESH)
            cp.start()
            copies.append(cp)
        for cp in copies:
            cp.wait()
        acc = slots[0]
        for d in range(1, N_DEV):
            acc = acc + slots[d]
        out_ref[...] = acc

    vm = pl.BlockSpec(memory_space=pltpu.VMEM)
    return pl.pallas_call(
        body, in_specs=[vm], out_specs=vm, out_shape=jax.ShapeDtypeStruct((r, 128), F32),
        scratch_shapes=[pltpu.VMEM((N_DEV, r, 128), F32), pltpu.SemaphoreType.DMA((N_DEV - 1,)),
                        pltpu.SemaphoreType.DMA((N_DEV - 1,))],
        name=name)(buf)


def _pack(arrs):
    flat = jnp.concatenate([a.reshape(-1).astype(F32) for a in arrs])
    pad = (-flat.shape[0]) % 1024
    return jnp.pad(flat, (0, pad)).reshape(-1, 128)


def _unpack(buf, shapes):
    flat = buf.reshape(-1)
    out, pos = [], 0
    for s in shapes:
        size = math.prod(s)
        out.append(flat[pos:pos + size].reshape(s))
        pos += size
    return out


TILE_ELEMS = 256 * 1024


def _tile_rows(r, c):
    if r * c <= TILE_ELEMS:
        return r
    for tr in range(TILE_ELEMS // c // 16 * 16, 0, -16):
        if r % tr == 0:
            return tr
    raise ValueError((r, c))


def _sum4(own, recvs, *, name):
    r, c = own.shape
    parts = len(recvs)
    tr = _tile_rows(r // parts, c)
    per = r // parts // tr

    def body(o_ref, *refs):
        out_ref = refs[parts]
        for p in range(parts):
            @pl.when(pl.program_id(0) // per == p)
            def _():
                acc = o_ref[...].astype(F32)
                for k in range(3):
                    acc = acc + refs[p][k].astype(F32)
                out_ref[...] = acc.astype(BF16)

    part_spec = lambda p: pl.BlockSpec((3, tr, c), lambda i: (0, jnp.clip(i - p * per, 0, per - 1), 0))
    return pl.pallas_call(
        body, grid=(r // tr,),
        in_specs=[pl.BlockSpec((tr, c), lambda i: (i, 0))] + [part_spec(p) for p in range(parts)],
        out_specs=pl.BlockSpec((tr, c), lambda i: (i, 0)), out_shape=jax.ShapeDtypeStruct((r, c), BF16),
        name=name, compiler_params=_cp(("parallel",)))(own, *recvs)


def _adam_step(w, g, m, v):
    bc1 = 1.0 - ADAM_B1 ** ADAM_STEP
    bc2 = 1.0 - ADAM_B2 ** ADAM_STEP
    mn = ADAM_B1 * m + (1.0 - ADAM_B1) * g
    vn = ADAM_B2 * v + (1.0 - ADAM_B2) * (g * g)
    m_hat = mn / bc1
    v_hat = vn / bc2
    return -ADAM_LR * (m_hat / (jnp.sqrt(v_hat) + ADAM_EPS) + ADAM_WD * w), mn, vn


def _adamw(w, ga, gb, m, v, *, name):
    r, c = w.shape
    tr = _tile_rows(r, c)

    def body(w_ref, ga_ref, gb_ref, m_ref, v_ref, g_out, d_out, m_out, v_out):
        g = ga_ref[...].astype(F32) + gb_ref[...].astype(F32)
        g_out[...] = g
        d_out[...], m_out[...], v_out[...] = _adam_step(w_ref[...], g, m_ref[...], v_ref[...])

    spec = pl.BlockSpec((tr, c), lambda i: (i, 0))
    shp = jax.ShapeDtypeStruct((r, c), F32)
    return pl.pallas_call(
        body, grid=(r // tr,), in_specs=[spec] * 5, out_specs=[spec] * 4,
        out_shape=[shp] * 4, name=name, compiler_params=_cp(("parallel",)))(w, ga, gb, m, v)


def _lanes(t):
    return t.reshape(-1, 128) if t.size % 128 == 0 else t.reshape(1, -1)


def _adamw_small(ws, gs, ms, vs, *, name):
    n = len(ws)

    def body(*refs):
        for i in range(n):
            w_ref, g_ref, m_ref, v_ref = (refs[k * n + i] for k in range(4))
            outs = [refs[(4 + k) * n + i] for k in range(3)]
            outs[0][...], outs[1][...], outs[2][...] = _adam_step(w_ref[...], g_ref[...], m_ref[...], v_ref[...])

    flat = [_lanes(t) for group in (ws, gs, ms, vs) for t in group]
    shp = [jax.ShapeDtypeStruct(_lanes(t).shape, F32) for t in ws]
    res = pl.pallas_call(body, out_shape=shp * 3, name=name)(*flat)
    return [[res[k * n + i].reshape(ws[i].shape) for i in range(n)] for k in range(3)]


BIG = ['w_in', 'w_glu', 'w_branch_attn', 'w_branch_ssm', 'w_out', 'w_up', 'w_down']
BIG_KEY = {'w_in': 'w_in_t', 'w_glu': 'w_glu', 'w_branch_attn': 'w_ba', 'w_branch_ssm': 'w_bs',
           'w_out': 'w_out', 'w_up': 'w_up', 'w_down': 'w_down'}
TRANSPOSED = {'w_in'}
SMALL = ['attn_norm_g', 'b_in', 'attn_sinks', 'ssm_a_re', 'ssm_a_im', 'ssm_log_dt', 'ssm_b_re', 'ssm_b_im',
         'ssm_c_re', 'ssm_c_im', 'ssm_d', 'b_glu', 'ffn_norm_g', 'conv_b', 'final_norm_g']
WEIGHTS = ['attn_norm_g', 'w_in', 'b_in', 'attn_sinks', 'ssm_a_re', 'ssm_a_im', 'ssm_log_dt', 'ssm_b_re',
           'ssm_b_im', 'ssm_c_re', 'ssm_c_im', 'ssm_d', 'w_glu', 'b_glu', 'w_branch_attn', 'w_branch_ssm',
           'w_out', 'ffn_norm_g', 'w_up', 'conv_w', 'conv_b', 'w_down', 'final_norm_g']


def _shard_2d(name, t):
    t = t[0]
    return t.T if name in TRANSPOSED else t


def _unshard_2d(name, t):
    return (t.T if name in TRANSPOSED else t)[None]


def kernel(x, attn_norm_g, w_in, b_in, attn_sinks, ssm_a_re, ssm_a_im, ssm_log_dt, ssm_b_re, ssm_b_im, ssm_c_re, ssm_c_im, ssm_d, w_glu, b_glu, w_branch_attn, w_branch_ssm, w_out, ffn_norm_g, w_up, conv_w, conv_b, w_down, final_norm_g, loss_target, m_attn_norm_g, m_w_in, m_b_in, m_attn_sinks, m_ssm_a_re, m_ssm_a_im, m_ssm_log_dt, m_ssm_b_re, m_ssm_b_im, m_ssm_c_re, m_ssm_c_im, m_ssm_d, m_w_glu, m_b_glu, m_w_branch_attn, m_w_branch_ssm, m_w_out, m_ffn_norm_g, m_w_up, m_conv_w, m_conv_b, m_w_down, m_final_norm_g, v_attn_norm_g, v_w_in, v_b_in, v_attn_sinks, v_ssm_a_re, v_ssm_a_im, v_ssm_log_dt, v_ssm_b_re, v_ssm_b_im, v_ssm_c_re, v_ssm_c_im, v_ssm_d, v_w_glu, v_b_glu, v_w_branch_attn, v_w_branch_ssm, v_w_out, v_ffn_norm_g, v_w_up, v_conv_w, v_conv_b, v_w_down, v_final_norm_g):
    args = dict(locals())
    w = {n: args[n] for n in WEIGHTS}
    m = {n: args['m_' + n] for n in WEIGHTS}
    v = {n: args['v_' + n] for n in WEIGHTS}
    xi, yi, ci = _place()
    blk = 2 * xi + yi

    shards = {BIG_KEY[n]: _shard_2d(n, w[n]).astype(BF16) for n in BIG}
    cw_cols = w['conv_w'].shape[2]
    cw_place = lax.dynamic_update_slice(jnp.zeros((3, D_FF), F32), w['conv_w'][0] * (ci == 0).astype(F32),
                                        (0, blk * cw_cols))
    conv_w_full = _unpack(_all_reduce_small(_pack([cw_place]), name="gather_conv_w"), [(3, D_FF)])[0]

    small = {n: w[n] for n in SMALL}
    small['conv_w'] = conv_w_full
    loss_part, grad_x, grads, recvs, sgr = _local_step(x[0], loss_target[0], {}, small, shards)

    halves = []
    for n in BIG:
        key = BIG_KEY[n]
        full = grads[key]
        recv = [recvs[key]] if key in recvs else [recvs[key + '#0'], recvs[key + '#1']]
        axis, interleaved = LAYOUT[key]
        size = full.shape[axis] // N_CHIPS
        own = lax.dynamic_slice_in_dim(full, _block_pos(xi, yi, interleaved) * size, size, axis=axis)
        halves.append(_sum4(own, recv, name="sum4_" + n))
    others = _swap_cores(halves, name="swap_cores")
    out = {}
    for n, mine, other in zip(BIG, halves, others):
        res = _adamw(_shard_2d(n, w[n]), mine, other, _shard_2d(n, m[n]), _shard_2d(n, v[n]), name="adamw_" + n)
        out[n] = [_unshard_2d(n, t) for t in res]

    names = SMALL + ['conv_w']
    shapes = [w[n].shape for n in SMALL] + [(3, D_FF)]
    packed = _pack([sgr[n] for n in names] + [loss_part])
    summed = _unpack(_all_reduce_small(packed, name="all_reduce_small"), shapes + [(1, 1)])
    loss = summed[-1].reshape(())
    sg = dict(zip(names, summed[:-1]))
    sg['conv_w'] = lax.dynamic_slice_in_dim(sg['conv_w'], blk * cw_cols, cw_cols, axis=1)[None]
    deltas, new_m, new_v = _adamw_small([w[n] for n in names], [sg[n] for n in names], [m[n] for n in names],
                                        [v[n] for n in names], name="adamw_small")
    for i, n in enumerate(names):
        out[n] = [sg[n], deltas[i], new_m[i], new_v[i]]

    return (loss, grad_x[None], *[out[n][0] for n in WEIGHTS], *[out[n][1] for n in WEIGHTS],
            *[out[n][2] for n in WEIGHTS], *[out[n][3] for n in WEIGHTS])
```

```python
import functools
import math

import jax
import jax.numpy as jnp
from jax import lax
from jax.experimental import pallas as pl
from jax.experimental.pallas import tpu as pltpu

F32 = jnp.float32
BF16 = jnp.bfloat16

D_MODEL = 2048
N_Q_HEADS = 16
HEAD_DIM = 64
ATTN_WIDTH = 1024
KV_WIDTH = 128
BLOCK = 128
SSM_WIDTH = 512
SSM_GROUPS = 32
SSM_GROUP = 16
SSM_STATE = 64
D_FF = 5632
IN_COLS = 5888
RMS_EPS = 1e-6
NEG_BIG = -1e30
N_CHIPS = 4
N_DEV = 8

COL_K = 8
COL_V = 9
COL_U = 10
COL_GA = 14
COL_GS = 30

SSM_SPLIT = 4
SSM_U_BLK = 128
SSM_X_BLK = 512
SSM_CHUNK = 256

ADAM_LR = 0.001
ADAM_B1 = 0.9
ADAM_B2 = 0.999
ADAM_EPS = 1e-08
ADAM_WD = 0.01
ADAM_STEP = 10

VMEM_LIMIT_BYTES = 56 * 1024 * 1024
INV_SQRT2 = 1.0 / math.sqrt(2.0)
INV_SQRT2PI = 1.0 / math.sqrt(2.0 * math.pi)
MESH = pl.DeviceIdType.MESH
ANY = pl.BlockSpec(memory_space=pl.ANY)


def _cp(sem):
    return pltpu.CompilerParams(dimension_semantics=sem, vmem_limit_bytes=VMEM_LIMIT_BYTES)


def _gelu(x):
    return 0.5 * x * (1.0 + lax.erf(x * INV_SQRT2))


def _gelu_grad(x):
    return 0.5 * (1.0 + lax.erf(x * INV_SQRT2)) + x * jnp.exp(-0.5 * x * x) * INV_SQRT2PI


def _sigmoid(x):
    return 1.0 / (1.0 + jnp.exp(-x))


def _place():
    return lax.axis_index("x"), lax.axis_index("y"), lax.axis_index("c")


def _other_chips(x, y):
    return [(1 - x, y), (x, 1 - y), (1 - x, 1 - y)]


def _block_pos(x, y, interleaved):
    return x + 2 * y if interleaved else 2 * x + y


LAYOUT = {'w_in_t': (0, False), 'w_glu': (1, False), 'w_ba': (1, False), 'w_bs': (1, False),
          'w_out': (0, False), 'w_up': (1, True), 'w_down': (0, False)}


def _window(ref, axis, pos, size, rows=None):
    if axis == 0:
        start, count = (0, size) if rows is None else rows
        return ref.at[pl.ds(pos * size + start, count), :]
    cols = pl.ds(pos * size, size)
    return ref.at[:, cols] if rows is None else ref.at[pl.ds(rows[0], rows[1]), cols]


def _gathered_shape(shape, axis):
    return tuple(N_CHIPS * d if a == axis else d for a, d in enumerate(shape))


def _block_shape(shape, axis):
    return tuple(d // N_CHIPS if a == axis else d for a, d in enumerate(shape))


class _GatherPlan:
    def __init__(self, shards, layouts):
        self.arrays = list(shards)
        self.layouts = list(layouts)
        n = len(shards)
        self.out_shape = [jax.ShapeDtypeStruct(_gathered_shape(s.shape, lay[0]), s.dtype)
                          for s, lay in zip(shards, layouts)]
        self.scratch = [pltpu.SemaphoreType.DMA((6 * n,)), pltpu.SemaphoreType.DMA((6 * n,)),
                        pltpu.SemaphoreType.DMA((n,))]

    def _copies(self, kind, ins, outs, sems):
        send, recv, local = sems
        n = len(self.arrays)
        x, y, c = _place()
        copies = []
        for i in range(n):
            axis, interleaved = self.layouts[i]
            size = self.arrays[i].shape[axis]
            h = self.arrays[i].shape[0] // 2
            first = lambda core: core * h
            blk = _block_pos(x, y, interleaved)
            if kind == 'mine':
                copies.append(pltpu.make_async_copy(ins[i], _window(outs[i], axis, blk, size), local.at[i]))
                continue
            for k, (px, py) in enumerate(_other_chips(x, y)):
                theirs = _block_pos(px, py, interleaved)
                if kind in ('ici_out', 'ici_in'):
                    route = dict(send_sem=send.at[3 * i + k], recv_sem=recv.at[3 * i + k],
                                 device_id=(px, py, c), device_id_type=MESH)
                else:
                    route = dict(send_sem=send.at[3 * (n + i) + k], recv_sem=recv.at[3 * (n + i) + k],
                                 device_id=(x, y, 1 - c), device_id_type=MESH)
                if kind == 'ici_out':
                    src, dst = ins[i].at[pl.ds(first(c), h), :], _window(outs[i], axis, blk, size, (first(c), h))
                elif kind == 'd2d_in':
                    src = dst = _window(outs[i], axis, theirs, size, (first(1 - c), h))
                else:
                    src = dst = _window(outs[i], axis, theirs, size, (first(c), h))
                copies.append(pltpu.make_async_remote_copy(src_ref=src, dst_ref=dst, **route))
        return copies

    def start(self, ins, outs, sems):
        for cp in self._copies('mine', ins, outs, sems) + self._copies('ici_out', ins, outs, sems):
            cp.start()

    def middle(self, ins, outs, sems):
        for arrived, onward in zip(self._copies('ici_in', ins, outs, sems), self._copies('d2d_out', ins, outs, sems)):
            arrived.wait_recv()
            onward.start()

    def finish(self, ins, outs, sems):
        for cp in self._copies('d2d_in', ins, outs, sems):
            cp.wait_recv()
        for cp in self._copies('ici_out', ins, outs, sems) + self._copies('d2d_out', ins, outs, sems):
            cp.wait_send()
        for cp in self._copies('mine', ins, outs, sems):
            cp.wait()


class _ScatterPlan:
    def __init__(self, fulls, layouts, part=(0, 1)):
        self.arrays = list(fulls)
        self.layouts = list(layouts)
        self.part = part
        n = len(fulls)
        self.out_shape = []
        for f, lay in zip(fulls, layouts):
            rows, cols = _block_shape(f.shape, lay[0])
            self.out_shape.append(jax.ShapeDtypeStruct((3, rows // part[1], cols), f.dtype))
        self.scratch = [pltpu.SemaphoreType.DMA((3 * n,)), pltpu.SemaphoreType.DMA((3 * n,))]

    def _copies(self, ins, outs, sems):
        send, recv = sems
        x, y, c = _place()
        copies = []
        for i in range(len(self.arrays)):
            axis, interleaved = self.layouts[i]
            size = self.arrays[i].shape[axis] // N_CHIPS
            h = _block_shape(self.arrays[i].shape, axis)[0] // self.part[1]
            rows = (self.part[0] * h, h)
            for k, (px, py) in enumerate(_other_chips(x, y)):
                copies.append(pltpu.make_async_remote_copy(
                    src_ref=_window(ins[i], axis, _block_pos(px, py, interleaved), size, rows), dst_ref=outs[i].at[k],
                    send_sem=send.at[3 * i + k], recv_sem=recv.at[3 * i + k],
                    device_id=(px, py, c), device_id_type=MESH))
        return copies

    def start(self, ins, outs, sems):
        for cp in self._copies(ins, outs, sems):
            cp.start()

    def middle(self, ins, outs, sems):
        pass

    def finish(self, ins, outs, sems):
        for cp in self._copies(ins, outs, sems):
            cp.wait()


def _hosted_call(body, *, grid, in_specs, out_specs, out_shape, scratch_shapes, sem, name, args, comm=None,
                 aliases=None):
    aliases = aliases or {}
    if comm is None:
        outs = pl.pallas_call(body, grid=grid, in_specs=in_specs, out_specs=out_specs, out_shape=out_shape,
                              scratch_shapes=scratch_shapes, name=name, input_output_aliases=aliases,
                              compiler_params=_cp(sem))(*args)
        return outs, None
    n_in, n_out, n_scr = len(in_specs), len(out_specs), len(scratch_shapes)
    nc, ns = len(comm.arrays), len(comm.scratch)
    total = math.prod(grid)
    mid = total - max(1, total // 8)

    def wrapped(*refs):
        pos = 0
        ins = refs[pos:pos + n_in]; pos += n_in
        cins = refs[pos:pos + nc]; pos += nc
        outs = refs[pos:pos + n_out]; pos += n_out
        couts = refs[pos:pos + nc]; pos += nc
        scr = refs[pos:pos + n_scr]; pos += n_scr
        sems = refs[pos:pos + ns]
        step = 0
        for ax, g in enumerate(grid):
            step = step * g + pl.program_id(ax)

        @pl.when(step == 0)
        def _():
            comm.start(cins, couts, sems)

        body(*ins, *outs, *scr)

        @pl.when(step == mid)
        def _():
            comm.middle(cins, couts, sems)

        @pl.when(step == total - 1)
        def _():
            comm.finish(cins, couts, sems)

    res = pl.pallas_call(
        wrapped, grid=grid, in_specs=list(in_specs) + [ANY] * nc, out_specs=list(out_specs) + [ANY] * nc,
        out_shape=list(out_shape) + list(comm.out_shape), scratch_shapes=list(scratch_shapes) + list(comm.scratch),
        name=name, input_output_aliases=aliases,
        compiler_params=_cp(("arbitrary",) * len(grid)))(*args, *comm.arrays)
    return res[:n_out], res[n_out:]


class _Hook:
    def __init__(self, fn, ins=(), in_specs=(), outs=()):
        self.fn, self.ins, self.in_specs, self.outs = fn, list(ins), list(in_specs), list(outs)


def _matmul(a, b, *, ta=False, tb=False, tm, tn, tk, out_dtype=None, bias=None, res=None, inner='n',
            comm=None, prologue=None, epilogue=None, a_shape=None, sequential=False, name):
    if a is None:
        m, kdim = a_shape
    elif ta:
        kdim, m = a.shape
    else:
        m, kdim = a.shape
    if tb:
        n, k2 = b.shape
    else:
        k2, n = b.shape
    assert kdim == k2, (name, kdim, b.shape)
    tm, tn, tk = min(tm, m), min(tn, n), min(tk, kdim)
    assert m % tm == 0 and n % tn == 0 and kdim % tk == 0, (name, m, n, kdim, tm, tn, tk)
    nk = kdim // tk
    dn = (((0 if ta else 1,), (1 if tb else 0,)), ((), ()))
    hooks = [h for h in (prologue, epilogue) if h is not None]
    n_pro_in = len(prologue.ins) if prologue else 0
    n_epi_in = len(epilogue.ins) if epilogue else 0
    n_pro_out = len(prologue.outs) if prologue else 0
    n_epi_out = len(epilogue.outs) if epilogue else 0
    if inner == 'n':
        grid = (m // tm, n // tn, nk)
        mi = lambda g0, g1: g0
        ni = lambda g0, g1: g1
    else:
        grid = (n // tn, m // tm, nk)
        mi = lambda g0, g1: g1
        ni = lambda g0, g1: g0

    def body(*refs):
        refs = list(refs)
        take = lambda cnt: [refs.pop(0) for _ in range(cnt)]
        a_ref = take(1)[0] if a is not None else None
        b_ref = take(1)[0]
        bias_ref = take(1)[0] if bias is not None else None
        res_ref = take(1)[0] if res is not None else None
        pro_in, epi_in = take(n_pro_in), take(n_epi_in)
        o_ref = take(1)[0] if epilogue is None else None
        pro_out, epi_out = take(n_pro_out), take(n_epi_out)
        i, j, k = mi(pl.program_id(0), pl.program_id(1)), ni(pl.program_id(0), pl.program_id(1)), pl.program_id(2)

        def finish(src):
            def result(rows=slice(None)):
                r = src[rows, :]
                if bias_ref is not None:
                    r = r + bias_ref[...]
                if res_ref is not None:
                    r = r + res_ref[rows, :]
                return r

            if epilogue is None:
                o_ref[...] = result().astype(out_dtype)
            else:
                epilogue.fn(result, epi_in, epi_out, i, j)

        a_val = a_ref[...] if prologue is None else prologue.fn(a_ref, pro_in, pro_out, i, k)
        prod = lax.dot_general(a_val.astype(BF16), b_ref[...].astype(BF16), dn, preferred_element_type=F32)
        if nk == 1:
            finish(prod)
            return
        acc_ref = refs[0]

        @pl.when(k == 0)
        def _():
            acc_ref[...] = prod

        @pl.when(k > 0)
        def _():
            acc_ref[...] += prod

        @pl.when(k == nk - 1)
        def _():
            finish(acc_ref)

    spec = lambda shape, fn: pl.BlockSpec(shape, lambda g0, g1, k: fn(mi(g0, g1), ni(g0, g1), k))
    in_specs, args = [], []
    if a is not None:
        in_specs.append(spec((tk, tm), lambda i, j, k: (k, i)) if ta else spec((tm, tk), lambda i, j, k: (i, k)))
        args.append(a)
    in_specs.append(spec((tn, tk), lambda i, j, k: (j, k)) if tb else spec((tk, tn), lambda i, j, k: (k, j)))
    args.append(b)
    if bias is not None:
        in_specs.append(spec((1, tn), lambda i, j, k: (0, j)))
        args.append(bias)
    if res is not None:
        in_specs.append(spec((tm, tn), lambda i, j, k: (i, j)))
        args.append(res)
    for h in hooks:
        in_specs += [spec(shape, fn) for shape, fn in h.in_specs]
        args += h.ins
    out_specs, out_shape = [], []
    if epilogue is None:
        out_specs.append(spec((tm, tn), lambda i, j, k: (i, j)))
        out_shape.append(jax.ShapeDtypeStruct((m, n), out_dtype))
    for h in hooks:
        out_specs += [spec(blk, fn) for _, _, blk, fn in h.outs]
        out_shape += [jax.ShapeDtypeStruct(shape, dtype) for shape, dtype, _, _ in h.outs]
    outs, couts = _hosted_call(
        body, grid=grid, in_specs=in_specs, out_specs=out_specs, out_shape=out_shape,
        scratch_shapes=[pltpu.VMEM((tm, tn), F32)] if nk > 1 else [],
        sem=("arbitrary",) * 3 if sequential else ("parallel", "parallel", "arbitrary"),
        name=name, args=args, comm=comm)
    outs = outs[0] if not hooks else outs
    return outs if comm is None else (outs, couts)


def _rms_fwd(x, g, *, comm=None, name):
    l, d = x.shape
    tr = min(256, l)

    def body(x_ref, g_ref, h_ref):
        xf = x_ref[...]
        r = lax.rsqrt(jnp.mean(xf * xf, axis=-1, keepdims=True) + RMS_EPS)
        h_ref[...] = ((xf * r) * g_ref[...]).astype(BF16)

    row = pl.BlockSpec((tr, d), lambda i: (i, 0))
    return _hosted_call(
        body, grid=(l // tr,), in_specs=[row, pl.BlockSpec((1, d), lambda i: (0, 0))],
        out_specs=[row], out_shape=[jax.ShapeDtypeStruct((l, d), BF16)], scratch_shapes=[],
        sem=("parallel",), name=name, args=(x, g), comm=comm)


EPI_ROWS = 128
ROW_TILE = 256


def _row_chunks(tm):
    ch = min(EPI_ROWS, tm)
    return [slice(c * ch, (c + 1) * ch) for c in range(tm // ch)]


def _rowwise(hook, src, tm, *, name):
    l, d = src.shape
    n_in = len(hook.ins)

    def body(*refs):
        src_ref, ins, outs = refs[0], refs[1:1 + n_in], refs[1 + n_in:]
        hook.fn(lambda rows=slice(None): src_ref[rows, :], ins, outs, pl.program_id(0), 0)

    spec = lambda shape, fn: pl.BlockSpec(shape, lambda i: fn(i, 0, 0))
    return pl.pallas_call(
        body, grid=(l // tm,),
        in_specs=[pl.BlockSpec((tm, d), lambda i: (i, 0))] + [spec(shape, fn) for shape, fn in hook.in_specs],
        out_specs=[spec(blk, fn) for _, _, blk, fn in hook.outs],
        out_shape=[jax.ShapeDtypeStruct(shape, dtype) for shape, dtype, _, _ in hook.outs],
        name=name, compiler_params=_cp(("arbitrary",)))(src, *hook.ins)


def _rms_bwd_hook(x, g, dres, tm, out_dtype):
    def fn(result, ins, outs, i, j):
        x_ref, g_ref, dres_ref = ins
        dx_ref, dg_ref = outs

        @pl.when(i == 0)
        def _():
            dg_ref[...] = jnp.zeros_like(dg_ref)

        for rows in _row_chunks(tm):
            dyv = result(rows)
            xf = x_ref[rows, :]
            r = lax.rsqrt(jnp.mean(xf * xf, axis=-1, keepdims=True) + RMS_EPS)
            xhat = xf * r
            dxh = dyv * g_ref[...]
            dx = r * (dxh - xhat * jnp.mean(dxh * xhat, axis=-1, keepdims=True)) + dres_ref[rows, :].astype(F32)
            dx_ref[rows, :] = dx.astype(out_dtype)
            dg_ref[...] += jnp.sum(dyv * xhat, axis=0, keepdims=True)

    l, d = x.shape
    row = lambda i, j, k: (i, 0)
    vec = lambda i, j, k: (0, 0)
    return _Hook(fn, ins=[x, g, dres], in_specs=[((tm, d), row), ((1, d), vec), ((tm, d), row)],
                 outs=[((l, d), out_dtype, (tm, d), row), ((1, d), F32, (1, d), vec)])


def _final_loss_hook(g, target, tm):
    l, d = target.shape

    def fn(result, ins, outs, i, j):
        g_ref, t_ref = ins
        dxb_ref, dg_ref, loss_ref = outs
        gv = g_ref[...]

        @pl.when(i == 0)
        def _():
            dg_ref[...] = jnp.zeros_like(dg_ref)
            loss_ref[...] = jnp.zeros_like(loss_ref)

        for rows in _row_chunks(tm):
            xf = result(rows)
            r = lax.rsqrt(jnp.mean(xf * xf, axis=-1, keepdims=True) + RMS_EPS)
            xhat = xf * r
            diff = xhat * gv - t_ref[rows, :]
            dout = diff * (1.0 / d)
            dxh = dout * gv
            dx = r * (dxh - xhat * jnp.mean(dxh * xhat, axis=-1, keepdims=True))
            dxb_ref[rows, :] = dx.astype(BF16)
            dg_ref[...] += jnp.sum(dout * xhat, axis=0, keepdims=True)
            part = jnp.sum(jnp.mean(diff * diff, axis=-1, keepdims=True), axis=0, keepdims=True)
            loss_ref[...] += 0.5 * part

    row = lambda i, j, k: (i, 0)
    vec = lambda i, j, k: (0, 0)
    return _Hook(fn, ins=[g, target], in_specs=[((1, d), vec), ((tm, d), row)],
                 outs=[((l, d), BF16, (tm, d), row), ((1, d), F32, (1, d), vec), ((1, 1), F32, (1, 1), vec)])


Q_PER_KV = 8
GROUP_ROWS = Q_PER_KV * BLOCK


def _attn_masks(n, rows=GROUP_ROWS):
    q_idx = lax.broadcasted_iota(jnp.int32, (rows, 2 * BLOCK), 0) & (BLOCK - 1)
    s_idx = lax.broadcasted_iota(jnp.int32, (rows, 2 * BLOCK), 1)
    dist = q_idx + BLOCK - s_idx
    valid = (dist >= 0) & (dist < BLOCK) & ((n > 0) | (s_idx >= BLOCK))
    return dist.astype(F32), valid


def _dup_half(t, kv_head, lo):
    rolled = pltpu.roll(t, HEAD_DIM, axis=1)
    return jnp.where(lo, t, rolled) if kv_head == 0 else jnp.where(lo, rolled, t)


def _stack_heads(ref, kv_head, lo):
    pieces = []
    for r in range(Q_PER_KV):
        pair = kv_head * 4 + r // 2
        t = ref[:, pair * 128:(pair + 1) * 128].astype(BF16)
        sel = lo if r % 2 == 0 else jnp.logical_not(lo)
        pieces.append(jnp.where(sel, t, jnp.zeros_like(t)))
    return jnp.concatenate(pieces, axis=0)


def _unstack_heads(t, lo):
    return [jnp.where(lo, t[(2 * i) * BLOCK:(2 * i + 1) * BLOCK], t[(2 * i + 1) * BLOCK:(2 * i + 2) * BLOCK])
            for i in range(Q_PER_KV // 2)]


def _per_head_column(values):
    return jnp.concatenate([jnp.full((BLOCK, 1), v, F32) for v in values], axis=0)


def _group_probs(qm, kdup, dist, valid, sink_ref, kv_head):
    heads = [kv_head * Q_PER_KV + r for r in range(Q_PER_KV)]
    slope = _per_head_column([2.0 ** (-8.0 * (h + 1) / N_Q_HEADS) for h in heads])
    sink = _per_head_column([sink_ref[h] for h in heads])
    return _probs(qm, kdup, dist, valid, sink, slope)


def _probs(qm, kdup, dist, valid, sink, slope):
    s = lax.dot_general(qm, kdup, (((1,), (1,)), ((), ())), preferred_element_type=F32)
    s = s * (HEAD_DIM ** -0.5) - slope * dist
    s = jnp.where(valid, s, NEG_BIG)
    m = jnp.maximum(jnp.max(s, axis=-1, keepdims=True), sink)
    p = jnp.exp(s - m)
    esink = jnp.exp(sink - m)
    inv = 1.0 / (jnp.sum(p, axis=-1, keepdims=True) + esink)
    return p * inv, esink * inv


def _attn_fwd(proj, sinks, *, name):
    l = proj.shape[0]
    nb = l // BLOCK

    def body(sink_ref, q_ref, kc_ref, kp_ref, vc_ref, vp_ref, o_ref):
        n = pl.program_id(0)
        dist, valid = _attn_masks(n, BLOCK)
        lo = lax.broadcasted_iota(jnp.int32, (1, BLOCK), 1) < HEAD_DIM
        kx = jnp.concatenate([kp_ref[...], kc_ref[...]], axis=0).astype(BF16)
        vx = jnp.concatenate([vp_ref[...], vc_ref[...]], axis=0).astype(BF16)
        for kv_head in range(2):
            kdup = _dup_half(kx, kv_head, lo)
            vdup = _dup_half(vx, kv_head, lo)
            for pr in range(4):
                pair = kv_head * 4 + pr
                qp = q_ref[:, pair * 128:(pair + 1) * 128].astype(BF16)
                o_pair = jnp.zeros((BLOCK, 128), F32)
                for half in range(2):
                    head = 2 * pair + half
                    sel = lo if half == 0 else jnp.logical_not(lo)
                    qm = jnp.where(sel, qp, jnp.zeros_like(qp))
                    p, _ = _probs(qm, kdup, dist, valid, sink_ref[head], 2.0 ** (-8.0 * (head + 1) / N_Q_HEADS))
                    o = jnp.dot(p.astype(BF16), vdup, preferred_element_type=F32)
                    o_pair = o_pair + jnp.where(sel, o, 0.0)
                o_ref[:, pair * 128:(pair + 1) * 128] = o_pair.astype(BF16)

    kv = lambda col, prev: pl.BlockSpec(
        (BLOCK, KV_WIDTH), (lambda n: (jnp.maximum(n - 1, 0), col)) if prev else (lambda n: (n, col)))
    return pl.pallas_call(
        body, grid=(nb,),
        in_specs=[pl.BlockSpec(memory_space=pltpu.SMEM),
                  pl.BlockSpec((BLOCK, ATTN_WIDTH), lambda n: (n, 0)),
                  kv(COL_K, False), kv(COL_K, True), kv(COL_V, False), kv(COL_V, True)],
        out_specs=pl.BlockSpec((BLOCK, ATTN_WIDTH), lambda n: (n, 0)),
        out_shape=jax.ShapeDtypeStruct((l, ATTN_WIDTH), BF16), name=name,
        compiler_params=_cp(("parallel",)))(sinks, proj, proj, proj, proj, proj)


def _attn_bwd(proj, sinks, dattn, *, comm=None, name):
    l = proj.shape[0]
    nb = l // BLOCK

    def body(sink_ref, q_ref, kc_ref, kp_ref, vc_ref, vp_ref, do_ref,
             dq_ref, dkc_ref, dkp_ref, dvc_ref, dvp_ref, dsink_ref):
        n = pl.program_id(0)
        dist, valid = _attn_masks(n)
        lane = lax.broadcasted_iota(jnp.int32, (1, BLOCK), 1)
        lo = lane < HEAD_DIM
        kx = jnp.concatenate([kp_ref[...], kc_ref[...]], axis=0).astype(BF16)
        vx = jnp.concatenate([vp_ref[...], vc_ref[...]], axis=0).astype(BF16)
        dsink = jnp.zeros((1, BLOCK), F32)
        dk_heads, dv_heads = [], []
        for kv_head in range(2):
            kdup = _dup_half(kx, kv_head, lo)
            vdup = _dup_half(vx, kv_head, lo)
            qm = _stack_heads(q_ref, kv_head, lo)
            dom = _stack_heads(do_ref, kv_head, lo)
            p, psink = _group_probs(qm, kdup, dist, valid, sink_ref, kv_head)
            dp = lax.dot_general(dom, vdup, (((1,), (1,)), ((), ())), preferred_element_type=F32)
            delta = jnp.sum(p * dp, axis=-1, keepdims=True)
            ds = (p * (dp - delta) * (HEAD_DIM ** -0.5)).astype(BF16)
            dsink_rows = -psink * delta
            for r in range(Q_PER_KV):
                part = jnp.sum(dsink_rows[r * BLOCK:(r + 1) * BLOCK])
                dsink = dsink + jnp.where(lane == kv_head * Q_PER_KV + r, part, 0.0)
            dq = jnp.dot(ds, kdup, preferred_element_type=F32)
            for i, dq_pair in enumerate(_unstack_heads(dq, lo)):
                pair = kv_head * 4 + i
                dq_ref[:, pair * 128:(pair + 1) * 128] = dq_pair.astype(BF16)
            dk_acc = lax.dot_general(ds, qm, (((0,), (0,)), ((), ())), preferred_element_type=F32)
            dv_acc = lax.dot_general(p.astype(BF16), dom, (((0,), (0,)), ((), ())), preferred_element_type=F32)
            dk_heads.append(dk_acc + pltpu.roll(dk_acc, HEAD_DIM, axis=1))
            dv_heads.append(dv_acc + pltpu.roll(dv_acc, HEAD_DIM, axis=1))
        dk = jnp.where(lo, dk_heads[0], dk_heads[1])
        dv = jnp.where(lo, dv_heads[0], dv_heads[1])
        dkp_ref[...] = dk[:BLOCK]
        dkc_ref[...] = dk[BLOCK:]
        dvp_ref[...] = dv[:BLOCK]
        dvc_ref[...] = dv[BLOCK:]

        @pl.when(n == 0)
        def _():
            dsink_ref[...] = jnp.zeros_like(dsink_ref)

        dsink_ref[...] += dsink

    kv = lambda col, prev: pl.BlockSpec(
        (BLOCK, KV_WIDTH), (lambda n: (jnp.maximum(n - 1, 0), col)) if prev else (lambda n: (n, col)))
    qspec = pl.BlockSpec((BLOCK, ATTN_WIDTH), lambda n: (n, 0))
    kvout = pl.BlockSpec((BLOCK, KV_WIDTH), lambda n: (n, 0))
    kvshape = jax.ShapeDtypeStruct((l, KV_WIDTH), F32)
    return _hosted_call(
        body, grid=(nb,),
        in_specs=[pl.BlockSpec(memory_space=pltpu.SMEM), qspec,
                  kv(COL_K, False), kv(COL_K, True), kv(COL_V, False), kv(COL_V, True), qspec],
        out_specs=[qspec, kvout, kvout, kvout, kvout, pl.BlockSpec((1, BLOCK), lambda n: (0, 0))],
        out_shape=[jax.ShapeDtypeStruct((l, ATTN_WIDTH), BF16), kvshape, kvshape, kvshape, kvshape,
                   jax.ShapeDtypeStruct((1, BLOCK), F32)],
        scratch_shapes=[], sem=("arbitrary",), name=name,
        args=(sinks, proj, proj, proj, proj, proj, dattn), comm=comm)


def _kv_grad_merge(dkc, dkp, dvc, dvp, *, name):
    l = dkc.shape[0]
    nb = l // BLOCK

    def body(dkc_ref, dkp_ref, dvc_ref, dvp_ref, o_ref):
        last = pl.program_id(0) == nb - 1
        o_ref[:, :KV_WIDTH] = (dkc_ref[...] + jnp.where(last, 0.0, dkp_ref[...])).astype(BF16)
        o_ref[:, KV_WIDTH:] = (dvc_ref[...] + jnp.where(last, 0.0, dvp_ref[...])).astype(BF16)

    cur = pl.BlockSpec((BLOCK, KV_WIDTH), lambda n: (n, 0))
    nxt = pl.BlockSpec((BLOCK, KV_WIDTH), lambda n: (jnp.minimum(n + 1, nb - 1), 0))
    return pl.pallas_call(
        body, grid=(nb,), in_specs=[cur, nxt, cur, nxt],
        out_specs=pl.BlockSpec((BLOCK, 2 * KV_WIDTH), lambda n: (n, 0)),
        out_shape=jax.ShapeDtypeStruct((l, 2 * KV_WIDTH), BF16), name=name,
        compiler_params=_cp(("parallel",)))(dkc, dkp, dvc, dvp)


def _discretize(a_re, a_im, log_dt, b_re, b_im):
    dt = jnp.exp(log_dt)
    mag = jnp.exp(a_re * dt)
    ab_re = mag * jnp.cos(a_im * dt)
    ab_im = mag * jnp.sin(a_im * dt)
    nr = ab_re - 1.0
    ni = ab_im
    den = a_re * a_re + a_im * a_im
    z_re = (nr * a_re + ni * a_im) / den
    z_im = (ni * a_re - nr * a_im) / den
    bb_re = z_re * b_re - z_im * b_im
    bb_im = z_re * b_im + z_im * b_re
    return ab_re, ab_im, bb_re, bb_im


def _ssm_disc_fwd(a_re, a_im, log_dt, b_re, b_im, *, name):
    def body(ar, ai, ld, br, bi, o_ar, o_ai, o_br, o_bi):
        r = _discretize(ar[...], ai[...], ld[...], br[...], bi[...])
        o_ar[...], o_ai[...], o_br[...], o_bi[...] = r

    col = jax.ShapeDtypeStruct(a_re.shape, F32)
    mat = jax.ShapeDtypeStruct(b_re.shape, F32)
    return pl.pallas_call(body, out_shape=[col, col, mat, mat], name=name)(a_re, a_im, log_dt, b_re, b_im)


def _ssm_disc_bwd(a_re, a_im, log_dt, b_re, b_im, d_ab_re, d_ab_im, d_bb_re, d_bb_im, *, name):
    def body(ar, ai, ld, br, bi, g0, g1, g2, g3, o_ar, o_ai, o_ld, o_br, o_bi):
        _, vjp = jax.vjp(_discretize, ar[...], ai[...], ld[...], br[...], bi[...])
        r = vjp((g0[...], g1[...], g2[...], g3[...]))
        o_ar[...], o_ai[...], o_ld[...], o_br[...], o_bi[...] = r

    col = jax.ShapeDtypeStruct(a_re.shape, F32)
    mat = jax.ShapeDtypeStruct(b_re.shape, F32)
    return pl.pallas_call(body, out_shape=[col, col, col, mat, mat], name=name)(
        a_re, a_im, log_dt, b_re, b_im, d_ab_re, d_ab_im, d_bb_re, d_bb_im)


def _shift_rows(x, d, rows, *, down):
    t = x.shape[0]
    if down:
        return jnp.where(rows >= d, pltpu.roll(x, d, axis=0), 0.0)
    return jnp.where(rows < t - d, pltpu.roll(x, t - d, axis=0), 0.0)


def _scan_chunk(xr, xi, ar, ai, *, down):
    t = xr.shape[0]
    rows = lax.broadcasted_iota(jnp.int32, (t, 1), 0)
    pr, pi = ar, ai
    d = 1
    while d < t:
        sr = _shift_rows(xr, d, rows, down=down)
        si = _shift_rows(xi, d, rows, down=down)
        xr, xi = xr + pr * sr - pi * si, xi + pr * si + pi * sr
        pr, pi = pr * pr - pi * pi, 2.0 * pr * pi
        d *= 2
    return xr, xi


def _ssm_fwd(proj, ab, bd, cd, dskip, *, comm=None, name):
    l = proj.shape[0]
    t = min(SSM_CHUNK, l)
    nc = l // t

    def body(u_ref, ab_ref, bd_ref, cd_ref, ds_ref, y_ref, gy_ref, xs_ref, carry_ref):
        c = pl.program_id(1)

        @pl.when(c == 0)
        def _():
            carry_ref[...] = jnp.zeros_like(carry_ref)

        u = u_ref[...]
        ar, ai = ab_ref[0, 0:1, :], ab_ref[0, 1:2, :]
        bu = jnp.dot(u.astype(BF16), bd_ref[0], preferred_element_type=F32)
        rows = lax.broadcasted_iota(jnp.int32, (t, 1), 0)
        cr, ci = carry_ref[0:1, :], carry_ref[1:2, :]
        xr = bu[:, :SSM_X_BLK] + jnp.where(rows == 0, ar * cr - ai * ci, 0.0)
        xi = bu[:, SSM_X_BLK:] + jnp.where(rows == 0, ar * ci + ai * cr, 0.0)
        xr, xi = _scan_chunk(xr, xi, ar, ai, down=True)
        xs_ref[0, :, :SSM_X_BLK] = xr
        xs_ref[0, :, SSM_X_BLK:] = xi
        carry_ref[0:1, :] = xs_ref[0, t - 1:t, :SSM_X_BLK]
        carry_ref[1:2, :] = xs_ref[0, t - 1:t, SSM_X_BLK:]
        y = jnp.dot(xs_ref[0].astype(BF16), cd_ref[0], preferred_element_type=F32) + ds_ref[...] * u
        y_ref[...] = y
        gy_ref[...] = _gelu(y).astype(BF16)

    blk = lambda shape: pl.BlockSpec((1,) + shape, lambda j, c: (j, 0, 0))
    ycol = pl.BlockSpec((t, SSM_U_BLK), lambda j, c: (c, j))
    return _hosted_call(
        body, grid=(SSM_SPLIT, nc),
        in_specs=[pl.BlockSpec((t, SSM_U_BLK), lambda j, c: (c, COL_U + j)),
                  blk((2, SSM_X_BLK)), blk((SSM_U_BLK, 2 * SSM_X_BLK)), blk((2 * SSM_X_BLK, SSM_U_BLK)),
                  pl.BlockSpec((1, SSM_U_BLK), lambda j, c: (0, j))],
        out_specs=[ycol, ycol, pl.BlockSpec((1, t, 2 * SSM_X_BLK), lambda j, c: (j, c, 0))],
        out_shape=[jax.ShapeDtypeStruct((l, SSM_WIDTH), F32), jax.ShapeDtypeStruct((l, SSM_WIDTH), BF16),
                   jax.ShapeDtypeStruct((SSM_SPLIT, l, 2 * SSM_X_BLK), F32)],
        scratch_shapes=[pltpu.VMEM((2, SSM_X_BLK), F32)], sem=("parallel", "arbitrary"), name=name,
        args=(proj, ab, bd, cd, dskip), comm=comm)


def _ssm_bwd(proj, y, dgy, xs, ab, bdt, cdt, dskip, *, comm=None, name):
    l = proj.shape[0]
    t = min(SSM_CHUNK, l)
    nc = l // t

    def body(u_ref, y_ref, dgy_ref, xs_ref, halo_ref, ab_ref, bdt_ref, cdt_ref, ds_ref,
             du_ref, dbd_ref, dcd_ref, dab_ref, dd_ref, carry_ref):
        c = pl.program_id(1)
        ci_ = nc - 1 - c

        @pl.when(c == 0)
        def _():
            carry_ref[...] = jnp.zeros_like(carry_ref)
            dbd_ref[...] = jnp.zeros_like(dbd_ref)
            dcd_ref[...] = jnp.zeros_like(dcd_ref)
            dab_ref[...] = jnp.zeros_like(dab_ref)
            dd_ref[...] = jnp.zeros_like(dd_ref)

        u = u_ref[...]
        dy = dgy_ref[...] * _gelu_grad(y_ref[...])
        dyb = dy.astype(BF16)
        ar, ai = ab_ref[0, 0:1, :], ab_ref[0, 1:2, :]
        g = jnp.dot(dyb, cdt_ref[0], preferred_element_type=F32)
        rows = lax.broadcasted_iota(jnp.int32, (t, 1), 0)
        cr, ci = carry_ref[0:1, :], carry_ref[1:2, :]
        lr = g[:, :SSM_X_BLK] + jnp.where(rows == t - 1, ar * cr + ai * ci, 0.0)
        li = g[:, SSM_X_BLK:] + jnp.where(rows == t - 1, ar * ci - ai * cr, 0.0)
        lr, li = _scan_chunk(lr, li, ar, -ai, down=False)
        lam = jnp.concatenate([lr, li], axis=1)
        carry_ref[0:1, :] = lr[0:1, :]
        carry_ref[1:2, :] = li[0:1, :]
        lamb = lam.astype(BF16)
        du_ref[...] = (jnp.dot(lamb, bdt_ref[0], preferred_element_type=F32) + ds_ref[...] * dy).astype(BF16)
        dbd_ref[0] += lax.dot_general(u.astype(BF16), lamb, (((0,), (0,)), ((), ())),
                                      preferred_element_type=F32)
        xs = xs_ref[0]
        dcd_ref[0] += lax.dot_general(xs.astype(BF16), dyb, (((0,), (0,)), ((), ())),
                                      preferred_element_type=F32)
        halo = jnp.where(ci_ > 0, halo_ref[0, 7:8, :], 0.0)
        xprev = jnp.where(rows == 0, halo, pltpu.roll(xs, 1, axis=0))
        xpr, xpi = xprev[:, :SSM_X_BLK], xprev[:, SSM_X_BLK:]
        dab_ref[0, 0:1, :] += jnp.sum(lr * xpr + li * xpi, axis=0, keepdims=True)
        dab_ref[0, 1:2, :] += jnp.sum(li * xpr - lr * xpi, axis=0, keepdims=True)
        dd_ref[...] += jnp.sum(dy * u, axis=0, keepdims=True)

    blk = lambda shape: pl.BlockSpec((1,) + shape, lambda j, c: (j, 0, 0))
    rev = lambda j, c: (nc - 1 - c, j)
    ycol = pl.BlockSpec((t, SSM_U_BLK), rev)
    hb = t // 8
    return _hosted_call(
        body, grid=(SSM_SPLIT, nc), comm=comm, sem=("parallel", "arbitrary"), name=name,
        args=(proj, y, dgy, xs, xs, ab, bdt, cdt, dskip), scratch_shapes=[pltpu.VMEM((2, SSM_X_BLK), F32)],
        in_specs=[pl.BlockSpec((t, SSM_U_BLK), lambda j, c: (nc - 1 - c, COL_U + j)), ycol, ycol,
                  pl.BlockSpec((1, t, 2 * SSM_X_BLK), lambda j, c: (j, nc - 1 - c, 0)),
                  pl.BlockSpec((1, 8, 2 * SSM_X_BLK),
                               lambda j, c: (j, jnp.maximum((nc - 1 - c) * hb - 1, 0), 0)),
                  blk((2, SSM_X_BLK)), blk((2 * SSM_X_BLK, SSM_U_BLK)), blk((SSM_U_BLK, 2 * SSM_X_BLK)),
                  pl.BlockSpec((1, SSM_U_BLK), lambda j, c: (0, j))],
        out_specs=[ycol, blk((SSM_U_BLK, 2 * SSM_X_BLK)), blk((2 * SSM_X_BLK, SSM_U_BLK)),
                   blk((2, SSM_X_BLK)), pl.BlockSpec((1, SSM_U_BLK), lambda j, c: (0, j))],
        out_shape=[jax.ShapeDtypeStruct((l, SSM_WIDTH), BF16),
                   jax.ShapeDtypeStruct((SSM_SPLIT, SSM_U_BLK, 2 * SSM_X_BLK), F32),
                   jax.ShapeDtypeStruct((SSM_SPLIT, 2 * SSM_X_BLK, SSM_U_BLK), F32),
                   jax.ShapeDtypeStruct((SSM_SPLIT, 2, SSM_X_BLK), F32),
                   jax.ShapeDtypeStruct((1, SSM_WIDTH), F32)])


def _block_diag(t):
    s, g, a, b = t.shape
    return jnp.einsum('sgab,gk->sgakb', t, jnp.eye(g, dtype=t.dtype)).reshape(s, g * a, g * b)


def _block_diag_take(t, a, b):
    s = t.shape[0]
    return jnp.einsum('sgakb,gk->sgab', t.reshape(s, 8, a, 8, b), jnp.eye(8, dtype=t.dtype))


def _glu_fwd_hook(l, tm):
    def fn(result, ins, outs, i, j):
        z = result()
        outs[0][...] = z
        outs[1][...] = (z[:, :SSM_WIDTH] * _sigmoid(z[:, SSM_WIDTH:])).astype(BF16)

    row = lambda i, j, k: (i, 0)
    return _Hook(fn, outs=[((l, 2 * SSM_WIDTH), F32, (tm, 2 * SSM_WIDTH), row),
                           ((l, SSM_WIDTH), BF16, (tm, SSM_WIDTH), row)])


def _glu_bwd_hook(z, tm):
    l = z.shape[0]

    def fn(result, ins, outs, i, j):
        zv_ref, zg_ref = ins
        dz_ref, db_ref = outs
        d = result()
        sg = _sigmoid(zg_ref[...])
        dv = d * sg
        dg = d * zv_ref[...] * sg * (1.0 - sg)
        dz_ref[:, :SSM_WIDTH] = dv.astype(BF16)
        dz_ref[:, SSM_WIDTH:] = dg.astype(BF16)

        @pl.when(i == 0)
        def _():
            db_ref[...] = jnp.zeros_like(db_ref)

        db_ref[:, :SSM_WIDTH] += jnp.sum(dv, axis=0, keepdims=True)
        db_ref[:, SSM_WIDTH:] += jnp.sum(dg, axis=0, keepdims=True)

    half = (tm, SSM_WIDTH)
    return _Hook(fn, ins=[z, z], in_specs=[(half, lambda i, j, k: (i, 0)), (half, lambda i, j, k: (i, 1))],
                 outs=[((l, 2 * SSM_WIDTH), BF16, (tm, 2 * SSM_WIDTH), lambda i, j, k: (i, 0)),
                       ((1, 2 * SSM_WIDTH), F32, (1, 2 * SSM_WIDTH), lambda i, j, k: (0, 0))])


GATE_TC = 256


def _merge_fwd(proj, a, s, *, name):
    l = a.shape[0]
    tr = min(2048, l)

    def body(ga_ref, gs_ref, a_ref, s_ref, o_ref):
        o_ref[...] = (_sigmoid(ga_ref[...]) * a_ref[...] + _sigmoid(gs_ref[...]) * s_ref[...]).astype(BF16)

    own = pl.BlockSpec((tr, GATE_TC), lambda i, j: (i, j))
    return pl.pallas_call(
        body, grid=(l // tr, D_MODEL // GATE_TC),
        in_specs=[pl.BlockSpec((tr, GATE_TC), lambda i, j: (i, COL_GA // 2 + j)),
                  pl.BlockSpec((tr, GATE_TC), lambda i, j: (i, COL_GS // 2 + j)), own, own],
        out_specs=own, out_shape=jax.ShapeDtypeStruct((l, D_MODEL), BF16), name=name,
        compiler_params=_cp(("parallel", "parallel")))(proj, proj, a, s)


def _merge_bwd_hook(proj, a, s, tm):
    def fn(result, ins, outs, i, j):
        ga_ref, gs_ref, a_br, s_br = ins
        d = result()
        sa = _sigmoid(ga_ref[...])
        ss = _sigmoid(gs_ref[...])
        outs[0][...] = (d * sa).astype(BF16)
        outs[1][...] = (d * ss).astype(BF16)
        outs[2][...] = (d * a_br[...] * sa * (1.0 - sa)).astype(BF16)
        outs[3][...] = (d * s_br[...] * ss * (1.0 - ss)).astype(BF16)

    blk = (tm, GATE_TC)
    own = lambda i, j, k: (i, j)
    return _Hook(fn, ins=[proj, proj, a, s],
                 in_specs=[(blk, lambda i, j, k: (i, COL_GA // 2 + j)), (blk, lambda i, j, k: (i, COL_GS // 2 + j)),
                           (blk, own), (blk, own)],
                 outs=[(a.shape, BF16, blk, own)] * 4)


FF_TC = D_FF // 2
FF_NJ = 2
FF_ROWS = 128


FF_HALO = 16


def _conv_taps(ext, rows):
    h = FF_HALO
    return (ext[h:h + rows], pltpu.roll(ext, 1, axis=0)[h:h + rows], pltpu.roll(ext, 2, axis=0)[h:h + rows])


def _ff_specs(tr, l):
    hb = tr // FF_HALO
    last = l // FF_HALO - 1
    prev = lambda i: jnp.maximum(i * hb - 1, 0)
    nxt = lambda i: jnp.minimum((i + 1) * hb, last)
    return dict(
        own=pl.BlockSpec((tr, FF_TC), lambda j, i: (i, j)),
        own_next=pl.BlockSpec((FF_HALO, FF_TC), lambda j, i: (nxt(i), j)),
        val=pl.BlockSpec((tr, FF_TC), lambda j, i: (i, 2 * j)),
        val_next=pl.BlockSpec((FF_HALO, FF_TC), lambda j, i: (nxt(i), 2 * j)),
        gate=pl.BlockSpec((tr, FF_TC), lambda j, i: (i, 2 * j + 1)),
        gate_prev=pl.BlockSpec((FF_HALO, FF_TC), lambda j, i: (prev(i), 2 * j + 1)),
        gate_next=pl.BlockSpec((FF_HALO, FF_TC), lambda j, i: (nxt(i), 2 * j + 1)),
        pair=pl.BlockSpec((tr, 2 * FF_TC), lambda j, i: (i, j)),
        w=pl.BlockSpec((3, FF_TC), lambda j, i: (0, j)),
        b=pl.BlockSpec((1, FF_TC), lambda j, i: (0, j)))


def _ffn_act_fwd(up, conv_w, conv_b, *, name):
    l = up.shape[0]
    tr = min(FF_ROWS, l)

    def body(v_ref, g_ref, prev_ref, w_ref, b_ref, o_ref):
        prev = jnp.where(pl.program_id(1) == 0, 0.0, prev_ref[...].astype(F32))
        g0, g1, g2 = _conv_taps(jnp.concatenate([prev, g_ref[...].astype(F32)], axis=0), tr)
        gc = b_ref[...] + w_ref[0:1, :] * g2 + w_ref[1:2, :] * g1 + w_ref[2:3, :] * g0
        o_ref[...] = (v_ref[...].astype(F32) * _gelu(gc)).astype(BF16)

    sp = _ff_specs(tr, l)
    return pl.pallas_call(
        body, grid=(FF_NJ, l // tr), in_specs=[sp['val'], sp['gate'], sp['gate_prev'], sp['w'], sp['b']],
        out_specs=sp['own'], out_shape=jax.ShapeDtypeStruct((l, D_FF), BF16), name=name,
        compiler_params=_cp(("parallel", "parallel")))(up, up, up, conv_w, conv_b)


def _ffn_act_bwd(dact, up, conv_w, conv_b, *, comm=None, name):
    l = up.shape[0]
    tr = min(FF_ROWS, l)
    ni = l // tr
    te = tr + 8

    def body(d_ref, dn_ref, v_ref, vn_ref, g_ref, gp_ref, gn_ref, w_ref, b_ref, dup_ref, dw_ref, db_ref):
        i = pl.program_id(1)
        f32 = lambda ref, rows=None: ref[...].astype(F32)[:rows]
        prev = jnp.where(i == 0, 0.0, f32(gp_ref))
        g0, g1, g2 = _conv_taps(jnp.concatenate([prev, f32(g_ref), f32(gn_ref, 8)], axis=0), te)
        w0, w1, w2 = w_ref[0:1, :], w_ref[1:2, :], w_ref[2:3, :]
        gc = b_ref[...] + w0 * g2 + w1 * g1 + w2 * g0
        d_own = f32(d_ref)
        d = jnp.concatenate([d_own, jnp.where(i == ni - 1, 0.0, f32(dn_ref, 8))], axis=0)
        v = jnp.concatenate([f32(v_ref), f32(vn_ref, 8)], axis=0)
        dgc = d * v * _gelu_grad(gc)
        ahead1 = pltpu.roll(dgc, te - 1, axis=0)[:tr]
        ahead2 = pltpu.roll(dgc, te - 2, axis=0)[:tr]
        own = dgc[:tr]
        dup_ref[:, :FF_TC] = (d_own * _gelu(gc[:tr])).astype(BF16)
        dup_ref[:, FF_TC:] = (w2 * own + w1 * ahead1 + w0 * ahead2).astype(BF16)

        @pl.when(i == 0)
        def _():
            dw_ref[...] = jnp.zeros_like(dw_ref)
            db_ref[...] = jnp.zeros_like(db_ref)

        dw_ref[0:1, :] += jnp.sum(own * g2[:tr], axis=0, keepdims=True)
        dw_ref[1:2, :] += jnp.sum(own * g1[:tr], axis=0, keepdims=True)
        dw_ref[2:3, :] += jnp.sum(own * g0[:tr], axis=0, keepdims=True)
        db_ref[...] += jnp.sum(own, axis=0, keepdims=True)

    sp = _ff_specs(tr, l)
    return _hosted_call(
        body, grid=(FF_NJ, ni),
        in_specs=[sp['own'], sp['own_next'], sp['val'], sp['val_next'], sp['gate'], sp['gate_prev'],
                  sp['gate_next'], sp['w'], sp['b']],
        out_specs=[sp['pair'], sp['w'], sp['b']],
        out_shape=[jax.ShapeDtypeStruct((l, 2 * D_FF), BF16), jax.ShapeDtypeStruct((3, D_FF), F32),
                   jax.ShapeDtypeStruct((1, D_FF), F32)],
        scratch_shapes=[], sem=("parallel", "arbitrary"), name=name,
        args=(dact, dact, up, up, up, up, up, conv_w, conv_b), comm=comm)


def _join_columns(pieces, *, name):
    l = pieces[0].shape[0]
    widths = [p.shape[1] for p in pieces]
    n = sum(widths)
    tr = min(512, l)

    def body(*refs):
        ins, o_ref, s_ref = refs[:len(pieces)], refs[-2], refs[-1]

        @pl.when(pl.program_id(0) == 0)
        def _():
            s_ref[...] = jnp.zeros_like(s_ref)

        col = 0
        for p_ref, width in zip(ins, widths):
            v = p_ref[...]
            o_ref[:, col:col + width] = v
            s_ref[:, col:col + width] += jnp.sum(v.astype(F32), axis=0, keepdims=True)
            col += width

    return pl.pallas_call(
        body, grid=(l // tr,), in_specs=[pl.BlockSpec((tr, width), lambda i: (i, 0)) for width in widths],
        out_specs=[pl.BlockSpec((tr, n), lambda i: (i, 0)), pl.BlockSpec((1, n), lambda i: (0, 0))],
        out_shape=[jax.ShapeDtypeStruct((l, n), BF16), jax.ShapeDtypeStruct((1, n), F32)],
        name=name, compiler_params=_cp(("arbitrary",)))(*pieces)


def _local_step(x, target, wts, small, shards=None):
    l = x.shape[0]
    wts = dict(wts)
    grads, recvs, sgr = {}, {}, {}
    lay = lambda keys: [LAYOUT[k] for k in keys]
    none = lambda keys: None
    gather = (lambda keys: _GatherPlan([shards[k] for k in keys], lay(keys))) if shards is not None else none
    scatter = (lambda keys: _ScatterPlan([grads[k] for k in keys], lay(keys))) if shards is not None else none

    mm = _matmul

    def take(res, plan, keys, store):
        outs, couts = res
        if plan is not None:
            store.update(zip(keys, couts))
        return outs

    def mm_plan(plan, keys, store, *args, **kw):
        if plan is None:
            return _matmul(*args, **kw)
        return take(_matmul(*args, comm=plan, **kw), plan, keys, store)

    def mm_host(keys, make_plan, store, *args, **kw):
        return mm_plan(make_plan(keys), keys, store, *args, **kw)

    up_scatter = lambda p: _ScatterPlan([grads['w_up']], lay(['w_up']), part=(p, 2)) if shards is not None else None

    col = lambda t: t.reshape(SSM_GROUPS * SSM_STATE, 1)
    a_re, a_im = col(small['ssm_a_re']), col(small['ssm_a_im'])
    log_dt = jnp.repeat(small['ssm_log_dt'].reshape(SSM_GROUPS), SSM_STATE).reshape(-1, 1)
    b_re = small['ssm_b_re'].reshape(SSM_GROUPS * SSM_STATE, SSM_GROUP)
    b_im = small['ssm_b_im'].reshape(SSM_GROUPS * SSM_STATE, SSM_GROUP)
    ab_re, ab_im, bb_re, bb_im = _ssm_disc_fwd(a_re, a_im, log_dt, b_re, b_im, name="ssm_disc_fwd")
    ab = jnp.stack([ab_re.reshape(SSM_SPLIT, SSM_X_BLK), ab_im.reshape(SSM_SPLIT, SSM_X_BLK)], axis=1)
    to_bd = lambda t: _block_diag(t.reshape(SSM_SPLIT, 8, SSM_STATE, SSM_GROUP).transpose(0, 1, 3, 2))
    bd = jnp.concatenate([to_bd(bb_re), to_bd(bb_im)], axis=2)
    c_re = small['ssm_c_re'].reshape(SSM_SPLIT, 8, SSM_GROUP, SSM_STATE)
    c_im = small['ssm_c_im'].reshape(SSM_SPLIT, 8, SSM_GROUP, SSM_STATE)
    cdt = jnp.concatenate([_block_diag(c_re), -_block_diag(c_im)], axis=2)
    bd_b, cdt_b = bd.astype(BF16), cdt.astype(BF16)
    bdt_b, cd_b = bd_b.transpose(0, 2, 1), cdt_b.transpose(0, 2, 1)
    dskip = small['ssm_d'].reshape(1, SSM_WIDTH)

    sinks = small['attn_sinks'].reshape(N_Q_HEADS)
    plan = gather(['w_in_t'])
    h1, = take(_rms_fwd(x, small['attn_norm_g'], comm=plan, name="rms1_fwd"), plan, ['w_in_t'], wts)
    proj = mm_host(['w_glu', 'w_ba', 'w_bs', 'w_out'], gather, wts,
                   h1, wts['w_in_t'], tb=True, tm=512, tn=2944, tk=2048, inner='m', out_dtype=F32,
                   bias=small['b_in'], name="mm_in")
    attn = _attn_fwd(proj, sinks, name="attn_fwd")
    plan = gather(['w_up'])
    y, gy, xs = take(_ssm_fwd(proj, ab, bd_b, cd_b, dskip, comm=plan, name="ssm_fwd"), plan, ['w_up'], wts)
    z, ssm = mm(gy, wts['w_glu'], tm=1024, tn=1024, tk=512, bias=small['b_glu'],
                epilogue=_glu_fwd_hook(l, min(1024, l)), name="mm_glu")
    a_br = mm(attn, wts['w_ba'], tm=1024, tn=1024, tk=1024, out_dtype=F32, name="mm_ba")
    s_br = mm(ssm, wts['w_bs'], tm=1024, tn=1024, tk=512, out_dtype=F32, name="mm_bs")
    tr = min(ROW_TILE, l)
    merged = _merge_fwd(proj, a_br, s_br, name="merge_fwd")
    x2 = mm(merged, wts['w_out'], tm=1024, tn=1024, tk=2048, inner='m', out_dtype=F32, res=x, name="mm_out")
    h2, = take(_rms_fwd(x2, small['ffn_norm_g'], name="rms2_fwd"), None, [], wts)
    up = mm_host(['w_down'], gather, wts,
                 h2, wts['w_up'], tm=1024, tn=1024, tk=2048, out_dtype=BF16, name="mm_up")
    conv_w, conv_b = small['conv_w'], small['conv_b']
    act = _ffn_act_fwd(up, conv_w, conv_b, name="ffn_act_fwd")
    x3 = mm(act, wts['w_down'], tm=1024, tn=1024, tk=2816, out_dtype=F32, res=x2, name="mm_down")
    dx3b, d_g3, loss = _rowwise(
        _final_loss_hook(small['final_norm_g'].reshape(1, D_MODEL), target, tr), x3, tr, name="final_loss")

    sgr['final_norm_g'] = d_g3.reshape(D_MODEL)
    dact = mm(dx3b, wts['w_down'], tb=True, tm=512, tn=2816, tk=2048, inner='m', out_dtype=BF16, name="mm_dact")
    grads['w_down'] = mm(act, dx3b, ta=True, tm=512, tn=1024, tk=2048, out_dtype=BF16, name="mm_dw_down")
    plan = scatter(['w_down'])
    dup, sgr['conv_w'], sgr['conv_b'] = take(
        _ffn_act_bwd(dact, up, conv_w, conv_b, comm=plan, name="ffn_act_bwd"), plan, ['w_down'], recvs)
    grads['w_up'] = mm(h2, dup, ta=True, tm=1024, tn=1024, tk=2048, out_dtype=BF16, name="mm_dw_up")
    dh2 = mm_plan(up_scatter(0), ['w_up#0'], recvs,
                  dup, wts['w_up'], tb=True, tm=1024, tn=1024, tk=2816, out_dtype=F32, name="mm_dh2")
    dx2b, sgr['ffn_norm_g'] = _rowwise(
        _rms_bwd_hook(x2, small['ffn_norm_g'], dx3b, tr, BF16), dh2, tr, name="rms2_bwd")

    d_a, d_s, dga, dgs = mm(dx2b, wts['w_out'], tb=True, tm=1024, tn=GATE_TC, tk=2048,
                            epilogue=_merge_bwd_hook(proj, a_br, s_br, min(1024, l)), name="mm_dmerged")
    grads['w_out'] = mm(merged, dx2b, ta=True, tm=1024, tn=1024, tk=2048, out_dtype=BF16, name="mm_dw_out")
    dattn = mm(d_a, wts['w_ba'], tb=True, tm=1024, tn=1024, tk=2048, inner='m', out_dtype=BF16, name="mm_dattn")
    grads['w_ba'] = mm(attn, d_a, ta=True, tm=1024, tn=1024, tk=2048, out_dtype=BF16, name="mm_dw_ba")
    dz, sgr['b_glu'] = mm(d_s, wts['w_bs'], tb=True, tm=1024, tn=512, tk=2048, sequential=True,
                          epilogue=_glu_bwd_hook(z, min(1024, l)), name="mm_dssm")
    grads['w_bs'] = mm(ssm, d_s, ta=True, tm=512, tn=1024, tk=2048, out_dtype=BF16, name="mm_dw_bs")
    grads['w_glu'] = mm(gy, dz, ta=True, tm=512, tn=1024, tk=2048, out_dtype=BF16, name="mm_dw_glu")
    dgy = mm(dz, wts['w_glu'], tb=True, tm=1024, tn=512, tk=1024, inner='m', out_dtype=F32, name="mm_dgy")
    plan = up_scatter(1)
    du, d_bd, d_cd, d_ab, sgr['ssm_d'] = take(
        _ssm_bwd(proj, y, dgy, xs, ab, bdt_b, cdt_b, dskip, comm=plan, name="ssm_bwd"), plan, ['w_up#1'], recvs)
    keys = ['w_out', 'w_ba', 'w_bs', 'w_glu']
    plan = scatter(keys)
    dq, dkc, dkp, dvc, dvp, dsink = take(
        _attn_bwd(proj, sinks, dattn, comm=plan, name="attn_bwd"), plan, keys, recvs)
    dkv = _kv_grad_merge(dkc, dkp, dvc, dvp, name="kv_grad_merge")
    sgr['attn_sinks'] = dsink[:, :N_Q_HEADS]
    dproj, sgr['b_in'] = _join_columns([dq, dkv, du, dga, dgs], name="join_dproj")
    grads['w_in_t'] = mm(dproj, h1, ta=True, tm=2944, tn=1024, tk=1024, out_dtype=BF16, name="mm_dw_in")
    dh1 = mm_host(['w_in_t'], scatter, recvs,
                  dproj, wts['w_in_t'], tm=1024, tn=1024, tk=2944, out_dtype=F32, name="mm_dh1")
    grad_x, sgr['attn_norm_g'] = _rowwise(
        _rms_bwd_hook(x, small['attn_norm_g'], dx2b, tr, F32), dh1, tr, name="rms1_bwd")

    from_bd = lambda t: _block_diag_take(t, SSM_GROUP, SSM_STATE).transpose(0, 1, 3, 2).reshape(
        SSM_GROUPS * SSM_STATE, SSM_GROUP)
    d_bb_re = from_bd(d_bd[:, :, :SSM_X_BLK])
    d_bb_im = from_bd(d_bd[:, :, SSM_X_BLK:])
    d_cdt = d_cd.transpose(0, 2, 1)
    shape_c = (1, SSM_GROUPS, SSM_GROUP, SSM_STATE)
    sgr['ssm_c_re'] = _block_diag_take(d_cdt[:, :, :SSM_X_BLK], SSM_GROUP, SSM_STATE).reshape(shape_c)
    sgr['ssm_c_im'] = -_block_diag_take(d_cdt[:, :, SSM_X_BLK:], SSM_GROUP, SSM_STATE).reshape(shape_c)
    d_a_re, d_a_im, d_ldt, d_b_re, d_b_im = _ssm_disc_bwd(
        a_re, a_im, log_dt, b_re, b_im, d_ab[:, 0, :].reshape(-1, 1), d_ab[:, 1, :].reshape(-1, 1),
        d_bb_re, d_bb_im, name="ssm_disc_bwd")
    sgr['ssm_a_re'] = d_a_re.reshape(1, SSM_GROUPS, SSM_STATE)
    sgr['ssm_a_im'] = d_a_im.reshape(1, SSM_GROUPS, SSM_STATE)
    sgr['ssm_log_dt'] = d_ldt.reshape(SSM_GROUPS, SSM_STATE).sum(axis=1).reshape(1, SSM_GROUPS)
    sgr['ssm_b_re'] = d_b_re.reshape(1, SSM_GROUPS, SSM_STATE, SSM_GROUP)
    sgr['ssm_b_im'] = d_b_im.reshape(1, SSM_GROUPS, SSM_STATE, SSM_GROUP)
    return loss, grad_x, grads, recvs, sgr


def _swap_cores(arrs, *, name):
    n = len(arrs)

    def body(*refs):
        ins, outs = refs[:n], refs[n:2 * n]
        send_sems, recv_sems = refs[2 * n:]
        x, y, c = _place()
        copies = []
        for i in range(n):
            cp = pltpu.make_async_remote_copy(
                src_ref=ins[i], dst_ref=outs[i], send_sem=send_sems.at[i], recv_sem=recv_sems.at[i],
                device_id=(x, y, 1 - c), device_id_type=MESH)
            cp.start()
            copies.append(cp)
        for cp in copies:
            cp.wait()

    return pl.pallas_call(
        body, in_specs=[ANY] * n, out_specs=[ANY] * n,
        out_shape=[jax.ShapeDtypeStruct(a.shape, a.dtype) for a in arrs],
        scratch_shapes=[pltpu.SemaphoreType.DMA((n,)), pltpu.SemaphoreType.DMA((n,))],
        name=name)(*arrs)


def _all_reduce_small(buf, *, name):
    r = buf.shape[0]

    def body(in_ref, out_ref, slots, send_sems, recv_sems):
        x, y, c = _place()
        me = 4 * x + 2 * y + c
        slots[pl.ds(me, 1)] = in_ref[...][None]
        copies = []
        for k in range(N_DEV - 1):
            bx, by, bc = ((k + 1) >> 2) & 1, ((k + 1) >> 1) & 1, (k + 1) & 1
            peer = (1 - x if bx else x, 1 - y if by else y, 1 - c if bc else c)
            cp = pltpu.make_async_remote_copy(
                src_ref=in_ref, dst_ref=slots.at[me], send_sem=send_sems.at[k], recv_sem=recv_sems.at[k],
                device_id=peer, device_id_type=MESH)
            cp.start()
            copies.append(cp)
        for cp in copies:
            cp.wait()
        acc = slots[0]
        for d in range(1, N_DEV):
            acc = acc + slots[d]
        out_ref[...] = acc

    vm = pl.BlockSpec(memory_space=pltpu.VMEM)
    return pl.pallas_call(
        body, in_specs=[vm], out_specs=vm, out_shape=jax.ShapeDtypeStruct((r, 128), F32),
        scratch_shapes=[pltpu.VMEM((N_DEV, r, 128), F32), pltpu.SemaphoreType.DMA((N_DEV - 1,)),
                        pltpu.SemaphoreType.DMA((N_DEV - 1,))],
        name=name)(buf)


def _pack(arrs):
    flat = jnp.concatenate([a.reshape(-1).astype(F32) for a in arrs])
    pad = (-flat.shape[0]) % 1024
    return jnp.pad(flat, (0, pad)).reshape(-1, 128)


def _unpack(buf, shapes):
    flat = buf.reshape(-1)
    out, pos = [], 0
    for s in shapes:
        size = math.prod(s)
        out.append(flat[pos:pos + size].reshape(s))
        pos += size
    return out


TILE_ELEMS = 256 * 1024


def _tile_rows(r, c):
    if r * c <= TILE_ELEMS:
        return r
    for tr in range(TILE_ELEMS // c // 16 * 16, 0, -16):
        if r % tr == 0:
            return tr
    raise ValueError((r, c))


def _sum4(full, axis, pos, recvs, *, name):
    r, c = _block_shape(full.shape, axis)
    parts = len(recvs)
    tr = _tile_rows(r // parts, c)
    per = r // parts // tr
    nt = r // tr

    def body(pos_ref, o_ref, *refs):
        out_ref = refs[parts]
        for p in range(parts):
            @pl.when(pl.program_id(0) // per == p)
            def _():
                acc = o_ref[...].astype(F32)
                for k in range(3):
                    acc = acc + refs[p][k].astype(F32)
                out_ref[...] = acc.astype(BF16)

    own = (pl.BlockSpec((tr, c), lambda i, pos_ref: (pos_ref[0] * nt + i, 0)) if axis == 0
           else pl.BlockSpec((tr, c), lambda i, pos_ref: (i, pos_ref[0])))
    part_spec = lambda p: pl.BlockSpec((3, tr, c), lambda i, pos_ref: (0, jnp.clip(i - p * per, 0, per - 1), 0))
    grid_spec = pltpu.PrefetchScalarGridSpec(
        num_scalar_prefetch=1, grid=(nt,), in_specs=[own] + [part_spec(p) for p in range(parts)],
        out_specs=pl.BlockSpec((tr, c), lambda i, pos_ref: (i, 0)))
    return pl.pallas_call(
        body, grid_spec=grid_spec, out_shape=jax.ShapeDtypeStruct((r, c), BF16),
        name=name, compiler_params=_cp(("parallel",)))(pos, full, *recvs)


def _adam_step(w, g, m, v):
    bc1 = 1.0 - ADAM_B1 ** ADAM_STEP
    bc2 = 1.0 - ADAM_B2 ** ADAM_STEP
    mn = ADAM_B1 * m + (1.0 - ADAM_B1) * g
    vn = ADAM_B2 * v + (1.0 - ADAM_B2) * (g * g)
    m_hat = mn / bc1
    v_hat = vn / bc2
    return -ADAM_LR * (m_hat / (jnp.sqrt(v_hat) + ADAM_EPS) + ADAM_WD * w), mn, vn


def _adamw(w, ga, gb, m, v, *, name):
    r, c = w.shape
    tr = _tile_rows(r, c)

    def body(w_ref, ga_ref, gb_ref, m_ref, v_ref, g_out, d_out, m_out, v_out):
        g = ga_ref[...].astype(F32) + gb_ref[...].astype(F32)
        g_out[...] = g
        d_out[...], m_out[...], v_out[...] = _adam_step(w_ref[...], g, m_ref[...], v_ref[...])

    spec = pl.BlockSpec((tr, c), lambda i: (i, 0))
    shp = jax.ShapeDtypeStruct((r, c), F32)
    return pl.pallas_call(
        body, grid=(r // tr,), in_specs=[spec] * 5, out_specs=[spec] * 4,
        out_shape=[shp] * 4, name=name, compiler_params=_cp(("parallel",)))(w, ga, gb, m, v)


def _lanes(t):
    return t.reshape(-1, 128) if t.size % 128 == 0 else t.reshape(1, -1)


def _adamw_small(ws, gs, ms, vs, *, name):
    n = len(ws)

    def body(*refs):
        for i in range(n):
            w_ref, g_ref, m_ref, v_ref = (refs[k * n + i] for k in range(4))
            outs = [refs[(4 + k) * n + i] for k in range(3)]
            outs[0][...], outs[1][...], outs[2][...] = _adam_step(w_ref[...], g_ref[...], m_ref[...], v_ref[...])

    flat = [_lanes(t) for group in (ws, gs, ms, vs) for t in group]
    shp = [jax.ShapeDtypeStruct(_lanes(t).shape, F32) for t in ws]
    res = pl.pallas_call(body, out_shape=shp * 3, name=name)(*flat)
    return [[res[k * n + i].reshape(ws[i].shape) for i in range(n)] for k in range(3)]


BIG = ['w_in', 'w_glu', 'w_branch_attn', 'w_branch_ssm', 'w_out', 'w_up', 'w_down']
BIG_KEY = {'w_in': 'w_in_t', 'w_glu': 'w_glu', 'w_branch_attn': 'w_ba', 'w_branch_ssm': 'w_bs',
           'w_out': 'w_out', 'w_up': 'w_up', 'w_down': 'w_down'}
TRANSPOSED = {'w_in'}
SMALL = ['attn_norm_g', 'b_in', 'attn_sinks', 'ssm_a_re', 'ssm_a_im', 'ssm_log_dt', 'ssm_b_re', 'ssm_b_im',
         'ssm_c_re', 'ssm_c_im', 'ssm_d', 'b_glu', 'ffn_norm_g', 'conv_b', 'final_norm_g']
WEIGHTS = ['attn_norm_g', 'w_in', 'b_in', 'attn_sinks', 'ssm_a_re', 'ssm_a_im', 'ssm_log_dt', 'ssm_b_re',
           'ssm_b_im', 'ssm_c_re', 'ssm_c_im', 'ssm_d', 'w_glu', 'b_glu', 'w_branch_attn', 'w_branch_ssm',
           'w_out', 'ffn_norm_g', 'w_up', 'conv_w', 'conv_b', 'w_down', 'final_norm_g']


def _shard_2d(name, t):
    t = t[0]
    return t.T if name in TRANSPOSED else t


def _unshard_2d(name, t):
    return (t.T if name in TRANSPOSED else t)[None]


def kernel(x, attn_norm_g, w_in, b_in, attn_sinks, ssm_a_re, ssm_a_im, ssm_log_dt, ssm_b_re, ssm_b_im, ssm_c_re, ssm_c_im, ssm_d, w_glu, b_glu, w_branch_attn, w_branch_ssm, w_out, ffn_norm_g, w_up, conv_w, conv_b, w_down, final_norm_g, loss_target, m_attn_norm_g, m_w_in, m_b_in, m_attn_sinks, m_ssm_a_re, m_ssm_a_im, m_ssm_log_dt, m_ssm_b_re, m_ssm_b_im, m_ssm_c_re, m_ssm_c_im, m_ssm_d, m_w_glu, m_b_glu, m_w_branch_attn, m_w_branch_ssm, m_w_out, m_ffn_norm_g, m_w_up, m_conv_w, m_conv_b, m_w_down, m_final_norm_g, v_attn_norm_g, v_w_in, v_b_in, v_attn_sinks, v_ssm_a_re, v_ssm_a_im, v_ssm_log_dt, v_ssm_b_re, v_ssm_b_im, v_ssm_c_re, v_ssm_c_im, v_ssm_d, v_w_glu, v_b_glu, v_w_branch_attn, v_w_branch_ssm, v_w_out, v_ffn_norm_g, v_w_up, v_conv_w, v_conv_b, v_w_down, v_final_norm_g):
    args = dict(locals())
    w = {n: args[n] for n in WEIGHTS}
    m = {n: args['m_' + n] for n in WEIGHTS}
    v = {n: args['v_' + n] for n in WEIGHTS}
    xi, yi, ci = _place()
    blk = 2 * xi + yi

    shards = {BIG_KEY[n]: _shard_2d(n, w[n]).astype(BF16) for n in BIG}
    cw_cols = w['conv_w'].shape[2]
    cw_place = lax.dynamic_update_slice(jnp.zeros((3, D_FF), F32), w['conv_w'][0] * (ci == 0).astype(F32),
                                        (0, blk * cw_cols))
    conv_w_full = _unpack(_all_reduce_small(_pack([cw_place]), name="gather_conv_w"), [(3, D_FF)])[0]

    small = {n: w[n] for n in SMALL}
    small['conv_w'] = conv_w_full
    loss_part, grad_x, grads, recvs, sgr = _local_step(x[0], loss_target[0], {}, small, shards)

    halves = []
    for n in BIG:
        key = BIG_KEY[n]
        full = grads[key]
        recv = [recvs[key]] if key in recvs else [recvs[key + '#0'], recvs[key + '#1']]
        axis, interleaved = LAYOUT[key]
        pos = _block_pos(xi, yi, interleaved).astype(jnp.int32).reshape(1)
        halves.append(_sum4(full, axis, pos, recv, name="sum4_" + n))
    others = _swap_cores(halves, name="swap_cores")
    out = {}
    for n, mine, other in zip(BIG, halves, others):
        res = _adamw(_shard_2d(n, w[n]), mine, other, _shard_2d(n, m[n]), _shard_2d(n, v[n]), name="adamw_" + n)
        out[n] = [_unshard_2d(n, t) for t in res]

    names = SMALL + ['conv_w']
    shapes = [w[n].shape for n in SMALL] + [(3, D_FF)]
    packed = _pack([sgr[n] for n in names] + [loss_part])
    summed = _unpack(_all_reduce_small(packed, name="all_reduce_small"), shapes + [(1, 1)])
    loss = summed[-1].reshape(())
    sg = dict(zip(names, summed[:-1]))
    sg['conv_w'] = lax.dynamic_slice_in_dim(sg['conv_w'], blk * cw_cols, cw_cols, axis=1)[None]
    deltas, new_m, new_v = _adamw_small([w[n] for n in names], [sg[n] for n in names], [m[n] for n in names],
                                        [v[n] for n in names], name="adamw_small")
    for i, n in enumerate(names):
        out[n] = [sg[n], deltas[i], new_m[i], new_v[i]]

    return (loss, grad_x[None], *[out[n][0] for n in WEIGHTS], *[out[n][1] for n in WEIGHTS],
            *[out[n][2] for n in WEIGHTS], *[out[n][3] for n in WEIGHTS])
```

```python
import functools
import math

import jax
import jax.numpy as jnp
from jax import lax
from jax.experimental import pallas as pl
from jax.experimental.pallas import tpu as pltpu

F32 = jnp.float32
BF16 = jnp.bfloat16

D_MODEL = 2048
N_Q_HEADS = 16
HEAD_DIM = 64
ATTN_WIDTH = 1024
KV_WIDTH = 128
BLOCK = 128
SSM_WIDTH = 512
SSM_GROUPS = 32
SSM_GROUP = 16
SSM_STATE = 64
D_FF = 5632
IN_COLS = 5888
RMS_EPS = 1e-6
NEG_BIG = -1e30
N_CHIPS = 4
N_DEV = 8

COL_K = 8
COL_V = 9
COL_U = 10
COL_GA = 14
COL_GS = 30

SSM_SPLIT = 4
SSM_U_BLK = 128
SSM_X_BLK = 512
SSM_CHUNK = 256

ADAM_LR = 0.001
ADAM_B1 = 0.9
ADAM_B2 = 0.999
ADAM_EPS = 1e-08
ADAM_WD = 0.01
ADAM_STEP = 10

VMEM_LIMIT_BYTES = 56 * 1024 * 1024
INV_SQRT2 = 1.0 / math.sqrt(2.0)
INV_SQRT2PI = 1.0 / math.sqrt(2.0 * math.pi)
MESH = pl.DeviceIdType.MESH
ANY = pl.BlockSpec(memory_space=pl.ANY)


def _cp(sem):
    return pltpu.CompilerParams(dimension_semantics=sem, vmem_limit_bytes=VMEM_LIMIT_BYTES)


def _gelu(x):
    return 0.5 * x * (1.0 + lax.erf(x * INV_SQRT2))


def _gelu_grad(x):
    return 0.5 * (1.0 + lax.erf(x * INV_SQRT2)) + x * jnp.exp(-0.5 * x * x) * INV_SQRT2PI


def _sigmoid(x):
    return 1.0 / (1.0 + jnp.exp(-x))


def _place():
    return lax.axis_index("x"), lax.axis_index("y"), lax.axis_index("c")


def _other_chips(x, y):
    return [(1 - x, y), (x, 1 - y), (1 - x, 1 - y)]


def _block_pos(x, y, interleaved):
    return x + 2 * y if interleaved else 2 * x + y


LAYOUT = {'w_in_t': (0, False), 'w_glu': (1, False), 'w_ba': (1, False), 'w_bs': (1, False),
          'w_out': (0, False), 'w_up': (1, True), 'w_down': (0, False)}


def _window(ref, axis, pos, size, rows=None):
    if axis == 0:
        start, count = (0, size) if rows is None else rows
        return ref.at[pl.ds(pos * size + start, count), :]
    cols = pl.ds(pos * size, size)
    return ref.at[:, cols] if rows is None else ref.at[pl.ds(rows[0], rows[1]), cols]


def _gathered_shape(shape, axis):
    return tuple(N_CHIPS * d if a == axis else d for a, d in enumerate(shape))


def _block_shape(shape, axis):
    return tuple(d // N_CHIPS if a == axis else d for a, d in enumerate(shape))


class _GatherPlan:
    def __init__(self, shards, layouts):
        self.arrays = list(shards)
        self.layouts = list(layouts)
        n = len(shards)
        self.out_shape = [jax.ShapeDtypeStruct(_gathered_shape(s.shape, lay[0]), s.dtype)
                          for s, lay in zip(shards, layouts)]
        self.scratch = [pltpu.SemaphoreType.DMA((6 * n,)), pltpu.SemaphoreType.DMA((6 * n,)),
                        pltpu.SemaphoreType.DMA((n,))]

    def _copies(self, kind, ins, outs, sems):
        send, recv, local = sems
        n = len(self.arrays)
        x, y, c = _place()
        copies = []
        for i in range(n):
            axis, interleaved = self.layouts[i]
            size = self.arrays[i].shape[axis]
            h = self.arrays[i].shape[0] // 2
            first = lambda core: core * h
            blk = _block_pos(x, y, interleaved)
            if kind == 'mine':
                copies.append(pltpu.make_async_copy(ins[i], _window(outs[i], axis, blk, size), local.at[i]))
                continue
            for k, (px, py) in enumerate(_other_chips(x, y)):
                theirs = _block_pos(px, py, interleaved)
                if kind in ('ici_out', 'ici_in'):
                    route = dict(send_sem=send.at[3 * i + k], recv_sem=recv.at[3 * i + k],
                                 device_id=(px, py, c), device_id_type=MESH)
                else:
                    route = dict(send_sem=send.at[3 * (n + i) + k], recv_sem=recv.at[3 * (n + i) + k],
                                 device_id=(x, y, 1 - c), device_id_type=MESH)
                if kind == 'ici_out':
                    src, dst = ins[i].at[pl.ds(first(c), h), :], _window(outs[i], axis, blk, size, (first(c), h))
                elif kind == 'd2d_in':
                    src = dst = _window(outs[i], axis, theirs, size, (first(1 - c), h))
                else:
                    src = dst = _window(outs[i], axis, theirs, size, (first(c), h))
                copies.append(pltpu.make_async_remote_copy(src_ref=src, dst_ref=dst, **route))
        return copies

    def start(self, ins, outs, sems):
        for cp in self._copies('mine', ins, outs, sems) + self._copies('ici_out', ins, outs, sems):
            cp.start()

    def middle(self, ins, outs, sems):
        for arrived, onward in zip(self._copies('ici_in', ins, outs, sems), self._copies('d2d_out', ins, outs, sems)):
            arrived.wait_recv()
            onward.start()

    def finish(self, ins, outs, sems):
        for cp in self._copies('d2d_in', ins, outs, sems):
            cp.wait_recv()
        for cp in self._copies('ici_out', ins, outs, sems) + self._copies('d2d_out', ins, outs, sems):
            cp.wait_send()
        for cp in self._copies('mine', ins, outs, sems):
            cp.wait()


class _ScatterPlan:
    def __init__(self, fulls, layouts, part=(0, 1)):
        self.arrays = list(fulls)
        self.layouts = list(layouts)
        self.part = part
        n = len(fulls)
        self.out_shape = []
        for f, lay in zip(fulls, layouts):
            rows, cols = _block_shape(f.shape, lay[0])
            self.out_shape.append(jax.ShapeDtypeStruct((3, rows // part[1], cols), f.dtype))
        self.scratch = [pltpu.SemaphoreType.DMA((3 * n,)), pltpu.SemaphoreType.DMA((3 * n,))]

    def _copies(self, ins, outs, sems):
        send, recv = sems
        x, y, c = _place()
        copies = []
        for i in range(len(self.arrays)):
            axis, interleaved = self.layouts[i]
            size = self.arrays[i].shape[axis] // N_CHIPS
            h = _block_shape(self.arrays[i].shape, axis)[0] // self.part[1]
            rows = (self.part[0] * h, h)
            for k, (px, py) in enumerate(_other_chips(x, y)):
                copies.append(pltpu.make_async_remote_copy(
                    src_ref=_window(ins[i], axis, _block_pos(px, py, interleaved), size, rows), dst_ref=outs[i].at[k],
                    send_sem=send.at[3 * i + k], recv_sem=recv.at[3 * i + k],
                    device_id=(px, py, c), device_id_type=MESH))
        return copies

    def start(self, ins, outs, sems):
        for cp in self._copies(ins, outs, sems):
            cp.start()

    def middle(self, ins, outs, sems):
        pass

    def finish(self, ins, outs, sems):
        for cp in self._copies(ins, outs, sems):
            cp.wait()


def _hosted_call(body, *, grid, in_specs, out_specs, out_shape, scratch_shapes, sem, name, args, comm=None,
                 aliases=None):
    aliases = aliases or {}
    if comm is None:
        outs = pl.pallas_call(body, grid=grid, in_specs=in_specs, out_specs=out_specs, out_shape=out_shape,
                              scratch_shapes=scratch_shapes, name=name, input_output_aliases=aliases,
                              compiler_params=_cp(sem))(*args)
        return outs, None
    n_in, n_out, n_scr = len(in_specs), len(out_specs), len(scratch_shapes)
    nc, ns = len(comm.arrays), len(comm.scratch)
    total = math.prod(grid)
    mid = total - max(1, total // 8)

    def wrapped(*refs):
        pos = 0
        ins = refs[pos:pos + n_in]; pos += n_in
        cins = refs[pos:pos + nc]; pos += nc
        outs = refs[pos:pos + n_out]; pos += n_out
        couts = refs[pos:pos + nc]; pos += nc
        scr = refs[pos:pos + n_scr]; pos += n_scr
        sems = refs[pos:pos + ns]
        step = 0
        for ax, g in enumerate(grid):
            step = step * g + pl.program_id(ax)

        @pl.when(step == 0)
        def _():
            comm.start(cins, couts, sems)

        body(*ins, *outs, *scr)

        @pl.when(step == mid)
        def _():
            comm.middle(cins, couts, sems)

        @pl.when(step == total - 1)
        def _():
            comm.finish(cins, couts, sems)

    res = pl.pallas_call(
        wrapped, grid=grid, in_specs=list(in_specs) + [ANY] * nc, out_specs=list(out_specs) + [ANY] * nc,
        out_shape=list(out_shape) + list(comm.out_shape), scratch_shapes=list(scratch_shapes) + list(comm.scratch),
        name=name, input_output_aliases=aliases,
        compiler_params=_cp(("arbitrary",) * len(grid)))(*args, *comm.arrays)
    return res[:n_out], res[n_out:]


class _Hook:
    def __init__(self, fn, ins=(), in_specs=(), outs=()):
        self.fn, self.ins, self.in_specs, self.outs = fn, list(ins), list(in_specs), list(outs)


def _matmul(a, b, *, ta=False, tb=False, tm, tn, tk, out_dtype=None, bias=None, res=None, inner='n',
            comm=None, prologue=None, epilogue=None, a_shape=None, sequential=False, name):
    if a is None:
        m, kdim = a_shape
    elif ta:
        kdim, m = a.shape
    else:
        m, kdim = a.shape
    if tb:
        n, k2 = b.shape
    else:
        k2, n = b.shape
    assert kdim == k2, (name, kdim, b.shape)
    tm, tn, tk = min(tm, m), min(tn, n), min(tk, kdim)
    assert m % tm == 0 and n % tn == 0 and kdim % tk == 0, (name, m, n, kdim, tm, tn, tk)
    nk = kdim // tk
    dn = (((0 if ta else 1,), (1 if tb else 0,)), ((), ()))
    hooks = [h for h in (prologue, epilogue) if h is not None]
    n_pro_in = len(prologue.ins) if prologue else 0
    n_epi_in = len(epilogue.ins) if epilogue else 0
    n_pro_out = len(prologue.outs) if prologue else 0
    n_epi_out = len(epilogue.outs) if epilogue else 0
    if inner == 'n':
        grid = (m // tm, n // tn, nk)
        mi = lambda g0, g1: g0
        ni = lambda g0, g1: g1
    else:
        grid = (n // tn, m // tm, nk)
        mi = lambda g0, g1: g1
        ni = lambda g0, g1: g0

    def body(*refs):
        refs = list(refs)
        take = lambda cnt: [refs.pop(0) for _ in range(cnt)]
        a_ref = take(1)[0] if a is not None else None
        b_ref = take(1)[0]
        bias_ref = take(1)[0] if bias is not None else None
        res_ref = take(1)[0] if res is not None else None
        pro_in, epi_in = take(n_pro_in), take(n_epi_in)
        o_ref = take(1)[0] if epilogue is None else None
        pro_out, epi_out = take(n_pro_out), take(n_epi_out)
        i, j, k = mi(pl.program_id(0), pl.program_id(1)), ni(pl.program_id(0), pl.program_id(1)), pl.program_id(2)

        def finish(src):
            def result(rows=slice(None)):
                r = src[rows, :]
                if bias_ref is not None:
                    r = r + bias_ref[...]
                if res_ref is not None:
                    r = r + res_ref[rows, :]
                return r

            if epilogue is None:
                o_ref[...] = result().astype(out_dtype)
            else:
                epilogue.fn(result, epi_in, epi_out, i, j)

        a_val = a_ref[...] if prologue is None else prologue.fn(a_ref, pro_in, pro_out, i, k)
        prod = lax.dot_general(a_val.astype(BF16), b_ref[...].astype(BF16), dn, preferred_element_type=F32)
        if nk == 1:
            finish(prod)
            return
        acc_ref = refs[0]

        @pl.when(k == 0)
        def _():
            acc_ref[...] = prod

        @pl.when(k > 0)
        def _():
            acc_ref[...] += prod

        @pl.when(k == nk - 1)
        def _():
            finish(acc_ref)

    spec = lambda shape, fn: pl.BlockSpec(shape, lambda g0, g1, k: fn(mi(g0, g1), ni(g0, g1), k))
    in_specs, args = [], []
    if a is not None:
        in_specs.append(spec((tk, tm), lambda i, j, k: (k, i)) if ta else spec((tm, tk), lambda i, j, k: (i, k)))
        args.append(a)
    in_specs.append(spec((tn, tk), lambda i, j, k: (j, k)) if tb else spec((tk, tn), lambda i, j, k: (k, j)))
    args.append(b)
    if bias is not None:
        in_specs.append(spec((1, tn), lambda i, j, k: (0, j)))
        args.append(bias)
    if res is not None:
        in_specs.append(spec((tm, tn), lambda i, j, k: (i, j)))
        args.append(res)
    for h in hooks:
        in_specs += [spec(shape, fn) for shape, fn in h.in_specs]
        args += h.ins
    out_specs, out_shape = [], []
    if epilogue is None:
        out_specs.append(spec((tm, tn), lambda i, j, k: (i, j)))
        out_shape.append(jax.ShapeDtypeStruct((m, n), out_dtype))
    for h in hooks:
        out_specs += [spec(blk, fn) for _, _, blk, fn in h.outs]
        out_shape += [jax.ShapeDtypeStruct(shape, dtype) for shape, dtype, _, _ in h.outs]
    outs, couts = _hosted_call(
        body, grid=grid, in_specs=in_specs, out_specs=out_specs, out_shape=out_shape,
        scratch_shapes=[pltpu.VMEM((tm, tn), F32)] if nk > 1 else [],
        sem=("arbitrary",) * 3 if sequential else ("parallel", "parallel", "arbitrary"),
        name=name, args=args, comm=comm)
    outs = outs[0] if not hooks else outs
    return outs if comm is None else (outs, couts)


def _rms_fwd(x, g, *, comm=None, name):
    l, d = x.shape
    tr = min(256, l)

    def body(x_ref, g_ref, h_ref):
        xf = x_ref[...]
        r = lax.rsqrt(jnp.mean(xf * xf, axis=-1, keepdims=True) + RMS_EPS)
        h_ref[...] = ((xf * r) * g_ref[...]).astype(BF16)

    row = pl.BlockSpec((tr, d), lambda i: (i, 0))
    return _hosted_call(
        body, grid=(l // tr,), in_specs=[row, pl.BlockSpec((1, d), lambda i: (0, 0))],
        out_specs=[row], out_shape=[jax.ShapeDtypeStruct((l, d), BF16)], scratch_shapes=[],
        sem=("parallel",), name=name, args=(x, g), comm=comm)


EPI_ROWS = 128
ROW_TILE = 256


def _row_chunks(tm):
    ch = min(EPI_ROWS, tm)
    return [slice(c * ch, (c + 1) * ch) for c in range(tm // ch)]


def _rowwise(hook, src, tm, *, name):
    l, d = src.shape
    n_in = len(hook.ins)

    def body(*refs):
        src_ref, ins, outs = refs[0], refs[1:1 + n_in], refs[1 + n_in:]
        hook.fn(lambda rows=slice(None): src_ref[rows, :], ins, outs, pl.program_id(0), 0)

    spec = lambda shape, fn: pl.BlockSpec(shape, lambda i: fn(i, 0, 0))
    return pl.pallas_call(
        body, grid=(l // tm,),
        in_specs=[pl.BlockSpec((tm, d), lambda i: (i, 0))] + [spec(shape, fn) for shape, fn in hook.in_specs],
        out_specs=[spec(blk, fn) for _, _, blk, fn in hook.outs],
        out_shape=[jax.ShapeDtypeStruct(shape, dtype) for shape, dtype, _, _ in hook.outs],
        name=name, compiler_params=_cp(("arbitrary",)))(src, *hook.ins)


def _rms_bwd_hook(x, g, dres, tm, out_dtype):
    def fn(result, ins, outs, i, j):
        x_ref, g_ref, dres_ref = ins
        dx_ref, dg_ref = outs

        @pl.when(i == 0)
        def _():
            dg_ref[...] = jnp.zeros_like(dg_ref)

        for rows in _row_chunks(tm):
            dyv = result(rows).astype(F32)
            xf = x_ref[rows, :]
            r = lax.rsqrt(jnp.mean(xf * xf, axis=-1, keepdims=True) + RMS_EPS)
            xhat = xf * r
            dxh = dyv * g_ref[...]
            dx = r * (dxh - xhat * jnp.mean(dxh * xhat, axis=-1, keepdims=True)) + dres_ref[rows, :].astype(F32)
            dx_ref[rows, :] = dx.astype(out_dtype)
            dg_ref[...] += jnp.sum(dyv * xhat, axis=0, keepdims=True)

    l, d = x.shape
    row = lambda i, j, k: (i, 0)
    vec = lambda i, j, k: (0, 0)
    return _Hook(fn, ins=[x, g, dres], in_specs=[((tm, d), row), ((1, d), vec), ((tm, d), row)],
                 outs=[((l, d), out_dtype, (tm, d), row), ((1, d), F32, (1, d), vec)])


def _final_loss_hook(g, target, tm):
    l, d = target.shape

    def fn(result, ins, outs, i, j):
        g_ref, t_ref = ins
        dxb_ref, dg_ref, loss_ref = outs
        gv = g_ref[...]

        @pl.when(i == 0)
        def _():
            dg_ref[...] = jnp.zeros_like(dg_ref)
            loss_ref[...] = jnp.zeros_like(loss_ref)

        for rows in _row_chunks(tm):
            xf = result(rows)
            r = lax.rsqrt(jnp.mean(xf * xf, axis=-1, keepdims=True) + RMS_EPS)
            xhat = xf * r
            diff = xhat * gv - t_ref[rows, :]
            dout = diff * (1.0 / d)
            dxh = dout * gv
            dx = r * (dxh - xhat * jnp.mean(dxh * xhat, axis=-1, keepdims=True))
            dxb_ref[rows, :] = dx.astype(BF16)
            dg_ref[...] += jnp.sum(dout * xhat, axis=0, keepdims=True)
            part = jnp.sum(jnp.mean(diff * diff, axis=-1, keepdims=True), axis=0, keepdims=True)
            loss_ref[...] += 0.5 * part

    row = lambda i, j, k: (i, 0)
    vec = lambda i, j, k: (0, 0)
    return _Hook(fn, ins=[g, target], in_specs=[((1, d), vec), ((tm, d), row)],
                 outs=[((l, d), BF16, (tm, d), row), ((1, d), F32, (1, d), vec), ((1, 1), F32, (1, 1), vec)])


Q_PER_KV = 8
GROUP_ROWS = Q_PER_KV * BLOCK


def _attn_masks(n, rows=GROUP_ROWS):
    q_idx = lax.broadcasted_iota(jnp.int32, (rows, 2 * BLOCK), 0) & (BLOCK - 1)
    s_idx = lax.broadcasted_iota(jnp.int32, (rows, 2 * BLOCK), 1)
    dist = q_idx + BLOCK - s_idx
    valid = (dist >= 0) & (dist < BLOCK) & ((n > 0) | (s_idx >= BLOCK))
    return dist.astype(F32), valid


def _dup_half(t, kv_head, lo):
    rolled = pltpu.roll(t, HEAD_DIM, axis=1)
    return jnp.where(lo, t, rolled) if kv_head == 0 else jnp.where(lo, rolled, t)


def _stack_heads(ref, kv_head, lo):
    pieces = []
    for r in range(Q_PER_KV):
        pair = kv_head * 4 + r // 2
        t = ref[:, pair * 128:(pair + 1) * 128].astype(BF16)
        sel = lo if r % 2 == 0 else jnp.logical_not(lo)
        pieces.append(jnp.where(sel, t, jnp.zeros_like(t)))
    return jnp.concatenate(pieces, axis=0)


def _unstack_heads(t, lo):
    return [jnp.where(lo, t[(2 * i) * BLOCK:(2 * i + 1) * BLOCK], t[(2 * i + 1) * BLOCK:(2 * i + 2) * BLOCK])
            for i in range(Q_PER_KV // 2)]


def _per_head_column(values):
    return jnp.concatenate([jnp.full((BLOCK, 1), v, F32) for v in values], axis=0)


def _group_probs(qm, kdup, dist, valid, sink_ref, kv_head):
    heads = [kv_head * Q_PER_KV + r for r in range(Q_PER_KV)]
    slope = _per_head_column([2.0 ** (-8.0 * (h + 1) / N_Q_HEADS) for h in heads])
    sink = _per_head_column([sink_ref[h] for h in heads])
    return _probs(qm, kdup, dist, valid, sink, slope)


def _probs(qm, kdup, dist, valid, sink, slope):
    s = lax.dot_general(qm, kdup, (((1,), (1,)), ((), ())), preferred_element_type=F32)
    s = s * (HEAD_DIM ** -0.5) - slope * dist
    s = jnp.where(valid, s, NEG_BIG)
    m = jnp.maximum(jnp.max(s, axis=-1, keepdims=True), sink)
    p = jnp.exp(s - m)
    esink = jnp.exp(sink - m)
    inv = 1.0 / (jnp.sum(p, axis=-1, keepdims=True) + esink)
    return p * inv, esink * inv


def _attn_fwd(proj, sinks, *, name):
    l = proj.shape[0]
    nb = l // BLOCK

    def body(sink_ref, q_ref, kc_ref, kp_ref, vc_ref, vp_ref, o_ref):
        n = pl.program_id(0)
        dist, valid = _attn_masks(n, BLOCK)
        lo = lax.broadcasted_iota(jnp.int32, (1, BLOCK), 1) < HEAD_DIM
        kx = jnp.concatenate([kp_ref[...], kc_ref[...]], axis=0).astype(BF16)
        vx = jnp.concatenate([vp_ref[...], vc_ref[...]], axis=0).astype(BF16)
        for kv_head in range(2):
            kdup = _dup_half(kx, kv_head, lo)
            vdup = _dup_half(vx, kv_head, lo)
            for pr in range(4):
                pair = kv_head * 4 + pr
                qp = q_ref[:, pair * 128:(pair + 1) * 128].astype(BF16)
                o_pair = jnp.zeros((BLOCK, 128), F32)
                for half in range(2):
                    head = 2 * pair + half
                    sel = lo if half == 0 else jnp.logical_not(lo)
                    qm = jnp.where(sel, qp, jnp.zeros_like(qp))
                    p, _ = _probs(qm, kdup, dist, valid, sink_ref[head], 2.0 ** (-8.0 * (head + 1) / N_Q_HEADS))
                    o = jnp.dot(p.astype(BF16), vdup, preferred_element_type=F32)
                    o_pair = o_pair + jnp.where(sel, o, 0.0)
                o_ref[:, pair * 128:(pair + 1) * 128] = o_pair.astype(BF16)

    kv = lambda col, prev: pl.BlockSpec(
        (BLOCK, KV_WIDTH), (lambda n: (jnp.maximum(n - 1, 0), col)) if prev else (lambda n: (n, col)))
    return pl.pallas_call(
        body, grid=(nb,),
        in_specs=[pl.BlockSpec(memory_space=pltpu.SMEM),
                  pl.BlockSpec((BLOCK, ATTN_WIDTH), lambda n: (n, 0)),
                  kv(COL_K, False), kv(COL_K, True), kv(COL_V, False), kv(COL_V, True)],
        out_specs=pl.BlockSpec((BLOCK, ATTN_WIDTH), lambda n: (n, 0)),
        out_shape=jax.ShapeDtypeStruct((l, ATTN_WIDTH), BF16), name=name,
        compiler_params=_cp(("parallel",)))(sinks, proj, proj, proj, proj, proj)


def _attn_bwd(proj, sinks, dattn, *, comm=None, name):
    l = proj.shape[0]
    nb = l // BLOCK

    def body(sink_ref, q_ref, kc_ref, kp_ref, vc_ref, vp_ref, do_ref,
             dq_ref, dkc_ref, dkp_ref, dvc_ref, dvp_ref, dsink_ref):
        n = pl.program_id(0)
        dist, valid = _attn_masks(n)
        lane = lax.broadcasted_iota(jnp.int32, (1, BLOCK), 1)
        lo = lane < HEAD_DIM
        kx = jnp.concatenate([kp_ref[...], kc_ref[...]], axis=0).astype(BF16)
        vx = jnp.concatenate([vp_ref[...], vc_ref[...]], axis=0).astype(BF16)
        dsink = jnp.zeros((1, BLOCK), F32)
        dk_heads, dv_heads = [], []
        for kv_head in range(2):
            kdup = _dup_half(kx, kv_head, lo)
            vdup = _dup_half(vx, kv_head, lo)
            qm = _stack_heads(q_ref, kv_head, lo)
            dom = _stack_heads(do_ref, kv_head, lo)
            p, psink = _group_probs(qm, kdup, dist, valid, sink_ref, kv_head)
            dp = lax.dot_general(dom, vdup, (((1,), (1,)), ((), ())), preferred_element_type=F32)
            delta = jnp.sum(p * dp, axis=-1, keepdims=True)
            ds = (p * (dp - delta) * (HEAD_DIM ** -0.5)).astype(BF16)
            dsink_rows = -psink * delta
            for r in range(Q_PER_KV):
                part = jnp.sum(dsink_rows[r * BLOCK:(r + 1) * BLOCK])
                dsink = dsink + jnp.where(lane == kv_head * Q_PER_KV + r, part, 0.0)
            dq = jnp.dot(ds, kdup, preferred_element_type=F32)
            for i, dq_pair in enumerate(_unstack_heads(dq, lo)):
                pair = kv_head * 4 + i
                dq_ref[:, pair * 128:(pair + 1) * 128] = dq_pair.astype(BF16)
            dk_acc = lax.dot_general(ds, qm, (((0,), (0,)), ((), ())), preferred_element_type=F32)
            dv_acc = lax.dot_general(p.astype(BF16), dom, (((0,), (0,)), ((), ())), preferred_element_type=F32)
            dk_heads.append(dk_acc + pltpu.roll(dk_acc, HEAD_DIM, axis=1))
            dv_heads.append(dv_acc + pltpu.roll(dv_acc, HEAD_DIM, axis=1))
        dk = jnp.where(lo, dk_heads[0], dk_heads[1])
        dv = jnp.where(lo, dv_heads[0], dv_heads[1])
        dkp_ref[...] = dk[:BLOCK]
        dkc_ref[...] = dk[BLOCK:]
        dvp_ref[...] = dv[:BLOCK]
        dvc_ref[...] = dv[BLOCK:]

        @pl.when(n == 0)
        def _():
            dsink_ref[...] = jnp.zeros_like(dsink_ref)

        dsink_ref[...] += dsink

    kv = lambda col, prev: pl.BlockSpec(
        (BLOCK, KV_WIDTH), (lambda n: (jnp.maximum(n - 1, 0), col)) if prev else (lambda n: (n, col)))
    qspec = pl.BlockSpec((BLOCK, ATTN_WIDTH), lambda n: (n, 0))
    kvout = pl.BlockSpec((BLOCK, KV_WIDTH), lambda n: (n, 0))
    kvshape = jax.ShapeDtypeStruct((l, KV_WIDTH), F32)
    return _hosted_call(
        body, grid=(nb,),
        in_specs=[pl.BlockSpec(memory_space=pltpu.SMEM), qspec,
                  kv(COL_K, False), kv(COL_K, True), kv(COL_V, False), kv(COL_V, True), qspec],
        out_specs=[qspec, kvout, kvout, kvout, kvout, pl.BlockSpec((1, BLOCK), lambda n: (0, 0))],
        out_shape=[jax.ShapeDtypeStruct((l, ATTN_WIDTH), BF16), kvshape, kvshape, kvshape, kvshape,
                   jax.ShapeDtypeStruct((1, BLOCK), F32)],
        scratch_shapes=[], sem=("arbitrary",), name=name,
        args=(sinks, proj, proj, proj, proj, proj, dattn), comm=comm)


def _kv_grad_merge(dkc, dkp, dvc, dvp, *, name):
    l = dkc.shape[0]
    nb = l // BLOCK

    def body(dkc_ref, dkp_ref, dvc_ref, dvp_ref, o_ref):
        last = pl.program_id(0) == nb - 1
        o_ref[:, :KV_WIDTH] = (dkc_ref[...] + jnp.where(last, 0.0, dkp_ref[...])).astype(BF16)
        o_ref[:, KV_WIDTH:] = (dvc_ref[...] + jnp.where(last, 0.0, dvp_ref[...])).astype(BF16)

    cur = pl.BlockSpec((BLOCK, KV_WIDTH), lambda n: (n, 0))
    nxt = pl.BlockSpec((BLOCK, KV_WIDTH), lambda n: (jnp.minimum(n + 1, nb - 1), 0))
    return pl.pallas_call(
        body, grid=(nb,), in_specs=[cur, nxt, cur, nxt],
        out_specs=pl.BlockSpec((BLOCK, 2 * KV_WIDTH), lambda n: (n, 0)),
        out_shape=jax.ShapeDtypeStruct((l, 2 * KV_WIDTH), BF16), name=name,
        compiler_params=_cp(("parallel",)))(dkc, dkp, dvc, dvp)


def _discretize(a_re, a_im, log_dt, b_re, b_im):
    dt = jnp.exp(log_dt)
    mag = jnp.exp(a_re * dt)
    ab_re = mag * jnp.cos(a_im * dt)
    ab_im = mag * jnp.sin(a_im * dt)
    nr = ab_re - 1.0
    ni = ab_im
    den = a_re * a_re + a_im * a_im
    z_re = (nr * a_re + ni * a_im) / den
    z_im = (ni * a_re - nr * a_im) / den
    bb_re = z_re * b_re - z_im * b_im
    bb_im = z_re * b_im + z_im * b_re
    return ab_re, ab_im, bb_re, bb_im


def _ssm_disc_fwd(a_re, a_im, log_dt, b_re, b_im, *, name):
    def body(ar, ai, ld, br, bi, o_ar, o_ai, o_br, o_bi):
        r = _discretize(ar[...], ai[...], ld[...], br[...], bi[...])
        o_ar[...], o_ai[...], o_br[...], o_bi[...] = r

    col = jax.ShapeDtypeStruct(a_re.shape, F32)
    mat = jax.ShapeDtypeStruct(b_re.shape, F32)
    return pl.pallas_call(body, out_shape=[col, col, mat, mat], name=name)(a_re, a_im, log_dt, b_re, b_im)


def _ssm_disc_bwd(a_re, a_im, log_dt, b_re, b_im, d_ab_re, d_ab_im, d_bb_re, d_bb_im, *, name):
    def body(ar, ai, ld, br, bi, g0, g1, g2, g3, o_ar, o_ai, o_ld, o_br, o_bi):
        _, vjp = jax.vjp(_discretize, ar[...], ai[...], ld[...], br[...], bi[...])
        r = vjp((g0[...], g1[...], g2[...], g3[...]))
        o_ar[...], o_ai[...], o_ld[...], o_br[...], o_bi[...] = r

    col = jax.ShapeDtypeStruct(a_re.shape, F32)
    mat = jax.ShapeDtypeStruct(b_re.shape, F32)
    return pl.pallas_call(body, out_shape=[col, col, col, mat, mat], name=name)(
        a_re, a_im, log_dt, b_re, b_im, d_ab_re, d_ab_im, d_bb_re, d_bb_im)


def _shift_rows(x, d, rows, *, down):
    t = x.shape[0]
    if down:
        return jnp.where(rows >= d, pltpu.roll(x, d, axis=0), 0.0)
    return jnp.where(rows < t - d, pltpu.roll(x, t - d, axis=0), 0.0)


def _scan_chunk(xr, xi, ar, ai, *, down):
    t = xr.shape[0]
    rows = lax.broadcasted_iota(jnp.int32, (t, 1), 0)
    pr, pi = ar, ai
    d = 1
    while d < t:
        sr = _shift_rows(xr, d, rows, down=down)
        si = _shift_rows(xi, d, rows, down=down)
        xr, xi = xr + pr * sr - pi * si, xi + pr * si + pi * sr
        pr, pi = pr * pr - pi * pi, 2.0 * pr * pi
        d *= 2
    return xr, xi


def _ssm_fwd(proj, ab, bd, cd, dskip, *, comm=None, name):
    l = proj.shape[0]
    t = min(SSM_CHUNK, l)
    nc = l // t

    def body(u_ref, ab_ref, bd_ref, cd_ref, ds_ref, y_ref, gy_ref, xs_ref, carry_ref):
        c = pl.program_id(1)

        @pl.when(c == 0)
        def _():
            carry_ref[...] = jnp.zeros_like(carry_ref)

        u = u_ref[...]
        ar, ai = ab_ref[0, 0:1, :], ab_ref[0, 1:2, :]
        bu = jnp.dot(u.astype(BF16), bd_ref[0], preferred_element_type=F32)
        rows = lax.broadcasted_iota(jnp.int32, (t, 1), 0)
        cr, ci = carry_ref[0:1, :], carry_ref[1:2, :]
        xr = bu[:, :SSM_X_BLK] + jnp.where(rows == 0, ar * cr - ai * ci, 0.0)
        xi = bu[:, SSM_X_BLK:] + jnp.where(rows == 0, ar * ci + ai * cr, 0.0)
        xr, xi = _scan_chunk(xr, xi, ar, ai, down=True)
        xs_ref[0, :, :SSM_X_BLK] = xr
        xs_ref[0, :, SSM_X_BLK:] = xi
        carry_ref[0:1, :] = xs_ref[0, t - 1:t, :SSM_X_BLK]
        carry_ref[1:2, :] = xs_ref[0, t - 1:t, SSM_X_BLK:]
        y = jnp.dot(xs_ref[0].astype(BF16), cd_ref[0], preferred_element_type=F32) + ds_ref[...] * u
        y_ref[...] = y
        gy_ref[...] = _gelu(y).astype(BF16)

    blk = lambda shape: pl.BlockSpec((1,) + shape, lambda j, c: (j, 0, 0))
    ycol = pl.BlockSpec((t, SSM_U_BLK), lambda j, c: (c, j))
    return _hosted_call(
        body, grid=(SSM_SPLIT, nc),
        in_specs=[pl.BlockSpec((t, SSM_U_BLK), lambda j, c: (c, COL_U + j)),
                  blk((2, SSM_X_BLK)), blk((SSM_U_BLK, 2 * SSM_X_BLK)), blk((2 * SSM_X_BLK, SSM_U_BLK)),
                  pl.BlockSpec((1, SSM_U_BLK), lambda j, c: (0, j))],
        out_specs=[ycol, ycol, pl.BlockSpec((1, t, 2 * SSM_X_BLK), lambda j, c: (j, c, 0))],
        out_shape=[jax.ShapeDtypeStruct((l, SSM_WIDTH), F32), jax.ShapeDtypeStruct((l, SSM_WIDTH), BF16),
                   jax.ShapeDtypeStruct((SSM_SPLIT, l, 2 * SSM_X_BLK), F32)],
        scratch_shapes=[pltpu.VMEM((2, SSM_X_BLK), F32)], sem=("parallel", "arbitrary"), name=name,
        args=(proj, ab, bd, cd, dskip), comm=comm)


def _ssm_bwd(proj, y, dgy, xs, ab, bdt, cdt, dskip, *, comm=None, name):
    l = proj.shape[0]
    t = min(SSM_CHUNK, l)
    nc = l // t

    def body(u_ref, y_ref, dgy_ref, xs_ref, halo_ref, ab_ref, bdt_ref, cdt_ref, ds_ref,
             du_ref, dbd_ref, dcd_ref, dab_ref, dd_ref, carry_ref):
        c = pl.program_id(1)
        ci_ = nc - 1 - c

        @pl.when(c == 0)
        def _():
            carry_ref[...] = jnp.zeros_like(carry_ref)
            dbd_ref[...] = jnp.zeros_like(dbd_ref)
            dcd_ref[...] = jnp.zeros_like(dcd_ref)
            dab_ref[...] = jnp.zeros_like(dab_ref)
            dd_ref[...] = jnp.zeros_like(dd_ref)

        u = u_ref[...]
        dy = dgy_ref[...] * _gelu_grad(y_ref[...])
        dyb = dy.astype(BF16)
        ar, ai = ab_ref[0, 0:1, :], ab_ref[0, 1:2, :]
        g = jnp.dot(dyb, cdt_ref[0], preferred_element_type=F32)
        rows = lax.broadcasted_iota(jnp.int32, (t, 1), 0)
        cr, ci = carry_ref[0:1, :], carry_ref[1:2, :]
        lr = g[:, :SSM_X_BLK] + jnp.where(rows == t - 1, ar * cr + ai * ci, 0.0)
        li = g[:, SSM_X_BLK:] + jnp.where(rows == t - 1, ar * ci - ai * cr, 0.0)
        lr, li = _scan_chunk(lr, li, ar, -ai, down=False)
        lam = jnp.concatenate([lr, li], axis=1)
        carry_ref[0:1, :] = lr[0:1, :]
        carry_ref[1:2, :] = li[0:1, :]
        lamb = lam.astype(BF16)
        du_ref[...] = (jnp.dot(lamb, bdt_ref[0], preferred_element_type=F32) + ds_ref[...] * dy).astype(BF16)
        dbd_ref[0] += lax.dot_general(u.astype(BF16), lamb, (((0,), (0,)), ((), ())),
                                      preferred_element_type=F32)
        xs = xs_ref[0]
        dcd_ref[0] += lax.dot_general(xs.astype(BF16), dyb, (((0,), (0,)), ((), ())),
                                      preferred_element_type=F32)
        halo = jnp.where(ci_ > 0, halo_ref[0, 7:8, :], 0.0)
        xprev = jnp.where(rows == 0, halo, pltpu.roll(xs, 1, axis=0))
        xpr, xpi = xprev[:, :SSM_X_BLK], xprev[:, SSM_X_BLK:]
        dab_ref[0, 0:1, :] += jnp.sum(lr * xpr + li * xpi, axis=0, keepdims=True)
        dab_ref[0, 1:2, :] += jnp.sum(li * xpr - lr * xpi, axis=0, keepdims=True)
        dd_ref[...] += jnp.sum(dy * u, axis=0, keepdims=True)

    blk = lambda shape: pl.BlockSpec((1,) + shape, lambda j, c: (j, 0, 0))
    rev = lambda j, c: (nc - 1 - c, j)
    ycol = pl.BlockSpec((t, SSM_U_BLK), rev)
    hb = t // 8
    return _hosted_call(
        body, grid=(SSM_SPLIT, nc), comm=comm, sem=("parallel", "arbitrary"), name=name,
        args=(proj, y, dgy, xs, xs, ab, bdt, cdt, dskip), scratch_shapes=[pltpu.VMEM((2, SSM_X_BLK), F32)],
        in_specs=[pl.BlockSpec((t, SSM_U_BLK), lambda j, c: (nc - 1 - c, COL_U + j)), ycol, ycol,
                  pl.BlockSpec((1, t, 2 * SSM_X_BLK), lambda j, c: (j, nc - 1 - c, 0)),
                  pl.BlockSpec((1, 8, 2 * SSM_X_BLK),
                               lambda j, c: (j, jnp.maximum((nc - 1 - c) * hb - 1, 0), 0)),
                  blk((2, SSM_X_BLK)), blk((2 * SSM_X_BLK, SSM_U_BLK)), blk((SSM_U_BLK, 2 * SSM_X_BLK)),
                  pl.BlockSpec((1, SSM_U_BLK), lambda j, c: (0, j))],
        out_specs=[ycol, blk((SSM_U_BLK, 2 * SSM_X_BLK)), blk((2 * SSM_X_BLK, SSM_U_BLK)),
                   blk((2, SSM_X_BLK)), pl.BlockSpec((1, SSM_U_BLK), lambda j, c: (0, j))],
        out_shape=[jax.ShapeDtypeStruct((l, SSM_WIDTH), BF16),
                   jax.ShapeDtypeStruct((SSM_SPLIT, SSM_U_BLK, 2 * SSM_X_BLK), F32),
                   jax.ShapeDtypeStruct((SSM_SPLIT, 2 * SSM_X_BLK, SSM_U_BLK), F32),
                   jax.ShapeDtypeStruct((SSM_SPLIT, 2, SSM_X_BLK), F32),
                   jax.ShapeDtypeStruct((1, SSM_WIDTH), F32)])


def _block_diag(t):
    s, g, a, b = t.shape
    return jnp.einsum('sgab,gk->sgakb', t, jnp.eye(g, dtype=t.dtype)).reshape(s, g * a, g * b)


def _block_diag_take(t, a, b):
    s = t.shape[0]
    return jnp.einsum('sgakb,gk->sgab', t.reshape(s, 8, a, 8, b), jnp.eye(8, dtype=t.dtype))


def _glu_fwd_hook(l, tm):
    def fn(result, ins, outs, i, j):
        z = result()
        outs[0][...] = z
        outs[1][...] = (z[:, :SSM_WIDTH] * _sigmoid(z[:, SSM_WIDTH:])).astype(BF16)

    row = lambda i, j, k: (i, 0)
    return _Hook(fn, outs=[((l, 2 * SSM_WIDTH), F32, (tm, 2 * SSM_WIDTH), row),
                           ((l, SSM_WIDTH), BF16, (tm, SSM_WIDTH), row)])


def _glu_bwd_hook(z, tm):
    l = z.shape[0]

    def fn(result, ins, outs, i, j):
        zv_ref, zg_ref = ins
        dz_ref, db_ref = outs
        d = result()
        sg = _sigmoid(zg_ref[...])
        dv = d * sg
        dg = d * zv_ref[...] * sg * (1.0 - sg)
        dz_ref[:, :SSM_WIDTH] = dv.astype(BF16)
        dz_ref[:, SSM_WIDTH:] = dg.astype(BF16)

        @pl.when(i == 0)
        def _():
            db_ref[...] = jnp.zeros_like(db_ref)

        db_ref[:, :SSM_WIDTH] += jnp.sum(dv, axis=0, keepdims=True)
        db_ref[:, SSM_WIDTH:] += jnp.sum(dg, axis=0, keepdims=True)

    half = (tm, SSM_WIDTH)
    return _Hook(fn, ins=[z, z], in_specs=[(half, lambda i, j, k: (i, 0)), (half, lambda i, j, k: (i, 1))],
                 outs=[((l, 2 * SSM_WIDTH), BF16, (tm, 2 * SSM_WIDTH), lambda i, j, k: (i, 0)),
                       ((1, 2 * SSM_WIDTH), F32, (1, 2 * SSM_WIDTH), lambda i, j, k: (0, 0))])


GATE_TC = 256


def _merge_fwd(proj, a, s, *, name):
    l = a.shape[0]
    tr = min(2048, l)

    def body(ga_ref, gs_ref, a_ref, s_ref, o_ref):
        o_ref[...] = (_sigmoid(ga_ref[...]) * a_ref[...].astype(F32)
                      + _sigmoid(gs_ref[...]) * s_ref[...].astype(F32)).astype(BF16)

    own = pl.BlockSpec((tr, GATE_TC), lambda i, j: (i, j))
    return pl.pallas_call(
        body, grid=(l // tr, D_MODEL // GATE_TC),
        in_specs=[pl.BlockSpec((tr, GATE_TC), lambda i, j: (i, COL_GA // 2 + j)),
                  pl.BlockSpec((tr, GATE_TC), lambda i, j: (i, COL_GS // 2 + j)), own, own],
        out_specs=own, out_shape=jax.ShapeDtypeStruct((l, D_MODEL), BF16), name=name,
        compiler_params=_cp(("parallel", "parallel")))(proj, proj, a, s)


def _merge_bwd_hook(proj, a, s, tm):
    def fn(result, ins, outs, i, j):
        ga_ref, gs_ref, a_br, s_br = ins
        d = result()
        sa = _sigmoid(ga_ref[...])
        ss = _sigmoid(gs_ref[...])
        outs[0][...] = (d * sa).astype(BF16)
        outs[1][...] = (d * ss).astype(BF16)
        outs[2][...] = (d * a_br[...].astype(F32) * sa * (1.0 - sa)).astype(BF16)
        outs[3][...] = (d * s_br[...].astype(F32) * ss * (1.0 - ss)).astype(BF16)

    blk = (tm, GATE_TC)
    own = lambda i, j, k: (i, j)
    return _Hook(fn, ins=[proj, proj, a, s],
                 in_specs=[(blk, lambda i, j, k: (i, COL_GA // 2 + j)), (blk, lambda i, j, k: (i, COL_GS // 2 + j)),
                           (blk, own), (blk, own)],
                 outs=[(a.shape, BF16, blk, own)] * 4)


FF_TC = D_FF // 2
FF_NJ = 2
FF_ROWS = 128


FF_HALO = 16


def _conv_taps(ext, rows):
    h = FF_HALO
    return (ext[h:h + rows], pltpu.roll(ext, 1, axis=0)[h:h + rows], pltpu.roll(ext, 2, axis=0)[h:h + rows])


def _ff_specs(tr, l):
    hb = tr // FF_HALO
    last = l // FF_HALO - 1
    prev = lambda i: jnp.maximum(i * hb - 1, 0)
    nxt = lambda i: jnp.minimum((i + 1) * hb, last)
    return dict(
        own=pl.BlockSpec((tr, FF_TC), lambda j, i: (i, j)),
        own_next=pl.BlockSpec((FF_HALO, FF_TC), lambda j, i: (nxt(i), j)),
        val=pl.BlockSpec((tr, FF_TC), lambda j, i: (i, 2 * j)),
        val_next=pl.BlockSpec((FF_HALO, FF_TC), lambda j, i: (nxt(i), 2 * j)),
        gate=pl.BlockSpec((tr, FF_TC), lambda j, i: (i, 2 * j + 1)),
        gate_prev=pl.BlockSpec((FF_HALO, FF_TC), lambda j, i: (prev(i), 2 * j + 1)),
        gate_next=pl.BlockSpec((FF_HALO, FF_TC), lambda j, i: (nxt(i), 2 * j + 1)),
        pair=pl.BlockSpec((tr, 2 * FF_TC), lambda j, i: (i, j)),
        w=pl.BlockSpec((3, FF_TC), lambda j, i: (0, j)),
        b=pl.BlockSpec((1, FF_TC), lambda j, i: (0, j)))


def _ffn_act_fwd(up, conv_w, conv_b, *, name):
    l = up.shape[0]
    tr = min(FF_ROWS, l)

    def body(v_ref, g_ref, prev_ref, w_ref, b_ref, o_ref):
        prev = jnp.where(pl.program_id(1) == 0, 0.0, prev_ref[...].astype(F32))
        g0, g1, g2 = _conv_taps(jnp.concatenate([prev, g_ref[...].astype(F32)], axis=0), tr)
        gc = b_ref[...] + w_ref[0:1, :] * g2 + w_ref[1:2, :] * g1 + w_ref[2:3, :] * g0
        o_ref[...] = (v_ref[...].astype(F32) * _gelu(gc)).astype(BF16)

    sp = _ff_specs(tr, l)
    return pl.pallas_call(
        body, grid=(FF_NJ, l // tr), in_specs=[sp['val'], sp['gate'], sp['gate_prev'], sp['w'], sp['b']],
        out_specs=sp['own'], out_shape=jax.ShapeDtypeStruct((l, D_FF), BF16), name=name,
        compiler_params=_cp(("parallel", "parallel")))(up, up, up, conv_w, conv_b)


def _ffn_act_bwd(dact, up, conv_w, conv_b, *, comm=None, name):
    l = up.shape[0]
    tr = min(FF_ROWS, l)
    ni = l // tr
    te = tr + 8

    def body(d_ref, dn_ref, v_ref, vn_ref, g_ref, gp_ref, gn_ref, w_ref, b_ref, dup_ref, dw_ref, db_ref):
        i = pl.program_id(1)
        f32 = lambda ref, rows=None: ref[...].astype(F32)[:rows]
        prev = jnp.where(i == 0, 0.0, f32(gp_ref))
        g0, g1, g2 = _conv_taps(jnp.concatenate([prev, f32(g_ref), f32(gn_ref, 8)], axis=0), te)
        w0, w1, w2 = w_ref[0:1, :], w_ref[1:2, :], w_ref[2:3, :]
        gc = b_ref[...] + w0 * g2 + w1 * g1 + w2 * g0
        d_own = f32(d_ref)
        d = jnp.concatenate([d_own, jnp.where(i == ni - 1, 0.0, f32(dn_ref, 8))], axis=0)
        v = jnp.concatenate([f32(v_ref), f32(vn_ref, 8)], axis=0)
        dgc = d * v * _gelu_grad(gc)
        ahead1 = pltpu.roll(dgc, te - 1, axis=0)[:tr]
        ahead2 = pltpu.roll(dgc, te - 2, axis=0)[:tr]
        own = dgc[:tr]
        dup_ref[:, :FF_TC] = (d_own * _gelu(gc[:tr])).astype(BF16)
        dup_ref[:, FF_TC:] = (w2 * own + w1 * ahead1 + w0 * ahead2).astype(BF16)

        @pl.when(i == 0)
        def _():
            dw_ref[...] = jnp.zeros_like(dw_ref)
            db_ref[...] = jnp.zeros_like(db_ref)

        dw_ref[0:1, :] += jnp.sum(own * g2[:tr], axis=0, keepdims=True)
        dw_ref[1:2, :] += jnp.sum(own * g1[:tr], axis=0, keepdims=True)
        dw_ref[2:3, :] += jnp.sum(own * g0[:tr], axis=0, keepdims=True)
        db_ref[...] += jnp.sum(own, axis=0, keepdims=True)

    sp = _ff_specs(tr, l)
    return _hosted_call(
        body, grid=(FF_NJ, ni),
        in_specs=[sp['own'], sp['own_next'], sp['val'], sp['val_next'], sp['gate'], sp['gate_prev'],
                  sp['gate_next'], sp['w'], sp['b']],
        out_specs=[sp['pair'], sp['w'], sp['b']],
        out_shape=[jax.ShapeDtypeStruct((l, 2 * D_FF), BF16), jax.ShapeDtypeStruct((3, D_FF), F32),
                   jax.ShapeDtypeStruct((1, D_FF), F32)],
        scratch_shapes=[], sem=("parallel", "arbitrary"), name=name,
        args=(dact, dact, up, up, up, up, up, conv_w, conv_b), comm=comm)


def _join_columns(pieces, *, name):
    l = pieces[0].shape[0]
    widths = [p.shape[1] for p in pieces]
    n = sum(widths)
    tr = min(512, l)

    def body(*refs):
        ins, o_ref, s_ref = refs[:len(pieces)], refs[-2], refs[-1]

        @pl.when(pl.program_id(0) == 0)
        def _():
            s_ref[...] = jnp.zeros_like(s_ref)

        col = 0
        for p_ref, width in zip(ins, widths):
            v = p_ref[...]
            o_ref[:, col:col + width] = v
            s_ref[:, col:col + width] += jnp.sum(v.astype(F32), axis=0, keepdims=True)
            col += width

    return pl.pallas_call(
        body, grid=(l // tr,), in_specs=[pl.BlockSpec((tr, width), lambda i: (i, 0)) for width in widths],
        out_specs=[pl.BlockSpec((tr, n), lambda i: (i, 0)), pl.BlockSpec((1, n), lambda i: (0, 0))],
        out_shape=[jax.ShapeDtypeStruct((l, n), BF16), jax.ShapeDtypeStruct((1, n), F32)],
        name=name, compiler_params=_cp(("arbitrary",)))(*pieces)


def _local_step(x, target, wts, small, shards=None):
    l = x.shape[0]
    wts = dict(wts)
    grads, recvs, sgr = {}, {}, {}
    lay = lambda keys: [LAYOUT[k] for k in keys]
    none = lambda keys: None
    gather = (lambda keys: _GatherPlan([shards[k] for k in keys], lay(keys))) if shards is not None else none
    scatter = (lambda keys: _ScatterPlan([grads[k] for k in keys], lay(keys))) if shards is not None else none

    mm = _matmul

    def take(res, plan, keys, store):
        outs, couts = res
        if plan is not None:
            store.update(zip(keys, couts))
        return outs

    def mm_plan(plan, keys, store, *args, **kw):
        if plan is None:
            return _matmul(*args, **kw)
        return take(_matmul(*args, comm=plan, **kw), plan, keys, store)

    def mm_host(keys, make_plan, store, *args, **kw):
        return mm_plan(make_plan(keys), keys, store, *args, **kw)

    up_scatter = lambda p: _ScatterPlan([grads['w_up']], lay(['w_up']), part=(p, 2)) if shards is not None else None

    col = lambda t: t.reshape(SSM_GROUPS * SSM_STATE, 1)
    a_re, a_im = col(small['ssm_a_re']), col(small['ssm_a_im'])
    log_dt = jnp.repeat(small['ssm_log_dt'].reshape(SSM_GROUPS), SSM_STATE).reshape(-1, 1)
    b_re = small['ssm_b_re'].reshape(SSM_GROUPS * SSM_STATE, SSM_GROUP)
    b_im = small['ssm_b_im'].reshape(SSM_GROUPS * SSM_STATE, SSM_GROUP)
    ab_re, ab_im, bb_re, bb_im = _ssm_disc_fwd(a_re, a_im, log_dt, b_re, b_im, name="ssm_disc_fwd")
    ab = jnp.stack([ab_re.reshape(SSM_SPLIT, SSM_X_BLK), ab_im.reshape(SSM_SPLIT, SSM_X_BLK)], axis=1)
    to_bd = lambda t: _block_diag(t.reshape(SSM_SPLIT, 8, SSM_STATE, SSM_GROUP).transpose(0, 1, 3, 2))
    bd = jnp.concatenate([to_bd(bb_re), to_bd(bb_im)], axis=2)
    c_re = small['ssm_c_re'].reshape(SSM_SPLIT, 8, SSM_GROUP, SSM_STATE)
    c_im = small['ssm_c_im'].reshape(SSM_SPLIT, 8, SSM_GROUP, SSM_STATE)
    cdt = jnp.concatenate([_block_diag(c_re), -_block_diag(c_im)], axis=2)
    bd_b, cdt_b = bd.astype(BF16), cdt.astype(BF16)
    bdt_b, cd_b = bd_b.transpose(0, 2, 1), cdt_b.transpose(0, 2, 1)
    dskip = small['ssm_d'].reshape(1, SSM_WIDTH)

    sinks = small['attn_sinks'].reshape(N_Q_HEADS)
    plan = gather(['w_in_t'])
    h1, = take(_rms_fwd(x, small['attn_norm_g'], comm=plan, name="rms1_fwd"), plan, ['w_in_t'], wts)
    proj = mm_host(['w_glu', 'w_ba', 'w_bs', 'w_out'], gather, wts,
                   h1, wts['w_in_t'], tb=True, tm=512, tn=2944, tk=2048, inner='m', out_dtype=F32,
                   bias=small['b_in'], name="mm_in")
    attn = _attn_fwd(proj, sinks, name="attn_fwd")
    plan = gather(['w_up'])
    y, gy, xs = take(_ssm_fwd(proj, ab, bd_b, cd_b, dskip, comm=plan, name="ssm_fwd"), plan, ['w_up'], wts)
    z, ssm = mm(gy, wts['w_glu'], tm=1024, tn=1024, tk=512, bias=small['b_glu'],
                epilogue=_glu_fwd_hook(l, min(1024, l)), name="mm_glu")
    a_br = mm(attn, wts['w_ba'], tm=1024, tn=1024, tk=1024, out_dtype=BF16, name="mm_ba")
    s_br = mm(ssm, wts['w_bs'], tm=1024, tn=1024, tk=512, out_dtype=BF16, name="mm_bs")
    tr = min(ROW_TILE, l)
    merged = _merge_fwd(proj, a_br, s_br, name="merge_fwd")
    x2 = mm(merged, wts['w_out'], tm=1024, tn=1024, tk=2048, inner='m', out_dtype=F32, res=x, name="mm_out")
    h2, = take(_rms_fwd(x2, small['ffn_norm_g'], name="rms2_fwd"), None, [], wts)
    up = mm_host(['w_down'], gather, wts,
                 h2, wts['w_up'], tm=1024, tn=1024, tk=2048, out_dtype=BF16, name="mm_up")
    conv_w, conv_b = small['conv_w'], small['conv_b']
    act = _ffn_act_fwd(up, conv_w, conv_b, name="ffn_act_fwd")
    x3 = mm(act, wts['w_down'], tm=1024, tn=1024, tk=2816, out_dtype=F32, res=x2, name="mm_down")
    dx3b, d_g3, loss = _rowwise(
        _final_loss_hook(small['final_norm_g'].reshape(1, D_MODEL), target, tr), x3, tr, name="final_loss")

    sgr['final_norm_g'] = d_g3.reshape(D_MODEL)
    dact = mm(dx3b, wts['w_down'], tb=True, tm=512, tn=2816, tk=2048, inner='m', out_dtype=BF16, name="mm_dact")
    grads['w_down'] = mm(act, dx3b, ta=True, tm=512, tn=1024, tk=2048, out_dtype=BF16, name="mm_dw_down")
    plan = scatter(['w_down'])
    dup, sgr['conv_w'], sgr['conv_b'] = take(
        _ffn_act_bwd(dact, up, conv_w, conv_b, comm=plan, name="ffn_act_bwd"), plan, ['w_down'], recvs)
    grads['w_up'] = mm(h2, dup, ta=True, tm=1024, tn=1024, tk=2048, out_dtype=BF16, name="mm_dw_up")
    dh2 = mm_plan(up_scatter(0), ['w_up#0'], recvs,
                  dup, wts['w_up'], tb=True, tm=1024, tn=1024, tk=2816, out_dtype=BF16, name="mm_dh2")
    dx2b, sgr['ffn_norm_g'] = _rowwise(
        _rms_bwd_hook(x2, small['ffn_norm_g'], dx3b, tr, BF16), dh2, tr, name="rms2_bwd")

    d_a, d_s, dga, dgs = mm(dx2b, wts['w_out'], tb=True, tm=1024, tn=GATE_TC, tk=2048,
                            epilogue=_merge_bwd_hook(proj, a_br, s_br, min(1024, l)), name="mm_dmerged")
    grads['w_out'] = mm(merged, dx2b, ta=True, tm=1024, tn=1024, tk=2048, out_dtype=BF16, name="mm_dw_out")
    dattn = mm(d_a, wts['w_ba'], tb=True, tm=1024, tn=1024, tk=2048, inner='m', out_dtype=BF16, name="mm_dattn")
    grads['w_ba'] = mm(attn, d_a, ta=True, tm=1024, tn=1024, tk=2048, out_dtype=BF16, name="mm_dw_ba")
    dz, sgr['b_glu'] = mm(d_s, wts['w_bs'], tb=True, tm=1024, tn=512, tk=2048, sequential=True,
                          epilogue=_glu_bwd_hook(z, min(1024, l)), name="mm_dssm")
    grads['w_bs'] = mm(ssm, d_s, ta=True, tm=512, tn=1024, tk=2048, out_dtype=BF16, name="mm_dw_bs")
    grads['w_glu'] = mm(gy, dz, ta=True, tm=512, tn=1024, tk=2048, out_dtype=BF16, name="mm_dw_glu")
    dgy = mm(dz, wts['w_glu'], tb=True, tm=1024, tn=512, tk=1024, inner='m', out_dtype=F32, name="mm_dgy")
    plan = up_scatter(1)
    du, d_bd, d_cd, d_ab, sgr['ssm_d'] = take(
        _ssm_bwd(proj, y, dgy, xs, ab, bdt_b, cdt_b, dskip, comm=plan, name="ssm_bwd"), plan, ['w_up#1'], recvs)
    keys = ['w_out', 'w_ba', 'w_bs', 'w_glu']
    plan = scatter(keys)
    dq, dkc, dkp, dvc, dvp, dsink = take(
        _attn_bwd(proj, sinks, dattn, comm=plan, name="attn_bwd"), plan, keys, recvs)
    dkv = _kv_grad_merge(dkc, dkp, dvc, dvp, name="kv_grad_merge")
    sgr['attn_sinks'] = dsink[:, :N_Q_HEADS]
    dproj, sgr['b_in'] = _join_columns([dq, dkv, du, dga, dgs], name="join_dproj")
    grads['w_in_t'] = mm(dproj, h1, ta=True, tm=2944, tn=1024, tk=1024, out_dtype=BF16, name="mm_dw_in")
    dh1 = mm_host(['w_in_t'], scatter, recvs,
                  dproj, wts['w_in_t'], tm=1024, tn=1024, tk=2944, out_dtype=BF16, name="mm_dh1")
    grad_x, sgr['attn_norm_g'] = _rowwise(
        _rms_bwd_hook(x, small['attn_norm_g'], dx2b, tr, F32), dh1, tr, name="rms1_bwd")

    from_bd = lambda t: _block_diag_take(t, SSM_GROUP, SSM_STATE).transpose(0, 1, 3, 2).reshape(
        SSM_GROUPS * SSM_STATE, SSM_GROUP)
    d_bb_re = from_bd(d_bd[:, :, :SSM_X_BLK])
    d_bb_im = from_bd(d_bd[:, :, SSM_X_BLK:])
    d_cdt = d_cd.transpose(0, 2, 1)
    shape_c = (1, SSM_GROUPS, SSM_GROUP, SSM_STATE)
    sgr['ssm_c_re'] = _block_diag_take(d_cdt[:, :, :SSM_X_BLK], SSM_GROUP, SSM_STATE).reshape(shape_c)
    sgr['ssm_c_im'] = -_block_diag_take(d_cdt[:, :, SSM_X_BLK:], SSM_GROUP, SSM_STATE).reshape(shape_c)
    d_a_re, d_a_im, d_ldt, d_b_re, d_b_im = _ssm_disc_bwd(
        a_re, a_im, log_dt, b_re, b_im, d_ab[:, 0, :].reshape(-1, 1), d_ab[:, 1, :].reshape(-1, 1),
        d_bb_re, d_bb_im, name="ssm_disc_bwd")
    sgr['ssm_a_re'] = d_a_re.reshape(1, SSM_GROUPS, SSM_STATE)
    sgr['ssm_a_im'] = d_a_im.reshape(1, SSM_GROUPS, SSM_STATE)
    sgr['ssm_log_dt'] = d_ldt.reshape(SSM_GROUPS, SSM_STATE).sum(axis=1).reshape(1, SSM_GROUPS)
    sgr['ssm_b_re'] = d_b_re.reshape(1, SSM_GROUPS, SSM_STATE, SSM_GROUP)
    sgr['ssm_b_im'] = d_b_im.reshape(1, SSM_GROUPS, SSM_STATE, SSM_GROUP)
    return loss, grad_x, grads, recvs, sgr


def _swap_cores(arrs, *, name):
    n = len(arrs)

    def body(*refs):
        ins, outs = refs[:n], refs[n:2 * n]
        send_sems, recv_sems = refs[2 * n:]
        x, y, c = _place()
        copies = []
        for i in range(n):
            cp = pltpu.make_async_remote_copy(
                src_ref=ins[i], dst_ref=outs[i], send_sem=send_sems.at[i], recv_sem=recv_sems.at[i],
                device_id=(x, y, 1 - c), device_id_type=MESH)
            cp.start()
            copies.append(cp)
        for cp in copies:
            cp.wait()

    return pl.pallas_call(
        body, in_specs=[ANY] * n, out_specs=[ANY] * n,
        out_shape=[jax.ShapeDtypeStruct(a.shape, a.dtype) for a in arrs],
        scratch_shapes=[pltpu.SemaphoreType.DMA((n,)), pltpu.SemaphoreType.DMA((n,))],
        name=name)(*arrs)


def _all_reduce_small(buf, *, name):
    r = buf.shape[0]

    def body(in_ref, out_ref, slots, send_sems, recv_sems):
        x, y, c = _place()
        me = 4 * x + 2 * y + c
        slots[pl.ds(me, 1)] = in_ref[...][None]
        copies = []
        for k in range(N_DEV - 1):
            bx, by, bc = ((k + 1) >> 2) & 1, ((k + 1) >> 1) & 1, (k + 1) & 1
            peer = (1 - x if bx else x, 1 - y if by else y, 1 - c if bc else c)
            cp = pltpu.make_async_remote_copy(
                src_ref=in_ref, dst_ref=slots.at[me], send_sem=send_sems.at[k], recv_sem=recv_sems.at[k],
                device_id=peer, device_id_type=MESH)
            cp.start()
            copies.append(cp)
        for cp in copies:
            cp.wait()
        acc = slots[0]
        for d in range(1, N_DEV):
            acc = acc + slots[d]
        out_ref[...] = acc

    vm = pl.BlockSpec(memory_space=pltpu.VMEM)
    return pl.pallas_call(
        body, in_specs=[vm], out_specs=vm, out_shape=jax.ShapeDtypeStruct((r, 128), F32),
        scratch_shapes=[pltpu.VMEM((N_DEV, r, 128), F32), pltpu.SemaphoreType.DMA((N_DEV - 1,)),
                        pltpu.SemaphoreType.DMA((N_DEV - 1,))],
        name=name)(buf)


def _pack(arrs):
    flat = jnp.concatenate([a.reshape(-1).astype(F32) for a in arrs])
    pad = (-flat.shape[0]) % 1024
    return jnp.pad(flat, (0, pad)).reshape(-1, 128)


def _unpack(buf, shapes):
    flat = buf.reshape(-1)
    out, pos = [], 0
    for s in shapes:
        size = math.prod(s)
        out.append(flat[pos:pos + size].reshape(s))
        pos += size
    return out


TILE_ELEMS = 256 * 1024


def _tile_rows(r, c):
    if r * c <= TILE_ELEMS:
        return r
    for tr in range(TILE_ELEMS // c // 16 * 16, 0, -16):
        if r % tr == 0:
            return tr
    raise ValueError((r, c))


def _sum4(full, axis, pos, recvs, *, name):
    r, c = _block_shape(full.shape, axis)
    parts = len(recvs)
    tr = _tile_rows(r // parts, c)
    per = r // parts // tr
    nt = r // tr

    def body(pos_ref, o_ref, *refs):
        out_ref = refs[parts]
        for p in range(parts):
            @pl.when(pl.program_id(0) // per == p)
            def _():
                acc = o_ref[...].astype(F32)
                for k in range(3):
                    acc = acc + refs[p][k].astype(F32)
                out_ref[...] = acc.astype(BF16)

    own = (pl.BlockSpec((tr, c), lambda i, pos_ref: (pos_ref[0] * nt + i, 0)) if axis == 0
           else pl.BlockSpec((tr, c), lambda i, pos_ref: (i, pos_ref[0])))
    part_spec = lambda p: pl.BlockSpec((3, tr, c), lambda i, pos_ref: (0, jnp.clip(i - p * per, 0, per - 1), 0))
    grid_spec = pltpu.PrefetchScalarGridSpec(
        num_scalar_prefetch=1, grid=(nt,), in_specs=[own] + [part_spec(p) for p in range(parts)],
        out_specs=pl.BlockSpec((tr, c), lambda i, pos_ref: (i, 0)))
    return pl.pallas_call(
        body, grid_spec=grid_spec, out_shape=jax.ShapeDtypeStruct((r, c), BF16),
        name=name, compiler_params=_cp(("parallel",)))(pos, full, *recvs)


def _adam_step(w, g, m, v):
    bc1 = 1.0 - ADAM_B1 ** ADAM_STEP
    bc2 = 1.0 - ADAM_B2 ** ADAM_STEP
    mn = ADAM_B1 * m + (1.0 - ADAM_B1) * g
    vn = ADAM_B2 * v + (1.0 - ADAM_B2) * (g * g)
    m_hat = mn / bc1
    v_hat = vn / bc2
    return -ADAM_LR * (m_hat / (jnp.sqrt(v_hat) + ADAM_EPS) + ADAM_WD * w), mn, vn


def _adamw(w, ga, gb, m, v, *, name):
    r, c = w.shape
    tr = _tile_rows(r, c)

    def body(w_ref, ga_ref, gb_ref, m_ref, v_ref, g_out, d_out, m_out, v_out):
        g = ga_ref[...].astype(F32) + gb_ref[...].astype(F32)
        g_out[...] = g
        d_out[...], m_out[...], v_out[...] = _adam_step(w_ref[...], g, m_ref[...], v_ref[...])

    spec = pl.BlockSpec((tr, c), lambda i: (i, 0))
    shp = jax.ShapeDtypeStruct((r, c), F32)
    return pl.pallas_call(
        body, grid=(r // tr,), in_specs=[spec] * 5, out_specs=[spec] * 4,
        out_shape=[shp] * 4, name=name, compiler_params=_cp(("parallel",)))(w, ga, gb, m, v)


def _lanes(t):
    return t.reshape(-1, 128) if t.size % 128 == 0 else t.reshape(1, -1)


def _adamw_small(ws, gs, ms, vs, *, name):
    n = len(ws)

    def body(*refs):
        for i in range(n):
            w_ref, g_ref, m_ref, v_ref = (refs[k * n + i] for k in range(4))
            outs = [refs[(4 + k) * n + i] for k in range(3)]
            outs[0][...], outs[1][...], outs[2][...] = _adam_step(w_ref[...], g_ref[...], m_ref[...], v_ref[...])

    flat = [_lanes(t) for group in (ws, gs, ms, vs) for t in group]
    shp = [jax.ShapeDtypeStruct(_lanes(t).shape, F32) for t in ws]
    res = pl.pallas_call(body, out_shape=shp * 3, name=name)(*flat)
    return [[res[k * n + i].reshape(ws[i].shape) for i in range(n)] for k in range(3)]


BIG = ['w_in', 'w_glu', 'w_branch_attn', 'w_branch_ssm', 'w_out', 'w_up', 'w_down']
BIG_KEY = {'w_in': 'w_in_t', 'w_glu': 'w_glu', 'w_branch_attn': 'w_ba', 'w_branch_ssm': 'w_bs',
           'w_out': 'w_out', 'w_up': 'w_up', 'w_down': 'w_down'}
TRANSPOSED = {'w_in'}
SMALL = ['attn_norm_g', 'b_in', 'attn_sinks', 'ssm_a_re', 'ssm_a_im', 'ssm_log_dt', 'ssm_b_re', 'ssm_b_im',
         'ssm_c_re', 'ssm_c_im', 'ssm_d', 'b_glu', 'ffn_norm_g', 'conv_b', 'final_norm_g']
WEIGHTS = ['attn_norm_g', 'w_in', 'b_in', 'attn_sinks', 'ssm_a_re', 'ssm_a_im', 'ssm_log_dt', 'ssm_b_re',
           'ssm_b_im', 'ssm_c_re', 'ssm_c_im', 'ssm_d', 'w_glu', 'b_glu', 'w_branch_attn', 'w_branch_ssm',
           'w_out', 'ffn_norm_g', 'w_up', 'conv_w', 'conv_b', 'w_down', 'final_norm_g']


def _shard_2d(name, t):
    t = t[0]
    return t.T if name in TRANSPOSED else t


def _unshard_2d(name, t):
    return (t.T if name in TRANSPOSED else t)[None]


def kernel(x, attn_norm_g, w_in, b_in, attn_sinks, ssm_a_re, ssm_a_im, ssm_log_dt, ssm_b_re, ssm_b_im, ssm_c_re, ssm_c_im, ssm_d, w_glu, b_glu, w_branch_attn, w_branch_ssm, w_out, ffn_norm_g, w_up, conv_w, conv_b, w_down, final_norm_g, loss_target, m_attn_norm_g, m_w_in, m_b_in, m_attn_sinks, m_ssm_a_re, m_ssm_a_im, m_ssm_log_dt, m_ssm_b_re, m_ssm_b_im, m_ssm_c_re, m_ssm_c_im, m_ssm_d, m_w_glu, m_b_glu, m_w_branch_attn, m_w_branch_ssm, m_w_out, m_ffn_norm_g, m_w_up, m_conv_w, m_conv_b, m_w_down, m_final_norm_g, v_attn_norm_g, v_w_in, v_b_in, v_attn_sinks, v_ssm_a_re, v_ssm_a_im, v_ssm_log_dt, v_ssm_b_re, v_ssm_b_im, v_ssm_c_re, v_ssm_c_im, v_ssm_d, v_w_glu, v_b_glu, v_w_branch_attn, v_w_branch_ssm, v_w_out, v_ffn_norm_g, v_w_up, v_conv_w, v_conv_b, v_w_down, v_final_norm_g):
    args = dict(locals())
    w = {n: args[n] for n in WEIGHTS}
    m = {n: args['m_' + n] for n in WEIGHTS}
    v = {n: args['v_' + n] for n in WEIGHTS}
    xi, yi, ci = _place()
    blk = 2 * xi + yi

    shards = {BIG_KEY[n]: _shard_2d(n, w[n]).astype(BF16) for n in BIG}
    cw_cols = w['conv_w'].shape[2]
    cw_place = lax.dynamic_update_slice(jnp.zeros((3, D_FF), F32), w['conv_w'][0] * (ci == 0).astype(F32),
                                        (0, blk * cw_cols))
    conv_w_full = _unpack(_all_reduce_small(_pack([cw_place]), name="gather_conv_w"), [(3, D_FF)])[0]

    small = {n: w[n] for n in SMALL}
    small['conv_w'] = conv_w_full
    loss_part, grad_x, grads, recvs, sgr = _local_step(x[0], loss_target[0], {}, small, shards)

    halves = []
    for n in BIG:
        key = BIG_KEY[n]
        full = grads[key]
        recv = [recvs[key]] if key in recvs else [recvs[key + '#0'], recvs[key + '#1']]
        axis, interleaved = LAYOUT[key]
        pos = _block_pos(xi, yi, interleaved).astype(jnp.int32).reshape(1)
        halves.append(_sum4(full, axis, pos, recv, name="sum4_" + n))
    others = _swap_cores(halves, name="swap_cores")
    out = {}
    for n, mine, other in zip(BIG, halves, others):
        res = _adamw(_shard_2d(n, w[n]), mine, other, _shard_2d(n, m[n]), _shard_2d(n, v[n]), name="adamw_" + n)
        out[n] = [_unshard_2d(n, t) for t in res]

    names = SMALL + ['conv_w']
    shapes = [w[n].shape for n in SMALL] + [(3, D_FF)]
    packed = _pack([sgr[n] for n in names] + [loss_part])
    summed = _unpack(_all_reduce_small(packed, name="all_reduce_small"), shapes + [(1, 1)])
    loss = summed[-1].reshape(())
    sg = dict(zip(names, summed[:-1]))
    sg['conv_w'] = lax.dynamic_slice_in_dim(sg['conv_w'], blk * cw_cols, cw_cols, axis=1)[None]
    deltas, new_m, new_v = _adamw_small([w[n] for n in names], [sg[n] for n in names], [m[n] for n in names],
                                        [v[n] for n in names], name="adamw_small")
    for i, n in enumerate(names):
        out[n] = [sg[n], deltas[i], new_m[i], new_v[i]]

    return (loss, grad_x[None], *[out[n][0] for n in WEIGHTS], *[out[n][1] for n in WEIGHTS],
            *[out[n][2] for n in WEIGHTS], *[out[n][3] for n in WEIGHTS])
```

```python
import functools
import math

import jax
import jax.numpy as jnp
from jax import lax
from jax.experimental import pallas as pl
from jax.experimental.pallas import tpu as pltpu

F32 = jnp.float32
BF16 = jnp.bfloat16

D_MODEL = 2048
N_Q_HEADS = 16
HEAD_DIM = 64
ATTN_WIDTH = 1024
KV_WIDTH = 128
BLOCK = 128
SSM_WIDTH = 512
SSM_GROUPS = 32
SSM_GROUP = 16
SSM_STATE = 64
D_FF = 5632
IN_COLS = 5888
RMS_EPS = 1e-6
NEG_BIG = -1e30
N_CHIPS = 4
N_DEV = 8

COL_K = 8
COL_V = 9
COL_U = 10
COL_GA = 14
COL_GS = 30

SSM_SPLIT = 4
SSM_U_BLK = 128
SSM_X_BLK = 512
SSM_CHUNK = 256

ADAM_LR = 0.001
ADAM_B1 = 0.9
ADAM_B2 = 0.999
ADAM_EPS = 1e-08
ADAM_WD = 0.01
ADAM_STEP = 10

VMEM_LIMIT_BYTES = 56 * 1024 * 1024
INV_SQRT2 = 1.0 / math.sqrt(2.0)
INV_SQRT2PI = 1.0 / math.sqrt(2.0 * math.pi)
MESH = pl.DeviceIdType.MESH
ANY = pl.BlockSpec(memory_space=pl.ANY)


def _cp(sem):
    return pltpu.CompilerParams(dimension_semantics=sem, vmem_limit_bytes=VMEM_LIMIT_BYTES)


def _gelu(x):
    return 0.5 * x * (1.0 + lax.erf(x * INV_SQRT2))


def _gelu_grad(x):
    return 0.5 * (1.0 + lax.erf(x * INV_SQRT2)) + x * jnp.exp(-0.5 * x * x) * INV_SQRT2PI


def _sigmoid(x):
    return 1.0 / (1.0 + jnp.exp(-x))


def _place():
    return lax.axis_index("x"), lax.axis_index("y"), lax.axis_index("c")


def _other_chips(x, y):
    return [(1 - x, y), (x, 1 - y), (1 - x, 1 - y)]


def _block_pos(x, y, interleaved):
    return x + 2 * y if interleaved else 2 * x + y


LAYOUT = {'w_in_t': (0, False), 'w_glu': (1, False), 'w_ba': (1, False), 'w_bs': (1, False),
          'w_out': (0, False), 'w_up': (1, True), 'w_down': (0, False)}


def _window(ref, axis, pos, size, rows=None):
    if axis == 0:
        start, count = (0, size) if rows is None else rows
        return ref.at[pl.ds(pos * size + start, count), :]
    cols = pl.ds(pos * size, size)
    return ref.at[:, cols] if rows is None else ref.at[pl.ds(rows[0], rows[1]), cols]


def _gathered_shape(shape, axis):
    return tuple(N_CHIPS * d if a == axis else d for a, d in enumerate(shape))


def _block_shape(shape, axis):
    return tuple(d // N_CHIPS if a == axis else d for a, d in enumerate(shape))


class _GatherPlan:
    def __init__(self, shards, layouts):
        self.arrays = list(shards)
        self.layouts = list(layouts)
        n = len(shards)
        self.out_shape = [jax.ShapeDtypeStruct(_gathered_shape(s.shape, lay[0]), s.dtype)
                          for s, lay in zip(shards, layouts)]
        self.scratch = [pltpu.SemaphoreType.DMA((6 * n,)), pltpu.SemaphoreType.DMA((6 * n,)),
                        pltpu.SemaphoreType.DMA((n,))]

    def _copies(self, kind, ins, outs, sems):
        send, recv, local = sems
        n = len(self.arrays)
        x, y, c = _place()
        copies = []
        for i in range(n):
            axis, interleaved = self.layouts[i]
            size = self.arrays[i].shape[axis]
            h = self.arrays[i].shape[0] // 2
            first = lambda core: core * h
            blk = _block_pos(x, y, interleaved)
            if kind == 'mine':
                copies.append(pltpu.make_async_copy(ins[i], _window(outs[i], axis, blk, size), local.at[i]))
                continue
            for k, (px, py) in enumerate(_other_chips(x, y)):
                theirs = _block_pos(px, py, interleaved)
                if kind in ('ici_out', 'ici_in'):
                    route = dict(send_sem=send.at[3 * i + k], recv_sem=recv.at[3 * i + k],
                                 device_id=(px, py, c), device_id_type=MESH)
                else:
                    route = dict(send_sem=send.at[3 * (n + i) + k], recv_sem=recv.at[3 * (n + i) + k],
                                 device_id=(x, y, 1 - c), device_id_type=MESH)
                if kind == 'ici_out':
                    src, dst = ins[i].at[pl.ds(first(c), h), :], _window(outs[i], axis, blk, size, (first(c), h))
                elif kind == 'd2d_in':
                    src = dst = _window(outs[i], axis, theirs, size, (first(1 - c), h))
                else:
                    src = dst = _window(outs[i], axis, theirs, size, (first(c), h))
                copies.append(pltpu.make_async_remote_copy(src_ref=src, dst_ref=dst, **route))
        return copies

    def start(self, ins, outs, sems):
        for cp in self._copies('mine', ins, outs, sems) + self._copies('ici_out', ins, outs, sems):
            cp.start()

    def middle(self, ins, outs, sems):
        for arrived, onward in zip(self._copies('ici_in', ins, outs, sems), self._copies('d2d_out', ins, outs, sems)):
            arrived.wait_recv()
            onward.start()

    def finish(self, ins, outs, sems):
        for cp in self._copies('d2d_in', ins, outs, sems):
            cp.wait_recv()
        for cp in self._copies('ici_out', ins, outs, sems) + self._copies('d2d_out', ins, outs, sems):
            cp.wait_send()
        for cp in self._copies('mine', ins, outs, sems):
            cp.wait()


class _ScatterPlan:
    def __init__(self, fulls, layouts, part=(0, 1)):
        self.arrays = list(fulls)
        self.layouts = list(layouts)
        self.part = part
        n = len(fulls)
        self.out_shape = []
        for f, lay in zip(fulls, layouts):
            rows, cols = _block_shape(f.shape, lay[0])
            self.out_shape.append(jax.ShapeDtypeStruct((3, rows // part[1], cols), f.dtype))
        self.scratch = [pltpu.SemaphoreType.DMA((3 * n,)), pltpu.SemaphoreType.DMA((3 * n,))]

    def _copies(self, ins, outs, sems):
        send, recv = sems
        x, y, c = _place()
        copies = []
        for i in range(len(self.arrays)):
            axis, interleaved = self.layouts[i]
            size = self.arrays[i].shape[axis] // N_CHIPS
            h = _block_shape(self.arrays[i].shape, axis)[0] // self.part[1]
            rows = (self.part[0] * h, h)
            for k, (px, py) in enumerate(_other_chips(x, y)):
                copies.append(pltpu.make_async_remote_copy(
                    src_ref=_window(ins[i], axis, _block_pos(px, py, interleaved), size, rows), dst_ref=outs[i].at[k],
                    send_sem=send.at[3 * i + k], recv_sem=recv.at[3 * i + k],
                    device_id=(px, py, c), device_id_type=MESH))
        return copies

    def start(self, ins, outs, sems):
        for cp in self._copies(ins, outs, sems):
            cp.start()

    def middle(self, ins, outs, sems):
        pass

    def finish(self, ins, outs, sems):
        for cp in self._copies(ins, outs, sems):
            cp.wait()


def _hosted_call(body, *, grid, in_specs, out_specs, out_shape, scratch_shapes, sem, name, args, comm=None,
                 aliases=None):
    aliases = aliases or {}
    if comm is None:
        outs = pl.pallas_call(body, grid=grid, in_specs=in_specs, out_specs=out_specs, out_shape=out_shape,
                              scratch_shapes=scratch_shapes, name=name, input_output_aliases=aliases,
                              compiler_params=_cp(sem))(*args)
        return outs, None
    n_in, n_out, n_scr = len(in_specs), len(out_specs), len(scratch_shapes)
    nc, ns = len(comm.arrays), len(comm.scratch)
    total = math.prod(grid)
    mid = total - max(1, total // 8)

    def wrapped(*refs):
        pos = 0
        ins = refs[pos:pos + n_in]; pos += n_in
        cins = refs[pos:pos + nc]; pos += nc
        outs = refs[pos:pos + n_out]; pos += n_out
        couts = refs[pos:pos + nc]; pos += nc
        scr = refs[pos:pos + n_scr]; pos += n_scr
        sems = refs[pos:pos + ns]
        step = 0
        for ax, g in enumerate(grid):
            step = step * g + pl.program_id(ax)

        @pl.when(step == 0)
        def _():
            comm.start(cins, couts, sems)

        body(*ins, *outs, *scr)

        @pl.when(step == mid)
        def _():
            comm.middle(cins, couts, sems)

        @pl.when(step == total - 1)
        def _():
            comm.finish(cins, couts, sems)

    res = pl.pallas_call(
        wrapped, grid=grid, in_specs=list(in_specs) + [ANY] * nc, out_specs=list(out_specs) + [ANY] * nc,
        out_shape=list(out_shape) + list(comm.out_shape), scratch_shapes=list(scratch_shapes) + list(comm.scratch),
        name=name, input_output_aliases=aliases,
        compiler_params=_cp(("arbitrary",) * len(grid)))(*args, *comm.arrays)
    return res[:n_out], res[n_out:]


class _Hook:
    def __init__(self, fn, ins=(), in_specs=(), outs=()):
        self.fn, self.ins, self.in_specs, self.outs = fn, list(ins), list(in_specs), list(outs)


def _matmul(a, b, *, ta=False, tb=False, tm, tn, tk, out_dtype=None, bias=None, res=None, inner='n',
            comm=None, prologue=None, epilogue=None, a_shape=None, sequential=False, name):
    if a is None:
        m, kdim = a_shape
    elif ta:
        kdim, m = a.shape
    else:
        m, kdim = a.shape
    if tb:
        n, k2 = b.shape
    else:
        k2, n = b.shape
    assert kdim == k2, (name, kdim, b.shape)
    tm, tn, tk = min(tm, m), min(tn, n), min(tk, kdim)
    assert m % tm == 0 and n % tn == 0 and kdim % tk == 0, (name, m, n, kdim, tm, tn, tk)
    nk = kdim // tk
    dn = (((0 if ta else 1,), (1 if tb else 0,)), ((), ()))
    hooks = [h for h in (prologue, epilogue) if h is not None]
    n_pro_in = len(prologue.ins) if prologue else 0
    n_epi_in = len(epilogue.ins) if epilogue else 0
    n_pro_out = len(prologue.outs) if prologue else 0
    n_epi_out = len(epilogue.outs) if epilogue else 0
    if inner == 'n':
        grid = (m // tm, n // tn, nk)
        mi = lambda g0, g1: g0
        ni = lambda g0, g1: g1
    else:
        grid = (n // tn, m // tm, nk)
        mi = lambda g0, g1: g1
        ni = lambda g0, g1: g0

    def body(*refs):
        refs = list(refs)
        take = lambda cnt: [refs.pop(0) for _ in range(cnt)]
        a_ref = take(1)[0] if a is not None else None
        b_ref = take(1)[0]
        bias_ref = take(1)[0] if bias is not None else None
        res_ref = take(1)[0] if res is not None else None
        pro_in, epi_in = take(n_pro_in), take(n_epi_in)
        o_ref = take(1)[0] if epilogue is None else None
        pro_out, epi_out = take(n_pro_out), take(n_epi_out)
        i, j, k = mi(pl.program_id(0), pl.program_id(1)), ni(pl.program_id(0), pl.program_id(1)), pl.program_id(2)

        def finish(src):
            def result(rows=slice(None)):
                r = src[rows, :]
                if bias_ref is not None:
                    r = r + bias_ref[...]
                if res_ref is not None:
                    r = r + res_ref[rows, :]
                return r

            if epilogue is None:
                o_ref[...] = result().astype(out_dtype)
            else:
                epilogue.fn(result, epi_in, epi_out, i, j)

        a_val = a_ref[...] if prologue is None else prologue.fn(a_ref, pro_in, pro_out, i, k)
        prod = lax.dot_general(a_val.astype(BF16), b_ref[...].astype(BF16), dn, preferred_element_type=F32)
        if nk == 1:
            finish(prod)
            return
        acc_ref = refs[0]

        @pl.when(k == 0)
        def _():
            acc_ref[...] = prod

        @pl.when(k > 0)
        def _():
            acc_ref[...] += prod

        @pl.when(k == nk - 1)
        def _():
            finish(acc_ref)

    spec = lambda shape, fn: pl.BlockSpec(shape, lambda g0, g1, k: fn(mi(g0, g1), ni(g0, g1), k))
    in_specs, args = [], []
    if a is not None:
        in_specs.append(spec((tk, tm), lambda i, j, k: (k, i)) if ta else spec((tm, tk), lambda i, j, k: (i, k)))
        args.append(a)
    in_specs.append(spec((tn, tk), lambda i, j, k: (j, k)) if tb else spec((tk, tn), lambda i, j, k: (k, j)))
    args.append(b)
    if bias is not None:
        in_specs.append(spec((1, tn), lambda i, j, k: (0, j)))
        args.append(bias)
    if res is not None:
        in_specs.append(spec((tm, tn), lambda i, j, k: (i, j)))
        args.append(res)
    for h in hooks:
        in_specs += [spec(shape, fn) for shape, fn in h.in_specs]
        args += h.ins
    out_specs, out_shape = [], []
    if epilogue is None:
        out_specs.append(spec((tm, tn), lambda i, j, k: (i, j)))
        out_shape.append(jax.ShapeDtypeStruct((m, n), out_dtype))
    for h in hooks:
        out_specs += [spec(blk, fn) for _, _, blk, fn in h.outs]
        out_shape += [jax.ShapeDtypeStruct(shape, dtype) for shape, dtype, _, _ in h.outs]
    outs, couts = _hosted_call(
        body, grid=grid, in_specs=in_specs, out_specs=out_specs, out_shape=out_shape,
        scratch_shapes=[pltpu.VMEM((tm, tn), F32)] if nk > 1 else [],
        sem=("arbitrary",) * 3 if sequential else ("parallel", "parallel", "arbitrary"),
        name=name, args=args, comm=comm)
    outs = outs[0] if not hooks else outs
    return outs if comm is None else (outs, couts)


def _rms_fwd(x, g, *, comm=None, name):
    l, d = x.shape
    tr = min(256, l)

    def body(x_ref, g_ref, h_ref):
        xf = x_ref[...]
        r = lax.rsqrt(jnp.mean(xf * xf, axis=-1, keepdims=True) + RMS_EPS)
        h_ref[...] = ((xf * r) * g_ref[...]).astype(BF16)

    row = pl.BlockSpec((tr, d), lambda i: (i, 0))
    return _hosted_call(
        body, grid=(l // tr,), in_specs=[row, pl.BlockSpec((1, d), lambda i: (0, 0))],
        out_specs=[row], out_shape=[jax.ShapeDtypeStruct((l, d), BF16)], scratch_shapes=[],
        sem=("parallel",), name=name, args=(x, g), comm=comm)


EPI_ROWS = 128
ROW_TILE = 256


def _row_chunks(tm):
    ch = min(EPI_ROWS, tm)
    return [slice(c * ch, (c + 1) * ch) for c in range(tm // ch)]


def _rowwise(hook, src, tm, *, name):
    l, d = src.shape
    n_in = len(hook.ins)

    def body(*refs):
        src_ref, ins, outs = refs[0], refs[1:1 + n_in], refs[1 + n_in:]
        hook.fn(lambda rows=slice(None): src_ref[rows, :], ins, outs, pl.program_id(0), 0)

    spec = lambda shape, fn: pl.BlockSpec(shape, lambda i: fn(i, 0, 0))
    return pl.pallas_call(
        body, grid=(l // tm,),
        in_specs=[pl.BlockSpec((tm, d), lambda i: (i, 0))] + [spec(shape, fn) for shape, fn in hook.in_specs],
        out_specs=[spec(blk, fn) for _, _, blk, fn in hook.outs],
        out_shape=[jax.ShapeDtypeStruct(shape, dtype) for shape, dtype, _, _ in hook.outs],
        name=name, compiler_params=_cp(("arbitrary",)))(src, *hook.ins)


def _rms_bwd_hook(x, g, dres, tm, out_dtype):
    def fn(result, ins, outs, i, j):
        x_ref, g_ref, dres_ref = ins
        dx_ref, dg_ref = outs

        @pl.when(i == 0)
        def _():
            dg_ref[...] = jnp.zeros_like(dg_ref)

        for rows in _row_chunks(tm):
            dyv = result(rows).astype(F32)
            xf = x_ref[rows, :]
            r = lax.rsqrt(jnp.mean(xf * xf, axis=-1, keepdims=True) + RMS_EPS)
            xhat = xf * r
            dxh = dyv * g_ref[...]
            dx = r * (dxh - xhat * jnp.mean(dxh * xhat, axis=-1, keepdims=True)) + dres_ref[rows, :].astype(F32)
            dx_ref[rows, :] = dx.astype(out_dtype)
            dg_ref[...] += jnp.sum(dyv * xhat, axis=0, keepdims=True)

    l, d = x.shape
    row = lambda i, j, k: (i, 0)
    vec = lambda i, j, k: (0, 0)
    return _Hook(fn, ins=[x, g, dres], in_specs=[((tm, d), row), ((1, d), vec), ((tm, d), row)],
                 outs=[((l, d), out_dtype, (tm, d), row), ((1, d), F32, (1, d), vec)])


def _final_loss_hook(g, target, tm):
    l, d = target.shape

    def fn(result, ins, outs, i, j):
        g_ref, t_ref = ins
        dxb_ref, dg_ref, loss_ref = outs
        gv = g_ref[...]

        @pl.when(i == 0)
        def _():
            dg_ref[...] = jnp.zeros_like(dg_ref)
            loss_ref[...] = jnp.zeros_like(loss_ref)

        for rows in _row_chunks(tm):
            xf = result(rows)
            r = lax.rsqrt(jnp.mean(xf * xf, axis=-1, keepdims=True) + RMS_EPS)
            xhat = xf * r
            diff = xhat * gv - t_ref[rows, :]
            dout = diff * (1.0 / d)
            dxh = dout * gv
            dx = r * (dxh - xhat * jnp.mean(dxh * xhat, axis=-1, keepdims=True))
            dxb_ref[rows, :] = dx.astype(BF16)
            dg_ref[...] += jnp.sum(dout * xhat, axis=0, keepdims=True)
            part = jnp.sum(jnp.mean(diff * diff, axis=-1, keepdims=True), axis=0, keepdims=True)
            loss_ref[...] += 0.5 * part

    row = lambda i, j, k: (i, 0)
    vec = lambda i, j, k: (0, 0)
    return _Hook(fn, ins=[g, target], in_specs=[((1, d), vec), ((tm, d), row)],
                 outs=[((l, d), BF16, (tm, d), row), ((1, d), F32, (1, d), vec), ((1, 1), F32, (1, 1), vec)])


Q_PER_KV = 8
GROUP_ROWS = Q_PER_KV * BLOCK


def _attn_masks(n, rows=GROUP_ROWS):
    q_idx = lax.broadcasted_iota(jnp.int32, (rows, 2 * BLOCK), 0) & (BLOCK - 1)
    s_idx = lax.broadcasted_iota(jnp.int32, (rows, 2 * BLOCK), 1)
    dist = q_idx + BLOCK - s_idx
    valid = (dist >= 0) & (dist < BLOCK) & ((n > 0) | (s_idx >= BLOCK))
    return dist.astype(F32), valid


def _dup_half(t, kv_head, lo):
    rolled = pltpu.roll(t, HEAD_DIM, axis=1)
    return jnp.where(lo, t, rolled) if kv_head == 0 else jnp.where(lo, rolled, t)


def _stack_heads(ref, kv_head, lo):
    pieces = []
    for r in range(Q_PER_KV):
        pair = kv_head * 4 + r // 2
        t = ref[:, pair * 128:(pair + 1) * 128].astype(BF16)
        sel = lo if r % 2 == 0 else jnp.logical_not(lo)
        pieces.append(jnp.where(sel, t, jnp.zeros_like(t)))
    return jnp.concatenate(pieces, axis=0)


def _unstack_heads(t, lo):
    return [jnp.where(lo, t[(2 * i) * BLOCK:(2 * i + 1) * BLOCK], t[(2 * i + 1) * BLOCK:(2 * i + 2) * BLOCK])
            for i in range(Q_PER_KV // 2)]


def _per_head_column(values):
    return jnp.concatenate([jnp.full((BLOCK, 1), v, F32) for v in values], axis=0)


def _group_probs(qm, kdup, dist, valid, sink_ref, kv_head):
    heads = [kv_head * Q_PER_KV + r for r in range(Q_PER_KV)]
    slope = _per_head_column([2.0 ** (-8.0 * (h + 1) / N_Q_HEADS) for h in heads])
    sink = _per_head_column([sink_ref[h] for h in heads])
    return _probs(qm, kdup, dist, valid, sink, slope)


def _probs(qm, kdup, dist, valid, sink, slope):
    s = lax.dot_general(qm, kdup, (((1,), (1,)), ((), ())), preferred_element_type=F32)
    s = s * (HEAD_DIM ** -0.5) - slope * dist
    s = jnp.where(valid, s, NEG_BIG)
    m = jnp.maximum(jnp.max(s, axis=-1, keepdims=True), sink)
    p = jnp.exp(s - m)
    esink = jnp.exp(sink - m)
    inv = 1.0 / (jnp.sum(p, axis=-1, keepdims=True) + esink)
    return p * inv, esink * inv


def _attn_fwd(proj, sinks, *, name):
    l = proj.shape[0]
    nb = l // BLOCK

    def body(sink_ref, q_ref, kc_ref, kp_ref, vc_ref, vp_ref, o_ref):
        n = pl.program_id(0)
        dist, valid = _attn_masks(n, BLOCK)
        lo = lax.broadcasted_iota(jnp.int32, (1, BLOCK), 1) < HEAD_DIM
        kx = jnp.concatenate([kp_ref[...], kc_ref[...]], axis=0).astype(BF16)
        vx = jnp.concatenate([vp_ref[...], vc_ref[...]], axis=0).astype(BF16)
        for kv_head in range(2):
            kdup = _dup_half(kx, kv_head, lo)
            vdup = _dup_half(vx, kv_head, lo)
            for pr in range(4):
                pair = kv_head * 4 + pr
                qp = q_ref[:, pair * 128:(pair + 1) * 128].astype(BF16)
                o_pair = jnp.zeros((BLOCK, 128), F32)
                for half in range(2):
                    head = 2 * pair + half
                    sel = lo if half == 0 else jnp.logical_not(lo)
                    qm = jnp.where(sel, qp, jnp.zeros_like(qp))
                    p, _ = _probs(qm, kdup, dist, valid, sink_ref[head], 2.0 ** (-8.0 * (head + 1) / N_Q_HEADS))
                    o = jnp.dot(p.astype(BF16), vdup, preferred_element_type=F32)
                    o_pair = o_pair + jnp.where(sel, o, 0.0)
                o_ref[:, pair * 128:(pair + 1) * 128] = o_pair.astype(BF16)

    kv = lambda col, prev: pl.BlockSpec(
        (BLOCK, KV_WIDTH), (lambda n: (jnp.maximum(n - 1, 0), col)) if prev else (lambda n: (n, col)))
    return pl.pallas_call(
        body, grid=(nb,),
        in_specs=[pl.BlockSpec(memory_space=pltpu.SMEM),
                  pl.BlockSpec((BLOCK, ATTN_WIDTH), lambda n: (n, 0)),
                  kv(COL_K, False), kv(COL_K, True), kv(COL_V, False), kv(COL_V, True)],
        out_specs=pl.BlockSpec((BLOCK, ATTN_WIDTH), lambda n: (n, 0)),
        out_shape=jax.ShapeDtypeStruct((l, ATTN_WIDTH), BF16), name=name,
        compiler_params=_cp(("parallel",)))(sinks, proj, proj, proj, proj, proj)


def _attn_bwd(proj, sinks, dattn, *, comm=None, name):
    l = proj.shape[0]
    nb = l // BLOCK

    def body(sink_ref, q_ref, kc_ref, kp_ref, vc_ref, vp_ref, do_ref,
             dq_ref, dkc_ref, dkp_ref, dvc_ref, dvp_ref, dsink_ref):
        n = pl.program_id(0)
        dist, valid = _attn_masks(n)
        lane = lax.broadcasted_iota(jnp.int32, (1, BLOCK), 1)
        lo = lane < HEAD_DIM
        kx = jnp.concatenate([kp_ref[...], kc_ref[...]], axis=0).astype(BF16)
        vx = jnp.concatenate([vp_ref[...], vc_ref[...]], axis=0).astype(BF16)
        dsink = jnp.zeros((1, BLOCK), F32)
        dk_heads, dv_heads = [], []
        for kv_head in range(2):
            kdup = _dup_half(kx, kv_head, lo)
            vdup = _dup_half(vx, kv_head, lo)
            qm = _stack_heads(q_ref, kv_head, lo)
            dom = _stack_heads(do_ref, kv_head, lo)
            p, psink = _group_probs(qm, kdup, dist, valid, sink_ref, kv_head)
            dp = lax.dot_general(dom, vdup, (((1,), (1,)), ((), ())), preferred_element_type=F32)
            delta = jnp.sum(p * dp, axis=-1, keepdims=True)
            ds = (p * (dp - delta) * (HEAD_DIM ** -0.5)).astype(BF16)
            dsink_rows = -psink * delta
            for r in range(Q_PER_KV):
                part = jnp.sum(dsink_rows[r * BLOCK:(r + 1) * BLOCK])
                dsink = dsink + jnp.where(lane == kv_head * Q_PER_KV + r, part, 0.0)
            dq = jnp.dot(ds, kdup, preferred_element_type=F32)
            for i, dq_pair in enumerate(_unstack_heads(dq, lo)):
                pair = kv_head * 4 + i
                dq_ref[:, pair * 128:(pair + 1) * 128] = dq_pair.astype(BF16)
            dk_acc = lax.dot_general(ds, qm, (((0,), (0,)), ((), ())), preferred_element_type=F32)
            dv_acc = lax.dot_general(p.astype(BF16), dom, (((0,), (0,)), ((), ())), preferred_element_type=F32)
            dk_heads.append(dk_acc + pltpu.roll(dk_acc, HEAD_DIM, axis=1))
            dv_heads.append(dv_acc + pltpu.roll(dv_acc, HEAD_DIM, axis=1))
        dk = jnp.where(lo, dk_heads[0], dk_heads[1])
        dv = jnp.where(lo, dv_heads[0], dv_heads[1])
        dkp_ref[...] = dk[:BLOCK]
        dkc_ref[...] = dk[BLOCK:]
        dvp_ref[...] = dv[:BLOCK]
        dvc_ref[...] = dv[BLOCK:]

        @pl.when(n == 0)
        def _():
            dsink_ref[...] = jnp.zeros_like(dsink_ref)

        dsink_ref[...] += dsink

    kv = lambda col, prev: pl.BlockSpec(
        (BLOCK, KV_WIDTH), (lambda n: (jnp.maximum(n - 1, 0), col)) if prev else (lambda n: (n, col)))
    qspec = pl.BlockSpec((BLOCK, ATTN_WIDTH), lambda n: (n, 0))
    kvout = pl.BlockSpec((BLOCK, KV_WIDTH), lambda n: (n, 0))
    kvshape = jax.ShapeDtypeStruct((l, KV_WIDTH), F32)
    return _hosted_call(
        body, grid=(nb,),
        in_specs=[pl.BlockSpec(memory_space=pltpu.SMEM), qspec,
                  kv(COL_K, False), kv(COL_K, True), kv(COL_V, False), kv(COL_V, True), qspec],
        out_specs=[qspec, kvout, kvout, kvout, kvout, pl.BlockSpec((1, BLOCK), lambda n: (0, 0))],
        out_shape=[jax.ShapeDtypeStruct((l, ATTN_WIDTH), BF16), kvshape, kvshape, kvshape, kvshape,
                   jax.ShapeDtypeStruct((1, BLOCK), F32)],
        scratch_shapes=[], sem=("arbitrary",), name=name,
        args=(sinks, proj, proj, proj, proj, proj, dattn), comm=comm)


def _discretize(a_re, a_im, log_dt, b_re, b_im):
    dt = jnp.exp(log_dt)
    mag = jnp.exp(a_re * dt)
    ab_re = mag * jnp.cos(a_im * dt)
    ab_im = mag * jnp.sin(a_im * dt)
    nr = ab_re - 1.0
    ni = ab_im
    den = a_re * a_re + a_im * a_im
    z_re = (nr * a_re + ni * a_im) / den
    z_im = (ni * a_re - nr * a_im) / den
    bb_re = z_re * b_re - z_im * b_im
    bb_im = z_re * b_im + z_im * b_re
    return ab_re, ab_im, bb_re, bb_im


def _ssm_disc_fwd(a_re, a_im, log_dt, b_re, b_im, *, name):
    def body(ar, ai, ld, br, bi, o_ar, o_ai, o_br, o_bi):
        r = _discretize(ar[...], ai[...], ld[...], br[...], bi[...])
        o_ar[...], o_ai[...], o_br[...], o_bi[...] = r

    col = jax.ShapeDtypeStruct(a_re.shape, F32)
    mat = jax.ShapeDtypeStruct(b_re.shape, F32)
    return pl.pallas_call(body, out_shape=[col, col, mat, mat], name=name)(a_re, a_im, log_dt, b_re, b_im)


def _ssm_disc_bwd(a_re, a_im, log_dt, b_re, b_im, d_ab_re, d_ab_im, d_bb_re, d_bb_im, *, name):
    def body(ar, ai, ld, br, bi, g0, g1, g2, g3, o_ar, o_ai, o_ld, o_br, o_bi):
        _, vjp = jax.vjp(_discretize, ar[...], ai[...], ld[...], br[...], bi[...])
        r = vjp((g0[...], g1[...], g2[...], g3[...]))
        o_ar[...], o_ai[...], o_ld[...], o_br[...], o_bi[...] = r

    col = jax.ShapeDtypeStruct(a_re.shape, F32)
    mat = jax.ShapeDtypeStruct(b_re.shape, F32)
    return pl.pallas_call(body, out_shape=[col, col, col, mat, mat], name=name)(
        a_re, a_im, log_dt, b_re, b_im, d_ab_re, d_ab_im, d_bb_re, d_bb_im)


def _shift_rows(x, d, rows, *, down):
    t = x.shape[0]
    if down:
        return jnp.where(rows >= d, pltpu.roll(x, d, axis=0), 0.0)
    return jnp.where(rows < t - d, pltpu.roll(x, t - d, axis=0), 0.0)


def _scan_chunk(xr, xi, ar, ai, *, down):
    t = xr.shape[0]
    rows = lax.broadcasted_iota(jnp.int32, (t, 1), 0)
    pr, pi = ar, ai
    d = 1
    while d < t:
        sr = _shift_rows(xr, d, rows, down=down)
        si = _shift_rows(xi, d, rows, down=down)
        xr, xi = xr + pr * sr - pi * si, xi + pr * si + pi * sr
        pr, pi = pr * pr - pi * pi, 2.0 * pr * pi
        d *= 2
    return xr, xi


def _ssm_fwd(proj, ab, bd, cd, dskip, *, comm=None, name):
    l = proj.shape[0]
    t = min(SSM_CHUNK, l)
    nc = l // t

    def body(u_ref, ab_ref, bd_ref, cd_ref, ds_ref, y_ref, gy_ref, xs_ref, carry_ref):
        c = pl.program_id(1)

        @pl.when(c == 0)
        def _():
            carry_ref[...] = jnp.zeros_like(carry_ref)

        u = u_ref[...]
        ar, ai = ab_ref[0, 0:1, :], ab_ref[0, 1:2, :]
        bu = jnp.dot(u.astype(BF16), bd_ref[0], preferred_element_type=F32)
        rows = lax.broadcasted_iota(jnp.int32, (t, 1), 0)
        cr, ci = carry_ref[0:1, :], carry_ref[1:2, :]
        xr = bu[:, :SSM_X_BLK] + jnp.where(rows == 0, ar * cr - ai * ci, 0.0)
        xi = bu[:, SSM_X_BLK:] + jnp.where(rows == 0, ar * ci + ai * cr, 0.0)
        xr, xi = _scan_chunk(xr, xi, ar, ai, down=True)
        xs_ref[0, :, :SSM_X_BLK] = xr
        xs_ref[0, :, SSM_X_BLK:] = xi
        carry_ref[0:1, :] = xs_ref[0, t - 1:t, :SSM_X_BLK]
        carry_ref[1:2, :] = xs_ref[0, t - 1:t, SSM_X_BLK:]
        y = jnp.dot(xs_ref[0].astype(BF16), cd_ref[0], preferred_element_type=F32) + ds_ref[...] * u
        y_ref[...] = y
        gy_ref[...] = _gelu(y).astype(BF16)

    blk = lambda shape: pl.BlockSpec((1,) + shape, lambda j, c: (j, 0, 0))
    ycol = pl.BlockSpec((t, SSM_U_BLK), lambda j, c: (c, j))
    return _hosted_call(
        body, grid=(SSM_SPLIT, nc),
        in_specs=[pl.BlockSpec((t, SSM_U_BLK), lambda j, c: (c, COL_U + j)),
                  blk((2, SSM_X_BLK)), blk((SSM_U_BLK, 2 * SSM_X_BLK)), blk((2 * SSM_X_BLK, SSM_U_BLK)),
                  pl.BlockSpec((1, SSM_U_BLK), lambda j, c: (0, j))],
        out_specs=[ycol, ycol, pl.BlockSpec((1, t, 2 * SSM_X_BLK), lambda j, c: (j, c, 0))],
        out_shape=[jax.ShapeDtypeStruct((l, SSM_WIDTH), F32), jax.ShapeDtypeStruct((l, SSM_WIDTH), BF16),
                   jax.ShapeDtypeStruct((SSM_SPLIT, l, 2 * SSM_X_BLK), F32)],
        scratch_shapes=[pltpu.VMEM((2, SSM_X_BLK), F32)], sem=("parallel", "arbitrary"), name=name,
        args=(proj, ab, bd, cd, dskip), comm=comm)


def _ssm_bwd(proj, y, dgy, xs, ab, bdt, cdt, dskip, *, comm=None, name):
    l = proj.shape[0]
    t = min(SSM_CHUNK, l)
    nc = l // t

    def body(u_ref, y_ref, dgy_ref, xs_ref, halo_ref, ab_ref, bdt_ref, cdt_ref, ds_ref,
             du_ref, dbd_ref, dcd_ref, dab_ref, dd_ref, carry_ref):
        c = pl.program_id(1)
        ci_ = nc - 1 - c

        @pl.when(c == 0)
        def _():
            carry_ref[...] = jnp.zeros_like(carry_ref)
            dbd_ref[...] = jnp.zeros_like(dbd_ref)
            dcd_ref[...] = jnp.zeros_like(dcd_ref)
            dab_ref[...] = jnp.zeros_like(dab_ref)
            dd_ref[...] = jnp.zeros_like(dd_ref)

        u = u_ref[...]
        dy = dgy_ref[...] * _gelu_grad(y_ref[...])
        dyb = dy.astype(BF16)
        ar, ai = ab_ref[0, 0:1, :], ab_ref[0, 1:2, :]
        g = jnp.dot(dyb, cdt_ref[0], preferred_element_type=F32)
        rows = lax.broadcasted_iota(jnp.int32, (t, 1), 0)
        cr, ci = carry_ref[0:1, :], carry_ref[1:2, :]
        lr = g[:, :SSM_X_BLK] + jnp.where(rows == t - 1, ar * cr + ai * ci, 0.0)
        li = g[:, SSM_X_BLK:] + jnp.where(rows == t - 1, ar * ci - ai * cr, 0.0)
        lr, li = _scan_chunk(lr, li, ar, -ai, down=False)
        lam = jnp.concatenate([lr, li], axis=1)
        carry_ref[0:1, :] = lr[0:1, :]
        carry_ref[1:2, :] = li[0:1, :]
        lamb = lam.astype(BF16)
        du_ref[...] = (jnp.dot(lamb, bdt_ref[0], preferred_element_type=F32) + ds_ref[...] * dy).astype(BF16)
        dbd_ref[0] += lax.dot_general(u.astype(BF16), lamb, (((0,), (0,)), ((), ())),
                                      preferred_element_type=F32)
        xs = xs_ref[0]
        dcd_ref[0] += lax.dot_general(xs.astype(BF16), dyb, (((0,), (0,)), ((), ())),
                                      preferred_element_type=F32)
        halo = jnp.where(ci_ > 0, halo_ref[0, 7:8, :], 0.0)
        xprev = jnp.where(rows == 0, halo, pltpu.roll(xs, 1, axis=0))
        xpr, xpi = xprev[:, :SSM_X_BLK], xprev[:, SSM_X_BLK:]
        dab_ref[0, 0:1, :] += jnp.sum(lr * xpr + li * xpi, axis=0, keepdims=True)
        dab_ref[0, 1:2, :] += jnp.sum(li * xpr - lr * xpi, axis=0, keepdims=True)
        dd_ref[...] += jnp.sum(dy * u, axis=0, keepdims=True)

    blk = lambda shape: pl.BlockSpec((1,) + shape, lambda j, c: (j, 0, 0))
    rev = lambda j, c: (nc - 1 - c, j)
    ycol = pl.BlockSpec((t, SSM_U_BLK), rev)
    hb = t // 8
    return _hosted_call(
        body, grid=(SSM_SPLIT, nc), comm=comm, sem=("parallel", "arbitrary"), name=name,
        args=(proj, y, dgy, xs, xs, ab, bdt, cdt, dskip), scratch_shapes=[pltpu.VMEM((2, SSM_X_BLK), F32)],
        in_specs=[pl.BlockSpec((t, SSM_U_BLK), lambda j, c: (nc - 1 - c, COL_U + j)), ycol, ycol,
                  pl.BlockSpec((1, t, 2 * SSM_X_BLK), lambda j, c: (j, nc - 1 - c, 0)),
                  pl.BlockSpec((1, 8, 2 * SSM_X_BLK),
                               lambda j, c: (j, jnp.maximum((nc - 1 - c) * hb - 1, 0), 0)),
                  blk((2, SSM_X_BLK)), blk((2 * SSM_X_BLK, SSM_U_BLK)), blk((SSM_U_BLK, 2 * SSM_X_BLK)),
                  pl.BlockSpec((1, SSM_U_BLK), lambda j, c: (0, j))],
        out_specs=[ycol, blk((SSM_U_BLK, 2 * SSM_X_BLK)), blk((2 * SSM_X_BLK, SSM_U_BLK)),
                   blk((2, SSM_X_BLK)), pl.BlockSpec((1, SSM_U_BLK), lambda j, c: (0, j))],
        out_shape=[jax.ShapeDtypeStruct((l, SSM_WIDTH), BF16),
                   jax.ShapeDtypeStruct((SSM_SPLIT, SSM_U_BLK, 2 * SSM_X_BLK), F32),
                   jax.ShapeDtypeStruct((SSM_SPLIT, 2 * SSM_X_BLK, SSM_U_BLK), F32),
                   jax.ShapeDtypeStruct((SSM_SPLIT, 2, SSM_X_BLK), F32),
                   jax.ShapeDtypeStruct((1, SSM_WIDTH), F32)])


def _block_diag(t):
    s, g, a, b = t.shape
    return jnp.einsum('sgab,gk->sgakb', t, jnp.eye(g, dtype=t.dtype)).reshape(s, g * a, g * b)


def _block_diag_take(t, a, b):
    s = t.shape[0]
    return jnp.einsum('sgakb,gk->sgab', t.reshape(s, 8, a, 8, b), jnp.eye(8, dtype=t.dtype))


def _glu_fwd_hook(l, tm):
    def fn(result, ins, outs, i, j):
        z = result()
        outs[0][...] = z
        outs[1][...] = (z[:, :SSM_WIDTH] * _sigmoid(z[:, SSM_WIDTH:])).astype(BF16)

    row = lambda i, j, k: (i, 0)
    return _Hook(fn, outs=[((l, 2 * SSM_WIDTH), F32, (tm, 2 * SSM_WIDTH), row),
                           ((l, SSM_WIDTH), BF16, (tm, SSM_WIDTH), row)])


def _glu_bwd_hook(z, tm):
    l = z.shape[0]

    def fn(result, ins, outs, i, j):
        zv_ref, zg_ref = ins
        dz_ref, db_ref = outs
        d = result()
        sg = _sigmoid(zg_ref[...])
        dv = d * sg
        dg = d * zv_ref[...] * sg * (1.0 - sg)
        dz_ref[:, :SSM_WIDTH] = dv.astype(BF16)
        dz_ref[:, SSM_WIDTH:] = dg.astype(BF16)

        @pl.when(i == 0)
        def _():
            db_ref[...] = jnp.zeros_like(db_ref)

        db_ref[:, :SSM_WIDTH] += jnp.sum(dv, axis=0, keepdims=True)
        db_ref[:, SSM_WIDTH:] += jnp.sum(dg, axis=0, keepdims=True)

    half = (tm, SSM_WIDTH)
    return _Hook(fn, ins=[z, z], in_specs=[(half, lambda i, j, k: (i, 0)), (half, lambda i, j, k: (i, 1))],
                 outs=[((l, 2 * SSM_WIDTH), BF16, (tm, 2 * SSM_WIDTH), lambda i, j, k: (i, 0)),
                       ((1, 2 * SSM_WIDTH), F32, (1, 2 * SSM_WIDTH), lambda i, j, k: (0, 0))])


GATE_TC = 256


def _merge_fwd(proj, a, s, *, name):
    l = a.shape[0]
    tr = min(2048, l)

    def body(ga_ref, gs_ref, a_ref, s_ref, o_ref):
        o_ref[...] = (_sigmoid(ga_ref[...]) * a_ref[...].astype(F32)
                      + _sigmoid(gs_ref[...]) * s_ref[...].astype(F32)).astype(BF16)

    own = pl.BlockSpec((tr, GATE_TC), lambda i, j: (i, j))
    return pl.pallas_call(
        body, grid=(l // tr, D_MODEL // GATE_TC),
        in_specs=[pl.BlockSpec((tr, GATE_TC), lambda i, j: (i, COL_GA // 2 + j)),
                  pl.BlockSpec((tr, GATE_TC), lambda i, j: (i, COL_GS // 2 + j)), own, own],
        out_specs=own, out_shape=jax.ShapeDtypeStruct((l, D_MODEL), BF16), name=name,
        compiler_params=_cp(("parallel", "parallel")))(proj, proj, a, s)


def _merge_bwd_hook(proj, a, s, tm):
    def fn(result, ins, outs, i, j):
        ga_ref, gs_ref, a_br, s_br = ins
        d = result()
        sa = _sigmoid(ga_ref[...])
        ss = _sigmoid(gs_ref[...])
        outs[0][...] = (d * sa).astype(BF16)
        outs[1][...] = (d * ss).astype(BF16)
        outs[2][...] = (d * a_br[...].astype(F32) * sa * (1.0 - sa)).astype(BF16)
        outs[3][...] = (d * s_br[...].astype(F32) * ss * (1.0 - ss)).astype(BF16)

    blk = (tm, GATE_TC)
    own = lambda i, j, k: (i, j)
    return _Hook(fn, ins=[proj, proj, a, s],
                 in_specs=[(blk, lambda i, j, k: (i, COL_GA // 2 + j)), (blk, lambda i, j, k: (i, COL_GS // 2 + j)),
                           (blk, own), (blk, own)],
                 outs=[(a.shape, BF16, blk, own)] * 4)


FF_TC = D_FF // 2
FF_NJ = 2
FF_ROWS = 128


FF_HALO = 16


def _conv_taps(ext, rows):
    h = FF_HALO
    return (ext[h:h + rows], pltpu.roll(ext, 1, axis=0)[h:h + rows], pltpu.roll(ext, 2, axis=0)[h:h + rows])


def _ff_specs(tr, l):
    hb = tr // FF_HALO
    last = l // FF_HALO - 1
    prev = lambda i: jnp.maximum(i * hb - 1, 0)
    nxt = lambda i: jnp.minimum((i + 1) * hb, last)
    return dict(
        own=pl.BlockSpec((tr, FF_TC), lambda j, i: (i, j)),
        own_next=pl.BlockSpec((FF_HALO, FF_TC), lambda j, i: (nxt(i), j)),
        val=pl.BlockSpec((tr, FF_TC), lambda j, i: (i, 2 * j)),
        val_next=pl.BlockSpec((FF_HALO, FF_TC), lambda j, i: (nxt(i), 2 * j)),
        gate=pl.BlockSpec((tr, FF_TC), lambda j, i: (i, 2 * j + 1)),
        gate_prev=pl.BlockSpec((FF_HALO, FF_TC), lambda j, i: (prev(i), 2 * j + 1)),
        gate_next=pl.BlockSpec((FF_HALO, FF_TC), lambda j, i: (nxt(i), 2 * j + 1)),
        pair=pl.BlockSpec((tr, 2 * FF_TC), lambda j, i: (i, j)),
        w=pl.BlockSpec((3, FF_TC), lambda j, i: (0, j)),
        b=pl.BlockSpec((1, FF_TC), lambda j, i: (0, j)))


def _ffn_act_fwd(up, conv_w, conv_b, *, name):
    l = up.shape[0]
    tr = min(FF_ROWS, l)

    def body(v_ref, g_ref, prev_ref, w_ref, b_ref, o_ref):
        prev = jnp.where(pl.program_id(1) == 0, 0.0, prev_ref[...].astype(F32))
        g0, g1, g2 = _conv_taps(jnp.concatenate([prev, g_ref[...].astype(F32)], axis=0), tr)
        gc = b_ref[...] + w_ref[0:1, :] * g2 + w_ref[1:2, :] * g1 + w_ref[2:3, :] * g0
        o_ref[...] = (v_ref[...].astype(F32) * _gelu(gc)).astype(BF16)

    sp = _ff_specs(tr, l)
    return pl.pallas_call(
        body, grid=(FF_NJ, l // tr), in_specs=[sp['val'], sp['gate'], sp['gate_prev'], sp['w'], sp['b']],
        out_specs=sp['own'], out_shape=jax.ShapeDtypeStruct((l, D_FF), BF16), name=name,
        compiler_params=_cp(("parallel", "parallel")))(up, up, up, conv_w, conv_b)


def _ffn_act_bwd(dact, up, conv_w, conv_b, *, comm=None, name):
    l = up.shape[0]
    tr = min(FF_ROWS, l)
    ni = l // tr
    te = tr + 8

    def body(d_ref, dn_ref, v_ref, vn_ref, g_ref, gp_ref, gn_ref, w_ref, b_ref, dup_ref, dw_ref, db_ref):
        i = pl.program_id(1)
        f32 = lambda ref, rows=None: ref[...].astype(F32)[:rows]
        prev = jnp.where(i == 0, 0.0, f32(gp_ref))
        g0, g1, g2 = _conv_taps(jnp.concatenate([prev, f32(g_ref), f32(gn_ref, 8)], axis=0), te)
        w0, w1, w2 = w_ref[0:1, :], w_ref[1:2, :], w_ref[2:3, :]
        gc = b_ref[...] + w0 * g2 + w1 * g1 + w2 * g0
        d_own = f32(d_ref)
        d = jnp.concatenate([d_own, jnp.where(i == ni - 1, 0.0, f32(dn_ref, 8))], axis=0)
        v = jnp.concatenate([f32(v_ref), f32(vn_ref, 8)], axis=0)
        dgc = d * v * _gelu_grad(gc)
        ahead1 = pltpu.roll(dgc, te - 1, axis=0)[:tr]
        ahead2 = pltpu.roll(dgc, te - 2, axis=0)[:tr]
        own = dgc[:tr]
        dup_ref[:, :FF_TC] = (d_own * _gelu(gc[:tr])).astype(BF16)
        dup_ref[:, FF_TC:] = (w2 * own + w1 * ahead1 + w0 * ahead2).astype(BF16)

        @pl.when(i == 0)
        def _():
            dw_ref[...] = jnp.zeros_like(dw_ref)
            db_ref[...] = jnp.zeros_like(db_ref)

        dw_ref[0:1, :] += jnp.sum(own * g2[:tr], axis=0, keepdims=True)
        dw_ref[1:2, :] += jnp.sum(own * g1[:tr], axis=0, keepdims=True)
        dw_ref[2:3, :] += jnp.sum(own * g0[:tr], axis=0, keepdims=True)
        db_ref[...] += jnp.sum(own, axis=0, keepdims=True)

    sp = _ff_specs(tr, l)
    return _hosted_call(
        body, grid=(FF_NJ, ni),
        in_specs=[sp['own'], sp['own_next'], sp['val'], sp['val_next'], sp['gate'], sp['gate_prev'],
                  sp['gate_next'], sp['w'], sp['b']],
        out_specs=[sp['pair'], sp['w'], sp['b']],
        out_shape=[jax.ShapeDtypeStruct((l, 2 * D_FF), BF16), jax.ShapeDtypeStruct((3, D_FF), F32),
                   jax.ShapeDtypeStruct((1, D_FF), F32)],
        scratch_shapes=[], sem=("parallel", "arbitrary"), name=name,
        args=(dact, dact, up, up, up, up, up, conv_w, conv_b), comm=comm)


def _join_dproj(dq, dkc, dkp, dvc, dvp, du, dga, dgs, *, name):
    l = dq.shape[0]
    nb = l // BLOCK
    widths = [ATTN_WIDTH, KV_WIDTH, KV_WIDTH, SSM_WIDTH, D_MODEL, D_MODEL]

    def body(dq_ref, dkc_ref, dkp_ref, dvc_ref, dvp_ref, du_ref, dga_ref, dgs_ref, o_ref, s_ref):
        n = pl.program_id(0)

        @pl.when(n == 0)
        def _():
            s_ref[...] = jnp.zeros_like(s_ref)

        last = n == nb - 1
        dk = (dkc_ref[...] + jnp.where(last, 0.0, dkp_ref[...])).astype(BF16)
        dv = (dvc_ref[...] + jnp.where(last, 0.0, dvp_ref[...])).astype(BF16)
        col = 0
        for v, width in zip([dq_ref[...], dk, dv, du_ref[...], dga_ref[...], dgs_ref[...]], widths):
            o_ref[:, col:col + width] = v
            s_ref[:, col:col + width] += jnp.sum(v.astype(F32), axis=0, keepdims=True)
            col += width

    cur = lambda width: pl.BlockSpec((BLOCK, width), lambda n: (n, 0))
    nxt = pl.BlockSpec((BLOCK, KV_WIDTH), lambda n: (jnp.minimum(n + 1, nb - 1), 0))
    return pl.pallas_call(
        body, grid=(nb,),
        in_specs=[cur(ATTN_WIDTH), cur(KV_WIDTH), nxt, cur(KV_WIDTH), nxt, cur(SSM_WIDTH), cur(D_MODEL), cur(D_MODEL)],
        out_specs=[pl.BlockSpec((BLOCK, IN_COLS), lambda n: (n, 0)), pl.BlockSpec((1, IN_COLS), lambda n: (0, 0))],
        out_shape=[jax.ShapeDtypeStruct((l, IN_COLS), BF16), jax.ShapeDtypeStruct((1, IN_COLS), F32)],
        name=name, compiler_params=_cp(("arbitrary",)))(dq, dkc, dkp, dvc, dvp, du, dga, dgs)


def _local_step(x, target, wts, small, shards=None):
    l = x.shape[0]
    wts = dict(wts)
    grads, recvs, sgr = {}, {}, {}
    lay = lambda keys: [LAYOUT[k] for k in keys]
    none = lambda keys: None
    gather = (lambda keys: _GatherPlan([shards[k] for k in keys], lay(keys))) if shards is not None else none
    scatter = (lambda keys: _ScatterPlan([grads[k] for k in keys], lay(keys))) if shards is not None else none

    mm = _matmul

    def take(res, plan, keys, store):
        outs, couts = res
        if plan is not None:
            store.update(zip(keys, couts))
        return outs

    def mm_plan(plan, keys, store, *args, **kw):
        if plan is None:
            return _matmul(*args, **kw)
        return take(_matmul(*args, comm=plan, **kw), plan, keys, store)

    def mm_host(keys, make_plan, store, *args, **kw):
        return mm_plan(make_plan(keys), keys, store, *args, **kw)

    up_scatter = lambda p: _ScatterPlan([grads['w_up']], lay(['w_up']), part=(p, 2)) if shards is not None else None

    col = lambda t: t.reshape(SSM_GROUPS * SSM_STATE, 1)
    a_re, a_im = col(small['ssm_a_re']), col(small['ssm_a_im'])
    log_dt = jnp.repeat(small['ssm_log_dt'].reshape(SSM_GROUPS), SSM_STATE).reshape(-1, 1)
    b_re = small['ssm_b_re'].reshape(SSM_GROUPS * SSM_STATE, SSM_GROUP)
    b_im = small['ssm_b_im'].reshape(SSM_GROUPS * SSM_STATE, SSM_GROUP)
    ab_re, ab_im, bb_re, bb_im = _ssm_disc_fwd(a_re, a_im, log_dt, b_re, b_im, name="ssm_disc_fwd")
    ab = jnp.stack([ab_re.reshape(SSM_SPLIT, SSM_X_BLK), ab_im.reshape(SSM_SPLIT, SSM_X_BLK)], axis=1)
    to_bd = lambda t: _block_diag(t.reshape(SSM_SPLIT, 8, SSM_STATE, SSM_GROUP).transpose(0, 1, 3, 2))
    bd = jnp.concatenate([to_bd(bb_re), to_bd(bb_im)], axis=2)
    c_re = small['ssm_c_re'].reshape(SSM_SPLIT, 8, SSM_GROUP, SSM_STATE)
    c_im = small['ssm_c_im'].reshape(SSM_SPLIT, 8, SSM_GROUP, SSM_STATE)
    cdt = jnp.concatenate([_block_diag(c_re), -_block_diag(c_im)], axis=2)
    bd_b, cdt_b = bd.astype(BF16), cdt.astype(BF16)
    bdt_b, cd_b = bd_b.transpose(0, 2, 1), cdt_b.transpose(0, 2, 1)
    dskip = small['ssm_d'].reshape(1, SSM_WIDTH)

    sinks = small['attn_sinks'].reshape(N_Q_HEADS)
    plan = gather(['w_in_t'])
    h1, = take(_rms_fwd(x, small['attn_norm_g'], comm=plan, name="rms1_fwd"), plan, ['w_in_t'], wts)
    proj = mm_host(['w_glu', 'w_ba', 'w_bs', 'w_out'], gather, wts,
                   h1, wts['w_in_t'], tb=True, tm=512, tn=2944, tk=2048, inner='m', out_dtype=F32,
                   bias=small['b_in'], name="mm_in")
    attn = _attn_fwd(proj, sinks, name="attn_fwd")
    plan = gather(['w_up'])
    y, gy, xs = take(_ssm_fwd(proj, ab, bd_b, cd_b, dskip, comm=plan, name="ssm_fwd"), plan, ['w_up'], wts)
    z, ssm = mm(gy, wts['w_glu'], tm=1024, tn=1024, tk=512, bias=small['b_glu'],
                epilogue=_glu_fwd_hook(l, min(1024, l)), name="mm_glu")
    a_br = mm(attn, wts['w_ba'], tm=1024, tn=1024, tk=1024, out_dtype=BF16, name="mm_ba")
    s_br = mm(ssm, wts['w_bs'], tm=1024, tn=1024, tk=512, out_dtype=BF16, name="mm_bs")
    tr = min(ROW_TILE, l)
    merged = _merge_fwd(proj, a_br, s_br, name="merge_fwd")
    x2 = mm(merged, wts['w_out'], tm=1024, tn=1024, tk=2048, inner='m', out_dtype=F32, res=x, name="mm_out")
    h2, = take(_rms_fwd(x2, small['ffn_norm_g'], name="rms2_fwd"), None, [], wts)
    up = mm_host(['w_down'], gather, wts,
                 h2, wts['w_up'], tm=1024, tn=1024, tk=2048, out_dtype=BF16, name="mm_up")
    conv_w, conv_b = small['conv_w'], small['conv_b']
    act = _ffn_act_fwd(up, conv_w, conv_b, name="ffn_act_fwd")
    x3 = mm(act, wts['w_down'], tm=1024, tn=1024, tk=2816, out_dtype=F32, res=x2, name="mm_down")
    dx3b, d_g3, loss = _rowwise(
        _final_loss_hook(small['final_norm_g'].reshape(1, D_MODEL), target, tr), x3, tr, name="final_loss")

    sgr['final_norm_g'] = d_g3.reshape(D_MODEL)
    dact = mm(dx3b, wts['w_down'], tb=True, tm=512, tn=2816, tk=2048, inner='m', out_dtype=BF16, name="mm_dact")
    grads['w_down'] = mm(act, dx3b, ta=True, tm=1408, tn=1024, tk=2048, out_dtype=BF16, name="mm_dw_down")
    plan = scatter(['w_down'])
    dup, sgr['conv_w'], sgr['conv_b'] = take(
        _ffn_act_bwd(dact, up, conv_w, conv_b, comm=plan, name="ffn_act_bwd"), plan, ['w_down'], recvs)
    grads['w_up'] = mm(h2, dup, ta=True, tm=1024, tn=1024, tk=2048, out_dtype=BF16, name="mm_dw_up")
    dh2 = mm_plan(up_scatter(0), ['w_up#0'], recvs,
                  dup, wts['w_up'], tb=True, tm=1024, tn=1024, tk=2816, out_dtype=BF16, name="mm_dh2")
    dx2b, sgr['ffn_norm_g'] = _rowwise(
        _rms_bwd_hook(x2, small['ffn_norm_g'], dx3b, tr, BF16), dh2, tr, name="rms2_bwd")

    d_a, d_s, dga, dgs = mm(dx2b, wts['w_out'], tb=True, tm=1024, tn=GATE_TC, tk=2048,
                            epilogue=_merge_bwd_hook(proj, a_br, s_br, min(1024, l)), name="mm_dmerged")
    grads['w_out'] = mm(merged, dx2b, ta=True, tm=1024, tn=1024, tk=2048, out_dtype=BF16, name="mm_dw_out")
    dattn = mm(d_a, wts['w_ba'], tb=True, tm=1024, tn=1024, tk=2048, inner='m', out_dtype=BF16, name="mm_dattn")
    grads['w_ba'] = mm(attn, d_a, ta=True, tm=1024, tn=1024, tk=2048, out_dtype=BF16, name="mm_dw_ba")
    dz, sgr['b_glu'] = mm(d_s, wts['w_bs'], tb=True, tm=1024, tn=512, tk=2048, sequential=True,
                          epilogue=_glu_bwd_hook(z, min(1024, l)), name="mm_dssm")
    grads['w_bs'] = mm(ssm, d_s, ta=True, tm=512, tn=1024, tk=2048, out_dtype=BF16, name="mm_dw_bs")
    grads['w_glu'] = mm(gy, dz, ta=True, tm=512, tn=1024, tk=2048, out_dtype=BF16, name="mm_dw_glu")
    dgy = mm(dz, wts['w_glu'], tb=True, tm=1024, tn=512, tk=1024, inner='m', out_dtype=F32, name="mm_dgy")
    plan = up_scatter(1)
    du, d_bd, d_cd, d_ab, sgr['ssm_d'] = take(
        _ssm_bwd(proj, y, dgy, xs, ab, bdt_b, cdt_b, dskip, comm=plan, name="ssm_bwd"), plan, ['w_up#1'], recvs)
    keys = ['w_out', 'w_ba', 'w_bs', 'w_glu']
    plan = scatter(keys)
    dq, dkc, dkp, dvc, dvp, dsink = take(
        _attn_bwd(proj, sinks, dattn, comm=plan, name="attn_bwd"), plan, keys, recvs)
    sgr['attn_sinks'] = dsink[:, :N_Q_HEADS]
    dproj, sgr['b_in'] = _join_dproj(dq, dkc, dkp, dvc, dvp, du, dga, dgs, name="join_dproj")
    grads['w_in_t'] = mm(dproj, h1, ta=True, tm=2944, tn=1024, tk=1024, out_dtype=BF16, name="mm_dw_in")
    dh1 = mm_host(['w_in_t'], scatter, recvs,
                  dproj, wts['w_in_t'], tm=1024, tn=1024, tk=2944, out_dtype=BF16, name="mm_dh1")
    grad_x, sgr['attn_norm_g'] = _rowwise(
        _rms_bwd_hook(x, small['attn_norm_g'], dx2b, tr, F32), dh1, tr, name="rms1_bwd")

    from_bd = lambda t: _block_diag_take(t, SSM_GROUP, SSM_STATE).transpose(0, 1, 3, 2).reshape(
        SSM_GROUPS * SSM_STATE, SSM_GROUP)
    d_bb_re = from_bd(d_bd[:, :, :SSM_X_BLK])
    d_bb_im = from_bd(d_bd[:, :, SSM_X_BLK:])
    d_cdt = d_cd.transpose(0, 2, 1)
    shape_c = (1, SSM_GROUPS, SSM_GROUP, SSM_STATE)
    sgr['ssm_c_re'] = _block_diag_take(d_cdt[:, :, :SSM_X_BLK], SSM_GROUP, SSM_STATE).reshape(shape_c)
    sgr['ssm_c_im'] = -_block_diag_take(d_cdt[:, :, SSM_X_BLK:], SSM_GROUP, SSM_STATE).reshape(shape_c)
    d_a_re, d_a_im, d_ldt, d_b_re, d_b_im = _ssm_disc_bwd(
        a_re, a_im, log_dt, b_re, b_im, d_ab[:, 0, :].reshape(-1, 1), d_ab[:, 1, :].reshape(-1, 1),
        d_bb_re, d_bb_im, name="ssm_disc_bwd")
    sgr['ssm_a_re'] = d_a_re.reshape(1, SSM_GROUPS, SSM_STATE)
    sgr['ssm_a_im'] = d_a_im.reshape(1, SSM_GROUPS, SSM_STATE)
    sgr['ssm_log_dt'] = d_ldt.reshape(SSM_GROUPS, SSM_STATE).sum(axis=1).reshape(1, SSM_GROUPS)
    sgr['ssm_b_re'] = d_b_re.reshape(1, SSM_GROUPS, SSM_STATE, SSM_GROUP)
    sgr['ssm_b_im'] = d_b_im.reshape(1, SSM_GROUPS, SSM_STATE, SSM_GROUP)
    return loss, grad_x, grads, recvs, sgr


def _swap_cores(arrs, *, name):
    n = len(arrs)

    def body(*refs):
        ins, outs = refs[:n], refs[n:2 * n]
        send_sems, recv_sems = refs[2 * n:]
        x, y, c = _place()
        copies = []
        for i in range(n):
            cp = pltpu.make_async_remote_copy(
                src_ref=ins[i], dst_ref=outs[i], send_sem=send_sems.at[i], recv_sem=recv_sems.at[i],
                device_id=(x, y, 1 - c), device_id_type=MESH)
            cp.start()
            copies.append(cp)
        for cp in copies:
            cp.wait()

    return pl.pallas_call(
        body, in_specs=[ANY] * n, out_specs=[ANY] * n,
        out_shape=[jax.ShapeDtypeStruct(a.shape, a.dtype) for a in arrs],
        scratch_shapes=[pltpu.SemaphoreType.DMA((n,)), pltpu.SemaphoreType.DMA((n,))],
        name=name)(*arrs)


def _all_reduce_small(buf, *, name):
    r = buf.shape[0]

    def body(in_ref, out_ref, slots, send_sems, recv_sems):
        x, y, c = _place()
        me = 4 * x + 2 * y + c
        slots[pl.ds(me, 1)] = in_ref[...][None]
        copies = []
        for k in range(N_DEV - 1):
            bx, by, bc = ((k + 1) >> 2) & 1, ((k + 1) >> 1) & 1, (k + 1) & 1
            peer = (1 - x if bx else x, 1 - y if by else y, 1 - c if bc else c)
            cp = pltpu.make_async_remote_copy(
                src_ref=in_ref, dst_ref=slots.at[me], send_sem=send_sems.at[k], recv_sem=recv_sems.at[k],
                device_id=peer, device_id_type=MESH)
            cp.start()
            copies.append(cp)
        for cp in copies:
            cp.wait()
        acc = slots[0]
        for d in range(1, N_DEV):
            acc = acc + slots[d]
        out_ref[...] = acc

    vm = pl.BlockSpec(memory_space=pltpu.VMEM)
    return pl.pallas_call(
        body, in_specs=[vm], out_specs=vm, out_shape=jax.ShapeDtypeStruct((r, 128), F32),
        scratch_shapes=[pltpu.VMEM((N_DEV, r, 128), F32), pltpu.SemaphoreType.DMA((N_DEV - 1,)),
                        pltpu.SemaphoreType.DMA((N_DEV - 1,))],
        name=name)(buf)


def _pack(arrs):
    flat = jnp.concatenate([a.reshape(-1).astype(F32) for a in arrs])
    pad = (-flat.shape[0]) % 1024
    return jnp.pad(flat, (0, pad)).reshape(-1, 128)


def _unpack(buf, shapes):
    flat = buf.reshape(-1)
    out, pos = [], 0
    for s in shapes:
        size = math.prod(s)
        out.append(flat[pos:pos + size].reshape(s))
        pos += size
    return out


TILE_ELEMS = 256 * 1024


def _tile_rows(r, c):
    if r * c <= TILE_ELEMS:
        return r
    for tr in range(TILE_ELEMS // c // 16 * 16, 0, -16):
        if r % tr == 0:
            return tr
    raise ValueError((r, c))


def _sum4(full, axis, pos, recvs, *, name):
    r, c = _block_shape(full.shape, axis)
    parts = len(recvs)
    tr = _tile_rows(r // parts, c)
    per = r // parts // tr
    nt = r // tr

    def body(pos_ref, o_ref, *refs):
        out_ref = refs[parts]
        for p in range(parts):
            @pl.when(pl.program_id(0) // per == p)
            def _():
                acc = o_ref[...].astype(F32)
                for k in range(3):
                    acc = acc + refs[p][k].astype(F32)
                out_ref[...] = acc.astype(BF16)

    own = (pl.BlockSpec((tr, c), lambda i, pos_ref: (pos_ref[0] * nt + i, 0)) if axis == 0
           else pl.BlockSpec((tr, c), lambda i, pos_ref: (i, pos_ref[0])))
    part_spec = lambda p: pl.BlockSpec((3, tr, c), lambda i, pos_ref: (0, jnp.clip(i - p * per, 0, per - 1), 0))
    grid_spec = pltpu.PrefetchScalarGridSpec(
        num_scalar_prefetch=1, grid=(nt,), in_specs=[own] + [part_spec(p) for p in range(parts)],
        out_specs=pl.BlockSpec((tr, c), lambda i, pos_ref: (i, 0)))
    return pl.pallas_call(
        body, grid_spec=grid_spec, out_shape=jax.ShapeDtypeStruct((r, c), BF16),
        name=name, compiler_params=_cp(("parallel",)))(pos, full, *recvs)


def _adam_step(w, g, m, v):
    bc1 = 1.0 - ADAM_B1 ** ADAM_STEP
    bc2 = 1.0 - ADAM_B2 ** ADAM_STEP
    mn = ADAM_B1 * m + (1.0 - ADAM_B1) * g
    vn = ADAM_B2 * v + (1.0 - ADAM_B2) * (g * g)
    m_hat = mn / bc1
    v_hat = vn / bc2
    return -ADAM_LR * (m_hat / (jnp.sqrt(v_hat) + ADAM_EPS) + ADAM_WD * w), mn, vn


def _adamw(w, ga, gb, m, v, *, name):
    r, c = w.shape
    tr = _tile_rows(r, c)

    def body(w_ref, ga_ref, gb_ref, m_ref, v_ref, g_out, d_out, m_out, v_out):
        g = ga_ref[...].astype(F32) + gb_ref[...].astype(F32)
        g_out[...] = g
        d_out[...], m_out[...], v_out[...] = _adam_step(w_ref[...], g, m_ref[...], v_ref[...])

    spec = pl.BlockSpec((tr, c), lambda i: (i, 0))
    shp = jax.ShapeDtypeStruct((r, c), F32)
    return pl.pallas_call(
        body, grid=(r // tr,), in_specs=[spec] * 5, out_specs=[spec] * 4,
        out_shape=[shp] * 4, name=name, compiler_params=_cp(("parallel",)))(w, ga, gb, m, v)


def _lanes(t):
    return t.reshape(-1, 128) if t.size % 128 == 0 else t.reshape(1, -1)


def _adamw_small(ws, gs, ms, vs, *, name):
    n = len(ws)

    def body(*refs):
        for i in range(n):
            w_ref, g_ref, m_ref, v_ref = (refs[k * n + i] for k in range(4))
            outs = [refs[(4 + k) * n + i] for k in range(3)]
            outs[0][...], outs[1][...], outs[2][...] = _adam_step(w_ref[...], g_ref[...], m_ref[...], v_ref[...])

    flat = [_lanes(t) for group in (ws, gs, ms, vs) for t in group]
    shp = [jax.ShapeDtypeStruct(_lanes(t).shape, F32) for t in ws]
    res = pl.pallas_call(body, out_shape=shp * 3, name=name)(*flat)
    return [[res[k * n + i].reshape(ws[i].shape) for i in range(n)] for k in range(3)]


BIG = ['w_in', 'w_glu', 'w_branch_attn', 'w_branch_ssm', 'w_out', 'w_up', 'w_down']
BIG_KEY = {'w_in': 'w_in_t', 'w_glu': 'w_glu', 'w_branch_attn': 'w_ba', 'w_branch_ssm': 'w_bs',
           'w_out': 'w_out', 'w_up': 'w_up', 'w_down': 'w_down'}
TRANSPOSED = {'w_in'}
SMALL = ['attn_norm_g', 'b_in', 'attn_sinks', 'ssm_a_re', 'ssm_a_im', 'ssm_log_dt', 'ssm_b_re', 'ssm_b_im',
         'ssm_c_re', 'ssm_c_im', 'ssm_d', 'b_glu', 'ffn_norm_g', 'conv_b', 'final_norm_g']
WEIGHTS = ['attn_norm_g', 'w_in', 'b_in', 'attn_sinks', 'ssm_a_re', 'ssm_a_im', 'ssm_log_dt', 'ssm_b_re',
           'ssm_b_im', 'ssm_c_re', 'ssm_c_im', 'ssm_d', 'w_glu', 'b_glu', 'w_branch_attn', 'w_branch_ssm',
           'w_out', 'ffn_norm_g', 'w_up', 'conv_w', 'conv_b', 'w_down', 'final_norm_g']


def _shard_2d(name, t):
    t = t[0]
    return t.T if name in TRANSPOSED else t


def _unshard_2d(name, t):
    return (t.T if name in TRANSPOSED else t)[None]


def kernel(x, attn_norm_g, w_in, b_in, attn_sinks, ssm_a_re, ssm_a_im, ssm_log_dt, ssm_b_re, ssm_b_im, ssm_c_re, ssm_c_im, ssm_d, w_glu, b_glu, w_branch_attn, w_branch_ssm, w_out, ffn_norm_g, w_up, conv_w, conv_b, w_down, final_norm_g, loss_target, m_attn_norm_g, m_w_in, m_b_in, m_attn_sinks, m_ssm_a_re, m_ssm_a_im, m_ssm_log_dt, m_ssm_b_re, m_ssm_b_im, m_ssm_c_re, m_ssm_c_im, m_ssm_d, m_w_glu, m_b_glu, m_w_branch_attn, m_w_branch_ssm, m_w_out, m_ffn_norm_g, m_w_up, m_conv_w, m_conv_b, m_w_down, m_final_norm_g, v_attn_norm_g, v_w_in, v_b_in, v_attn_sinks, v_ssm_a_re, v_ssm_a_im, v_ssm_log_dt, v_ssm_b_re, v_ssm_b_im, v_ssm_c_re, v_ssm_c_im, v_ssm_d, v_w_glu, v_b_glu, v_w_branch_attn, v_w_branch_ssm, v_w_out, v_ffn_norm_g, v_w_up, v_conv_w, v_conv_b, v_w_down, v_final_norm_g):
    args = dict(locals())
    w = {n: args[n] for n in WEIGHTS}
    m = {n: args['m_' + n] for n in WEIGHTS}
    v = {n: args['v_' + n] for n in WEIGHTS}
    xi, yi, ci = _place()
    blk = 2 * xi + yi

    shards = {BIG_KEY[n]: _shard_2d(n, w[n]).astype(BF16) for n in BIG}
    cw_cols = w['conv_w'].shape[2]
    cw_place = lax.dynamic_update_slice(jnp.zeros((3, D_FF), F32), w['conv_w'][0] * (ci == 0).astype(F32),
                                        (0, blk * cw_cols))
    conv_w_full = _unpack(_all_reduce_small(_pack([cw_place]), name="gather_conv_w"), [(3, D_FF)])[0]

    small = {n: w[n] for n in SMALL}
    small['conv_w'] = conv_w_full
    loss_part, grad_x, grads, recvs, sgr = _local_step(x[0], loss_target[0], {}, small, shards)

    halves = []
    for n in BIG:
        key = BIG_KEY[n]
        full = grads[key]
        recv = [recvs[key]] if key in recvs else [recvs[key + '#0'], recvs[key + '#1']]
        axis, interleaved = LAYOUT[key]
        pos = _block_pos(xi, yi, interleaved).astype(jnp.int32).reshape(1)
        halves.append(_sum4(full, axis, pos, recv, name="sum4_" + n))
    others = _swap_cores(halves, name="swap_cores")
    out = {}
    for n, mine, other in zip(BIG, halves, others):
        res = _adamw(_shard_2d(n, w[n]), mine, other, _shard_2d(n, m[n]), _shard_2d(n, v[n]), name="adamw_" + n)
        out[n] = [_unshard_2d(n, t) for t in res]

    names = SMALL + ['conv_w']
    shapes = [w[n].shape for n in SMALL] + [(3, D_FF)]
    packed = _pack([sgr[n] for n in names] + [loss_part])
    summed = _unpack(_all_reduce_small(packed, name="all_reduce_small"), shapes + [(1, 1)])
    loss = summed[-1].reshape(())
    sg = dict(zip(names, summed[:-1]))
    sg['conv_w'] = lax.dynamic_slice_in_dim(sg['conv_w'], blk * cw_cols, cw_cols, axis=1)[None]
    deltas, new_m, new_v = _adamw_small([w[n] for n in names], [sg[n] for n in names], [m[n] for n in names],
                                        [v[n] for n in names], name="adamw_small")
    for i, n in enumerate(names):
        out[n] = [sg[n], deltas[i], new_m[i], new_v[i]]

    return (loss, grad_x[None], *[out[n][0] for n in WEIGHTS], *[out[n][1] for n in WEIGHTS],
            *[out[n][2] for n in WEIGHTS], *[out[n][3] for n in WEIGHTS])
```

```python
import functools
import math

import jax
import jax.numpy as jnp
from jax import lax
from jax.experimental import pallas as pl
from jax.experimental.pallas import tpu as pltpu

F32 = jnp.float32
BF16 = jnp.bfloat16

D_MODEL = 2048
N_Q_HEADS = 16
HEAD_DIM = 64
ATTN_WIDTH = 1024
KV_WIDTH = 128
BLOCK = 128
SSM_WIDTH = 512
SSM_GROUPS = 32
SSM_GROUP = 16
SSM_STATE = 64
D_FF = 5632
IN_COLS = 5888
RMS_EPS = 1e-6
NEG_BIG = -1e30
N_CHIPS = 4
N_DEV = 8

COL_K = 8
COL_V = 9
COL_U = 10
COL_GA = 14
COL_GS = 30

SSM_SPLIT = 4
SSM_U_BLK = 128
SSM_X_BLK = 512
SSM_CHUNK = 256

ADAM_LR = 0.001
ADAM_B1 = 0.9
ADAM_B2 = 0.999
ADAM_EPS = 1e-08
ADAM_WD = 0.01
ADAM_STEP = 10

VMEM_LIMIT_BYTES = 56 * 1024 * 1024
INV_SQRT2 = 1.0 / math.sqrt(2.0)
INV_SQRT2PI = 1.0 / math.sqrt(2.0 * math.pi)
MESH = pl.DeviceIdType.MESH
ANY = pl.BlockSpec(memory_space=pl.ANY)


def _cp(sem):
    return pltpu.CompilerParams(dimension_semantics=sem, vmem_limit_bytes=VMEM_LIMIT_BYTES)


def _gelu(x):
    return 0.5 * x * (1.0 + lax.erf(x * INV_SQRT2))


def _gelu_grad(x):
    return 0.5 * (1.0 + lax.erf(x * INV_SQRT2)) + x * jnp.exp(-0.5 * x * x) * INV_SQRT2PI


def _sigmoid(x):
    return 1.0 / (1.0 + jnp.exp(-x))


def _place():
    return lax.axis_index("x"), lax.axis_index("y"), lax.axis_index("c")


def _other_chips(x, y):
    return [(1 - x, y), (x, 1 - y), (1 - x, 1 - y)]


def _block_pos(x, y, interleaved):
    return x + 2 * y if interleaved else 2 * x + y


LAYOUT = {'w_in_t': (0, False), 'w_glu': (1, False), 'w_ba': (1, False), 'w_bs': (1, False),
          'w_out': (0, False), 'w_up': (1, True), 'w_down': (0, False)}


def _window(ref, axis, pos, size, rows=None):
    if axis == 0:
        start, count = (0, size) if rows is None else rows
        return ref.at[pl.ds(pos * size + start, count), :]
    cols = pl.ds(pos * size, size)
    return ref.at[:, cols] if rows is None else ref.at[pl.ds(rows[0], rows[1]), cols]


def _gathered_shape(shape, axis):
    return tuple(N_CHIPS * d if a == axis else d for a, d in enumerate(shape))


def _block_shape(shape, axis):
    return tuple(d // N_CHIPS if a == axis else d for a, d in enumerate(shape))


class _GatherPlan:
    def __init__(self, shards, layouts):
        self.arrays = list(shards)
        self.layouts = list(layouts)
        n = len(shards)
        self.out_shape = [jax.ShapeDtypeStruct(_gathered_shape(s.shape, lay[0]), s.dtype)
                          for s, lay in zip(shards, layouts)]
        self.scratch = [pltpu.SemaphoreType.DMA((6 * n,)), pltpu.SemaphoreType.DMA((6 * n,)),
                        pltpu.SemaphoreType.DMA((n,))]

    def _copies(self, kind, ins, outs, sems):
        send, recv, local = sems
        n = len(self.arrays)
        x, y, c = _place()
        copies = []
        for i in range(n):
            axis, interleaved = self.layouts[i]
            size = self.arrays[i].shape[axis]
            h = self.arrays[i].shape[0] // 2
            first = lambda core: core * h
            blk = _block_pos(x, y, interleaved)
            if kind == 'mine':
                copies.append(pltpu.make_async_copy(ins[i], _window(outs[i], axis, blk, size), local.at[i]))
                continue
            for k, (px, py) in enumerate(_other_chips(x, y)):
                theirs = _block_pos(px, py, interleaved)
                if kind in ('ici_out', 'ici_in'):
                    route = dict(send_sem=send.at[3 * i + k], recv_sem=recv.at[3 * i + k],
                                 device_id=(px, py, c), device_id_type=MESH)
                else:
                    route = dict(send_sem=send.at[3 * (n + i) + k], recv_sem=recv.at[3 * (n + i) + k],
                                 device_id=(x, y, 1 - c), device_id_type=MESH)
                if kind == 'ici_out':
                    src, dst = ins[i].at[pl.ds(first(c), h), :], _window(outs[i], axis, blk, size, (first(c), h))
                elif kind == 'd2d_in':
                    src = dst = _window(outs[i], axis, theirs, size, (first(1 - c), h))
                else:
                    src = dst = _window(outs[i], axis, theirs, size, (first(c), h))
                copies.append(pltpu.make_async_remote_copy(src_ref=src, dst_ref=dst, **route))
        return copies

    def start(self, ins, outs, sems):
        for cp in self._copies('mine', ins, outs, sems) + self._copies('ici_out', ins, outs, sems):
            cp.start()

    def middle(self, ins, outs, sems):
        for arrived, onward in zip(self._copies('ici_in', ins, outs, sems), self._copies('d2d_out', ins, outs, sems)):
            arrived.wait_recv()
            onward.start()

    def finish(self, ins, outs, sems):
        for cp in self._copies('d2d_in', ins, outs, sems):
            cp.wait_recv()
        for cp in self._copies('ici_out', ins, outs, sems) + self._copies('d2d_out', ins, outs, sems):
            cp.wait_send()
        for cp in self._copies('mine', ins, outs, sems):
            cp.wait()


class _ScatterPlan:
    def __init__(self, fulls, layouts, part=(0, 1)):
        self.arrays = list(fulls)
        self.layouts = list(layouts)
        self.part = part
        n = len(fulls)
        self.out_shape = []
        for f, lay in zip(fulls, layouts):
            rows, cols = _block_shape(f.shape, lay[0])
            self.out_shape.append(jax.ShapeDtypeStruct((3, rows // part[1], cols), f.dtype))
        self.scratch = [pltpu.SemaphoreType.DMA((3 * n,)), pltpu.SemaphoreType.DMA((3 * n,))]

    def _copies(self, ins, outs, sems):
        send, recv = sems
        x, y, c = _place()
        copies = []
        for i in range(len(self.arrays)):
            axis, interleaved = self.layouts[i]
            size = self.arrays[i].shape[axis] // N_CHIPS
            h = _block_shape(self.arrays[i].shape, axis)[0] // self.part[1]
            rows = (self.part[0] * h, h)
            for k, (px, py) in enumerate(_other_chips(x, y)):
                copies.append(pltpu.make_async_remote_copy(
                    src_ref=_window(ins[i], axis, _block_pos(px, py, interleaved), size, rows), dst_ref=outs[i].at[k],
                    send_sem=send.at[3 * i + k], recv_sem=recv.at[3 * i + k],
                    device_id=(px, py, c), device_id_type=MESH))
        return copies

    def start(self, ins, outs, sems):
        for cp in self._copies(ins, outs, sems):
            cp.start()

    def middle(self, ins, outs, sems):
        pass

    def finish(self, ins, outs, sems):
        for cp in self._copies(ins, outs, sems):
            cp.wait()


def _hosted_call(body, *, grid, in_specs, out_specs, out_shape, scratch_shapes, sem, name, args, comm=None,
                 aliases=None):
    aliases = aliases or {}
    if comm is None:
        outs = pl.pallas_call(body, grid=grid, in_specs=in_specs, out_specs=out_specs, out_shape=out_shape,
                              scratch_shapes=scratch_shapes, name=name, input_output_aliases=aliases,
                              compiler_params=_cp(sem))(*args)
        return outs, None
    n_in, n_out, n_scr = len(in_specs), len(out_specs), len(scratch_shapes)
    nc, ns = len(comm.arrays), len(comm.scratch)
    total = math.prod(grid)
    mid = total - max(1, total // 8)

    def wrapped(*refs):
        pos = 0
        ins = refs[pos:pos + n_in]; pos += n_in
        cins = refs[pos:pos + nc]; pos += nc
        outs = refs[pos:pos + n_out]; pos += n_out
        couts = refs[pos:pos + nc]; pos += nc
        scr = refs[pos:pos + n_scr]; pos += n_scr
        sems = refs[pos:pos + ns]
        step = 0
        for ax, g in enumerate(grid):
            step = step * g + pl.program_id(ax)

        @pl.when(step == 0)
        def _():
            comm.start(cins, couts, sems)

        body(*ins, *outs, *scr)

        @pl.when(step == mid)
        def _():
            comm.middle(cins, couts, sems)

        @pl.when(step == total - 1)
        def _():
            comm.finish(cins, couts, sems)

    res = pl.pallas_call(
        wrapped, grid=grid, in_specs=list(in_specs) + [ANY] * nc, out_specs=list(out_specs) + [ANY] * nc,
        out_shape=list(out_shape) + list(comm.out_shape), scratch_shapes=list(scratch_shapes) + list(comm.scratch),
        name=name, input_output_aliases=aliases,
        compiler_params=_cp(("arbitrary",) * len(grid)))(*args, *comm.arrays)
    return res[:n_out], res[n_out:]


class _Hook:
    def __init__(self, fn, ins=(), in_specs=(), outs=()):
        self.fn, self.ins, self.in_specs, self.outs = fn, list(ins), list(in_specs), list(outs)


def _matmul(a, b, *, ta=False, tb=False, tm, tn, tk, out_dtype=None, bias=None, res=None, inner='n',
            comm=None, prologue=None, epilogue=None, a_shape=None, sequential=False, name):
    if a is None:
        m, kdim = a_shape
    elif ta:
        kdim, m = a.shape
    else:
        m, kdim = a.shape
    if tb:
        n, k2 = b.shape
    else:
        k2, n = b.shape
    assert kdim == k2, (name, kdim, b.shape)
    tm, tn, tk = min(tm, m), min(tn, n), min(tk, kdim)
    assert m % tm == 0 and n % tn == 0 and kdim % tk == 0, (name, m, n, kdim, tm, tn, tk)
    nk = kdim // tk
    dn = (((0 if ta else 1,), (1 if tb else 0,)), ((), ()))
    hooks = [h for h in (prologue, epilogue) if h is not None]
    n_pro_in = len(prologue.ins) if prologue else 0
    n_epi_in = len(epilogue.ins) if epilogue else 0
    n_pro_out = len(prologue.outs) if prologue else 0
    n_epi_out = len(epilogue.outs) if epilogue else 0
    if inner == 'n':
        grid = (m // tm, n // tn, nk)
        mi = lambda g0, g1: g0
        ni = lambda g0, g1: g1
    else:
        grid = (n // tn, m // tm, nk)
        mi = lambda g0, g1: g1
        ni = lambda g0, g1: g0

    def body(*refs):
        refs = list(refs)
        take = lambda cnt: [refs.pop(0) for _ in range(cnt)]
        a_ref = take(1)[0] if a is not None else None
        b_ref = take(1)[0]
        bias_ref = take(1)[0] if bias is not None else None
        res_ref = take(1)[0] if res is not None else None
        pro_in, epi_in = take(n_pro_in), take(n_epi_in)
        o_ref = take(1)[0] if epilogue is None else None
        pro_out, epi_out = take(n_pro_out), take(n_epi_out)
        i, j, k = mi(pl.program_id(0), pl.program_id(1)), ni(pl.program_id(0), pl.program_id(1)), pl.program_id(2)

        def finish(src):
            def result(rows=slice(None)):
                r = src[rows, :]
                if bias_ref is not None:
                    r = r + bias_ref[...]
                if res_ref is not None:
                    r = r + res_ref[rows, :]
                return r

            if epilogue is None:
                o_ref[...] = result().astype(out_dtype)
            else:
                epilogue.fn(result, epi_in, epi_out, i, j)

        a_val = a_ref[...] if prologue is None else prologue.fn(a_ref, pro_in, pro_out, i, k)
        prod = lax.dot_general(a_val.astype(BF16), b_ref[...].astype(BF16), dn, preferred_element_type=F32)
        if nk == 1:
            finish(prod)
            return
        acc_ref = refs[0]

        @pl.when(k == 0)
        def _():
            acc_ref[...] = prod

        @pl.when(k > 0)
        def _():
            acc_ref[...] += prod

        @pl.when(k == nk - 1)
        def _():
            finish(acc_ref)

    spec = lambda shape, fn: pl.BlockSpec(shape, lambda g0, g1, k: fn(mi(g0, g1), ni(g0, g1), k))
    in_specs, args = [], []
    if a is not None:
        in_specs.append(spec((tk, tm), lambda i, j, k: (k, i)) if ta else spec((tm, tk), lambda i, j, k: (i, k)))
        args.append(a)
    in_specs.append(spec((tn, tk), lambda i, j, k: (j, k)) if tb else spec((tk, tn), lambda i, j, k: (k, j)))
    args.append(b)
    if bias is not None:
        in_specs.append(spec((1, tn), lambda i, j, k: (0, j)))
        args.append(bias)
    if res is not None:
        in_specs.append(spec((tm, tn), lambda i, j, k: (i, j)))
        args.append(res)
    for h in hooks:
        in_specs += [spec(shape, fn) for shape, fn in h.in_specs]
        args += h.ins
    out_specs, out_shape = [], []
    if epilogue is None:
        out_specs.append(spec((tm, tn), lambda i, j, k: (i, j)))
        out_shape.append(jax.ShapeDtypeStruct((m, n), out_dtype))
    for h in hooks:
        out_specs += [spec(blk, fn) for _, _, blk, fn in h.outs]
        out_shape += [jax.ShapeDtypeStruct(shape, dtype) for shape, dtype, _, _ in h.outs]
    outs, couts = _hosted_call(
        body, grid=grid, in_specs=in_specs, out_specs=out_specs, out_shape=out_shape,
        scratch_shapes=[pltpu.VMEM((tm, tn), F32)] if nk > 1 else [],
        sem=("arbitrary",) * 3 if sequential else ("parallel", "parallel", "arbitrary"),
        name=name, args=args, comm=comm)
    outs = outs[0] if not hooks else outs
    return outs if comm is None else (outs, couts)


def _rms_fwd(x, g, *, comm=None, name):
    l, d = x.shape
    tr = min(256, l)

    def body(x_ref, g_ref, h_ref):
        xf = x_ref[...]
        r = lax.rsqrt(jnp.mean(xf * xf, axis=-1, keepdims=True) + RMS_EPS)
        h_ref[...] = ((xf * r) * g_ref[...]).astype(BF16)

    row = pl.BlockSpec((tr, d), lambda i: (i, 0))
    return _hosted_call(
        body, grid=(l // tr,), in_specs=[row, pl.BlockSpec((1, d), lambda i: (0, 0))],
        out_specs=[row], out_shape=[jax.ShapeDtypeStruct((l, d), BF16)], scratch_shapes=[],
        sem=("parallel",), name=name, args=(x, g), comm=comm)


EPI_ROWS = 128
ROW_TILE = 256


def _row_chunks(tm):
    ch = min(EPI_ROWS, tm)
    return [slice(c * ch, (c + 1) * ch) for c in range(tm // ch)]


def _rowwise(hook, src, tm, *, name):
    l, d = src.shape
    n_in = len(hook.ins)

    def body(*refs):
        src_ref, ins, outs = refs[0], refs[1:1 + n_in], refs[1 + n_in:]
        hook.fn(lambda rows=slice(None): src_ref[rows, :], ins, outs, pl.program_id(0), 0)

    spec = lambda shape, fn: pl.BlockSpec(shape, lambda i: fn(i, 0, 0))
    return pl.pallas_call(
        body, grid=(l // tm,),
        in_specs=[pl.BlockSpec((tm, d), lambda i: (i, 0))] + [spec(shape, fn) for shape, fn in hook.in_specs],
        out_specs=[spec(blk, fn) for _, _, blk, fn in hook.outs],
        out_shape=[jax.ShapeDtypeStruct(shape, dtype) for shape, dtype, _, _ in hook.outs],
        name=name, compiler_params=_cp(("arbitrary",)))(src, *hook.ins)


def _rms_bwd_hook(x, g, dres, tm, out_dtype):
    def fn(result, ins, outs, i, j):
        x_ref, g_ref, dres_ref = ins
        dx_ref, dg_ref = outs

        @pl.when(i == 0)
        def _():
            dg_ref[...] = jnp.zeros_like(dg_ref)

        for rows in _row_chunks(tm):
            dyv = result(rows).astype(F32)
            xf = x_ref[rows, :]
            r = lax.rsqrt(jnp.mean(xf * xf, axis=-1, keepdims=True) + RMS_EPS)
            xhat = xf * r
            dxh = dyv * g_ref[...]
            dx = r * (dxh - xhat * jnp.mean(dxh * xhat, axis=-1, keepdims=True)) + dres_ref[rows, :].astype(F32)
            dx_ref[rows, :] = dx.astype(out_dtype)
            dg_ref[...] += jnp.sum(dyv * xhat, axis=0, keepdims=True)

    l, d = x.shape
    row = lambda i, j, k: (i, 0)
    vec = lambda i, j, k: (0, 0)
    return _Hook(fn, ins=[x, g, dres], in_specs=[((tm, d), row), ((1, d), vec), ((tm, d), row)],
                 outs=[((l, d), out_dtype, (tm, d), row), ((1, d), F32, (1, d), vec)])


def _final_loss_hook(g, target, tm):
    l, d = target.shape

    def fn(result, ins, outs, i, j):
        g_ref, t_ref = ins
        dxb_ref, dg_ref, loss_ref = outs
        gv = g_ref[...]

        @pl.when(i == 0)
        def _():
            dg_ref[...] = jnp.zeros_like(dg_ref)
            loss_ref[...] = jnp.zeros_like(loss_ref)

        for rows in _row_chunks(tm):
            xf = result(rows)
            r = lax.rsqrt(jnp.mean(xf * xf, axis=-1, keepdims=True) + RMS_EPS)
            xhat = xf * r
            diff = xhat * gv - t_ref[rows, :]
            dout = diff * (1.0 / d)
            dxh = dout * gv
            dx = r * (dxh - xhat * jnp.mean(dxh * xhat, axis=-1, keepdims=True))
            dxb_ref[rows, :] = dx.astype(BF16)
            dg_ref[...] += jnp.sum(dout * xhat, axis=0, keepdims=True)
            part = jnp.sum(jnp.mean(diff * diff, axis=-1, keepdims=True), axis=0, keepdims=True)
            loss_ref[...] += 0.5 * part

    row = lambda i, j, k: (i, 0)
    vec = lambda i, j, k: (0, 0)
    return _Hook(fn, ins=[g, target], in_specs=[((1, d), vec), ((tm, d), row)],
                 outs=[((l, d), BF16, (tm, d), row), ((1, d), F32, (1, d), vec), ((1, 1), F32, (1, 1), vec)])


Q_PER_KV = 8
GROUP_ROWS = Q_PER_KV * BLOCK


def _attn_masks(n, rows=GROUP_ROWS):
    q_idx = lax.broadcasted_iota(jnp.int32, (rows, 2 * BLOCK), 0) & (BLOCK - 1)
    s_idx = lax.broadcasted_iota(jnp.int32, (rows, 2 * BLOCK), 1)
    dist = q_idx + BLOCK - s_idx
    valid = (dist >= 0) & (dist < BLOCK) & ((n > 0) | (s_idx >= BLOCK))
    return dist.astype(F32), valid


def _dup_half(t, kv_head, lo):
    rolled = pltpu.roll(t, HEAD_DIM, axis=1)
    return jnp.where(lo, t, rolled) if kv_head == 0 else jnp.where(lo, rolled, t)


def _stack_heads(ref, kv_head, lo):
    pieces = []
    for r in range(Q_PER_KV):
        pair = kv_head * 4 + r // 2
        t = ref[:, pair * 128:(pair + 1) * 128].astype(BF16)
        sel = lo if r % 2 == 0 else jnp.logical_not(lo)
        pieces.append(jnp.where(sel, t, jnp.zeros_like(t)))
    return jnp.concatenate(pieces, axis=0)


def _unstack_heads(t, lo):
    return [jnp.where(lo, t[(2 * i) * BLOCK:(2 * i + 1) * BLOCK], t[(2 * i + 1) * BLOCK:(2 * i + 2) * BLOCK])
            for i in range(Q_PER_KV // 2)]


def _per_head_column(values):
    return jnp.concatenate([jnp.full((BLOCK, 1), v, F32) for v in values], axis=0)


def _group_probs(qm, kdup, dist, valid, sink_ref, kv_head):
    heads = [kv_head * Q_PER_KV + r for r in range(Q_PER_KV)]
    slope = _per_head_column([2.0 ** (-8.0 * (h + 1) / N_Q_HEADS) for h in heads])
    sink = _per_head_column([sink_ref[h] for h in heads])
    return _probs(qm, kdup, dist, valid, sink, slope)


def _probs(qm, kdup, dist, valid, sink, slope):
    s = lax.dot_general(qm, kdup, (((1,), (1,)), ((), ())), preferred_element_type=F32)
    s = s * (HEAD_DIM ** -0.5) - slope * dist
    s = jnp.where(valid, s, NEG_BIG)
    m = jnp.maximum(jnp.max(s, axis=-1, keepdims=True), sink)
    p = jnp.exp(s - m)
    esink = jnp.exp(sink - m)
    inv = 1.0 / (jnp.sum(p, axis=-1, keepdims=True) + esink)
    return p * inv, esink * inv


def _attn_fwd(proj, sinks, *, name):
    l = proj.shape[0]
    nb = l // BLOCK

    def body(sink_ref, q_ref, kc_ref, kp_ref, vc_ref, vp_ref, o_ref):
        n = pl.program_id(0)
        dist, valid = _attn_masks(n, BLOCK)
        lo = lax.broadcasted_iota(jnp.int32, (1, BLOCK), 1) < HEAD_DIM
        kx = jnp.concatenate([kp_ref[...], kc_ref[...]], axis=0).astype(BF16)
        vx = jnp.concatenate([vp_ref[...], vc_ref[...]], axis=0).astype(BF16)
        for kv_head in range(2):
            kdup = _dup_half(kx, kv_head, lo)
            vdup = _dup_half(vx, kv_head, lo)
            for pr in range(4):
                pair = kv_head * 4 + pr
                qp = q_ref[:, pair * 128:(pair + 1) * 128].astype(BF16)
                o_pair = jnp.zeros((BLOCK, 128), F32)
                for half in range(2):
                    head = 2 * pair + half
                    sel = lo if half == 0 else jnp.logical_not(lo)
                    qm = jnp.where(sel, qp, jnp.zeros_like(qp))
                    p, _ = _probs(qm, kdup, dist, valid, sink_ref[head], 2.0 ** (-8.0 * (head + 1) / N_Q_HEADS))
                    o = jnp.dot(p.astype(BF16), vdup, preferred_element_type=F32)
                    o_pair = o_pair + jnp.where(sel, o, 0.0)
                o_ref[:, pair * 128:(pair + 1) * 128] = o_pair.astype(BF16)

    kv = lambda col, prev: pl.BlockSpec(
        (BLOCK, KV_WIDTH), (lambda n: (jnp.maximum(n - 1, 0), col)) if prev else (lambda n: (n, col)))
    return pl.pallas_call(
        body, grid=(nb,),
        in_specs=[pl.BlockSpec(memory_space=pltpu.SMEM),
                  pl.BlockSpec((BLOCK, ATTN_WIDTH), lambda n: (n, 0)),
                  kv(COL_K, False), kv(COL_K, True), kv(COL_V, False), kv(COL_V, True)],
        out_specs=pl.BlockSpec((BLOCK, ATTN_WIDTH), lambda n: (n, 0)),
        out_shape=jax.ShapeDtypeStruct((l, ATTN_WIDTH), BF16), name=name,
        compiler_params=_cp(("parallel",)))(sinks, proj, proj, proj, proj, proj)


def _attn_bwd(proj, sinks, dattn, *, comm=None, name):
    l = proj.shape[0]
    nb = l // BLOCK

    def body(sink_ref, q_ref, kc_ref, kp_ref, vc_ref, vp_ref, do_ref,
             dq_ref, dkc_ref, dkp_ref, dvc_ref, dvp_ref, dsink_ref):
        n = pl.program_id(0)
        dist, valid = _attn_masks(n)
        lane = lax.broadcasted_iota(jnp.int32, (1, BLOCK), 1)
        lo = lane < HEAD_DIM
        kx = jnp.concatenate([kp_ref[...], kc_ref[...]], axis=0).astype(BF16)
        vx = jnp.concatenate([vp_ref[...], vc_ref[...]], axis=0).astype(BF16)
        dsink = jnp.zeros((1, BLOCK), F32)
        dk_heads, dv_heads = [], []
        for kv_head in range(2):
            kdup = _dup_half(kx, kv_head, lo)
            vdup = _dup_half(vx, kv_head, lo)
            qm = _stack_heads(q_ref, kv_head, lo)
            dom = _stack_heads(do_ref, kv_head, lo)
            p, psink = _group_probs(qm, kdup, dist, valid, sink_ref, kv_head)
            dp = lax.dot_general(dom, vdup, (((1,), (1,)), ((), ())), preferred_element_type=F32)
            delta = jnp.sum(p * dp, axis=-1, keepdims=True)
            ds = (p * (dp - delta) * (HEAD_DIM ** -0.5)).astype(BF16)
            dsink_rows = -psink * delta
            for r in range(Q_PER_KV):
                part = jnp.sum(dsink_rows[r * BLOCK:(r + 1) * BLOCK])
                dsink = dsink + jnp.where(lane == kv_head * Q_PER_KV + r, part, 0.0)
            dq = jnp.dot(ds, kdup, preferred_element_type=F32)
            for i, dq_pair in enumerate(_unstack_heads(dq, lo)):
                pair = kv_head * 4 + i
                dq_ref[:, pair * 128:(pair + 1) * 128] = dq_pair.astype(BF16)
            dk_acc = lax.dot_general(ds, qm, (((0,), (0,)), ((), ())), preferred_element_type=F32)
            dv_acc = lax.dot_general(p.astype(BF16), dom, (((0,), (0,)), ((), ())), preferred_element_type=F32)
            dk_heads.append(dk_acc + pltpu.roll(dk_acc, HEAD_DIM, axis=1))
            dv_heads.append(dv_acc + pltpu.roll(dv_acc, HEAD_DIM, axis=1))
        dk = jnp.where(lo, dk_heads[0], dk_heads[1])
        dv = jnp.where(lo, dv_heads[0], dv_heads[1])
        dkp_ref[...] = dk[:BLOCK]
        dkc_ref[...] = dk[BLOCK:]
        dvp_ref[...] = dv[:BLOCK]
        dvc_ref[...] = dv[BLOCK:]

        @pl.when(n == 0)
        def _():
            dsink_ref[...] = jnp.zeros_like(dsink_ref)

        dsink_ref[...] += dsink

    kv = lambda col, prev: pl.BlockSpec(
        (BLOCK, KV_WIDTH), (lambda n: (jnp.maximum(n - 1, 0), col)) if prev else (lambda n: (n, col)))
    qspec = pl.BlockSpec((BLOCK, ATTN_WIDTH), lambda n: (n, 0))
    kvout = pl.BlockSpec((BLOCK, KV_WIDTH), lambda n: (n, 0))
    kvshape = jax.ShapeDtypeStruct((l, KV_WIDTH), F32)
    return _hosted_call(
        body, grid=(nb,),
        in_specs=[pl.BlockSpec(memory_space=pltpu.SMEM), qspec,
                  kv(COL_K, False), kv(COL_K, True), kv(COL_V, False), kv(COL_V, True), qspec],
        out_specs=[qspec, kvout, kvout, kvout, kvout, pl.BlockSpec((1, BLOCK), lambda n: (0, 0))],
        out_shape=[jax.ShapeDtypeStruct((l, ATTN_WIDTH), BF16), kvshape, kvshape, kvshape, kvshape,
                   jax.ShapeDtypeStruct((1, BLOCK), F32)],
        scratch_shapes=[], sem=("arbitrary",), name=name,
        args=(sinks, proj, proj, proj, proj, proj, dattn), comm=comm)


def _discretize(a_re, a_im, log_dt, b_re, b_im):
    dt = jnp.exp(log_dt)
    mag = jnp.exp(a_re * dt)
    ab_re = mag * jnp.cos(a_im * dt)
    ab_im = mag * jnp.sin(a_im * dt)
    nr = ab_re - 1.0
    ni = ab_im
    den = a_re * a_re + a_im * a_im
    z_re = (nr * a_re + ni * a_im) / den
    z_im = (ni * a_re - nr * a_im) / den
    bb_re = z_re * b_re - z_im * b_im
    bb_im = z_re * b_im + z_im * b_re
    return ab_re, ab_im, bb_re, bb_im


def _ssm_disc_fwd(a_re, a_im, log_dt, b_re, b_im, *, name):
    def body(ar, ai, ld, br, bi, o_ar, o_ai, o_br, o_bi):
        r = _discretize(ar[...], ai[...], ld[...], br[...], bi[...])
        o_ar[...], o_ai[...], o_br[...], o_bi[...] = r

    col = jax.ShapeDtypeStruct(a_re.shape, F32)
    mat = jax.ShapeDtypeStruct(b_re.shape, F32)
    return pl.pallas_call(body, out_shape=[col, col, mat, mat], name=name)(a_re, a_im, log_dt, b_re, b_im)


def _ssm_disc_bwd(a_re, a_im, log_dt, b_re, b_im, d_ab_re, d_ab_im, d_bb_re, d_bb_im, *, name):
    def body(ar, ai, ld, br, bi, g0, g1, g2, g3, o_ar, o_ai, o_ld, o_br, o_bi):
        _, vjp = jax.vjp(_discretize, ar[...], ai[...], ld[...], br[...], bi[...])
        r = vjp((g0[...], g1[...], g2[...], g3[...]))
        o_ar[...], o_ai[...], o_ld[...], o_br[...], o_bi[...] = r

    col = jax.ShapeDtypeStruct(a_re.shape, F32)
    mat = jax.ShapeDtypeStruct(b_re.shape, F32)
    return pl.pallas_call(body, out_shape=[col, col, col, mat, mat], name=name)(
        a_re, a_im, log_dt, b_re, b_im, d_ab_re, d_ab_im, d_bb_re, d_bb_im)


def _shift_rows(x, d, rows, *, down):
    t = x.shape[0]
    if down:
        return jnp.where(rows >= d, pltpu.roll(x, d, axis=0), 0.0)
    return jnp.where(rows < t - d, pltpu.roll(x, t - d, axis=0), 0.0)


def _scan_chunk(xr, xi, ar, ai, *, down):
    t = xr.shape[0]
    rows = lax.broadcasted_iota(jnp.int32, (t, 1), 0)
    pr, pi = ar, ai
    d = 1
    while d < t:
        sr = _shift_rows(xr, d, rows, down=down)
        si = _shift_rows(xi, d, rows, down=down)
        xr, xi = xr + pr * sr - pi * si, xi + pr * si + pi * sr
        pr, pi = pr * pr - pi * pi, 2.0 * pr * pi
        d *= 2
    return xr, xi


def _ssm_fwd(proj, ab, bd, cd, dskip, *, comm=None, name):
    l = proj.shape[0]
    t = min(SSM_CHUNK, l)
    nc = l // t

    def body(u_ref, ab_ref, bd_ref, cd_ref, ds_ref, y_ref, gy_ref, xs_ref, carry_ref):
        c = pl.program_id(1)

        @pl.when(c == 0)
        def _():
            carry_ref[...] = jnp.zeros_like(carry_ref)

        u = u_ref[...]
        ar, ai = ab_ref[0, 0:1, :], ab_ref[0, 1:2, :]
        bu = jnp.dot(u.astype(BF16), bd_ref[0], preferred_element_type=F32)
        rows = lax.broadcasted_iota(jnp.int32, (t, 1), 0)
        cr, ci = carry_ref[0:1, :], carry_ref[1:2, :]
        xr = bu[:, :SSM_X_BLK] + jnp.where(rows == 0, ar * cr - ai * ci, 0.0)
        xi = bu[:, SSM_X_BLK:] + jnp.where(rows == 0, ar * ci + ai * cr, 0.0)
        xr, xi = _scan_chunk(xr, xi, ar, ai, down=True)
        xs_ref[0, :, :SSM_X_BLK] = xr
        xs_ref[0, :, SSM_X_BLK:] = xi
        carry_ref[0:1, :] = xs_ref[0, t - 1:t, :SSM_X_BLK]
        carry_ref[1:2, :] = xs_ref[0, t - 1:t, SSM_X_BLK:]
        y = jnp.dot(xs_ref[0].astype(BF16), cd_ref[0], preferred_element_type=F32) + ds_ref[...] * u
        y_ref[...] = y
        gy_ref[...] = _gelu(y).astype(BF16)

    blk = lambda shape: pl.BlockSpec((1,) + shape, lambda j, c: (j, 0, 0))
    ycol = pl.BlockSpec((t, SSM_U_BLK), lambda j, c: (c, j))
    return _hosted_call(
        body, grid=(SSM_SPLIT, nc),
        in_specs=[pl.BlockSpec((t, SSM_U_BLK), lambda j, c: (c, COL_U + j)),
                  blk((2, SSM_X_BLK)), blk((SSM_U_BLK, 2 * SSM_X_BLK)), blk((2 * SSM_X_BLK, SSM_U_BLK)),
                  pl.BlockSpec((1, SSM_U_BLK), lambda j, c: (0, j))],
        out_specs=[ycol, ycol, pl.BlockSpec((1, t, 2 * SSM_X_BLK), lambda j, c: (j, c, 0))],
        out_shape=[jax.ShapeDtypeStruct((l, SSM_WIDTH), F32), jax.ShapeDtypeStruct((l, SSM_WIDTH), BF16),
                   jax.ShapeDtypeStruct((SSM_SPLIT, l, 2 * SSM_X_BLK), F32)],
        scratch_shapes=[pltpu.VMEM((2, SSM_X_BLK), F32)], sem=("parallel", "arbitrary"), name=name,
        args=(proj, ab, bd, cd, dskip), comm=comm)


def _ssm_bwd(proj, y, dgy, xs, ab, bdt, cdt, dskip, *, comm=None, name):
    l = proj.shape[0]
    t = min(SSM_CHUNK, l)
    nc = l // t

    def body(u_ref, y_ref, dgy_ref, xs_ref, halo_ref, ab_ref, bdt_ref, cdt_ref, ds_ref,
             du_ref, dbd_ref, dcd_ref, dab_ref, dd_ref, carry_ref):
        c = pl.program_id(1)
        ci_ = nc - 1 - c

        @pl.when(c == 0)
        def _():
            carry_ref[...] = jnp.zeros_like(carry_ref)
            dbd_ref[...] = jnp.zeros_like(dbd_ref)
            dcd_ref[...] = jnp.zeros_like(dcd_ref)
            dab_ref[...] = jnp.zeros_like(dab_ref)
            dd_ref[...] = jnp.zeros_like(dd_ref)

        u = u_ref[...]
        dy = dgy_ref[...] * _gelu_grad(y_ref[...])
        dyb = dy.astype(BF16)
        ar, ai = ab_ref[0, 0:1, :], ab_ref[0, 1:2, :]
        g = jnp.dot(dyb, cdt_ref[0], preferred_element_type=F32)
        rows = lax.broadcasted_iota(jnp.int32, (t, 1), 0)
        cr, ci = carry_ref[0:1, :], carry_ref[1:2, :]
        lr = g[:, :SSM_X_BLK] + jnp.where(rows == t - 1, ar * cr + ai * ci, 0.0)
        li = g[:, SSM_X_BLK:] + jnp.where(rows == t - 1, ar * ci - ai * cr, 0.0)
        lr, li = _scan_chunk(lr, li, ar, -ai, down=False)
        lam = jnp.concatenate([lr, li], axis=1)
        carry_ref[0:1, :] = lr[0:1, :]
        carry_ref[1:2, :] = li[0:1, :]
        lamb = lam.astype(BF16)
        du_ref[...] = (jnp.dot(lamb, bdt_ref[0], preferred_element_type=F32) + ds_ref[...] * dy).astype(BF16)
        dbd_ref[0] += lax.dot_general(u.astype(BF16), lamb, (((0,), (0,)), ((), ())),
                                      preferred_element_type=F32)
        xs = xs_ref[0]
        dcd_ref[0] += lax.dot_general(xs.astype(BF16), dyb, (((0,), (0,)), ((), ())),
                                      preferred_element_type=F32)
        halo = jnp.where(ci_ > 0, halo_ref[0, 7:8, :], 0.0)
        xprev = jnp.where(rows == 0, halo, pltpu.roll(xs, 1, axis=0))
        xpr, xpi = xprev[:, :SSM_X_BLK], xprev[:, SSM_X_BLK:]
        dab_ref[0, 0:1, :] += jnp.sum(lr * xpr + li * xpi, axis=0, keepdims=True)
        dab_ref[0, 1:2, :] += jnp.sum(li * xpr - lr * xpi, axis=0, keepdims=True)
        dd_ref[...] += jnp.sum(dy * u, axis=0, keepdims=True)

    blk = lambda shape: pl.BlockSpec((1,) + shape, lambda j, c: (j, 0, 0))
    rev = lambda j, c: (nc - 1 - c, j)
    ycol = pl.BlockSpec((t, SSM_U_BLK), rev)
    hb = t // 8
    return _hosted_call(
        body, grid=(SSM_SPLIT, nc), comm=comm, sem=("parallel", "arbitrary"), name=name,
        args=(proj, y, dgy, xs, xs, ab, bdt, cdt, dskip), scratch_shapes=[pltpu.VMEM((2, SSM_X_BLK), F32)],
        in_specs=[pl.BlockSpec((t, SSM_U_BLK), lambda j, c: (nc - 1 - c, COL_U + j)), ycol, ycol,
                  pl.BlockSpec((1, t, 2 * SSM_X_BLK), lambda j, c: (j, nc - 1 - c, 0)),
                  pl.BlockSpec((1, 8, 2 * SSM_X_BLK),
                               lambda j, c: (j, jnp.maximum((nc - 1 - c) * hb - 1, 0), 0)),
                  blk((2, SSM_X_BLK)), blk((2 * SSM_X_BLK, SSM_U_BLK)), blk((SSM_U_BLK, 2 * SSM_X_BLK)),
                  pl.BlockSpec((1, SSM_U_BLK), lambda j, c: (0, j))],
        out_specs=[ycol, blk((SSM_U_BLK, 2 * SSM_X_BLK)), blk((2 * SSM_X_BLK, SSM_U_BLK)),
                   blk((2, SSM_X_BLK)), pl.BlockSpec((1, SSM_U_BLK), lambda j, c: (0, j))],
        out_shape=[jax.ShapeDtypeStruct((l, SSM_WIDTH), BF16),
                   jax.ShapeDtypeStruct((SSM_SPLIT, SSM_U_BLK, 2 * SSM_X_BLK), F32),
                   jax.ShapeDtypeStruct((SSM_SPLIT, 2 * SSM_X_BLK, SSM_U_BLK), F32),
                   jax.ShapeDtypeStruct((SSM_SPLIT, 2, SSM_X_BLK), F32),
                   jax.ShapeDtypeStruct((1, SSM_WIDTH), F32)])


def _block_diag(t):
    s, g, a, b = t.shape
    return jnp.einsum('sgab,gk->sgakb', t, jnp.eye(g, dtype=t.dtype)).reshape(s, g * a, g * b)


def _block_diag_take(t, a, b):
    s = t.shape[0]
    return jnp.einsum('sgakb,gk->sgab', t.reshape(s, 8, a, 8, b), jnp.eye(8, dtype=t.dtype))


def _glu_fwd_hook(l, tm):
    def fn(result, ins, outs, i, j):
        z = result()
        outs[0][...] = z
        outs[1][...] = (z[:, :SSM_WIDTH] * _sigmoid(z[:, SSM_WIDTH:])).astype(BF16)

    row = lambda i, j, k: (i, 0)
    return _Hook(fn, outs=[((l, 2 * SSM_WIDTH), F32, (tm, 2 * SSM_WIDTH), row),
                           ((l, SSM_WIDTH), BF16, (tm, SSM_WIDTH), row)])


def _glu_bwd_hook(z, tm):
    l = z.shape[0]

    def fn(result, ins, outs, i, j):
        zv_ref, zg_ref = ins
        dz_ref, db_ref = outs
        d = result()
        sg = _sigmoid(zg_ref[...])
        dv = d * sg
        dg = d * zv_ref[...] * sg * (1.0 - sg)
        dz_ref[:, :SSM_WIDTH] = dv.astype(BF16)
        dz_ref[:, SSM_WIDTH:] = dg.astype(BF16)

        @pl.when(i == 0)
        def _():
            db_ref[...] = jnp.zeros_like(db_ref)

        db_ref[:, :SSM_WIDTH] += jnp.sum(dv, axis=0, keepdims=True)
        db_ref[:, SSM_WIDTH:] += jnp.sum(dg, axis=0, keepdims=True)

    half = (tm, SSM_WIDTH)
    return _Hook(fn, ins=[z, z], in_specs=[(half, lambda i, j, k: (i, 0)), (half, lambda i, j, k: (i, 1))],
                 outs=[((l, 2 * SSM_WIDTH), BF16, (tm, 2 * SSM_WIDTH), lambda i, j, k: (i, 0)),
                       ((1, 2 * SSM_WIDTH), F32, (1, 2 * SSM_WIDTH), lambda i, j, k: (0, 0))])


GATE_TC = 256


def _merge_fwd(proj, a, s, *, name):
    l = a.shape[0]
    tr = min(2048, l)

    def body(ga_ref, gs_ref, a_ref, s_ref, o_ref):
        o_ref[...] = (_sigmoid(ga_ref[...]) * a_ref[...].astype(F32)
                      + _sigmoid(gs_ref[...]) * s_ref[...].astype(F32)).astype(BF16)

    own = pl.BlockSpec((tr, GATE_TC), lambda i, j: (i, j))
    return pl.pallas_call(
        body, grid=(l // tr, D_MODEL // GATE_TC),
        in_specs=[pl.BlockSpec((tr, GATE_TC), lambda i, j: (i, COL_GA // 2 + j)),
                  pl.BlockSpec((tr, GATE_TC), lambda i, j: (i, COL_GS // 2 + j)), own, own],
        out_specs=own, out_shape=jax.ShapeDtypeStruct((l, D_MODEL), BF16), name=name,
        compiler_params=_cp(("parallel", "parallel")))(proj, proj, a, s)


def _merge_bwd_hook(proj, a, s, tm):
    def fn(result, ins, outs, i, j):
        ga_ref, gs_ref, a_br, s_br = ins
        d = result()
        sa = _sigmoid(ga_ref[...])
        ss = _sigmoid(gs_ref[...])
        outs[0][...] = (d * sa).astype(BF16)
        outs[1][...] = (d * ss).astype(BF16)
        outs[2][...] = (d * a_br[...].astype(F32) * sa * (1.0 - sa)).astype(BF16)
        outs[3][...] = (d * s_br[...].astype(F32) * ss * (1.0 - ss)).astype(BF16)

    blk = (tm, GATE_TC)
    own = lambda i, j, k: (i, j)
    return _Hook(fn, ins=[proj, proj, a, s],
                 in_specs=[(blk, lambda i, j, k: (i, COL_GA // 2 + j)), (blk, lambda i, j, k: (i, COL_GS // 2 + j)),
                           (blk, own), (blk, own)],
                 outs=[(a.shape, BF16, blk, own)] * 4)


FF_TC = D_FF // 2
FF_NJ = 2
FF_ROWS = 128


FF_HALO = 16


def _conv_taps(ext, rows):
    h = FF_HALO
    return (ext[h:h + rows], pltpu.roll(ext, 1, axis=0)[h:h + rows], pltpu.roll(ext, 2, axis=0)[h:h + rows])


def _ff_specs(tr, l):
    hb = tr // FF_HALO
    last = l // FF_HALO - 1
    prev = lambda i: jnp.maximum(i * hb - 1, 0)
    nxt = lambda i: jnp.minimum((i + 1) * hb, last)
    return dict(
        own=pl.BlockSpec((tr, FF_TC), lambda j, i: (i, j)),
        own_next=pl.BlockSpec((FF_HALO, FF_TC), lambda j, i: (nxt(i), j)),
        val=pl.BlockSpec((tr, FF_TC), lambda j, i: (i, 2 * j)),
        val_next=pl.BlockSpec((FF_HALO, FF_TC), lambda j, i: (nxt(i), 2 * j)),
        gate=pl.BlockSpec((tr, FF_TC), lambda j, i: (i, 2 * j + 1)),
        gate_prev=pl.BlockSpec((FF_HALO, FF_TC), lambda j, i: (prev(i), 2 * j + 1)),
        gate_next=pl.BlockSpec((FF_HALO, FF_TC), lambda j, i: (nxt(i), 2 * j + 1)),
        pair=pl.BlockSpec((tr, 2 * FF_TC), lambda j, i: (i, j)),
        w=pl.BlockSpec((3, FF_TC), lambda j, i: (0, j)),
        b=pl.BlockSpec((1, FF_TC), lambda j, i: (0, j)))


def _ffn_act_fwd(up, conv_w, conv_b, *, name):
    l = up.shape[0]
    tr = min(FF_ROWS, l)

    def body(v_ref, g_ref, prev_ref, w_ref, b_ref, o_ref):
        prev = jnp.where(pl.program_id(1) == 0, 0.0, prev_ref[...].astype(F32))
        g0, g1, g2 = _conv_taps(jnp.concatenate([prev, g_ref[...].astype(F32)], axis=0), tr)
        gc = b_ref[...] + w_ref[0:1, :] * g2 + w_ref[1:2, :] * g1 + w_ref[2:3, :] * g0
        o_ref[...] = (v_ref[...].astype(F32) * _gelu(gc)).astype(BF16)

    sp = _ff_specs(tr, l)
    return pl.pallas_call(
        body, grid=(FF_NJ, l // tr), in_specs=[sp['val'], sp['gate'], sp['gate_prev'], sp['w'], sp['b']],
        out_specs=sp['own'], out_shape=jax.ShapeDtypeStruct((l, D_FF), BF16), name=name,
        compiler_params=_cp(("parallel", "parallel")))(up, up, up, conv_w, conv_b)


def _ffn_act_bwd(dact, up, conv_w, conv_b, *, comm=None, name):
    l = up.shape[0]
    tr = min(FF_ROWS, l)
    ni = l // tr
    te = tr + 8

    def body(d_ref, dn_ref, v_ref, vn_ref, g_ref, gp_ref, gn_ref, w_ref, b_ref, dup_ref, dw_ref, db_ref):
        i = pl.program_id(1)
        f32 = lambda ref, rows=None: ref[...].astype(F32)[:rows]
        prev = jnp.where(i == 0, 0.0, f32(gp_ref))
        g0, g1, g2 = _conv_taps(jnp.concatenate([prev, f32(g_ref), f32(gn_ref, 8)], axis=0), te)
        w0, w1, w2 = w_ref[0:1, :], w_ref[1:2, :], w_ref[2:3, :]
        gc = b_ref[...] + w0 * g2 + w1 * g1 + w2 * g0
        d_own = f32(d_ref)
        d = jnp.concatenate([d_own, jnp.where(i == ni - 1, 0.0, f32(dn_ref, 8))], axis=0)
        v = jnp.concatenate([f32(v_ref), f32(vn_ref, 8)], axis=0)
        dgc = d * v * _gelu_grad(gc)
        ahead1 = pltpu.roll(dgc, te - 1, axis=0)[:tr]
        ahead2 = pltpu.roll(dgc, te - 2, axis=0)[:tr]
        own = dgc[:tr]
        dup_ref[:, :FF_TC] = (d_own * _gelu(gc[:tr])).astype(BF16)
        dup_ref[:, FF_TC:] = (w2 * own + w1 * ahead1 + w0 * ahead2).astype(BF16)

        @pl.when(i == 0)
        def _():
            dw_ref[...] = jnp.zeros_like(dw_ref)
            db_ref[...] = jnp.zeros_like(db_ref)

        dw_ref[0:1, :] += jnp.sum(own * g2[:tr], axis=0, keepdims=True)
        dw_ref[1:2, :] += jnp.sum(own * g1[:tr], axis=0, keepdims=True)
        dw_ref[2:3, :] += jnp.sum(own * g0[:tr], axis=0, keepdims=True)
        db_ref[...] += jnp.sum(own, axis=0, keepdims=True)

    sp = _ff_specs(tr, l)
    return _hosted_call(
        body, grid=(FF_NJ, ni),
        in_specs=[sp['own'], sp['own_next'], sp['val'], sp['val_next'], sp['gate'], sp['gate_prev'],
                  sp['gate_next'], sp['w'], sp['b']],
        out_specs=[sp['pair'], sp['w'], sp['b']],
        out_shape=[jax.ShapeDtypeStruct((l, 2 * D_FF), BF16), jax.ShapeDtypeStruct((3, D_FF), F32),
                   jax.ShapeDtypeStruct((1, D_FF), F32)],
        scratch_shapes=[], sem=("parallel", "arbitrary"), name=name,
        args=(dact, dact, up, up, up, up, up, conv_w, conv_b), comm=comm)


def _join_dproj(dq, dkc, dkp, dvc, dvp, du, dga, dgs, *, name):
    l = dq.shape[0]
    nb = l // BLOCK
    widths = [ATTN_WIDTH, KV_WIDTH, KV_WIDTH, SSM_WIDTH, D_MODEL, D_MODEL]

    def body(dq_ref, dkc_ref, dkp_ref, dvc_ref, dvp_ref, du_ref, dga_ref, dgs_ref, o_ref, s_ref):
        n = pl.program_id(0)

        @pl.when(n == 0)
        def _():
            s_ref[...] = jnp.zeros_like(s_ref)

        last = n == nb - 1
        dk = (dkc_ref[...] + jnp.where(last, 0.0, dkp_ref[...])).astype(BF16)
        dv = (dvc_ref[...] + jnp.where(last, 0.0, dvp_ref[...])).astype(BF16)
        col = 0
        for v, width in zip([dq_ref[...], dk, dv, du_ref[...], dga_ref[...], dgs_ref[...]], widths):
            o_ref[:, col:col + width] = v
            s_ref[:, col:col + width] += jnp.sum(v.astype(F32), axis=0, keepdims=True)
            col += width

    cur = lambda width: pl.BlockSpec((BLOCK, width), lambda n: (n, 0))
    nxt = pl.BlockSpec((BLOCK, KV_WIDTH), lambda n: (jnp.minimum(n + 1, nb - 1), 0))
    return pl.pallas_call(
        body, grid=(nb,),
        in_specs=[cur(ATTN_WIDTH), cur(KV_WIDTH), nxt, cur(KV_WIDTH), nxt, cur(SSM_WIDTH), cur(D_MODEL), cur(D_MODEL)],
        out_specs=[pl.BlockSpec((BLOCK, IN_COLS), lambda n: (n, 0)), pl.BlockSpec((1, IN_COLS), lambda n: (0, 0))],
        out_shape=[jax.ShapeDtypeStruct((l, IN_COLS), BF16), jax.ShapeDtypeStruct((1, IN_COLS), F32)],
        name=name, compiler_params=_cp(("arbitrary",)))(dq, dkc, dkp, dvc, dvp, du, dga, dgs)


def _local_step(x, target, wts, small, shards=None):
    l = x.shape[0]
    wts = dict(wts)
    grads, recvs, sgr = {}, {}, {}
    lay = lambda keys: [LAYOUT[k] for k in keys]
    none = lambda keys: None
    gather = (lambda keys: _GatherPlan([shards[k] for k in keys], lay(keys))) if shards is not None else none
    scatter = (lambda keys: _ScatterPlan([grads[k] for k in keys], lay(keys))) if shards is not None else none

    mm = _matmul

    def take(res, plan, keys, store):
        outs, couts = res
        if plan is not None:
            store.update(zip(keys, couts))
        return outs

    def mm_plan(plan, keys, store, *args, **kw):
        if plan is None:
            return _matmul(*args, **kw)
        return take(_matmul(*args, comm=plan, **kw), plan, keys, store)

    def mm_host(keys, make_plan, store, *args, **kw):
        return mm_plan(make_plan(keys), keys, store, *args, **kw)

    up_scatter = lambda p: _ScatterPlan([grads['w_up']], lay(['w_up']), part=(p, 2)) if shards is not None else None

    col = lambda t: t.reshape(SSM_GROUPS * SSM_STATE, 1)
    a_re, a_im = col(small['ssm_a_re']), col(small['ssm_a_im'])
    log_dt = jnp.repeat(small['ssm_log_dt'].reshape(SSM_GROUPS), SSM_STATE).reshape(-1, 1)
    b_re = small['ssm_b_re'].reshape(SSM_GROUPS * SSM_STATE, SSM_GROUP)
    b_im = small['ssm_b_im'].reshape(SSM_GROUPS * SSM_STATE, SSM_GROUP)
    ab_re, ab_im, bb_re, bb_im = _ssm_disc_fwd(a_re, a_im, log_dt, b_re, b_im, name="ssm_disc_fwd")
    ab = jnp.stack([ab_re.reshape(SSM_SPLIT, SSM_X_BLK), ab_im.reshape(SSM_SPLIT, SSM_X_BLK)], axis=1)
    to_bd = lambda t: _block_diag(t.reshape(SSM_SPLIT, 8, SSM_STATE, SSM_GROUP).transpose(0, 1, 3, 2))
    bd = jnp.concatenate([to_bd(bb_re), to_bd(bb_im)], axis=2)
    c_re = small['ssm_c_re'].reshape(SSM_SPLIT, 8, SSM_GROUP, SSM_STATE)
    c_im = small['ssm_c_im'].reshape(SSM_SPLIT, 8, SSM_GROUP, SSM_STATE)
    cdt = jnp.concatenate([_block_diag(c_re), -_block_diag(c_im)], axis=2)
    bd_b, cdt_b = bd.astype(BF16), cdt.astype(BF16)
    bdt_b, cd_b = bd_b.transpose(0, 2, 1), cdt_b.transpose(0, 2, 1)
    dskip = small['ssm_d'].reshape(1, SSM_WIDTH)

    sinks = small['attn_sinks'].reshape(N_Q_HEADS)
    plan = gather(['w_in_t'])
    h1, = take(_rms_fwd(x, small['attn_norm_g'], comm=plan, name="rms1_fwd"), plan, ['w_in_t'], wts)
    proj = mm_host(['w_glu', 'w_ba', 'w_bs', 'w_out'], gather, wts,
                   h1, wts['w_in_t'], tb=True, tm=512, tn=2944, tk=2048, inner='m', out_dtype=F32,
                   bias=small['b_in'], name="mm_in")
    attn = _attn_fwd(proj, sinks, name="attn_fwd")
    plan = gather(['w_up'])
    y, gy, xs = take(_ssm_fwd(proj, ab, bd_b, cd_b, dskip, comm=plan, name="ssm_fwd"), plan, ['w_up'], wts)
    z, ssm = mm(gy, wts['w_glu'], tm=1024, tn=1024, tk=512, bias=small['b_glu'],
                epilogue=_glu_fwd_hook(l, min(1024, l)), name="mm_glu")
    a_br = mm(attn, wts['w_ba'], tm=1024, tn=1024, tk=1024, out_dtype=BF16, name="mm_ba")
    s_br = mm(ssm, wts['w_bs'], tm=1024, tn=1024, tk=512, out_dtype=BF16, name="mm_bs")
    tr = min(ROW_TILE, l)
    merged = _merge_fwd(proj, a_br, s_br, name="merge_fwd")
    x2 = mm(merged, wts['w_out'], tm=1024, tn=1024, tk=2048, inner='m', out_dtype=F32, res=x, name="mm_out")
    h2, = take(_rms_fwd(x2, small['ffn_norm_g'], name="rms2_fwd"), None, [], wts)
    up = mm_host(['w_down'], gather, wts,
                 h2, wts['w_up'], tm=1024, tn=1024, tk=2048, out_dtype=BF16, name="mm_up")
    conv_w, conv_b = small['conv_w'], small['conv_b']
    act = _ffn_act_fwd(up, conv_w, conv_b, name="ffn_act_fwd")
    x3 = mm(act, wts['w_down'], tm=1024, tn=1024, tk=2816, out_dtype=F32, res=x2, name="mm_down")
    dx3b, d_g3, loss = _rowwise(
        _final_loss_hook(small['final_norm_g'].reshape(1, D_MODEL), target, tr), x3, tr, name="final_loss")

    sgr['final_norm_g'] = d_g3.reshape(D_MODEL)
    dact = mm(dx3b, wts['w_down'], tb=True, tm=512, tn=2816, tk=2048, inner='m', out_dtype=BF16, name="mm_dact")
    grads['w_down'] = mm(act, dx3b, ta=True, tm=1408, tn=1024, tk=2048, out_dtype=BF16, name="mm_dw_down")
    plan = scatter(['w_down'])
    dup, sgr['conv_w'], sgr['conv_b'] = take(
        _ffn_act_bwd(dact, up, conv_w, conv_b, comm=plan, name="ffn_act_bwd"), plan, ['w_down'], recvs)
    grads['w_up'] = mm(h2, dup, ta=True, tm=2048, tn=1024, tk=2048, out_dtype=BF16, name="mm_dw_up")
    dh2 = mm_plan(up_scatter(0), ['w_up#0'], recvs,
                  dup, wts['w_up'], tb=True, tm=1024, tn=1024, tk=2816, out_dtype=BF16, name="mm_dh2")
    dx2b, sgr['ffn_norm_g'] = _rowwise(
        _rms_bwd_hook(x2, small['ffn_norm_g'], dx3b, tr, BF16), dh2, tr, name="rms2_bwd")

    d_a, d_s, dga, dgs = mm(dx2b, wts['w_out'], tb=True, tm=1024, tn=GATE_TC, tk=2048,
                            epilogue=_merge_bwd_hook(proj, a_br, s_br, min(1024, l)), name="mm_dmerged")
    grads['w_out'] = mm(merged, dx2b, ta=True, tm=1024, tn=1024, tk=2048, out_dtype=BF16, name="mm_dw_out")
    dattn = mm(d_a, wts['w_ba'], tb=True, tm=1024, tn=1024, tk=2048, inner='m', out_dtype=BF16, name="mm_dattn")
    grads['w_ba'] = mm(attn, d_a, ta=True, tm=1024, tn=1024, tk=2048, out_dtype=BF16, name="mm_dw_ba")
    dz, sgr['b_glu'] = mm(d_s, wts['w_bs'], tb=True, tm=1024, tn=512, tk=2048, sequential=True,
                          epilogue=_glu_bwd_hook(z, min(1024, l)), name="mm_dssm")
    grads['w_bs'] = mm(ssm, d_s, ta=True, tm=512, tn=1024, tk=2048, out_dtype=BF16, name="mm_dw_bs")
    grads['w_glu'] = mm(gy, dz, ta=True, tm=512, tn=1024, tk=2048, out_dtype=BF16, name="mm_dw_glu")
    dgy = mm(dz, wts['w_glu'], tb=True, tm=1024, tn=512, tk=1024, inner='m', out_dtype=F32, name="mm_dgy")
    plan = up_scatter(1)
    du, d_bd, d_cd, d_ab, sgr['ssm_d'] = take(
        _ssm_bwd(proj, y, dgy, xs, ab, bdt_b, cdt_b, dskip, comm=plan, name="ssm_bwd"), plan, ['w_up#1'], recvs)
    keys = ['w_out', 'w_ba', 'w_bs', 'w_glu']
    plan = scatter(keys)
    dq, dkc, dkp, dvc, dvp, dsink = take(
        _attn_bwd(proj, sinks, dattn, comm=plan, name="attn_bwd"), plan, keys, recvs)
    sgr['attn_sinks'] = dsink[:, :N_Q_HEADS]
    dproj, sgr['b_in'] = _join_dproj(dq, dkc, dkp, dvc, dvp, du, dga, dgs, name="join_dproj")
    grads['w_in_t'] = mm(dproj, h1, ta=True, tm=2944, tn=1024, tk=1024, out_dtype=BF16, name="mm_dw_in")
    dh1 = mm_host(['w_in_t'], scatter, recvs,
                  dproj, wts['w_in_t'], tm=1024, tn=1024, tk=2944, out_dtype=BF16, name="mm_dh1")
    grad_x, sgr['attn_norm_g'] = _rowwise(
        _rms_bwd_hook(x, small['attn_norm_g'], dx2b, tr, F32), dh1, tr, name="rms1_bwd")

    from_bd = lambda t: _block_diag_take(t, SSM_GROUP, SSM_STATE).transpose(0, 1, 3, 2).reshape(
        SSM_GROUPS * SSM_STATE, SSM_GROUP)
    d_bb_re = from_bd(d_bd[:, :, :SSM_X_BLK])
    d_bb_im = from_bd(d_bd[:, :, SSM_X_BLK:])
    d_cdt = d_cd.transpose(0, 2, 1)
    shape_c = (1, SSM_GROUPS, SSM_GROUP, SSM_STATE)
    sgr['ssm_c_re'] = _block_diag_take(d_cdt[:, :, :SSM_X_BLK], SSM_GROUP, SSM_STATE).reshape(shape_c)
    sgr['ssm_c_im'] = -_block_diag_take(d_cdt[:, :, SSM_X_BLK:], SSM_GROUP, SSM_STATE).reshape(shape_c)
    d_a_re, d_a_im, d_ldt, d_b_re, d_b_im = _ssm_disc_bwd(
        a_re, a_im, log_dt, b_re, b_im, d_ab[:, 0, :].reshape(-1, 1), d_ab[:, 1, :].reshape(-1, 1),
        d_bb_re, d_bb_im, name="ssm_disc_bwd")
    sgr['ssm_a_re'] = d_a_re.reshape(1, SSM_GROUPS, SSM_STATE)
    sgr['ssm_a_im'] = d_a_im.reshape(1, SSM_GROUPS, SSM_STATE)
    sgr['ssm_log_dt'] = d_ldt.reshape(SSM_GROUPS, SSM_STATE).sum(axis=1).reshape(1, SSM_GROUPS)
    sgr['ssm_b_re'] = d_b_re.reshape(1, SSM_GROUPS, SSM_STATE, SSM_GROUP)
    sgr['ssm_b_im'] = d_b_im.reshape(1, SSM_GROUPS, SSM_STATE, SSM_GROUP)
    return loss, grad_x, grads, recvs, sgr


def _swap_cores(arrs, *, name):
    n = len(arrs)

    def body(*refs):
        ins, outs = refs[:n], refs[n:2 * n]
        send_sems, recv_sems = refs[2 * n:]
        x, y, c = _place()
        copies = []
        for i in range(n):
            cp = pltpu.make_async_remote_copy(
                src_ref=ins[i], dst_ref=outs[i], send_sem=send_sems.at[i], recv_sem=recv_sems.at[i],
                device_id=(x, y, 1 - c), device_id_type=MESH)
            cp.start()
            copies.append(cp)
        for cp in copies:
            cp.wait()

    return pl.pallas_call(
        body, in_specs=[ANY] * n, out_specs=[ANY] * n,
        out_shape=[jax.ShapeDtypeStruct(a.shape, a.dtype) for a in arrs],
        scratch_shapes=[pltpu.SemaphoreType.DMA((n,)), pltpu.SemaphoreType.DMA((n,))],
        name=name)(*arrs)


def _all_reduce_small(buf, *, name):
    r = buf.shape[0]

    def body(in_ref, out_ref, slots, send_sems, recv_sems):
        x, y, c = _place()
        me = 4 * x + 2 * y + c
        slots[pl.ds(me, 1)] = in_ref[...][None]
        copies = []
        for k in range(N_DEV - 1):
            bx, by, bc = ((k + 1) >> 2) & 1, ((k + 1) >> 1) & 1, (k + 1) & 1
            peer = (1 - x if bx else x, 1 - y if by else y, 1 - c if bc else c)
            cp = pltpu.make_async_remote_copy(
                src_ref=in_ref, dst_ref=slots.at[me], send_sem=send_sems.at[k], recv_sem=recv_sems.at[k],
                device_id=peer, device_id_type=MESH)
            cp.start()
            copies.append(cp)
        for cp in copies:
            cp.wait()
        acc = slots[0]
        for d in range(1, N_DEV):
            acc = acc + slots[d]
        out_ref[...] = acc

    vm = pl.BlockSpec(memory_space=pltpu.VMEM)
    return pl.pallas_call(
        body, in_specs=[vm], out_specs=vm, out_shape=jax.ShapeDtypeStruct((r, 128), F32),
        scratch_shapes=[pltpu.VMEM((N_DEV, r, 128), F32), pltpu.SemaphoreType.DMA((N_DEV - 1,)),
                        pltpu.SemaphoreType.DMA((N_DEV - 1,))],
        name=name)(buf)


def _pack(arrs):
    flat = jnp.concatenate([a.reshape(-1).astype(F32) for a in arrs])
    pad = (-flat.shape[0]) % 1024
    return jnp.pad(flat, (0, pad)).reshape(-1, 128)


def _unpack(buf, shapes):
    flat = buf.reshape(-1)
    out, pos = [], 0
    for s in shapes:
        size = math.prod(s)
        out.append(flat[pos:pos + size].reshape(s))
        pos += size
    return out


TILE_ELEMS = 256 * 1024


def _tile_rows(r, c):
    if r * c <= TILE_ELEMS:
        return r
    for tr in range(TILE_ELEMS // c // 16 * 16, 0, -16):
        if r % tr == 0:
            return tr
    raise ValueError((r, c))


def _sum4(full, axis, pos, recvs, *, name):
    r, c = _block_shape(full.shape, axis)
    parts = len(recvs)
    tr = _tile_rows(r // parts, c)
    per = r // parts // tr
    nt = r // tr

    def body(pos_ref, o_ref, *refs):
        out_ref = refs[parts]
        for p in range(parts):
            @pl.when(pl.program_id(0) // per == p)
            def _():
                acc = o_ref[...].astype(F32)
                for k in range(3):
                    acc = acc + refs[p][k].astype(F32)
                out_ref[...] = acc.astype(BF16)

    own = (pl.BlockSpec((tr, c), lambda i, pos_ref: (pos_ref[0] * nt + i, 0)) if axis == 0
           else pl.BlockSpec((tr, c), lambda i, pos_ref: (i, pos_ref[0])))
    part_spec = lambda p: pl.BlockSpec((3, tr, c), lambda i, pos_ref: (0, jnp.clip(i - p * per, 0, per - 1), 0))
    grid_spec = pltpu.PrefetchScalarGridSpec(
        num_scalar_prefetch=1, grid=(nt,), in_specs=[own] + [part_spec(p) for p in range(parts)],
        out_specs=pl.BlockSpec((tr, c), lambda i, pos_ref: (i, 0)))
    return pl.pallas_call(
        body, grid_spec=grid_spec, out_shape=jax.ShapeDtypeStruct((r, c), BF16),
        name=name, compiler_params=_cp(("parallel",)))(pos, full, *recvs)


def _adam_step(w, g, m, v):
    bc1 = 1.0 - ADAM_B1 ** ADAM_STEP
    bc2 = 1.0 - ADAM_B2 ** ADAM_STEP
    mn = ADAM_B1 * m + (1.0 - ADAM_B1) * g
    vn = ADAM_B2 * v + (1.0 - ADAM_B2) * (g * g)
    m_hat = mn / bc1
    v_hat = vn / bc2
    return -ADAM_LR * (m_hat / (jnp.sqrt(v_hat) + ADAM_EPS) + ADAM_WD * w), mn, vn


def _adamw(w, ga, gb, m, v, *, name):
    r, c = w.shape
    tr = _tile_rows(r, c)

    def body(w_ref, ga_ref, gb_ref, m_ref, v_ref, g_out, d_out, m_out, v_out):
        g = ga_ref[...].astype(F32) + gb_ref[...].astype(F32)
        g_out[...] = g
        d_out[...], m_out[...], v_out[...] = _adam_step(w_ref[...], g, m_ref[...], v_ref[...])

    spec = pl.BlockSpec((tr, c), lambda i: (i, 0))
    shp = jax.ShapeDtypeStruct((r, c), F32)
    return pl.pallas_call(
        body, grid=(r // tr,), in_specs=[spec] * 5, out_specs=[spec] * 4,
        out_shape=[shp] * 4, name=name, compiler_params=_cp(("parallel",)))(w, ga, gb, m, v)


def _lanes(t):
    return t.reshape(-1, 128) if t.size % 128 == 0 else t.reshape(1, -1)


def _adamw_small(ws, gs, ms, vs, *, name):
    n = len(ws)

    def body(*refs):
        for i in range(n):
            w_ref, g_ref, m_ref, v_ref = (refs[k * n + i] for k in range(4))
            outs = [refs[(4 + k) * n + i] for k in range(3)]
            outs[0][...], outs[1][...], outs[2][...] = _adam_step(w_ref[...], g_ref[...], m_ref[...], v_ref[...])

    flat = [_lanes(t) for group in (ws, gs, ms, vs) for t in group]
    shp = [jax.ShapeDtypeStruct(_lanes(t).shape, F32) for t in ws]
    res = pl.pallas_call(body, out_shape=shp * 3, name=name)(*flat)
    return [[res[k * n + i].reshape(ws[i].shape) for i in range(n)] for k in range(3)]


BIG = ['w_in', 'w_glu', 'w_branch_attn', 'w_branch_ssm', 'w_out', 'w_up', 'w_down']
BIG_KEY = {'w_in': 'w_in_t', 'w_glu': 'w_glu', 'w_branch_attn': 'w_ba', 'w_branch_ssm': 'w_bs',
           'w_out': 'w_out', 'w_up': 'w_up', 'w_down': 'w_down'}
TRANSPOSED = {'w_in'}
SMALL = ['attn_norm_g', 'b_in', 'attn_sinks', 'ssm_a_re', 'ssm_a_im', 'ssm_log_dt', 'ssm_b_re', 'ssm_b_im',
         'ssm_c_re', 'ssm_c_im', 'ssm_d', 'b_glu', 'ffn_norm_g', 'conv_b', 'final_norm_g']
WEIGHTS = ['attn_norm_g', 'w_in', 'b_in', 'attn_sinks', 'ssm_a_re', 'ssm_a_im', 'ssm_log_dt', 'ssm_b_re',
           'ssm_b_im', 'ssm_c_re', 'ssm_c_im', 'ssm_d', 'w_glu', 'b_glu', 'w_branch_attn', 'w_branch_ssm',
           'w_out', 'ffn_norm_g', 'w_up', 'conv_w', 'conv_b', 'w_down', 'final_norm_g']


def _shard_2d(name, t):
    t = t[0]
    return t.T if name in TRANSPOSED else t


def _unshard_2d(name, t):
    return (t.T if name in TRANSPOSED else t)[None]


def kernel(x, attn_norm_g, w_in, b_in, attn_sinks, ssm_a_re, ssm_a_im, ssm_log_dt, ssm_b_re, ssm_b_im, ssm_c_re, ssm_c_im, ssm_d, w_glu, b_glu, w_branch_attn, w_branch_ssm, w_out, ffn_norm_g, w_up, conv_w, conv_b, w_down, final_norm_g, loss_target, m_attn_norm_g, m_w_in, m_b_in, m_attn_sinks, m_ssm_a_re, m_ssm_a_im, m_ssm_log_dt, m_ssm_b_re, m_ssm_b_im, m_ssm_c_re, m_ssm_c_im, m_ssm_d, m_w_glu, m_b_glu, m_w_branch_attn, m_w_branch_ssm, m_w_out, m_ffn_norm_g, m_w_up, m_conv_w, m_conv_b, m_w_down, m_final_norm_g, v_attn_norm_g, v_w_in, v_b_in, v_attn_sinks, v_ssm_a_re, v_ssm_a_im, v_ssm_log_dt, v_ssm_b_re, v_ssm_b_im, v_ssm_c_re, v_ssm_c_im, v_ssm_d, v_w_glu, v_b_glu, v_w_branch_attn, v_w_branch_ssm, v_w_out, v_ffn_norm_g, v_w_up, v_conv_w, v_conv_b, v_w_down, v_final_norm_g):
    args = dict(locals())
    w = {n: args[n] for n in WEIGHTS}
    m = {n: args['m_' + n] for n in WEIGHTS}
    v = {n: args['v_' + n] for n in WEIGHTS}
    xi, yi, ci = _place()
    blk = 2 * xi + yi

    shards = {BIG_KEY[n]: _shard_2d(n, w[n]).astype(BF16) for n in BIG}
    cw_cols = w['conv_w'].shape[2]
    cw_place = lax.dynamic_update_slice(jnp.zeros((3, D_FF), F32), w['conv_w'][0] * (ci == 0).astype(F32),
                                        (0, blk * cw_cols))
    conv_w_full = _unpack(_all_reduce_small(_pack([cw_place]), name="gather_conv_w"), [(3, D_FF)])[0]

    small = {n: w[n] for n in SMALL}
    small['conv_w'] = conv_w_full
    loss_part, grad_x, grads, recvs, sgr = _local_step(x[0], loss_target[0], {}, small, shards)

    halves = []
    for n in BIG:
        key = BIG_KEY[n]
        full = grads[key]
        recv = [recvs[key]] if key in recvs else [recvs[key + '#0'], recvs[key + '#1']]
        axis, interleaved = LAYOUT[key]
        pos = _block_pos(xi, yi, interleaved).astype(jnp.int32).reshape(1)
        halves.append(_sum4(full, axis, pos, recv, name="sum4_" + n))
    others = _swap_cores(halves, name="swap_cores")
    out = {}
    for n, mine, other in zip(BIG, halves, others):
        res = _adamw(_shard_2d(n, w[n]), mine, other, _shard_2d(n, m[n]), _shard_2d(n, v[n]), name="adamw_" + n)
        out[n] = [_unshard_2d(n, t) for t in res]

    names = SMALL + ['conv_w']
    shapes = [w[n].shape for n in SMALL] + [(3, D_FF)]
    packed = _pack([sgr[n] for n in names] + [loss_part])
    summed = _unpack(_all_reduce_small(packed, name="all_reduce_small"), shapes + [(1, 1)])
    loss = summed[-1].reshape(())
    sg = dict(zip(names, summed[:-1]))
    sg['conv_w'] = lax.dynamic_slice_in_dim(sg['conv_w'], blk * cw_cols, cw_cols, axis=1)[None]
    deltas, new_m, new_v = _adamw_small([w[n] for n in names], [sg[n] for n in names], [m[n] for n in names],
                                        [v[n] for n in names], name="adamw_small")
    for i, n in enumerate(names):
        out[n] = [sg[n], deltas[i], new_m[i], new_v[i]]

    return (loss, grad_x[None], *[out[n][0] for n in WEIGHTS], *[out[n][1] for n in WEIGHTS],
            *[out[n][2] for n in WEIGHTS], *[out[n][3] for n in WEIGHTS])
```

```python
import functools
import math

import jax
import jax.numpy as jnp
from jax import lax
from jax.experimental import pallas as pl
from jax.experimental.pallas import tpu as pltpu

F32 = jnp.float32
BF16 = jnp.bfloat16

D_MODEL = 2048
N_Q_HEADS = 16
HEAD_DIM = 64
ATTN_WIDTH = 1024
KV_WIDTH = 128
BLOCK = 128
SSM_WIDTH = 512
SSM_GROUPS = 32
SSM_GROUP = 16
SSM_STATE = 64
D_FF = 5632
IN_COLS = 5888
RMS_EPS = 1e-6
NEG_BIG = -1e30
N_CHIPS = 4
N_DEV = 8

COL_K = 8
COL_V = 9
COL_U = 10
COL_GA = 14
COL_GS = 30

SSM_SPLIT = 4
SSM_U_BLK = 128
SSM_X_BLK = 512
SSM_CHUNK = 256

ADAM_LR = 0.001
ADAM_B1 = 0.9
ADAM_B2 = 0.999
ADAM_EPS = 1e-08
ADAM_WD = 0.01
ADAM_STEP = 10

VMEM_LIMIT_BYTES = 56 * 1024 * 1024
INV_SQRT2 = 1.0 / math.sqrt(2.0)
INV_SQRT2PI = 1.0 / math.sqrt(2.0 * math.pi)
MESH = pl.DeviceIdType.MESH
ANY = pl.BlockSpec(memory_space=pl.ANY)


def _cp(sem):
    return pltpu.CompilerParams(dimension_semantics=sem, vmem_limit_bytes=VMEM_LIMIT_BYTES)


def _gelu(x):
    return 0.5 * x * (1.0 + lax.erf(x * INV_SQRT2))


def _gelu_grad(x):
    return 0.5 * (1.0 + lax.erf(x * INV_SQRT2)) + x * jnp.exp(-0.5 * x * x) * INV_SQRT2PI


def _sigmoid(x):
    return 1.0 / (1.0 + jnp.exp(-x))


def _place():
    return lax.axis_index("x"), lax.axis_index("y"), lax.axis_index("c")


def _other_chips(x, y):
    return [(1 - x, y), (x, 1 - y), (1 - x, 1 - y)]


def _block_pos(x, y, interleaved):
    return x + 2 * y if interleaved else 2 * x + y


LAYOUT = {'w_in_t': (0, False), 'w_glu': (1, False), 'w_ba': (1, False), 'w_bs': (1, False),
          'w_out': (0, False), 'w_up': (1, True), 'w_down': (0, False)}


def _window(ref, axis, pos, size, rows=None):
    if axis == 0:
        start, count = (0, size) if rows is None else rows
        return ref.at[pl.ds(pos * size + start, count), :]
    cols = pl.ds(pos * size, size)
    return ref.at[:, cols] if rows is None else ref.at[pl.ds(rows[0], rows[1]), cols]


def _gathered_shape(shape, axis):
    return tuple(N_CHIPS * d if a == axis else d for a, d in enumerate(shape))


def _block_shape(shape, axis):
    return tuple(d // N_CHIPS if a == axis else d for a, d in enumerate(shape))


class _GatherPlan:
    def __init__(self, shards, layouts):
        self.arrays = list(shards)
        self.layouts = list(layouts)
        n = len(shards)
        self.out_shape = [jax.ShapeDtypeStruct(_gathered_shape(s.shape, lay[0]), s.dtype)
                          for s, lay in zip(shards, layouts)]
        self.scratch = [pltpu.SemaphoreType.DMA((6 * n,)), pltpu.SemaphoreType.DMA((6 * n,)),
                        pltpu.SemaphoreType.DMA((n,))]

    def _copies(self, kind, ins, outs, sems):
        send, recv, local = sems
        n = len(self.arrays)
        x, y, c = _place()
        copies = []
        for i in range(n):
            axis, interleaved = self.layouts[i]
            size = self.arrays[i].shape[axis]
            h = self.arrays[i].shape[0] // 2
            first = lambda core: core * h
            blk = _block_pos(x, y, interleaved)
            if kind == 'mine':
                copies.append(pltpu.make_async_copy(ins[i], _window(outs[i], axis, blk, size), local.at[i]))
                continue
            for k, (px, py) in enumerate(_other_chips(x, y)):
                theirs = _block_pos(px, py, interleaved)
                if kind in ('ici_out', 'ici_in'):
                    route = dict(send_sem=send.at[3 * i + k], recv_sem=recv.at[3 * i + k],
                                 device_id=(px, py, c), device_id_type=MESH)
                else:
                    route = dict(send_sem=send.at[3 * (n + i) + k], recv_sem=recv.at[3 * (n + i) + k],
                                 device_id=(x, y, 1 - c), device_id_type=MESH)
                if kind == 'ici_out':
                    src, dst = ins[i].at[pl.ds(first(c), h), :], _window(outs[i], axis, blk, size, (first(c), h))
                elif kind == 'd2d_in':
                    src = dst = _window(outs[i], axis, theirs, size, (first(1 - c), h))
                else:
                    src = dst = _window(outs[i], axis, theirs, size, (first(c), h))
                copies.append(pltpu.make_async_remote_copy(src_ref=src, dst_ref=dst, **route))
        return copies

    def start(self, ins, outs, sems):
        for cp in self._copies('mine', ins, outs, sems) + self._copies('ici_out', ins, outs, sems):
            cp.start()

    def middle(self, ins, outs, sems):
        for arrived, onward in zip(self._copies('ici_in', ins, outs, sems), self._copies('d2d_out', ins, outs, sems)):
            arrived.wait_recv()
            onward.start()

    def finish(self, ins, outs, sems):
        for cp in self._copies('d2d_in', ins, outs, sems):
            cp.wait_recv()
        for cp in self._copies('ici_out', ins, outs, sems) + self._copies('d2d_out', ins, outs, sems):
            cp.wait_send()
        for cp in self._copies('mine', ins, outs, sems):
            cp.wait()


class _ScatterPlan:
    def __init__(self, fulls, layouts, part=(0, 1)):
        self.arrays = list(fulls)
        self.layouts = list(layouts)
        self.part = part
        n = len(fulls)
        self.out_shape = []
        for f, lay in zip(fulls, layouts):
            rows, cols = _block_shape(f.shape, lay[0])
            self.out_shape.append(jax.ShapeDtypeStruct((3, rows // part[1], cols), f.dtype))
        self.scratch = [pltpu.SemaphoreType.DMA((3 * n,)), pltpu.SemaphoreType.DMA((3 * n,))]

    def _copies(self, ins, outs, sems):
        send, recv = sems
        x, y, c = _place()
        copies = []
        for i in range(len(self.arrays)):
            axis, interleaved = self.layouts[i]
            size = self.arrays[i].shape[axis] // N_CHIPS
            h = _block_shape(self.arrays[i].shape, axis)[0] // self.part[1]
            rows = (self.part[0] * h, h)
            for k, (px, py) in enumerate(_other_chips(x, y)):
                copies.append(pltpu.make_async_remote_copy(
                    src_ref=_window(ins[i], axis, _block_pos(px, py, interleaved), size, rows), dst_ref=outs[i].at[k],
                    send_sem=send.at[3 * i + k], recv_sem=recv.at[3 * i + k],
                    device_id=(px, py, c), device_id_type=MESH))
        return copies

    def start(self, ins, outs, sems):
        for cp in self._copies(ins, outs, sems):
            cp.start()

    def middle(self, ins, outs, sems):
        pass

    def finish(self, ins, outs, sems):
        for cp in self._copies(ins, outs, sems):
            cp.wait()


def _hosted_call(body, *, grid, in_specs, out_specs, out_shape, scratch_shapes, sem, name, args, comm=None,
                 aliases=None):
    aliases = aliases or {}
    if comm is None:
        outs = pl.pallas_call(body, grid=grid, in_specs=in_specs, out_specs=out_specs, out_shape=out_shape,
                              scratch_shapes=scratch_shapes, name=name, input_output_aliases=aliases,
                              compiler_params=_cp(sem))(*args)
        return outs, None
    n_in, n_out, n_scr = len(in_specs), len(out_specs), len(scratch_shapes)
    nc, ns = len(comm.arrays), len(comm.scratch)
    total = math.prod(grid)
    mid = total - max(1, total // 8)

    def wrapped(*refs):
        pos = 0
        ins = refs[pos:pos + n_in]; pos += n_in
        cins = refs[pos:pos + nc]; pos += nc
        outs = refs[pos:pos + n_out]; pos += n_out
        couts = refs[pos:pos + nc]; pos += nc
        scr = refs[pos:pos + n_scr]; pos += n_scr
        sems = refs[pos:pos + ns]
        step = 0
        for ax, g in enumerate(grid):
            step = step * g + pl.program_id(ax)

        @pl.when(step == 0)
        def _():
            comm.start(cins, couts, sems)

        body(*ins, *outs, *scr)

        @pl.when(step == mid)
        def _():
            comm.middle(cins, couts, sems)

        @pl.when(step == total - 1)
        def _():
            comm.finish(cins, couts, sems)

    res = pl.pallas_call(
        wrapped, grid=grid, in_specs=list(in_specs) + [ANY] * nc, out_specs=list(out_specs) + [ANY] * nc,
        out_shape=list(out_shape) + list(comm.out_shape), scratch_shapes=list(scratch_shapes) + list(comm.scratch),
        name=name, input_output_aliases=aliases,
        compiler_params=_cp(("arbitrary",) * len(grid)))(*args, *comm.arrays)
    return res[:n_out], res[n_out:]


class _Hook:
    def __init__(self, fn, ins=(), in_specs=(), outs=()):
        self.fn, self.ins, self.in_specs, self.outs = fn, list(ins), list(in_specs), list(outs)


def _matmul(a, b, *, ta=False, tb=False, tm, tn, tk, out_dtype=None, bias=None, res=None, inner='n',
            comm=None, prologue=None, epilogue=None, a_shape=None, sequential=False, name):
    if a is None:
        m, kdim = a_shape
    elif ta:
        kdim, m = a.shape
    else:
        m, kdim = a.shape
    if tb:
        n, k2 = b.shape
    else:
        k2, n = b.shape
    assert kdim == k2, (name, kdim, b.shape)
    tm, tn, tk = min(tm, m), min(tn, n), min(tk, kdim)
    assert m % tm == 0 and n % tn == 0 and kdim % tk == 0, (name, m, n, kdim, tm, tn, tk)
    nk = kdim // tk
    dn = (((0 if ta else 1,), (1 if tb else 0,)), ((), ()))
    hooks = [h for h in (prologue, epilogue) if h is not None]
    n_pro_in = len(prologue.ins) if prologue else 0
    n_epi_in = len(epilogue.ins) if epilogue else 0
    n_pro_out = len(prologue.outs) if prologue else 0
    n_epi_out = len(epilogue.outs) if epilogue else 0
    if inner == 'n':
        grid = (m // tm, n // tn, nk)
        mi = lambda g0, g1: g0
        ni = lambda g0, g1: g1
    else:
        grid = (n // tn, m // tm, nk)
        mi = lambda g0, g1: g1
        ni = lambda g0, g1: g0

    def body(*refs):
        refs = list(refs)
        take = lambda cnt: [refs.pop(0) for _ in range(cnt)]
        a_ref = take(1)[0] if a is not None else None
        b_ref = take(1)[0]
        bias_ref = take(1)[0] if bias is not None else None
        res_ref = take(1)[0] if res is not None else None
        pro_in, epi_in = take(n_pro_in), take(n_epi_in)
        o_ref = take(1)[0] if epilogue is None else None
        pro_out, epi_out = take(n_pro_out), take(n_epi_out)
        i, j, k = mi(pl.program_id(0), pl.program_id(1)), ni(pl.program_id(0), pl.program_id(1)), pl.program_id(2)

        def finish(src):
            def result(rows=slice(None)):
                r = src[rows, :]
                if bias_ref is not None:
                    r = r + bias_ref[...]
                if res_ref is not None:
                    r = r + res_ref[rows, :]
                return r

            if epilogue is None:
                o_ref[...] = result().astype(out_dtype)
            else:
                epilogue.fn(result, epi_in, epi_out, i, j)

        a_val = a_ref[...] if prologue is None else prologue.fn(a_ref, pro_in, pro_out, i, k)
        prod = lax.dot_general(a_val.astype(BF16), b_ref[...].astype(BF16), dn, preferred_element_type=F32)
        if nk == 1:
            finish(prod)
            return
        acc_ref = refs[0]

        @pl.when(k == 0)
        def _():
            acc_ref[...] = prod

        @pl.when(k > 0)
        def _():
            acc_ref[...] += prod

        @pl.when(k == nk - 1)
        def _():
            finish(acc_ref)

    spec = lambda shape, fn: pl.BlockSpec(shape, lambda g0, g1, k: fn(mi(g0, g1), ni(g0, g1), k))
    in_specs, args = [], []
    if a is not None:
        in_specs.append(spec((tk, tm), lambda i, j, k: (k, i)) if ta else spec((tm, tk), lambda i, j, k: (i, k)))
        args.append(a)
    in_specs.append(spec((tn, tk), lambda i, j, k: (j, k)) if tb else spec((tk, tn), lambda i, j, k: (k, j)))
    args.append(b)
    if bias is not None:
        in_specs.append(spec((1, tn), lambda i, j, k: (0, j)))
        args.append(bias)
    if res is not None:
        in_specs.append(spec((tm, tn), lambda i, j, k: (i, j)))
        args.append(res)
    for h in hooks:
        in_specs += [spec(shape, fn) for shape, fn in h.in_specs]
        args += h.ins
    out_specs, out_shape = [], []
    if epilogue is None:
        out_specs.append(spec((tm, tn), lambda i, j, k: (i, j)))
        out_shape.append(jax.ShapeDtypeStruct((m, n), out_dtype))
    for h in hooks:
        out_specs += [spec(blk, fn) for _, _, blk, fn in h.outs]
        out_shape += [jax.ShapeDtypeStruct(shape, dtype) for shape, dtype, _, _ in h.outs]
    outs, couts = _hosted_call(
        body, grid=grid, in_specs=in_specs, out_specs=out_specs, out_shape=out_shape,
        scratch_shapes=[pltpu.VMEM((tm, tn), F32)] if nk > 1 else [],
        sem=("arbitrary",) * 3 if sequential else ("parallel", "parallel", "arbitrary"),
        name=name, args=args, comm=comm)
    outs = outs[0] if not hooks else outs
    return outs if comm is None else (outs, couts)


def _rms_fwd(x, g, *, comm=None, name):
    l, d = x.shape
    tr = min(256, l)

    def body(x_ref, g_ref, h_ref):
        xf = x_ref[...]
        r = lax.rsqrt(jnp.mean(xf * xf, axis=-1, keepdims=True) + RMS_EPS)
        h_ref[...] = ((xf * r) * g_ref[...]).astype(BF16)

    row = pl.BlockSpec((tr, d), lambda i: (i, 0))
    return _hosted_call(
        body, grid=(l // tr,), in_specs=[row, pl.BlockSpec((1, d), lambda i: (0, 0))],
        out_specs=[row], out_shape=[jax.ShapeDtypeStruct((l, d), BF16)], scratch_shapes=[],
        sem=("parallel",), name=name, args=(x, g), comm=comm)


EPI_ROWS = 128
ROW_TILE = 256


def _row_chunks(tm):
    ch = min(EPI_ROWS, tm)
    return [slice(c * ch, (c + 1) * ch) for c in range(tm // ch)]


def _rowwise(hook, src, tm, *, name):
    l, d = src.shape
    n_in = len(hook.ins)

    def body(*refs):
        src_ref, ins, outs = refs[0], refs[1:1 + n_in], refs[1 + n_in:]
        hook.fn(lambda rows=slice(None): src_ref[rows, :], ins, outs, pl.program_id(0), 0)

    spec = lambda shape, fn: pl.BlockSpec(shape, lambda i: fn(i, 0, 0))
    return pl.pallas_call(
        body, grid=(l // tm,),
        in_specs=[pl.BlockSpec((tm, d), lambda i: (i, 0))] + [spec(shape, fn) for shape, fn in hook.in_specs],
        out_specs=[spec(blk, fn) for _, _, blk, fn in hook.outs],
        out_shape=[jax.ShapeDtypeStruct(shape, dtype) for shape, dtype, _, _ in hook.outs],
        name=name, compiler_params=_cp(("arbitrary",)))(src, *hook.ins)


def _rms_bwd_hook(x, g, dres, tm, out_dtype):
    def fn(result, ins, outs, i, j):
        x_ref, g_ref, dres_ref = ins
        dx_ref, dg_ref = outs

        @pl.when(i == 0)
        def _():
            dg_ref[...] = jnp.zeros_like(dg_ref)

        for rows in _row_chunks(tm):
            dyv = result(rows).astype(F32)
            xf = x_ref[rows, :]
            r = lax.rsqrt(jnp.mean(xf * xf, axis=-1, keepdims=True) + RMS_EPS)
            xhat = xf * r
            dxh = dyv * g_ref[...]
            dx = r * (dxh - xhat * jnp.mean(dxh * xhat, axis=-1, keepdims=True)) + dres_ref[rows, :].astype(F32)
            dx_ref[rows, :] = dx.astype(out_dtype)
            dg_ref[...] += jnp.sum(dyv * xhat, axis=0, keepdims=True)

    l, d = x.shape
    row = lambda i, j, k: (i, 0)
    vec = lambda i, j, k: (0, 0)
    return _Hook(fn, ins=[x, g, dres], in_specs=[((tm, d), row), ((1, d), vec), ((tm, d), row)],
                 outs=[((l, d), out_dtype, (tm, d), row), ((1, d), F32, (1, d), vec)])


def _final_loss_hook(g, target, tm):
    l, d = target.shape

    def fn(result, ins, outs, i, j):
        g_ref, t_ref = ins
        dxb_ref, dg_ref, loss_ref = outs
        gv = g_ref[...]

        @pl.when(i == 0)
        def _():
            dg_ref[...] = jnp.zeros_like(dg_ref)
            loss_ref[...] = jnp.zeros_like(loss_ref)

        for rows in _row_chunks(tm):
            xf = result(rows)
            r = lax.rsqrt(jnp.mean(xf * xf, axis=-1, keepdims=True) + RMS_EPS)
            xhat = xf * r
            diff = xhat * gv - t_ref[rows, :]
            dout = diff * (1.0 / d)
            dxh = dout * gv
            dx = r * (dxh - xhat * jnp.mean(dxh * xhat, axis=-1, keepdims=True))
            dxb_ref[rows, :] = dx.astype(BF16)
            dg_ref[...] += jnp.sum(dout * xhat, axis=0, keepdims=True)
            part = jnp.sum(jnp.mean(diff * diff, axis=-1, keepdims=True), axis=0, keepdims=True)
            loss_ref[...] += 0.5 * part

    row = lambda i, j, k: (i, 0)
    vec = lambda i, j, k: (0, 0)
    return _Hook(fn, ins=[g, target], in_specs=[((1, d), vec), ((tm, d), row)],
                 outs=[((l, d), BF16, (tm, d), row), ((1, d), F32, (1, d), vec), ((1, 1), F32, (1, 1), vec)])


Q_PER_KV = 8
GROUP_ROWS = Q_PER_KV * BLOCK


def _attn_masks(n, rows=GROUP_ROWS):
    q_idx = lax.broadcasted_iota(jnp.int32, (rows, 2 * BLOCK), 0) & (BLOCK - 1)
    s_idx = lax.broadcasted_iota(jnp.int32, (rows, 2 * BLOCK), 1)
    dist = q_idx + BLOCK - s_idx
    valid = (dist >= 0) & (dist < BLOCK) & ((n > 0) | (s_idx >= BLOCK))
    return dist.astype(F32), valid


def _dup_half(t, kv_head, lo):
    rolled = pltpu.roll(t, HEAD_DIM, axis=1)
    return jnp.where(lo, t, rolled) if kv_head == 0 else jnp.where(lo, rolled, t)


def _stack_heads(ref, kv_head, lo):
    pieces = []
    for r in range(Q_PER_KV):
        pair = kv_head * 4 + r // 2
        t = ref[:, pair * 128:(pair + 1) * 128].astype(BF16)
        sel = lo if r % 2 == 0 else jnp.logical_not(lo)
        pieces.append(jnp.where(sel, t, jnp.zeros_like(t)))
    return jnp.concatenate(pieces, axis=0)


def _unstack_heads(t, lo):
    return [jnp.where(lo, t[(2 * i) * BLOCK:(2 * i + 1) * BLOCK], t[(2 * i + 1) * BLOCK:(2 * i + 2) * BLOCK])
            for i in range(Q_PER_KV // 2)]


def _per_head_column(values):
    return jnp.concatenate([jnp.full((BLOCK, 1), v, F32) for v in values], axis=0)


def _group_probs(qm, kdup, dist, valid, sink_ref, kv_head):
    heads = [kv_head * Q_PER_KV + r for r in range(Q_PER_KV)]
    slope = _per_head_column([2.0 ** (-8.0 * (h + 1) / N_Q_HEADS) for h in heads])
    sink = _per_head_column([sink_ref[h] for h in heads])
    return _probs(qm, kdup, dist, valid, sink, slope)


def _probs(qm, kdup, dist, valid, sink, slope):
    s = lax.dot_general(qm, kdup, (((1,), (1,)), ((), ())), preferred_element_type=F32)
    s = s * (HEAD_DIM ** -0.5) - slope * dist
    s = jnp.where(valid, s, NEG_BIG)
    m = jnp.maximum(jnp.max(s, axis=-1, keepdims=True), sink)
    p = jnp.exp(s - m)
    esink = jnp.exp(sink - m)
    inv = 1.0 / (jnp.sum(p, axis=-1, keepdims=True) + esink)
    return p * inv, esink * inv


def _attn_fwd(proj, sinks, *, name):
    l = proj.shape[0]
    nb = l // BLOCK

    def body(sink_ref, q_ref, kc_ref, kp_ref, vc_ref, vp_ref, o_ref):
        n = pl.program_id(0)
        dist, valid = _attn_masks(n, BLOCK)
        lo = lax.broadcasted_iota(jnp.int32, (1, BLOCK), 1) < HEAD_DIM
        kx = jnp.concatenate([kp_ref[...], kc_ref[...]], axis=0).astype(BF16)
        vx = jnp.concatenate([vp_ref[...], vc_ref[...]], axis=0).astype(BF16)
        for kv_head in range(2):
            kdup = _dup_half(kx, kv_head, lo)
            vdup = _dup_half(vx, kv_head, lo)
            for pr in range(4):
                pair = kv_head * 4 + pr
                qp = q_ref[:, pair * 128:(pair + 1) * 128].astype(BF16)
                o_pair = jnp.zeros((BLOCK, 128), F32)
                for half in range(2):
                    head = 2 * pair + half
                    sel = lo if half == 0 else jnp.logical_not(lo)
                    qm = jnp.where(sel, qp, jnp.zeros_like(qp))
                    p, _ = _probs(qm, kdup, dist, valid, sink_ref[head], 2.0 ** (-8.0 * (head + 1) / N_Q_HEADS))
                    o = jnp.dot(p.astype(BF16), vdup, preferred_element_type=F32)
                    o_pair = o_pair + jnp.where(sel, o, 0.0)
                o_ref[:, pair * 128:(pair + 1) * 128] = o_pair.astype(BF16)

    kv = lambda col, prev: pl.BlockSpec(
        (BLOCK, KV_WIDTH), (lambda n: (jnp.maximum(n - 1, 0), col)) if prev else (lambda n: (n, col)))
    return pl.pallas_call(
        body, grid=(nb,),
        in_specs=[pl.BlockSpec(memory_space=pltpu.SMEM),
                  pl.BlockSpec((BLOCK, ATTN_WIDTH), lambda n: (n, 0)),
                  kv(COL_K, False), kv(COL_K, True), kv(COL_V, False), kv(COL_V, True)],
        out_specs=pl.BlockSpec((BLOCK, ATTN_WIDTH), lambda n: (n, 0)),
        out_shape=jax.ShapeDtypeStruct((l, ATTN_WIDTH), BF16), name=name,
        compiler_params=_cp(("parallel",)))(sinks, proj, proj, proj, proj, proj)


def _attn_bwd(proj, sinks, dattn, *, comm=None, name):
    l = proj.shape[0]
    nb = l // BLOCK

    def body(sink_ref, q_ref, kc_ref, kp_ref, vc_ref, vp_ref, do_ref,
             dq_ref, dkc_ref, dkp_ref, dvc_ref, dvp_ref, dsink_ref):
        n = pl.program_id(0)
        dist, valid = _attn_masks(n)
        lane = lax.broadcasted_iota(jnp.int32, (1, BLOCK), 1)
        lo = lane < HEAD_DIM
        kx = jnp.concatenate([kp_ref[...], kc_ref[...]], axis=0).astype(BF16)
        vx = jnp.concatenate([vp_ref[...], vc_ref[...]], axis=0).astype(BF16)
        dsink = jnp.zeros((1, BLOCK), F32)
        dk_heads, dv_heads = [], []
        for kv_head in range(2):
            kdup = _dup_half(kx, kv_head, lo)
            vdup = _dup_half(vx, kv_head, lo)
            qm = _stack_heads(q_ref, kv_head, lo)
            dom = _stack_heads(do_ref, kv_head, lo)
            p, psink = _group_probs(qm, kdup, dist, valid, sink_ref, kv_head)
            dp = lax.dot_general(dom, vdup, (((1,), (1,)), ((), ())), preferred_element_type=F32)
            delta = jnp.sum(p * dp, axis=-1, keepdims=True)
            ds = (p * (dp - delta) * (HEAD_DIM ** -0.5)).astype(BF16)
            dsink_rows = -psink * delta
            for r in range(Q_PER_KV):
                part = jnp.sum(dsink_rows[r * BLOCK:(r + 1) * BLOCK])
                dsink = dsink + jnp.where(lane == kv_head * Q_PER_KV + r, part, 0.0)
            dq = jnp.dot(ds, kdup, preferred_element_type=F32)
            for i, dq_pair in enumerate(_unstack_heads(dq, lo)):
                pair = kv_head * 4 + i
                dq_ref[:, pair * 128:(pair + 1) * 128] = dq_pair.astype(BF16)
            dk_acc = lax.dot_general(ds, qm, (((0,), (0,)), ((), ())), preferred_element_type=F32)
            dv_acc = lax.dot_general(p.astype(BF16), dom, (((0,), (0,)), ((), ())), preferred_element_type=F32)
            dk_heads.append(dk_acc + pltpu.roll(dk_acc, HEAD_DIM, axis=1))
            dv_heads.append(dv_acc + pltpu.roll(dv_acc, HEAD_DIM, axis=1))
        dk = jnp.where(lo, dk_heads[0], dk_heads[1])
        dv = jnp.where(lo, dv_heads[0], dv_heads[1])
        dkp_ref[...] = dk[:BLOCK]
        dkc_ref[...] = dk[BLOCK:]
        dvp_ref[...] = dv[:BLOCK]
        dvc_ref[...] = dv[BLOCK:]

        @pl.when(n == 0)
        def _():
            dsink_ref[...] = jnp.zeros_like(dsink_ref)

        dsink_ref[...] += dsink

    kv = lambda col, prev: pl.BlockSpec(
        (BLOCK, KV_WIDTH), (lambda n: (jnp.maximum(n - 1, 0), col)) if prev else (lambda n: (n, col)))
    qspec = pl.BlockSpec((BLOCK, ATTN_WIDTH), lambda n: (n, 0))
    kvout = pl.BlockSpec((BLOCK, KV_WIDTH), lambda n: (n, 0))
    kvshape = jax.ShapeDtypeStruct((l, KV_WIDTH), F32)
    return _hosted_call(
        body, grid=(nb,),
        in_specs=[pl.BlockSpec(memory_space=pltpu.SMEM), qspec,
                  kv(COL_K, False), kv(COL_K, True), kv(COL_V, False), kv(COL_V, True), qspec],
        out_specs=[qspec, kvout, kvout, kvout, kvout, pl.BlockSpec((1, BLOCK), lambda n: (0, 0))],
        out_shape=[jax.ShapeDtypeStruct((l, ATTN_WIDTH), BF16), kvshape, kvshape, kvshape, kvshape,
                   jax.ShapeDtypeStruct((1, BLOCK), F32)],
        scratch_shapes=[], sem=("arbitrary",), name=name,
        args=(sinks, proj, proj, proj, proj, proj, dattn), comm=comm)


def _discretize(a_re, a_im, log_dt, b_re, b_im):
    dt = jnp.exp(log_dt)
    mag = jnp.exp(a_re * dt)
    ab_re = mag * jnp.cos(a_im * dt)
    ab_im = mag * jnp.sin(a_im * dt)
    nr = ab_re - 1.0
    ni = ab_im
    den = a_re * a_re + a_im * a_im
    z_re = (nr * a_re + ni * a_im) / den
    z_im = (ni * a_re - nr * a_im) / den
    bb_re = z_re * b_re - z_im * b_im
    bb_im = z_re * b_im + z_im * b_re
    return ab_re, ab_im, bb_re, bb_im


def _ssm_disc_fwd(a_re, a_im, log_dt, b_re, b_im, *, name):
    def body(ar, ai, ld, br, bi, o_ar, o_ai, o_br, o_bi):
        r = _discretize(ar[...], ai[...], ld[...], br[...], bi[...])
        o_ar[...], o_ai[...], o_br[...], o_bi[...] = r

    col = jax.ShapeDtypeStruct(a_re.shape, F32)
    mat = jax.ShapeDtypeStruct(b_re.shape, F32)
    return pl.pallas_call(body, out_shape=[col, col, mat, mat], name=name)(a_re, a_im, log_dt, b_re, b_im)


def _ssm_disc_bwd(a_re, a_im, log_dt, b_re, b_im, d_ab_re, d_ab_im, d_bb_re, d_bb_im, *, name):
    def body(ar, ai, ld, br, bi, g0, g1, g2, g3, o_ar, o_ai, o_ld, o_br, o_bi):
        _, vjp = jax.vjp(_discretize, ar[...], ai[...], ld[...], br[...], bi[...])
        r = vjp((g0[...], g1[...], g2[...], g3[...]))
        o_ar[...], o_ai[...], o_ld[...], o_br[...], o_bi[...] = r

    col = jax.ShapeDtypeStruct(a_re.shape, F32)
    mat = jax.ShapeDtypeStruct(b_re.shape, F32)
    return pl.pallas_call(body, out_shape=[col, col, col, mat, mat], name=name)(
        a_re, a_im, log_dt, b_re, b_im, d_ab_re, d_ab_im, d_bb_re, d_bb_im)


def _shift_rows(x, d, rows, *, down):
    t = x.shape[0]
    if down:
        return jnp.where(rows >= d, pltpu.roll(x, d, axis=0), 0.0)
    return jnp.where(rows < t - d, pltpu.roll(x, t - d, axis=0), 0.0)


def _scan_chunk(xr, xi, ar, ai, *, down):
    t = xr.shape[0]
    rows = lax.broadcasted_iota(jnp.int32, (t, 1), 0)
    pr, pi = ar, ai
    d = 1
    while d < t:
        sr = _shift_rows(xr, d, rows, down=down)
        si = _shift_rows(xi, d, rows, down=down)
        xr, xi = xr + pr * sr - pi * si, xi + pr * si + pi * sr
        pr, pi = pr * pr - pi * pi, 2.0 * pr * pi
        d *= 2
    return xr, xi


def _ssm_fwd(proj, ab, bd, cd, dskip, *, comm=None, name):
    l = proj.shape[0]
    t = min(SSM_CHUNK, l)
    nc = l // t

    def body(u_ref, ab_ref, bd_ref, cd_ref, ds_ref, y_ref, gy_ref, xs_ref, carry_ref):
        c = pl.program_id(1)

        @pl.when(c == 0)
        def _():
            carry_ref[...] = jnp.zeros_like(carry_ref)

        u = u_ref[...]
        ar, ai = ab_ref[0, 0:1, :], ab_ref[0, 1:2, :]
        bu = jnp.dot(u.astype(BF16), bd_ref[0], preferred_element_type=F32)
        rows = lax.broadcasted_iota(jnp.int32, (t, 1), 0)
        cr, ci = carry_ref[0:1, :], carry_ref[1:2, :]
        xr = bu[:, :SSM_X_BLK] + jnp.where(rows == 0, ar * cr - ai * ci, 0.0)
        xi = bu[:, SSM_X_BLK:] + jnp.where(rows == 0, ar * ci + ai * cr, 0.0)
        xr, xi = _scan_chunk(xr, xi, ar, ai, down=True)
        xs_ref[0, :, :SSM_X_BLK] = xr
        xs_ref[0, :, SSM_X_BLK:] = xi
        carry_ref[0:1, :] = xs_ref[0, t - 1:t, :SSM_X_BLK]
        carry_ref[1:2, :] = xs_ref[0, t - 1:t, SSM_X_BLK:]
        y = jnp.dot(xs_ref[0].astype(BF16), cd_ref[0], preferred_element_type=F32) + ds_ref[...] * u
        y_ref[...] = y
        gy_ref[...] = _gelu(y).astype(BF16)

    blk = lambda shape: pl.BlockSpec((1,) + shape, lambda j, c: (j, 0, 0))
    ycol = pl.BlockSpec((t, SSM_U_BLK), lambda j, c: (c, j))
    return _hosted_call(
        body, grid=(SSM_SPLIT, nc),
        in_specs=[pl.BlockSpec((t, SSM_U_BLK), lambda j, c: (c, COL_U + j)),
                  blk((2, SSM_X_BLK)), blk((SSM_U_BLK, 2 * SSM_X_BLK)), blk((2 * SSM_X_BLK, SSM_U_BLK)),
                  pl.BlockSpec((1, SSM_U_BLK), lambda j, c: (0, j))],
        out_specs=[ycol, ycol, pl.BlockSpec((1, t, 2 * SSM_X_BLK), lambda j, c: (j, c, 0))],
        out_shape=[jax.ShapeDtypeStruct((l, SSM_WIDTH), F32), jax.ShapeDtypeStruct((l, SSM_WIDTH), BF16),
                   jax.ShapeDtypeStruct((SSM_SPLIT, l, 2 * SSM_X_BLK), F32)],
        scratch_shapes=[pltpu.VMEM((2, SSM_X_BLK), F32)], sem=("parallel", "arbitrary"), name=name,
        args=(proj, ab, bd, cd, dskip), comm=comm)


def _ssm_bwd(proj, y, dgy, xs, ab, bdt, cdt, dskip, *, comm=None, name):
    l = proj.shape[0]
    t = min(SSM_CHUNK, l)
    nc = l // t

    def body(u_ref, y_ref, dgy_ref, xs_ref, halo_ref, ab_ref, bdt_ref, cdt_ref, ds_ref,
             du_ref, dbd_ref, dcd_ref, dab_ref, dd_ref, carry_ref):
        c = pl.program_id(1)
        ci_ = nc - 1 - c

        @pl.when(c == 0)
        def _():
            carry_ref[...] = jnp.zeros_like(carry_ref)
            dbd_ref[...] = jnp.zeros_like(dbd_ref)
            dcd_ref[...] = jnp.zeros_like(dcd_ref)
            dab_ref[...] = jnp.zeros_like(dab_ref)
            dd_ref[...] = jnp.zeros_like(dd_ref)

        u = u_ref[...]
        dy = dgy_ref[...] * _gelu_grad(y_ref[...])
        dyb = dy.astype(BF16)
        ar, ai = ab_ref[0, 0:1, :], ab_ref[0, 1:2, :]
        g = jnp.dot(dyb, cdt_ref[0], preferred_element_type=F32)
        rows = lax.broadcasted_iota(jnp.int32, (t, 1), 0)
        cr, ci = carry_ref[0:1, :], carry_ref[1:2, :]
        lr = g[:, :SSM_X_BLK] + jnp.where(rows == t - 1, ar * cr + ai * ci, 0.0)
        li = g[:, SSM_X_BLK:] + jnp.where(rows == t - 1, ar * ci - ai * cr, 0.0)
        lr, li = _scan_chunk(lr, li, ar, -ai, down=False)
        lam = jnp.concatenate([lr, li], axis=1)
        carry_ref[0:1, :] = lr[0:1, :]
        carry_ref[1:2, :] = li[0:1, :]
        lamb = lam.astype(BF16)
        du_ref[...] = (jnp.dot(lamb, bdt_ref[0], preferred_element_type=F32) + ds_ref[...] * dy).astype(BF16)
        dbd_ref[0] += lax.dot_general(u.astype(BF16), lamb, (((0,), (0,)), ((), ())),
                                      preferred_element_type=F32)
        xs = xs_ref[0]
        dcd_ref[0] += lax.dot_general(xs.astype(BF16), dyb, (((0,), (0,)), ((), ())),
                                      preferred_element_type=F32)
        halo = jnp.where(ci_ > 0, halo_ref[0, 7:8, :], 0.0)
        xprev = jnp.where(rows == 0, halo, pltpu.roll(xs, 1, axis=0))
        xpr, xpi = xprev[:, :SSM_X_BLK], xprev[:, SSM_X_BLK:]
        dab_ref[0, 0:1, :] += jnp.sum(lr * xpr + li * xpi, axis=0, keepdims=True)
        dab_ref[0, 1:2, :] += jnp.sum(li * xpr - lr * xpi, axis=0, keepdims=True)
        dd_ref[...] += jnp.sum(dy * u, axis=0, keepdims=True)

    blk = lambda shape: pl.BlockSpec((1,) + shape, lambda j, c: (j, 0, 0))
    rev = lambda j, c: (nc - 1 - c, j)
    ycol = pl.BlockSpec((t, SSM_U_BLK), rev)
    hb = t // 8
    return _hosted_call(
        body, grid=(SSM_SPLIT, nc), comm=comm, sem=("parallel", "arbitrary"), name=name,
        args=(proj, y, dgy, xs, xs, ab, bdt, cdt, dskip), scratch_shapes=[pltpu.VMEM((2, SSM_X_BLK), F32)],
        in_specs=[pl.BlockSpec((t, SSM_U_BLK), lambda j, c: (nc - 1 - c, COL_U + j)), ycol, ycol,
                  pl.BlockSpec((1, t, 2 * SSM_X_BLK), lambda j, c: (j, nc - 1 - c, 0)),
                  pl.BlockSpec((1, 8, 2 * SSM_X_BLK),
                               lambda j, c: (j, jnp.maximum((nc - 1 - c) * hb - 1, 0), 0)),
                  blk((2, SSM_X_BLK)), blk((2 * SSM_X_BLK, SSM_U_BLK)), blk((SSM_U_BLK, 2 * SSM_X_BLK)),
                  pl.BlockSpec((1, SSM_U_BLK), lambda j, c: (0, j))],
        out_specs=[ycol, blk((SSM_U_BLK, 2 * SSM_X_BLK)), blk((2 * SSM_X_BLK, SSM_U_BLK)),
                   blk((2, SSM_X_BLK)), pl.BlockSpec((1, SSM_U_BLK), lambda j, c: (0, j))],
        out_shape=[jax.ShapeDtypeStruct((l, SSM_WIDTH), BF16),
                   jax.ShapeDtypeStruct((SSM_SPLIT, SSM_U_BLK, 2 * SSM_X_BLK), F32),
                   jax.ShapeDtypeStruct((SSM_SPLIT, 2 * SSM_X_BLK, SSM_U_BLK), F32),
                   jax.ShapeDtypeStruct((SSM_SPLIT, 2, SSM_X_BLK), F32),
                   jax.ShapeDtypeStruct((1, SSM_WIDTH), F32)])


def _block_diag(t):
    s, g, a, b = t.shape
    return jnp.einsum('sgab,gk->sgakb', t, jnp.eye(g, dtype=t.dtype)).reshape(s, g * a, g * b)


def _block_diag_take(t, a, b):
    s = t.shape[0]
    return jnp.einsum('sgakb,gk->sgab', t.reshape(s, 8, a, 8, b), jnp.eye(8, dtype=t.dtype))


def _glu_fwd_hook(l, tm):
    def fn(result, ins, outs, i, j):
        z = result()
        outs[0][...] = z
        outs[1][...] = (z[:, :SSM_WIDTH] * _sigmoid(z[:, SSM_WIDTH:])).astype(BF16)

    row = lambda i, j, k: (i, 0)
    return _Hook(fn, outs=[((l, 2 * SSM_WIDTH), F32, (tm, 2 * SSM_WIDTH), row),
                           ((l, SSM_WIDTH), BF16, (tm, SSM_WIDTH), row)])


def _glu_bwd_hook(z, tm):
    l = z.shape[0]

    def fn(result, ins, outs, i, j):
        zv_ref, zg_ref = ins
        dz_ref, db_ref = outs
        d = result()
        sg = _sigmoid(zg_ref[...])
        dv = d * sg
        dg = d * zv_ref[...] * sg * (1.0 - sg)
        dz_ref[:, :SSM_WIDTH] = dv.astype(BF16)
        dz_ref[:, SSM_WIDTH:] = dg.astype(BF16)

        @pl.when(i == 0)
        def _():
            db_ref[...] = jnp.zeros_like(db_ref)

        db_ref[:, :SSM_WIDTH] += jnp.sum(dv, axis=0, keepdims=True)
        db_ref[:, SSM_WIDTH:] += jnp.sum(dg, axis=0, keepdims=True)

    half = (tm, SSM_WIDTH)
    return _Hook(fn, ins=[z, z], in_specs=[(half, lambda i, j, k: (i, 0)), (half, lambda i, j, k: (i, 1))],
                 outs=[((l, 2 * SSM_WIDTH), BF16, (tm, 2 * SSM_WIDTH), lambda i, j, k: (i, 0)),
                       ((1, 2 * SSM_WIDTH), F32, (1, 2 * SSM_WIDTH), lambda i, j, k: (0, 0))])


GATE_TC = 256


def _merge_fwd(proj, a, s, *, name):
    l = a.shape[0]
    tr = min(2048, l)

    def body(ga_ref, gs_ref, a_ref, s_ref, o_ref):
        o_ref[...] = (_sigmoid(ga_ref[...]) * a_ref[...].astype(F32)
                      + _sigmoid(gs_ref[...]) * s_ref[...].astype(F32)).astype(BF16)

    own = pl.BlockSpec((tr, GATE_TC), lambda i, j: (i, j))
    return pl.pallas_call(
        body, grid=(l // tr, D_MODEL // GATE_TC),
        in_specs=[pl.BlockSpec((tr, GATE_TC), lambda i, j: (i, COL_GA // 2 + j)),
                  pl.BlockSpec((tr, GATE_TC), lambda i, j: (i, COL_GS // 2 + j)), own, own],
        out_specs=own, out_shape=jax.ShapeDtypeStruct((l, D_MODEL), BF16), name=name,
        compiler_params=_cp(("parallel", "parallel")))(proj, proj, a, s)


def _merge_bwd_hook(proj, a, s, tm):
    def fn(result, ins, outs, i, j):
        ga_ref, gs_ref, a_br, s_br = ins
        d = result()
        sa = _sigmoid(ga_ref[...])
        ss = _sigmoid(gs_ref[...])
        outs[0][...] = (d * sa).astype(BF16)
        outs[1][...] = (d * ss).astype(BF16)
        outs[2][...] = (d * a_br[...].astype(F32) * sa * (1.0 - sa)).astype(BF16)
        outs[3][...] = (d * s_br[...].astype(F32) * ss * (1.0 - ss)).astype(BF16)

    blk = (tm, GATE_TC)
    own = lambda i, j, k: (i, j)
    return _Hook(fn, ins=[proj, proj, a, s],
                 in_specs=[(blk, lambda i, j, k: (i, COL_GA // 2 + j)), (blk, lambda i, j, k: (i, COL_GS // 2 + j)),
                           (blk, own), (blk, own)],
                 outs=[(a.shape, BF16, blk, own)] * 4)


FF_TC = D_FF // 2
FF_NJ = 2
FF_ROWS = 256


FF_HALO = 16


def _conv_taps(ext, rows):
    h = FF_HALO
    return (ext[h:h + rows], pltpu.roll(ext, 1, axis=0)[h:h + rows], pltpu.roll(ext, 2, axis=0)[h:h + rows])


def _ff_specs(tr, l):
    hb = tr // FF_HALO
    last = l // FF_HALO - 1
    prev = lambda i: jnp.maximum(i * hb - 1, 0)
    nxt = lambda i: jnp.minimum((i + 1) * hb, last)
    return dict(
        own=pl.BlockSpec((tr, FF_TC), lambda j, i: (i, j)),
        own_next=pl.BlockSpec((FF_HALO, FF_TC), lambda j, i: (nxt(i), j)),
        val=pl.BlockSpec((tr, FF_TC), lambda j, i: (i, 2 * j)),
        val_next=pl.BlockSpec((FF_HALO, FF_TC), lambda j, i: (nxt(i), 2 * j)),
        gate=pl.BlockSpec((tr, FF_TC), lambda j, i: (i, 2 * j + 1)),
        gate_prev=pl.BlockSpec((FF_HALO, FF_TC), lambda j, i: (prev(i), 2 * j + 1)),
        gate_next=pl.BlockSpec((FF_HALO, FF_TC), lambda j, i: (nxt(i), 2 * j + 1)),
        pair=pl.BlockSpec((tr, 2 * FF_TC), lambda j, i: (i, j)),
        w=pl.BlockSpec((3, FF_TC), lambda j, i: (0, j)),
        b=pl.BlockSpec((1, FF_TC), lambda j, i: (0, j)))


def _ffn_act_fwd(up, conv_w, conv_b, *, name):
    l = up.shape[0]
    tr = min(FF_ROWS, l)

    def body(v_ref, g_ref, prev_ref, w_ref, b_ref, o_ref):
        prev = jnp.where(pl.program_id(1) == 0, 0.0, prev_ref[...].astype(F32))
        g0, g1, g2 = _conv_taps(jnp.concatenate([prev, g_ref[...].astype(F32)], axis=0), tr)
        gc = b_ref[...] + w_ref[0:1, :] * g2 + w_ref[1:2, :] * g1 + w_ref[2:3, :] * g0
        o_ref[...] = (v_ref[...].astype(F32) * _gelu(gc)).astype(BF16)

    sp = _ff_specs(tr, l)
    return pl.pallas_call(
        body, grid=(FF_NJ, l // tr), in_specs=[sp['val'], sp['gate'], sp['gate_prev'], sp['w'], sp['b']],
        out_specs=sp['own'], out_shape=jax.ShapeDtypeStruct((l, D_FF), BF16), name=name,
        compiler_params=_cp(("parallel", "parallel")))(up, up, up, conv_w, conv_b)


def _ffn_act_bwd(dact, up, conv_w, conv_b, *, comm=None, name):
    l = up.shape[0]
    tr = min(FF_ROWS, l)
    ni = l // tr
    te = tr + 8

    def body(d_ref, dn_ref, v_ref, vn_ref, g_ref, gp_ref, gn_ref, w_ref, b_ref, dup_ref, dw_ref, db_ref):
        i = pl.program_id(1)
        f32 = lambda ref, rows=None: ref[...].astype(F32)[:rows]
        prev = jnp.where(i == 0, 0.0, f32(gp_ref))
        g0, g1, g2 = _conv_taps(jnp.concatenate([prev, f32(g_ref), f32(gn_ref, 8)], axis=0), te)
        w0, w1, w2 = w_ref[0:1, :], w_ref[1:2, :], w_ref[2:3, :]
        gc = b_ref[...] + w0 * g2 + w1 * g1 + w2 * g0
        d_own = f32(d_ref)
        d = jnp.concatenate([d_own, jnp.where(i == ni - 1, 0.0, f32(dn_ref, 8))], axis=0)
        v = jnp.concatenate([f32(v_ref), f32(vn_ref, 8)], axis=0)
        dgc = d * v * _gelu_grad(gc)
        ahead1 = pltpu.roll(dgc, te - 1, axis=0)[:tr]
        ahead2 = pltpu.roll(dgc, te - 2, axis=0)[:tr]
        own = dgc[:tr]
        dup_ref[:, :FF_TC] = (d_own * _gelu(gc[:tr])).astype(BF16)
        dup_ref[:, FF_TC:] = (w2 * own + w1 * ahead1 + w0 * ahead2).astype(BF16)

        @pl.when(i == 0)
        def _():
            dw_ref[...] = jnp.zeros_like(dw_ref)
            db_ref[...] = jnp.zeros_like(db_ref)

        dw_ref[0:1, :] += jnp.sum(own * g2[:tr], axis=0, keepdims=True)
        dw_ref[1:2, :] += jnp.sum(own * g1[:tr], axis=0, keepdims=True)
        dw_ref[2:3, :] += jnp.sum(own * g0[:tr], axis=0, keepdims=True)
        db_ref[...] += jnp.sum(own, axis=0, keepdims=True)

    sp = _ff_specs(tr, l)
    return _hosted_call(
        body, grid=(FF_NJ, ni),
        in_specs=[sp['own'], sp['own_next'], sp['val'], sp['val_next'], sp['gate'], sp['gate_prev'],
                  sp['gate_next'], sp['w'], sp['b']],
        out_specs=[sp['pair'], sp['w'], sp['b']],
        out_shape=[jax.ShapeDtypeStruct((l, 2 * D_FF), BF16), jax.ShapeDtypeStruct((3, D_FF), F32),
                   jax.ShapeDtypeStruct((1, D_FF), F32)],
        scratch_shapes=[], sem=("parallel", "arbitrary"), name=name,
        args=(dact, dact, up, up, up, up, up, conv_w, conv_b), comm=comm)


def _join_dproj(dq, dkc, dkp, dvc, dvp, du, dga, dgs, *, name):
    l = dq.shape[0]
    nb = l // BLOCK
    widths = [ATTN_WIDTH, KV_WIDTH, KV_WIDTH, SSM_WIDTH, D_MODEL, D_MODEL]

    def body(dq_ref, dkc_ref, dkp_ref, dvc_ref, dvp_ref, du_ref, dga_ref, dgs_ref, o_ref, s_ref):
        n = pl.program_id(0)

        @pl.when(n == 0)
        def _():
            s_ref[...] = jnp.zeros_like(s_ref)

        last = n == nb - 1
        dk = (dkc_ref[...] + jnp.where(last, 0.0, dkp_ref[...])).astype(BF16)
        dv = (dvc_ref[...] + jnp.where(last, 0.0, dvp_ref[...])).astype(BF16)
        col = 0
        for v, width in zip([dq_ref[...], dk, dv, du_ref[...], dga_ref[...], dgs_ref[...]], widths):
            o_ref[:, col:col + width] = v
            s_ref[:, col:col + width] += jnp.sum(v.astype(F32), axis=0, keepdims=True)
            col += width

    cur = lambda width: pl.BlockSpec((BLOCK, width), lambda n: (n, 0))
    nxt = pl.BlockSpec((BLOCK, KV_WIDTH), lambda n: (jnp.minimum(n + 1, nb - 1), 0))
    return pl.pallas_call(
        body, grid=(nb,),
        in_specs=[cur(ATTN_WIDTH), cur(KV_WIDTH), nxt, cur(KV_WIDTH), nxt, cur(SSM_WIDTH), cur(D_MODEL), cur(D_MODEL)],
        out_specs=[pl.BlockSpec((BLOCK, IN_COLS), lambda n: (n, 0)), pl.BlockSpec((1, IN_COLS), lambda n: (0, 0))],
        out_shape=[jax.ShapeDtypeStruct((l, IN_COLS), BF16), jax.ShapeDtypeStruct((1, IN_COLS), F32)],
        name=name, compiler_params=_cp(("arbitrary",)))(dq, dkc, dkp, dvc, dvp, du, dga, dgs)


def _local_step(x, target, wts, small, shards=None):
    l = x.shape[0]
    wts = dict(wts)
    grads, recvs, sgr = {}, {}, {}
    lay = lambda keys: [LAYOUT[k] for k in keys]
    none = lambda keys: None
    gather = (lambda keys: _GatherPlan([shards[k] for k in keys], lay(keys))) if shards is not None else none
    scatter = (lambda keys: _ScatterPlan([grads[k] for k in keys], lay(keys))) if shards is not None else none

    mm = _matmul

    def take(res, plan, keys, store):
        outs, couts = res
        if plan is not None:
            store.update(zip(keys, couts))
        return outs

    def mm_plan(plan, keys, store, *args, **kw):
        if plan is None:
            return _matmul(*args, **kw)
        return take(_matmul(*args, comm=plan, **kw), plan, keys, store)

    def mm_host(keys, make_plan, store, *args, **kw):
        return mm_plan(make_plan(keys), keys, store, *args, **kw)

    up_scatter = lambda p: _ScatterPlan([grads['w_up']], lay(['w_up']), part=(p, 2)) if shards is not None else None

    col = lambda t: t.reshape(SSM_GROUPS * SSM_STATE, 1)
    a_re, a_im = col(small['ssm_a_re']), col(small['ssm_a_im'])
    log_dt = jnp.repeat(small['ssm_log_dt'].reshape(SSM_GROUPS), SSM_STATE).reshape(-1, 1)
    b_re = small['ssm_b_re'].reshape(SSM_GROUPS * SSM_STATE, SSM_GROUP)
    b_im = small['ssm_b_im'].reshape(SSM_GROUPS * SSM_STATE, SSM_GROUP)
    ab_re, ab_im, bb_re, bb_im = _ssm_disc_fwd(a_re, a_im, log_dt, b_re, b_im, name="ssm_disc_fwd")
    ab = jnp.stack([ab_re.reshape(SSM_SPLIT, SSM_X_BLK), ab_im.reshape(SSM_SPLIT, SSM_X_BLK)], axis=1)
    to_bd = lambda t: _block_diag(t.reshape(SSM_SPLIT, 8, SSM_STATE, SSM_GROUP).transpose(0, 1, 3, 2))
    bd = jnp.concatenate([to_bd(bb_re), to_bd(bb_im)], axis=2)
    c_re = small['ssm_c_re'].reshape(SSM_SPLIT, 8, SSM_GROUP, SSM_STATE)
    c_im = small['ssm_c_im'].reshape(SSM_SPLIT, 8, SSM_GROUP, SSM_STATE)
    cdt = jnp.concatenate([_block_diag(c_re), -_block_diag(c_im)], axis=2)
    bd_b, cdt_b = bd.astype(BF16), cdt.astype(BF16)
    bdt_b, cd_b = bd_b.transpose(0, 2, 1), cdt_b.transpose(0, 2, 1)
    dskip = small['ssm_d'].reshape(1, SSM_WIDTH)

    sinks = small['attn_sinks'].reshape(N_Q_HEADS)
    plan = gather(['w_in_t'])
    h1, = take(_rms_fwd(x, small['attn_norm_g'], comm=plan, name="rms1_fwd"), plan, ['w_in_t'], wts)
    proj = mm_host(['w_glu', 'w_ba', 'w_bs', 'w_out'], gather, wts,
                   h1, wts['w_in_t'], tb=True, tm=512, tn=2944, tk=2048, inner='m', out_dtype=F32,
                   bias=small['b_in'], name="mm_in")
    attn = _attn_fwd(proj, sinks, name="attn_fwd")
    plan = gather(['w_up'])
    y, gy, xs = take(_ssm_fwd(proj, ab, bd_b, cd_b, dskip, comm=plan, name="ssm_fwd"), plan, ['w_up'], wts)
    z, ssm = mm(gy, wts['w_glu'], tm=1024, tn=1024, tk=512, bias=small['b_glu'],
                epilogue=_glu_fwd_hook(l, min(1024, l)), name="mm_glu")
    a_br = mm(attn, wts['w_ba'], tm=1024, tn=1024, tk=1024, out_dtype=BF16, name="mm_ba")
    s_br = mm(ssm, wts['w_bs'], tm=1024, tn=1024, tk=512, out_dtype=BF16, name="mm_bs")
    tr = min(ROW_TILE, l)
    merged = _merge_fwd(proj, a_br, s_br, name="merge_fwd")
    x2 = mm(merged, wts['w_out'], tm=1024, tn=1024, tk=2048, inner='m', out_dtype=F32, res=x, name="mm_out")
    h2, = take(_rms_fwd(x2, small['ffn_norm_g'], name="rms2_fwd"), None, [], wts)
    up = mm_host(['w_down'], gather, wts,
                 h2, wts['w_up'], tm=1024, tn=1024, tk=2048, out_dtype=BF16, name="mm_up")
    conv_w, conv_b = small['conv_w'], small['conv_b']
    act = _ffn_act_fwd(up, conv_w, conv_b, name="ffn_act_fwd")
    x3 = mm(act, wts['w_down'], tm=1024, tn=1024, tk=2816, out_dtype=F32, res=x2, name="mm_down")
    dx3b, d_g3, loss = _rowwise(
        _final_loss_hook(small['final_norm_g'].reshape(1, D_MODEL), target, tr), x3, tr, name="final_loss")

    sgr['final_norm_g'] = d_g3.reshape(D_MODEL)
    dact = mm(dx3b, wts['w_down'], tb=True, tm=512, tn=2816, tk=2048, inner='m', out_dtype=BF16, name="mm_dact")
    grads['w_down'] = mm(act, dx3b, ta=True, tm=1408, tn=1024, tk=2048, out_dtype=BF16, name="mm_dw_down")
    plan = scatter(['w_down'])
    dup, sgr['conv_w'], sgr['conv_b'] = take(
        _ffn_act_bwd(dact, up, conv_w, conv_b, comm=plan, name="ffn_act_bwd"), plan, ['w_down'], recvs)
    grads['w_up'] = mm(h2, dup, ta=True, tm=2048, tn=1024, tk=2048, out_dtype=BF16, name="mm_dw_up")
    dh2 = mm_plan(up_scatter(0), ['w_up#0'], recvs,
                  dup, wts['w_up'], tb=True, tm=1024, tn=1024, tk=2816, out_dtype=BF16, name="mm_dh2")
    dx2b, sgr['ffn_norm_g'] = _rowwise(
        _rms_bwd_hook(x2, small['ffn_norm_g'], dx3b, tr, BF16), dh2, tr, name="rms2_bwd")

    d_a, d_s, dga, dgs = mm(dx2b, wts['w_out'], tb=True, tm=1024, tn=GATE_TC, tk=2048,
                            epilogue=_merge_bwd_hook(proj, a_br, s_br, min(1024, l)), name="mm_dmerged")
    grads['w_out'] = mm(merged, dx2b, ta=True, tm=1024, tn=1024, tk=2048, out_dtype=BF16, name="mm_dw_out")
    dattn = mm(d_a, wts['w_ba'], tb=True, tm=1024, tn=1024, tk=2048, inner='m', out_dtype=BF16, name="mm_dattn")
    grads['w_ba'] = mm(attn, d_a, ta=True, tm=1024, tn=1024, tk=2048, out_dtype=BF16, name="mm_dw_ba")
    dz, sgr['b_glu'] = mm(d_s, wts['w_bs'], tb=True, tm=1024, tn=512, tk=2048, sequential=True,
                          epilogue=_glu_bwd_hook(z, min(1024, l)), name="mm_dssm")
    grads['w_bs'] = mm(ssm, d_s, ta=True, tm=512, tn=1024, tk=2048, out_dtype=BF16, name="mm_dw_bs")
    grads['w_glu'] = mm(gy, dz, ta=True, tm=512, tn=1024, tk=2048, out_dtype=BF16, name="mm_dw_glu")
    dgy = mm(dz, wts['w_glu'], tb=True, tm=1024, tn=512, tk=1024, inner='m', out_dtype=F32, name="mm_dgy")
    plan = up_scatter(1)
    du, d_bd, d_cd, d_ab, sgr['ssm_d'] = take(
        _ssm_bwd(proj, y, dgy, xs, ab, bdt_b, cdt_b, dskip, comm=plan, name="ssm_bwd"), plan, ['w_up#1'], recvs)
    keys = ['w_out', 'w_ba', 'w_bs', 'w_glu']
    plan = scatter(keys)
    dq, dkc, dkp, dvc, dvp, dsink = take(
        _attn_bwd(proj, sinks, dattn, comm=plan, name="attn_bwd"), plan, keys, recvs)
    sgr['attn_sinks'] = dsink[:, :N_Q_HEADS]
    dproj, sgr['b_in'] = _join_dproj(dq, dkc, dkp, dvc, dvp, du, dga, dgs, name="join_dproj")
    grads['w_in_t'] = mm(dproj, h1, ta=True, tm=2944, tn=1024, tk=1024, out_dtype=BF16, name="mm_dw_in")
    dh1 = mm_host(['w_in_t'], scatter, recvs,
                  dproj, wts['w_in_t'], tm=1024, tn=1024, tk=2944, out_dtype=BF16, name="mm_dh1")
    grad_x, sgr['attn_norm_g'] = _rowwise(
        _rms_bwd_hook(x, small['attn_norm_g'], dx2b, tr, F32), dh1, tr, name="rms1_bwd")

    from_bd = lambda t: _block_diag_take(t, SSM_GROUP, SSM_STATE).transpose(0, 1, 3, 2).reshape(
        SSM_GROUPS * SSM_STATE, SSM_GROUP)
    d_bb_re = from_bd(d_bd[:, :, :SSM_X_BLK])
    d_bb_im = from_bd(d_bd[:, :, SSM_X_BLK:])
    d_cdt = d_cd.transpose(0, 2, 1)
    shape_c = (1, SSM_GROUPS, SSM_GROUP, SSM_STATE)
    sgr['ssm_c_re'] = _block_diag_take(d_cdt[:, :, :SSM_X_BLK], SSM_GROUP, SSM_STATE).reshape(shape_c)
    sgr['ssm_c_im'] = -_block_diag_take(d_cdt[:, :, SSM_X_BLK:], SSM_GROUP, SSM_STATE).reshape(shape_c)
    d_a_re, d_a_im, d_ldt, d_b_re, d_b_im = _ssm_disc_bwd(
        a_re, a_im, log_dt, b_re, b_im, d_ab[:, 0, :].reshape(-1, 1), d_ab[:, 1, :].reshape(-1, 1),
        d_bb_re, d_bb_im, name="ssm_disc_bwd")
    sgr['ssm_a_re'] = d_a_re.reshape(1, SSM_GROUPS, SSM_STATE)
    sgr['ssm_a_im'] = d_a_im.reshape(1, SSM_GROUPS, SSM_STATE)
    sgr['ssm_log_dt'] = d_ldt.reshape(SSM_GROUPS, SSM_STATE).sum(axis=1).reshape(1, SSM_GROUPS)
    sgr['ssm_b_re'] = d_b_re.reshape(1, SSM_GROUPS, SSM_STATE, SSM_GROUP)
    sgr['ssm_b_im'] = d_b_im.reshape(1, SSM_GROUPS, SSM_STATE, SSM_GROUP)
    return loss, grad_x, grads, recvs, sgr


def _swap_cores(arrs, *, name):
    n = len(arrs)

    def body(*refs):
        ins, outs = refs[:n], refs[n:2 * n]
        send_sems, recv_sems = refs[2 * n:]
        x, y, c = _place()
        copies = []
        for i in range(n):
            cp = pltpu.make_async_remote_copy(
                src_ref=ins[i], dst_ref=outs[i], send_sem=send_sems.at[i], recv_sem=recv_sems.at[i],
                device_id=(x, y, 1 - c), device_id_type=MESH)
            cp.start()
            copies.append(cp)
        for cp in copies:
            cp.wait()

    return pl.pallas_call(
        body, in_specs=[ANY] * n, out_specs=[ANY] * n,
        out_shape=[jax.ShapeDtypeStruct(a.shape, a.dtype) for a in arrs],
        scratch_shapes=[pltpu.SemaphoreType.DMA((n,)), pltpu.SemaphoreType.DMA((n,))],
        name=name)(*arrs)


def _all_reduce_small(buf, *, name):
    r = buf.shape[0]

    def body(in_ref, out_ref, slots, send_sems, recv_sems):
        x, y, c = _place()
        me = 4 * x + 2 * y + c
        slots[pl.ds(me, 1)] = in_ref[...][None]
        copies = []
        for k in range(N_DEV - 1):
            bx, by, bc = ((k + 1) >> 2) & 1, ((k + 1) >> 1) & 1, (k + 1) & 1
            peer = (1 - x if bx else x, 1 - y if by else y, 1 - c if bc else c)
            cp = pltpu.make_async_remote_copy(
                src_ref=in_ref, dst_ref=slots.at[me], send_sem=send_sems.at[k], recv_sem=recv_sems.at[k],
                device_id=peer, device_id_type=MESH)
            cp.start()
            copies.append(cp)
        for cp in copies:
            cp.wait()
        acc = slots[0]
        for d in range(1, N_DEV):
            acc = acc + slots[d]
        out_ref[...] = acc

    vm = pl.BlockSpec(memory_space=pltpu.VMEM)
    return pl.pallas_call(
        body, in_specs=[vm], out_specs=vm, out_shape=jax.ShapeDtypeStruct((r, 128), F32),
        scratch_shapes=[pltpu.VMEM((N_DEV, r, 128), F32), pltpu.SemaphoreType.DMA((N_DEV - 1,)),
                        pltpu.SemaphoreType.DMA((N_DEV - 1,))],
        name=name)(buf)


def _pack(arrs):
    flat = jnp.concatenate([a.reshape(-1).astype(F32) for a in arrs])
    pad = (-flat.shape[0]) % 1024
    return jnp.pad(flat, (0, pad)).reshape(-1, 128)


def _unpack(buf, shapes):
    flat = buf.reshape(-1)
    out, pos = [], 0
    for s in shapes:
        size = math.prod(s)
        out.append(flat[pos:pos + size].reshape(s))
        pos += size
    return out


TILE_ELEMS = 256 * 1024


def _tile_rows(r, c):
    if r * c <= TILE_ELEMS:
        return r
    for tr in range(TILE_ELEMS // c // 16 * 16, 0, -16):
        if r % tr == 0:
            return tr
    raise ValueError((r, c))


def _sum4(full, axis, pos, recvs, *, name):
    r, c = _block_shape(full.shape, axis)
    parts = len(recvs)
    tr = _tile_rows(r // parts, c)
    per = r // parts // tr
    nt = r // tr

    def body(pos_ref, o_ref, *refs):
        out_ref = refs[parts]
        for p in range(parts):
            @pl.when(pl.program_id(0) // per == p)
            def _():
                acc = o_ref[...].astype(F32)
                for k in range(3):
                    acc = acc + refs[p][k].astype(F32)
                out_ref[...] = acc.astype(BF16)

    own = (pl.BlockSpec((tr, c), lambda i, pos_ref: (pos_ref[0] * nt + i, 0)) if axis == 0
           else pl.BlockSpec((tr, c), lambda i, pos_ref: (i, pos_ref[0])))
    part_spec = lambda p: pl.BlockSpec((3, tr, c), lambda i, pos_ref: (0, jnp.clip(i - p * per, 0, per - 1), 0))
    grid_spec = pltpu.PrefetchScalarGridSpec(
        num_scalar_prefetch=1, grid=(nt,), in_specs=[own] + [part_spec(p) for p in range(parts)],
        out_specs=pl.BlockSpec((tr, c), lambda i, pos_ref: (i, 0)))
    return pl.pallas_call(
        body, grid_spec=grid_spec, out_shape=jax.ShapeDtypeStruct((r, c), BF16),
        name=name, compiler_params=_cp(("parallel",)))(pos, full, *recvs)


def _adam_step(w, g, m, v):
    bc1 = 1.0 - ADAM_B1 ** ADAM_STEP
    bc2 = 1.0 - ADAM_B2 ** ADAM_STEP
    mn = ADAM_B1 * m + (1.0 - ADAM_B1) * g
    vn = ADAM_B2 * v + (1.0 - ADAM_B2) * (g * g)
    m_hat = mn / bc1
    v_hat = vn / bc2
    return -ADAM_LR * (m_hat / (jnp.sqrt(v_hat) + ADAM_EPS) + ADAM_WD * w), mn, vn


def _adamw(w, ga, gb, m, v, *, name):
    r, c = w.shape
    tr = _tile_rows(r, c)

    def body(w_ref, ga_ref, gb_ref, m_ref, v_ref, g_out, d_out, m_out, v_out):
        g = ga_ref[...].astype(F32) + gb_ref[...].astype(F32)
        g_out[...] = g
        d_out[...], m_out[...], v_out[...] = _adam_step(w_ref[...], g, m_ref[...], v_ref[...])

    spec = pl.BlockSpec((tr, c), lambda i: (i, 0))
    shp = jax.ShapeDtypeStruct((r, c), F32)
    return pl.pallas_call(
        body, grid=(r // tr,), in_specs=[spec] * 5, out_specs=[spec] * 4,
        out_shape=[shp] * 4, name=name, compiler_params=_cp(("parallel",)))(w, ga, gb, m, v)


def _lanes(t):
    return t.reshape(-1, 128) if t.size % 128 == 0 else t.reshape(1, -1)


def _adamw_small(ws, gs, ms, vs, *, name):
    n = len(ws)

    def body(*refs):
        for i in range(n):
            w_ref, g_ref, m_ref, v_ref = (refs[k * n + i] for k in range(4))
            outs = [refs[(4 + k) * n + i] for k in range(3)]
            outs[0][...], outs[1][...], outs[2][...] = _adam_step(w_ref[...], g_ref[...], m_ref[...], v_ref[...])

    flat = [_lanes(t) for group in (ws, gs, ms, vs) for t in group]
    shp = [jax.ShapeDtypeStruct(_lanes(t).shape, F32) for t in ws]
    res = pl.pallas_call(body, out_shape=shp * 3, name=name)(*flat)
    return [[res[k * n + i].reshape(ws[i].shape) for i in range(n)] for k in range(3)]


BIG = ['w_in', 'w_glu', 'w_branch_attn', 'w_branch_ssm', 'w_out', 'w_up', 'w_down']
BIG_KEY = {'w_in': 'w_in_t', 'w_glu': 'w_glu', 'w_branch_attn': 'w_ba', 'w_branch_ssm': 'w_bs',
           'w_out': 'w_out', 'w_up': 'w_up', 'w_down': 'w_down'}
TRANSPOSED = {'w_in'}
SMALL = ['attn_norm_g', 'b_in', 'attn_sinks', 'ssm_a_re', 'ssm_a_im', 'ssm_log_dt', 'ssm_b_re', 'ssm_b_im',
         'ssm_c_re', 'ssm_c_im', 'ssm_d', 'b_glu', 'ffn_norm_g', 'conv_b', 'final_norm_g']
WEIGHTS = ['attn_norm_g', 'w_in', 'b_in', 'attn_sinks', 'ssm_a_re', 'ssm_a_im', 'ssm_log_dt', 'ssm_b_re',
           'ssm_b_im', 'ssm_c_re', 'ssm_c_im', 'ssm_d', 'w_glu', 'b_glu', 'w_branch_attn', 'w_branch_ssm',
           'w_out', 'ffn_norm_g', 'w_up', 'conv_w', 'conv_b', 'w_down', 'final_norm_g']


def _shard_2d(name, t):
    t = t[0]
    return t.T if name in TRANSPOSED else t


def _unshard_2d(name, t):
    return (t.T if name in TRANSPOSED else t)[None]


def kernel(x, attn_norm_g, w_in, b_in, attn_sinks, ssm_a_re, ssm_a_im, ssm_log_dt, ssm_b_re, ssm_b_im, ssm_c_re, ssm_c_im, ssm_d, w_glu, b_glu, w_branch_attn, w_branch_ssm, w_out, ffn_norm_g, w_up, conv_w, conv_b, w_down, final_norm_g, loss_target, m_attn_norm_g, m_w_in, m_b_in, m_attn_sinks, m_ssm_a_re, m_ssm_a_im, m_ssm_log_dt, m_ssm_b_re, m_ssm_b_im, m_ssm_c_re, m_ssm_c_im, m_ssm_d, m_w_glu, m_b_glu, m_w_branch_attn, m_w_branch_ssm, m_w_out, m_ffn_norm_g, m_w_up, m_conv_w, m_conv_b, m_w_down, m_final_norm_g, v_attn_norm_g, v_w_in, v_b_in, v_attn_sinks, v_ssm_a_re, v_ssm_a_im, v_ssm_log_dt, v_ssm_b_re, v_ssm_b_im, v_ssm_c_re, v_ssm_c_im, v_ssm_d, v_w_glu, v_b_glu, v_w_branch_attn, v_w_branch_ssm, v_w_out, v_ffn_norm_g, v_w_up, v_conv_w, v_conv_b, v_w_down, v_final_norm_g):
    args = dict(locals())
    w = {n: args[n] for n in WEIGHTS}
    m = {n: args['m_' + n] for n in WEIGHTS}
    v = {n: args['v_' + n] for n in WEIGHTS}
    xi, yi, ci = _place()
    blk = 2 * xi + yi

    shards = {BIG_KEY[n]: _shard_2d(n, w[n]).astype(BF16) for n in BIG}
    cw_cols = w['conv_w'].shape[2]
    cw_place = lax.dynamic_update_slice(jnp.zeros((3, D_FF), F32), w['conv_w'][0] * (ci == 0).astype(F32),
                                        (0, blk * cw_cols))
    conv_w_full = _unpack(_all_reduce_small(_pack([cw_place]), name="gather_conv_w"), [(3, D_FF)])[0]

    small = {n: w[n] for n in SMALL}
    small['conv_w'] = conv_w_full
    loss_part, grad_x, grads, recvs, sgr = _local_step(x[0], loss_target[0], {}, small, shards)

    halves = []
    for n in BIG:
        key = BIG_KEY[n]
        full = grads[key]
        recv = [recvs[key]] if key in recvs else [recvs[key + '#0'], recvs[key + '#1']]
        axis, interleaved = LAYOUT[key]
        pos = _block_pos(xi, yi, interleaved).astype(jnp.int32).reshape(1)
        halves.append(_sum4(full, axis, pos, recv, name="sum4_" + n))
    others = _swap_cores(halves, name="swap_cores")
    out = {}
    for n, mine, other in zip(BIG, halves, others):
        res = _adamw(_shard_2d(n, w[n]), mine, other, _shard_2d(n, m[n]), _shard_2d(n, v[n]), name="adamw_" + n)
        out[n] = [_unshard_2d(n, t) for t in res]

    names = SMALL + ['conv_w']
    shapes = [w[n].shape for n in SMALL] + [(3, D_FF)]
    packed = _pack([sgr[n] for n in names] + [loss_part])
    summed = _unpack(_all_reduce_small(packed, name="all_reduce_small"), shapes + [(1, 1)])
    loss = summed[-1].reshape(())
    sg = dict(zip(names, summed[:-1]))
    sg['conv_w'] = lax.dynamic_slice_in_dim(sg['conv_w'], blk * cw_cols, cw_cols, axis=1)[None]
    deltas, new_m, new_v = _adamw_small([w[n] for n in names], [sg[n] for n in names], [m[n] for n in names],
                                        [v[n] for n in names], name="adamw_small")
    for i, n in enumerate(names):
        out[n] = [sg[n], deltas[i], new_m[i], new_v[i]]

    return (loss, grad_x[None], *[out[n][0] for n in WEIGHTS], *[out[n][1] for n in WEIGHTS],
            *[out[n][2] for n in WEIGHTS], *[out[n][3] for n in WEIGHTS])
```

```python
import functools
import math

import jax
import jax.numpy as jnp
from jax import lax
from jax.experimental import pallas as pl
from jax.experimental.pallas import tpu as pltpu

F32 = jnp.float32
BF16 = jnp.bfloat16

D_MODEL = 2048
N_Q_HEADS = 16
HEAD_DIM = 64
ATTN_WIDTH = 1024
KV_WIDTH = 128
BLOCK = 128
SSM_WIDTH = 512
SSM_GROUPS = 32
SSM_GROUP = 16
SSM_STATE = 64
D_FF = 5632
IN_COLS = 5888
RMS_EPS = 1e-6
NEG_BIG = -1e30
N_CHIPS = 4
N_DEV = 8

COL_K = 8
COL_V = 9
COL_U = 10
COL_GA = 14
COL_GS = 30

SSM_SPLIT = 4
SSM_U_BLK = 128
SSM_X_BLK = 512
SSM_CHUNK = 256

ADAM_LR = 0.001
ADAM_B1 = 0.9
ADAM_B2 = 0.999
ADAM_EPS = 1e-08
ADAM_WD = 0.01
ADAM_STEP = 10

VMEM_LIMIT_BYTES = 56 * 1024 * 1024
INV_SQRT2 = 1.0 / math.sqrt(2.0)
INV_SQRT2PI = 1.0 / math.sqrt(2.0 * math.pi)
MESH = pl.DeviceIdType.MESH
ANY = pl.BlockSpec(memory_space=pl.ANY)


def _cp(sem):
    return pltpu.CompilerParams(dimension_semantics=sem, vmem_limit_bytes=VMEM_LIMIT_BYTES)


def _gelu(x):
    return 0.5 * x * (1.0 + lax.erf(x * INV_SQRT2))


def _gelu_grad(x):
    return 0.5 * (1.0 + lax.erf(x * INV_SQRT2)) + x * jnp.exp(-0.5 * x * x) * INV_SQRT2PI


def _sigmoid(x):
    return 1.0 / (1.0 + jnp.exp(-x))


def _place():
    return lax.axis_index("x"), lax.axis_index("y"), lax.axis_index("c")


def _other_chips(x, y):
    return [(1 - x, y), (x, 1 - y), (1 - x, 1 - y)]


def _block_pos(x, y, interleaved):
    return x + 2 * y if interleaved else 2 * x + y


LAYOUT = {'w_in_t': (0, False), 'w_glu': (1, False), 'w_ba': (1, False), 'w_bs': (1, False),
          'w_out': (0, False), 'w_up': (1, True), 'w_down': (0, False)}


def _window(ref, axis, pos, size, rows=None):
    if axis == 0:
        start, count = (0, size) if rows is None else rows
        return ref.at[pl.ds(pos * size + start, count), :]
    cols = pl.ds(pos * size, size)
    return ref.at[:, cols] if rows is None else ref.at[pl.ds(rows[0], rows[1]), cols]


def _gathered_shape(shape, axis):
    return tuple(N_CHIPS * d if a == axis else d for a, d in enumerate(shape))


def _block_shape(shape, axis):
    return tuple(d // N_CHIPS if a == axis else d for a, d in enumerate(shape))


class _GatherPlan:
    def __init__(self, shards, layouts):
        self.arrays = list(shards)
        self.layouts = list(layouts)
        n = len(shards)
        self.out_shape = [jax.ShapeDtypeStruct(_gathered_shape(s.shape, lay[0]), s.dtype)
                          for s, lay in zip(shards, layouts)]
        self.scratch = [pltpu.SemaphoreType.DMA((6 * n,)), pltpu.SemaphoreType.DMA((6 * n,)),
                        pltpu.SemaphoreType.DMA((n,))]

    def _copies(self, kind, ins, outs, sems):
        send, recv, local = sems
        n = len(self.arrays)
        x, y, c = _place()
        copies = []
        for i in range(n):
            axis, interleaved = self.layouts[i]
            size = self.arrays[i].shape[axis]
            h = self.arrays[i].shape[0] // 2
            first = lambda core: core * h
            blk = _block_pos(x, y, interleaved)
            if kind == 'mine':
                copies.append(pltpu.make_async_copy(ins[i], _window(outs[i], axis, blk, size), local.at[i]))
                continue
            for k, (px, py) in enumerate(_other_chips(x, y)):
                theirs = _block_pos(px, py, interleaved)
                if kind in ('ici_out', 'ici_in'):
                    route = dict(send_sem=send.at[3 * i + k], recv_sem=recv.at[3 * i + k],
                                 device_id=(px, py, c), device_id_type=MESH)
                else:
                    route = dict(send_sem=send.at[3 * (n + i) + k], recv_sem=recv.at[3 * (n + i) + k],
                                 device_id=(x, y, 1 - c), device_id_type=MESH)
                if kind == 'ici_out':
                    src, dst = ins[i].at[pl.ds(first(c), h), :], _window(outs[i], axis, blk, size, (first(c), h))
                elif kind == 'd2d_in':
                    src = dst = _window(outs[i], axis, theirs, size, (first(1 - c), h))
                else:
                    src = dst = _window(outs[i], axis, theirs, size, (first(c), h))
                copies.append(pltpu.make_async_remote_copy(src_ref=src, dst_ref=dst, **route))
        return copies

    def start(self, ins, outs, sems):
        for cp in self._copies('mine', ins, outs, sems) + self._copies('ici_out', ins, outs, sems):
            cp.start()

    def middle(self, ins, outs, sems):
        for arrived, onward in zip(self._copies('ici_in', ins, outs, sems), self._copies('d2d_out', ins, outs, sems)):
            arrived.wait_recv()
            onward.start()

    def finish(self, ins, outs, sems):
        for cp in self._copies('d2d_in', ins, outs, sems):
            cp.wait_recv()
        for cp in self._copies('ici_out', ins, outs, sems) + self._copies('d2d_out', ins, outs, sems):
            cp.wait_send()
        for cp in self._copies('mine', ins, outs, sems):
            cp.wait()


class _ScatterPlan:
    def __init__(self, fulls, layouts, part=(0, 1)):
        self.arrays = list(fulls)
        self.layouts = list(layouts)
        self.part = part
        n = len(fulls)
        self.out_shape = []
        for f, lay in zip(fulls, layouts):
            rows, cols = _block_shape(f.shape, lay[0])
            self.out_shape.append(jax.ShapeDtypeStruct((3, rows // part[1], cols), f.dtype))
        self.scratch = [pltpu.SemaphoreType.DMA((3 * n,)), pltpu.SemaphoreType.DMA((3 * n,))]

    def _copies(self, ins, outs, sems):
        send, recv = sems
        x, y, c = _place()
        copies = []
        for i in range(len(self.arrays)):
            axis, interleaved = self.layouts[i]
            size = self.arrays[i].shape[axis] // N_CHIPS
            h = _block_shape(self.arrays[i].shape, axis)[0] // self.part[1]
            rows = (self.part[0] * h, h)
            for k, (px, py) in enumerate(_other_chips(x, y)):
                copies.append(pltpu.make_async_remote_copy(
                    src_ref=_window(ins[i], axis, _block_pos(px, py, interleaved), size, rows), dst_ref=outs[i].at[k],
                    send_sem=send.at[3 * i + k], recv_sem=recv.at[3 * i + k],
                    device_id=(px, py, c), device_id_type=MESH))
        return copies

    def start(self, ins, outs, sems):
        for cp in self._copies(ins, outs, sems):
            cp.start()

    def middle(self, ins, outs, sems):
        pass

    def finish(self, ins, outs, sems):
        for cp in self._copies(ins, outs, sems):
            cp.wait()


def _hosted_call(body, *, grid, in_specs, out_specs, out_shape, scratch_shapes, sem, name, args, comm=None,
                 aliases=None):
    aliases = aliases or {}
    if comm is None:
        outs = pl.pallas_call(body, grid=grid, in_specs=in_specs, out_specs=out_specs, out_shape=out_shape,
                              scratch_shapes=scratch_shapes, name=name, input_output_aliases=aliases,
                              compiler_params=_cp(sem))(*args)
        return outs, None
    n_in, n_out, n_scr = len(in_specs), len(out_specs), len(scratch_shapes)
    nc, ns = len(comm.arrays), len(comm.scratch)
    total = math.prod(grid)
    mid = total - max(1, total // 8)

    def wrapped(*refs):
        pos = 0
        ins = refs[pos:pos + n_in]; pos += n_in
        cins = refs[pos:pos + nc]; pos += nc
        outs = refs[pos:pos + n_out]; pos += n_out
        couts = refs[pos:pos + nc]; pos += nc
        scr = refs[pos:pos + n_scr]; pos += n_scr
        sems = refs[pos:pos + ns]
        step = 0
        for ax, g in enumerate(grid):
            step = step * g + pl.program_id(ax)

        @pl.when(step == 0)
        def _():
            comm.start(cins, couts, sems)

        body(*ins, *outs, *scr)

        @pl.when(step == mid)
        def _():
            comm.middle(cins, couts, sems)

        @pl.when(step == total - 1)
        def _():
            comm.finish(cins, couts, sems)

    res = pl.pallas_call(
        wrapped, grid=grid, in_specs=list(in_specs) + [ANY] * nc, out_specs=list(out_specs) + [ANY] * nc,
        out_shape=list(out_shape) + list(comm.out_shape), scratch_shapes=list(scratch_shapes) + list(comm.scratch),
        name=name, input_output_aliases=aliases,
        compiler_params=_cp(("arbitrary",) * len(grid)))(*args, *comm.arrays)
    return res[:n_out], res[n_out:]


class _Hook:
    def __init__(self, fn, ins=(), in_specs=(), outs=()):
        self.fn, self.ins, self.in_specs, self.outs = fn, list(ins), list(in_specs), list(outs)


def _matmul(a, b, *, ta=False, tb=False, tm, tn, tk, out_dtype=None, bias=None, res=None, inner='n',
            comm=None, prologue=None, epilogue=None, a_shape=None, sequential=False, name):
    if a is None:
        m, kdim = a_shape
    elif ta:
        kdim, m = a.shape
    else:
        m, kdim = a.shape
    if tb:
        n, k2 = b.shape
    else:
        k2, n = b.shape
    assert kdim == k2, (name, kdim, b.shape)
    tm, tn, tk = min(tm, m), min(tn, n), min(tk, kdim)
    assert m % tm == 0 and n % tn == 0 and kdim % tk == 0, (name, m, n, kdim, tm, tn, tk)
    nk = kdim // tk
    dn = (((0 if ta else 1,), (1 if tb else 0,)), ((), ()))
    hooks = [h for h in (prologue, epilogue) if h is not None]
    n_pro_in = len(prologue.ins) if prologue else 0
    n_epi_in = len(epilogue.ins) if epilogue else 0
    n_pro_out = len(prologue.outs) if prologue else 0
    n_epi_out = len(epilogue.outs) if epilogue else 0
    if inner == 'n':
        grid = (m // tm, n // tn, nk)
        mi = lambda g0, g1: g0
        ni = lambda g0, g1: g1
    else:
        grid = (n // tn, m // tm, nk)
        mi = lambda g0, g1: g1
        ni = lambda g0, g1: g0

    def body(*refs):
        refs = list(refs)
        take = lambda cnt: [refs.pop(0) for _ in range(cnt)]
        a_ref = take(1)[0] if a is not None else None
        b_ref = take(1)[0]
        bias_ref = take(1)[0] if bias is not None else None
        res_ref = take(1)[0] if res is not None else None
        pro_in, epi_in = take(n_pro_in), take(n_epi_in)
        o_ref = take(1)[0] if epilogue is None else None
        pro_out, epi_out = take(n_pro_out), take(n_epi_out)
        i, j, k = mi(pl.program_id(0), pl.program_id(1)), ni(pl.program_id(0), pl.program_id(1)), pl.program_id(2)

        def finish(src):
            def result(rows=slice(None)):
                r = src[rows, :]
                if bias_ref is not None:
                    r = r + bias_ref[...]
                if res_ref is not None:
                    r = r + res_ref[rows, :]
                return r

            if epilogue is None:
                o_ref[...] = result().astype(out_dtype)
            else:
                epilogue.fn(result, epi_in, epi_out, i, j)

        a_val = a_ref[...] if prologue is None else prologue.fn(a_ref, pro_in, pro_out, i, k)
        prod = lax.dot_general(a_val.astype(BF16), b_ref[...].astype(BF16), dn, preferred_element_type=F32)
        if nk == 1:
            finish(prod)
            return
        acc_ref = refs[0]

        @pl.when(k == 0)
        def _():
            acc_ref[...] = prod

        @pl.when(k > 0)
        def _():
            acc_ref[...] += prod

        @pl.when(k == nk - 1)
        def _():
            finish(acc_ref)

    spec = lambda shape, fn: pl.BlockSpec(shape, lambda g0, g1, k: fn(mi(g0, g1), ni(g0, g1), k))
    in_specs, args = [], []
    if a is not None:
        in_specs.append(spec((tk, tm), lambda i, j, k: (k, i)) if ta else spec((tm, tk), lambda i, j, k: (i, k)))
        args.append(a)
    in_specs.append(spec((tn, tk), lambda i, j, k: (j, k)) if tb else spec((tk, tn), lambda i, j, k: (k, j)))
    args.append(b)
    if bias is not None:
        in_specs.append(spec((1, tn), lambda i, j, k: (0, j)))
        args.append(bias)
    if res is not None:
        in_specs.append(spec((tm, tn), lambda i, j, k: (i, j)))
        args.append(res)
    for h in hooks:
        in_specs += [spec(shape, fn) for shape, fn in h.in_specs]
        args += h.ins
    out_specs, out_shape = [], []
    if epilogue is None:
        out_specs.append(spec((tm, tn), lambda i, j, k: (i, j)))
        out_shape.append(jax.ShapeDtypeStruct((m, n), out_dtype))
    for h in hooks:
        out_specs += [spec(blk, fn) for _, _, blk, fn in h.outs]
        out_shape += [jax.ShapeDtypeStruct(shape, dtype) for shape, dtype, _, _ in h.outs]
    outs, couts = _hosted_call(
        body, grid=grid, in_specs=in_specs, out_specs=out_specs, out_shape=out_shape,
        scratch_shapes=[pltpu.VMEM((tm, tn), F32)] if nk > 1 else [],
        sem=("arbitrary",) * 3 if sequential else ("parallel", "parallel", "arbitrary"),
        name=name, args=args, comm=comm)
    outs = outs[0] if not hooks else outs
    return outs if comm is None else (outs, couts)


def _rms_fwd(x, g, *, comm=None, name):
    l, d = x.shape
    tr = min(256, l)

    def body(x_ref, g_ref, h_ref):
        xf = x_ref[...]
        r = lax.rsqrt(jnp.mean(xf * xf, axis=-1, keepdims=True) + RMS_EPS)
        h_ref[...] = ((xf * r) * g_ref[...]).astype(BF16)

    row = pl.BlockSpec((tr, d), lambda i: (i, 0))
    return _hosted_call(
        body, grid=(l // tr,), in_specs=[row, pl.BlockSpec((1, d), lambda i: (0, 0))],
        out_specs=[row], out_shape=[jax.ShapeDtypeStruct((l, d), BF16)], scratch_shapes=[],
        sem=("parallel",), name=name, args=(x, g), comm=comm)


EPI_ROWS = 128
ROW_TILE = 256


def _row_chunks(tm):
    ch = min(EPI_ROWS, tm)
    return [slice(c * ch, (c + 1) * ch) for c in range(tm // ch)]


def _rowwise(hook, src, tm, *, name):
    l, d = src.shape
    n_in = len(hook.ins)

    def body(*refs):
        src_ref, ins, outs = refs[0], refs[1:1 + n_in], refs[1 + n_in:]
        hook.fn(lambda rows=slice(None): src_ref[rows, :], ins, outs, pl.program_id(0), 0)

    spec = lambda shape, fn: pl.BlockSpec(shape, lambda i: fn(i, 0, 0))
    return pl.pallas_call(
        body, grid=(l // tm,),
        in_specs=[pl.BlockSpec((tm, d), lambda i: (i, 0))] + [spec(shape, fn) for shape, fn in hook.in_specs],
        out_specs=[spec(blk, fn) for _, _, blk, fn in hook.outs],
        out_shape=[jax.ShapeDtypeStruct(shape, dtype) for shape, dtype, _, _ in hook.outs],
        name=name, compiler_params=_cp(("arbitrary",)))(src, *hook.ins)


def _rms_bwd_hook(x, g, dres, tm, out_dtype):
    def fn(result, ins, outs, i, j):
        x_ref, g_ref, dres_ref = ins
        dx_ref, dg_ref = outs

        @pl.when(i == 0)
        def _():
            dg_ref[...] = jnp.zeros_like(dg_ref)

        for rows in _row_chunks(tm):
            dyv = result(rows).astype(F32)
            xf = x_ref[rows, :]
            r = lax.rsqrt(jnp.mean(xf * xf, axis=-1, keepdims=True) + RMS_EPS)
            xhat = xf * r
            dxh = dyv * g_ref[...]
            dx = r * (dxh - xhat * jnp.mean(dxh * xhat, axis=-1, keepdims=True)) + dres_ref[rows, :].astype(F32)
            dx_ref[rows, :] = dx.astype(out_dtype)
            dg_ref[...] += jnp.sum(dyv * xhat, axis=0, keepdims=True)

    l, d = x.shape
    row = lambda i, j, k: (i, 0)
    vec = lambda i, j, k: (0, 0)
    return _Hook(fn, ins=[x, g, dres], in_specs=[((tm, d), row), ((1, d), vec), ((tm, d), row)],
                 outs=[((l, d), out_dtype, (tm, d), row), ((1, d), F32, (1, d), vec)])


def _final_loss_hook(g, target, tm):
    l, d = target.shape

    def fn(result, ins, outs, i, j):
        g_ref, t_ref = ins
        dxb_ref, dg_ref, loss_ref = outs
        gv = g_ref[...]

        @pl.when(i == 0)
        def _():
            dg_ref[...] = jnp.zeros_like(dg_ref)
            loss_ref[...] = jnp.zeros_like(loss_ref)

        for rows in _row_chunks(tm):
            xf = result(rows)
            r = lax.rsqrt(jnp.mean(xf * xf, axis=-1, keepdims=True) + RMS_EPS)
            xhat = xf * r
            diff = xhat * gv - t_ref[rows, :]
            dout = diff * (1.0 / d)
            dxh = dout * gv
            dx = r * (dxh - xhat * jnp.mean(dxh * xhat, axis=-1, keepdims=True))
            dxb_ref[rows, :] = dx.astype(BF16)
            dg_ref[...] += jnp.sum(dout * xhat, axis=0, keepdims=True)
            part = jnp.sum(jnp.mean(diff * diff, axis=-1, keepdims=True), axis=0, keepdims=True)
            loss_ref[...] += 0.5 * part

    row = lambda i, j, k: (i, 0)
    vec = lambda i, j, k: (0, 0)
    return _Hook(fn, ins=[g, target], in_specs=[((1, d), vec), ((tm, d), row)],
                 outs=[((l, d), BF16, (tm, d), row), ((1, d), F32, (1, d), vec), ((1, 1), F32, (1, 1), vec)])


Q_PER_KV = 8
GROUP_ROWS = Q_PER_KV * BLOCK


def _attn_masks(n, rows=GROUP_ROWS):
    q_idx = lax.broadcasted_iota(jnp.int32, (rows, 2 * BLOCK), 0) & (BLOCK - 1)
    s_idx = lax.broadcasted_iota(jnp.int32, (rows, 2 * BLOCK), 1)
    dist = q_idx + BLOCK - s_idx
    valid = (dist >= 0) & (dist < BLOCK) & ((n > 0) | (s_idx >= BLOCK))
    return dist.astype(F32), valid


def _dup_half(t, kv_head, lo):
    rolled = pltpu.roll(t, HEAD_DIM, axis=1)
    return jnp.where(lo, t, rolled) if kv_head == 0 else jnp.where(lo, rolled, t)


def _stack_heads(ref, kv_head, lo):
    pieces = []
    for r in range(Q_PER_KV):
        pair = kv_head * 4 + r // 2
        t = ref[:, pair * 128:(pair + 1) * 128].astype(BF16)
        sel = lo if r % 2 == 0 else jnp.logical_not(lo)
        pieces.append(jnp.where(sel, t, jnp.zeros_like(t)))
    return jnp.concatenate(pieces, axis=0)


def _unstack_heads(t, lo):
    return [jnp.where(lo, t[(2 * i) * BLOCK:(2 * i + 1) * BLOCK], t[(2 * i + 1) * BLOCK:(2 * i + 2) * BLOCK])
            for i in range(Q_PER_KV // 2)]


def _per_head_column(values):
    return jnp.concatenate([jnp.full((BLOCK, 1), v, F32) for v in values], axis=0)


def _group_probs(qm, kdup, dist, valid, sink_ref, kv_head):
    heads = [kv_head * Q_PER_KV + r for r in range(Q_PER_KV)]
    slope = _per_head_column([2.0 ** (-8.0 * (h + 1) / N_Q_HEADS) for h in heads])
    sink = _per_head_column([sink_ref[h] for h in heads])
    return _probs(qm, kdup, dist, valid, sink, slope)


def _probs(qm, kdup, dist, valid, sink, slope):
    s = lax.dot_general(qm, kdup, (((1,), (1,)), ((), ())), preferred_element_type=F32)
    s = s * (HEAD_DIM ** -0.5) - slope * dist
    s = jnp.where(valid, s, NEG_BIG)
    m = jnp.maximum(jnp.max(s, axis=-1, keepdims=True), sink)
    p = jnp.exp(s - m)
    esink = jnp.exp(sink - m)
    inv = 1.0 / (jnp.sum(p, axis=-1, keepdims=True) + esink)
    return p * inv, esink * inv


def _attn_fwd(proj, sinks, *, name):
    l = proj.shape[0]
    nb = l // BLOCK

    def body(sink_ref, q_ref, kc_ref, kp_ref, vc_ref, vp_ref, o_ref):
        n = pl.program_id(0)
        dist, valid = _attn_masks(n, BLOCK)
        lo = lax.broadcasted_iota(jnp.int32, (1, BLOCK), 1) < HEAD_DIM
        kx = jnp.concatenate([kp_ref[...], kc_ref[...]], axis=0).astype(BF16)
        vx = jnp.concatenate([vp_ref[...], vc_ref[...]], axis=0).astype(BF16)
        for kv_head in range(2):
            kdup = _dup_half(kx, kv_head, lo)
            vdup = _dup_half(vx, kv_head, lo)
            for pr in range(4):
                pair = kv_head * 4 + pr
                qp = q_ref[:, pair * 128:(pair + 1) * 128].astype(BF16)
                o_pair = jnp.zeros((BLOCK, 128), F32)
                for half in range(2):
                    head = 2 * pair + half
                    sel = lo if half == 0 else jnp.logical_not(lo)
                    qm = jnp.where(sel, qp, jnp.zeros_like(qp))
                    p, _ = _probs(qm, kdup, dist, valid, sink_ref[head], 2.0 ** (-8.0 * (head + 1) / N_Q_HEADS))
                    o = jnp.dot(p.astype(BF16), vdup, preferred_element_type=F32)
                    o_pair = o_pair + jnp.where(sel, o, 0.0)
                o_ref[:, pair * 128:(pair + 1) * 128] = o_pair.astype(BF16)

    kv = lambda col, prev: pl.BlockSpec(
        (BLOCK, KV_WIDTH), (lambda n: (jnp.maximum(n - 1, 0), col)) if prev else (lambda n: (n, col)))
    return pl.pallas_call(
        body, grid=(nb,),
        in_specs=[pl.BlockSpec(memory_space=pltpu.SMEM),
                  pl.BlockSpec((BLOCK, ATTN_WIDTH), lambda n: (n, 0)),
                  kv(COL_K, False), kv(COL_K, True), kv(COL_V, False), kv(COL_V, True)],
        out_specs=pl.BlockSpec((BLOCK, ATTN_WIDTH), lambda n: (n, 0)),
        out_shape=jax.ShapeDtypeStruct((l, ATTN_WIDTH), BF16), name=name,
        compiler_params=_cp(("parallel",)))(sinks, proj, proj, proj, proj, proj)


def _attn_bwd(proj, sinks, dattn, *, comm=None, name):
    l = proj.shape[0]
    nb = l // BLOCK

    def body(sink_ref, q_ref, kc_ref, kp_ref, vc_ref, vp_ref, do_ref,
             dq_ref, dkc_ref, dkp_ref, dvc_ref, dvp_ref, dsink_ref):
        n = pl.program_id(0)
        dist, valid = _attn_masks(n)
        lane = lax.broadcasted_iota(jnp.int32, (1, BLOCK), 1)
        lo = lane < HEAD_DIM
        kx = jnp.concatenate([kp_ref[...], kc_ref[...]], axis=0).astype(BF16)
        vx = jnp.concatenate([vp_ref[...], vc_ref[...]], axis=0).astype(BF16)
        dsink = jnp.zeros((1, BLOCK), F32)
        dk_heads, dv_heads = [], []
        for kv_head in range(2):
            kdup = _dup_half(kx, kv_head, lo)
            vdup = _dup_half(vx, kv_head, lo)
            qm = _stack_heads(q_ref, kv_head, lo)
            dom = _stack_heads(do_ref, kv_head, lo)
            p, psink = _group_probs(qm, kdup, dist, valid, sink_ref, kv_head)
            dp = lax.dot_general(dom, vdup, (((1,), (1,)), ((), ())), preferred_element_type=F32)
            delta = jnp.sum(p * dp, axis=-1, keepdims=True)
            ds = (p * (dp - delta) * (HEAD_DIM ** -0.5)).astype(BF16)
            dsink_rows = -psink * delta
            for r in range(Q_PER_KV):
                part = jnp.sum(dsink_rows[r * BLOCK:(r + 1) * BLOCK])
                dsink = dsink + jnp.where(lane == kv_head * Q_PER_KV + r, part, 0.0)
            dq = jnp.dot(ds, kdup, preferred_element_type=F32)
            for i, dq_pair in enumerate(_unstack_heads(dq, lo)):
                pair = kv_head * 4 + i
                dq_ref[:, pair * 128:(pair + 1) * 128] = dq_pair.astype(BF16)
            dk_acc = lax.dot_general(ds, qm, (((0,), (0,)), ((), ())), preferred_element_type=F32)
            dv_acc = lax.dot_general(p.astype(BF16), dom, (((0,), (0,)), ((), ())), preferred_element_type=F32)
            dk_heads.append(dk_acc + pltpu.roll(dk_acc, HEAD_DIM, axis=1))
            dv_heads.append(dv_acc + pltpu.roll(dv_acc, HEAD_DIM, axis=1))
        dk = jnp.where(lo, dk_heads[0], dk_heads[1])
        dv = jnp.where(lo, dv_heads[0], dv_heads[1])
        dkp_ref[...] = dk[:BLOCK]
        dkc_ref[...] = dk[BLOCK:]
        dvp_ref[...] = dv[:BLOCK]
        dvc_ref[...] = dv[BLOCK:]

        @pl.when(n == 0)
        def _():
            dsink_ref[...] = jnp.zeros_like(dsink_ref)

        dsink_ref[...] += dsink

    kv = lambda col, prev: pl.BlockSpec(
        (BLOCK, KV_WIDTH), (lambda n: (jnp.maximum(n - 1, 0), col)) if prev else (lambda n: (n, col)))
    qspec = pl.BlockSpec((BLOCK, ATTN_WIDTH), lambda n: (n, 0))
    kvout = pl.BlockSpec((BLOCK, KV_WIDTH), lambda n: (n, 0))
    kvshape = jax.ShapeDtypeStruct((l, KV_WIDTH), F32)
    return _hosted_call(
        body, grid=(nb,),
        in_specs=[pl.BlockSpec(memory_space=pltpu.SMEM), qspec,
                  kv(COL_K, False), kv(COL_K, True), kv(COL_V, False), kv(COL_V, True), qspec],
        out_specs=[qspec, kvout, kvout, kvout, kvout, pl.BlockSpec((1, BLOCK), lambda n: (0, 0))],
        out_shape=[jax.ShapeDtypeStruct((l, ATTN_WIDTH), BF16), kvshape, kvshape, kvshape, kvshape,
                   jax.ShapeDtypeStruct((1, BLOCK), F32)],
        scratch_shapes=[], sem=("arbitrary",), name=name,
        args=(sinks, proj, proj, proj, proj, proj, dattn), comm=comm)


def _discretize(a_re, a_im, log_dt, b_re, b_im):
    dt = jnp.exp(log_dt)
    mag = jnp.exp(a_re * dt)
    ab_re = mag * jnp.cos(a_im * dt)
    ab_im = mag * jnp.sin(a_im * dt)
    nr = ab_re - 1.0
    ni = ab_im
    den = a_re * a_re + a_im * a_im
    z_re = (nr * a_re + ni * a_im) / den
    z_im = (ni * a_re - nr * a_im) / den
    bb_re = z_re * b_re - z_im * b_im
    bb_im = z_re * b_im + z_im * b_re
    return ab_re, ab_im, bb_re, bb_im


def _ssm_disc_fwd(a_re, a_im, log_dt, b_re, b_im, *, name):
    def body(ar, ai, ld, br, bi, o_ar, o_ai, o_br, o_bi):
        r = _discretize(ar[...], ai[...], ld[...], br[...], bi[...])
        o_ar[...], o_ai[...], o_br[...], o_bi[...] = r

    col = jax.ShapeDtypeStruct(a_re.shape, F32)
    mat = jax.ShapeDtypeStruct(b_re.shape, F32)
    return pl.pallas_call(body, out_shape=[col, col, mat, mat], name=name)(a_re, a_im, log_dt, b_re, b_im)


def _ssm_disc_bwd(a_re, a_im, log_dt, b_re, b_im, d_ab_re, d_ab_im, d_bb_re, d_bb_im, *, name):
    def body(ar, ai, ld, br, bi, g0, g1, g2, g3, o_ar, o_ai, o_ld, o_br, o_bi):
        _, vjp = jax.vjp(_discretize, ar[...], ai[...], ld[...], br[...], bi[...])
        r = vjp((g0[...], g1[...], g2[...], g3[...]))
        o_ar[...], o_ai[...], o_ld[...], o_br[...], o_bi[...] = r

    col = jax.ShapeDtypeStruct(a_re.shape, F32)
    mat = jax.ShapeDtypeStruct(b_re.shape, F32)
    return pl.pallas_call(body, out_shape=[col, col, col, mat, mat], name=name)(
        a_re, a_im, log_dt, b_re, b_im, d_ab_re, d_ab_im, d_bb_re, d_bb_im)


def _shift_rows(x, d, rows, *, down):
    t = x.shape[0]
    if down:
        return jnp.where(rows >= d, pltpu.roll(x, d, axis=0), 0.0)
    return jnp.where(rows < t - d, pltpu.roll(x, t - d, axis=0), 0.0)


def _scan_chunk(xr, xi, ar, ai, *, down):
    t = xr.shape[0]
    rows = lax.broadcasted_iota(jnp.int32, (t, 1), 0)
    pr, pi = ar, ai
    d = 1
    while d < t:
        sr = _shift_rows(xr, d, rows, down=down)
        si = _shift_rows(xi, d, rows, down=down)
        xr, xi = xr + pr * sr - pi * si, xi + pr * si + pi * sr
        pr, pi = pr * pr - pi * pi, 2.0 * pr * pi
        d *= 2
    return xr, xi


def _ssm_fwd(proj, ab, bd, cd, dskip, *, comm=None, name):
    l = proj.shape[0]
    t = min(SSM_CHUNK, l)
    nc = l // t

    def body(u_ref, ab_ref, bd_ref, cd_ref, ds_ref, y_ref, gy_ref, xs_ref, carry_ref):
        c = pl.program_id(1)

        @pl.when(c == 0)
        def _():
            carry_ref[...] = jnp.zeros_like(carry_ref)

        u = u_ref[...]
        ar, ai = ab_ref[0, 0:1, :], ab_ref[0, 1:2, :]
        bu = jnp.dot(u.astype(BF16), bd_ref[0], preferred_element_type=F32)
        rows = lax.broadcasted_iota(jnp.int32, (t, 1), 0)
        cr, ci = carry_ref[0:1, :], carry_ref[1:2, :]
        xr = bu[:, :SSM_X_BLK] + jnp.where(rows == 0, ar * cr - ai * ci, 0.0)
        xi = bu[:, SSM_X_BLK:] + jnp.where(rows == 0, ar * ci + ai * cr, 0.0)
        xr, xi = _scan_chunk(xr, xi, ar, ai, down=True)
        xs_ref[0, :, :SSM_X_BLK] = xr
        xs_ref[0, :, SSM_X_BLK:] = xi
        carry_ref[0:1, :] = xs_ref[0, t - 1:t, :SSM_X_BLK]
        carry_ref[1:2, :] = xs_ref[0, t - 1:t, SSM_X_BLK:]
        y = jnp.dot(xs_ref[0].astype(BF16), cd_ref[0], preferred_element_type=F32) + ds_ref[...] * u
        y_ref[...] = y
        gy_ref[...] = _gelu(y).astype(BF16)

    blk = lambda shape: pl.BlockSpec((1,) + shape, lambda j, c: (j, 0, 0))
    ycol = pl.BlockSpec((t, SSM_U_BLK), lambda j, c: (c, j))
    return _hosted_call(
        body, grid=(SSM_SPLIT, nc),
        in_specs=[pl.BlockSpec((t, SSM_U_BLK), lambda j, c: (c, COL_U + j)),
                  blk((2, SSM_X_BLK)), blk((SSM_U_BLK, 2 * SSM_X_BLK)), blk((2 * SSM_X_BLK, SSM_U_BLK)),
                  pl.BlockSpec((1, SSM_U_BLK), lambda j, c: (0, j))],
        out_specs=[ycol, ycol, pl.BlockSpec((1, t, 2 * SSM_X_BLK), lambda j, c: (j, c, 0))],
        out_shape=[jax.ShapeDtypeStruct((l, SSM_WIDTH), F32), jax.ShapeDtypeStruct((l, SSM_WIDTH), BF16),
                   jax.ShapeDtypeStruct((SSM_SPLIT, l, 2 * SSM_X_BLK), F32)],
        scratch_shapes=[pltpu.VMEM((2, SSM_X_BLK), F32)], sem=("parallel", "arbitrary"), name=name,
        args=(proj, ab, bd, cd, dskip), comm=comm)


def _ssm_bwd(proj, y, dgy, xs, ab, bdt, cdt, dskip, *, comm=None, name):
    l = proj.shape[0]
    t = min(SSM_CHUNK, l)
    nc = l // t

    def body(u_ref, y_ref, dgy_ref, xs_ref, halo_ref, ab_ref, bdt_ref, cdt_ref, ds_ref,
             du_ref, dbd_ref, dcd_ref, dab_ref, dd_ref, carry_ref):
        c = pl.program_id(1)
        ci_ = nc - 1 - c

        @pl.when(c == 0)
        def _():
            carry_ref[...] = jnp.zeros_like(carry_ref)
            dbd_ref[...] = jnp.zeros_like(dbd_ref)
            dcd_ref[...] = jnp.zeros_like(dcd_ref)
            dab_ref[...] = jnp.zeros_like(dab_ref)
            dd_ref[...] = jnp.zeros_like(dd_ref)

        u = u_ref[...]
        dy = dgy_ref[...] * _gelu_grad(y_ref[...])
        dyb = dy.astype(BF16)
        ar, ai = ab_ref[0, 0:1, :], ab_ref[0, 1:2, :]
        g = jnp.dot(dyb, cdt_ref[0], preferred_element_type=F32)
        rows = lax.broadcasted_iota(jnp.int32, (t, 1), 0)
        cr, ci = carry_ref[0:1, :], carry_ref[1:2, :]
        lr = g[:, :SSM_X_BLK] + jnp.where(rows == t - 1, ar * cr + ai * ci, 0.0)
        li = g[:, SSM_X_BLK:] + jnp.where(rows == t - 1, ar * ci - ai * cr, 0.0)
        lr, li = _scan_chunk(lr, li, ar, -ai, down=False)
        lam = jnp.concatenate([lr, li], axis=1)
        carry_ref[0:1, :] = lr[0:1, :]
        carry_ref[1:2, :] = li[0:1, :]
        lamb = lam.astype(BF16)
        du_ref[...] = (jnp.dot(lamb, bdt_ref[0], preferred_element_type=F32) + ds_ref[...] * dy).astype(BF16)
        dbd_ref[0] += lax.dot_general(u.astype(BF16), lamb, (((0,), (0,)), ((), ())),
                                      preferred_element_type=F32)
        xs = xs_ref[0]
        dcd_ref[0] += lax.dot_general(xs.astype(BF16), dyb, (((0,), (0,)), ((), ())),
                                      preferred_element_type=F32)
        halo = jnp.where(ci_ > 0, halo_ref[0, 7:8, :], 0.0)
        xprev = jnp.where(rows == 0, halo, pltpu.roll(xs, 1, axis=0))
        xpr, xpi = xprev[:, :SSM_X_BLK], xprev[:, SSM_X_BLK:]
        dab_ref[0, 0:1, :] += jnp.sum(lr * xpr + li * xpi, axis=0, keepdims=True)
        dab_ref[0, 1:2, :] += jnp.sum(li * xpr - lr * xpi, axis=0, keepdims=True)
        dd_ref[...] += jnp.sum(dy * u, axis=0, keepdims=True)

    blk = lambda shape: pl.BlockSpec((1,) + shape, lambda j, c: (j, 0, 0))
    rev = lambda j, c: (nc - 1 - c, j)
    ycol = pl.BlockSpec((t, SSM_U_BLK), rev)
    hb = t // 8
    return _hosted_call(
        body, grid=(SSM_SPLIT, nc), comm=comm, sem=("parallel", "arbitrary"), name=name,
        args=(proj, y, dgy, xs, xs, ab, bdt, cdt, dskip), scratch_shapes=[pltpu.VMEM((2, SSM_X_BLK), F32)],
        in_specs=[pl.BlockSpec((t, SSM_U_BLK), lambda j, c: (nc - 1 - c, COL_U + j)), ycol, ycol,
                  pl.BlockSpec((1, t, 2 * SSM_X_BLK), lambda j, c: (j, nc - 1 - c, 0)),
                  pl.BlockSpec((1, 8, 2 * SSM_X_BLK),
                               lambda j, c: (j, jnp.maximum((nc - 1 - c) * hb - 1, 0), 0)),
                  blk((2, SSM_X_BLK)), blk((2 * SSM_X_BLK, SSM_U_BLK)), blk((SSM_U_BLK, 2 * SSM_X_BLK)),
                  pl.BlockSpec((1, SSM_U_BLK), lambda j, c: (0, j))],
        out_specs=[ycol, blk((SSM_U_BLK, 2 * SSM_X_BLK)), blk((2 * SSM_X_BLK, SSM_U_BLK)),
                   blk((2, SSM_X_BLK)), pl.BlockSpec((1, SSM_U_BLK), lambda j, c: (0, j))],
        out_shape=[jax.ShapeDtypeStruct((l, SSM_WIDTH), BF16),
                   jax.ShapeDtypeStruct((SSM_SPLIT, SSM_U_BLK, 2 * SSM_X_BLK), F32),
                   jax.ShapeDtypeStruct((SSM_SPLIT, 2 * SSM_X_BLK, SSM_U_BLK), F32),
                   jax.ShapeDtypeStruct((SSM_SPLIT, 2, SSM_X_BLK), F32),
                   jax.ShapeDtypeStruct((1, SSM_WIDTH), F32)])


def _block_diag(t):
    s, g, a, b = t.shape
    return jnp.einsum('sgab,gk->sgakb', t, jnp.eye(g, dtype=t.dtype)).reshape(s, g * a, g * b)


def _block_diag_take(t, a, b):
    s = t.shape[0]
    return jnp.einsum('sgakb,gk->sgab', t.reshape(s, 8, a, 8, b), jnp.eye(8, dtype=t.dtype))


def _glu_fwd_hook(l, tm):
    def fn(result, ins, outs, i, j):
        z = result()
        outs[0][...] = z
        outs[1][...] = (z[:, :SSM_WIDTH] * _sigmoid(z[:, SSM_WIDTH:])).astype(BF16)

    row = lambda i, j, k: (i, 0)
    return _Hook(fn, outs=[((l, 2 * SSM_WIDTH), F32, (tm, 2 * SSM_WIDTH), row),
                           ((l, SSM_WIDTH), BF16, (tm, SSM_WIDTH), row)])


def _glu_bwd_hook(z, tm):
    l = z.shape[0]

    def fn(result, ins, outs, i, j):
        zv_ref, zg_ref = ins
        dz_ref, db_ref = outs
        d = result()
        sg = _sigmoid(zg_ref[...])
        dv = d * sg
        dg = d * zv_ref[...] * sg * (1.0 - sg)
        dz_ref[:, :SSM_WIDTH] = dv.astype(BF16)
        dz_ref[:, SSM_WIDTH:] = dg.astype(BF16)

        @pl.when(i == 0)
        def _():
            db_ref[...] = jnp.zeros_like(db_ref)

        db_ref[:, :SSM_WIDTH] += jnp.sum(dv, axis=0, keepdims=True)
        db_ref[:, SSM_WIDTH:] += jnp.sum(dg, axis=0, keepdims=True)

    half = (tm, SSM_WIDTH)
    return _Hook(fn, ins=[z, z], in_specs=[(half, lambda i, j, k: (i, 0)), (half, lambda i, j, k: (i, 1))],
                 outs=[((l, 2 * SSM_WIDTH), BF16, (tm, 2 * SSM_WIDTH), lambda i, j, k: (i, 0)),
                       ((1, 2 * SSM_WIDTH), F32, (1, 2 * SSM_WIDTH), lambda i, j, k: (0, 0))])


GATE_TC = 256


def _merge_fwd(proj, a, s, *, name):
    l = a.shape[0]
    tr = min(2048, l)

    def body(ga_ref, gs_ref, a_ref, s_ref, o_ref):
        o_ref[...] = (_sigmoid(ga_ref[...]) * a_ref[...].astype(F32)
                      + _sigmoid(gs_ref[...]) * s_ref[...].astype(F32)).astype(BF16)

    own = pl.BlockSpec((tr, GATE_TC), lambda i, j: (i, j))
    return pl.pallas_call(
        body, grid=(l // tr, D_MODEL // GATE_TC),
        in_specs=[pl.BlockSpec((tr, GATE_TC), lambda i, j: (i, COL_GA // 2 + j)),
                  pl.BlockSpec((tr, GATE_TC), lambda i, j: (i, COL_GS // 2 + j)), own, own],
        out_specs=own, out_shape=jax.ShapeDtypeStruct((l, D_MODEL), BF16), name=name,
        compiler_params=_cp(("parallel", "parallel")))(proj, proj, a, s)


def _merge_bwd_hook(proj, a, s, tm):
    def fn(result, ins, outs, i, j):
        ga_ref, gs_ref, a_br, s_br = ins
        d = result()
        sa = _sigmoid(ga_ref[...])
        ss = _sigmoid(gs_ref[...])
        outs[0][...] = (d * sa).astype(BF16)
        outs[1][...] = (d * ss).astype(BF16)
        outs[2][...] = (d * a_br[...].astype(F32) * sa * (1.0 - sa)).astype(BF16)
        outs[3][...] = (d * s_br[...].astype(F32) * ss * (1.0 - ss)).astype(BF16)

    blk = (tm, GATE_TC)
    own = lambda i, j, k: (i, j)
    return _Hook(fn, ins=[proj, proj, a, s],
                 in_specs=[(blk, lambda i, j, k: (i, COL_GA // 2 + j)), (blk, lambda i, j, k: (i, COL_GS // 2 + j)),
                           (blk, own), (blk, own)],
                 outs=[(a.shape, BF16, blk, own)] * 4)


FF_TC = D_FF // 2
FF_NJ = 2
FF_ROWS = 512


FF_HALO = 16


def _conv_taps(ext, rows):
    h = FF_HALO
    return (ext[h:h + rows], pltpu.roll(ext, 1, axis=0)[h:h + rows], pltpu.roll(ext, 2, axis=0)[h:h + rows])


def _ff_specs(tr, l):
    hb = tr // FF_HALO
    last = l // FF_HALO - 1
    prev = lambda i: jnp.maximum(i * hb - 1, 0)
    nxt = lambda i: jnp.minimum((i + 1) * hb, last)
    return dict(
        own=pl.BlockSpec((tr, FF_TC), lambda j, i: (i, j)),
        own_next=pl.BlockSpec((FF_HALO, FF_TC), lambda j, i: (nxt(i), j)),
        val=pl.BlockSpec((tr, FF_TC), lambda j, i: (i, 2 * j)),
        val_next=pl.BlockSpec((FF_HALO, FF_TC), lambda j, i: (nxt(i), 2 * j)),
        gate=pl.BlockSpec((tr, FF_TC), lambda j, i: (i, 2 * j + 1)),
        gate_prev=pl.BlockSpec((FF_HALO, FF_TC), lambda j, i: (prev(i), 2 * j + 1)),
        gate_next=pl.BlockSpec((FF_HALO, FF_TC), lambda j, i: (nxt(i), 2 * j + 1)),
        pair=pl.BlockSpec((tr, 2 * FF_TC), lambda j, i: (i, j)),
        w=pl.BlockSpec((3, FF_TC), lambda j, i: (0, j)),
        b=pl.BlockSpec((1, FF_TC), lambda j, i: (0, j)))


def _ffn_act_fwd(up, conv_w, conv_b, *, name):
    l = up.shape[0]
    tr = min(FF_ROWS, l)

    def body(v_ref, g_ref, prev_ref, w_ref, b_ref, o_ref):
        prev = jnp.where(pl.program_id(1) == 0, 0.0, prev_ref[...].astype(F32))
        g0, g1, g2 = _conv_taps(jnp.concatenate([prev, g_ref[...].astype(F32)], axis=0), tr)
        gc = b_ref[...] + w_ref[0:1, :] * g2 + w_ref[1:2, :] * g1 + w_ref[2:3, :] * g0
        o_ref[...] = (v_ref[...].astype(F32) * _gelu(gc)).astype(BF16)

    sp = _ff_specs(tr, l)
    return pl.pallas_call(
        body, grid=(FF_NJ, l // tr), in_specs=[sp['val'], sp['gate'], sp['gate_prev'], sp['w'], sp['b']],
        out_specs=sp['own'], out_shape=jax.ShapeDtypeStruct((l, D_FF), BF16), name=name,
        compiler_params=_cp(("parallel", "parallel")))(up, up, up, conv_w, conv_b)


def _ffn_act_bwd(dact, up, conv_w, conv_b, *, comm=None, name):
    l = up.shape[0]
    tr = min(FF_ROWS, l)
    ni = l // tr
    te = tr + 8

    def body(d_ref, dn_ref, v_ref, vn_ref, g_ref, gp_ref, gn_ref, w_ref, b_ref, dup_ref, dw_ref, db_ref):
        i = pl.program_id(1)
        f32 = lambda ref, rows=None: ref[...].astype(F32)[:rows]
        prev = jnp.where(i == 0, 0.0, f32(gp_ref))
        g0, g1, g2 = _conv_taps(jnp.concatenate([prev, f32(g_ref), f32(gn_ref, 8)], axis=0), te)
        w0, w1, w2 = w_ref[0:1, :], w_ref[1:2, :], w_ref[2:3, :]
        gc = b_ref[...] + w0 * g2 + w1 * g1 + w2 * g0
        d_own = f32(d_ref)
        d = jnp.concatenate([d_own, jnp.where(i == ni - 1, 0.0, f32(dn_ref, 8))], axis=0)
        v = jnp.concatenate([f32(v_ref), f32(vn_ref, 8)], axis=0)
        dgc = d * v * _gelu_grad(gc)
        ahead1 = pltpu.roll(dgc, te - 1, axis=0)[:tr]
        ahead2 = pltpu.roll(dgc, te - 2, axis=0)[:tr]
        own = dgc[:tr]
        dup_ref[:, :FF_TC] = (d_own * _gelu(gc[:tr])).astype(BF16)
        dup_ref[:, FF_TC:] = (w2 * own + w1 * ahead1 + w0 * ahead2).astype(BF16)

        @pl.when(i == 0)
        def _():
            dw_ref[...] = jnp.zeros_like(dw_ref)
            db_ref[...] = jnp.zeros_like(db_ref)

        dw_ref[0:1, :] += jnp.sum(own * g2[:tr], axis=0, keepdims=True)
        dw_ref[1:2, :] += jnp.sum(own * g1[:tr], axis=0, keepdims=True)
        dw_ref[2:3, :] += jnp.sum(own * g0[:tr], axis=0, keepdims=True)
        db_ref[...] += jnp.sum(own, axis=0, keepdims=True)

    sp = _ff_specs(tr, l)
    return _hosted_call(
        body, grid=(FF_NJ, ni),
        in_specs=[sp['own'], sp['own_next'], sp['val'], sp['val_next'], sp['gate'], sp['gate_prev'],
                  sp['gate_next'], sp['w'], sp['b']],
        out_specs=[sp['pair'], sp['w'], sp['b']],
        out_shape=[jax.ShapeDtypeStruct((l, 2 * D_FF), BF16), jax.ShapeDtypeStruct((3, D_FF), F32),
                   jax.ShapeDtypeStruct((1, D_FF), F32)],
        scratch_shapes=[], sem=("parallel", "arbitrary"), name=name,
        args=(dact, dact, up, up, up, up, up, conv_w, conv_b), comm=comm)


def _join_dproj(dq, dkc, dkp, dvc, dvp, du, dga, dgs, *, name):
    l = dq.shape[0]
    nb = l // BLOCK
    widths = [ATTN_WIDTH, KV_WIDTH, KV_WIDTH, SSM_WIDTH, D_MODEL, D_MODEL]

    def body(dq_ref, dkc_ref, dkp_ref, dvc_ref, dvp_ref, du_ref, dga_ref, dgs_ref, o_ref, s_ref):
        n = pl.program_id(0)

        @pl.when(n == 0)
        def _():
            s_ref[...] = jnp.zeros_like(s_ref)

        last = n == nb - 1
        dk = (dkc_ref[...] + jnp.where(last, 0.0, dkp_ref[...])).astype(BF16)
        dv = (dvc_ref[...] + jnp.where(last, 0.0, dvp_ref[...])).astype(BF16)
        col = 0
        for v, width in zip([dq_ref[...], dk, dv, du_ref[...], dga_ref[...], dgs_ref[...]], widths):
            o_ref[:, col:col + width] = v
            s_ref[:, col:col + width] += jnp.sum(v.astype(F32), axis=0, keepdims=True)
            col += width

    cur = lambda width: pl.BlockSpec((BLOCK, width), lambda n: (n, 0))
    nxt = pl.BlockSpec((BLOCK, KV_WIDTH), lambda n: (jnp.minimum(n + 1, nb - 1), 0))
    return pl.pallas_call(
        body, grid=(nb,),
        in_specs=[cur(ATTN_WIDTH), cur(KV_WIDTH), nxt, cur(KV_WIDTH), nxt, cur(SSM_WIDTH), cur(D_MODEL), cur(D_MODEL)],
        out_specs=[pl.BlockSpec((BLOCK, IN_COLS), lambda n: (n, 0)), pl.BlockSpec((1, IN_COLS), lambda n: (0, 0))],
        out_shape=[jax.ShapeDtypeStruct((l, IN_COLS), BF16), jax.ShapeDtypeStruct((1, IN_COLS), F32)],
        name=name, compiler_params=_cp(("arbitrary",)))(dq, dkc, dkp, dvc, dvp, du, dga, dgs)


def _local_step(x, target, wts, small, shards=None):
    l = x.shape[0]
    wts = dict(wts)
    grads, recvs, sgr = {}, {}, {}
    lay = lambda keys: [LAYOUT[k] for k in keys]
    none = lambda keys: None
    gather = (lambda keys: _GatherPlan([shards[k] for k in keys], lay(keys))) if shards is not None else none
    scatter = (lambda keys: _ScatterPlan([grads[k] for k in keys], lay(keys))) if shards is not None else none

    mm = _matmul

    def take(res, plan, keys, store):
        outs, couts = res
        if plan is not None:
            store.update(zip(keys, couts))
        return outs

    def mm_plan(plan, keys, store, *args, **kw):
        if plan is None:
            return _matmul(*args, **kw)
        return take(_matmul(*args, comm=plan, **kw), plan, keys, store)

    def mm_host(keys, make_plan, store, *args, **kw):
        return mm_plan(make_plan(keys), keys, store, *args, **kw)

    up_scatter = lambda p: _ScatterPlan([grads['w_up']], lay(['w_up']), part=(p, 2)) if shards is not None else None

    col = lambda t: t.reshape(SSM_GROUPS * SSM_STATE, 1)
    a_re, a_im = col(small['ssm_a_re']), col(small['ssm_a_im'])
    log_dt = jnp.repeat(small['ssm_log_dt'].reshape(SSM_GROUPS), SSM_STATE).reshape(-1, 1)
    b_re = small['ssm_b_re'].reshape(SSM_GROUPS * SSM_STATE, SSM_GROUP)
    b_im = small['ssm_b_im'].reshape(SSM_GROUPS * SSM_STATE, SSM_GROUP)
    ab_re, ab_im, bb_re, bb_im = _ssm_disc_fwd(a_re, a_im, log_dt, b_re, b_im, name="ssm_disc_fwd")
    ab = jnp.stack([ab_re.reshape(SSM_SPLIT, SSM_X_BLK), ab_im.reshape(SSM_SPLIT, SSM_X_BLK)], axis=1)
    to_bd = lambda t: _block_diag(t.reshape(SSM_SPLIT, 8, SSM_STATE, SSM_GROUP).transpose(0, 1, 3, 2))
    bd = jnp.concatenate([to_bd(bb_re), to_bd(bb_im)], axis=2)
    c_re = small['ssm_c_re'].reshape(SSM_SPLIT, 8, SSM_GROUP, SSM_STATE)
    c_im = small['ssm_c_im'].reshape(SSM_SPLIT, 8, SSM_GROUP, SSM_STATE)
    cdt = jnp.concatenate([_block_diag(c_re), -_block_diag(c_im)], axis=2)
    bd_b, cdt_b = bd.astype(BF16), cdt.astype(BF16)
    bdt_b, cd_b = bd_b.transpose(0, 2, 1), cdt_b.transpose(0, 2, 1)
    dskip = small['ssm_d'].reshape(1, SSM_WIDTH)

    sinks = small['attn_sinks'].reshape(N_Q_HEADS)
    plan = gather(['w_in_t'])
    h1, = take(_rms_fwd(x, small['attn_norm_g'], comm=plan, name="rms1_fwd"), plan, ['w_in_t'], wts)
    proj = mm_host(['w_glu', 'w_ba', 'w_bs', 'w_out'], gather, wts,
                   h1, wts['w_in_t'], tb=True, tm=512, tn=2944, tk=2048, inner='m', out_dtype=F32,
                   bias=small['b_in'], name="mm_in")
    attn = _attn_fwd(proj, sinks, name="attn_fwd")
    plan = gather(['w_up'])
    y, gy, xs = take(_ssm_fwd(proj, ab, bd_b, cd_b, dskip, comm=plan, name="ssm_fwd"), plan, ['w_up'], wts)
    z, ssm = mm(gy, wts['w_glu'], tm=1024, tn=1024, tk=512, bias=small['b_glu'],
                epilogue=_glu_fwd_hook(l, min(1024, l)), name="mm_glu")
    a_br = mm(attn, wts['w_ba'], tm=1024, tn=1024, tk=1024, out_dtype=BF16, name="mm_ba")
    s_br = mm(ssm, wts['w_bs'], tm=1024, tn=1024, tk=512, out_dtype=BF16, name="mm_bs")
    tr = min(ROW_TILE, l)
    merged = _merge_fwd(proj, a_br, s_br, name="merge_fwd")
    x2 = mm(merged, wts['w_out'], tm=1024, tn=1024, tk=2048, inner='m', out_dtype=F32, res=x, name="mm_out")
    h2, = take(_rms_fwd(x2, small['ffn_norm_g'], name="rms2_fwd"), None, [], wts)
    up = mm_host(['w_down'], gather, wts,
                 h2, wts['w_up'], tm=1024, tn=1024, tk=2048, out_dtype=BF16, name="mm_up")
    conv_w, conv_b = small['conv_w'], small['conv_b']
    act = _ffn_act_fwd(up, conv_w, conv_b, name="ffn_act_fwd")
    x3 = mm(act, wts['w_down'], tm=1024, tn=1024, tk=2816, out_dtype=F32, res=x2, name="mm_down")
    dx3b, d_g3, loss = _rowwise(
        _final_loss_hook(small['final_norm_g'].reshape(1, D_MODEL), target, tr), x3, tr, name="final_loss")

    sgr['final_norm_g'] = d_g3.reshape(D_MODEL)
    dact = mm(dx3b, wts['w_down'], tb=True, tm=512, tn=2816, tk=2048, inner='m', out_dtype=BF16, name="mm_dact")
    grads['w_down'] = mm(act, dx3b, ta=True, tm=1408, tn=1024, tk=2048, out_dtype=BF16, name="mm_dw_down")
    plan = scatter(['w_down'])
    dup, sgr['conv_w'], sgr['conv_b'] = take(
        _ffn_act_bwd(dact, up, conv_w, conv_b, comm=plan, name="ffn_act_bwd"), plan, ['w_down'], recvs)
    grads['w_up'] = mm(h2, dup, ta=True, tm=2048, tn=1024, tk=2048, out_dtype=BF16, name="mm_dw_up")
    dh2 = mm_plan(up_scatter(0), ['w_up#0'], recvs,
                  dup, wts['w_up'], tb=True, tm=1024, tn=1024, tk=2816, out_dtype=BF16, name="mm_dh2")
    dx2b, sgr['ffn_norm_g'] = _rowwise(
        _rms_bwd_hook(x2, small['ffn_norm_g'], dx3b, tr, BF16), dh2, tr, name="rms2_bwd")

    d_a, d_s, dga, dgs = mm(dx2b, wts['w_out'], tb=True, tm=1024, tn=GATE_TC, tk=2048,
                            epilogue=_merge_bwd_hook(proj, a_br, s_br, min(1024, l)), name="mm_dmerged")
    grads['w_out'] = mm(merged, dx2b, ta=True, tm=1024, tn=1024, tk=2048, out_dtype=BF16, name="mm_dw_out")
    dattn = mm(d_a, wts['w_ba'], tb=True, tm=1024, tn=1024, tk=2048, inner='m', out_dtype=BF16, name="mm_dattn")
    grads['w_ba'] = mm(attn, d_a, ta=True, tm=1024, tn=1024, tk=2048, out_dtype=BF16, name="mm_dw_ba")
    dz, sgr['b_glu'] = mm(d_s, wts['w_bs'], tb=True, tm=1024, tn=512, tk=2048, sequential=True,
                          epilogue=_glu_bwd_hook(z, min(1024, l)), name="mm_dssm")
    grads['w_bs'] = mm(ssm, d_s, ta=True, tm=512, tn=1024, tk=2048, out_dtype=BF16, name="mm_dw_bs")
    grads['w_glu'] = mm(gy, dz, ta=True, tm=512, tn=1024, tk=2048, out_dtype=BF16, name="mm_dw_glu")
    dgy = mm(dz, wts['w_glu'], tb=True, tm=1024, tn=512, tk=1024, inner='m', out_dtype=F32, name="mm_dgy")
    plan = up_scatter(1)
    du, d_bd, d_cd, d_ab, sgr['ssm_d'] = take(
        _ssm_bwd(proj, y, dgy, xs, ab, bdt_b, cdt_b, dskip, comm=plan, name="ssm_bwd"), plan, ['w_up#1'], recvs)
    keys = ['w_out', 'w_ba', 'w_bs', 'w_glu']
    plan = scatter(keys)
    dq, dkc, dkp, dvc, dvp, dsink = take(
        _attn_bwd(proj, sinks, dattn, comm=plan, name="attn_bwd"), plan, keys, recvs)
    sgr['attn_sinks'] = dsink[:, :N_Q_HEADS]
    dproj, sgr['b_in'] = _join_dproj(dq, dkc, dkp, dvc, dvp, du, dga, dgs, name="join_dproj")
    grads['w_in_t'] = mm(dproj, h1, ta=True, tm=2944, tn=1024, tk=1024, out_dtype=BF16, name="mm_dw_in")
    dh1 = mm_host(['w_in_t'], scatter, recvs,
                  dproj, wts['w_in_t'], tm=1024, tn=1024, tk=2944, out_dtype=BF16, name="mm_dh1")
    grad_x, sgr['attn_norm_g'] = _rowwise(
        _rms_bwd_hook(x, small['attn_norm_g'], dx2b, tr, F32), dh1, tr, name="rms1_bwd")

    from_bd = lambda t: _block_diag_take(t, SSM_GROUP, SSM_STATE).transpose(0, 1, 3, 2).reshape(
        SSM_GROUPS * SSM_STATE, SSM_GROUP)
    d_bb_re = from_bd(d_bd[:, :, :SSM_X_BLK])
    d_bb_im = from_bd(d_bd[:, :, SSM_X_BLK:])
    d_cdt = d_cd.transpose(0, 2, 1)
    shape_c = (1, SSM_GROUPS, SSM_GROUP, SSM_STATE)
    sgr['ssm_c_re'] = _block_diag_take(d_cdt[:, :, :SSM_X_BLK], SSM_GROUP, SSM_STATE).reshape(shape_c)
    sgr['ssm_c_im'] = -_block_diag_take(d_cdt[:, :, SSM_X_BLK:], SSM_GROUP, SSM_STATE).reshape(shape_c)
    d_a_re, d_a_im, d_ldt, d_b_re, d_b_im = _ssm_disc_bwd(
        a_re, a_im, log_dt, b_re, b_im, d_ab[:, 0, :].reshape(-1, 1), d_ab[:, 1, :].reshape(-1, 1),
        d_bb_re, d_bb_im, name="ssm_disc_bwd")
    sgr['ssm_a_re'] = d_a_re.reshape(1, SSM_GROUPS, SSM_STATE)
    sgr['ssm_a_im'] = d_a_im.reshape(1, SSM_GROUPS, SSM_STATE)
    sgr['ssm_log_dt'] = d_ldt.reshape(SSM_GROUPS, SSM_STATE).sum(axis=1).reshape(1, SSM_GROUPS)
    sgr['ssm_b_re'] = d_b_re.reshape(1, SSM_GROUPS, SSM_STATE, SSM_GROUP)
    sgr['ssm_b_im'] = d_b_im.reshape(1, SSM_GROUPS, SSM_STATE, SSM_GROUP)
    return loss, grad_x, grads, recvs, sgr


def _swap_cores(arrs, *, name):
    n = len(arrs)

    def body(*refs):
        ins, outs = refs[:n], refs[n:2 * n]
        send_sems, recv_sems = refs[2 * n:]
        x, y, c = _place()
        copies = []
        for i in range(n):
            cp = pltpu.make_async_remote_copy(
                src_ref=ins[i], dst_ref=outs[i], send_sem=send_sems.at[i], recv_sem=recv_sems.at[i],
                device_id=(x, y, 1 - c), device_id_type=MESH)
            cp.start()
            copies.append(cp)
        for cp in copies:
            cp.wait()

    return pl.pallas_call(
        body, in_specs=[ANY] * n, out_specs=[ANY] * n,
        out_shape=[jax.ShapeDtypeStruct(a.shape, a.dtype) for a in arrs],
        scratch_shapes=[pltpu.SemaphoreType.DMA((n,)), pltpu.SemaphoreType.DMA((n,))],
        name=name)(*arrs)


def _all_reduce_small(buf, *, name):
    r = buf.shape[0]

    def body(in_ref, out_ref, slots, send_sems, recv_sems):
        x, y, c = _place()
        me = 4 * x + 2 * y + c
        slots[pl.ds(me, 1)] = in_ref[...][None]
        copies = []
        for k in range(N_DEV - 1):
            bx, by, bc = ((k + 1) >> 2) & 1, ((k + 1) >> 1) & 1, (k + 1) & 1
            peer = (1 - x if bx else x, 1 - y if by else y, 1 - c if bc else c)
            cp = pltpu.make_async_remote_copy(
                src_ref=in_ref, dst_ref=slots.at[me], send_sem=send_sems.at[k], recv_sem=recv_sems.at[k],
                device_id=peer, device_id_type=MESH)
            cp.start()
            copies.append(cp)
        for cp in copies:
            cp.wait()
        acc = slots[0]
        for d in range(1, N_DEV):
            acc = acc + slots[d]
        out_ref[...] = acc

    vm = pl.BlockSpec(memory_space=pltpu.VMEM)
    return pl.pallas_call(
        body, in_specs=[vm], out_specs=vm, out_shape=jax.ShapeDtypeStruct((r, 128), F32),
        scratch_shapes=[pltpu.VMEM((N_DEV, r, 128), F32), pltpu.SemaphoreType.DMA((N_DEV - 1,)),
                        pltpu.SemaphoreType.DMA((N_DEV - 1,))],
        name=name)(buf)


def _pack(arrs):
    flat = jnp.concatenate([a.reshape(-1).astype(F32) for a in arrs])
    pad = (-flat.shape[0]) % 1024
    return jnp.pad(flat, (0, pad)).reshape(-1, 128)


def _unpack(buf, shapes):
    flat = buf.reshape(-1)
    out, pos = [], 0
    for s in shapes:
        size = math.prod(s)
        out.append(flat[pos:pos + size].reshape(s))
        pos += size
    return out


TILE_ELEMS = 256 * 1024


def _tile_rows(r, c):
    if r * c <= TILE_ELEMS:
        return r
    for tr in range(TILE_ELEMS // c // 16 * 16, 0, -16):
        if r % tr == 0:
            return tr
    raise ValueError((r, c))


def _sum4(full, axis, pos, recvs, *, name):
    r, c = _block_shape(full.shape, axis)
    parts = len(recvs)
    tr = _tile_rows(r // parts, c)
    per = r // parts // tr
    nt = r // tr

    def body(pos_ref, o_ref, *refs):
        out_ref = refs[parts]
        for p in range(parts):
            @pl.when(pl.program_id(0) // per == p)
            def _():
                acc = o_ref[...].astype(F32)
                for k in range(3):
                    acc = acc + refs[p][k].astype(F32)
                out_ref[...] = acc.astype(BF16)

    own = (pl.BlockSpec((tr, c), lambda i, pos_ref: (pos_ref[0] * nt + i, 0)) if axis == 0
           else pl.BlockSpec((tr, c), lambda i, pos_ref: (i, pos_ref[0])))
    part_spec = lambda p: pl.BlockSpec((3, tr, c), lambda i, pos_ref: (0, jnp.clip(i - p * per, 0, per - 1), 0))
    grid_spec = pltpu.PrefetchScalarGridSpec(
        num_scalar_prefetch=1, grid=(nt,), in_specs=[own] + [part_spec(p) for p in range(parts)],
        out_specs=pl.BlockSpec((tr, c), lambda i, pos_ref: (i, 0)))
    return pl.pallas_call(
        body, grid_spec=grid_spec, out_shape=jax.ShapeDtypeStruct((r, c), BF16),
        name=name, compiler_params=_cp(("parallel",)))(pos, full, *recvs)


def _adam_step(w, g, m, v):
    bc1 = 1.0 - ADAM_B1 ** ADAM_STEP
    bc2 = 1.0 - ADAM_B2 ** ADAM_STEP
    mn = ADAM_B1 * m + (1.0 - ADAM_B1) * g
    vn = ADAM_B2 * v + (1.0 - ADAM_B2) * (g * g)
    m_hat = mn / bc1
    v_hat = vn / bc2
    return -ADAM_LR * (m_hat / (jnp.sqrt(v_hat) + ADAM_EPS) + ADAM_WD * w), mn, vn


def _adamw(w, ga, gb, m, v, *, name):
    r, c = w.shape
    tr = _tile_rows(r, c)

    def body(w_ref, ga_ref, gb_ref, m_ref, v_ref, g_out, d_out, m_out, v_out):
        g = ga_ref[...].astype(F32) + gb_ref[...].astype(F32)
        g_out[...] = g
        d_out[...], m_out[...], v_out[...] = _adam_step(w_ref[...], g, m_ref[...], v_ref[...])

    spec = pl.BlockSpec((tr, c), lambda i: (i, 0))
    shp = jax.ShapeDtypeStruct((r, c), F32)
    return pl.pallas_call(
        body, grid=(r // tr,), in_specs=[spec] * 5, out_specs=[spec] * 4,
        out_shape=[shp] * 4, name=name, compiler_params=_cp(("parallel",)))(w, ga, gb, m, v)


def _lanes(t):
    return t.reshape(-1, 128) if t.size % 128 == 0 else t.reshape(1, -1)


def _adamw_small(ws, gs, ms, vs, *, name):
    n = len(ws)

    def body(*refs):
        for i in range(n):
            w_ref, g_ref, m_ref, v_ref = (refs[k * n + i] for k in range(4))
            outs = [refs[(4 + k) * n + i] for k in range(3)]
            outs[0][...], outs[1][...], outs[2][...] = _adam_step(w_ref[...], g_ref[...], m_ref[...], v_ref[...])

    flat = [_lanes(t) for group in (ws, gs, ms, vs) for t in group]
    shp = [jax.ShapeDtypeStruct(_lanes(t).shape, F32) for t in ws]
    res = pl.pallas_call(body, out_shape=shp * 3, name=name)(*flat)
    return [[res[k * n + i].reshape(ws[i].shape) for i in range(n)] for k in range(3)]


BIG = ['w_in', 'w_glu', 'w_branch_attn', 'w_branch_ssm', 'w_out', 'w_up', 'w_down']
BIG_KEY = {'w_in': 'w_in_t', 'w_glu': 'w_glu', 'w_branch_attn': 'w_ba', 'w_branch_ssm': 'w_bs',
           'w_out': 'w_out', 'w_up': 'w_up', 'w_down': 'w_down'}
TRANSPOSED = {'w_in'}
SMALL = ['attn_norm_g', 'b_in', 'attn_sinks', 'ssm_a_re', 'ssm_a_im', 'ssm_log_dt', 'ssm_b_re', 'ssm_b_im',
         'ssm_c_re', 'ssm_c_im', 'ssm_d', 'b_glu', 'ffn_norm_g', 'conv_b', 'final_norm_g']
WEIGHTS = ['attn_norm_g', 'w_in', 'b_in', 'attn_sinks', 'ssm_a_re', 'ssm_a_im', 'ssm_log_dt', 'ssm_b_re',
           'ssm_b_im', 'ssm_c_re', 'ssm_c_im', 'ssm_d', 'w_glu', 'b_glu', 'w_branch_attn', 'w_branch_ssm',
           'w_out', 'ffn_norm_g', 'w_up', 'conv_w', 'conv_b', 'w_down', 'final_norm_g']


def _shard_2d(name, t):
    t = t[0]
    return t.T if name in TRANSPOSED else t


def _unshard_2d(name, t):
    return (t.T if name in TRANSPOSED else t)[None]


def kernel(x, attn_norm_g, w_in, b_in, attn_sinks, ssm_a_re, ssm_a_im, ssm_log_dt, ssm_b_re, ssm_b_im, ssm_c_re, ssm_c_im, ssm_d, w_glu, b_glu, w_branch_attn, w_branch_ssm, w_out, ffn_norm_g, w_up, conv_w, conv_b, w_down, final_norm_g, loss_target, m_attn_norm_g, m_w_in, m_b_in, m_attn_sinks, m_ssm_a_re, m_ssm_a_im, m_ssm_log_dt, m_ssm_b_re, m_ssm_b_im, m_ssm_c_re, m_ssm_c_im, m_ssm_d, m_w_glu, m_b_glu, m_w_branch_attn, m_w_branch_ssm, m_w_out, m_ffn_norm_g, m_w_up, m_conv_w, m_conv_b, m_w_down, m_final_norm_g, v_attn_norm_g, v_w_in, v_b_in, v_attn_sinks, v_ssm_a_re, v_ssm_a_im, v_ssm_log_dt, v_ssm_b_re, v_ssm_b_im, v_ssm_c_re, v_ssm_c_im, v_ssm_d, v_w_glu, v_b_glu, v_w_branch_attn, v_w_branch_ssm, v_w_out, v_ffn_norm_g, v_w_up, v_conv_w, v_conv_b, v_w_down, v_final_norm_g):
    args = dict(locals())
    w = {n: args[n] for n in WEIGHTS}
    m = {n: args['m_' + n] for n in WEIGHTS}
    v = {n: args['v_' + n] for n in WEIGHTS}
    xi, yi, ci = _place()
    blk = 2 * xi + yi

    shards = {BIG_KEY[n]: _shard_2d(n, w[n]).astype(BF16) for n in BIG}
    cw_cols = w['conv_w'].shape[2]
    cw_place = lax.dynamic_update_slice(jnp.zeros((3, D_FF), F32), w['conv_w'][0] * (ci == 0).astype(F32),
                                        (0, blk * cw_cols))
    conv_w_full = _unpack(_all_reduce_small(_pack([cw_place]), name="gather_conv_w"), [(3, D_FF)])[0]

    small = {n: w[n] for n in SMALL}
    small['conv_w'] = conv_w_full
    loss_part, grad_x, grads, recvs, sgr = _local_step(x[0], loss_target[0], {}, small, shards)

    halves = []
    for n in BIG:
        key = BIG_KEY[n]
        full = grads[key]
        recv = [recvs[key]] if key in recvs else [recvs[key + '#0'], recvs[key + '#1']]
        axis, interleaved = LAYOUT[key]
        pos = _block_pos(xi, yi, interleaved).astype(jnp.int32).reshape(1)
        halves.append(_sum4(full, axis, pos, recv, name="sum4_" + n))
    others = _swap_cores(halves, name="swap_cores")
    out = {}
    for n, mine, other in zip(BIG, halves, others):
        res = _adamw(_shard_2d(n, w[n]), mine, other, _shard_2d(n, m[n]), _shard_2d(n, v[n]), name="adamw_" + n)
        out[n] = [_unshard_2d(n, t) for t in res]

    names = SMALL + ['conv_w']
    shapes = [w[n].shape for n in SMALL] + [(3, D_FF)]
    packed = _pack([sgr[n] for n in names] + [loss_part])
    summed = _unpack(_all_reduce_small(packed, name="all_reduce_small"), shapes + [(1, 1)])
    loss = summed[-1].reshape(())
    sg = dict(zip(names, summed[:-1]))
    sg['conv_w'] = lax.dynamic_slice_in_dim(sg['conv_w'], blk * cw_cols, cw_cols, axis=1)[None]
    deltas, new_m, new_v = _adamw_small([w[n] for n in names], [sg[n] for n in names], [m[n] for n in names],
                                        [v[n] for n in names], name="adamw_small")
    for i, n in enumerate(names):
        out[n] = [sg[n], deltas[i], new_m[i], new_v[i]]

    return (loss, grad_x[None], *[out[n][0] for n in WEIGHTS], *[out[n][1] for n in WEIGHTS],
            *[out[n][2] for n in WEIGHTS], *[out[n][3] for n in WEIGHTS])
```
